```python
import math
import jax, jax.numpy as jnp
from jax import lax
import numpy as np

D_MODEL = 2048
BATCH = 8
SEQ = 4096
DEPTH = 1

MEM_LEN = 256

POOL_WIDTH = D_MODEL // 2
POOL_WINDOWS = (2, 4, 8, 16)
POOL_GROUPS = len(POOL_WINDOWS)
POOL_GROUP_DIM = POOL_WIDTH // POOL_GROUPS

SGU_WIDTH = D_MODEL // 2
SGU_CHUNK = 128
SGU_HEADS = 8
SGU_HEAD_DIM = SGU_WIDTH // SGU_HEADS

XATTN_HEADS = 4
XATTN_HEAD_DIM = D_MODEL // 8
XATTN_WIDTH = XATTN_HEADS * XATTN_HEAD_DIM

BRANCH_WIDTHS = (POOL_WIDTH, SGU_WIDTH, XATTN_WIDTH)
MIX_WIDTH = POOL_WIDTH + SGU_WIDTH + XATTN_WIDTH
IN_SPLITS = (POOL_WIDTH, POOL_WIDTH, SGU_WIDTH, SGU_WIDTH, SGU_WIDTH, XATTN_WIDTH, XATTN_WIDTH)
IN_WIDTH = sum(IN_SPLITS)
EPS = 1e-6

kernel_name = "hybrid_pool_sgu_memxattn_layer"


def rmsnorm(x, g):
    xf = x.astype(jnp.float32)
    y = xf * lax.rsqrt(jnp.mean(xf * xf, axis=-1, keepdims=True) + EPS)
    return (y * g.astype(jnp.float32)).astype(x.dtype)


def layernorm(x, g, b):
    xf = x.astype(jnp.float32)
    mu = jnp.mean(xf, axis=-1, keepdims=True)
    xc = xf - mu
    y = xc * lax.rsqrt(jnp.mean(xc * xc, axis=-1, keepdims=True) + EPS)
    return (y * g.astype(jnp.float32) + b.astype(jnp.float32)).astype(x.dtype)


def split_cols(a, sizes):
    idx = list(np.cumsum(sizes)[:-1])
    return jnp.split(a, idx, axis=-1)


def pool_mixer(xa, w_pool, scale):
    B, S, _ = xa.shape
    xg = xa.reshape(B, S, POOL_GROUPS, POOL_GROUP_DIM).astype(jnp.float32)
    csum = jnp.cumsum(xg, axis=1)
    t = jnp.arange(1, S + 1, dtype=jnp.float32)
    outs = []
    for g, w in enumerate(POOL_WINDOWS):
        cg = csum[:, :, g]
        lower = jnp.pad(cg[:, :S - w], ((0, 0), (w, 0), (0, 0)))
        count = jnp.minimum(t, float(w))[None, :, None]
        outs.append((cg - lower) / count - xg[:, :, g])
    d = jnp.stack(outs, axis=2).astype(xa.dtype)
    y = jnp.einsum('bsgc,gcd->bsgd', d, w_pool)
    return y.reshape(B, S, POOL_WIDTH) * scale


def spatial_gating(u, v, ln_g, ln_b, w_s, b_s):
    B, S, _ = v.shape
    n_chunks = S // SGU_CHUNK
    vn = layernorm(v, ln_g, ln_b)
    vc = vn.reshape(B, n_chunks, SGU_CHUNK, SGU_HEADS, SGU_HEAD_DIM)
    causal = jnp.tril(jnp.ones((SGU_CHUNK, SGU_CHUNK), dtype=bool))
    w = jnp.where(causal[None], w_s, jnp.zeros_like(w_s))
    z = jnp.einsum('hts,bnshd->bnthd', w, vc) + jnp.transpose(b_s)[None, None, :, :, None]
    return u * z.reshape(B, S, SGU_WIDTH)


def memory_cross_attention(q, k, v):
    B, S, _ = q.shape
    M = k.shape[1]
    qh = q.reshape(B, S, XATTN_HEADS, XATTN_HEAD_DIM)
    kh = k.reshape(B, M, XATTN_HEADS, XATTN_HEAD_DIM)
    vh = v.reshape(B, M, XATTN_HEADS, XATTN_HEAD_DIM)
    s = jnp.einsum('bshd,bmhd->bhsm', qh, kh).astype(jnp.float32) * (1.0 / math.sqrt(XATTN_HEAD_DIM))
    p = jax.nn.softmax(s, axis=-1).astype(vh.dtype)
    o = jnp.einsum('bhsm,bmhd->bshd', p, vh)
    return o.reshape(B, S, XATTN_WIDTH)


def _fwd_setup_inputs(seed: int = 0) -> dict:
    key = jax.random.key(seed)
    ks = jax.random.split(key, 16)
    f32 = jnp.float32
    nrm = lambda k, shape, s: jax.random.normal(k, shape, f32) * s
    return {
        "x": nrm(ks[0], (BATCH, SEQ, D_MODEL), 1.0),
        "mem": nrm(ks[1], (BATCH, MEM_LEN, D_MODEL), 1.0),
        "norm_pre": 1.0 + nrm(ks[2], (DEPTH, D_MODEL), 0.05),
        "w_in": nrm(ks[3], (DEPTH, D_MODEL, IN_WIDTH), D_MODEL ** -0.5),
        "pool_w": nrm(ks[4], (DEPTH, POOL_GROUPS, POOL_GROUP_DIM, POOL_GROUP_DIM), POOL_GROUP_DIM ** -0.5),
        "pool_scale": 1.0 + nrm(ks[5], (DEPTH, POOL_WIDTH), 0.1),
        "sgu_ln_g": 1.0 + nrm(ks[6], (DEPTH, SGU_WIDTH), 0.05),
        "sgu_ln_b": nrm(ks[7], (DEPTH, SGU_WIDTH), 0.02),
        "sgu_w": nrm(ks[8], (DEPTH, SGU_HEADS, SGU_CHUNK, SGU_CHUNK), SGU_CHUNK ** -0.5),
        "sgu_b": 1.0 + nrm(ks[9], (DEPTH, SGU_HEADS, SGU_CHUNK), 0.1),
        "mem_norm": 1.0 + nrm(ks[10], (D_MODEL,), 0.05),
        "w_kv": nrm(ks[11], (DEPTH, D_MODEL, 2 * XATTN_WIDTH), D_MODEL ** -0.5),
        "branch_norm": 1.0 + nrm(ks[12], (DEPTH, MIX_WIDTH), 0.05),
        "w_out": nrm(ks[13], (DEPTH, MIX_WIDTH, D_MODEL), MIX_WIDTH ** -0.5),
        "norm_post": 1.0 + nrm(ks[14], (DEPTH, D_MODEL), 0.05),
    }


def _fwd_reference(x, mem, norm_pre, w_in, pool_w, pool_scale, sgu_ln_g, sgu_ln_b, sgu_w, sgu_b,
              mem_norm, w_kv, branch_norm, w_out, norm_post):
    mem_n = rmsnorm(mem, mem_norm)
    for l in range(DEPTH):
        h = rmsnorm(x, norm_pre[l])
        proj = jnp.einsum('bsd,de->bse', h, w_in[l])
        xa, ga, u, vb, gb, q, gc = split_cols(proj, IN_SPLITS)
        k_m, v_m = split_cols(jnp.einsum('bmd,de->bme', mem_n, w_kv[l]), (XATTN_WIDTH, XATTN_WIDTH))

        ya = pool_mixer(xa, pool_w[l], pool_scale[l]) * jax.nn.silu(ga)
        yb = spatial_gating(u, vb, sgu_ln_g[l], sgu_ln_b[l], sgu_w[l], sgu_b[l]) * jax.nn.silu(gb)
        yc = memory_cross_attention(q, k_m, v_m) * jax.nn.silu(gc)

        g_a, g_b, g_c = split_cols(branch_norm[l], BRANCH_WIDTHS)
        y = jnp.concatenate([rmsnorm(ya, g_a), rmsnorm(yb, g_b), rmsnorm(yc, g_c)], axis=-1)
        out = jnp.einsum('bse,ed->bsd', y, w_out[l])
        x = x + rmsnorm(out, norm_post[l])
    return x


import jax as _jax
import jax.numpy as _jnp

TWIN_FORMAT = 'train_step'
FWD_PARAMS = ['x', 'mem', 'norm_pre', 'w_in', 'pool_w', 'pool_scale', 'sgu_ln_g', 'sgu_ln_b', 'sgu_w', 'sgu_b', 'mem_norm', 'w_kv', 'branch_norm', 'w_out', 'norm_post']
TWIN_WEIGHTS = ['norm_pre', 'w_in', 'pool_w', 'pool_scale', 'sgu_ln_g', 'sgu_ln_b', 'sgu_w', 'sgu_b', 'mem_norm', 'w_kv', 'branch_norm', 'w_out', 'norm_post']
TWIN_DIFF_INPUT = 'x'
TWIN_INPUTS = ['x', 'mem', 'norm_pre', 'w_in', 'pool_w', 'pool_scale', 'sgu_ln_g', 'sgu_ln_b', 'sgu_w', 'sgu_b', 'mem_norm', 'w_kv', 'branch_norm', 'w_out', 'norm_post', 'loss_target', 'm_norm_pre', 'm_w_in', 'm_pool_w', 'm_pool_scale', 'm_sgu_ln_g', 'm_sgu_ln_b', 'm_sgu_w', 'm_sgu_b', 'm_mem_norm', 'm_w_kv', 'm_branch_norm', 'm_w_out', 'm_norm_post', 'v_norm_pre', 'v_w_in', 'v_pool_w', 'v_pool_scale', 'v_sgu_ln_g', 'v_sgu_ln_b', 'v_sgu_w', 'v_sgu_b', 'v_mem_norm', 'v_w_kv', 'v_branch_norm', 'v_w_out', 'v_norm_post']
TWIN_OUTPUTS = ['loss', 'grad_x', 'grad_norm_pre', 'grad_w_in', 'grad_pool_w', 'grad_pool_scale', 'grad_sgu_ln_g', 'grad_sgu_ln_b', 'grad_sgu_w', 'grad_sgu_b', 'grad_mem_norm', 'grad_w_kv', 'grad_branch_norm', 'grad_w_out', 'grad_norm_post', 'delta_norm_pre', 'delta_w_in', 'delta_pool_w', 'delta_pool_scale', 'delta_sgu_ln_g', 'delta_sgu_ln_b', 'delta_sgu_w', 'delta_sgu_b', 'delta_mem_norm', 'delta_w_kv', 'delta_branch_norm', 'delta_w_out', 'delta_norm_post', 'new_m_norm_pre', 'new_m_w_in', 'new_m_pool_w', 'new_m_pool_scale', 'new_m_sgu_ln_g', 'new_m_sgu_ln_b', 'new_m_sgu_w', 'new_m_sgu_b', 'new_m_mem_norm', 'new_m_w_kv', 'new_m_branch_norm', 'new_m_w_out', 'new_m_norm_post', 'new_v_norm_pre', 'new_v_w_in', 'new_v_pool_w', 'new_v_pool_scale', 'new_v_sgu_ln_g', 'new_v_sgu_ln_b', 'new_v_sgu_w', 'new_v_sgu_b', 'new_v_mem_norm', 'new_v_w_kv', 'new_v_branch_norm', 'new_v_w_out', 'new_v_norm_post']
TWIN_LEAF_KINDS = {'loss': 'loss', 'grad_x': 'grad_x', 'grad_norm_pre': 'grad_w', 'grad_w_in': 'grad_w', 'grad_pool_w': 'grad_w', 'grad_pool_scale': 'grad_w', 'grad_sgu_ln_g': 'grad_w', 'grad_sgu_ln_b': 'grad_w', 'grad_sgu_w': 'grad_w', 'grad_sgu_b': 'grad_w', 'grad_mem_norm': 'grad_w', 'grad_w_kv': 'grad_w', 'grad_branch_norm': 'grad_w', 'grad_w_out': 'grad_w', 'grad_norm_post': 'grad_w', 'delta_norm_pre': 'delta_w', 'delta_w_in': 'delta_w', 'delta_pool_w': 'delta_w', 'delta_pool_scale': 'delta_w', 'delta_sgu_ln_g': 'delta_w', 'delta_sgu_ln_b': 'delta_w', 'delta_sgu_w': 'delta_w', 'delta_sgu_b': 'delta_w', 'delta_mem_norm': 'delta_w', 'delta_w_kv': 'delta_w', 'delta_branch_norm': 'delta_w', 'delta_w_out': 'delta_w', 'delta_norm_post': 'delta_w', 'new_m_norm_pre': 'new_m', 'new_m_w_in': 'new_m', 'new_m_pool_w': 'new_m', 'new_m_pool_scale': 'new_m', 'new_m_sgu_ln_g': 'new_m', 'new_m_sgu_ln_b': 'new_m', 'new_m_sgu_w': 'new_m', 'new_m_sgu_b': 'new_m', 'new_m_mem_norm': 'new_m', 'new_m_w_kv': 'new_m', 'new_m_branch_norm': 'new_m', 'new_m_w_out': 'new_m', 'new_m_norm_post': 'new_m', 'new_v_norm_pre': 'new_v', 'new_v_w_in': 'new_v', 'new_v_pool_w': 'new_v', 'new_v_pool_scale': 'new_v', 'new_v_sgu_ln_g': 'new_v', 'new_v_sgu_ln_b': 'new_v', 'new_v_sgu_w': 'new_v', 'new_v_sgu_b': 'new_v', 'new_v_mem_norm': 'new_v', 'new_v_w_kv': 'new_v', 'new_v_branch_norm': 'new_v', 'new_v_w_out': 'new_v', 'new_v_norm_post': 'new_v'}


def _forward(args):
    return _fwd_reference(*[args[k] for k in FWD_PARAMS])


def _output_shape():
    def fwd():
        inp = _fwd_setup_inputs(0)
        return _fwd_reference(*[inp[k] for k in FWD_PARAMS])
    out = _jax.eval_shape(fwd)
    return out.shape, out.dtype

N_MICROBATCH = 1
ADAM_LR = 0.001
ADAM_B1 = 0.9
ADAM_B2 = 0.999
ADAM_EPS = 1e-08
ADAM_WD = 0.01
ADAM_STEP = 10
PER_EXAMPLE_BATCH_AXIS = {'x': 0, 'mem': 0, 'loss_target': 0}
SHARED_INPUTS = []
_WEIGHT_DTYPES = {'norm_pre': _jnp.float32, 'w_in': _jnp.float32, 'pool_w': _jnp.float32, 'pool_scale': _jnp.float32, 'sgu_ln_g': _jnp.float32, 'sgu_ln_b': _jnp.float32, 'sgu_w': _jnp.float32, 'sgu_b': _jnp.float32, 'mem_norm': _jnp.float32, 'w_kv': _jnp.float32, 'branch_norm': _jnp.float32, 'w_out': _jnp.float32, 'norm_post': _jnp.float32}
MOMENT_SCALE = {'norm_pre': 1.906205e-01, 'w_in': 1.041570e-01, 'pool_w': 1.201959e-01, 'pool_scale': 1.217058e-01, 'sgu_ln_g': 5.772113e-02, 'sgu_ln_b': 5.645055e-02, 'sgu_w': 5.695016e-02, 'sgu_b': 8.217681e-02, 'mem_norm': 1.049375e-01, 'w_kv': 1.050712e-01, 'branch_norm': 1.188553e-01, 'w_out': 1.423897e-01, 'norm_post': 1.598154e+01}


def _to_microbatches(a, axis):
    t = _jnp.moveaxis(a, axis, 0)
    t = t.reshape((N_MICROBATCH, t.shape[0] // N_MICROBATCH) + t.shape[1:])
    return _jnp.moveaxis(t, 1, axis + 1)


def setup_inputs(seed: int = 0) -> dict:
    inp = _fwd_setup_inputs(seed)
    key = _jax.random.fold_in(_jax.random.key(seed), 7919)
    shape, _ = _output_shape()
    out = dict(inp)
    out["loss_target"] = _jax.random.normal(_jax.random.fold_in(key, 0), shape, _jnp.float32)
    for i, name in enumerate(TWIN_WEIGHTS):
        w = inp[name].astype(_jnp.float32)
        if MOMENT_SCALE is None:
            s = _jnp.sqrt(_jnp.mean(_jnp.square(w)) + 1e-30)
        else:
            s = MOMENT_SCALE[name]
        km, kv = _jax.random.split(_jax.random.fold_in(key, i + 1))
        out[name] = w
        out["m_" + name] = s * _jax.random.normal(km, w.shape, _jnp.float32)
        out["v_" + name] = (s * s) * _jax.random.uniform(kv, w.shape, _jnp.float32, 0.5, 1.5)
    if N_MICROBATCH > 1:
        for name, axis in PER_EXAMPLE_BATCH_AXIS.items():
            out[name] = _to_microbatches(out[name], axis)
    return {'x': out['x'], 'mem': out['mem'], 'norm_pre': out['norm_pre'], 'w_in': out['w_in'], 'pool_w': out['pool_w'], 'pool_scale': out['pool_scale'], 'sgu_ln_g': out['sgu_ln_g'], 'sgu_ln_b': out['sgu_ln_b'], 'sgu_w': out['sgu_w'], 'sgu_b': out['sgu_b'], 'mem_norm': out['mem_norm'], 'w_kv': out['w_kv'], 'branch_norm': out['branch_norm'], 'w_out': out['w_out'], 'norm_post': out['norm_post'], 'loss_target': out['loss_target'], 'm_norm_pre': out['m_norm_pre'], 'm_w_in': out['m_w_in'], 'm_pool_w': out['m_pool_w'], 'm_pool_scale': out['m_pool_scale'], 'm_sgu_ln_g': out['m_sgu_ln_g'], 'm_sgu_ln_b': out['m_sgu_ln_b'], 'm_sgu_w': out['m_sgu_w'], 'm_sgu_b': out['m_sgu_b'], 'm_mem_norm': out['m_mem_norm'], 'm_w_kv': out['m_w_kv'], 'm_branch_norm': out['m_branch_norm'], 'm_w_out': out['m_w_out'], 'm_norm_post': out['m_norm_post'], 'v_norm_pre': out['v_norm_pre'], 'v_w_in': out['v_w_in'], 'v_pool_w': out['v_pool_w'], 'v_pool_scale': out['v_pool_scale'], 'v_sgu_ln_g': out['v_sgu_ln_g'], 'v_sgu_ln_b': out['v_sgu_ln_b'], 'v_sgu_w': out['v_sgu_w'], 'v_sgu_b': out['v_sgu_b'], 'v_mem_norm': out['v_mem_norm'], 'v_w_kv': out['v_w_kv'], 'v_branch_norm': out['v_branch_norm'], 'v_w_out': out['v_w_out'], 'v_norm_post': out['v_norm_post']}


def _loss(weights, diff, rest, loss_target):
    with _jax.named_scope("forward"):
        args = {**rest, TWIN_DIFF_INPUT: diff, **{k: w.astype(_WEIGHT_DTYPES[k]) for k, w in weights.items()}}
        y = _forward(args)
    with _jax.named_scope("loss_head"):
        err = _jnp.square(y.astype(_jnp.float32) - loss_target)
        return 0.5 * _jnp.sum(_jnp.mean(err, axis=-1)) if err.ndim else 0.5 * err


def _adamw(w, g, m, v):
    m = ADAM_B1 * m + (1.0 - ADAM_B1) * g
    v = ADAM_B2 * v + (1.0 - ADAM_B2) * _jnp.square(g)
    m_hat = m / (1.0 - ADAM_B1 ** ADAM_STEP)
    v_hat = v / (1.0 - ADAM_B2 ** ADAM_STEP)
    delta = -ADAM_LR * (m_hat / (_jnp.sqrt(v_hat) + ADAM_EPS) + ADAM_WD * w)
    return delta, m, v


def reference(x, mem, norm_pre, w_in, pool_w, pool_scale, sgu_ln_g, sgu_ln_b, sgu_w, sgu_b, mem_norm, w_kv, branch_norm, w_out, norm_post, loss_target, m_norm_pre, m_w_in, m_pool_w, m_pool_scale, m_sgu_ln_g, m_sgu_ln_b, m_sgu_w, m_sgu_b, m_mem_norm, m_w_kv, m_branch_norm, m_w_out, m_norm_post, v_norm_pre, v_w_in, v_pool_w, v_pool_scale, v_sgu_ln_g, v_sgu_ln_b, v_sgu_w, v_sgu_b, v_mem_norm, v_w_kv, v_branch_norm, v_w_out, v_norm_post):
    given = dict(x=x, mem=mem, norm_pre=norm_pre, w_in=w_in, pool_w=pool_w, pool_scale=pool_scale, sgu_ln_g=sgu_ln_g, sgu_ln_b=sgu_ln_b, sgu_w=sgu_w, sgu_b=sgu_b, mem_norm=mem_norm, w_kv=w_kv, branch_norm=branch_norm, w_out=w_out, norm_post=norm_post, loss_target=loss_target, m_norm_pre=m_norm_pre, m_w_in=m_w_in, m_pool_w=m_pool_w, m_pool_scale=m_pool_scale, m_sgu_ln_g=m_sgu_ln_g, m_sgu_ln_b=m_sgu_ln_b, m_sgu_w=m_sgu_w, m_sgu_b=m_sgu_b, m_mem_norm=m_mem_norm, m_w_kv=m_w_kv, m_branch_norm=m_branch_norm, m_w_out=m_w_out, m_norm_post=m_norm_post, v_norm_pre=v_norm_pre, v_w_in=v_w_in, v_pool_w=v_pool_w, v_pool_scale=v_pool_scale, v_sgu_ln_g=v_sgu_ln_g, v_sgu_ln_b=v_sgu_ln_b, v_sgu_w=v_sgu_w, v_sgu_b=v_sgu_b, v_mem_norm=v_mem_norm, v_w_kv=v_w_kv, v_branch_norm=v_branch_norm, v_w_out=v_w_out, v_norm_post=v_norm_post)
    weights = {n: given[n] for n in TWIN_WEIGHTS}
    shared = {n: given[n] for n in SHARED_INPUTS}
    per_example = {n: given[n] for n in ['x', 'mem']}
    grad_fn = _jax.value_and_grad(_loss, argnums=(0, 1))

    def one_microbatch(ex, loss_target):
        ex = dict(ex)
        diff = ex.pop(TWIN_DIFF_INPUT)
        return grad_fn(weights, diff, {**shared, **ex}, loss_target)

    if N_MICROBATCH == 1:
        loss, (grad_w, grad_x) = one_microbatch(per_example, given["loss_target"])
    else:
        def body(carry, xs):
            loss_sum, grad_sum = carry
            l_k, (gw_k, gx_k) = one_microbatch(xs[0], xs[1])
            with _jax.named_scope("update"):
                return (loss_sum + l_k, _jax.tree.map(_jnp.add, grad_sum, gw_k)), gx_k

        init = (_jnp.zeros((), _jnp.float32), _jax.tree.map(_jnp.zeros_like, weights))
        (loss, grad_w), grad_x = _jax.lax.scan(body, init, (per_example, given["loss_target"]))
    with _jax.named_scope("update"):
        delta_w, new_m, new_v = {}, {}, {}
        for n in TWIN_WEIGHTS:
            delta_w[n], new_m[n], new_v[n] = _adamw(weights[n], grad_w[n], given["m_" + n], given["v_" + n])
    return (loss, grad_x, *[grad_w[n] for n in TWIN_WEIGHTS], *[delta_w[n] for n in TWIN_WEIGHTS],
            *[new_m[n] for n in TWIN_WEIGHTS], *[new_v[n] for n in TWIN_WEIGHTS])
```

```python
import functools

import jax
import jax.numpy as jnp
from jax import lax
from jax.experimental import pallas as pl
from jax.experimental.pallas import tpu as pltpu

F32 = jnp.float32
BF16 = jnp.bfloat16
EPS = 1e-6
MESH = pl.DeviceIdType.MESH
ANY = pl.BlockSpec(memory_space=pl.ANY)

POOL_WINDOWS = (2, 4, 8, 16)
GROUP = 256
HALO = 16
CHUNK = 128
N_SGU_HEADS = 8
N_ATT_HEADS = 4
ATT_DIM = 256
WIDTH = 1024
ATT_SCALE = 1.0 / 16.0

ADAM_LR = 0.001
ADAM_B1 = 0.9
ADAM_B2 = 0.999
ADAM_EPS = 1e-08
ADAM_WD = 0.01
ADAM_STEP = 10

VMEM_LIMIT = 60 * 1024 * 1024


def _params(n_grid_axes, vmem=VMEM_LIMIT):
    return pltpu.CompilerParams(dimension_semantics=("arbitrary",) * n_grid_axes, vmem_limit_bytes=vmem)


def _dot(a, b, dims):
    return lax.dot_general(a, b, (dims, ((), ())), preferred_element_type=F32)


NN = ((1,), (0,))
NT = ((1,), (1,))
TN = ((0,), (0,))


def _matmul(a, b, dims, out_dtype, tm, tn, tk, name):
    if dims == NN:
        (m, k), n = a.shape, b.shape[1]
        a_spec = pl.BlockSpec((tm, tk), lambda i, j, kk: (i, kk))
        b_spec = pl.BlockSpec((tk, tn), lambda i, j, kk: (kk, j))
    else:
        (k, m), n = a.shape, b.shape[1]
        a_spec = pl.BlockSpec((tk, tm), lambda i, j, kk: (kk, i))
        b_spec = pl.BlockSpec((tk, tn), lambda i, j, kk: (kk, j))
    nk = k // tk

    def body(a_ref, b_ref, o_ref, acc_ref):
        kk = pl.program_id(2)
        part = _dot(a_ref[...], b_ref[...], dims)

        @pl.when(kk == 0)
        def _():
            acc_ref[...] = part

        @pl.when(kk > 0)
        def _():
            acc_ref[...] += part

        @pl.when(kk == nk - 1)
        def _():
            o_ref[...] = acc_ref[...].astype(out_dtype)

    return pl.pallas_call(
        body, name=name, grid=(m // tm, n // tn, nk),
        in_specs=[a_spec, b_spec],
        out_specs=pl.BlockSpec((tm, tn), lambda i, j, kk: (i, j)),
        out_shape=jax.ShapeDtypeStruct((m, n), out_dtype),
        scratch_shapes=[pltpu.VMEM((tm, tn), F32)],
        compiler_params=_params(3),
    )(a, b)


def _rms_pre(x, g, tm):
    s, d = x.shape

    def body(x_ref, g_ref, h_ref):
        xv = x_ref[...]
        r = lax.rsqrt(jnp.mean(xv * xv, axis=-1, keepdims=True) + EPS)
        h_ref[...] = (xv * r * g_ref[...]).astype(BF16)

    return pl.pallas_call(
        body, name="rms_pre", grid=(s // tm,),
        in_specs=[pl.BlockSpec((tm, d), lambda i: (i, 0)), pl.BlockSpec((1, d), lambda i: (0, 0))],
        out_specs=pl.BlockSpec((tm, d), lambda i: (i, 0)),
        out_shape=jax.ShapeDtypeStruct((s, d), BF16),
        compiler_params=_params(1),
    )(x, g)


def _kv_fwd(mem, g, w_kv):
    m, d = mem.shape

    def body(mem_ref, g_ref, w_ref, k_ref, v_ref):
        mv = mem_ref[...]
        r = lax.rsqrt(jnp.mean(mv * mv, axis=-1, keepdims=True) + EPS)
        mem_n = (mv * r * g_ref[...]).astype(BF16)
        kv = _dot(mem_n, w_ref[...], NN)
        k_ref[...] = kv[:, :WIDTH].astype(BF16)
        v_ref[...] = kv[:, WIDTH:].astype(BF16)

    return pl.pallas_call(
        body, name="kv_fwd",
        out_shape=(jax.ShapeDtypeStruct((m, WIDTH), BF16), jax.ShapeDtypeStruct((m, WIDTH), BF16)),
        compiler_params=_params(0),
    )(mem, g, w_kv)


def _kv_bwd(mem, g, w_kv, dk, dv):
    m, d = mem.shape
    n = w_kv.shape[1]
    col = 512

    def body(mem_ref, g_ref, w_ref, dk_ref, dv_ref, dw_ref, dg_ref):
        mv = mem_ref[...]
        r = lax.rsqrt(jnp.mean(mv * mv, axis=-1, keepdims=True) + EPS)
        mem_hat = mv * r
        mem_n = (mem_hat * g_ref[...]).astype(BF16)
        dkv = jnp.concatenate([dk_ref[...], dv_ref[...]], axis=1).astype(BF16)
        for j in range(n // col):
            dw_ref[:, j * col:(j + 1) * col] = _dot(mem_n, dkv[:, j * col:(j + 1) * col], TN).astype(BF16)
        dmem_n = _dot(dkv, w_ref[...], NT)
        dg_ref[...] = jnp.sum(dmem_n * mem_hat, axis=0, keepdims=True)

    return pl.pallas_call(
        body, name="kv_bwd",
        out_shape=(jax.ShapeDtypeStruct((d, n), BF16), jax.ShapeDtypeStruct((1, d), F32)),
        compiler_params=_params(0),
    )(mem, g, w_kv, dk, dv)


def _sigmoid(x):
    return 1.0 / (1.0 + jnp.exp(-x))


def _inv_counts(t0, t):
    pos = (t0 + lax.broadcasted_iota(jnp.int32, (t, 1), 0) + 1).astype(F32)
    return [1.0 / jnp.minimum(pos, float(w)) for w in POOL_WINDOWS]


def _window_sums(ext, t, backward):
    n = t + HALO
    parts = []
    for gi, w in enumerate(POOL_WINDOWS):
        s = ext[:, gi * GROUP:(gi + 1) * GROUP]
        k = 1
        while k < w:
            s = s + pltpu.roll(s, (n - k) if backward else k, axis=0)
            k *= 2
        parts.append(s[:t] if backward else s[HALO:])
    return parts


def _pool_fwd(xa, halo, inv, pool_w):
    t = xa.shape[0]
    sums = _window_sums(jnp.concatenate([halo, xa], axis=0), t, backward=False)
    d = jnp.concatenate([sums[gi] * inv[gi] - xa[:, gi * GROUP:(gi + 1) * GROUP] for gi in range(4)], axis=1)
    d = d.astype(BF16)
    y = jnp.concatenate([_dot(d[:, gi * GROUP:(gi + 1) * GROUP], pool_w[gi], NN) for gi in range(4)], axis=1)
    return d, y


def _layernorm_fwd(v):
    mu = jnp.mean(v, axis=-1, keepdims=True)
    xc = v - mu
    rstd = lax.rsqrt(jnp.mean(xc * xc, axis=-1, keepdims=True) + EPS)
    return xc * rstd, rstd


def _tril_mask(transposed):
    r = lax.broadcasted_iota(jnp.int32, (CHUNK, CHUNK), 0)
    c = lax.broadcasted_iota(jnp.int32, (CHUNK, CHUNK), 1)
    return (r <= c) if transposed else (r >= c)


def _sgu_mix(w_ref, vals, transposed):
    t = vals.shape[0]
    mask = _tril_mask(transposed)
    ws = [jnp.where(mask, w_ref[h], 0.0).astype(BF16) for h in range(N_SGU_HEADS)]
    rows = []
    for ci in range(t // CHUNK):
        blk = vals[ci * CHUNK:(ci + 1) * CHUNK]
        rows.append(jnp.concatenate(
            [_dot(ws[h], blk[:, h * CHUNK:(h + 1) * CHUNK], NN) for h in range(N_SGU_HEADS)], axis=1))
    return jnp.concatenate(rows, axis=0)


def _attn_fwd(q, k, v):
    ps, os_ = [], []
    for h in range(N_ATT_HEADS):
        sl = slice(h * ATT_DIM, (h + 1) * ATT_DIM)
        s = _dot(q[:, sl], k[:, sl], NT) * ATT_SCALE
        s = s - jnp.max(s, axis=-1, keepdims=True)
        e = jnp.exp(s)
        p = e / jnp.sum(e, axis=-1, keepdims=True)
        ps.append(p)
        os_.append(_dot(p.astype(BF16), v[:, sl], NN))
    return ps, jnp.concatenate(os_, axis=1)


def _rms_branch(y_pre):
    r = lax.rsqrt(jnp.mean(y_pre * y_pre, axis=-1, keepdims=True) + EPS)
    return y_pre * r, r


def _branch_specs(t, n_tiles, order):
    width_in = 7 * WIDTH
    tile = lambda i: order(i)
    per_halo = t // HALO
    const2 = lambda i: (0, 0)
    const3 = lambda i: (0, 0, 0)
    return [
        pl.BlockSpec((t, width_in), lambda i: (tile(i), 0)),
        pl.BlockSpec((HALO, WIDTH), lambda i: (jnp.maximum(tile(i) * per_halo - 1, 0), 0)),
        pl.BlockSpec((4, GROUP, GROUP), const3),
        pl.BlockSpec((1, WIDTH), const2),
        pl.BlockSpec((1, WIDTH), const2),
        pl.BlockSpec((1, WIDTH), const2),
        pl.BlockSpec((N_SGU_HEADS, CHUNK, CHUNK), const3),
        pl.BlockSpec((CHUNK, WIDTH), const2),
        pl.BlockSpec((MEM_ROWS, WIDTH), const2),
        pl.BlockSpec((MEM_ROWS, WIDTH), const2),
        pl.BlockSpec((1, 3 * WIDTH), const2),
    ]


MEM_ROWS = 256


def _branches_fwd(proj, pool_w, pool_scale, ln_g, ln_b, sgu_w, bias_full, k, v, branch_norm, t):
    s = proj.shape[0]
    n_tiles = s // t

    def body(proj_ref, halo_ref, pw_ref, ps_ref, lg_ref, lb_ref, sw_ref, sb_ref, k_ref, v_ref, bn_ref, y_ref):
        i = pl.program_id(0)
        col = lambda j: proj_ref[:, j * WIDTH:(j + 1) * WIDTH]
        bn = bn_ref[...]
        halo = jnp.where(i > 0, halo_ref[...], 0.0)
        _, y_pool = _pool_fwd(col(0), halo, _inv_counts(i * t, t), pw_ref[...])
        ga = col(1)
        ya = y_pool * ps_ref[...] * (ga * _sigmoid(ga))
        y_ref[:, 0:WIDTH] = (_rms_branch(ya)[0] * bn[:, 0:WIDTH]).astype(BF16)
        vhat, _ = _layernorm_fwd(col(3))
        vn = (vhat * lg_ref[...] + lb_ref[...]).astype(BF16)
        z = _sgu_mix(sw_ref, vn, transposed=False) + jnp.tile(sb_ref[...], (t // CHUNK, 1))
        gb = col(4)
        yb = col(2) * z * (gb * _sigmoid(gb))
        y_ref[:, WIDTH:2 * WIDTH] = (_rms_branch(yb)[0] * bn[:, WIDTH:2 * WIDTH]).astype(BF16)
        _, o = _attn_fwd(col(5).astype(BF16), k_ref[...], v_ref[...])
        gc = col(6)
        yc = o * (gc * _sigmoid(gc))
        y_ref[:, 2 * WIDTH:] = (_rms_branch(yc)[0] * bn[:, 2 * WIDTH:]).astype(BF16)

    return pl.pallas_call(
        body, name="branches_fwd", grid=(n_tiles,),
        in_specs=_branch_specs(t, n_tiles, lambda i: i),
        out_specs=pl.BlockSpec((t, 3 * WIDTH), lambda i: (i, 0)),
        out_shape=jax.ShapeDtypeStruct((s, 3 * WIDTH), BF16),
        compiler_params=_params(1),
    )(proj, proj, pool_w, pool_scale, ln_g, ln_b, sgu_w, bias_full, k, v, branch_norm)


def _branches_bwd(proj, dy, pool_w, pool_scale, ln_g, ln_b, sgu_w, sgu_wt, bias_full, k, v, branch_norm, t):
    s = proj.shape[0]
    n_tiles = s // t
    n_chunks = t // CHUNK
    order = lambda i: n_tiles - 1 - i

    def body(proj_ref, halo_ref, pw_ref, ps_ref, lg_ref, lb_ref, sw_ref, sb_ref, k_ref, v_ref, bn_ref,
             swt_ref, dy_ref,
             dproj_ref, dpw_ref, dps_ref, dlg_ref, dlb_ref, dsw_ref, dsb_ref, dbn_ref, dk_ref, dv_ref,
             carry_ref, dbias_ref):
        step = pl.program_id(0)
        i = order(step)

        @pl.when(step == 0)
        def _():
            for ref in (dpw_ref, dps_ref, dlg_ref, dlb_ref, dsw_ref, dbn_ref, dk_ref, dv_ref, carry_ref, dbias_ref):
                ref[...] = jnp.zeros(ref.shape, ref.dtype)

        col = lambda j: proj_ref[:, j * WIDTH:(j + 1) * WIDTH]
        bn = bn_ref[...]

        def norm_bwd(y_pre, sl):
            yhat, r = _rms_branch(y_pre)
            dyv = dy_ref[:, sl].astype(F32)
            dbn_ref[:, sl] += jnp.sum(dyv * yhat, axis=0, keepdims=True)
            dyhat = dyv * bn[:, sl]
            return r * (dyhat - yhat * jnp.mean(dyhat * yhat, axis=-1, keepdims=True))

        def gate(gv):
            sg = _sigmoid(gv)
            return gv * sg, sg * (1.0 + gv * (1.0 - sg))

        inv = _inv_counts(i * t, t)
        halo = jnp.where(i > 0, halo_ref[...], 0.0)
        pw = pw_ref[...]
        d, y_pool = _pool_fwd(col(0), halo, inv, pw)
        scale = ps_ref[...]
        silu_a, dsilu_a = gate(col(1))
        pa = y_pool * scale
        dya = norm_bwd(pa * silu_a, slice(0, WIDTH))
        dproj_ref[:, WIDTH:2 * WIDTH] = (dya * pa * dsilu_a).astype(BF16)
        dpa = dya * silu_a
        dps_ref[...] += jnp.sum(dpa * y_pool, axis=0, keepdims=True)
        dy_pool = (dpa * scale).astype(BF16)
        dd_parts, ddc_parts = [], []
        for gi in range(4):
            sl = slice(gi * GROUP, (gi + 1) * GROUP)
            dpw_ref[gi] += _dot(d[:, sl], dy_pool[:, sl], TN)
            dd = _dot(dy_pool[:, sl], pw[gi], NT)
            dd_parts.append(dd)
            ddc_parts.append(dd * inv[gi])
        ddc = jnp.concatenate(ddc_parts, axis=1)
        sums = _window_sums(jnp.concatenate([ddc, carry_ref[...]], axis=0), t, backward=True)
        carry_ref[...] = ddc[:HALO]
        dproj_ref[:, 0:WIDTH] = jnp.concatenate([sums[gi] - dd_parts[gi] for gi in range(4)], axis=1).astype(BF16)

        vhat, rstd = _layernorm_fwd(col(3))
        lg = lg_ref[...]
        vn = (vhat * lg + lb_ref[...]).astype(BF16)
        z = _sgu_mix(sw_ref, vn, transposed=False) + jnp.tile(sb_ref[...], (n_chunks, 1))
        u = col(2)
        silu_b, dsilu_b = gate(col(4))
        uz = u * z
        dyb = norm_bwd(uz * silu_b, slice(WIDTH, 2 * WIDTH))
        dproj_ref[:, 4 * WIDTH:5 * WIDTH] = (dyb * uz * dsilu_b).astype(BF16)
        duz = dyb * silu_b
        dproj_ref[:, 2 * WIDTH:3 * WIDTH] = (duz * z).astype(BF16)
        dz = duz * u
        dz_b = dz.astype(BF16)
        for ci in range(n_chunks):
            rows = slice(ci * CHUNK, (ci + 1) * CHUNK)
            dbias_ref[...] += dz[rows]
            for h in range(N_SGU_HEADS):
                sl = slice(h * CHUNK, (h + 1) * CHUNK)
                dsw_ref[h] += _dot(dz_b[rows, sl], vn[rows, sl], NT)
        dvn = _sgu_mix(swt_ref, dz_b, transposed=True)
        dlg_ref[...] += jnp.sum(dvn * vhat, axis=0, keepdims=True)
        dlb_ref[...] += jnp.sum(dvn, axis=0, keepdims=True)
        dvhat = dvn * lg
        dvb = rstd * (dvhat - jnp.mean(dvhat, axis=-1, keepdims=True)
                      - vhat * jnp.mean(dvhat * vhat, axis=-1, keepdims=True))
        dproj_ref[:, 3 * WIDTH:4 * WIDTH] = dvb.astype(BF16)

        q = col(5).astype(BF16)
        kv_k, kv_v = k_ref[...], v_ref[...]
        ps, o = _attn_fwd(q, kv_k, kv_v)
        silu_c, dsilu_c = gate(col(6))
        dyc = norm_bwd(o * silu_c, slice(2 * WIDTH, 3 * WIDTH))
        dproj_ref[:, 6 * WIDTH:7 * WIDTH] = (dyc * o * dsilu_c).astype(BF16)
        do = (dyc * silu_c).astype(BF16)
        dq_parts = []
        for h in range(N_ATT_HEADS):
            sl = slice(h * ATT_DIM, (h + 1) * ATT_DIM)
            p = ps[h]
            dp = _dot(do[:, sl], kv_v[:, sl], NT)
            ds = (p * (dp - jnp.sum(p * dp, axis=-1, keepdims=True)) * ATT_SCALE).astype(BF16)
            dq_parts.append(_dot(ds, kv_k[:, sl], NN))
            dk_ref[:, sl] += _dot(ds, q[:, sl], TN)
            dv_ref[:, sl] += _dot(p.astype(BF16), do[:, sl], TN)
        dproj_ref[:, 5 * WIDTH:6 * WIDTH] = jnp.concatenate(dq_parts, axis=1).astype(BF16)

        @pl.when(step == n_tiles - 1)
        def _():
            keep = _tril_mask(transposed=False)
            for h in range(N_SGU_HEADS):
                dsw_ref[h] = jnp.where(keep, dsw_ref[h], 0.0)
            dsb_ref[...] = jnp.concatenate(
                [jnp.sum(dbias_ref[:, h * CHUNK:(h + 1) * CHUNK], axis=1, keepdims=True)
                 for h in range(N_SGU_HEADS)], axis=1)

    const2 = lambda i: (0, 0)
    const3 = lambda i: (0, 0, 0)
    out_shapes = (
        jax.ShapeDtypeStruct((s, 7 * WIDTH), BF16),
        jax.ShapeDtypeStruct((4, GROUP, GROUP), F32),
        jax.ShapeDtypeStruct((1, WIDTH), F32),
        jax.ShapeDtypeStruct((1, WIDTH), F32),
        jax.ShapeDtypeStruct((1, WIDTH), F32),
        jax.ShapeDtypeStruct((N_SGU_HEADS, CHUNK, CHUNK), F32),
        jax.ShapeDtypeStruct((CHUNK, N_SGU_HEADS), F32),
        jax.ShapeDtypeStruct((1, 3 * WIDTH), F32),
        jax.ShapeDtypeStruct((MEM_ROWS, WIDTH), F32),
        jax.ShapeDtypeStruct((MEM_ROWS, WIDTH), F32),
    )
    out_specs = (
        pl.BlockSpec((t, 7 * WIDTH), lambda i: (order(i), 0)),
        pl.BlockSpec((4, GROUP, GROUP), const3),
        pl.BlockSpec((1, WIDTH), const2),
        pl.BlockSpec((1, WIDTH), const2),
        pl.BlockSpec((1, WIDTH), const2),
        pl.BlockSpec((N_SGU_HEADS, CHUNK, CHUNK), const3),
        pl.BlockSpec((CHUNK, N_SGU_HEADS), const2),
        pl.BlockSpec((1, 3 * WIDTH), const2),
        pl.BlockSpec((MEM_ROWS, WIDTH), const2),
        pl.BlockSpec((MEM_ROWS, WIDTH), const2),
    )
    in_specs = _branch_specs(t, n_tiles, order) + [
        pl.BlockSpec((N_SGU_HEADS, CHUNK, CHUNK), const3),
        pl.BlockSpec((t, 3 * WIDTH), lambda i: (order(i), 0)),
    ]
    return pl.pallas_call(
        body, name="branches_bwd", grid=(n_tiles,),
        in_specs=in_specs, out_specs=out_specs, out_shape=out_shapes,
        scratch_shapes=[pltpu.VMEM((HALO, WIDTH), F32), pltpu.VMEM((CHUNK, WIDTH), F32)],
        compiler_params=_params(1),
    )(proj, proj, pool_w, pool_scale, ln_g, ln_b, sgu_w, bias_full, k, v, branch_norm, sgu_wt, dy)


def _out_loss(y, w_out, x, target, g_post, tm):
    s, d = x.shape
    e_w = y.shape[1]
    n_tiles = s // tm

    def body(y_ref, w_ref, x_ref, t_ref, g_ref, loss_ref, dz_ref, dout_ref, dy_ref, dg_ref, sq_ref):
        i = pl.program_id(0)

        @pl.when(i == 0)
        def _():
            sq_ref[...] = jnp.zeros(sq_ref.shape, F32)
            dg_ref[...] = jnp.zeros(dg_ref.shape, F32)

        w = w_ref[...]
        out = _dot(y_ref[...], w, NN)
        r = lax.rsqrt(jnp.mean(out * out, axis=-1, keepdims=True) + EPS)
        outn = out * r
        g = g_ref[...]
        err = (x_ref[...] + outn * g) - t_ref[...]
        sq_ref[...] += jnp.sum(err * err, axis=0, keepdims=True)
        dz = err * (1.0 / d)
        dz_ref[...] = dz
        dg_ref[...] += jnp.sum(dz * outn, axis=0, keepdims=True)
        doutn = dz * g
        dout = (r * (doutn - outn * jnp.mean(doutn * outn, axis=-1, keepdims=True))).astype(BF16)
        dout_ref[...] = dout
        dy_ref[...] = _dot(dout, w, NT).astype(BF16)

        @pl.when(i == n_tiles - 1)
        def _():
            loss_ref[...] = 0.5 * jnp.sum(sq_ref[...], axis=1, keepdims=True) * (1.0 / d)

    row = lambda i: (i, 0)
    const2 = lambda i: (0, 0)
    return pl.pallas_call(
        body, name="out_loss", grid=(n_tiles,),
        in_specs=[
            pl.BlockSpec((tm, e_w), row),
            pl.BlockSpec((e_w, d), const2, pipeline_mode=pl.Buffered(1)),
            pl.BlockSpec((tm, d), row),
            pl.BlockSpec((tm, d), row),
            pl.BlockSpec((1, d), const2),
        ],
        out_specs=(
            pl.BlockSpec((1, 1), const2),
            pl.BlockSpec((tm, d), row),
            pl.BlockSpec((tm, d), row),
            pl.BlockSpec((tm, e_w), row),
            pl.BlockSpec((1, d), const2),
        ),
        out_shape=(
            jax.ShapeDtypeStruct((1, 1), F32),
            jax.ShapeDtypeStruct((s, d), F32),
            jax.ShapeDtypeStruct((s, d), BF16),
            jax.ShapeDtypeStruct((s, e_w), BF16),
            jax.ShapeDtypeStruct((1, d), F32),
        ),
        scratch_shapes=[pltpu.VMEM((1, d), F32)],
        compiler_params=_params(1),
    )(y, w_out, x, target, g_post)


def _dx_call(dproj, w_in, x, dz, g_pre, tm, tk):
    s, d = x.shape
    k_total = dproj.shape[1]
    nk = k_total // tk
    n_tiles = s // tm

    def body(dp_ref, w_ref, x_ref, dz_ref, g_ref, dx_ref, dg_ref, acc_ref):
        i, kk = pl.program_id(0), pl.program_id(1)
        part = _dot(dp_ref[...], w_ref[...], NT)

        @pl.when(kk == 0)
        def _():
            acc_ref[...] = part

        @pl.when(kk > 0)
        def _():
            acc_ref[...] += part

        @pl.when((i == 0) & (kk == 0))
        def _():
            dg_ref[...] = jnp.zeros(dg_ref.shape, F32)

        @pl.when(kk == nk - 1)
        def _():
            dh = acc_ref[...]
            xv = x_ref[...]
            r = lax.rsqrt(jnp.mean(xv * xv, axis=-1, keepdims=True) + EPS)
            xhat = xv * r
            dg_ref[...] += jnp.sum(dh * xhat, axis=0, keepdims=True)
            dxhat = dh * g_ref[...]
            dx_ref[...] = dz_ref[...] + r * (dxhat - xhat * jnp.mean(dxhat * xhat, axis=-1, keepdims=True))

    row = lambda i, kk: (i, 0)
    const2 = lambda i, kk: (0, 0)
    return pl.pallas_call(
        body, name="dx", grid=(n_tiles, nk),
        in_specs=[
            pl.BlockSpec((tm, tk), lambda i, kk: (i, kk)),
            pl.BlockSpec((d, tk), lambda i, kk: (0, kk)),
            pl.BlockSpec((tm, d), row),
            pl.BlockSpec((tm, d), row),
            pl.BlockSpec((1, d), const2),
        ],
        out_specs=(pl.BlockSpec((tm, d), row), pl.BlockSpec((1, d), const2)),
        out_shape=(jax.ShapeDtypeStruct((s, d), F32), jax.ShapeDtypeStruct((1, d), F32)),
        scratch_shapes=[pltpu.VMEM((tm, d), F32)],
        compiler_params=_params(2),
    )(dproj, w_in, x, dz, g_pre)


def _rows_tile(rows, cols, n_arrays, itemsize=4):
    budget = 24 * 1024 * 1024 // (2 * n_arrays * cols * itemsize)
    best = None
    for cand in range(16, rows + 1, 16):
        if rows % cand == 0 and cand <= max(budget, 16):
            best = cand
    return best if best is not None else rows


def _elementwise(fn, inputs, out_dtypes, name):
    rows, cols = inputs[0].shape
    tr = _rows_tile(rows, cols, len(inputs) + len(out_dtypes))
    n_in = len(inputs)

    def body(*refs):
        outs = fn(*[r[...] for r in refs[:n_in]])
        for o_ref, o in zip(refs[n_in:], outs):
            o_ref[...] = o.astype(o_ref.dtype)

    spec = pl.BlockSpec((tr, cols), lambda i: (i, 0))
    return pl.pallas_call(
        body, name=name, grid=(rows // tr,),
        in_specs=[spec] * n_in, out_specs=tuple([spec] * len(out_dtypes)),
        out_shape=tuple(jax.ShapeDtypeStruct((rows, cols), dt) for dt in out_dtypes),
        compiler_params=_params(1),
    )(*inputs)


def _pair_sum(mine, theirs):
    return ((mine.astype(F32) + theirs.astype(F32)),)


def _four_sum(own, t0, t1, t2):
    return ((((own.astype(F32) + t0.astype(F32)) + t1.astype(F32)) + t2.astype(F32)),)


def _adamw(w, g, m, v):
    m = ADAM_B1 * m + (1.0 - ADAM_B1) * g
    v = ADAM_B2 * v + (1.0 - ADAM_B2) * jnp.square(g)
    m_hat = m / (1.0 - ADAM_B1 ** ADAM_STEP)
    v_hat = v / (1.0 - ADAM_B2 ** ADAM_STEP)
    delta = -ADAM_LR * (m_hat / (jnp.sqrt(v_hat) + ADAM_EPS) + ADAM_WD * w)
    return delta, m, v


def _place():
    x, y, c = lax.axis_index("x"), lax.axis_index("y"), lax.axis_index("c")
    chips = [(1 - x, y), (x, 1 - y), (1 - x, 1 - y)]
    return x, y, c, chips


def _remote(src, dst, send_sem, recv_sem, to):
    return pltpu.make_async_remote_copy(src_ref=src, dst_ref=dst, send_sem=send_sem, recv_sem=recv_sem,
                                        device_id=to, device_id_type=MESH)


def _hbm_call(body, name, inputs, out_shapes, scratch):
    return pl.pallas_call(
        body, name=name,
        in_specs=[ANY] * len(inputs), out_specs=tuple([ANY] * len(out_shapes)), out_shape=tuple(out_shapes),
        scratch_shapes=scratch,
        compiler_params=pltpu.CompilerParams(has_side_effects=True),
    )(*inputs)


def _gather_weights(shards):
    n = len(shards)
    full_shapes = []
    for a, sh in enumerate(shards):
        if a == 0:
            full_shapes.append(jax.ShapeDtypeStruct((sh.shape[0], 4 * sh.shape[1]), sh.dtype))
        elif a == 3:
            full_shapes.append(jax.ShapeDtypeStruct((sh.shape[0], 4 * sh.shape[1], sh.shape[2]), sh.dtype))
        else:
            full_shapes.append(jax.ShapeDtypeStruct((4 * sh.shape[0], sh.shape[1]), sh.dtype))

    def shard_half(a, ref, cc):
        if a == 0:
            rows = ref.shape[0] // 2
            return ref.at[pl.ds(cc * rows, rows), :]
        if a == 3:
            rows = ref.shape[1] // 2
            return ref.at[:, pl.ds(cc * rows, rows), :]
        rows = ref.shape[0] // 2
        return ref.at[pl.ds(cc * rows, rows), :]

    def full_half(a, ref, chip, cc):
        if a == 0:
            rows, cols = ref.shape[0] // 2, ref.shape[1] // 4
            return ref.at[pl.ds(cc * rows, rows), pl.ds(pl.multiple_of(chip * cols, 128), cols)]
        if a == 3:
            rows = ref.shape[1] // 8
            return ref.at[:, pl.ds(pl.multiple_of((2 * chip + cc) * rows, 16), rows), :]
        rows = ref.shape[0] // 8
        return ref.at[pl.ds(pl.multiple_of((2 * chip + cc) * rows, 16), rows), :]

    def body(*refs):
        shard_refs, full_refs = refs[:n], refs[n:2 * n]
        send_sems, recv_sems, local_sems = refs[2 * n:]
        x, y, c, chips = _place()
        me = 2 * x + y
        sibling = (x, y, 1 - c)
        local = []
        for a in range(n):
            for cc in range(2):
                cp = pltpu.make_async_copy(shard_half(a, shard_refs[a], cc), full_half(a, full_refs[a], me, cc),
                                           local_sems.at[2 * a + cc])
                cp.start()
                local.append(cp)
        sends = []
        for p, chip in enumerate(chips):
            for a in range(n):
                cp = _remote(shard_half(a, shard_refs[a], c), full_half(a, full_refs[a], me, c),
                             send_sems.at[6 * a + p], recv_sems.at[6 * a + p], (*chip, c))
                cp.start()
                sends.append(cp)
        for p, chip in enumerate(chips):
            them = 2 * chip[0] + chip[1]
            for a in range(n):
                landed = full_half(a, full_refs[a], them, c)
                _remote(landed, landed, send_sems.at[6 * a + p], recv_sems.at[6 * a + p], (*chip, c)).wait_recv()
                cp = _remote(landed, landed, send_sems.at[6 * a + 3 + p], recv_sems.at[6 * a + 3 + p], sibling)
                cp.start()
                sends.append(cp)
        for p, chip in enumerate(chips):
            them = 2 * chip[0] + chip[1]
            for a in range(n):
                passed = full_half(a, full_refs[a], them, 1 - c)
                _remote(passed, passed, send_sems.at[6 * a + 3 + p], recv_sems.at[6 * a + 3 + p], sibling).wait_recv()
        for cp in sends:
            cp.wait_send()
        for cp in local:
            cp.wait()

    return _hbm_call(body, "gather_weights", shards, full_shapes,
                     [pltpu.SemaphoreType.DMA((6 * n,)), pltpu.SemaphoreType.DMA((6 * n,)),
                      pltpu.SemaphoreType.DMA((2 * n,))])


def _exchange_halves(grads):
    n = len(grads)
    arrays = [g for g, _ in grads]
    out_shapes = []
    for g, ax in grads:
        shape = tuple(1 if i == ax else dim for i, dim in enumerate(g.shape))
        out_shapes += [jax.ShapeDtypeStruct(shape, g.dtype)] * 2

    def half(ref, ax, cc):
        idx = tuple(pl.ds(cc, 1) if i == ax else slice(None) for i in range(len(ref.shape)))
        return ref.at[idx]

    def body(*refs):
        in_refs, out_refs = refs[:n], refs[n:3 * n]
        send_sems, recv_sems, local_sems = refs[3 * n:]
        x, y, c, _ = _place()
        sibling = (x, y, 1 - c)
        copies = []
        for a in range(n):
            ax = grads[a][1]
            mine, theirs = out_refs[2 * a], out_refs[2 * a + 1]
            loc = pltpu.make_async_copy(half(in_refs[a], ax, c), mine, local_sems.at[a])
            loc.start()
            rem = _remote(half(in_refs[a], ax, 1 - c), theirs, send_sems.at[a], recv_sems.at[a], sibling)
            rem.start()
            copies.append((loc, rem))
        for loc, rem in copies:
            rem.wait()
            loc.wait()

    outs = _hbm_call(body, "exchange_halves", arrays, out_shapes,
                     [pltpu.SemaphoreType.DMA((n,)), pltpu.SemaphoreType.DMA((n,)), pltpu.SemaphoreType.DMA((n,))])
    return [(outs[2 * a], outs[2 * a + 1]) for a in range(n)]


def _scatter_to_chips(parts):
    n = len(parts)
    arrays = [p for p, _ in parts]

    def block_shape(p, ax):
        if ax == len(p.shape) - 1:
            return p.shape[:-1] + (p.shape[-1] // 4,)
        return tuple(1 if i == ax else dim for i, dim in enumerate(p.shape))

    out_shapes = []
    for p, ax in parts:
        shape = block_shape(p, ax)
        out_shapes += [jax.ShapeDtypeStruct(shape, p.dtype), jax.ShapeDtypeStruct((3,) + shape, p.dtype)]

    def block(ref, ax, chip):
        rank = len(ref.shape)
        if ax == rank - 1:
            cols = ref.shape[-1] // 4
            last = pl.ds(pl.multiple_of(chip * cols, 128), cols)
            return ref.at[tuple([slice(None)] * (rank - 1) + [last])]
        return ref.at[tuple(pl.ds(chip, 1) if i == ax else slice(None) for i in range(rank))]

    def body(*refs):
        in_refs, out_refs = refs[:n], refs[n:3 * n]
        send_sems, recv_sems, local_sems = refs[3 * n:]
        x, y, c, chips = _place()
        me = 2 * x + y
        copies = []
        for a in range(n):
            ax = parts[a][1]
            own, landed = out_refs[2 * a], out_refs[2 * a + 1]
            loc = pltpu.make_async_copy(block(in_refs[a], ax, me), own, local_sems.at[a])
            loc.start()
            copies.append(loc)
            for p, chip in enumerate(chips):
                them = 2 * chip[0] + chip[1]
                rem = _remote(block(in_refs[a], ax, them), landed.at[p], send_sems.at[3 * a + p],
                              recv_sems.at[3 * a + p], (*chip, c))
                rem.start()
                copies.append(rem)
        for cp in copies:
            cp.wait()

    outs = _hbm_call(body, "scatter_to_chips", arrays, out_shapes,
                     [pltpu.SemaphoreType.DMA((3 * n,)), pltpu.SemaphoreType.DMA((3 * n,)),
                      pltpu.SemaphoreType.DMA((n,))])
    return [(outs[2 * a], outs[2 * a + 1]) for a in range(n)]


def _join_halves(halves):
    n = len(halves)
    out_shapes = [jax.ShapeDtypeStruct((2,) + h.shape, h.dtype) for h in halves]

    def body(*refs):
        in_refs, out_refs = refs[:n], refs[n:2 * n]
        send_sems, recv_sems, local_sems = refs[2 * n:]
        x, y, c, _ = _place()
        sibling = (x, y, 1 - c)
        copies = []
        for a in range(n):
            loc = pltpu.make_async_copy(in_refs[a], out_refs[a].at[c], local_sems.at[a])
            loc.start()
            rem = _remote(in_refs[a], out_refs[a].at[c], send_sems.at[a], recv_sems.at[a], sibling)
            rem.start()
            copies.append((loc, rem))
        for a, (loc, rem) in enumerate(copies):
            rem.wait_send()
            arrived = out_refs[a].at[1 - c]
            _remote(arrived, arrived, send_sems.at[a], recv_sems.at[a], sibling).wait_recv()
            loc.wait()

    return _hbm_call(body, "join_halves", halves, out_shapes,
                     [pltpu.SemaphoreType.DMA((n,)), pltpu.SemaphoreType.DMA((n,)), pltpu.SemaphoreType.DMA((n,))])


def _allreduce_small(packed):
    rows, lanes = packed.shape

    def body(in_ref, out_ref, gath_ref, send_sems, recv_sems):
        x, y, c, _ = _place()
        me = 4 * x + 2 * y + c
        gath_ref[me] = in_ref[...]
        peers = [(x ^ (k >> 2), y ^ ((k >> 1) & 1), c ^ (k & 1)) for k in range(1, 8)]
        sends = [_remote(in_ref, gath_ref.at[me], send_sems.at[k], recv_sems.at[k], peer)
                 for k, peer in enumerate(peers)]
        for cp in sends:
            cp.start()
        for k, peer in enumerate(peers):
            slot = gath_ref.at[4 * peer[0] + 2 * peer[1] + peer[2]]
            _remote(slot, slot, send_sems.at[k], recv_sems.at[k], peer).wait_recv()
        for cp in sends:
            cp.wait_send()
        total = gath_ref[0]
        for dev in range(1, 8):
            total = total + gath_ref[dev]
        out_ref[...] = total

    vmem = pl.BlockSpec(memory_space=pltpu.VMEM)
    return pl.pallas_call(
        body, name="allreduce_small",
        in_specs=[vmem], out_specs=vmem, out_shape=jax.ShapeDtypeStruct((rows, lanes), F32),
        scratch_shapes=[pltpu.VMEM((8, rows, lanes), F32), pltpu.SemaphoreType.DMA((7,)),
                        pltpu.SemaphoreType.DMA((7,))],
        compiler_params=pltpu.CompilerParams(has_side_effects=True, vmem_limit_bytes=32 * 1024 * 1024),
    )(packed)


SMALL = ("norm_pre", "pool_scale", "sgu_ln_g", "sgu_ln_b", "sgu_w", "sgu_b", "mem_norm", "branch_norm", "norm_post")
LARGE = ("w_in", "pool_w", "w_kv", "w_out")
ORDER = ("norm_pre", "w_in", "pool_w", "pool_scale", "sgu_ln_g", "sgu_ln_b", "sgu_w", "sgu_b", "mem_norm", "w_kv",
         "branch_norm", "w_out", "norm_post")


def _pack(arrays):
    return jnp.concatenate([a.reshape(-1, 128) for a in arrays], axis=0)


def _unpack(packed, like):
    out, row = [], 0
    for a in like:
        rows = a.size // 128
        out.append(packed[row:row + rows].reshape(a.shape))
        row += rows
    return out


def kernel(x, mem, norm_pre, w_in, pool_w, pool_scale, sgu_ln_g, sgu_ln_b, sgu_w, sgu_b, mem_norm, w_kv, branch_norm, w_out, norm_post, loss_target, m_norm_pre, m_w_in, m_pool_w, m_pool_scale, m_sgu_ln_g, m_sgu_ln_b, m_sgu_w, m_sgu_b, m_mem_norm, m_w_kv, m_branch_norm, m_w_out, m_norm_post, v_norm_pre, v_w_in, v_pool_w, v_pool_scale, v_sgu_ln_g, v_sgu_ln_b, v_sgu_w, v_sgu_b, v_mem_norm, v_w_kv, v_branch_norm, v_w_out, v_norm_post):
    weights = dict(norm_pre=norm_pre, w_in=w_in, pool_w=pool_w, pool_scale=pool_scale, sgu_ln_g=sgu_ln_g,
                   sgu_ln_b=sgu_ln_b, sgu_w=sgu_w, sgu_b=sgu_b, mem_norm=mem_norm, w_kv=w_kv, branch_norm=branch_norm,
                   w_out=w_out, norm_post=norm_post)
    mom1 = dict(norm_pre=m_norm_pre, w_in=m_w_in, pool_w=m_pool_w, pool_scale=m_pool_scale, sgu_ln_g=m_sgu_ln_g,
                sgu_ln_b=m_sgu_ln_b, sgu_w=m_sgu_w, sgu_b=m_sgu_b, mem_norm=m_mem_norm, w_kv=m_w_kv,
                branch_norm=m_branch_norm, w_out=m_w_out, norm_post=m_norm_post)
    mom2 = dict(norm_pre=v_norm_pre, w_in=v_w_in, pool_w=v_pool_w, pool_scale=v_pool_scale, sgu_ln_g=v_sgu_ln_g,
                sgu_ln_b=v_sgu_ln_b, sgu_w=v_sgu_w, sgu_b=v_sgu_b, mem_norm=v_mem_norm, w_kv=v_w_kv,
                branch_norm=v_branch_norm, w_out=v_w_out, norm_post=v_norm_post)

    s, d = x.shape[1], x.shape[2]
    x2, mem2, tgt2 = x[0], mem[0], loss_target[0]
    t_branch = min(256, s)
    tm = min(512, s)

    wi_full, wkv_full, wo_full, pw_full = _gather_weights(
        [w_in[0].astype(BF16), w_kv[0].astype(BF16), w_out[0].astype(BF16), pool_w[0].astype(BF16)])

    mem_g = mem_norm.reshape(1, d)
    k_m, v_m = _kv_fwd(mem2, mem_g, wkv_full)
    h = _rms_pre(x2, norm_pre, tm)
    proj = _matmul(h, wi_full, NN, F32, min(1024, s), 512, d, "proj")
    bias_full = jnp.repeat(sgu_b[0].T, CHUNK, axis=1)
    y = _branches_fwd(proj, pw_full, pool_scale, sgu_ln_g, sgu_ln_b, sgu_w[0], bias_full, k_m, v_m, branch_norm,
                      t_branch)
    loss_local, dz, dout, dy, g_norm_post = _out_loss(y, wo_full, x2, tgt2, norm_post, min(256, s))

    g_wo = _matmul(y, dout, TN, BF16, 1536, 1024, min(512, s), "grad_w_out")
    (dproj, g_pw, g_pool_scale, g_ln_g, g_ln_b, g_sgu_w, g_sgu_b_t, g_branch_norm, dk, dv) = _branches_bwd(
        proj, dy, pw_full, pool_scale, sgu_ln_g, sgu_ln_b, sgu_w[0], jnp.swapaxes(sgu_w[0], 1, 2), bias_full,
        k_m, v_m, branch_norm, t_branch)
    g_wkv, g_mem_norm = _kv_bwd(mem2, mem_g, wkv_full, dk, dv)
    g_wi = _matmul(h, dproj, TN, BF16, d, 1024, min(512, s), "grad_w_in")
    grad_x, g_norm_pre = _dx_call(dproj, wi_full, x2, dz, norm_pre, tm, 1024)

    n_in, n_kv, n_out = wi_full.shape[1], wkv_full.shape[0], wo_full.shape[0]
    pairs = _exchange_halves([
        (g_wi.reshape(2, d // 2, n_in), 0),
        (g_wkv.reshape(4, 2, n_kv // 8, wkv_full.shape[1]), 1),
        (g_wo.reshape(4, 2, n_out // 8, d), 1),
        (g_pw.astype(BF16).reshape(4, 4, 2, GROUP // 8, GROUP), 2),
    ])
    flat = lambda a: a.reshape(-1, a.shape[-1])
    psum = [_elementwise(_pair_sum, [flat(mine), flat(theirs)], [BF16], "pair_sum_%d" % i)[0]
            for i, (mine, theirs) in enumerate(pairs)]
    scattered = _scatter_to_chips([
        (psum[0], 1),
        (psum[1].reshape(4, n_kv // 8, wkv_full.shape[1]), 0),
        (psum[2].reshape(4, n_out // 8, d), 0),
        (psum[3].reshape(4, 4, GROUP // 8, GROUP), 1),
    ])
    halves = []
    for i, (own, landed) in enumerate(scattered):
        cols = own.shape[-1]
        halves.append(_elementwise(_four_sum, [own.reshape(-1, cols)] + [landed[p].reshape(-1, cols) for p in range(3)],
                                   [F32], "chip_sum_%d" % i)[0])
    joined = _join_halves(halves)
    grads = {
        "w_in": joined[0].reshape(w_in.shape),
        "w_kv": joined[1].reshape(w_kv.shape),
        "w_out": joined[2].reshape(w_out.shape),
        "pool_w": jnp.swapaxes(joined[3].reshape(2, 4, GROUP // 8, GROUP), 0, 1).reshape(pool_w.shape),
    }

    small_local = dict(norm_pre=g_norm_pre, pool_scale=g_pool_scale, sgu_ln_g=g_ln_g, sgu_ln_b=g_ln_b,
                       sgu_w=g_sgu_w, sgu_b=g_sgu_b_t.T, mem_norm=g_mem_norm, branch_norm=g_branch_norm,
                       norm_post=g_norm_post)
    small_sum = _allreduce_small(_pack([small_local[n] for n in SMALL]))
    for n, g in zip(SMALL, _unpack(small_sum, [weights[n] for n in SMALL])):
        grads[n] = g

    delta, new_m, new_v = {}, {}, {}
    packed = [_pack([src[n] for n in SMALL]) for src in (weights, grads, mom1, mom2)]
    outs = _elementwise(_adamw, packed, [F32, F32, F32], "adamw_small")
    for dst, o in zip((delta, new_m, new_v), outs):
        for n, a in zip(SMALL, _unpack(o, [weights[n] for n in SMALL])):
            dst[n] = a
    for n in LARGE:
        cols = weights[n].shape[-1]
        outs = _elementwise(_adamw, [src[n].reshape(-1, cols) for src in (weights, grads, mom1, mom2)],
                            [F32, F32, F32], "adamw_" + n)
        for dst, o in zip((delta, new_m, new_v), outs):
            dst[n] = o.reshape(weights[n].shape)

    loss = lax.psum(loss_local[0, 0], ("x", "y", "c"))
    return (loss, grad_x[None], *[grads[n] for n in ORDER], *[delta[n] for n in ORDER],
            *[new_m[n] for n in ORDER], *[new_v[n] for n in ORDER])
```

```python
import functools

import jax
import jax.numpy as jnp
from jax import lax
from jax.experimental import pallas as pl
from jax.experimental.pallas import tpu as pltpu

F32 = jnp.float32
BF16 = jnp.bfloat16
EPS = 1e-6
MESH = pl.DeviceIdType.MESH
ANY = pl.BlockSpec(memory_space=pl.ANY)

POOL_WINDOWS = (2, 4, 8, 16)
GROUP = 256
HALO = 16
CHUNK = 128
N_SGU_HEADS = 8
N_ATT_HEADS = 4
ATT_DIM = 256
WIDTH = 1024
ATT_SCALE = 1.0 / 16.0

ADAM_LR = 0.001
ADAM_B1 = 0.9
ADAM_B2 = 0.999
ADAM_EPS = 1e-08
ADAM_WD = 0.01
ADAM_STEP = 10

VMEM_LIMIT = 60 * 1024 * 1024


def _params(n_grid_axes, vmem=VMEM_LIMIT):
    return pltpu.CompilerParams(dimension_semantics=("arbitrary",) * n_grid_axes, vmem_limit_bytes=vmem)


def _dot(a, b, dims):
    return lax.dot_general(a, b, (dims, ((), ())), preferred_element_type=F32)


NN = ((1,), (0,))
NT = ((1,), (1,))
TN = ((0,), (0,))


def _matmul(a, b, dims, out_dtype, tm, tn, tk, name):
    if dims == NN:
        (m, k), n = a.shape, b.shape[1]
        a_spec = pl.BlockSpec((tm, tk), lambda i, j, kk: (i, kk))
        b_spec = pl.BlockSpec((tk, tn), lambda i, j, kk: (kk, j))
    else:
        (k, m), n = a.shape, b.shape[1]
        a_spec = pl.BlockSpec((tk, tm), lambda i, j, kk: (kk, i))
        b_spec = pl.BlockSpec((tk, tn), lambda i, j, kk: (kk, j))
    nk = k // tk

    def body(a_ref, b_ref, o_ref, acc_ref):
        kk = pl.program_id(2)
        part = _dot(a_ref[...], b_ref[...], dims)

        @pl.when(kk == 0)
        def _():
            acc_ref[...] = part

        @pl.when(kk > 0)
        def _():
            acc_ref[...] += part

        @pl.when(kk == nk - 1)
        def _():
            o_ref[...] = acc_ref[...].astype(out_dtype)

    return pl.pallas_call(
        body, name=name, grid=(m // tm, n // tn, nk),
        in_specs=[a_spec, b_spec],
        out_specs=pl.BlockSpec((tm, tn), lambda i, j, kk: (i, j)),
        out_shape=jax.ShapeDtypeStruct((m, n), out_dtype),
        scratch_shapes=[pltpu.VMEM((tm, tn), F32)],
        compiler_params=_params(3),
    )(a, b)


def _rms_pre(x, g, tm):
    s, d = x.shape

    def body(x_ref, g_ref, h_ref):
        xv = x_ref[...]
        r = lax.rsqrt(jnp.mean(xv * xv, axis=-1, keepdims=True) + EPS)
        h_ref[...] = (xv * r * g_ref[...]).astype(BF16)

    return pl.pallas_call(
        body, name="rms_pre", grid=(s // tm,),
        in_specs=[pl.BlockSpec((tm, d), lambda i: (i, 0)), pl.BlockSpec((1, d), lambda i: (0, 0))],
        out_specs=pl.BlockSpec((tm, d), lambda i: (i, 0)),
        out_shape=jax.ShapeDtypeStruct((s, d), BF16),
        compiler_params=_params(1),
    )(x, g)


def _kv_fwd(mem, g, w_kv):
    m, d = mem.shape

    def body(mem_ref, g_ref, w_ref, k_ref, v_ref):
        mv = mem_ref[...]
        r = lax.rsqrt(jnp.mean(mv * mv, axis=-1, keepdims=True) + EPS)
        mem_n = (mv * r * g_ref[...]).astype(BF16)
        kv = _dot(mem_n, w_ref[...], NN)
        k_ref[...] = kv[:, :WIDTH].astype(BF16)
        v_ref[...] = kv[:, WIDTH:].astype(BF16)

    return pl.pallas_call(
        body, name="kv_fwd",
        out_shape=(jax.ShapeDtypeStruct((m, WIDTH), BF16), jax.ShapeDtypeStruct((m, WIDTH), BF16)),
        compiler_params=_params(0),
    )(mem, g, w_kv)


def _kv_bwd(mem, g, w_kv, dk, dv):
    m, d = mem.shape
    n = w_kv.shape[1]
    col = 512

    def body(mem_ref, g_ref, w_ref, dk_ref, dv_ref, dw_ref, dg_ref):
        mv = mem_ref[...]
        r = lax.rsqrt(jnp.mean(mv * mv, axis=-1, keepdims=True) + EPS)
        mem_hat = mv * r
        mem_n = (mem_hat * g_ref[...]).astype(BF16)
        dkv = jnp.concatenate([dk_ref[...], dv_ref[...]], axis=1).astype(BF16)
        for j in range(n // col):
            dw_ref[:, j * col:(j + 1) * col] = _dot(mem_n, dkv[:, j * col:(j + 1) * col], TN).astype(BF16)
        dmem_n = _dot(dkv, w_ref[...], NT)
        dg_ref[...] = jnp.sum(dmem_n * mem_hat, axis=0, keepdims=True)

    return pl.pallas_call(
        body, name="kv_bwd",
        out_shape=(jax.ShapeDtypeStruct((d, n), BF16), jax.ShapeDtypeStruct((1, d), F32)),
        compiler_params=_params(0),
    )(mem, g, w_kv, dk, dv)


def _sigmoid(x):
    return 1.0 / (1.0 + jnp.exp(-x))


def _inv_counts(t0, t):
    pos = (t0 + lax.broadcasted_iota(jnp.int32, (t, 1), 0) + 1).astype(F32)
    return [1.0 / jnp.minimum(pos, float(w)) for w in POOL_WINDOWS]


def _window_sums(ext, t, backward):
    n = t + HALO
    parts = []
    for gi, w in enumerate(POOL_WINDOWS):
        s = ext[:, gi * GROUP:(gi + 1) * GROUP]
        k = 1
        while k < w:
            s = s + pltpu.roll(s, (n - k) if backward else k, axis=0)
            k *= 2
        parts.append(s[:t] if backward else s[HALO:])
    return parts


def _pool_fwd(xa, halo, inv, pool_w):
    t = xa.shape[0]
    sums = _window_sums(jnp.concatenate([halo, xa], axis=0), t, backward=False)
    d = jnp.concatenate([sums[gi] * inv[gi] - xa[:, gi * GROUP:(gi + 1) * GROUP] for gi in range(4)], axis=1)
    d = d.astype(BF16)
    y = jnp.concatenate([_dot(d[:, gi * GROUP:(gi + 1) * GROUP], pool_w[gi], NN) for gi in range(4)], axis=1)
    return d, y


def _layernorm_fwd(v):
    mu = jnp.mean(v, axis=-1, keepdims=True)
    xc = v - mu
    rstd = lax.rsqrt(jnp.mean(xc * xc, axis=-1, keepdims=True) + EPS)
    return xc * rstd, rstd


def _tril_mask(transposed):
    r = lax.broadcasted_iota(jnp.int32, (CHUNK, CHUNK), 0)
    c = lax.broadcasted_iota(jnp.int32, (CHUNK, CHUNK), 1)
    return (r <= c) if transposed else (r >= c)


def _sgu_mix(w_ref, vals, transposed):
    t = vals.shape[0]
    mask = _tril_mask(transposed)
    ws = [jnp.where(mask, w_ref[h], 0.0).astype(BF16) for h in range(N_SGU_HEADS)]
    rows = []
    for ci in range(t // CHUNK):
        blk = vals[ci * CHUNK:(ci + 1) * CHUNK]
        rows.append(jnp.concatenate(
            [_dot(ws[h], blk[:, h * CHUNK:(h + 1) * CHUNK], NN) for h in range(N_SGU_HEADS)], axis=1))
    return jnp.concatenate(rows, axis=0)


def _attn_fwd(q, k, v):
    ps, os_ = [], []
    for h in range(N_ATT_HEADS):
        sl = slice(h * ATT_DIM, (h + 1) * ATT_DIM)
        s = _dot(q[:, sl], k[:, sl], NT) * ATT_SCALE
        s = s - jnp.max(s, axis=-1, keepdims=True)
        e = jnp.exp(s)
        p = e / jnp.sum(e, axis=-1, keepdims=True)
        ps.append(p)
        os_.append(_dot(p.astype(BF16), v[:, sl], NN))
    return ps, jnp.concatenate(os_, axis=1)


def _rms_branch(y_pre):
    r = lax.rsqrt(jnp.mean(y_pre * y_pre, axis=-1, keepdims=True) + EPS)
    return y_pre * r, r


def _branch_specs(t, n_tiles, order):
    width_in = 7 * WIDTH
    tile = lambda i: order(i)
    per_halo = t // HALO
    const2 = lambda i: (0, 0)
    const3 = lambda i: (0, 0, 0)
    return [
        pl.BlockSpec((t, width_in), lambda i: (tile(i), 0)),
        pl.BlockSpec((HALO, WIDTH), lambda i: (jnp.maximum(tile(i) * per_halo - 1, 0), 0)),
        pl.BlockSpec((4, GROUP, GROUP), const3),
        pl.BlockSpec((1, WIDTH), const2),
        pl.BlockSpec((1, WIDTH), const2),
        pl.BlockSpec((1, WIDTH), const2),
        pl.BlockSpec((N_SGU_HEADS, CHUNK, CHUNK), const3),
        pl.BlockSpec((CHUNK, WIDTH), const2),
        pl.BlockSpec((MEM_ROWS, WIDTH), const2),
        pl.BlockSpec((MEM_ROWS, WIDTH), const2),
        pl.BlockSpec((1, 3 * WIDTH), const2),
    ]


MEM_ROWS = 256


def _branches_fwd(proj, pool_w, pool_scale, ln_g, ln_b, sgu_w, bias_full, k, v, branch_norm, t):
    s = proj.shape[0]
    n_tiles = s // t

    def body(proj_ref, halo_ref, pw_ref, ps_ref, lg_ref, lb_ref, sw_ref, sb_ref, k_ref, v_ref, bn_ref, y_ref):
        i = pl.program_id(0)
        col = lambda j: proj_ref[:, j * WIDTH:(j + 1) * WIDTH]
        bn = bn_ref[...]
        halo = jnp.where(i > 0, halo_ref[...], 0.0)
        _, y_pool = _pool_fwd(col(0), halo, _inv_counts(i * t, t), pw_ref[...])
        ga = col(1)
        ya = y_pool * ps_ref[...] * (ga * _sigmoid(ga))
        y_ref[:, 0:WIDTH] = (_rms_branch(ya)[0] * bn[:, 0:WIDTH]).astype(BF16)
        vhat, _ = _layernorm_fwd(col(3))
        vn = (vhat * lg_ref[...] + lb_ref[...]).astype(BF16)
        z = _sgu_mix(sw_ref, vn, transposed=False) + jnp.tile(sb_ref[...], (t // CHUNK, 1))
        gb = col(4)
        yb = col(2) * z * (gb * _sigmoid(gb))
        y_ref[:, WIDTH:2 * WIDTH] = (_rms_branch(yb)[0] * bn[:, WIDTH:2 * WIDTH]).astype(BF16)
        _, o = _attn_fwd(col(5).astype(BF16), k_ref[...], v_ref[...])
        gc = col(6)
        yc = o * (gc * _sigmoid(gc))
        y_ref[:, 2 * WIDTH:] = (_rms_branch(yc)[0] * bn[:, 2 * WIDTH:]).astype(BF16)

    return pl.pallas_call(
        body, name="branches_fwd", grid=(n_tiles,),
        in_specs=_branch_specs(t, n_tiles, lambda i: i),
        out_specs=pl.BlockSpec((t, 3 * WIDTH), lambda i: (i, 0)),
        out_shape=jax.ShapeDtypeStruct((s, 3 * WIDTH), BF16),
        compiler_params=_params(1),
    )(proj, proj, pool_w, pool_scale, ln_g, ln_b, sgu_w, bias_full, k, v, branch_norm)


def _branches_bwd(proj, dy, pool_w, pool_scale, ln_g, ln_b, sgu_w, sgu_wt, bias_full, k, v, branch_norm, t):
    s = proj.shape[0]
    n_tiles = s // t
    n_chunks = t // CHUNK
    order = lambda i: n_tiles - 1 - i

    def body(proj_ref, halo_ref, pw_ref, ps_ref, lg_ref, lb_ref, sw_ref, sb_ref, k_ref, v_ref, bn_ref,
             swt_ref, dy_ref,
             dproj_ref, dpw_ref, dps_ref, dlg_ref, dlb_ref, dsw_ref, dsb_ref, dbn_ref, dk_ref, dv_ref,
             carry_ref, dbias_ref):
        step = pl.program_id(0)
        i = order(step)

        @pl.when(step == 0)
        def _():
            for ref in (dpw_ref, dps_ref, dlg_ref, dlb_ref, dsw_ref, dbn_ref, dk_ref, dv_ref, carry_ref, dbias_ref):
                ref[...] = jnp.zeros(ref.shape, ref.dtype)

        col = lambda j: proj_ref[:, j * WIDTH:(j + 1) * WIDTH]
        bn = bn_ref[...]

        def norm_bwd(y_pre, sl):
            yhat, r = _rms_branch(y_pre)
            dyv = dy_ref[:, sl].astype(F32)
            dbn_ref[:, sl] += jnp.sum(dyv * yhat, axis=0, keepdims=True)
            dyhat = dyv * bn[:, sl]
            return r * (dyhat - yhat * jnp.mean(dyhat * yhat, axis=-1, keepdims=True))

        def gate(gv):
            sg = _sigmoid(gv)
            return gv * sg, sg * (1.0 + gv * (1.0 - sg))

        inv = _inv_counts(i * t, t)
        halo = jnp.where(i > 0, halo_ref[...], 0.0)
        pw = pw_ref[...]
        d, y_pool = _pool_fwd(col(0), halo, inv, pw)
        scale = ps_ref[...]
        silu_a, dsilu_a = gate(col(1))
        pa = y_pool * scale
        dya = norm_bwd(pa * silu_a, slice(0, WIDTH))
        dproj_ref[:, WIDTH:2 * WIDTH] = (dya * pa * dsilu_a).astype(BF16)
        dpa = dya * silu_a
        dps_ref[...] += jnp.sum(dpa * y_pool, axis=0, keepdims=True)
        dy_pool = (dpa * scale).astype(BF16)
        dd_parts, ddc_parts = [], []
        for gi in range(4):
            sl = slice(gi * GROUP, (gi + 1) * GROUP)
            dpw_ref[gi] += _dot(d[:, sl], dy_pool[:, sl], TN)
            dd = _dot(dy_pool[:, sl], pw[gi], NT)
            dd_parts.append(dd)
            ddc_parts.append(dd * inv[gi])
        ddc = jnp.concatenate(ddc_parts, axis=1)
        sums = _window_sums(jnp.concatenate([ddc, carry_ref[...]], axis=0), t, backward=True)
        carry_ref[...] = ddc[:HALO]
        dproj_ref[:, 0:WIDTH] = jnp.concatenate([sums[gi] - dd_parts[gi] for gi in range(4)], axis=1).astype(BF16)

        vhat, rstd = _layernorm_fwd(col(3))
        lg = lg_ref[...]
        vn = (vhat * lg + lb_ref[...]).astype(BF16)
        z = _sgu_mix(sw_ref, vn, transposed=False) + jnp.tile(sb_ref[...], (n_chunks, 1))
        u = col(2)
        silu_b, dsilu_b = gate(col(4))
        uz = u * z
        dyb = norm_bwd(uz * silu_b, slice(WIDTH, 2 * WIDTH))
        dproj_ref[:, 4 * WIDTH:5 * WIDTH] = (dyb * uz * dsilu_b).astype(BF16)
        duz = dyb * silu_b
        dproj_ref[:, 2 * WIDTH:3 * WIDTH] = (duz * z).astype(BF16)
        dz = duz * u
        dz_b = dz.astype(BF16)
        for ci in range(n_chunks):
            rows = slice(ci * CHUNK, (ci + 1) * CHUNK)
            dbias_ref[...] += dz[rows]
            for h in range(N_SGU_HEADS):
                sl = slice(h * CHUNK, (h + 1) * CHUNK)
                dsw_ref[h] += _dot(dz_b[rows, sl], vn[rows, sl], NT)
        dvn = _sgu_mix(swt_ref, dz_b, transposed=True)
        dlg_ref[...] += jnp.sum(dvn * vhat, axis=0, keepdims=True)
        dlb_ref[...] += jnp.sum(dvn, axis=0, keepdims=True)
        dvhat = dvn * lg
        dvb = rstd * (dvhat - jnp.mean(dvhat, axis=-1, keepdims=True)
                      - vhat * jnp.mean(dvhat * vhat, axis=-1, keepdims=True))
        dproj_ref[:, 3 * WIDTH:4 * WIDTH] = dvb.astype(BF16)

        q = col(5).astype(BF16)
        kv_k, kv_v = k_ref[...], v_ref[...]
        ps, o = _attn_fwd(q, kv_k, kv_v)
        silu_c, dsilu_c = gate(col(6))
        dyc = norm_bwd(o * silu_c, slice(2 * WIDTH, 3 * WIDTH))
        dproj_ref[:, 6 * WIDTH:7 * WIDTH] = (dyc * o * dsilu_c).astype(BF16)
        do = (dyc * silu_c).astype(BF16)
        dq_parts = []
        for h in range(N_ATT_HEADS):
            sl = slice(h * ATT_DIM, (h + 1) * ATT_DIM)
            p = ps[h]
            dp = _dot(do[:, sl], kv_v[:, sl], NT)
            ds = (p * (dp - jnp.sum(p * dp, axis=-1, keepdims=True)) * ATT_SCALE).astype(BF16)
            dq_parts.append(_dot(ds, kv_k[:, sl], NN))
            dk_ref[:, sl] += _dot(ds, q[:, sl], TN)
            dv_ref[:, sl] += _dot(p.astype(BF16), do[:, sl], TN)
        dproj_ref[:, 5 * WIDTH:6 * WIDTH] = jnp.concatenate(dq_parts, axis=1).astype(BF16)

        @pl.when(step == n_tiles - 1)
        def _():
            keep = _tril_mask(transposed=False)
            for h in range(N_SGU_HEADS):
                dsw_ref[h] = jnp.where(keep, dsw_ref[h], 0.0)
            dsb_ref[...] = jnp.concatenate(
                [jnp.sum(dbias_ref[:, h * CHUNK:(h + 1) * CHUNK], axis=1, keepdims=True)
                 for h in range(N_SGU_HEADS)], axis=1)

    const2 = lambda i: (0, 0)
    const3 = lambda i: (0, 0, 0)
    out_shapes = (
        jax.ShapeDtypeStruct((s, 7 * WIDTH), BF16),
        jax.ShapeDtypeStruct((4, GROUP, GROUP), F32),
        jax.ShapeDtypeStruct((1, WIDTH), F32),
        jax.ShapeDtypeStruct((1, WIDTH), F32),
        jax.ShapeDtypeStruct((1, WIDTH), F32),
        jax.ShapeDtypeStruct((N_SGU_HEADS, CHUNK, CHUNK), F32),
        jax.ShapeDtypeStruct((CHUNK, N_SGU_HEADS), F32),
        jax.ShapeDtypeStruct((1, 3 * WIDTH), F32),
        jax.ShapeDtypeStruct((MEM_ROWS, WIDTH), F32),
        jax.ShapeDtypeStruct((MEM_ROWS, WIDTH), F32),
    )
    out_specs = (
        pl.BlockSpec((t, 7 * WIDTH), lambda i: (order(i), 0)),
        pl.BlockSpec((4, GROUP, GROUP), const3),
        pl.BlockSpec((1, WIDTH), const2),
        pl.BlockSpec((1, WIDTH), const2),
        pl.BlockSpec((1, WIDTH), const2),
        pl.BlockSpec((N_SGU_HEADS, CHUNK, CHUNK), const3),
        pl.BlockSpec((CHUNK, N_SGU_HEADS), const2),
        pl.BlockSpec((1, 3 * WIDTH), const2),
        pl.BlockSpec((MEM_ROWS, WIDTH), const2),
        pl.BlockSpec((MEM_ROWS, WIDTH), const2),
    )
    in_specs = _branch_specs(t, n_tiles, order) + [
        pl.BlockSpec((N_SGU_HEADS, CHUNK, CHUNK), const3),
        pl.BlockSpec((t, 3 * WIDTH), lambda i: (order(i), 0)),
    ]
    return pl.pallas_call(
        body, name="branches_bwd", grid=(n_tiles,),
        in_specs=in_specs, out_specs=out_specs, out_shape=out_shapes,
        scratch_shapes=[pltpu.VMEM((HALO, WIDTH), F32), pltpu.VMEM((CHUNK, WIDTH), F32)],
        compiler_params=_params(1),
    )(proj, proj, pool_w, pool_scale, ln_g, ln_b, sgu_w, bias_full, k, v, branch_norm, sgu_wt, dy)


def _out_loss(y, w_out, x, target, g_post, tm):
    s, d = x.shape
    e_w = y.shape[1]
    n_tiles = s // tm

    def body(y_ref, w_ref, x_ref, t_ref, g_ref, loss_ref, dz_ref, dout_ref, dy_ref, dg_ref, sq_ref):
        i = pl.program_id(0)

        @pl.when(i == 0)
        def _():
            sq_ref[...] = jnp.zeros(sq_ref.shape, F32)
            dg_ref[...] = jnp.zeros(dg_ref.shape, F32)

        w = w_ref[...]
        out = _dot(y_ref[...], w, NN)
        r = lax.rsqrt(jnp.mean(out * out, axis=-1, keepdims=True) + EPS)
        outn = out * r
        g = g_ref[...]
        err = (x_ref[...] + outn * g) - t_ref[...]
        sq_ref[...] += jnp.sum(err * err, axis=0, keepdims=True)
        dz = err * (1.0 / d)
        dz_ref[...] = dz
        dg_ref[...] += jnp.sum(dz * outn, axis=0, keepdims=True)
        doutn = dz * g
        dout = (r * (doutn - outn * jnp.mean(doutn * outn, axis=-1, keepdims=True))).astype(BF16)
        dout_ref[...] = dout
        dy_ref[...] = _dot(dout, w, NT).astype(BF16)

        @pl.when(i == n_tiles - 1)
        def _():
            loss_ref[...] = 0.5 * jnp.sum(sq_ref[...], axis=1, keepdims=True) * (1.0 / d)

    row = lambda i: (i, 0)
    const2 = lambda i: (0, 0)
    return pl.pallas_call(
        body, name="out_loss", grid=(n_tiles,),
        in_specs=[
            pl.BlockSpec((tm, e_w), row),
            pl.BlockSpec((e_w, d), const2, pipeline_mode=pl.Buffered(1)),
            pl.BlockSpec((tm, d), row),
            pl.BlockSpec((tm, d), row),
            pl.BlockSpec((1, d), const2),
        ],
        out_specs=(
            pl.BlockSpec((1, 1), const2),
            pl.BlockSpec((tm, d), row),
            pl.BlockSpec((tm, d), row),
            pl.BlockSpec((tm, e_w), row),
            pl.BlockSpec((1, d), const2),
        ),
        out_shape=(
            jax.ShapeDtypeStruct((1, 1), F32),
            jax.ShapeDtypeStruct((s, d), F32),
            jax.ShapeDtypeStruct((s, d), BF16),
            jax.ShapeDtypeStruct((s, e_w), BF16),
            jax.ShapeDtypeStruct((1, d), F32),
        ),
        scratch_shapes=[pltpu.VMEM((1, d), F32)],
        compiler_params=_params(1),
    )(y, w_out, x, target, g_post)


def _dx_call(dproj, w_in, x, dz, g_pre, tm, tk):
    s, d = x.shape
    k_total = dproj.shape[1]
    nk = k_total // tk
    n_tiles = s // tm

    def body(dp_ref, w_ref, x_ref, dz_ref, g_ref, dx_ref, dg_ref, acc_ref):
        i, kk = pl.program_id(0), pl.program_id(1)
        part = _dot(dp_ref[...], w_ref[...], NT)

        @pl.when(kk == 0)
        def _():
            acc_ref[...] = part

        @pl.when(kk > 0)
        def _():
            acc_ref[...] += part

        @pl.when((i == 0) & (kk == 0))
        def _():
            dg_ref[...] = jnp.zeros(dg_ref.shape, F32)

        @pl.when(kk == nk - 1)
        def _():
            dh = acc_ref[...]
            xv = x_ref[...]
            r = lax.rsqrt(jnp.mean(xv * xv, axis=-1, keepdims=True) + EPS)
            xhat = xv * r
            dg_ref[...] += jnp.sum(dh * xhat, axis=0, keepdims=True)
            dxhat = dh * g_ref[...]
            dx_ref[...] = dz_ref[...] + r * (dxhat - xhat * jnp.mean(dxhat * xhat, axis=-1, keepdims=True))

    row = lambda i, kk: (i, 0)
    const2 = lambda i, kk: (0, 0)
    return pl.pallas_call(
        body, name="dx", grid=(n_tiles, nk),
        in_specs=[
            pl.BlockSpec((tm, tk), lambda i, kk: (i, kk)),
            pl.BlockSpec((d, tk), lambda i, kk: (0, kk)),
            pl.BlockSpec((tm, d), row),
            pl.BlockSpec((tm, d), row),
            pl.BlockSpec((1, d), const2),
        ],
        out_specs=(pl.BlockSpec((tm, d), row), pl.BlockSpec((1, d), const2)),
        out_shape=(jax.ShapeDtypeStruct((s, d), F32), jax.ShapeDtypeStruct((1, d), F32)),
        scratch_shapes=[pltpu.VMEM((tm, d), F32)],
        compiler_params=_params(2),
    )(dproj, w_in, x, dz, g_pre)


def _rows_tile(rows, cols, n_arrays, itemsize=4):
    budget = 24 * 1024 * 1024 // (2 * n_arrays * cols * itemsize)
    best = None
    for cand in range(16, rows + 1, 16):
        if rows % cand == 0 and cand <= max(budget, 16):
            best = cand
    return best if best is not None else rows


def _elementwise(fn, inputs, out_dtypes, name):
    rows, cols = inputs[0].shape
    tr = _rows_tile(rows, cols, len(inputs) + len(out_dtypes))
    n_in = len(inputs)

    def body(*refs):
        outs = fn(*[r[...] for r in refs[:n_in]])
        for o_ref, o in zip(refs[n_in:], outs):
            o_ref[...] = o.astype(o_ref.dtype)

    spec = pl.BlockSpec((tr, cols), lambda i: (i, 0))
    return pl.pallas_call(
        body, name=name, grid=(rows // tr,),
        in_specs=[spec] * n_in, out_specs=tuple([spec] * len(out_dtypes)),
        out_shape=tuple(jax.ShapeDtypeStruct((rows, cols), dt) for dt in out_dtypes),
        compiler_params=_params(1),
    )(*inputs)


def _pair_sum(mine, theirs):
    return ((mine.astype(F32) + theirs.astype(F32)),)


def _four_sum(own, t0, t1, t2):
    return ((((own.astype(F32) + t0.astype(F32)) + t1.astype(F32)) + t2.astype(F32)),)


def _adamw(w, g, m, v):
    m = ADAM_B1 * m + (1.0 - ADAM_B1) * g
    v = ADAM_B2 * v + (1.0 - ADAM_B2) * jnp.square(g)
    m_hat = m / (1.0 - ADAM_B1 ** ADAM_STEP)
    v_hat = v / (1.0 - ADAM_B2 ** ADAM_STEP)
    delta = -ADAM_LR * (m_hat / (jnp.sqrt(v_hat) + ADAM_EPS) + ADAM_WD * w)
    return delta, m, v


def _place():
    x, y, c = lax.axis_index("x"), lax.axis_index("y"), lax.axis_index("c")
    chips = [(1 - x, y), (x, 1 - y), (1 - x, 1 - y)]
    return x, y, c, chips


def _remote(src, dst, send_sem, recv_sem, to):
    return pltpu.make_async_remote_copy(src_ref=src, dst_ref=dst, send_sem=send_sem, recv_sem=recv_sem,
                                        device_id=to, device_id_type=MESH)


def _split(ref, plan):
    views = [ref]
    for axis, parts in plan:
        size = ref.shape[axis] // parts
        assert size * parts == ref.shape[axis]
        views = [v.at[tuple(pl.ds(q * size, size) if i == axis else slice(None) for i in range(len(ref.shape)))]
                 for v in views for q in range(parts)]
    return views


def _remote_in_parts(src, dst, send_sem, recv_sem, to, plan):
    for s, d in zip(_split(src, plan), _split(dst, plan)):
        _remote(s, d, send_sem, recv_sem, to).start()
    return _remote(src, dst, send_sem, recv_sem, to)


def _local_in_parts(src, dst, sem, plan):
    for s, d in zip(_split(src, plan), _split(dst, plan)):
        pltpu.make_async_copy(s, d, sem).start()
    return pltpu.make_async_copy(src, dst, sem)


def _hbm_call(body, name, inputs, out_shapes, scratch):
    return pl.pallas_call(
        body, name=name,
        in_specs=[ANY] * len(inputs), out_specs=tuple([ANY] * len(out_shapes)), out_shape=tuple(out_shapes),
        scratch_shapes=scratch,
        compiler_params=pltpu.CompilerParams(has_side_effects=True),
    )(*inputs)


def _gather_weights(shards):
    n = len(shards)
    full_shapes = []
    for a, sh in enumerate(shards):
        if a == 0:
            full_shapes.append(jax.ShapeDtypeStruct((sh.shape[0], 4 * sh.shape[1]), sh.dtype))
        elif a == 3:
            full_shapes.append(jax.ShapeDtypeStruct((sh.shape[0], 4 * sh.shape[1], sh.shape[2]), sh.dtype))
        else:
            full_shapes.append(jax.ShapeDtypeStruct((4 * sh.shape[0], sh.shape[1]), sh.dtype))

    def shard_half(a, ref, cc):
        if a == 0:
            rows = ref.shape[0] // 2
            return ref.at[pl.ds(cc * rows, rows), :]
        if a == 3:
            rows = ref.shape[1] // 2
            return ref.at[:, pl.ds(cc * rows, rows), :]
        rows = ref.shape[0] // 2
        return ref.at[pl.ds(cc * rows, rows), :]

    def full_half(a, ref, chip, cc):
        if a == 0:
            rows, cols = ref.shape[0] // 2, ref.shape[1] // 4
            return ref.at[pl.ds(cc * rows, rows), pl.ds(pl.multiple_of(chip * cols, 128), cols)]
        if a == 3:
            rows = ref.shape[1] // 8
            return ref.at[:, pl.ds(pl.multiple_of((2 * chip + cc) * rows, 16), rows), :]
        rows = ref.shape[0] // 8
        return ref.at[pl.ds(pl.multiple_of((2 * chip + cc) * rows, 16), rows), :]

    def body(*refs):
        shard_refs, full_refs = refs[:n], refs[n:2 * n]
        send_sems, recv_sems, local_sems = refs[2 * n:]
        x, y, c, chips = _place()
        me = 2 * x + y
        sibling = (x, y, 1 - c)
        plans = [[(0, 8)], [(0, 2)], [(0, 2)], []]
        local = []
        for a in range(n):
            for cc in range(2):
                local.append(_local_in_parts(shard_half(a, shard_refs[a], cc), full_half(a, full_refs[a], me, cc),
                                             local_sems.at[2 * a + cc], plans[a]))
        sends = []
        for p, chip in enumerate(chips):
            for a in range(n):
                sends.append(_remote_in_parts(shard_half(a, shard_refs[a], c), full_half(a, full_refs[a], me, c),
                                              send_sems.at[6 * a + p], recv_sems.at[6 * a + p], (*chip, c), plans[a]))
        for p, chip in enumerate(chips):
            them = 2 * chip[0] + chip[1]
            for a in range(n):
                landed = full_half(a, full_refs[a], them, c)
                _remote(landed, landed, send_sems.at[6 * a + p], recv_sems.at[6 * a + p], (*chip, c)).wait_recv()
                sends.append(_remote_in_parts(landed, landed, send_sems.at[6 * a + 3 + p],
                                              recv_sems.at[6 * a + 3 + p], sibling, plans[a]))
        for p, chip in enumerate(chips):
            them = 2 * chip[0] + chip[1]
            for a in range(n):
                passed = full_half(a, full_refs[a], them, 1 - c)
                _remote(passed, passed, send_sems.at[6 * a + 3 + p], recv_sems.at[6 * a + 3 + p], sibling).wait_recv()
        for cp in sends:
            cp.wait_send()
        for cp in local:
            cp.wait()

    return _hbm_call(body, "gather_weights", shards, full_shapes,
                     [pltpu.SemaphoreType.DMA((6 * n,)), pltpu.SemaphoreType.DMA((6 * n,)),
                      pltpu.SemaphoreType.DMA((2 * n,))])


def _exchange_halves(grads):
    n = len(grads)
    arrays = [g for g, _, _ in grads]
    out_shapes = []
    for g, ax, _ in grads:
        shape = tuple(1 if i == ax else dim for i, dim in enumerate(g.shape))
        out_shapes += [jax.ShapeDtypeStruct(shape, g.dtype)] * 2

    def half(ref, ax, cc):
        idx = tuple(pl.ds(cc, 1) if i == ax else slice(None) for i in range(len(ref.shape)))
        return ref.at[idx]

    def body(*refs):
        in_refs, out_refs = refs[:n], refs[n:3 * n]
        send_sems, recv_sems, local_sems = refs[3 * n:]
        x, y, c, _ = _place()
        sibling = (x, y, 1 - c)
        copies = []
        for a in range(n):
            ax, plan = grads[a][1], grads[a][2]
            mine, theirs = out_refs[2 * a], out_refs[2 * a + 1]
            loc = _local_in_parts(half(in_refs[a], ax, c), mine, local_sems.at[a], plan)
            rem = _remote_in_parts(half(in_refs[a], ax, 1 - c), theirs, send_sems.at[a], recv_sems.at[a], sibling,
                                   plan)
            copies.append((loc, rem))
        for loc, rem in copies:
            rem.wait()
            loc.wait()

    outs = _hbm_call(body, "exchange_halves", arrays, out_shapes,
                     [pltpu.SemaphoreType.DMA((n,)), pltpu.SemaphoreType.DMA((n,)), pltpu.SemaphoreType.DMA((n,))])
    return [(outs[2 * a], outs[2 * a + 1]) for a in range(n)]


def _scatter_to_chips(parts):
    n = len(parts)
    arrays = [p for p, _, _ in parts]

    def block_shape(p, ax):
        if ax == len(p.shape) - 1:
            return p.shape[:-1] + (p.shape[-1] // 4,)
        return tuple(1 if i == ax else dim for i, dim in enumerate(p.shape))

    out_shapes = []
    for p, ax, _ in parts:
        shape = block_shape(p, ax)
        out_shapes += [jax.ShapeDtypeStruct(shape, p.dtype), jax.ShapeDtypeStruct((3,) + shape, p.dtype)]

    def block(ref, ax, chip):
        rank = len(ref.shape)
        if ax == rank - 1:
            cols = ref.shape[-1] // 4
            last = pl.ds(pl.multiple_of(chip * cols, 128), cols)
            return ref.at[tuple([slice(None)] * (rank - 1) + [last])]
        return ref.at[tuple(pl.ds(chip, 1) if i == ax else slice(None) for i in range(rank))]

    def body(*refs):
        in_refs, out_refs = refs[:n], refs[n:3 * n]
        send_sems, recv_sems, local_sems = refs[3 * n:]
        x, y, c, chips = _place()
        me = 2 * x + y
        copies = []
        for a in range(n):
            ax, plan = parts[a][1], parts[a][2]
            own, landed = out_refs[2 * a], out_refs[2 * a + 1]
            copies.append(_local_in_parts(block(in_refs[a], ax, me), own, local_sems.at[a], plan))
            for p, chip in enumerate(chips):
                them = 2 * chip[0] + chip[1]
                copies.append(_remote_in_parts(block(in_refs[a], ax, them), landed.at[p], send_sems.at[3 * a + p],
                                               recv_sems.at[3 * a + p], (*chip, c), plan))
        for cp in copies:
            cp.wait()

    outs = _hbm_call(body, "scatter_to_chips", arrays, out_shapes,
                     [pltpu.SemaphoreType.DMA((3 * n,)), pltpu.SemaphoreType.DMA((3 * n,)),
                      pltpu.SemaphoreType.DMA((n,))])
    return [(outs[2 * a], outs[2 * a + 1]) for a in range(n)]


def _join_halves(halves):
    n = len(halves)
    out_shapes = [jax.ShapeDtypeStruct((2,) + h.shape, h.dtype) for h in halves]

    def body(*refs):
        in_refs, out_refs = refs[:n], refs[n:2 * n]
        send_sems, recv_sems, local_sems = refs[2 * n:]
        x, y, c, _ = _place()
        sibling = (x, y, 1 - c)
        copies = []
        for a in range(n):
            plan = [(0, 8 if in_refs[a].shape[0] % 128 == 0 and in_refs[a].shape[0] >= 1024 else 2)]
            loc = _local_in_parts(in_refs[a], out_refs[a].at[c], local_sems.at[a], plan)
            rem = _remote_in_parts(in_refs[a], out_refs[a].at[c], send_sems.at[a], recv_sems.at[a], sibling, plan)
            copies.append((loc, rem))
        for a, (loc, rem) in enumerate(copies):
            rem.wait_send()
            arrived = out_refs[a].at[1 - c]
            _remote(arrived, arrived, send_sems.at[a], recv_sems.at[a], sibling).wait_recv()
            loc.wait()

    return _hbm_call(body, "join_halves", halves, out_shapes,
                     [pltpu.SemaphoreType.DMA((n,)), pltpu.SemaphoreType.DMA((n,)), pltpu.SemaphoreType.DMA((n,))])


def _allreduce_small(packed):
    rows, lanes = packed.shape

    def body(in_ref, out_ref, gath_ref, send_sems, recv_sems):
        x, y, c, _ = _place()
        me = 4 * x + 2 * y + c
        gath_ref[me] = in_ref[...]
        peers = [(x ^ (k >> 2), y ^ ((k >> 1) & 1), c ^ (k & 1)) for k in range(1, 8)]
        sends = [_remote(in_ref, gath_ref.at[me], send_sems.at[k], recv_sems.at[k], peer)
                 for k, peer in enumerate(peers)]
        for cp in sends:
            cp.start()
        for k, peer in enumerate(peers):
            slot = gath_ref.at[4 * peer[0] + 2 * peer[1] + peer[2]]
            _remote(slot, slot, send_sems.at[k], recv_sems.at[k], peer).wait_recv()
        for cp in sends:
            cp.wait_send()
        total = gath_ref[0]
        for dev in range(1, 8):
            total = total + gath_ref[dev]
        out_ref[...] = total

    vmem = pl.BlockSpec(memory_space=pltpu.VMEM)
    return pl.pallas_call(
        body, name="allreduce_small",
        in_specs=[vmem], out_specs=vmem, out_shape=jax.ShapeDtypeStruct((rows, lanes), F32),
        scratch_shapes=[pltpu.VMEM((8, rows, lanes), F32), pltpu.SemaphoreType.DMA((7,)),
                        pltpu.SemaphoreType.DMA((7,))],
        compiler_params=pltpu.CompilerParams(has_side_effects=True, vmem_limit_bytes=32 * 1024 * 1024),
    )(packed)


SMALL = ("norm_pre", "pool_scale", "sgu_ln_g", "sgu_ln_b", "sgu_w", "sgu_b", "mem_norm", "branch_norm", "norm_post")
LARGE = ("w_in", "pool_w", "w_kv", "w_out")
ORDER = ("norm_pre", "w_in", "pool_w", "pool_scale", "sgu_ln_g", "sgu_ln_b", "sgu_w", "sgu_b", "mem_norm", "w_kv",
         "branch_norm", "w_out", "norm_post")


def _pack(arrays):
    return jnp.concatenate([a.reshape(-1, 128) for a in arrays], axis=0)


def _unpack(packed, like):
    out, row = [], 0
    for a in like:
        rows = a.size // 128
        out.append(packed[row:row + rows].reshape(a.shape))
        row += rows
    return out


def kernel(x, mem, norm_pre, w_in, pool_w, pool_scale, sgu_ln_g, sgu_ln_b, sgu_w, sgu_b, mem_norm, w_kv, branch_norm, w_out, norm_post, loss_target, m_norm_pre, m_w_in, m_pool_w, m_pool_scale, m_sgu_ln_g, m_sgu_ln_b, m_sgu_w, m_sgu_b, m_mem_norm, m_w_kv, m_branch_norm, m_w_out, m_norm_post, v_norm_pre, v_w_in, v_pool_w, v_pool_scale, v_sgu_ln_g, v_sgu_ln_b, v_sgu_w, v_sgu_b, v_mem_norm, v_w_kv, v_branch_norm, v_w_out, v_norm_post):
    weights = dict(norm_pre=norm_pre, w_in=w_in, pool_w=pool_w, pool_scale=pool_scale, sgu_ln_g=sgu_ln_g,
                   sgu_ln_b=sgu_ln_b, sgu_w=sgu_w, sgu_b=sgu_b, mem_norm=mem_norm, w_kv=w_kv, branch_norm=branch_norm,
                   w_out=w_out, norm_post=norm_post)
    mom1 = dict(norm_pre=m_norm_pre, w_in=m_w_in, pool_w=m_pool_w, pool_scale=m_pool_scale, sgu_ln_g=m_sgu_ln_g,
                sgu_ln_b=m_sgu_ln_b, sgu_w=m_sgu_w, sgu_b=m_sgu_b, mem_norm=m_mem_norm, w_kv=m_w_kv,
                branch_norm=m_branch_norm, w_out=m_w_out, norm_post=m_norm_post)
    mom2 = dict(norm_pre=v_norm_pre, w_in=v_w_in, pool_w=v_pool_w, pool_scale=v_pool_scale, sgu_ln_g=v_sgu_ln_g,
                sgu_ln_b=v_sgu_ln_b, sgu_w=v_sgu_w, sgu_b=v_sgu_b, mem_norm=v_mem_norm, w_kv=v_w_kv,
                branch_norm=v_branch_norm, w_out=v_w_out, norm_post=v_norm_post)

    s, d = x.shape[1], x.shape[2]
    x2, mem2, tgt2 = x[0], mem[0], loss_target[0]
    t_branch = min(256, s)
    tm = min(512, s)

    wi_full, wkv_full, wo_full, pw_full = _gather_weights(
        [w_in[0].astype(BF16), w_kv[0].astype(BF16), w_out[0].astype(BF16), pool_w[0].astype(BF16)])

    mem_g = mem_norm.reshape(1, d)
    k_m, v_m = _kv_fwd(mem2, mem_g, wkv_full)
    h = _rms_pre(x2, norm_pre, tm)
    proj = _matmul(h, wi_full, NN, F32, min(1024, s), 512, d, "proj")
    bias_full = jnp.repeat(sgu_b[0].T, CHUNK, axis=1)
    y = _branches_fwd(proj, pw_full, pool_scale, sgu_ln_g, sgu_ln_b, sgu_w[0], bias_full, k_m, v_m, branch_norm,
                      t_branch)
    loss_local, dz, dout, dy, g_norm_post = _out_loss(y, wo_full, x2, tgt2, norm_post, min(256, s))

    g_wo = _matmul(y, dout, TN, BF16, 1536, 1024, min(512, s), "grad_w_out")
    (dproj, g_pw, g_pool_scale, g_ln_g, g_ln_b, g_sgu_w, g_sgu_b_t, g_branch_norm, dk, dv) = _branches_bwd(
        proj, dy, pw_full, pool_scale, sgu_ln_g, sgu_ln_b, sgu_w[0], jnp.swapaxes(sgu_w[0], 1, 2), bias_full,
        k_m, v_m, branch_norm, t_branch)
    g_wkv, g_mem_norm = _kv_bwd(mem2, mem_g, wkv_full, dk, dv)
    g_wi = _matmul(h, dproj, TN, BF16, d, 1024, min(512, s), "grad_w_in")
    grad_x, g_norm_pre = _dx_call(dproj, wi_full, x2, dz, norm_pre, tm, 1024)

    n_in, n_kv, n_out = wi_full.shape[1], wkv_full.shape[0], wo_full.shape[0]
    pairs = _exchange_halves([
        (g_wi.reshape(2, d // 2, n_in), 0, [(1, 16)]),
        (g_wkv.reshape(4, 2, n_kv // 8, wkv_full.shape[1]), 1, [(0, 4), (2, 2)]),
        (g_wo.reshape(4, 2, n_out // 8, d), 1, [(0, 4), (2, 2)]),
        (g_pw.astype(BF16).reshape(4, 4, 2, GROUP // 8, GROUP), 2, [(0, 4)]),
    ])
    flat = lambda a: a.reshape(-1, a.shape[-1])
    psum = [_elementwise(_pair_sum, [flat(mine), flat(theirs)], [BF16], "pair_sum_%d" % i)[0]
            for i, (mine, theirs) in enumerate(pairs)]
    scattered = _scatter_to_chips([
        (psum[0], 1, [(0, 4)]),
        (psum[1].reshape(4, n_kv // 8, wkv_full.shape[1]), 0, [(1, 2)]),
        (psum[2].reshape(4, n_out // 8, d), 0, [(1, 2)]),
        (psum[3].reshape(4, 4, GROUP // 8, GROUP), 1, []),
    ])
    halves = []
    for i, (own, landed) in enumerate(scattered):
        cols = own.shape[-1]
        halves.append(_elementwise(_four_sum, [own.reshape(-1, cols)] + [landed[p].reshape(-1, cols) for p in range(3)],
                                   [F32], "chip_sum_%d" % i)[0])
    joined = _join_halves(halves)
    grads = {
        "w_in": joined[0].reshape(w_in.shape),
        "w_kv": joined[1].reshape(w_kv.shape),
        "w_out": joined[2].reshape(w_out.shape),
        "pool_w": jnp.swapaxes(joined[3].reshape(2, 4, GROUP // 8, GROUP), 0, 1).reshape(pool_w.shape),
    }

    small_local = dict(norm_pre=g_norm_pre, pool_scale=g_pool_scale, sgu_ln_g=g_ln_g, sgu_ln_b=g_ln_b,
                       sgu_w=g_sgu_w, sgu_b=g_sgu_b_t.T, mem_norm=g_mem_norm, branch_norm=g_branch_norm,
                       norm_post=g_norm_post)
    small_sum = _allreduce_small(_pack([small_local[n] for n in SMALL]))
    for n, g in zip(SMALL, _unpack(small_sum, [weights[n] for n in SMALL])):
        grads[n] = g

    delta, new_m, new_v = {}, {}, {}
    packed = [_pack([src[n] for n in SMALL]) for src in (weights, grads, mom1, mom2)]
    outs = _elementwise(_adamw, packed, [F32, F32, F32], "adamw_small")
    for dst, o in zip((delta, new_m, new_v), outs):
        for n, a in zip(SMALL, _unpack(o, [weights[n] for n in SMALL])):
            dst[n] = a
    for n in LARGE:
        cols = weights[n].shape[-1]
        outs = _elementwise(_adamw, [src[n].reshape(-1, cols) for src in (weights, grads, mom1, mom2)],
                            [F32, F32, F32], "adamw_" + n)
        for dst, o in zip((delta, new_m, new_v), outs):
            dst[n] = o.reshape(weights[n].shape)

    loss = lax.psum(loss_local[0, 0], ("x", "y", "c"))
    return (loss, grad_x[None], *[grads[n] for n in ORDER], *[delta[n] for n in ORDER],
            *[new_m[n] for n in ORDER], *[new_v[n] for n in ORDER])
```

```python
import functools

import jax
import jax.numpy as jnp
from jax import lax
from jax.experimental import pallas as pl
from jax.experimental.pallas import tpu as pltpu

F32 = jnp.float32
BF16 = jnp.bfloat16
EPS = 1e-6
MESH = pl.DeviceIdType.MESH
ANY = pl.BlockSpec(memory_space=pl.ANY)

POOL_WINDOWS = (2, 4, 8, 16)
GROUP = 256
HALO = 16
CHUNK = 128
N_SGU_HEADS = 8
N_ATT_HEADS = 4
ATT_DIM = 256
WIDTH = 1024
ATT_SCALE = 1.0 / 16.0

ADAM_LR = 0.001
ADAM_B1 = 0.9
ADAM_B2 = 0.999
ADAM_EPS = 1e-08
ADAM_WD = 0.01
ADAM_STEP = 10

VMEM_LIMIT = 60 * 1024 * 1024


def _params(n_grid_axes, vmem=VMEM_LIMIT):
    return pltpu.CompilerParams(dimension_semantics=("arbitrary",) * n_grid_axes, vmem_limit_bytes=vmem)


def _dot(a, b, dims):
    return lax.dot_general(a, b, (dims, ((), ())), preferred_element_type=F32)


NN = ((1,), (0,))
NT = ((1,), (1,))
TN = ((0,), (0,))


def _matmul(a, b, dims, out_dtype, tm, tn, tk, name):
    if dims == NN:
        (m, k), n = a.shape, b.shape[1]
        a_spec = pl.BlockSpec((tm, tk), lambda i, j, kk: (i, kk))
        b_spec = pl.BlockSpec((tk, tn), lambda i, j, kk: (kk, j))
    else:
        (k, m), n = a.shape, b.shape[1]
        a_spec = pl.BlockSpec((tk, tm), lambda i, j, kk: (kk, i))
        b_spec = pl.BlockSpec((tk, tn), lambda i, j, kk: (kk, j))
    nk = k // tk

    def body(a_ref, b_ref, o_ref, acc_ref):
        kk = pl.program_id(2)
        part = _dot(a_ref[...], b_ref[...], dims)

        @pl.when(kk == 0)
        def _():
            acc_ref[...] = part

        @pl.when(kk > 0)
        def _():
            acc_ref[...] += part

        @pl.when(kk == nk - 1)
        def _():
            o_ref[...] = acc_ref[...].astype(out_dtype)

    return pl.pallas_call(
        body, name=name, grid=(m // tm, n // tn, nk),
        in_specs=[a_spec, b_spec],
        out_specs=pl.BlockSpec((tm, tn), lambda i, j, kk: (i, j)),
        out_shape=jax.ShapeDtypeStruct((m, n), out_dtype),
        scratch_shapes=[pltpu.VMEM((tm, tn), F32)],
        compiler_params=_params(3),
    )(a, b)


def _rms_pre(x, g, tm):
    s, d = x.shape

    def body(x_ref, g_ref, h_ref):
        xv = x_ref[...]
        r = lax.rsqrt(jnp.mean(xv * xv, axis=-1, keepdims=True) + EPS)
        h_ref[...] = (xv * r * g_ref[...]).astype(BF16)

    return pl.pallas_call(
        body, name="rms_pre", grid=(s // tm,),
        in_specs=[pl.BlockSpec((tm, d), lambda i: (i, 0)), pl.BlockSpec((1, d), lambda i: (0, 0))],
        out_specs=pl.BlockSpec((tm, d), lambda i: (i, 0)),
        out_shape=jax.ShapeDtypeStruct((s, d), BF16),
        compiler_params=_params(1),
    )(x, g)


def _kv_fwd(mem, g, w_kv):
    m, d = mem.shape

    def body(mem_ref, g_ref, w_ref, k_ref, v_ref):
        mv = mem_ref[...]
        r = lax.rsqrt(jnp.mean(mv * mv, axis=-1, keepdims=True) + EPS)
        mem_n = (mv * r * g_ref[...]).astype(BF16)
        kv = _dot(mem_n, w_ref[...], NN)
        k_ref[...] = kv[:, :WIDTH].astype(BF16)
        v_ref[...] = kv[:, WIDTH:].astype(BF16)

    return pl.pallas_call(
        body, name="kv_fwd",
        out_shape=(jax.ShapeDtypeStruct((m, WIDTH), BF16), jax.ShapeDtypeStruct((m, WIDTH), BF16)),
        compiler_params=_params(0),
    )(mem, g, w_kv)


def _kv_bwd(mem, g, w_kv, dk, dv):
    m, d = mem.shape
    n = w_kv.shape[1]
    col = 512

    def body(mem_ref, g_ref, w_ref, dk_ref, dv_ref, dw_ref, dg_ref):
        mv = mem_ref[...]
        r = lax.rsqrt(jnp.mean(mv * mv, axis=-1, keepdims=True) + EPS)
        mem_hat = mv * r
        mem_n = (mem_hat * g_ref[...]).astype(BF16)
        dkv = jnp.concatenate([dk_ref[...], dv_ref[...]], axis=1).astype(BF16)
        for j in range(n // col):
            dw_ref[:, j * col:(j + 1) * col] = _dot(mem_n, dkv[:, j * col:(j + 1) * col], TN).astype(BF16)
        dmem_n = _dot(dkv, w_ref[...], NT)
        dg_ref[...] = jnp.sum(dmem_n * mem_hat, axis=0, keepdims=True)

    return pl.pallas_call(
        body, name="kv_bwd",
        out_shape=(jax.ShapeDtypeStruct((d, n), BF16), jax.ShapeDtypeStruct((1, d), F32)),
        compiler_params=_params(0),
    )(mem, g, w_kv, dk, dv)


def _sigmoid(x):
    return 1.0 / (1.0 + jnp.exp(-x))


def _inv_counts(t0, t):
    pos = (t0 + lax.broadcasted_iota(jnp.int32, (t, 1), 0) + 1).astype(F32)
    return [1.0 / jnp.minimum(pos, float(w)) for w in POOL_WINDOWS]


def _window_sums(ext, t, backward):
    n = t + HALO
    parts = []
    for gi, w in enumerate(POOL_WINDOWS):
        s = ext[:, gi * GROUP:(gi + 1) * GROUP]
        k = 1
        while k < w:
            s = s + pltpu.roll(s, (n - k) if backward else k, axis=0)
            k *= 2
        parts.append(s[:t] if backward else s[HALO:])
    return parts


def _pool_fwd(xa, halo, inv, pool_w):
    t = xa.shape[0]
    sums = _window_sums(jnp.concatenate([halo, xa], axis=0), t, backward=False)
    d = jnp.concatenate([sums[gi] * inv[gi] - xa[:, gi * GROUP:(gi + 1) * GROUP] for gi in range(4)], axis=1)
    d = d.astype(BF16)
    y = jnp.concatenate([_dot(d[:, gi * GROUP:(gi + 1) * GROUP], pool_w[gi], NN) for gi in range(4)], axis=1)
    return d, y


def _layernorm_fwd(v):
    mu = jnp.mean(v, axis=-1, keepdims=True)
    xc = v - mu
    rstd = lax.rsqrt(jnp.mean(xc * xc, axis=-1, keepdims=True) + EPS)
    return xc * rstd, rstd


def _tril_mask(transposed):
    r = lax.broadcasted_iota(jnp.int32, (CHUNK, CHUNK), 0)
    c = lax.broadcasted_iota(jnp.int32, (CHUNK, CHUNK), 1)
    return (r <= c) if transposed else (r >= c)


def _sgu_mix(w_ref, vals, transposed):
    t = vals.shape[0]
    mask = _tril_mask(transposed)
    ws = [jnp.where(mask, w_ref[h], 0.0).astype(BF16) for h in range(N_SGU_HEADS)]
    rows = []
    for ci in range(t // CHUNK):
        blk = vals[ci * CHUNK:(ci + 1) * CHUNK]
        rows.append(jnp.concatenate(
            [_dot(ws[h], blk[:, h * CHUNK:(h + 1) * CHUNK], NN) for h in range(N_SGU_HEADS)], axis=1))
    return jnp.concatenate(rows, axis=0)


def _attn_fwd(q, k, v):
    ps, os_ = [], []
    for h in range(N_ATT_HEADS):
        sl = slice(h * ATT_DIM, (h + 1) * ATT_DIM)
        s = _dot(q[:, sl], k[:, sl], NT) * ATT_SCALE
        s = s - jnp.max(s, axis=-1, keepdims=True)
        e = jnp.exp(s)
        p = e / jnp.sum(e, axis=-1, keepdims=True)
        ps.append(p)
        os_.append(_dot(p.astype(BF16), v[:, sl], NN))
    return ps, jnp.concatenate(os_, axis=1)


def _rms_branch(y_pre):
    r = lax.rsqrt(jnp.mean(y_pre * y_pre, axis=-1, keepdims=True) + EPS)
    return y_pre * r, r


def _branch_specs(t, n_tiles, order):
    width_in = 7 * WIDTH
    tile = lambda i: order(i)
    per_halo = t // HALO
    const2 = lambda i: (0, 0)
    const3 = lambda i: (0, 0, 0)
    return [
        pl.BlockSpec((t, width_in), lambda i: (tile(i), 0)),
        pl.BlockSpec((HALO, WIDTH), lambda i: (jnp.maximum(tile(i) * per_halo - 1, 0), 0)),
        pl.BlockSpec((4, GROUP, GROUP), const3),
        pl.BlockSpec((1, WIDTH), const2),
        pl.BlockSpec((1, WIDTH), const2),
        pl.BlockSpec((1, WIDTH), const2),
        pl.BlockSpec((N_SGU_HEADS, CHUNK, CHUNK), const3),
        pl.BlockSpec((CHUNK, WIDTH), const2),
        pl.BlockSpec((MEM_ROWS, WIDTH), const2),
        pl.BlockSpec((MEM_ROWS, WIDTH), const2),
        pl.BlockSpec((1, 3 * WIDTH), const2),
    ]


MEM_ROWS = 256


def _branches_fwd(proj, pool_w, pool_scale, ln_g, ln_b, sgu_w, bias_full, k, v, branch_norm, t):
    s = proj.shape[0]
    n_tiles = s // t

    def body(proj_ref, halo_ref, pw_ref, ps_ref, lg_ref, lb_ref, sw_ref, sb_ref, k_ref, v_ref, bn_ref, y_ref):
        i = pl.program_id(0)
        col = lambda j: proj_ref[:, j * WIDTH:(j + 1) * WIDTH]
        bn = bn_ref[...]
        halo = jnp.where(i > 0, halo_ref[...], 0.0)
        _, y_pool = _pool_fwd(col(0), halo, _inv_counts(i * t, t), pw_ref[...])
        ga = col(1)
        ya = y_pool * ps_ref[...] * (ga * _sigmoid(ga))
        y_ref[:, 0:WIDTH] = (_rms_branch(ya)[0] * bn[:, 0:WIDTH]).astype(BF16)
        vhat, _ = _layernorm_fwd(col(3))
        vn = (vhat * lg_ref[...] + lb_ref[...]).astype(BF16)
        z = _sgu_mix(sw_ref, vn, transposed=False) + jnp.tile(sb_ref[...], (t // CHUNK, 1))
        gb = col(4)
        yb = col(2) * z * (gb * _sigmoid(gb))
        y_ref[:, WIDTH:2 * WIDTH] = (_rms_branch(yb)[0] * bn[:, WIDTH:2 * WIDTH]).astype(BF16)
        _, o = _attn_fwd(col(5).astype(BF16), k_ref[...], v_ref[...])
        gc = col(6)
        yc = o * (gc * _sigmoid(gc))
        y_ref[:, 2 * WIDTH:] = (_rms_branch(yc)[0] * bn[:, 2 * WIDTH:]).astype(BF16)

    return pl.pallas_call(
        body, name="branches_fwd", grid=(n_tiles,),
        in_specs=_branch_specs(t, n_tiles, lambda i: i),
        out_specs=pl.BlockSpec((t, 3 * WIDTH), lambda i: (i, 0)),
        out_shape=jax.ShapeDtypeStruct((s, 3 * WIDTH), BF16),
        compiler_params=_params(1),
    )(proj, proj, pool_w, pool_scale, ln_g, ln_b, sgu_w, bias_full, k, v, branch_norm)


def _branches_bwd(proj, dy, pool_w, pool_scale, ln_g, ln_b, sgu_w, sgu_wt, bias_full, k, v, branch_norm, t):
    s = proj.shape[0]
    n_tiles = s // t
    n_chunks = t // CHUNK
    order = lambda i: n_tiles - 1 - i

    def body(proj_ref, halo_ref, pw_ref, ps_ref, lg_ref, lb_ref, sw_ref, sb_ref, k_ref, v_ref, bn_ref,
             swt_ref, dy_ref,
             dproj_ref, dpw_ref, dps_ref, dlg_ref, dlb_ref, dsw_ref, dsb_ref, dbn_ref, dk_ref, dv_ref,
             carry_ref, dbias_ref):
        step = pl.program_id(0)
        i = order(step)

        @pl.when(step == 0)
        def _():
            for ref in (dpw_ref, dps_ref, dlg_ref, dlb_ref, dsw_ref, dbn_ref, dk_ref, dv_ref, carry_ref, dbias_ref):
                ref[...] = jnp.zeros(ref.shape, ref.dtype)

        col = lambda j: proj_ref[:, j * WIDTH:(j + 1) * WIDTH]
        bn = bn_ref[...]

        def norm_bwd(y_pre, sl):
            yhat, r = _rms_branch(y_pre)
            dyv = dy_ref[:, sl].astype(F32)
            dbn_ref[:, sl] += jnp.sum(dyv * yhat, axis=0, keepdims=True)
            dyhat = dyv * bn[:, sl]
            return r * (dyhat - yhat * jnp.mean(dyhat * yhat, axis=-1, keepdims=True))

        def gate(gv):
            sg = _sigmoid(gv)
            return gv * sg, sg * (1.0 + gv * (1.0 - sg))

        inv = _inv_counts(i * t, t)
        halo = jnp.where(i > 0, halo_ref[...], 0.0)
        pw = pw_ref[...]
        d, y_pool = _pool_fwd(col(0), halo, inv, pw)
        scale = ps_ref[...]
        silu_a, dsilu_a = gate(col(1))
        pa = y_pool * scale
        dya = norm_bwd(pa * silu_a, slice(0, WIDTH))
        dproj_ref[:, WIDTH:2 * WIDTH] = (dya * pa * dsilu_a).astype(BF16)
        dpa = dya * silu_a
        dps_ref[...] += jnp.sum(dpa * y_pool, axis=0, keepdims=True)
        dy_pool = (dpa * scale).astype(BF16)
        dd_parts, ddc_parts = [], []
        for gi in range(4):
            sl = slice(gi * GROUP, (gi + 1) * GROUP)
            dpw_ref[gi] += _dot(d[:, sl], dy_pool[:, sl], TN)
            dd = _dot(dy_pool[:, sl], pw[gi], NT)
            dd_parts.append(dd)
            ddc_parts.append(dd * inv[gi])
        ddc = jnp.concatenate(ddc_parts, axis=1)
        sums = _window_sums(jnp.concatenate([ddc, carry_ref[...]], axis=0), t, backward=True)
        carry_ref[...] = ddc[:HALO]
        dproj_ref[:, 0:WIDTH] = jnp.concatenate([sums[gi] - dd_parts[gi] for gi in range(4)], axis=1).astype(BF16)

        vhat, rstd = _layernorm_fwd(col(3))
        lg = lg_ref[...]
        vn = (vhat * lg + lb_ref[...]).astype(BF16)
        z = _sgu_mix(sw_ref, vn, transposed=False) + jnp.tile(sb_ref[...], (n_chunks, 1))
        u = col(2)
        silu_b, dsilu_b = gate(col(4))
        uz = u * z
        dyb = norm_bwd(uz * silu_b, slice(WIDTH, 2 * WIDTH))
        dproj_ref[:, 4 * WIDTH:5 * WIDTH] = (dyb * uz * dsilu_b).astype(BF16)
        duz = dyb * silu_b
        dproj_ref[:, 2 * WIDTH:3 * WIDTH] = (duz * z).astype(BF16)
        dz = duz * u
        dz_b = dz.astype(BF16)
        for ci in range(n_chunks):
            rows = slice(ci * CHUNK, (ci + 1) * CHUNK)
            dbias_ref[...] += dz[rows]
            for h in range(N_SGU_HEADS):
                sl = slice(h * CHUNK, (h + 1) * CHUNK)
                dsw_ref[h] += _dot(dz_b[rows, sl], vn[rows, sl], NT)
        dvn = _sgu_mix(swt_ref, dz_b, transposed=True)
        dlg_ref[...] += jnp.sum(dvn * vhat, axis=0, keepdims=True)
        dlb_ref[...] += jnp.sum(dvn, axis=0, keepdims=True)
        dvhat = dvn * lg
        dvb = rstd * (dvhat - jnp.mean(dvhat, axis=-1, keepdims=True)
                      - vhat * jnp.mean(dvhat * vhat, axis=-1, keepdims=True))
        dproj_ref[:, 3 * WIDTH:4 * WIDTH] = dvb.astype(BF16)

        q = col(5).astype(BF16)
        kv_k, kv_v = k_ref[...], v_ref[...]
        ps, o = _attn_fwd(q, kv_k, kv_v)
        silu_c, dsilu_c = gate(col(6))
        dyc = norm_bwd(o * silu_c, slice(2 * WIDTH, 3 * WIDTH))
        dproj_ref[:, 6 * WIDTH:7 * WIDTH] = (dyc * o * dsilu_c).astype(BF16)
        do = (dyc * silu_c).astype(BF16)
        dq_parts = []
        for h in range(N_ATT_HEADS):
            sl = slice(h * ATT_DIM, (h + 1) * ATT_DIM)
            p = ps[h]
            dp = _dot(do[:, sl], kv_v[:, sl], NT)
            ds = (p * (dp - jnp.sum(p * dp, axis=-1, keepdims=True)) * ATT_SCALE).astype(BF16)
            dq_parts.append(_dot(ds, kv_k[:, sl], NN))
            dk_ref[:, sl] += _dot(ds, q[:, sl], TN)
            dv_ref[:, sl] += _dot(p.astype(BF16), do[:, sl], TN)
        dproj_ref[:, 5 * WIDTH:6 * WIDTH] = jnp.concatenate(dq_parts, axis=1).astype(BF16)

        @pl.when(step == n_tiles - 1)
        def _():
            keep = _tril_mask(transposed=False)
            for h in range(N_SGU_HEADS):
                dsw_ref[h] = jnp.where(keep, dsw_ref[h], 0.0)
            dsb_ref[...] = jnp.concatenate(
                [jnp.sum(dbias_ref[:, h * CHUNK:(h + 1) * CHUNK], axis=1, keepdims=True)
                 for h in range(N_SGU_HEADS)], axis=1)

    const2 = lambda i: (0, 0)
    const3 = lambda i: (0, 0, 0)
    out_shapes = (
        jax.ShapeDtypeStruct((s, 7 * WIDTH), BF16),
        jax.ShapeDtypeStruct((4, GROUP, GROUP), F32),
        jax.ShapeDtypeStruct((1, WIDTH), F32),
        jax.ShapeDtypeStruct((1, WIDTH), F32),
        jax.ShapeDtypeStruct((1, WIDTH), F32),
        jax.ShapeDtypeStruct((N_SGU_HEADS, CHUNK, CHUNK), F32),
        jax.ShapeDtypeStruct((CHUNK, N_SGU_HEADS), F32),
        jax.ShapeDtypeStruct((1, 3 * WIDTH), F32),
        jax.ShapeDtypeStruct((MEM_ROWS, WIDTH), F32),
        jax.ShapeDtypeStruct((MEM_ROWS, WIDTH), F32),
    )
    out_specs = (
        pl.BlockSpec((t, 7 * WIDTH), lambda i: (order(i), 0)),
        pl.BlockSpec((4, GROUP, GROUP), const3),
        pl.BlockSpec((1, WIDTH), const2),
        pl.BlockSpec((1, WIDTH), const2),
        pl.BlockSpec((1, WIDTH), const2),
        pl.BlockSpec((N_SGU_HEADS, CHUNK, CHUNK), const3),
        pl.BlockSpec((CHUNK, N_SGU_HEADS), const2),
        pl.BlockSpec((1, 3 * WIDTH), const2),
        pl.BlockSpec((MEM_ROWS, WIDTH), const2),
        pl.BlockSpec((MEM_ROWS, WIDTH), const2),
    )
    in_specs = _branch_specs(t, n_tiles, order) + [
        pl.BlockSpec((N_SGU_HEADS, CHUNK, CHUNK), const3),
        pl.BlockSpec((t, 3 * WIDTH), lambda i: (order(i), 0)),
    ]
    return pl.pallas_call(
        body, name="branches_bwd", grid=(n_tiles,),
        in_specs=in_specs, out_specs=out_specs, out_shape=out_shapes,
        scratch_shapes=[pltpu.VMEM((HALO, WIDTH), F32), pltpu.VMEM((CHUNK, WIDTH), F32)],
        compiler_params=_params(1),
    )(proj, proj, pool_w, pool_scale, ln_g, ln_b, sgu_w, bias_full, k, v, branch_norm, sgu_wt, dy)


def _out_loss(y, w_out, x, target, g_post, tm):
    s, d = x.shape
    e_w = y.shape[1]
    n_tiles = s // tm

    def body(y_ref, w_ref, x_ref, t_ref, g_ref, loss_ref, dz_ref, dout_ref, dy_ref, dg_ref, sq_ref):
        i = pl.program_id(0)

        @pl.when(i == 0)
        def _():
            sq_ref[...] = jnp.zeros(sq_ref.shape, F32)
            dg_ref[...] = jnp.zeros(dg_ref.shape, F32)

        w = w_ref[...]
        out = _dot(y_ref[...], w, NN)
        r = lax.rsqrt(jnp.mean(out * out, axis=-1, keepdims=True) + EPS)
        outn = out * r
        g = g_ref[...]
        err = (x_ref[...] + outn * g) - t_ref[...]
        sq_ref[...] += jnp.sum(err * err, axis=0, keepdims=True)
        dz = err * (1.0 / d)
        dz_ref[...] = dz
        dg_ref[...] += jnp.sum(dz * outn, axis=0, keepdims=True)
        doutn = dz * g
        dout = (r * (doutn - outn * jnp.mean(doutn * outn, axis=-1, keepdims=True))).astype(BF16)
        dout_ref[...] = dout
        dy_ref[...] = _dot(dout, w, NT).astype(BF16)

        @pl.when(i == n_tiles - 1)
        def _():
            loss_ref[...] = 0.5 * jnp.sum(sq_ref[...], axis=1, keepdims=True) * (1.0 / d)

    row = lambda i: (i, 0)
    const2 = lambda i: (0, 0)
    return pl.pallas_call(
        body, name="out_loss", grid=(n_tiles,),
        in_specs=[
            pl.BlockSpec((tm, e_w), row),
            pl.BlockSpec((e_w, d), const2, pipeline_mode=pl.Buffered(1)),
            pl.BlockSpec((tm, d), row),
            pl.BlockSpec((tm, d), row),
            pl.BlockSpec((1, d), const2),
        ],
        out_specs=(
            pl.BlockSpec((1, 1), const2),
            pl.BlockSpec((tm, d), row),
            pl.BlockSpec((tm, d), row),
            pl.BlockSpec((tm, e_w), row),
            pl.BlockSpec((1, d), const2),
        ),
        out_shape=(
            jax.ShapeDtypeStruct((1, 1), F32),
            jax.ShapeDtypeStruct((s, d), F32),
            jax.ShapeDtypeStruct((s, d), BF16),
            jax.ShapeDtypeStruct((s, e_w), BF16),
            jax.ShapeDtypeStruct((1, d), F32),
        ),
        scratch_shapes=[pltpu.VMEM((1, d), F32)],
        compiler_params=_params(1),
    )(y, w_out, x, target, g_post)


def _dx_call(dproj, w_in, x, dz, g_pre, tm, tk):
    s, d = x.shape
    k_total = dproj.shape[1]
    nk = k_total // tk
    n_tiles = s // tm

    def body(dp_ref, w_ref, x_ref, dz_ref, g_ref, dx_ref, dg_ref, acc_ref):
        i, kk = pl.program_id(0), pl.program_id(1)
        part = _dot(dp_ref[...], w_ref[...], NT)

        @pl.when(kk == 0)
        def _():
            acc_ref[...] = part

        @pl.when(kk > 0)
        def _():
            acc_ref[...] += part

        @pl.when((i == 0) & (kk == 0))
        def _():
            dg_ref[...] = jnp.zeros(dg_ref.shape, F32)

        @pl.when(kk == nk - 1)
        def _():
            dh = acc_ref[...]
            xv = x_ref[...]
            r = lax.rsqrt(jnp.mean(xv * xv, axis=-1, keepdims=True) + EPS)
            xhat = xv * r
            dg_ref[...] += jnp.sum(dh * xhat, axis=0, keepdims=True)
            dxhat = dh * g_ref[...]
            dx_ref[...] = dz_ref[...] + r * (dxhat - xhat * jnp.mean(dxhat * xhat, axis=-1, keepdims=True))

    row = lambda i, kk: (i, 0)
    const2 = lambda i, kk: (0, 0)
    return pl.pallas_call(
        body, name="dx", grid=(n_tiles, nk),
        in_specs=[
            pl.BlockSpec((tm, tk), lambda i, kk: (i, kk)),
            pl.BlockSpec((d, tk), lambda i, kk: (0, kk)),
            pl.BlockSpec((tm, d), row),
            pl.BlockSpec((tm, d), row),
            pl.BlockSpec((1, d), const2),
        ],
        out_specs=(pl.BlockSpec((tm, d), row), pl.BlockSpec((1, d), const2)),
        out_shape=(jax.ShapeDtypeStruct((s, d), F32), jax.ShapeDtypeStruct((1, d), F32)),
        scratch_shapes=[pltpu.VMEM((tm, d), F32)],
        compiler_params=_params(2),
    )(dproj, w_in, x, dz, g_pre)


def _rows_tile(rows, cols, n_arrays, itemsize=4):
    budget = 24 * 1024 * 1024 // (2 * n_arrays * cols * itemsize)
    if rows <= budget:
        return rows
    best = None
    for cand in range(16, rows + 1, 16):
        if rows % cand == 0 and cand <= max(budget, 16):
            best = cand
    return best if best is not None else rows


def _elementwise(fn, inputs, out_dtypes, name):
    rows, cols = inputs[0].shape
    tr = _rows_tile(rows, cols, len(inputs) + len(out_dtypes))
    n_in = len(inputs)

    def body(*refs):
        outs = fn(*[r[...] for r in refs[:n_in]])
        for o_ref, o in zip(refs[n_in:], outs):
            o_ref[...] = o.astype(o_ref.dtype)

    spec = pl.BlockSpec((tr, cols), lambda i: (i, 0))
    return pl.pallas_call(
        body, name=name, grid=(rows // tr,),
        in_specs=[spec] * n_in, out_specs=tuple([spec] * len(out_dtypes)),
        out_shape=tuple(jax.ShapeDtypeStruct((rows, cols), dt) for dt in out_dtypes),
        compiler_params=_params(1),
    )(*inputs)


def _blockwise(fn, pos, inputs, in_specs, out_shape, out_spec, grid, name):
    n_in = len(inputs)

    def body(pos_ref, *refs):
        o_ref = refs[n_in]
        (out,) = fn(*[r[...].reshape(o_ref.shape) for r in refs[:n_in]])
        o_ref[...] = out.astype(o_ref.dtype)

    return pl.pallas_call(
        body, name=name,
        grid_spec=pltpu.PrefetchScalarGridSpec(num_scalar_prefetch=1, grid=grid, in_specs=in_specs,
                                               out_specs=out_spec),
        out_shape=out_shape,
        compiler_params=_params(len(grid)),
    )(pos, *inputs)


def _cast_copy(x):
    return (x,)


def _pair_sum(mine, theirs):
    return ((mine.astype(F32) + theirs.astype(F32)),)


def _four_sum(own, t0, t1, t2):
    return ((((own.astype(F32) + t0.astype(F32)) + t1.astype(F32)) + t2.astype(F32)),)


def _adamw(w, g, m, v):
    m = ADAM_B1 * m + (1.0 - ADAM_B1) * g
    v = ADAM_B2 * v + (1.0 - ADAM_B2) * jnp.square(g)
    m_hat = m / (1.0 - ADAM_B1 ** ADAM_STEP)
    v_hat = v / (1.0 - ADAM_B2 ** ADAM_STEP)
    delta = -ADAM_LR * (m_hat / (jnp.sqrt(v_hat) + ADAM_EPS) + ADAM_WD * w)
    return delta, m, v


def _place():
    x, y, c = lax.axis_index("x"), lax.axis_index("y"), lax.axis_index("c")
    chips = [(1 - x, y), (x, 1 - y), (1 - x, 1 - y)]
    return x, y, c, chips


def _remote(src, dst, send_sem, recv_sem, to):
    return pltpu.make_async_remote_copy(src_ref=src, dst_ref=dst, send_sem=send_sem, recv_sem=recv_sem,
                                        device_id=to, device_id_type=MESH)


def _split(ref, plan):
    views = [ref]
    for axis, parts in plan:
        size = ref.shape[axis] // parts
        assert size * parts == ref.shape[axis]
        views = [v.at[tuple(pl.ds(q * size, size) if i == axis else slice(None) for i in range(len(ref.shape)))]
                 for v in views for q in range(parts)]
    return views


def _remote_in_parts(src, dst, send_sem, recv_sem, to, plan):
    for s, d in zip(_split(src, plan), _split(dst, plan)):
        _remote(s, d, send_sem, recv_sem, to).start()
    return _remote(src, dst, send_sem, recv_sem, to)


def _local_in_parts(src, dst, sem, plan):
    for s, d in zip(_split(src, plan), _split(dst, plan)):
        pltpu.make_async_copy(s, d, sem).start()
    return pltpu.make_async_copy(src, dst, sem)


def _hbm_call(body, name, inputs, out_shapes, scratch, aliases=None):
    return pl.pallas_call(
        body, name=name,
        in_specs=[ANY] * len(inputs), out_specs=tuple([ANY] * len(out_shapes)), out_shape=tuple(out_shapes),
        scratch_shapes=scratch, input_output_aliases=aliases or {},
        compiler_params=pltpu.CompilerParams(has_side_effects=True),
    )(*inputs)


def _gather_weights(fulls):
    n = len(fulls)

    def full_half(a, ref, chip, cc):
        if a == 0:
            rows, cols = ref.shape[0] // 2, ref.shape[1] // 4
            return ref.at[pl.ds(cc * rows, rows), pl.ds(pl.multiple_of(chip * cols, 128), cols)]
        if a == 3:
            rows = ref.shape[1] // 8
            return ref.at[:, pl.ds(pl.multiple_of((2 * chip + cc) * rows, 16), rows), :]
        rows = ref.shape[0] // 8
        return ref.at[pl.ds(pl.multiple_of((2 * chip + cc) * rows, 16), rows), :]

    def body(*refs):
        full_refs = refs[n:2 * n]
        send_sems, recv_sems = refs[2 * n:]
        x, y, c, chips = _place()
        me = 2 * x + y
        sibling = (x, y, 1 - c)
        plans = [[(0, 8)], [(0, 2)], [(0, 2)], []]
        sends = []
        for p, chip in enumerate(chips):
            for a in range(n):
                mine = full_half(a, full_refs[a], me, c)
                sends.append(_remote_in_parts(mine, mine, send_sems.at[6 * a + p], recv_sems.at[6 * a + p],
                                              (*chip, c), plans[a]))
        for p, chip in enumerate(chips):
            them = 2 * chip[0] + chip[1]
            for a in range(n):
                landed = full_half(a, full_refs[a], them, c)
                _remote(landed, landed, send_sems.at[6 * a + p], recv_sems.at[6 * a + p], (*chip, c)).wait_recv()
                sends.append(_remote_in_parts(landed, landed, send_sems.at[6 * a + 3 + p],
                                              recv_sems.at[6 * a + 3 + p], sibling, plans[a]))
        for p, chip in enumerate(chips):
            them = 2 * chip[0] + chip[1]
            for a in range(n):
                passed = full_half(a, full_refs[a], them, 1 - c)
                _remote(passed, passed, send_sems.at[6 * a + 3 + p], recv_sems.at[6 * a + 3 + p], sibling).wait_recv()
        for cp in sends:
            cp.wait_send()

    return _hbm_call(body, "gather_weights", fulls, [jax.ShapeDtypeStruct(f.shape, f.dtype) for f in fulls],
                     [pltpu.SemaphoreType.DMA((6 * n,)), pltpu.SemaphoreType.DMA((6 * n,))],
                     aliases={a: a for a in range(n)})


def _exchange_halves(grads):
    n = len(grads)
    arrays = [g for g, _, _ in grads]
    out_shapes = [jax.ShapeDtypeStruct(tuple(1 if i == ax else dim for i, dim in enumerate(g.shape)), g.dtype)
                  for g, ax, _ in grads]

    def half(ref, ax, cc):
        idx = tuple(pl.ds(cc, 1) if i == ax else slice(None) for i in range(len(ref.shape)))
        return ref.at[idx]

    def body(*refs):
        in_refs, out_refs = refs[:n], refs[n:2 * n]
        send_sems, recv_sems = refs[2 * n:]
        x, y, c, _ = _place()
        sibling = (x, y, 1 - c)
        copies = [_remote_in_parts(half(in_refs[a], grads[a][1], 1 - c), out_refs[a], send_sems.at[a],
                                   recv_sems.at[a], sibling, grads[a][2]) for a in range(n)]
        for rem in copies:
            rem.wait()

    return _hbm_call(body, "exchange_halves", arrays, out_shapes,
                     [pltpu.SemaphoreType.DMA((n,)), pltpu.SemaphoreType.DMA((n,))])


def _scatter_to_chips(parts):
    n = len(parts)
    arrays = [p for p, _, _ in parts]

    def block_shape(p, ax):
        if ax == len(p.shape) - 1:
            return p.shape[:-1] + (p.shape[-1] // 4,)
        return tuple(1 if i == ax else dim for i, dim in enumerate(p.shape))

    out_shapes = [jax.ShapeDtypeStruct((3,) + block_shape(p, ax), p.dtype) for p, ax, _ in parts]

    def block(ref, ax, chip):
        rank = len(ref.shape)
        if ax == rank - 1:
            cols = ref.shape[-1] // 4
            last = pl.ds(pl.multiple_of(chip * cols, 128), cols)
            return ref.at[tuple([slice(None)] * (rank - 1) + [last])]
        return ref.at[tuple(pl.ds(chip, 1) if i == ax else slice(None) for i in range(rank))]

    def body(*refs):
        in_refs, out_refs = refs[:n], refs[n:2 * n]
        send_sems, recv_sems = refs[2 * n:]
        x, y, c, chips = _place()
        copies = []
        for a in range(n):
            ax, plan = parts[a][1], parts[a][2]
            for p, chip in enumerate(chips):
                them = 2 * chip[0] + chip[1]
                copies.append(_remote_in_parts(block(in_refs[a], ax, them), out_refs[a].at[p],
                                               send_sems.at[3 * a + p], recv_sems.at[3 * a + p], (*chip, c), plan))
        for cp in copies:
            cp.wait()

    return _hbm_call(body, "scatter_to_chips", arrays, out_shapes,
                     [pltpu.SemaphoreType.DMA((3 * n,)), pltpu.SemaphoreType.DMA((3 * n,))])


def _join_halves(joined):
    n = len(joined)
    arrays = [j for j, _, _ in joined]

    def body(*refs):
        out_refs = refs[n:2 * n]
        send_sems, recv_sems = refs[2 * n:]
        x, y, c, _ = _place()
        sibling = (x, y, 1 - c)

        def half(a, cc):
            rank = len(out_refs[a].shape)
            return out_refs[a].at[tuple(pl.ds(cc, 1) if i == joined[a][1] else slice(None) for i in range(rank))]

        sends = [_remote_in_parts(half(a, c), half(a, c), send_sems.at[a], recv_sems.at[a], sibling, joined[a][2])
                 for a in range(n)]
        for a, rem in enumerate(sends):
            rem.wait_send()
            _remote(half(a, 1 - c), half(a, 1 - c), send_sems.at[a], recv_sems.at[a], sibling).wait_recv()

    return _hbm_call(body, "join_halves", arrays, [jax.ShapeDtypeStruct(j.shape, j.dtype) for j in arrays],
                     [pltpu.SemaphoreType.DMA((n,)), pltpu.SemaphoreType.DMA((n,))],
                     aliases={a: a for a in range(n)})


def _allreduce_small(packed):
    rows, lanes = packed.shape
    half = rows // 2

    def body(in_ref, out_ref, pair_ref, gath_ref, send_sems, recv_sems):
        x, y, c, chips = _place()
        me = 2 * x + y
        sibling = (x, y, 1 - c)
        mine = pl.ds(pl.multiple_of(c * half, 8), half)
        theirs = pl.ds(pl.multiple_of((1 - c) * half, 8), half)
        to_sib = _remote(in_ref.at[theirs], pair_ref, send_sems.at[0], recv_sems.at[0], sibling)
        to_sib.start()
        to_sib.wait()
        gath_ref[me] = in_ref[mine] + pair_ref[...]
        sends = [_remote(gath_ref.at[me], gath_ref.at[me], send_sems.at[1 + p], recv_sems.at[1 + p], (*chip, c))
                 for p, chip in enumerate(chips)]
        for cp in sends:
            cp.start()
        for p, chip in enumerate(chips):
            slot = gath_ref.at[2 * chip[0] + chip[1]]
            _remote(slot, slot, send_sems.at[1 + p], recv_sems.at[1 + p], (*chip, c)).wait_recv()
        for cp in sends:
            cp.wait_send()
        out_ref[mine] = ((gath_ref[0] + gath_ref[1]) + gath_ref[2]) + gath_ref[3]
        back = _remote(out_ref.at[mine], out_ref.at[mine], send_sems.at[4], recv_sems.at[4], sibling)
        back.start()
        back.wait_send()
        _remote(out_ref.at[theirs], out_ref.at[theirs], send_sems.at[4], recv_sems.at[4], sibling).wait_recv()

    vmem = pl.BlockSpec(memory_space=pltpu.VMEM)
    return pl.pallas_call(
        body, name="allreduce_small",
        in_specs=[vmem], out_specs=vmem, out_shape=jax.ShapeDtypeStruct((rows, lanes), F32),
        scratch_shapes=[pltpu.VMEM((half, lanes), F32), pltpu.VMEM((4, half, lanes), F32),
                        pltpu.SemaphoreType.DMA((5,)), pltpu.SemaphoreType.DMA((5,))],
        compiler_params=pltpu.CompilerParams(has_side_effects=True, vmem_limit_bytes=32 * 1024 * 1024),
    )(packed)


SMALL = ("norm_pre", "pool_scale", "sgu_ln_g", "sgu_ln_b", "sgu_w", "sgu_b", "mem_norm", "branch_norm", "norm_post")
LARGE = ("w_in", "pool_w", "w_kv", "w_out")
ORDER = ("norm_pre", "w_in", "pool_w", "pool_scale", "sgu_ln_g", "sgu_ln_b", "sgu_w", "sgu_b", "mem_norm", "w_kv",
         "branch_norm", "w_out", "norm_post")


def _pack(arrays):
    rows = [a.reshape(-1, 128) for a in arrays]
    pad = -sum(r.shape[0] for r in rows) % 16
    return jnp.concatenate(rows + ([jnp.zeros((pad, 128), F32)] if pad else []), axis=0)


def _unpack(packed, like):
    out, row = [], 0
    for a in like:
        rows = a.size // 128
        out.append(packed[row:row + rows].reshape(a.shape))
        row += rows
    return out


def kernel(x, mem, norm_pre, w_in, pool_w, pool_scale, sgu_ln_g, sgu_ln_b, sgu_w, sgu_b, mem_norm, w_kv, branch_norm, w_out, norm_post, loss_target, m_norm_pre, m_w_in, m_pool_w, m_pool_scale, m_sgu_ln_g, m_sgu_ln_b, m_sgu_w, m_sgu_b, m_mem_norm, m_w_kv, m_branch_norm, m_w_out, m_norm_post, v_norm_pre, v_w_in, v_pool_w, v_pool_scale, v_sgu_ln_g, v_sgu_ln_b, v_sgu_w, v_sgu_b, v_mem_norm, v_w_kv, v_branch_norm, v_w_out, v_norm_post):
    weights = dict(norm_pre=norm_pre, w_in=w_in, pool_w=pool_w, pool_scale=pool_scale, sgu_ln_g=sgu_ln_g,
                   sgu_ln_b=sgu_ln_b, sgu_w=sgu_w, sgu_b=sgu_b, mem_norm=mem_norm, w_kv=w_kv, branch_norm=branch_norm,
                   w_out=w_out, norm_post=norm_post)
    mom1 = dict(norm_pre=m_norm_pre, w_in=m_w_in, pool_w=m_pool_w, pool_scale=m_pool_scale, sgu_ln_g=m_sgu_ln_g,
                sgu_ln_b=m_sgu_ln_b, sgu_w=m_sgu_w, sgu_b=m_sgu_b, mem_norm=m_mem_norm, w_kv=m_w_kv,
                branch_norm=m_branch_norm, w_out=m_w_out, norm_post=m_norm_post)
    mom2 = dict(norm_pre=v_norm_pre, w_in=v_w_in, pool_w=v_pool_w, pool_scale=v_pool_scale, sgu_ln_g=v_sgu_ln_g,
                sgu_ln_b=v_sgu_ln_b, sgu_w=v_sgu_w, sgu_b=v_sgu_b, mem_norm=v_mem_norm, w_kv=v_w_kv,
                branch_norm=v_branch_norm, w_out=v_w_out, norm_post=v_norm_post)

    s, d = x.shape[1], x.shape[2]
    x2, mem2, tgt2 = x[0], mem[0], loss_target[0]
    t_branch = min(256, s)
    tm = min(512, s)

    core = lax.axis_index("c")
    chip = 2 * lax.axis_index("x") + lax.axis_index("y")
    pos = jnp.stack([core, chip]).astype(jnp.int32)
    n_in, n_kv, n_out = 4 * w_in.shape[2], 4 * w_kv.shape[1], 4 * w_out.shape[1]
    wi_rows, kv_rows, wo_rows = d // 8, n_kv // 8, n_out // 8

    def placed(shard, full_shape, block, grid, in_map, out_map, name):
        return _blockwise(_cast_copy, pos, [shard], [pl.BlockSpec(block, in_map)],
                          jax.ShapeDtypeStruct(full_shape, BF16), pl.BlockSpec(block, out_map), grid, name)

    wi_full, wkv_full, wo_full, pw_full = _gather_weights([
        placed(w_in[0], (d, n_in), (wi_rows, n_in // 4), (8,), lambda i, p: (i, 0), lambda i, p: (i, p[1]),
               "place_w_in"),
        placed(w_kv[0], (n_kv, w_kv.shape[2]), (kv_rows, w_kv.shape[2]), (2,), lambda i, p: (i, 0),
               lambda i, p: (2 * p[1] + i, 0), "place_w_kv"),
        placed(w_out[0], (n_out, d), (wo_rows, d), (2,), lambda i, p: (i, 0), lambda i, p: (2 * p[1] + i, 0),
               "place_w_out"),
        placed(pool_w[0], (4, GROUP, GROUP), (4, GROUP // 4, GROUP), (1,), lambda i, p: (0, 0, 0),
               lambda i, p: (0, p[1], 0), "place_pool_w"),
    ])

    mem_g = mem_norm.reshape(1, d)
    k_m, v_m = _kv_fwd(mem2, mem_g, wkv_full)
    h = _rms_pre(x2, norm_pre, tm)
    proj = _matmul(h, wi_full, NN, F32, min(1024, s), 512, d, "proj")
    bias_full = jnp.repeat(sgu_b[0].T, CHUNK, axis=1)
    y = _branches_fwd(proj, pw_full, pool_scale, sgu_ln_g, sgu_ln_b, sgu_w[0], bias_full, k_m, v_m, branch_norm,
                      t_branch)
    loss_local, dz, dout, dy, g_norm_post = _out_loss(y, wo_full, x2, tgt2, norm_post, min(256, s))

    g_wo = _matmul(y, dout, TN, BF16, 1536, 1024, min(512, s), "grad_w_out")
    (dproj, g_pw, g_pool_scale, g_ln_g, g_ln_b, g_sgu_w, g_sgu_b_t, g_branch_norm, dk, dv) = _branches_bwd(
        proj, dy, pw_full, pool_scale, sgu_ln_g, sgu_ln_b, sgu_w[0], jnp.swapaxes(sgu_w[0], 1, 2), bias_full,
        k_m, v_m, branch_norm, t_branch)
    g_wkv, g_mem_norm = _kv_bwd(mem2, mem_g, wkv_full, dk, dv)
    g_wi = _matmul(h, dproj, TN, BF16, d, 1024, min(512, s), "grad_w_in")
    grad_x, g_norm_pre = _dx_call(dproj, wi_full, x2, dz, norm_pre, tm, 1024)

    kv_cols, pw_rows = wkv_full.shape[1], GROUP // 8
    g_views = [g_wi.reshape(2, d // 2, n_in), g_wkv.reshape(4, 2, kv_rows, kv_cols), g_wo.reshape(4, 2, wo_rows, d),
               g_pw.astype(BF16).reshape(4, 4, 2, pw_rows, GROUP)]
    theirs = _exchange_halves([(g_views[0], 0, [(1, 16)]), (g_views[1], 1, [(0, 4), (2, 2)]),
                               (g_views[2], 1, [(0, 4), (2, 2)]), (g_views[3], 2, [(0, 4)])])
    psum = [
        _blockwise(_pair_sum, pos, [g_views[0], theirs[0]],
                   [pl.BlockSpec((1, 128, n_in), lambda i, p: (p[0], i, 0)),
                    pl.BlockSpec((1, 128, n_in), lambda i, p: (0, i, 0))],
                   jax.ShapeDtypeStruct((d // 2, n_in), BF16), pl.BlockSpec((128, n_in), lambda i, p: (i, 0)),
                   (d // 2 // 128,), "pair_sum_w_in"),
        _blockwise(_pair_sum, pos, [g_views[1], theirs[1]],
                   [pl.BlockSpec((1, 1, kv_rows, kv_cols), lambda i, p: (i, p[0], 0, 0)),
                    pl.BlockSpec((1, 1, kv_rows, kv_cols), lambda i, p: (i, 0, 0, 0))],
                   jax.ShapeDtypeStruct((4, kv_rows, kv_cols), BF16),
                   pl.BlockSpec((1, kv_rows, kv_cols), lambda i, p: (i, 0, 0)), (4,), "pair_sum_w_kv"),
        _blockwise(_pair_sum, pos, [g_views[2], theirs[2]],
                   [pl.BlockSpec((1, 1, wo_rows, d), lambda i, p: (i, p[0], 0, 0)),
                    pl.BlockSpec((1, 1, wo_rows, d), lambda i, p: (i, 0, 0, 0))],
                   jax.ShapeDtypeStruct((4, wo_rows, d), BF16),
                   pl.BlockSpec((1, wo_rows, d), lambda i, p: (i, 0, 0)), (4,), "pair_sum_w_out"),
        _blockwise(_pair_sum, pos, [g_views[3], theirs[3]],
                   [pl.BlockSpec((1, 4, 1, pw_rows, GROUP), lambda i, p: (i, 0, p[0], 0, 0)),
                    pl.BlockSpec((1, 4, 1, pw_rows, GROUP), lambda i, p: (i, 0, 0, 0, 0))],
                   jax.ShapeDtypeStruct((4, 4, pw_rows, GROUP), BF16),
                   pl.BlockSpec((1, 4, pw_rows, GROUP), lambda i, p: (i, 0, 0, 0)), (4,), "pair_sum_pool_w"),
    ]
    landed = _scatter_to_chips([(psum[0], 1, [(0, 4)]), (psum[1], 0, [(1, 2)]), (psum[2], 0, [(1, 2)]),
                                (psum[3], 1, [])])
    from_chip = lambda spec_shape, rank: [
        pl.BlockSpec(spec_shape, functools.partial(lambda i, p, q: (q, i) + (0,) * (rank - 2), q=q))
        for q in range(3)]
    joined = _join_halves([
        (_blockwise(_four_sum, pos, [psum[0]] + [landed[0]] * 3,
                    [pl.BlockSpec((256, n_in // 4), lambda i, p: (i, p[1]))] + from_chip((1, 256, n_in // 4), 3),
                    jax.ShapeDtypeStruct((2, d // 2, n_in // 4), F32),
                    pl.BlockSpec((1, 256, n_in // 4), lambda i, p: (p[0], i, 0)), (d // 2 // 256,), "chip_sum_w_in"),
         0, [(1, 8)]),
        (_blockwise(_four_sum, pos, [psum[1]] + [landed[1]] * 3,
                    [pl.BlockSpec((1, kv_rows, kv_cols), lambda i, p: (p[1], 0, 0))]
                    + from_chip((1, 1, kv_rows, kv_cols), 4),
                    jax.ShapeDtypeStruct((2, kv_rows, kv_cols), F32),
                    pl.BlockSpec((1, kv_rows, kv_cols), lambda i, p: (p[0], 0, 0)), (1,), "chip_sum_w_kv"),
         0, [(1, 2)]),
        (_blockwise(_four_sum, pos, [psum[2]] + [landed[2]] * 3,
                    [pl.BlockSpec((1, wo_rows, d), lambda i, p: (p[1], 0, 0))] + from_chip((1, 1, wo_rows, d), 4),
                    jax.ShapeDtypeStruct((2, wo_rows, d), F32),
                    pl.BlockSpec((1, wo_rows, d), lambda i, p: (p[0], 0, 0)), (1,), "chip_sum_w_out"),
         0, [(1, 2)]),
        (_blockwise(_four_sum, pos, [psum[3]] + [landed[3]] * 3,
                    [pl.BlockSpec((4, 1, pw_rows, GROUP), lambda i, p: (0, p[1], 0, 0))]
                    + from_chip((1, 4, 1, pw_rows, GROUP), 5),
                    jax.ShapeDtypeStruct((4, 2, pw_rows, GROUP), F32),
                    pl.BlockSpec((4, 1, pw_rows, GROUP), lambda i, p: (0, p[0], 0, 0)), (1,), "chip_sum_pool_w"),
         1, []),
    ])
    grads = {"w_in": joined[0].reshape(w_in.shape), "w_kv": joined[1].reshape(w_kv.shape),
             "w_out": joined[2].reshape(w_out.shape), "pool_w": joined[3].reshape(pool_w.shape)}

    small_local = dict(norm_pre=g_norm_pre, pool_scale=g_pool_scale, sgu_ln_g=g_ln_g, sgu_ln_b=g_ln_b,
                       sgu_w=g_sgu_w, sgu_b=g_sgu_b_t.T, mem_norm=g_mem_norm, branch_norm=g_branch_norm,
                       norm_post=g_norm_post)
    small_sum = _allreduce_small(_pack([small_local[n] for n in SMALL]))
    for n, g in zip(SMALL, _unpack(small_sum, [weights[n] for n in SMALL])):
        grads[n] = g

    delta, new_m, new_v = {}, {}, {}
    packed = [small_sum if src is grads else _pack([src[n] for n in SMALL]) for src in (weights, grads, mom1, mom2)]
    outs = _elementwise(_adamw, packed, [F32, F32, F32], "adamw_small")
    for dst, o in zip((delta, new_m, new_v), outs):
        for n, a in zip(SMALL, _unpack(o, [weights[n] for n in SMALL])):
            dst[n] = a
    for n in LARGE:
        cols = weights[n].shape[-1]
        outs = _elementwise(_adamw, [src[n].reshape(-1, cols) for src in (weights, grads, mom1, mom2)],
                            [F32, F32, F32], "adamw_" + n)
        for dst, o in zip((delta, new_m, new_v), outs):
            dst[n] = o.reshape(weights[n].shape)

    loss = lax.psum(loss_local[0, 0], ("x", "y", "c"))
    return (loss, grad_x[None], *[grads[n] for n in ORDER], *[delta[n] for n in ORDER],
            *[new_m[n] for n in ORDER], *[new_v[n] for n in ORDER])
```

```python
import functools

import jax
import jax.numpy as jnp
from jax import lax
from jax.experimental import pallas as pl
from jax.experimental.pallas import tpu as pltpu

F32 = jnp.float32
BF16 = jnp.bfloat16
EPS = 1e-6
MESH = pl.DeviceIdType.MESH
ANY = pl.BlockSpec(memory_space=pl.ANY)

POOL_WINDOWS = (2, 4, 8, 16)
GROUP = 256
HALO = 16
CHUNK = 128
N_SGU_HEADS = 8
N_ATT_HEADS = 4
ATT_DIM = 256
WIDTH = 1024
ATT_SCALE = 1.0 / 16.0

ADAM_LR = 0.001
ADAM_B1 = 0.9
ADAM_B2 = 0.999
ADAM_EPS = 1e-08
ADAM_WD = 0.01
ADAM_STEP = 10

VMEM_LIMIT = 60 * 1024 * 1024


def _params(n_grid_axes, vmem=VMEM_LIMIT):
    return pltpu.CompilerParams(dimension_semantics=("arbitrary",) * n_grid_axes, vmem_limit_bytes=vmem)


def _dot(a, b, dims):
    return lax.dot_general(a, b, (dims, ((), ())), preferred_element_type=F32)


NN = ((1,), (0,))
NT = ((1,), (1,))
TN = ((0,), (0,))


class _Rider:
    def __init__(self, inputs, out_shapes, n_sems, run, aliases=None):
        self.inputs, self.out_shapes, self.n_sems, self.run = list(inputs), list(out_shapes), n_sems, run
        self.aliases = aliases or {}


def _call(body, name, grid, in_specs, out_specs, out_shape, scratch_shapes, inputs, rider=None):
    n_in, n_out, n_scr = len(in_specs), len(out_specs), len(scratch_shapes)
    if rider is None:
        outs = pl.pallas_call(body, name=name, grid=grid, in_specs=in_specs, out_specs=tuple(out_specs),
                              out_shape=tuple(out_shape), scratch_shapes=scratch_shapes,
                              compiler_params=_params(len(grid)))(*inputs)
        return tuple(outs)
    r_in, r_out = len(rider.inputs), len(rider.out_shapes)

    def body_with_rider(*refs):
        ins, rider_ins = refs[:n_in], refs[n_in:n_in + r_in]
        refs = refs[n_in + r_in:]
        outs, rider_outs = refs[:n_out], refs[n_out:n_out + r_out]
        refs = refs[n_out + r_out:]
        scratch, (send_sems, recv_sems) = refs[:n_scr], refs[n_scr:]
        ids = [pl.program_id(ax) for ax in range(len(grid))]
        first = functools.reduce(lambda p, q: p & q, [i == 0 for i in ids])
        last = functools.reduce(lambda p, q: p & q, [i == g - 1 for i, g in zip(ids, grid)])

        @pl.when(first)
        def _():
            rider.run(rider_ins, rider_outs, send_sems, recv_sems, True)

        body(*ins, *outs, *scratch)

        @pl.when(last)
        def _():
            rider.run(rider_ins, rider_outs, send_sems, recv_sems, False)

    outs = pl.pallas_call(
        body_with_rider, name=name, grid=grid,
        in_specs=list(in_specs) + [ANY] * r_in, out_specs=tuple(out_specs) + (ANY,) * r_out,
        out_shape=tuple(out_shape) + tuple(rider.out_shapes),
        scratch_shapes=list(scratch_shapes) + [pltpu.SemaphoreType.DMA((rider.n_sems,)),
                                               pltpu.SemaphoreType.DMA((rider.n_sems,))],
        input_output_aliases={n_in + i: n_out + o for i, o in rider.aliases.items()},
        compiler_params=_params(len(grid)),
    )(*inputs, *rider.inputs)
    return tuple(outs)


def _matmul(a, b, dims, out_dtype, tm, tn, tk, name, rider=None):
    if dims == NN:
        (m, k), n = a.shape, b.shape[1]
        a_spec = pl.BlockSpec((tm, tk), lambda i, j, kk: (i, kk))
        b_spec = pl.BlockSpec((tk, tn), lambda i, j, kk: (kk, j))
    else:
        (k, m), n = a.shape, b.shape[1]
        a_spec = pl.BlockSpec((tk, tm), lambda i, j, kk: (kk, i))
        b_spec = pl.BlockSpec((tk, tn), lambda i, j, kk: (kk, j))
    nk = k // tk

    def body(a_ref, b_ref, o_ref, acc_ref):
        kk = pl.program_id(2)
        part = _dot(a_ref[...], b_ref[...], dims)

        @pl.when(kk == 0)
        def _():
            acc_ref[...] = part

        @pl.when(kk > 0)
        def _():
            acc_ref[...] += part

        @pl.when(kk == nk - 1)
        def _():
            o_ref[...] = acc_ref[...].astype(out_dtype)

    return _call(body, name, (m // tm, n // tn, nk), [a_spec, b_spec],
                 [pl.BlockSpec((tm, tn), lambda i, j, kk: (i, j))], [jax.ShapeDtypeStruct((m, n), out_dtype)],
                 [pltpu.VMEM((tm, tn), F32)], [a, b], rider)


def _rms_pre(x, g, tm):
    s, d = x.shape

    def body(x_ref, g_ref, h_ref):
        xv = x_ref[...]
        r = lax.rsqrt(jnp.mean(xv * xv, axis=-1, keepdims=True) + EPS)
        h_ref[...] = (xv * r * g_ref[...]).astype(BF16)

    return pl.pallas_call(
        body, name="rms_pre", grid=(s // tm,),
        in_specs=[pl.BlockSpec((tm, d), lambda i: (i, 0)), pl.BlockSpec((1, d), lambda i: (0, 0))],
        out_specs=pl.BlockSpec((tm, d), lambda i: (i, 0)),
        out_shape=jax.ShapeDtypeStruct((s, d), BF16),
        compiler_params=_params(1),
    )(x, g)


def _kv_fwd(mem, g, w_kv):
    m, d = mem.shape

    def body(mem_ref, g_ref, w_ref, k_ref, v_ref):
        mv = mem_ref[...]
        r = lax.rsqrt(jnp.mean(mv * mv, axis=-1, keepdims=True) + EPS)
        mem_n = (mv * r * g_ref[...]).astype(BF16)
        kv = _dot(mem_n, w_ref[...], NN)
        k_ref[...] = kv[:, :WIDTH].astype(BF16)
        v_ref[...] = kv[:, WIDTH:].astype(BF16)

    return pl.pallas_call(
        body, name="kv_fwd",
        out_shape=(jax.ShapeDtypeStruct((m, WIDTH), BF16), jax.ShapeDtypeStruct((m, WIDTH), BF16)),
        compiler_params=_params(0),
    )(mem, g, w_kv)


def _kv_bwd(mem, g, w_kv, dk, dv):
    m, d = mem.shape
    n = w_kv.shape[1]
    col = 512

    def body(mem_ref, g_ref, w_ref, dk_ref, dv_ref, dw_ref, dg_ref):
        mv = mem_ref[...]
        r = lax.rsqrt(jnp.mean(mv * mv, axis=-1, keepdims=True) + EPS)
        mem_hat = mv * r
        mem_n = (mem_hat * g_ref[...]).astype(BF16)
        dkv = jnp.concatenate([dk_ref[...], dv_ref[...]], axis=1).astype(BF16)
        for j in range(n // col):
            dw_ref[:, j * col:(j + 1) * col] = _dot(mem_n, dkv[:, j * col:(j + 1) * col], TN).astype(BF16)
        dmem_n = _dot(dkv, w_ref[...], NT)
        dg_ref[...] = jnp.sum(dmem_n * mem_hat, axis=0, keepdims=True)

    return pl.pallas_call(
        body, name="kv_bwd",
        out_shape=(jax.ShapeDtypeStruct((d, n), BF16), jax.ShapeDtypeStruct((1, d), F32)),
        compiler_params=_params(0),
    )(mem, g, w_kv, dk, dv)


def _sigmoid(x):
    return 1.0 / (1.0 + jnp.exp(-x))


def _inv_counts(t0, t):
    pos = (t0 + lax.broadcasted_iota(jnp.int32, (t, 1), 0) + 1).astype(F32)
    return [1.0 / jnp.minimum(pos, float(w)) for w in POOL_WINDOWS]


def _window_sums(ext, t, backward):
    n = t + HALO
    parts = []
    for gi, w in enumerate(POOL_WINDOWS):
        s = ext[:, gi * GROUP:(gi + 1) * GROUP]
        k = 1
        while k < w:
            s = s + pltpu.roll(s, (n - k) if backward else k, axis=0)
            k *= 2
        parts.append(s[:t] if backward else s[HALO:])
    return parts


def _pool_fwd(xa, halo, inv, pool_w):
    t = xa.shape[0]
    sums = _window_sums(jnp.concatenate([halo, xa], axis=0), t, backward=False)
    d = jnp.concatenate([sums[gi] * inv[gi] - xa[:, gi * GROUP:(gi + 1) * GROUP] for gi in range(4)], axis=1)
    d = d.astype(BF16)
    y = jnp.concatenate([_dot(d[:, gi * GROUP:(gi + 1) * GROUP], pool_w[gi], NN) for gi in range(4)], axis=1)
    return d, y


def _layernorm_fwd(v):
    mu = jnp.mean(v, axis=-1, keepdims=True)
    xc = v - mu
    rstd = lax.rsqrt(jnp.mean(xc * xc, axis=-1, keepdims=True) + EPS)
    return xc * rstd, rstd


def _tril_mask(transposed):
    r = lax.broadcasted_iota(jnp.int32, (CHUNK, CHUNK), 0)
    c = lax.broadcasted_iota(jnp.int32, (CHUNK, CHUNK), 1)
    return (r <= c) if transposed else (r >= c)


def _sgu_mix(w_ref, vals, transposed):
    t = vals.shape[0]
    mask = _tril_mask(transposed)
    ws = [jnp.where(mask, w_ref[h], 0.0).astype(BF16) for h in range(N_SGU_HEADS)]
    rows = []
    for ci in range(t // CHUNK):
        blk = vals[ci * CHUNK:(ci + 1) * CHUNK]
        rows.append(jnp.concatenate(
            [_dot(ws[h], blk[:, h * CHUNK:(h + 1) * CHUNK], NN) for h in range(N_SGU_HEADS)], axis=1))
    return jnp.concatenate(rows, axis=0)


def _attn_fwd(q, k, v):
    ps, os_ = [], []
    for h in range(N_ATT_HEADS):
        sl = slice(h * ATT_DIM, (h + 1) * ATT_DIM)
        s = _dot(q[:, sl], k[:, sl], NT) * ATT_SCALE
        s = s - jnp.max(s, axis=-1, keepdims=True)
        e = jnp.exp(s)
        p = e / jnp.sum(e, axis=-1, keepdims=True)
        ps.append(p)
        os_.append(_dot(p.astype(BF16), v[:, sl], NN))
    return ps, jnp.concatenate(os_, axis=1)


def _rms_branch(y_pre):
    r = lax.rsqrt(jnp.mean(y_pre * y_pre, axis=-1, keepdims=True) + EPS)
    return y_pre * r, r


def _branch_specs(t, n_tiles, order):
    width_in = 7 * WIDTH
    tile = lambda i: order(i)
    per_halo = t // HALO
    const2 = lambda i: (0, 0)
    const3 = lambda i: (0, 0, 0)
    return [
        pl.BlockSpec((t, width_in), lambda i: (tile(i), 0)),
        pl.BlockSpec((HALO, WIDTH), lambda i: (jnp.maximum(tile(i) * per_halo - 1, 0), 0)),
        pl.BlockSpec((4, GROUP, GROUP), const3),
        pl.BlockSpec((1, WIDTH), const2),
        pl.BlockSpec((1, WIDTH), const2),
        pl.BlockSpec((1, WIDTH), const2),
        pl.BlockSpec((N_SGU_HEADS, CHUNK, CHUNK), const3),
        pl.BlockSpec((CHUNK, WIDTH), const2),
        pl.BlockSpec((MEM_ROWS, WIDTH), const2),
        pl.BlockSpec((MEM_ROWS, WIDTH), const2),
        pl.BlockSpec((1, 3 * WIDTH), const2),
    ]


MEM_ROWS = 256


def _branches_fwd(proj, pool_w, pool_scale, ln_g, ln_b, sgu_w, bias_full, k, v, branch_norm, t):
    s = proj.shape[0]
    n_tiles = s // t

    def body(proj_ref, halo_ref, pw_ref, ps_ref, lg_ref, lb_ref, sw_ref, sb_ref, k_ref, v_ref, bn_ref, y_ref):
        i = pl.program_id(0)
        col = lambda j: proj_ref[:, j * WIDTH:(j + 1) * WIDTH]
        bn = bn_ref[...]
        halo = jnp.where(i > 0, halo_ref[...], 0.0)
        _, y_pool = _pool_fwd(col(0), halo, _inv_counts(i * t, t), pw_ref[...])
        ga = col(1)
        ya = y_pool * ps_ref[...] * (ga * _sigmoid(ga))
        y_ref[:, 0:WIDTH] = (_rms_branch(ya)[0] * bn[:, 0:WIDTH]).astype(BF16)
        vhat, _ = _layernorm_fwd(col(3))
        vn = (vhat * lg_ref[...] + lb_ref[...]).astype(BF16)
        z = _sgu_mix(sw_ref, vn, transposed=False) + jnp.tile(sb_ref[...], (t // CHUNK, 1))
        gb = col(4)
        yb = col(2) * z * (gb * _sigmoid(gb))
        y_ref[:, WIDTH:2 * WIDTH] = (_rms_branch(yb)[0] * bn[:, WIDTH:2 * WIDTH]).astype(BF16)
        _, o = _attn_fwd(col(5).astype(BF16), k_ref[...], v_ref[...])
        gc = col(6)
        yc = o * (gc * _sigmoid(gc))
        y_ref[:, 2 * WIDTH:] = (_rms_branch(yc)[0] * bn[:, 2 * WIDTH:]).astype(BF16)

    return pl.pallas_call(
        body, name="branches_fwd", grid=(n_tiles,),
        in_specs=_branch_specs(t, n_tiles, lambda i: i),
        out_specs=pl.BlockSpec((t, 3 * WIDTH), lambda i: (i, 0)),
        out_shape=jax.ShapeDtypeStruct((s, 3 * WIDTH), BF16),
        compiler_params=_params(1),
    )(proj, proj, pool_w, pool_scale, ln_g, ln_b, sgu_w, bias_full, k, v, branch_norm)


def _branches_bwd(proj, dy, pool_w, pool_scale, ln_g, ln_b, sgu_w, sgu_wt, bias_full, k, v, branch_norm, t, rider=None):
    s = proj.shape[0]
    n_tiles = s // t
    n_chunks = t // CHUNK
    order = lambda i: n_tiles - 1 - i

    def body(proj_ref, halo_ref, pw_ref, ps_ref, lg_ref, lb_ref, sw_ref, sb_ref, k_ref, v_ref, bn_ref,
             swt_ref, dy_ref,
             dproj_ref, dpw_ref, dps_ref, dlg_ref, dlb_ref, dsw_ref, dsb_ref, dbn_ref, dk_ref, dv_ref,
             carry_ref, dbias_ref):
        step = pl.program_id(0)
        i = order(step)

        @pl.when(step == 0)
        def _():
            for ref in (dpw_ref, dps_ref, dlg_ref, dlb_ref, dsw_ref, dbn_ref, dk_ref, dv_ref, carry_ref, dbias_ref):
                ref[...] = jnp.zeros(ref.shape, ref.dtype)

        col = lambda j: proj_ref[:, j * WIDTH:(j + 1) * WIDTH]
        bn = bn_ref[...]

        def norm_bwd(y_pre, sl):
            yhat, r = _rms_branch(y_pre)
            dyv = dy_ref[:, sl].astype(F32)
            dbn_ref[:, sl] += jnp.sum(dyv * yhat, axis=0, keepdims=True)
            dyhat = dyv * bn[:, sl]
            return r * (dyhat - yhat * jnp.mean(dyhat * yhat, axis=-1, keepdims=True))

        def gate(gv):
            sg = _sigmoid(gv)
            return gv * sg, sg * (1.0 + gv * (1.0 - sg))

        inv = _inv_counts(i * t, t)
        halo = jnp.where(i > 0, halo_ref[...], 0.0)
        pw = pw_ref[...]
        d, y_pool = _pool_fwd(col(0), halo, inv, pw)
        scale = ps_ref[...]
        silu_a, dsilu_a = gate(col(1))
        pa = y_pool * scale
        dya = norm_bwd(pa * silu_a, slice(0, WIDTH))
        dproj_ref[:, WIDTH:2 * WIDTH] = (dya * pa * dsilu_a).astype(BF16)
        dpa = dya * silu_a
        dps_ref[...] += jnp.sum(dpa * y_pool, axis=0, keepdims=True)
        dy_pool = (dpa * scale).astype(BF16)
        dd_parts, ddc_parts = [], []
        for gi in range(4):
            sl = slice(gi * GROUP, (gi + 1) * GROUP)
            dpw_ref[gi] += _dot(d[:, sl], dy_pool[:, sl], TN)
            dd = _dot(dy_pool[:, sl], pw[gi], NT)
            dd_parts.append(dd)
            ddc_parts.append(dd * inv[gi])
        ddc = jnp.concatenate(ddc_parts, axis=1)
        sums = _window_sums(jnp.concatenate([ddc, carry_ref[...]], axis=0), t, backward=True)
        carry_ref[...] = ddc[:HALO]
        dproj_ref[:, 0:WIDTH] = jnp.concatenate([sums[gi] - dd_parts[gi] for gi in range(4)], axis=1).astype(BF16)

        vhat, rstd = _layernorm_fwd(col(3))
        lg = lg_ref[...]
        vn = (vhat * lg + lb_ref[...]).astype(BF16)
        z = _sgu_mix(sw_ref, vn, transposed=False) + jnp.tile(sb_ref[...], (n_chunks, 1))
        u = col(2)
        silu_b, dsilu_b = gate(col(4))
        uz = u * z
        dyb = norm_bwd(uz * silu_b, slice(WIDTH, 2 * WIDTH))
        dproj_ref[:, 4 * WIDTH:5 * WIDTH] = (dyb * uz * dsilu_b).astype(BF16)
        duz = dyb * silu_b
        dproj_ref[:, 2 * WIDTH:3 * WIDTH] = (duz * z).astype(BF16)
        dz = duz * u
        dz_b = dz.astype(BF16)
        for ci in range(n_chunks):
            rows = slice(ci * CHUNK, (ci + 1) * CHUNK)
            dbias_ref[...] += dz[rows]
            for h in range(N_SGU_HEADS):
                sl = slice(h * CHUNK, (h + 1) * CHUNK)
                dsw_ref[h] += _dot(dz_b[rows, sl], vn[rows, sl], NT)
        dvn = _sgu_mix(swt_ref, dz_b, transposed=True)
        dlg_ref[...] += jnp.sum(dvn * vhat, axis=0, keepdims=True)
        dlb_ref[...] += jnp.sum(dvn, axis=0, keepdims=True)
        dvhat = dvn * lg
        dvb = rstd * (dvhat - jnp.mean(dvhat, axis=-1, keepdims=True)
                      - vhat * jnp.mean(dvhat * vhat, axis=-1, keepdims=True))
        dproj_ref[:, 3 * WIDTH:4 * WIDTH] = dvb.astype(BF16)

        q = col(5).astype(BF16)
        kv_k, kv_v = k_ref[...], v_ref[...]
        ps, o = _attn_fwd(q, kv_k, kv_v)
        silu_c, dsilu_c = gate(col(6))
        dyc = norm_bwd(o * silu_c, slice(2 * WIDTH, 3 * WIDTH))
        dproj_ref[:, 6 * WIDTH:7 * WIDTH] = (dyc * o * dsilu_c).astype(BF16)
        do = (dyc * silu_c).astype(BF16)
        dq_parts = []
        for h in range(N_ATT_HEADS):
            sl = slice(h * ATT_DIM, (h + 1) * ATT_DIM)
            p = ps[h]
            dp = _dot(do[:, sl], kv_v[:, sl], NT)
            ds = (p * (dp - jnp.sum(p * dp, axis=-1, keepdims=True)) * ATT_SCALE).astype(BF16)
            dq_parts.append(_dot(ds, kv_k[:, sl], NN))
            dk_ref[:, sl] += _dot(ds, q[:, sl], TN)
            dv_ref[:, sl] += _dot(p.astype(BF16), do[:, sl], TN)
        dproj_ref[:, 5 * WIDTH:6 * WIDTH] = jnp.concatenate(dq_parts, axis=1).astype(BF16)

        @pl.when(step == n_tiles - 1)
        def _():
            keep = _tril_mask(transposed=False)
            for h in range(N_SGU_HEADS):
                dsw_ref[h] = jnp.where(keep, dsw_ref[h], 0.0)
            dsb_ref[...] = jnp.concatenate(
                [jnp.sum(dbias_ref[:, h * CHUNK:(h + 1) * CHUNK], axis=1, keepdims=True)
                 for h in range(N_SGU_HEADS)], axis=1)

    const2 = lambda i: (0, 0)
    const3 = lambda i: (0, 0, 0)
    out_shapes = (
        jax.ShapeDtypeStruct((s, 7 * WIDTH), BF16),
        jax.ShapeDtypeStruct((4, GROUP, GROUP), F32),
        jax.ShapeDtypeStruct((1, WIDTH), F32),
        jax.ShapeDtypeStruct((1, WIDTH), F32),
        jax.ShapeDtypeStruct((1, WIDTH), F32),
        jax.ShapeDtypeStruct((N_SGU_HEADS, CHUNK, CHUNK), F32),
        jax.ShapeDtypeStruct((CHUNK, N_SGU_HEADS), F32),
        jax.ShapeDtypeStruct((1, 3 * WIDTH), F32),
        jax.ShapeDtypeStruct((MEM_ROWS, WIDTH), F32),
        jax.ShapeDtypeStruct((MEM_ROWS, WIDTH), F32),
    )
    out_specs = (
        pl.BlockSpec((t, 7 * WIDTH), lambda i: (order(i), 0)),
        pl.BlockSpec((4, GROUP, GROUP), const3),
        pl.BlockSpec((1, WIDTH), const2),
        pl.BlockSpec((1, WIDTH), const2),
        pl.BlockSpec((1, WIDTH), const2),
        pl.BlockSpec((N_SGU_HEADS, CHUNK, CHUNK), const3),
        pl.BlockSpec((CHUNK, N_SGU_HEADS), const2),
        pl.BlockSpec((1, 3 * WIDTH), const2),
        pl.BlockSpec((MEM_ROWS, WIDTH), const2),
        pl.BlockSpec((MEM_ROWS, WIDTH), const2),
    )
    in_specs = _branch_specs(t, n_tiles, order) + [
        pl.BlockSpec((N_SGU_HEADS, CHUNK, CHUNK), const3),
        pl.BlockSpec((t, 3 * WIDTH), lambda i: (order(i), 0)),
    ]
    return _call(body, "branches_bwd", (n_tiles,), in_specs, out_specs, out_shapes,
                 [pltpu.VMEM((HALO, WIDTH), F32), pltpu.VMEM((CHUNK, WIDTH), F32)],
                 [proj, proj, pool_w, pool_scale, ln_g, ln_b, sgu_w, bias_full, k, v, branch_norm, sgu_wt, dy], rider)


def _out_loss(y, w_out, x, target, g_post, tm):
    s, d = x.shape
    e_w = y.shape[1]
    n_tiles = s // tm

    def body(y_ref, w_ref, x_ref, t_ref, g_ref, loss_ref, dz_ref, dout_ref, dy_ref, dg_ref, sq_ref):
        i = pl.program_id(0)

        @pl.when(i == 0)
        def _():
            sq_ref[...] = jnp.zeros(sq_ref.shape, F32)
            dg_ref[...] = jnp.zeros(dg_ref.shape, F32)

        w = w_ref[...]
        out = _dot(y_ref[...], w, NN)
        r = lax.rsqrt(jnp.mean(out * out, axis=-1, keepdims=True) + EPS)
        outn = out * r
        g = g_ref[...]
        err = (x_ref[...] + outn * g) - t_ref[...]
        sq_ref[...] += jnp.sum(err * err, axis=0, keepdims=True)
        dz = err * (1.0 / d)
        dz_ref[...] = dz
        dg_ref[...] += jnp.sum(dz * outn, axis=0, keepdims=True)
        doutn = dz * g
        dout = (r * (doutn - outn * jnp.mean(doutn * outn, axis=-1, keepdims=True))).astype(BF16)
        dout_ref[...] = dout
        dy_ref[...] = _dot(dout, w, NT).astype(BF16)

        @pl.when(i == n_tiles - 1)
        def _():
            loss_ref[...] = 0.5 * jnp.sum(sq_ref[...], axis=1, keepdims=True) * (1.0 / d)

    row = lambda i: (i, 0)
    const2 = lambda i: (0, 0)
    return pl.pallas_call(
        body, name="out_loss", grid=(n_tiles,),
        in_specs=[
            pl.BlockSpec((tm, e_w), row),
            pl.BlockSpec((e_w, d), const2, pipeline_mode=pl.Buffered(1)),
            pl.BlockSpec((tm, d), row),
            pl.BlockSpec((tm, d), row),
            pl.BlockSpec((1, d), const2),
        ],
        out_specs=(
            pl.BlockSpec((1, 1), const2),
            pl.BlockSpec((tm, d), row),
            pl.BlockSpec((tm, d), row),
            pl.BlockSpec((tm, e_w), row),
            pl.BlockSpec((1, d), const2),
        ),
        out_shape=(
            jax.ShapeDtypeStruct((1, 1), F32),
            jax.ShapeDtypeStruct((s, d), F32),
            jax.ShapeDtypeStruct((s, d), BF16),
            jax.ShapeDtypeStruct((s, e_w), BF16),
            jax.ShapeDtypeStruct((1, d), F32),
        ),
        scratch_shapes=[pltpu.VMEM((1, d), F32)],
        compiler_params=_params(1),
    )(y, w_out, x, target, g_post)


def _dx_call(dproj, w_in, x, dz, g_pre, tm, tk, rider=None):
    s, d = x.shape
    k_total = dproj.shape[1]
    nk = k_total // tk
    n_tiles = s // tm

    def body(dp_ref, w_ref, x_ref, dz_ref, g_ref, dx_ref, dg_ref, acc_ref):
        i, kk = pl.program_id(0), pl.program_id(1)
        part = _dot(dp_ref[...], w_ref[...], NT)

        @pl.when(kk == 0)
        def _():
            acc_ref[...] = part

        @pl.when(kk > 0)
        def _():
            acc_ref[...] += part

        @pl.when((i == 0) & (kk == 0))
        def _():
            dg_ref[...] = jnp.zeros(dg_ref.shape, F32)

        @pl.when(kk == nk - 1)
        def _():
            dh = acc_ref[...]
            xv = x_ref[...]
            r = lax.rsqrt(jnp.mean(xv * xv, axis=-1, keepdims=True) + EPS)
            xhat = xv * r
            dg_ref[...] += jnp.sum(dh * xhat, axis=0, keepdims=True)
            dxhat = dh * g_ref[...]
            dx_ref[...] = dz_ref[...] + r * (dxhat - xhat * jnp.mean(dxhat * xhat, axis=-1, keepdims=True))

    row = lambda i, kk: (i, 0)
    const2 = lambda i, kk: (0, 0)
    return _call(
        body, "dx", (n_tiles, nk),
        [
            pl.BlockSpec((tm, tk), lambda i, kk: (i, kk)),
            pl.BlockSpec((d, tk), lambda i, kk: (0, kk)),
            pl.BlockSpec((tm, d), row),
            pl.BlockSpec((tm, d), row),
            pl.BlockSpec((1, d), const2),
        ],
        [pl.BlockSpec((tm, d), row), pl.BlockSpec((1, d), const2)],
        [jax.ShapeDtypeStruct((s, d), F32), jax.ShapeDtypeStruct((1, d), F32)],
        [pltpu.VMEM((tm, d), F32)], [dproj, w_in, x, dz, g_pre], rider)


def _rows_tile(rows, cols, n_arrays, itemsize=4):
    budget = 24 * 1024 * 1024 // (2 * n_arrays * cols * itemsize)
    if rows <= budget:
        return rows
    best = None
    for cand in range(16, rows + 1, 16):
        if rows % cand == 0 and cand <= max(budget, 16):
            best = cand
    return best if best is not None else rows


def _elementwise(fn, inputs, out_dtypes, name):
    rows, cols = inputs[0].shape
    tr = _rows_tile(rows, cols, len(inputs) + len(out_dtypes))
    n_in = len(inputs)

    def body(*refs):
        outs = fn(*[r[...] for r in refs[:n_in]])
        for o_ref, o in zip(refs[n_in:], outs):
            o_ref[...] = o.astype(o_ref.dtype)

    spec = pl.BlockSpec((tr, cols), lambda i: (i, 0))
    return pl.pallas_call(
        body, name=name, grid=(rows // tr,),
        in_specs=[spec] * n_in, out_specs=tuple([spec] * len(out_dtypes)),
        out_shape=tuple(jax.ShapeDtypeStruct((rows, cols), dt) for dt in out_dtypes),
        compiler_params=_params(1),
    )(*inputs)


def _blockwise(fn, pos, inputs, in_specs, out_shape, out_spec, grid, name):
    n_in = len(inputs)

    def body(pos_ref, *refs):
        o_ref = refs[n_in]
        (out,) = fn(*[r[...].reshape(o_ref.shape) for r in refs[:n_in]])
        o_ref[...] = out.astype(o_ref.dtype)

    return pl.pallas_call(
        body, name=name,
        grid_spec=pltpu.PrefetchScalarGridSpec(num_scalar_prefetch=1, grid=grid, in_specs=in_specs,
                                               out_specs=out_spec),
        out_shape=out_shape,
        compiler_params=_params(len(grid)),
    )(pos, *inputs)


def _cast_copy(x):
    return (x,)


def _pair_sum(mine, theirs):
    return ((mine.astype(F32) + theirs.astype(F32)),)


def _four_sum(own, t0, t1, t2):
    return ((((own.astype(F32) + t0.astype(F32)) + t1.astype(F32)) + t2.astype(F32)),)


def _adamw(w, g, m, v):
    m = ADAM_B1 * m + (1.0 - ADAM_B1) * g
    v = ADAM_B2 * v + (1.0 - ADAM_B2) * jnp.square(g)
    m_hat = m / (1.0 - ADAM_B1 ** ADAM_STEP)
    v_hat = v / (1.0 - ADAM_B2 ** ADAM_STEP)
    delta = -ADAM_LR * (m_hat / (jnp.sqrt(v_hat) + ADAM_EPS) + ADAM_WD * w)
    return delta, m, v


def _place():
    x, y, c = lax.axis_index("x"), lax.axis_index("y"), lax.axis_index("c")
    chips = [(1 - x, y), (x, 1 - y), (1 - x, 1 - y)]
    return x, y, c, chips


def _remote(src, dst, send_sem, recv_sem, to):
    return pltpu.make_async_remote_copy(src_ref=src, dst_ref=dst, send_sem=send_sem, recv_sem=recv_sem,
                                        device_id=to, device_id_type=MESH)


def _split(ref, plan):
    views = [ref]
    for axis, parts in plan:
        size = ref.shape[axis] // parts
        assert size * parts == ref.shape[axis]
        views = [v.at[tuple(pl.ds(q * size, size) if i == axis else slice(None) for i in range(len(ref.shape)))]
                 for v in views for q in range(parts)]
    return views


def _remote_in_parts(src, dst, send_sem, recv_sem, to, plan):
    for s, d in zip(_split(src, plan), _split(dst, plan)):
        _remote(s, d, send_sem, recv_sem, to).start()
    return _remote(src, dst, send_sem, recv_sem, to)


def _local_in_parts(src, dst, sem, plan):
    for s, d in zip(_split(src, plan), _split(dst, plan)):
        pltpu.make_async_copy(s, d, sem).start()
    return pltpu.make_async_copy(src, dst, sem)


def _hbm_call(body, name, inputs, out_shapes, scratch, aliases=None):
    return pl.pallas_call(
        body, name=name,
        in_specs=[ANY] * len(inputs), out_specs=tuple([ANY] * len(out_shapes)), out_shape=tuple(out_shapes),
        scratch_shapes=scratch, input_output_aliases=aliases or {},
        compiler_params=pltpu.CompilerParams(has_side_effects=True),
    )(*inputs)


def _gather_rider(fulls, kinds):
    n = len(fulls)

    def full_half(a, ref, chip, cc):
        if a == 0:
            rows, cols = ref.shape[0] // 2, ref.shape[1] // 4
            return ref.at[pl.ds(cc * rows, rows), pl.ds(pl.multiple_of(chip * cols, 128), cols)]
        if a == 3:
            rows = ref.shape[1] // 8
            return ref.at[:, pl.ds(pl.multiple_of((2 * chip + cc) * rows, 16), rows), :]
        rows = ref.shape[0] // 8
        return ref.at[pl.ds(pl.multiple_of((2 * chip + cc) * rows, 16), rows), :]

    def run(in_refs, full_refs, send_sems, recv_sems, start):
        x, y, c, chips = _place()
        me = 2 * x + y
        sibling = (x, y, 1 - c)
        plans = [[(0, 8)], [(0, 2)], [(0, 2)], []]
        if start:
            for p, chip in enumerate(chips):
                for a in range(n):
                    mine = full_half(kinds[a], full_refs[a], me, c)
                    _remote_in_parts(mine, mine, send_sems.at[6 * a + p], recv_sems.at[6 * a + p], (*chip, c),
                                     plans[kinds[a]])
            return
        passed_on = []
        for p, chip in enumerate(chips):
            them = 2 * chip[0] + chip[1]
            for a in range(n):
                landed = full_half(kinds[a], full_refs[a], them, c)
                _remote(landed, landed, send_sems.at[6 * a + p], recv_sems.at[6 * a + p], (*chip, c)).wait_recv()
                passed_on.append(_remote_in_parts(landed, landed, send_sems.at[6 * a + 3 + p],
                                                  recv_sems.at[6 * a + 3 + p], sibling, plans[kinds[a]]))
        for p, chip in enumerate(chips):
            them = 2 * chip[0] + chip[1]
            for a in range(n):
                passed = full_half(kinds[a], full_refs[a], them, 1 - c)
                _remote(passed, passed, send_sems.at[6 * a + 3 + p], recv_sems.at[6 * a + 3 + p], sibling).wait_recv()
                mine = full_half(kinds[a], full_refs[a], me, c)
                _remote(mine, mine, send_sems.at[6 * a + p], recv_sems.at[6 * a + p], (*chip, c)).wait_send()
        for cp in passed_on:
            cp.wait_send()

    return _Rider(fulls, [jax.ShapeDtypeStruct(f.shape, f.dtype) for f in fulls], 6 * n, run,
                  aliases={a: a for a in range(n)})


def _run_rider(rider, name):
    r_in = len(rider.inputs)

    def body(*refs):
        in_refs, out_refs = refs[:r_in], refs[r_in:r_in + len(rider.out_shapes)]
        send_sems, recv_sems = refs[r_in + len(rider.out_shapes):]
        rider.run(in_refs, out_refs, send_sems, recv_sems, True)
        rider.run(in_refs, out_refs, send_sems, recv_sems, False)

    return _hbm_call(body, name, rider.inputs, rider.out_shapes,
                     [pltpu.SemaphoreType.DMA((rider.n_sems,)), pltpu.SemaphoreType.DMA((rider.n_sems,))],
                     aliases=rider.aliases)


def _exchange_halves(grads, name):
    n = len(grads)
    arrays = [g for g, _, _ in grads]
    out_shapes = [jax.ShapeDtypeStruct(tuple(1 if i == ax else dim for i, dim in enumerate(g.shape)), g.dtype)
                  for g, ax, _ in grads]

    def half(ref, ax, cc):
        idx = tuple(pl.ds(cc, 1) if i == ax else slice(None) for i in range(len(ref.shape)))
        return ref.at[idx]

    def body(*refs):
        in_refs, out_refs = refs[:n], refs[n:2 * n]
        send_sems, recv_sems = refs[2 * n:]
        x, y, c, _ = _place()
        sibling = (x, y, 1 - c)
        copies = [_remote_in_parts(half(in_refs[a], grads[a][1], 1 - c), out_refs[a], send_sems.at[a],
                                   recv_sems.at[a], sibling, grads[a][2]) for a in range(n)]
        for rem in copies:
            rem.wait()

    return _hbm_call(body, name, arrays, out_shapes,
                     [pltpu.SemaphoreType.DMA((n,)), pltpu.SemaphoreType.DMA((n,))])


def _scatter_rider(parts):
    n = len(parts)
    arrays = [p for p, _, _ in parts]

    def block_shape(p, ax):
        if ax == len(p.shape) - 1:
            return p.shape[:-1] + (p.shape[-1] // 4,)
        return tuple(1 if i == ax else dim for i, dim in enumerate(p.shape))

    out_shapes = [jax.ShapeDtypeStruct((3,) + block_shape(p, ax), p.dtype) for p, ax, _ in parts]

    def block(ref, ax, chip):
        rank = len(ref.shape)
        if ax == rank - 1:
            cols = ref.shape[-1] // 4
            last = pl.ds(pl.multiple_of(chip * cols, 128), cols)
            return ref.at[tuple([slice(None)] * (rank - 1) + [last])]
        return ref.at[tuple(pl.ds(chip, 1) if i == ax else slice(None) for i in range(rank))]

    def run(in_refs, out_refs, send_sems, recv_sems, start):
        x, y, c, chips = _place()
        for a in range(n):
            ax, plan = parts[a][1], parts[a][2]
            for p, chip in enumerate(chips):
                src, dst = block(in_refs[a], ax, 2 * chip[0] + chip[1]), out_refs[a].at[p]
                sems = (send_sems.at[3 * a + p], recv_sems.at[3 * a + p])
                if start:
                    _remote_in_parts(src, dst, *sems, (*chip, c), plan)
                else:
                    _remote(src, dst, *sems, (*chip, c)).wait()

    return _Rider(arrays, out_shapes, 3 * n, run)


def _join_halves(joined):
    n = len(joined)
    arrays = [j for j, _, _ in joined]

    def body(*refs):
        out_refs = refs[n:2 * n]
        send_sems, recv_sems = refs[2 * n:]
        x, y, c, _ = _place()
        sibling = (x, y, 1 - c)

        def half(a, cc):
            rank = len(out_refs[a].shape)
            return out_refs[a].at[tuple(pl.ds(cc, 1) if i == joined[a][1] else slice(None) for i in range(rank))]

        sends = [_remote_in_parts(half(a, c), half(a, c), send_sems.at[a], recv_sems.at[a], sibling, joined[a][2])
                 for a in range(n)]
        for a, rem in enumerate(sends):
            rem.wait_send()
            _remote(half(a, 1 - c), half(a, 1 - c), send_sems.at[a], recv_sems.at[a], sibling).wait_recv()

    return _hbm_call(body, "join_halves", arrays, [jax.ShapeDtypeStruct(j.shape, j.dtype) for j in arrays],
                     [pltpu.SemaphoreType.DMA((n,)), pltpu.SemaphoreType.DMA((n,))],
                     aliases={a: a for a in range(n)})


def _allreduce_small(packed):
    rows, lanes = packed.shape
    half = rows // 2

    def body(in_ref, out_ref, pair_ref, gath_ref, send_sems, recv_sems):
        x, y, c, chips = _place()
        me = 2 * x + y
        sibling = (x, y, 1 - c)
        mine = pl.ds(pl.multiple_of(c * half, 8), half)
        theirs = pl.ds(pl.multiple_of((1 - c) * half, 8), half)
        to_sib = _remote(in_ref.at[theirs], pair_ref, send_sems.at[0], recv_sems.at[0], sibling)
        to_sib.start()
        to_sib.wait()
        gath_ref[me] = in_ref[mine] + pair_ref[...]
        sends = [_remote(gath_ref.at[me], gath_ref.at[me], send_sems.at[1 + p], recv_sems.at[1 + p], (*chip, c))
                 for p, chip in enumerate(chips)]
        for cp in sends:
            cp.start()
        for p, chip in enumerate(chips):
            slot = gath_ref.at[2 * chip[0] + chip[1]]
            _remote(slot, slot, send_sems.at[1 + p], recv_sems.at[1 + p], (*chip, c)).wait_recv()
        for cp in sends:
            cp.wait_send()
        out_ref[mine] = ((gath_ref[0] + gath_ref[1]) + gath_ref[2]) + gath_ref[3]
        back = _remote(out_ref.at[mine], out_ref.at[mine], send_sems.at[4], recv_sems.at[4], sibling)
        back.start()
        back.wait_send()
        _remote(out_ref.at[theirs], out_ref.at[theirs], send_sems.at[4], recv_sems.at[4], sibling).wait_recv()

    vmem = pl.BlockSpec(memory_space=pltpu.VMEM)
    return pl.pallas_call(
        body, name="allreduce_small",
        in_specs=[vmem], out_specs=vmem, out_shape=jax.ShapeDtypeStruct((rows, lanes), F32),
        scratch_shapes=[pltpu.VMEM((half, lanes), F32), pltpu.VMEM((4, half, lanes), F32),
                        pltpu.SemaphoreType.DMA((5,)), pltpu.SemaphoreType.DMA((5,))],
        compiler_params=pltpu.CompilerParams(has_side_effects=True, vmem_limit_bytes=32 * 1024 * 1024),
    )(packed)


SMALL = ("norm_pre", "pool_scale", "sgu_ln_g", "sgu_ln_b", "sgu_w", "sgu_b", "mem_norm", "branch_norm", "norm_post")
LARGE = ("w_in", "pool_w", "w_kv", "w_out")
ORDER = ("norm_pre", "w_in", "pool_w", "pool_scale", "sgu_ln_g", "sgu_ln_b", "sgu_w", "sgu_b", "mem_norm", "w_kv",
         "branch_norm", "w_out", "norm_post")


def _pack(arrays):
    rows = [a.reshape(-1, 128) for a in arrays]
    pad = -sum(r.shape[0] for r in rows) % 16
    return jnp.concatenate(rows + ([jnp.zeros((pad, 128), F32)] if pad else []), axis=0)


def _unpack(packed, like):
    out, row = [], 0
    for a in like:
        rows = a.size // 128
        out.append(packed[row:row + rows].reshape(a.shape))
        row += rows
    return out


def kernel(x, mem, norm_pre, w_in, pool_w, pool_scale, sgu_ln_g, sgu_ln_b, sgu_w, sgu_b, mem_norm, w_kv, branch_norm, w_out, norm_post, loss_target, m_norm_pre, m_w_in, m_pool_w, m_pool_scale, m_sgu_ln_g, m_sgu_ln_b, m_sgu_w, m_sgu_b, m_mem_norm, m_w_kv, m_branch_norm, m_w_out, m_norm_post, v_norm_pre, v_w_in, v_pool_w, v_pool_scale, v_sgu_ln_g, v_sgu_ln_b, v_sgu_w, v_sgu_b, v_mem_norm, v_w_kv, v_branch_norm, v_w_out, v_norm_post):
    weights = dict(norm_pre=norm_pre, w_in=w_in, pool_w=pool_w, pool_scale=pool_scale, sgu_ln_g=sgu_ln_g,
                   sgu_ln_b=sgu_ln_b, sgu_w=sgu_w, sgu_b=sgu_b, mem_norm=mem_norm, w_kv=w_kv, branch_norm=branch_norm,
                   w_out=w_out, norm_post=norm_post)
    mom1 = dict(norm_pre=m_norm_pre, w_in=m_w_in, pool_w=m_pool_w, pool_scale=m_pool_scale, sgu_ln_g=m_sgu_ln_g,
                sgu_ln_b=m_sgu_ln_b, sgu_w=m_sgu_w, sgu_b=m_sgu_b, mem_norm=m_mem_norm, w_kv=m_w_kv,
                branch_norm=m_branch_norm, w_out=m_w_out, norm_post=m_norm_post)
    mom2 = dict(norm_pre=v_norm_pre, w_in=v_w_in, pool_w=v_pool_w, pool_scale=v_pool_scale, sgu_ln_g=v_sgu_ln_g,
                sgu_ln_b=v_sgu_ln_b, sgu_w=v_sgu_w, sgu_b=v_sgu_b, mem_norm=v_mem_norm, w_kv=v_w_kv,
                branch_norm=v_branch_norm, w_out=v_w_out, norm_post=v_norm_post)

    s, d = x.shape[1], x.shape[2]
    x2, mem2, tgt2 = x[0], mem[0], loss_target[0]
    t_branch = min(256, s)
    tm = min(512, s)

    core = lax.axis_index("c")
    chip = 2 * lax.axis_index("x") + lax.axis_index("y")
    pos = jnp.stack([core, chip]).astype(jnp.int32)
    n_in, n_kv, n_out = 4 * w_in.shape[2], 4 * w_kv.shape[1], 4 * w_out.shape[1]
    wi_rows, kv_rows, wo_rows = d // 8, n_kv // 8, n_out // 8

    def placed(shard, full_shape, block, grid, in_map, out_map, name):
        return _blockwise(_cast_copy, pos, [shard], [pl.BlockSpec(block, in_map)],
                          jax.ShapeDtypeStruct(full_shape, BF16), pl.BlockSpec(block, out_map), grid, name)

    kv_cols, pw_rows = w_kv.shape[2], GROUP // 8
    (wi_full,) = _run_rider(_gather_rider([
        placed(w_in[0], (d, n_in), (wi_rows, n_in // 4), (8,), lambda i, p: (i, 0), lambda i, p: (i, p[1]),
               "place_w_in")], [0]), "gather_w_in")
    gather_rest = _gather_rider([
        placed(w_kv[0], (n_kv, kv_cols), (kv_rows, kv_cols), (2,), lambda i, p: (i, 0),
               lambda i, p: (2 * p[1] + i, 0), "place_w_kv"),
        placed(w_out[0], (n_out, d), (wo_rows, d), (2,), lambda i, p: (i, 0), lambda i, p: (2 * p[1] + i, 0),
               "place_w_out"),
        placed(pool_w[0], (4, GROUP, GROUP), (4, GROUP // 4, GROUP), (1,), lambda i, p: (0, 0, 0),
               lambda i, p: (0, p[1], 0), "place_pool_w"),
    ], [1, 2, 3])

    mem_g = mem_norm.reshape(1, d)
    h = _rms_pre(x2, norm_pre, tm)
    proj, wkv_full, wo_full, pw_full = _matmul(h, wi_full, NN, F32, min(1024, s), 512, d, "proj", gather_rest)
    k_m, v_m = _kv_fwd(mem2, mem_g, wkv_full)
    bias_full = jnp.repeat(sgu_b[0].T, CHUNK, axis=1)
    y = _branches_fwd(proj, pw_full, pool_scale, sgu_ln_g, sgu_ln_b, sgu_w[0], bias_full, k_m, v_m, branch_norm,
                      t_branch)
    loss_local, dz, dout, dy, g_norm_post = _out_loss(y, wo_full, x2, tgt2, norm_post, min(256, s))

    def pair_sums(views):
        theirs = _exchange_halves([(v[0], v[1], v[2]) for v in views], "exchange_for_" + views[0][9])
        return [_blockwise(_pair_sum, pos, [v[0], th], [pl.BlockSpec(v[3], v[4][0]), pl.BlockSpec(v[3], v[4][1])],
                           jax.ShapeDtypeStruct(v[5], BF16), pl.BlockSpec(v[6], v[7]), v[8], v[9])
                for v, th in zip(views, theirs)]

    (g_wo,) = _matmul(y, dout, TN, BF16, 1536, 1024, min(512, s), "grad_w_out")
    (ps_wo,) = pair_sums([
        (g_wo.reshape(4, 2, wo_rows, d), 1, [(0, 4), (2, 2)], (1, 1, wo_rows, d),
         (lambda i, p: (i, p[0], 0, 0), lambda i, p: (i, 0, 0, 0)), (4, wo_rows, d), (1, wo_rows, d),
         lambda i, p: (i, 0, 0), (4,), "pair_sum_w_out")])
    (dproj, g_pw, g_pool_scale, g_ln_g, g_ln_b, g_sgu_w, g_sgu_b_t, g_branch_norm, dk, dv, landed_wo) = _branches_bwd(
        proj, dy, pw_full, pool_scale, sgu_ln_g, sgu_ln_b, sgu_w[0], jnp.swapaxes(sgu_w[0], 1, 2), bias_full,
        k_m, v_m, branch_norm, t_branch, _scatter_rider([(ps_wo, 0, [(1, 2)])]))
    g_wkv, g_mem_norm = _kv_bwd(mem2, mem_g, wkv_full, dk, dv)
    ps_kv, ps_pw = pair_sums([
        (g_wkv.reshape(4, 2, kv_rows, kv_cols), 1, [(0, 4), (2, 2)], (1, 1, kv_rows, kv_cols),
         (lambda i, p: (i, p[0], 0, 0), lambda i, p: (i, 0, 0, 0)), (4, kv_rows, kv_cols), (1, kv_rows, kv_cols),
         lambda i, p: (i, 0, 0), (4,), "pair_sum_w_kv"),
        (g_pw.astype(BF16).reshape(4, 4, 2, pw_rows, GROUP), 2, [(0, 4)], (1, 4, 1, pw_rows, GROUP),
         (lambda i, p: (i, 0, p[0], 0, 0), lambda i, p: (i, 0, 0, 0, 0)), (4, 4, pw_rows, GROUP),
         (1, 4, pw_rows, GROUP), lambda i, p: (i, 0, 0, 0), (4,), "pair_sum_pool_w")])
    g_wi, landed_kv, landed_pw = _matmul(h, dproj, TN, BF16, d, 1024, min(512, s), "grad_w_in",
                                         _scatter_rider([(ps_kv, 0, [(1, 2)]), (ps_pw, 1, [])]))
    (ps_wi,) = pair_sums([
        (g_wi.reshape(2, d // 2, n_in), 0, [(1, 16)], (1, 128, n_in),
         (lambda i, p: (p[0], i, 0), lambda i, p: (0, i, 0)), (d // 2, n_in), (128, n_in), lambda i, p: (i, 0),
         (d // 2 // 128,), "pair_sum_w_in")])
    grad_x, g_norm_pre, landed_wi = _dx_call(dproj, wi_full, x2, dz, norm_pre, tm, 1024,
                                             _scatter_rider([(ps_wi, 1, [(0, 4)])]))
    psum = [ps_wi, ps_kv, ps_wo, ps_pw]
    landed = [landed_wi, landed_kv, landed_wo, landed_pw]
    from_chip = lambda spec_shape, rank: [
        pl.BlockSpec(spec_shape, functools.partial(lambda i, p, q: (q, i) + (0,) * (rank - 2), q=q))
        for q in range(3)]
    joined = _join_halves([
        (_blockwise(_four_sum, pos, [psum[0]] + [landed[0]] * 3,
                    [pl.BlockSpec((256, n_in // 4), lambda i, p: (i, p[1]))] + from_chip((1, 256, n_in // 4), 3),
                    jax.ShapeDtypeStruct((2, d // 2, n_in // 4), F32),
                    pl.BlockSpec((1, 256, n_in // 4), lambda i, p: (p[0], i, 0)), (d // 2 // 256,), "chip_sum_w_in"),
         0, [(1, 8)]),
        (_blockwise(_four_sum, pos, [psum[1]] + [landed[1]] * 3,
                    [pl.BlockSpec((1, kv_rows, kv_cols), lambda i, p: (p[1], 0, 0))]
                    + from_chip((1, 1, kv_rows, kv_cols), 4),
                    jax.ShapeDtypeStruct((2, kv_rows, kv_cols), F32),
                    pl.BlockSpec((1, kv_rows, kv_cols), lambda i, p: (p[0], 0, 0)), (1,), "chip_sum_w_kv"),
         0, [(1, 2)]),
        (_blockwise(_four_sum, pos, [psum[2]] + [landed[2]] * 3,
                    [pl.BlockSpec((1, wo_rows, d), lambda i, p: (p[1], 0, 0))] + from_chip((1, 1, wo_rows, d), 4),
                    jax.ShapeDtypeStruct((2, wo_rows, d), F32),
                    pl.BlockSpec((1, wo_rows, d), lambda i, p: (p[0], 0, 0)), (1,), "chip_sum_w_out"),
         0, [(1, 2)]),
        (_blockwise(_four_sum, pos, [psum[3]] + [landed[3]] * 3,
                    [pl.BlockSpec((4, 1, pw_rows, GROUP), lambda i, p: (0, p[1], 0, 0))]
                    + from_chip((1, 4, 1, pw_rows, GROUP), 5),
                    jax.ShapeDtypeStruct((4, 2, pw_rows, GROUP), F32),
                    pl.BlockSpec((4, 1, pw_rows, GROUP), lambda i, p: (0, p[0], 0, 0)), (1,), "chip_sum_pool_w"),
         1, []),
    ])
    grads = {"w_in": joined[0].reshape(w_in.shape), "w_kv": joined[1].reshape(w_kv.shape),
             "w_out": joined[2].reshape(w_out.shape), "pool_w": joined[3].reshape(pool_w.shape)}

    small_local = dict(norm_pre=g_norm_pre, pool_scale=g_pool_scale, sgu_ln_g=g_ln_g, sgu_ln_b=g_ln_b,
                       sgu_w=g_sgu_w, sgu_b=g_sgu_b_t.T, mem_norm=g_mem_norm, branch_norm=g_branch_norm,
                       norm_post=g_norm_post)
    small_sum = _allreduce_small(_pack([small_local[n] for n in SMALL]))
    for n, g in zip(SMALL, _unpack(small_sum, [weights[n] for n in SMALL])):
        grads[n] = g

    delta, new_m, new_v = {}, {}, {}
    packed = [small_sum if src is grads else _pack([src[n] for n in SMALL]) for src in (weights, grads, mom1, mom2)]
    outs = _elementwise(_adamw, packed, [F32, F32, F32], "adamw_small")
    for dst, o in zip((delta, new_m, new_v), outs):
        for n, a in zip(SMALL, _unpack(o, [weights[n] for n in SMALL])):
            dst[n] = a
    for n in LARGE:
        cols = weights[n].shape[-1]
        outs = _elementwise(_adamw, [src[n].reshape(-1, cols) for src in (weights, grads, mom1, mom2)],
                            [F32, F32, F32], "adamw_" + n)
        for dst, o in zip((delta, new_m, new_v), outs):
            dst[n] = o.reshape(weights[n].shape)

    loss = lax.psum(loss_local[0, 0], ("x", "y", "c"))
    return (loss, grad_x[None], *[grads[n] for n in ORDER], *[delta[n] for n in ORDER],
            *[new_m[n] for n in ORDER], *[new_v[n] for n in ORDER])
```

```python
import functools

import jax
import jax.numpy as jnp
from jax import lax
from jax.experimental import pallas as pl
from jax.experimental.pallas import tpu as pltpu

F32 = jnp.float32
BF16 = jnp.bfloat16
EPS = 1e-6
MESH = pl.DeviceIdType.MESH
ANY = pl.BlockSpec(memory_space=pl.ANY)

POOL_WINDOWS = (2, 4, 8, 16)
GROUP = 256
HALO = 16
CHUNK = 128
N_SGU_HEADS = 8
N_ATT_HEADS = 4
ATT_DIM = 256
WIDTH = 1024
ATT_SCALE = 1.0 / 16.0

ADAM_LR = 0.001
ADAM_B1 = 0.9
ADAM_B2 = 0.999
ADAM_EPS = 1e-08
ADAM_WD = 0.01
ADAM_STEP = 10

VMEM_LIMIT = 60 * 1024 * 1024


def _params(n_grid_axes, vmem=VMEM_LIMIT):
    return pltpu.CompilerParams(dimension_semantics=("arbitrary",) * n_grid_axes, vmem_limit_bytes=vmem)


def _dot(a, b, dims):
    return lax.dot_general(a, b, (dims, ((), ())), preferred_element_type=F32)


NN = ((1,), (0,))
NT = ((1,), (1,))
TN = ((0,), (0,))


class _Rider:
    def __init__(self, inputs, out_shapes, n_sems, run, aliases=None):
        self.inputs, self.out_shapes, self.n_sems, self.run = list(inputs), list(out_shapes), n_sems, run
        self.aliases = aliases or {}


def _call(body, name, grid, in_specs, out_specs, out_shape, scratch_shapes, inputs, rider=None):
    n_in, n_out, n_scr = len(in_specs), len(out_specs), len(scratch_shapes)
    if rider is None:
        outs = pl.pallas_call(body, name=name, grid=grid, in_specs=in_specs, out_specs=tuple(out_specs),
                              out_shape=tuple(out_shape), scratch_shapes=scratch_shapes,
                              compiler_params=_params(len(grid)))(*inputs)
        return tuple(outs)
    r_in, r_out = len(rider.inputs), len(rider.out_shapes)

    def body_with_rider(*refs):
        ins, rider_ins = refs[:n_in], refs[n_in:n_in + r_in]
        refs = refs[n_in + r_in:]
        outs, rider_outs = refs[:n_out], refs[n_out:n_out + r_out]
        refs = refs[n_out + r_out:]
        scratch, (send_sems, recv_sems) = refs[:n_scr], refs[n_scr:]
        ids = [pl.program_id(ax) for ax in range(len(grid))]
        first = functools.reduce(lambda p, q: p & q, [i == 0 for i in ids])
        last = functools.reduce(lambda p, q: p & q, [i == g - 1 for i, g in zip(ids, grid)])

        @pl.when(first)
        def _():
            rider.run(rider_ins, rider_outs, send_sems, recv_sems, True)

        body(*ins, *outs, *scratch)

        @pl.when(last)
        def _():
            rider.run(rider_ins, rider_outs, send_sems, recv_sems, False)

    outs = pl.pallas_call(
        body_with_rider, name=name, grid=grid,
        in_specs=list(in_specs) + [ANY] * r_in, out_specs=tuple(out_specs) + (ANY,) * r_out,
        out_shape=tuple(out_shape) + tuple(rider.out_shapes),
        scratch_shapes=list(scratch_shapes) + [pltpu.SemaphoreType.DMA((rider.n_sems,)),
                                               pltpu.SemaphoreType.DMA((rider.n_sems,))],
        input_output_aliases={n_in + i: n_out + o for i, o in rider.aliases.items()},
        compiler_params=_params(len(grid)),
    )(*inputs, *rider.inputs)
    return tuple(outs)


def _matmul(a, b, dims, out_dtype, tm, tn, tk, name, rider=None):
    if dims == NN:
        (m, k), n = a.shape, b.shape[1]
        a_spec = pl.BlockSpec((tm, tk), lambda i, j, kk: (i, kk))
        b_spec = pl.BlockSpec((tk, tn), lambda i, j, kk: (kk, j))
    else:
        (k, m), n = a.shape, b.shape[1]
        a_spec = pl.BlockSpec((tk, tm), lambda i, j, kk: (kk, i))
        b_spec = pl.BlockSpec((tk, tn), lambda i, j, kk: (kk, j))
    nk = k // tk

    def body(a_ref, b_ref, o_ref, *acc):
        part = lambda: _dot(a_ref[...], b_ref[...], dims)
        if nk == 1:
            o_ref[...] = part().astype(out_dtype)
            return
        (acc_ref,) = acc
        kk = pl.program_id(2)

        @pl.when(kk == 0)
        def _():
            acc_ref[...] = part()

        @pl.when((kk > 0) & (kk < nk - 1))
        def _():
            acc_ref[...] += part()

        @pl.when(kk == nk - 1)
        def _():
            o_ref[...] = (acc_ref[...] + part()).astype(out_dtype)

    return _call(body, name, (m // tm, n // tn, nk), [a_spec, b_spec],
                 [pl.BlockSpec((tm, tn), lambda i, j, kk: (i, j))], [jax.ShapeDtypeStruct((m, n), out_dtype)],
                 [pltpu.VMEM((tm, tn), F32)] if nk > 1 else [], [a, b], rider)


def _rms_pre(x, g, tm):
    s, d = x.shape

    def body(x_ref, g_ref, h_ref):
        xv = x_ref[...]
        r = lax.rsqrt(jnp.mean(xv * xv, axis=-1, keepdims=True) + EPS)
        h_ref[...] = (xv * r * g_ref[...]).astype(BF16)

    return pl.pallas_call(
        body, name="rms_pre", grid=(s // tm,),
        in_specs=[pl.BlockSpec((tm, d), lambda i: (i, 0)), pl.BlockSpec((1, d), lambda i: (0, 0))],
        out_specs=pl.BlockSpec((tm, d), lambda i: (i, 0)),
        out_shape=jax.ShapeDtypeStruct((s, d), BF16),
        compiler_params=_params(1),
    )(x, g)


def _kv_fwd(mem, g, w_kv):
    m, d = mem.shape

    def body(mem_ref, g_ref, w_ref, k_ref, v_ref):
        mv = mem_ref[...]
        r = lax.rsqrt(jnp.mean(mv * mv, axis=-1, keepdims=True) + EPS)
        mem_n = (mv * r * g_ref[...]).astype(BF16)
        kv = _dot(mem_n, w_ref[...], NN)
        k_ref[...] = kv[:, :WIDTH].astype(BF16)
        v_ref[...] = kv[:, WIDTH:].astype(BF16)

    return pl.pallas_call(
        body, name="kv_fwd",
        out_shape=(jax.ShapeDtypeStruct((m, WIDTH), BF16), jax.ShapeDtypeStruct((m, WIDTH), BF16)),
        compiler_params=_params(0),
    )(mem, g, w_kv)


def _kv_bwd(mem, g, w_kv, dk, dv):
    m, d = mem.shape
    n = w_kv.shape[1]
    col = 512

    def body(mem_ref, g_ref, w_ref, dk_ref, dv_ref, dw_ref, dg_ref):
        mv = mem_ref[...]
        r = lax.rsqrt(jnp.mean(mv * mv, axis=-1, keepdims=True) + EPS)
        mem_hat = mv * r
        mem_n = (mem_hat * g_ref[...]).astype(BF16)
        dkv = jnp.concatenate([dk_ref[...], dv_ref[...]], axis=1).astype(BF16)
        for j in range(n // col):
            dw_ref[:, j * col:(j + 1) * col] = _dot(mem_n, dkv[:, j * col:(j + 1) * col], TN).astype(BF16)
        dmem_n = _dot(dkv, w_ref[...], NT)
        dg_ref[...] = jnp.sum(dmem_n * mem_hat, axis=0, keepdims=True)

    return pl.pallas_call(
        body, name="kv_bwd",
        out_shape=(jax.ShapeDtypeStruct((d, n), BF16), jax.ShapeDtypeStruct((1, d), F32)),
        compiler_params=_params(0),
    )(mem, g, w_kv, dk, dv)


def _sigmoid(x):
    return 1.0 / (1.0 + jnp.exp(-x))


def _inv_counts(t0, t):
    pos = (t0 + lax.broadcasted_iota(jnp.int32, (t, 1), 0) + 1).astype(F32)
    return [1.0 / jnp.minimum(pos, float(w)) for w in POOL_WINDOWS]


def _window_sums(ext, t, backward):
    n = t + HALO
    parts = []
    for gi, w in enumerate(POOL_WINDOWS):
        s = ext[:, gi * GROUP:(gi + 1) * GROUP]
        k = 1
        while k < w:
            s = s + pltpu.roll(s, (n - k) if backward else k, axis=0)
            k *= 2
        parts.append(s[:t] if backward else s[HALO:])
    return parts


def _pool_fwd(xa, halo, inv, pool_w):
    t = xa.shape[0]
    sums = _window_sums(jnp.concatenate([halo, xa], axis=0), t, backward=False)
    d = jnp.concatenate([sums[gi] * inv[gi] - xa[:, gi * GROUP:(gi + 1) * GROUP] for gi in range(4)], axis=1)
    d = d.astype(BF16)
    y = jnp.concatenate([_dot(d[:, gi * GROUP:(gi + 1) * GROUP], pool_w[gi], NN) for gi in range(4)], axis=1)
    return d, y


def _layernorm_fwd(v):
    mu = jnp.mean(v, axis=-1, keepdims=True)
    xc = v - mu
    rstd = lax.rsqrt(jnp.mean(xc * xc, axis=-1, keepdims=True) + EPS)
    return xc * rstd, rstd


def _tril_mask(transposed):
    r = lax.broadcasted_iota(jnp.int32, (CHUNK, CHUNK), 0)
    c = lax.broadcasted_iota(jnp.int32, (CHUNK, CHUNK), 1)
    return (r <= c) if transposed else (r >= c)


def _sgu_mix(w_ref, vals, transposed):
    t = vals.shape[0]
    mask = _tril_mask(transposed)
    ws = [jnp.where(mask, w_ref[h], 0.0).astype(BF16) for h in range(N_SGU_HEADS)]
    rows = []
    for ci in range(t // CHUNK):
        blk = vals[ci * CHUNK:(ci + 1) * CHUNK]
        rows.append(jnp.concatenate(
            [_dot(ws[h], blk[:, h * CHUNK:(h + 1) * CHUNK], NN) for h in range(N_SGU_HEADS)], axis=1))
    return jnp.concatenate(rows, axis=0)


def _attn_fwd(q, k, v):
    ps, os_ = [], []
    for h in range(N_ATT_HEADS):
        sl = slice(h * ATT_DIM, (h + 1) * ATT_DIM)
        s = _dot(q[:, sl], k[:, sl], NT) * ATT_SCALE
        s = s - jnp.max(s, axis=-1, keepdims=True)
        e = jnp.exp(s)
        p = e / jnp.sum(e, axis=-1, keepdims=True)
        ps.append(p)
        os_.append(_dot(p.astype(BF16), v[:, sl], NN))
    return ps, jnp.concatenate(os_, axis=1)


def _rms_branch(y_pre):
    r = lax.rsqrt(jnp.mean(y_pre * y_pre, axis=-1, keepdims=True) + EPS)
    return y_pre * r, r


def _branch_specs(t, n_tiles, order):
    width_in = 7 * WIDTH
    tile = lambda i: order(i)
    per_halo = t // HALO
    const2 = lambda i: (0, 0)
    const3 = lambda i: (0, 0, 0)
    return [
        pl.BlockSpec((t, width_in), lambda i: (tile(i), 0)),
        pl.BlockSpec((HALO, WIDTH), lambda i: (jnp.maximum(tile(i) * per_halo - 1, 0), 0)),
        pl.BlockSpec((4, GROUP, GROUP), const3),
        pl.BlockSpec((1, WIDTH), const2),
        pl.BlockSpec((1, WIDTH), const2),
        pl.BlockSpec((1, WIDTH), const2),
        pl.BlockSpec((N_SGU_HEADS, CHUNK, CHUNK), const3),
        pl.BlockSpec((CHUNK, WIDTH), const2),
        pl.BlockSpec((MEM_ROWS, WIDTH), const2),
        pl.BlockSpec((MEM_ROWS, WIDTH), const2),
        pl.BlockSpec((1, 3 * WIDTH), const2),
    ]


MEM_ROWS = 256


def _branches_fwd(proj, pool_w, pool_scale, ln_g, ln_b, sgu_w, bias_full, k, v, branch_norm, t):
    s = proj.shape[0]
    n_tiles = s // t

    def body(proj_ref, halo_ref, pw_ref, ps_ref, lg_ref, lb_ref, sw_ref, sb_ref, k_ref, v_ref, bn_ref, y_ref):
        i = pl.program_id(0)
        col = lambda j: proj_ref[:, j * WIDTH:(j + 1) * WIDTH]
        bn = bn_ref[...]
        halo = jnp.where(i > 0, halo_ref[...], 0.0)
        _, y_pool = _pool_fwd(col(0), halo, _inv_counts(i * t, t), pw_ref[...])
        ga = col(1)
        ya = y_pool * ps_ref[...] * (ga * _sigmoid(ga))
        y_ref[:, 0:WIDTH] = (_rms_branch(ya)[0] * bn[:, 0:WIDTH]).astype(BF16)
        vhat, _ = _layernorm_fwd(col(3))
        vn = (vhat * lg_ref[...] + lb_ref[...]).astype(BF16)
        z = _sgu_mix(sw_ref, vn, transposed=False) + jnp.tile(sb_ref[...], (t // CHUNK, 1))
        gb = col(4)
        yb = col(2) * z * (gb * _sigmoid(gb))
        y_ref[:, WIDTH:2 * WIDTH] = (_rms_branch(yb)[0] * bn[:, WIDTH:2 * WIDTH]).astype(BF16)
        _, o = _attn_fwd(col(5).astype(BF16), k_ref[...], v_ref[...])
        gc = col(6)
        yc = o * (gc * _sigmoid(gc))
        y_ref[:, 2 * WIDTH:] = (_rms_branch(yc)[0] * bn[:, 2 * WIDTH:]).astype(BF16)

    return pl.pallas_call(
        body, name="branches_fwd", grid=(n_tiles,),
        in_specs=_branch_specs(t, n_tiles, lambda i: i),
        out_specs=pl.BlockSpec((t, 3 * WIDTH), lambda i: (i, 0)),
        out_shape=jax.ShapeDtypeStruct((s, 3 * WIDTH), BF16),
        compiler_params=_params(1),
    )(proj, proj, pool_w, pool_scale, ln_g, ln_b, sgu_w, bias_full, k, v, branch_norm)


def _branches_bwd(proj, dy, pool_w, pool_scale, ln_g, ln_b, sgu_w, sgu_wt, bias_full, k, v, branch_norm, t, rider=None):
    s = proj.shape[0]
    n_tiles = s // t
    n_chunks = t // CHUNK
    order = lambda i: n_tiles - 1 - i

    def body(proj_ref, halo_ref, pw_ref, ps_ref, lg_ref, lb_ref, sw_ref, sb_ref, k_ref, v_ref, bn_ref,
             swt_ref, dy_ref,
             dproj_ref, dpw_ref, dps_ref, dlg_ref, dlb_ref, dsw_ref, dsb_ref, dbn_ref, dk_ref, dv_ref,
             carry_ref, dbias_ref):
        step = pl.program_id(0)
        i = order(step)

        @pl.when(step == 0)
        def _():
            for ref in (dpw_ref, dps_ref, dlg_ref, dlb_ref, dsw_ref, dbn_ref, dk_ref, dv_ref, carry_ref, dbias_ref):
                ref[...] = jnp.zeros(ref.shape, ref.dtype)

        col = lambda j: proj_ref[:, j * WIDTH:(j + 1) * WIDTH]
        bn = bn_ref[...]

        def norm_bwd(y_pre, sl):
            yhat, r = _rms_branch(y_pre)
            dyv = dy_ref[:, sl].astype(F32)
            dbn_ref[:, sl] += jnp.sum(dyv * yhat, axis=0, keepdims=True)
            dyhat = dyv * bn[:, sl]
            return r * (dyhat - yhat * jnp.mean(dyhat * yhat, axis=-1, keepdims=True))

        def gate(gv):
            sg = _sigmoid(gv)
            return gv * sg, sg * (1.0 + gv * (1.0 - sg))

        inv = _inv_counts(i * t, t)
        halo = jnp.where(i > 0, halo_ref[...], 0.0)
        pw = pw_ref[...]
        d, y_pool = _pool_fwd(col(0), halo, inv, pw)
        scale = ps_ref[...]
        silu_a, dsilu_a = gate(col(1))
        pa = y_pool * scale
        dya = norm_bwd(pa * silu_a, slice(0, WIDTH))
        dproj_ref[:, WIDTH:2 * WIDTH] = (dya * pa * dsilu_a).astype(BF16)
        dpa = dya * silu_a
        dps_ref[...] += jnp.sum(dpa * y_pool, axis=0, keepdims=True)
        dy_pool = (dpa * scale).astype(BF16)
        dd_parts, ddc_parts = [], []
        for gi in range(4):
            sl = slice(gi * GROUP, (gi + 1) * GROUP)
            dpw_ref[gi] += _dot(d[:, sl], dy_pool[:, sl], TN)
            dd = _dot(dy_pool[:, sl], pw[gi], NT)
            dd_parts.append(dd)
            ddc_parts.append(dd * inv[gi])
        ddc = jnp.concatenate(ddc_parts, axis=1)
        sums = _window_sums(jnp.concatenate([ddc, carry_ref[...]], axis=0), t, backward=True)
        carry_ref[...] = ddc[:HALO]
        dproj_ref[:, 0:WIDTH] = jnp.concatenate([sums[gi] - dd_parts[gi] for gi in range(4)], axis=1).astype(BF16)

        vhat, rstd = _layernorm_fwd(col(3))
        lg = lg_ref[...]
        vn = (vhat * lg + lb_ref[...]).astype(BF16)
        z = _sgu_mix(sw_ref, vn, transposed=False) + jnp.tile(sb_ref[...], (n_chunks, 1))
        u = col(2)
        silu_b, dsilu_b = gate(col(4))
        uz = u * z
        dyb = norm_bwd(uz * silu_b, slice(WIDTH, 2 * WIDTH))
        dproj_ref[:, 4 * WIDTH:5 * WIDTH] = (dyb * uz * dsilu_b).astype(BF16)
        duz = dyb * silu_b
        dproj_ref[:, 2 * WIDTH:3 * WIDTH] = (duz * z).astype(BF16)
        dz = duz * u
        dz_b = dz.astype(BF16)
        for ci in range(n_chunks):
            rows = slice(ci * CHUNK, (ci + 1) * CHUNK)
            dbias_ref[...] += dz[rows]
            for h in range(N_SGU_HEADS):
                sl = slice(h * CHUNK, (h + 1) * CHUNK)
                dsw_ref[h] += _dot(dz_b[rows, sl], vn[rows, sl], NT)
        dvn = _sgu_mix(swt_ref, dz_b, transposed=True)
        dlg_ref[...] += jnp.sum(dvn * vhat, axis=0, keepdims=True)
        dlb_ref[...] += jnp.sum(dvn, axis=0, keepdims=True)
        dvhat = dvn * lg
        dvb = rstd * (dvhat - jnp.mean(dvhat, axis=-1, keepdims=True)
                      - vhat * jnp.mean(dvhat * vhat, axis=-1, keepdims=True))
        dproj_ref[:, 3 * WIDTH:4 * WIDTH] = dvb.astype(BF16)

        q = col(5).astype(BF16)
        kv_k, kv_v = k_ref[...], v_ref[...]
        ps, o = _attn_fwd(q, kv_k, kv_v)
        silu_c, dsilu_c = gate(col(6))
        dyc = norm_bwd(o * silu_c, slice(2 * WIDTH, 3 * WIDTH))
        dproj_ref[:, 6 * WIDTH:7 * WIDTH] = (dyc * o * dsilu_c).astype(BF16)
        do = (dyc * silu_c).astype(BF16)
        dq_parts = []
        for h in range(N_ATT_HEADS):
            sl = slice(h * ATT_DIM, (h + 1) * ATT_DIM)
            p = ps[h]
            dp = _dot(do[:, sl], kv_v[:, sl], NT)
            ds = (p * (dp - jnp.sum(p * dp, axis=-1, keepdims=True)) * ATT_SCALE).astype(BF16)
            dq_parts.append(_dot(ds, kv_k[:, sl], NN))
            dk_ref[:, sl] += _dot(ds, q[:, sl], TN)
            dv_ref[:, sl] += _dot(p.astype(BF16), do[:, sl], TN)
        dproj_ref[:, 5 * WIDTH:6 * WIDTH] = jnp.concatenate(dq_parts, axis=1).astype(BF16)

        @pl.when(step == n_tiles - 1)
        def _():
            keep = _tril_mask(transposed=False)
            for h in range(N_SGU_HEADS):
                dsw_ref[h] = jnp.where(keep, dsw_ref[h], 0.0)
            dsb_ref[...] = jnp.concatenate(
                [jnp.sum(dbias_ref[:, h * CHUNK:(h + 1) * CHUNK], axis=1, keepdims=True)
                 for h in range(N_SGU_HEADS)], axis=1)

    const2 = lambda i: (0, 0)
    const3 = lambda i: (0, 0, 0)
    out_shapes = (
        jax.ShapeDtypeStruct((s, 7 * WIDTH), BF16),
        jax.ShapeDtypeStruct((4, GROUP, GROUP), F32),
        jax.ShapeDtypeStruct((1, WIDTH), F32),
        jax.ShapeDtypeStruct((1, WIDTH), F32),
        jax.ShapeDtypeStruct((1, WIDTH), F32),
        jax.ShapeDtypeStruct((N_SGU_HEADS, CHUNK, CHUNK), F32),
        jax.ShapeDtypeStruct((CHUNK, N_SGU_HEADS), F32),
        jax.ShapeDtypeStruct((1, 3 * WIDTH), F32),
        jax.ShapeDtypeStruct((MEM_ROWS, WIDTH), F32),
        jax.ShapeDtypeStruct((MEM_ROWS, WIDTH), F32),
    )
    out_specs = (
        pl.BlockSpec((t, 7 * WIDTH), lambda i: (order(i), 0)),
        pl.BlockSpec((4, GROUP, GROUP), const3),
        pl.BlockSpec((1, WIDTH), const2),
        pl.BlockSpec((1, WIDTH), const2),
        pl.BlockSpec((1, WIDTH), const2),
        pl.BlockSpec((N_SGU_HEADS, CHUNK, CHUNK), const3),
        pl.BlockSpec((CHUNK, N_SGU_HEADS), const2),
        pl.BlockSpec((1, 3 * WIDTH), const2),
        pl.BlockSpec((MEM_ROWS, WIDTH), const2),
        pl.BlockSpec((MEM_ROWS, WIDTH), const2),
    )
    in_specs = _branch_specs(t, n_tiles, order) + [
        pl.BlockSpec((N_SGU_HEADS, CHUNK, CHUNK), const3),
        pl.BlockSpec((t, 3 * WIDTH), lambda i: (order(i), 0)),
    ]
    return _call(body, "branches_bwd", (n_tiles,), in_specs, out_specs, out_shapes,
                 [pltpu.VMEM((HALO, WIDTH), F32), pltpu.VMEM((CHUNK, WIDTH), F32)],
                 [proj, proj, pool_w, pool_scale, ln_g, ln_b, sgu_w, bias_full, k, v, branch_norm, sgu_wt, dy], rider)


def _out_loss(y, w_out, x, target, g_post, tm):
    s, d = x.shape
    e_w = y.shape[1]
    n_tiles = s // tm

    def body(y_ref, w_ref, x_ref, t_ref, g_ref, loss_ref, dz_ref, dout_ref, dy_ref, dg_ref, sq_ref):
        i = pl.program_id(0)

        @pl.when(i == 0)
        def _():
            sq_ref[...] = jnp.zeros(sq_ref.shape, F32)
            dg_ref[...] = jnp.zeros(dg_ref.shape, F32)

        w = w_ref[...]
        out = _dot(y_ref[...], w, NN)
        r = lax.rsqrt(jnp.mean(out * out, axis=-1, keepdims=True) + EPS)
        outn = out * r
        g = g_ref[...]
        err = (x_ref[...] + outn * g) - t_ref[...]
        sq_ref[...] += jnp.sum(err * err, axis=0, keepdims=True)
        dz = err * (1.0 / d)
        dz_ref[...] = dz
        dg_ref[...] += jnp.sum(dz * outn, axis=0, keepdims=True)
        doutn = dz * g
        dout = (r * (doutn - outn * jnp.mean(doutn * outn, axis=-1, keepdims=True))).astype(BF16)
        dout_ref[...] = dout
        dy_ref[...] = _dot(dout, w, NT).astype(BF16)

        @pl.when(i == n_tiles - 1)
        def _():
            loss_ref[...] = 0.5 * jnp.sum(sq_ref[...], axis=1, keepdims=True) * (1.0 / d)

    row = lambda i: (i, 0)
    const2 = lambda i: (0, 0)
    return pl.pallas_call(
        body, name="out_loss", grid=(n_tiles,),
        in_specs=[
            pl.BlockSpec((tm, e_w), row),
            pl.BlockSpec((e_w, d), const2, pipeline_mode=pl.Buffered(1)),
            pl.BlockSpec((tm, d), row),
            pl.BlockSpec((tm, d), row),
            pl.BlockSpec((1, d), const2),
        ],
        out_specs=(
            pl.BlockSpec((1, 1), const2),
            pl.BlockSpec((tm, d), row),
            pl.BlockSpec((tm, d), row),
            pl.BlockSpec((tm, e_w), row),
            pl.BlockSpec((1, d), const2),
        ),
        out_shape=(
            jax.ShapeDtypeStruct((1, 1), F32),
            jax.ShapeDtypeStruct((s, d), F32),
            jax.ShapeDtypeStruct((s, d), BF16),
            jax.ShapeDtypeStruct((s, e_w), BF16),
            jax.ShapeDtypeStruct((1, d), F32),
        ),
        scratch_shapes=[pltpu.VMEM((1, d), F32)],
        compiler_params=_params(1),
    )(y, w_out, x, target, g_post)


def _dx_call(dproj, w_in, x, dz, g_pre, tm, tk, rider=None):
    s, d = x.shape
    k_total = dproj.shape[1]
    nk = k_total // tk
    n_tiles = s // tm

    def body(dp_ref, w_ref, x_ref, dz_ref, g_ref, dx_ref, dg_ref, acc_ref):
        i, kk = pl.program_id(0), pl.program_id(1)
        part = lambda: _dot(dp_ref[...], w_ref[...], NT)

        @pl.when(kk == 0)
        def _():
            acc_ref[...] = part()

        @pl.when((kk > 0) & (kk < nk - 1))
        def _():
            acc_ref[...] += part()

        @pl.when((i == 0) & (kk == 0))
        def _():
            dg_ref[...] = jnp.zeros(dg_ref.shape, F32)

        @pl.when(kk == nk - 1)
        def _():
            dh = acc_ref[...] + part()
            xv = x_ref[...]
            r = lax.rsqrt(jnp.mean(xv * xv, axis=-1, keepdims=True) + EPS)
            xhat = xv * r
            dg_ref[...] += jnp.sum(dh * xhat, axis=0, keepdims=True)
            dxhat = dh * g_ref[...]
            dx_ref[...] = dz_ref[...] + r * (dxhat - xhat * jnp.mean(dxhat * xhat, axis=-1, keepdims=True))

    row = lambda i, kk: (i, 0)
    const2 = lambda i, kk: (0, 0)
    return _call(
        body, "dx", (n_tiles, nk),
        [
            pl.BlockSpec((tm, tk), lambda i, kk: (i, kk)),
            pl.BlockSpec((d, tk), lambda i, kk: (0, kk)),
            pl.BlockSpec((tm, d), row),
            pl.BlockSpec((tm, d), row),
            pl.BlockSpec((1, d), const2),
        ],
        [pl.BlockSpec((tm, d), row), pl.BlockSpec((1, d), const2)],
        [jax.ShapeDtypeStruct((s, d), F32), jax.ShapeDtypeStruct((1, d), F32)],
        [pltpu.VMEM((tm, d), F32)], [dproj, w_in, x, dz, g_pre], rider)


def _rows_tile(rows, cols, n_arrays, itemsize=4):
    budget = 24 * 1024 * 1024 // (2 * n_arrays * cols * itemsize)
    if rows <= budget:
        return rows
    best = None
    for cand in range(16, rows + 1, 16):
        if rows % cand == 0 and cand <= max(budget, 16):
            best = cand
    return best if best is not None else rows


def _elementwise(fn, inputs, out_dtypes, name):
    rows, cols = inputs[0].shape
    tr = _rows_tile(rows, cols, len(inputs) + len(out_dtypes))
    n_in = len(inputs)

    def body(*refs):
        outs = fn(*[r[...] for r in refs[:n_in]])
        for o_ref, o in zip(refs[n_in:], outs):
            o_ref[...] = o.astype(o_ref.dtype)

    spec = pl.BlockSpec((tr, cols), lambda i: (i, 0))
    return pl.pallas_call(
        body, name=name, grid=(rows // tr,),
        in_specs=[spec] * n_in, out_specs=tuple([spec] * len(out_dtypes)),
        out_shape=tuple(jax.ShapeDtypeStruct((rows, cols), dt) for dt in out_dtypes),
        compiler_params=_params(1),
    )(*inputs)


def _blockwise(fn, pos, inputs, in_specs, out_shape, out_spec, grid, name):
    n_in = len(inputs)

    def body(pos_ref, *refs):
        o_ref = refs[n_in]
        (out,) = fn(*[r[...].reshape(o_ref.shape) for r in refs[:n_in]])
        o_ref[...] = out.astype(o_ref.dtype)

    return pl.pallas_call(
        body, name=name,
        grid_spec=pltpu.PrefetchScalarGridSpec(num_scalar_prefetch=1, grid=grid, in_specs=in_specs,
                                               out_specs=out_spec),
        out_shape=out_shape,
        compiler_params=_params(len(grid)),
    )(pos, *inputs)


def _cast_copy(x):
    return (x,)


def _pair_sum(mine, theirs):
    return ((mine.astype(F32) + theirs.astype(F32)),)


def _four_sum(own, t0, t1, t2):
    return ((((own.astype(F32) + t0.astype(F32)) + t1.astype(F32)) + t2.astype(F32)),)


def _adamw(w, g, m, v):
    m = ADAM_B1 * m + (1.0 - ADAM_B1) * g
    v = ADAM_B2 * v + (1.0 - ADAM_B2) * jnp.square(g)
    m_hat = m / (1.0 - ADAM_B1 ** ADAM_STEP)
    v_hat = v / (1.0 - ADAM_B2 ** ADAM_STEP)
    delta = -ADAM_LR * (m_hat / (jnp.sqrt(v_hat) + ADAM_EPS) + ADAM_WD * w)
    return delta, m, v


def _place():
    x, y, c = lax.axis_index("x"), lax.axis_index("y"), lax.axis_index("c")
    chips = [(1 - x, y), (x, 1 - y), (1 - x, 1 - y)]
    return x, y, c, chips


def _remote(src, dst, send_sem, recv_sem, to):
    return pltpu.make_async_remote_copy(src_ref=src, dst_ref=dst, send_sem=send_sem, recv_sem=recv_sem,
                                        device_id=to, device_id_type=MESH)


def _split(ref, plan):
    views = [ref]
    for axis, parts in plan:
        size = ref.shape[axis] // parts
        assert size * parts == ref.shape[axis]
        views = [v.at[tuple(pl.ds(q * size, size) if i == axis else slice(None) for i in range(len(ref.shape)))]
                 for v in views for q in range(parts)]
    return views


def _remote_in_parts(src, dst, send_sem, recv_sem, to, plan):
    for s, d in zip(_split(src, plan), _split(dst, plan)):
        _remote(s, d, send_sem, recv_sem, to).start()
    return _remote(src, dst, send_sem, recv_sem, to)


def _local_in_parts(src, dst, sem, plan):
    for s, d in zip(_split(src, plan), _split(dst, plan)):
        pltpu.make_async_copy(s, d, sem).start()
    return pltpu.make_async_copy(src, dst, sem)


def _hbm_call(body, name, inputs, out_shapes, scratch, aliases=None):
    return pl.pallas_call(
        body, name=name,
        in_specs=[ANY] * len(inputs), out_specs=tuple([ANY] * len(out_shapes)), out_shape=tuple(out_shapes),
        scratch_shapes=scratch, input_output_aliases=aliases or {},
        compiler_params=pltpu.CompilerParams(has_side_effects=True),
    )(*inputs)


def _gather_rider(fulls, kinds):
    n = len(fulls)

    def full_half(a, ref, chip, cc):
        if a == 0:
            rows, cols = ref.shape[0] // 2, ref.shape[1] // 4
            return ref.at[pl.ds(cc * rows, rows), pl.ds(pl.multiple_of(chip * cols, 128), cols)]
        if a == 3:
            rows = ref.shape[1] // 8
            return ref.at[:, pl.ds(pl.multiple_of((2 * chip + cc) * rows, 16), rows), :]
        rows = ref.shape[0] // 8
        return ref.at[pl.ds(pl.multiple_of((2 * chip + cc) * rows, 16), rows), :]

    def run(in_refs, full_refs, send_sems, recv_sems, start):
        x, y, c, chips = _place()
        me = 2 * x + y
        sibling = (x, y, 1 - c)
        plans = [[(0, 8)], [(0, 2)], [(0, 2)], []]
        if start:
            for p, chip in enumerate(chips):
                for a in range(n):
                    mine = full_half(kinds[a], full_refs[a], me, c)
                    _remote_in_parts(mine, mine, send_sems.at[6 * a + p], recv_sems.at[6 * a + p], (*chip, c),
                                     plans[kinds[a]])
            return
        passed_on = []
        for p, chip in enumerate(chips):
            them = 2 * chip[0] + chip[1]
            for a in range(n):
                landed = full_half(kinds[a], full_refs[a], them, c)
                _remote(landed, landed, send_sems.at[6 * a + p], recv_sems.at[6 * a + p], (*chip, c)).wait_recv()
                passed_on.append(_remote_in_parts(landed, landed, send_sems.at[6 * a + 3 + p],
                                                  recv_sems.at[6 * a + 3 + p], sibling, plans[kinds[a]]))
        for p, chip in enumerate(chips):
            them = 2 * chip[0] + chip[1]
            for a in range(n):
                passed = full_half(kinds[a], full_refs[a], them, 1 - c)
                _remote(passed, passed, send_sems.at[6 * a + 3 + p], recv_sems.at[6 * a + 3 + p], sibling).wait_recv()
                mine = full_half(kinds[a], full_refs[a], me, c)
                _remote(mine, mine, send_sems.at[6 * a + p], recv_sems.at[6 * a + p], (*chip, c)).wait_send()
        for cp in passed_on:
            cp.wait_send()

    return _Rider(fulls, [jax.ShapeDtypeStruct(f.shape, f.dtype) for f in fulls], 6 * n, run,
                  aliases={a: a for a in range(n)})


def _run_rider(rider, name):
    r_in = len(rider.inputs)

    def body(*refs):
        in_refs, out_refs = refs[:r_in], refs[r_in:r_in + len(rider.out_shapes)]
        send_sems, recv_sems = refs[r_in + len(rider.out_shapes):]
        rider.run(in_refs, out_refs, send_sems, recv_sems, True)
        rider.run(in_refs, out_refs, send_sems, recv_sems, False)

    return _hbm_call(body, name, rider.inputs, rider.out_shapes,
                     [pltpu.SemaphoreType.DMA((rider.n_sems,)), pltpu.SemaphoreType.DMA((rider.n_sems,))],
                     aliases=rider.aliases)


def _exchange_halves(grads, name):
    n = len(grads)
    arrays = [g for g, _, _ in grads]
    out_shapes = [jax.ShapeDtypeStruct(tuple(1 if i == ax else dim for i, dim in enumerate(g.shape)), g.dtype)
                  for g, ax, _ in grads]

    def half(ref, ax, cc):
        idx = tuple(pl.ds(cc, 1) if i == ax else slice(None) for i in range(len(ref.shape)))
        return ref.at[idx]

    def body(*refs):
        in_refs, out_refs = refs[:n], refs[n:2 * n]
        send_sems, recv_sems = refs[2 * n:]
        x, y, c, _ = _place()
        sibling = (x, y, 1 - c)
        copies = [_remote_in_parts(half(in_refs[a], grads[a][1], 1 - c), out_refs[a], send_sems.at[a],
                                   recv_sems.at[a], sibling, grads[a][2]) for a in range(n)]
        for rem in copies:
            rem.wait()

    return _hbm_call(body, name, arrays, out_shapes,
                     [pltpu.SemaphoreType.DMA((n,)), pltpu.SemaphoreType.DMA((n,))])


def _scatter_rider(parts):
    n = len(parts)
    arrays = [p for p, _, _ in parts]

    def block_shape(p, ax):
        if ax == len(p.shape) - 1:
            return p.shape[:-1] + (p.shape[-1] // 4,)
        return tuple(1 if i == ax else dim for i, dim in enumerate(p.shape))

    out_shapes = [jax.ShapeDtypeStruct((3,) + block_shape(p, ax), p.dtype) for p, ax, _ in parts]

    def block(ref, ax, chip):
        rank = len(ref.shape)
        if ax == rank - 1:
            cols = ref.shape[-1] // 4
            last = pl.ds(pl.multiple_of(chip * cols, 128), cols)
            return ref.at[tuple([slice(None)] * (rank - 1) + [last])]
        return ref.at[tuple(pl.ds(chip, 1) if i == ax else slice(None) for i in range(rank))]

    def run(in_refs, out_refs, send_sems, recv_sems, start):
        x, y, c, chips = _place()
        for a in range(n):
            ax, plan = parts[a][1], parts[a][2]
            for p, chip in enumerate(chips):
                src, dst = block(in_refs[a], ax, 2 * chip[0] + chip[1]), out_refs[a].at[p]
                sems = (send_sems.at[3 * a + p], recv_sems.at[3 * a + p])
                if start:
                    _remote_in_parts(src, dst, *sems, (*chip, c), plan)
                else:
                    _remote(src, dst, *sems, (*chip, c)).wait()

    return _Rider(arrays, out_shapes, 3 * n, run)


def _join_halves(joined):
    n = len(joined)
    arrays = [j for j, _, _ in joined]

    def body(*refs):
        out_refs = refs[n:2 * n]
        send_sems, recv_sems = refs[2 * n:]
        x, y, c, _ = _place()
        sibling = (x, y, 1 - c)

        def half(a, cc):
            rank = len(out_refs[a].shape)
            return out_refs[a].at[tuple(pl.ds(cc, 1) if i == joined[a][1] else slice(None) for i in range(rank))]

        sends = [_remote_in_parts(half(a, c), half(a, c), send_sems.at[a], recv_sems.at[a], sibling, joined[a][2])
                 for a in range(n)]
        for a, rem in enumerate(sends):
            rem.wait_send()
            _remote(half(a, 1 - c), half(a, 1 - c), send_sems.at[a], recv_sems.at[a], sibling).wait_recv()

    return _hbm_call(body, "join_halves", arrays, [jax.ShapeDtypeStruct(j.shape, j.dtype) for j in arrays],
                     [pltpu.SemaphoreType.DMA((n,)), pltpu.SemaphoreType.DMA((n,))],
                     aliases={a: a for a in range(n)})


def _allreduce_small(packed):
    rows, lanes = packed.shape
    half = rows // 2

    def body(in_ref, out_ref, pair_ref, gath_ref, send_sems, recv_sems):
        x, y, c, chips = _place()
        me = 2 * x + y
        sibling = (x, y, 1 - c)
        mine = pl.ds(pl.multiple_of(c * half, 8), half)
        theirs = pl.ds(pl.multiple_of((1 - c) * half, 8), half)
        to_sib = _remote(in_ref.at[theirs], pair_ref, send_sems.at[0], recv_sems.at[0], sibling)
        to_sib.start()
        to_sib.wait()
        gath_ref[me] = in_ref[mine] + pair_ref[...]
        sends = [_remote(gath_ref.at[me], gath_ref.at[me], send_sems.at[1 + p], recv_sems.at[1 + p], (*chip, c))
                 for p, chip in enumerate(chips)]
        for cp in sends:
            cp.start()
        for p, chip in enumerate(chips):
            slot = gath_ref.at[2 * chip[0] + chip[1]]
            _remote(slot, slot, send_sems.at[1 + p], recv_sems.at[1 + p], (*chip, c)).wait_recv()
        for cp in sends:
            cp.wait_send()
        out_ref[mine] = ((gath_ref[0] + gath_ref[1]) + gath_ref[2]) + gath_ref[3]
        back = _remote(out_ref.at[mine], out_ref.at[mine], send_sems.at[4], recv_sems.at[4], sibling)
        back.start()
        back.wait_send()
        _remote(out_ref.at[theirs], out_ref.at[theirs], send_sems.at[4], recv_sems.at[4], sibling).wait_recv()

    vmem = pl.BlockSpec(memory_space=pltpu.VMEM)
    return pl.pallas_call(
        body, name="allreduce_small",
        in_specs=[vmem], out_specs=vmem, out_shape=jax.ShapeDtypeStruct((rows, lanes), F32),
        scratch_shapes=[pltpu.VMEM((half, lanes), F32), pltpu.VMEM((4, half, lanes), F32),
                        pltpu.SemaphoreType.DMA((5,)), pltpu.SemaphoreType.DMA((5,))],
        compiler_params=pltpu.CompilerParams(has_side_effects=True, vmem_limit_bytes=32 * 1024 * 1024),
    )(packed)


SMALL = ("norm_pre", "pool_scale", "sgu_ln_g", "sgu_ln_b", "sgu_w", "sgu_b", "mem_norm", "branch_norm", "norm_post")
LARGE = ("w_in", "pool_w", "w_kv", "w_out")
ORDER = ("norm_pre", "w_in", "pool_w", "pool_scale", "sgu_ln_g", "sgu_ln_b", "sgu_w", "sgu_b", "mem_norm", "w_kv",
         "branch_norm", "w_out", "norm_post")


def _pack(arrays):
    rows = [a.reshape(-1, 128) for a in arrays]
    pad = -sum(r.shape[0] for r in rows) % 16
    return jnp.concatenate(rows + ([jnp.zeros((pad, 128), F32)] if pad else []), axis=0)


def _unpack(packed, like):
    out, row = [], 0
    for a in like:
        rows = a.size // 128
        out.append(packed[row:row + rows].reshape(a.shape))
        row += rows
    return out


def kernel(x, mem, norm_pre, w_in, pool_w, pool_scale, sgu_ln_g, sgu_ln_b, sgu_w, sgu_b, mem_norm, w_kv, branch_norm, w_out, norm_post, loss_target, m_norm_pre, m_w_in, m_pool_w, m_pool_scale, m_sgu_ln_g, m_sgu_ln_b, m_sgu_w, m_sgu_b, m_mem_norm, m_w_kv, m_branch_norm, m_w_out, m_norm_post, v_norm_pre, v_w_in, v_pool_w, v_pool_scale, v_sgu_ln_g, v_sgu_ln_b, v_sgu_w, v_sgu_b, v_mem_norm, v_w_kv, v_branch_norm, v_w_out, v_norm_post):
    weights = dict(norm_pre=norm_pre, w_in=w_in, pool_w=pool_w, pool_scale=pool_scale, sgu_ln_g=sgu_ln_g,
                   sgu_ln_b=sgu_ln_b, sgu_w=sgu_w, sgu_b=sgu_b, mem_norm=mem_norm, w_kv=w_kv, branch_norm=branch_norm,
                   w_out=w_out, norm_post=norm_post)
    mom1 = dict(norm_pre=m_norm_pre, w_in=m_w_in, pool_w=m_pool_w, pool_scale=m_pool_scale, sgu_ln_g=m_sgu_ln_g,
                sgu_ln_b=m_sgu_ln_b, sgu_w=m_sgu_w, sgu_b=m_sgu_b, mem_norm=m_mem_norm, w_kv=m_w_kv,
                branch_norm=m_branch_norm, w_out=m_w_out, norm_post=m_norm_post)
    mom2 = dict(norm_pre=v_norm_pre, w_in=v_w_in, pool_w=v_pool_w, pool_scale=v_pool_scale, sgu_ln_g=v_sgu_ln_g,
                sgu_ln_b=v_sgu_ln_b, sgu_w=v_sgu_w, sgu_b=v_sgu_b, mem_norm=v_mem_norm, w_kv=v_w_kv,
                branch_norm=v_branch_norm, w_out=v_w_out, norm_post=v_norm_post)

    s, d = x.shape[1], x.shape[2]
    x2, mem2, tgt2 = x[0], mem[0], loss_target[0]
    t_branch = min(256, s)
    tm = min(512, s)

    core = lax.axis_index("c")
    chip = 2 * lax.axis_index("x") + lax.axis_index("y")
    pos = jnp.stack([core, chip]).astype(jnp.int32)
    n_in, n_kv, n_out = 4 * w_in.shape[2], 4 * w_kv.shape[1], 4 * w_out.shape[1]
    wi_rows, kv_rows, wo_rows = d // 8, n_kv // 8, n_out // 8

    def placed(shard, full_shape, block, grid, in_map, out_map, name):
        return _blockwise(_cast_copy, pos, [shard], [pl.BlockSpec(block, in_map)],
                          jax.ShapeDtypeStruct(full_shape, BF16), pl.BlockSpec(block, out_map), grid, name)

    kv_cols, pw_rows = w_kv.shape[2], GROUP // 8
    (wi_full,) = _run_rider(_gather_rider([
        placed(w_in[0], (d, n_in), (wi_rows, n_in // 4), (8,), lambda i, p: (i, 0), lambda i, p: (i, p[1]),
               "place_w_in")], [0]), "gather_w_in")
    gather_rest = _gather_rider([
        placed(w_kv[0], (n_kv, kv_cols), (kv_rows, kv_cols), (2,), lambda i, p: (i, 0),
               lambda i, p: (2 * p[1] + i, 0), "place_w_kv"),
        placed(w_out[0], (n_out, d), (wo_rows, d), (2,), lambda i, p: (i, 0), lambda i, p: (2 * p[1] + i, 0),
               "place_w_out"),
        placed(pool_w[0], (4, GROUP, GROUP), (4, GROUP // 4, GROUP), (1,), lambda i, p: (0, 0, 0),
               lambda i, p: (0, p[1], 0), "place_pool_w"),
    ], [1, 2, 3])

    mem_g = mem_norm.reshape(1, d)
    h = _rms_pre(x2, norm_pre, tm)
    proj, wkv_full, wo_full, pw_full = _matmul(h, wi_full, NN, F32, min(1024, s), 1024, d, "proj", gather_rest)
    k_m, v_m = _kv_fwd(mem2, mem_g, wkv_full)
    bias_full = jnp.repeat(sgu_b[0].T, CHUNK, axis=1)
    y = _branches_fwd(proj, pw_full, pool_scale, sgu_ln_g, sgu_ln_b, sgu_w[0], bias_full, k_m, v_m, branch_norm,
                      t_branch)
    loss_local, dz, dout, dy, g_norm_post = _out_loss(y, wo_full, x2, tgt2, norm_post, min(256, s))

    def pair_sums(views):
        theirs = _exchange_halves([(v[0], v[1], v[2]) for v in views], "exchange_for_" + views[0][9])
        return [_blockwise(_pair_sum, pos, [v[0], th], [pl.BlockSpec(v[3], v[4][0]), pl.BlockSpec(v[3], v[4][1])],
                           jax.ShapeDtypeStruct(v[5], BF16), pl.BlockSpec(v[6], v[7]), v[8], v[9])
                for v, th in zip(views, theirs)]

    (g_wo,) = _matmul(y, dout, TN, BF16, 1536, 1024, min(1024, s), "grad_w_out")
    (ps_wo,) = pair_sums([
        (g_wo.reshape(4, 2, wo_rows, d), 1, [(0, 4), (2, 2)], (1, 1, wo_rows, d),
         (lambda i, p: (i, p[0], 0, 0), lambda i, p: (i, 0, 0, 0)), (4, wo_rows, d), (1, wo_rows, d),
         lambda i, p: (i, 0, 0), (4,), "pair_sum_w_out")])
    (dproj, g_pw, g_pool_scale, g_ln_g, g_ln_b, g_sgu_w, g_sgu_b_t, g_branch_norm, dk, dv, landed_wo) = _branches_bwd(
        proj, dy, pw_full, pool_scale, sgu_ln_g, sgu_ln_b, sgu_w[0], jnp.swapaxes(sgu_w[0], 1, 2), bias_full,
        k_m, v_m, branch_norm, t_branch, _scatter_rider([(ps_wo, 0, [(1, 2)])]))
    g_wkv, g_mem_norm = _kv_bwd(mem2, mem_g, wkv_full, dk, dv)
    ps_kv, ps_pw = pair_sums([
        (g_wkv.reshape(4, 2, kv_rows, kv_cols), 1, [(0, 4), (2, 2)], (1, 1, kv_rows, kv_cols),
         (lambda i, p: (i, p[0], 0, 0), lambda i, p: (i, 0, 0, 0)), (4, kv_rows, kv_cols), (1, kv_rows, kv_cols),
         lambda i, p: (i, 0, 0), (4,), "pair_sum_w_kv"),
        (g_pw.astype(BF16).reshape(4, 4, 2, pw_rows, GROUP), 2, [(0, 4)], (1, 4, 1, pw_rows, GROUP),
         (lambda i, p: (i, 0, p[0], 0, 0), lambda i, p: (i, 0, 0, 0, 0)), (4, 4, pw_rows, GROUP),
         (1, 4, pw_rows, GROUP), lambda i, p: (i, 0, 0, 0), (4,), "pair_sum_pool_w")])
    g_wi, landed_kv, landed_pw = _matmul(h, dproj, TN, BF16, d, 1024, min(1024, s), "grad_w_in",
                                         _scatter_rider([(ps_kv, 0, [(1, 2)]), (ps_pw, 1, [])]))
    (ps_wi,) = pair_sums([
        (g_wi.reshape(2, d // 2, n_in), 0, [(1, 16)], (1, 128, n_in),
         (lambda i, p: (p[0], i, 0), lambda i, p: (0, i, 0)), (d // 2, n_in), (128, n_in), lambda i, p: (i, 0),
         (d // 2 // 128,), "pair_sum_w_in")])
    grad_x, g_norm_pre, landed_wi = _dx_call(dproj, wi_full, x2, dz, norm_pre, tm, 1024,
                                             _scatter_rider([(ps_wi, 1, [(0, 4)])]))
    psum = [ps_wi, ps_kv, ps_wo, ps_pw]
    landed = [landed_wi, landed_kv, landed_wo, landed_pw]
    from_chip = lambda spec_shape, rank: [
        pl.BlockSpec(spec_shape, functools.partial(lambda i, p, q: (q, i) + (0,) * (rank - 2), q=q))
        for q in range(3)]
    joined = _join_halves([
        (_blockwise(_four_sum, pos, [psum[0]] + [landed[0]] * 3,
                    [pl.BlockSpec((256, n_in // 4), lambda i, p: (i, p[1]))] + from_chip((1, 256, n_in // 4), 3),
                    jax.ShapeDtypeStruct((2, d // 2, n_in // 4), F32),
                    pl.BlockSpec((1, 256, n_in // 4), lambda i, p: (p[0], i, 0)), (d // 2 // 256,), "chip_sum_w_in"),
         0, [(1, 8)]),
        (_blockwise(_four_sum, pos, [psum[1]] + [landed[1]] * 3,
                    [pl.BlockSpec((1, kv_rows, kv_cols), lambda i, p: (p[1], 0, 0))]
                    + from_chip((1, 1, kv_rows, kv_cols), 4),
                    jax.ShapeDtypeStruct((2, kv_rows, kv_cols), F32),
                    pl.BlockSpec((1, kv_rows, kv_cols), lambda i, p: (p[0], 0, 0)), (1,), "chip_sum_w_kv"),
         0, [(1, 2)]),
        (_blockwise(_four_sum, pos, [psum[2]] + [landed[2]] * 3,
                    [pl.BlockSpec((1, wo_rows, d), lambda i, p: (p[1], 0, 0))] + from_chip((1, 1, wo_rows, d), 4),
                    jax.ShapeDtypeStruct((2, wo_rows, d), F32),
                    pl.BlockSpec((1, wo_rows, d), lambda i, p: (p[0], 0, 0)), (1,), "chip_sum_w_out"),
         0, [(1, 2)]),
        (_blockwise(_four_sum, pos, [psum[3]] + [landed[3]] * 3,
                    [pl.BlockSpec((4, 1, pw_rows, GROUP), lambda i, p: (0, p[1], 0, 0))]
                    + from_chip((1, 4, 1, pw_rows, GROUP), 5),
                    jax.ShapeDtypeStruct((4, 2, pw_rows, GROUP), F32),
                    pl.BlockSpec((4, 1, pw_rows, GROUP), lambda i, p: (0, p[0], 0, 0)), (1,), "chip_sum_pool_w"),
         1, []),
    ])
    grads = {"w_in": joined[0].reshape(w_in.shape), "w_kv": joined[1].reshape(w_kv.shape),
             "w_out": joined[2].reshape(w_out.shape), "pool_w": joined[3].reshape(pool_w.shape)}

    small_local = dict(norm_pre=g_norm_pre, pool_scale=g_pool_scale, sgu_ln_g=g_ln_g, sgu_ln_b=g_ln_b,
                       sgu_w=g_sgu_w, sgu_b=g_sgu_b_t.T, mem_norm=g_mem_norm, branch_norm=g_branch_norm,
                       norm_post=g_norm_post)
    small_sum = _allreduce_small(_pack([small_local[n] for n in SMALL]))
    for n, g in zip(SMALL, _unpack(small_sum, [weights[n] for n in SMALL])):
        grads[n] = g

    delta, new_m, new_v = {}, {}, {}
    packed = [small_sum if src is grads else _pack([src[n] for n in SMALL]) for src in (weights, grads, mom1, mom2)]
    outs = _elementwise(_adamw, packed, [F32, F32, F32], "adamw_small")
    for dst, o in zip((delta, new_m, new_v), outs):
        for n, a in zip(SMALL, _unpack(o, [weights[n] for n in SMALL])):
            dst[n] = a
    for n in LARGE:
        cols = weights[n].shape[-1]
        outs = _elementwise(_adamw, [src[n].reshape(-1, cols) for src in (weights, grads, mom1, mom2)],
                            [F32, F32, F32], "adamw_" + n)
        for dst, o in zip((delta, new_m, new_v), outs):
            dst[n] = o.reshape(weights[n].shape)

    loss = lax.psum(loss_local[0, 0], ("x", "y", "c"))
    return (loss, grad_x[None], *[grads[n] for n in ORDER], *[delta[n] for n in ORDER],
            *[new_m[n] for n in ORDER], *[new_v[n] for n in ORDER])
```

```python
import functools

import jax
import jax.numpy as jnp
from jax import lax
from jax.experimental import pallas as pl
from jax.experimental.pallas import tpu as pltpu

F32 = jnp.float32
BF16 = jnp.bfloat16
EPS = 1e-6
MESH = pl.DeviceIdType.MESH
ANY = pl.BlockSpec(memory_space=pl.ANY)

POOL_WINDOWS = (2, 4, 8, 16)
GROUP = 256
HALO = 16
CHUNK = 128
N_SGU_HEADS = 8
N_ATT_HEADS = 4
ATT_DIM = 256
WIDTH = 1024
ATT_SCALE = 1.0 / 16.0

ADAM_LR = 0.001
ADAM_B1 = 0.9
ADAM_B2 = 0.999
ADAM_EPS = 1e-08
ADAM_WD = 0.01
ADAM_STEP = 10

VMEM_LIMIT = 60 * 1024 * 1024


def _params(n_grid_axes, vmem=VMEM_LIMIT):
    return pltpu.CompilerParams(dimension_semantics=("arbitrary",) * n_grid_axes, vmem_limit_bytes=vmem)


def _dot(a, b, dims):
    return lax.dot_general(a, b, (dims, ((), ())), preferred_element_type=F32)


NN = ((1,), (0,))
NT = ((1,), (1,))
TN = ((0,), (0,))


class _Rider:
    def __init__(self, inputs, out_shapes, n_sems, run, aliases=None):
        self.inputs, self.out_shapes, self.n_sems, self.run = list(inputs), list(out_shapes), n_sems, run
        self.aliases = aliases or {}


def _call(body, name, grid, in_specs, out_specs, out_shape, scratch_shapes, inputs, rider=None, prefetch=None,
          aliases=None, rider_refs=False):
    n_in, n_out, n_scr = len(in_specs), len(out_specs), len(scratch_shapes)
    r_in = len(rider.inputs) if rider else 0
    r_out = len(rider.out_shapes) if rider else 0
    n_pre = 0 if prefetch is None else 1

    def whole_body(*refs):
        pre, refs = refs[:n_pre], refs[n_pre:]
        ins, rider_ins = refs[:n_in], refs[n_in:n_in + r_in]
        refs = refs[n_in + r_in:]
        outs, rider_outs = refs[:n_out], refs[n_out:n_out + r_out]
        refs = refs[n_out + r_out:]
        scratch, sems = refs[:n_scr], refs[n_scr:]
        extra = {"rider_outs": rider_outs} if rider_refs else {}
        if rider is None:
            body(*pre, *ins, *outs, *scratch, **extra)
            return
        ids = [pl.program_id(ax) for ax in range(len(grid))]
        first = functools.reduce(lambda p, q: p & q, [i == 0 for i in ids])
        last = functools.reduce(lambda p, q: p & q, [i == g - 1 for i, g in zip(ids, grid)])

        @pl.when(first)
        def _():
            rider.run(rider_ins, rider_outs, *sems, True)

        body(*pre, *ins, *outs, *scratch, **extra)

        @pl.when(last)
        def _():
            rider.run(rider_ins, rider_outs, *sems, False)

    io_aliases = {n_pre + i: o for i, o in (aliases or {}).items()}
    scratch_all = list(scratch_shapes)
    if rider:
        io_aliases.update({n_pre + n_in + i: n_out + o for i, o in rider.aliases.items()})
        scratch_all += [pltpu.SemaphoreType.DMA((rider.n_sems,)), pltpu.SemaphoreType.DMA((rider.n_sems,))]
    specs = dict(grid=grid, in_specs=list(in_specs) + [ANY] * r_in, out_specs=tuple(out_specs) + (ANY,) * r_out,
                 scratch_shapes=scratch_all)
    if n_pre:
        specs = dict(grid_spec=pltpu.PrefetchScalarGridSpec(num_scalar_prefetch=1, **specs))
    outs = pl.pallas_call(
        whole_body, name=name, **specs,
        out_shape=tuple(out_shape) + tuple(rider.out_shapes if rider else ()),
        input_output_aliases=io_aliases, compiler_params=_params(len(grid)),
    )(*([prefetch] if n_pre else []), *inputs, *(rider.inputs if rider else []))
    return tuple(outs)


def _matmul(a, b, dims, out_dtype, tm, tn, tk, name, rider=None):
    if dims == NN:
        (m, k), n = a.shape, b.shape[1]
        a_spec = pl.BlockSpec((tm, tk), lambda i, j, kk: (i, kk))
        b_spec = pl.BlockSpec((tk, tn), lambda i, j, kk: (kk, j))
    else:
        (k, m), n = a.shape, b.shape[1]
        a_spec = pl.BlockSpec((tk, tm), lambda i, j, kk: (kk, i))
        b_spec = pl.BlockSpec((tk, tn), lambda i, j, kk: (kk, j))
    nk = k // tk

    def body(a_ref, b_ref, o_ref, *acc):
        part = lambda: _dot(a_ref[...], b_ref[...], dims)
        if nk == 1:
            o_ref[...] = part().astype(out_dtype)
            return
        (acc_ref,) = acc
        kk = pl.program_id(2)

        @pl.when(kk == 0)
        def _():
            acc_ref[...] = part()

        @pl.when((kk > 0) & (kk < nk - 1))
        def _():
            acc_ref[...] += part()

        @pl.when(kk == nk - 1)
        def _():
            o_ref[...] = (acc_ref[...] + part()).astype(out_dtype)

    return _call(body, name, (m // tm, n // tn, nk), [a_spec, b_spec],
                 [pl.BlockSpec((tm, tn), lambda i, j, kk: (i, j))], [jax.ShapeDtypeStruct((m, n), out_dtype)],
                 [pltpu.VMEM((tm, tn), F32)] if nk > 1 else [], [a, b], rider)


def _proj_piece(chips, first, n_shards, src, g_pre, w_in, proj_in, n_cols, rider, tm, name):
    s, d = src.shape
    cols = n_cols // 4
    fused = g_pre is not None

    def body(chips_ref, *refs, rider_outs=()):
        refs = list(refs)
        src_ref = refs.pop(0)
        g_ref = refs.pop(0) if fused else None
        w_ref = refs.pop(0) if w_in is not None else rider_outs[0]
        if proj_in is not None:
            refs.pop(0)
        proj_ref = refs.pop(0)
        h_ref = refs.pop(0) if fused else None
        wbuf, sem = refs
        q, i = pl.program_id(0), pl.program_id(1)

        @pl.when(i == 0)
        def _():
            at = pl.multiple_of(chips_ref[first + q] * cols, 128)
            cp = pltpu.make_async_copy(w_ref.at[:, pl.ds(at, cols)], wbuf, sem)
            cp.start()
            cp.wait()

        if fused:
            xv = src_ref[...]
            r = lax.rsqrt(jnp.mean(xv * xv, axis=-1, keepdims=True) + EPS)
            h = (xv * r * g_ref[...]).astype(BF16)
            h_ref[...] = h
        else:
            h = src_ref[...]
        proj_ref[...] = _dot(h, wbuf[...], NN)

    row = lambda q, i, ch: (i, 0)
    inputs, in_specs = [src], [pl.BlockSpec((tm, d), row)]
    if fused:
        inputs.append(g_pre)
        in_specs.append(pl.BlockSpec((1, d), lambda q, i, ch: (0, 0)))
    if w_in is not None:
        inputs.append(w_in)
        in_specs.append(ANY)
    aliases = {}
    if proj_in is not None:
        aliases[len(inputs)] = 0
        inputs.append(proj_in)
        in_specs.append(ANY)
    out_specs = [pl.BlockSpec((tm, cols), lambda q, i, ch: (i, ch[first + q]))]
    out_shape = [jax.ShapeDtypeStruct((s, n_cols), F32)]
    if fused:
        assert n_shards == 1
        out_specs.append(pl.BlockSpec((tm, d), row))
        out_shape.append(jax.ShapeDtypeStruct((s, d), BF16))
    return _call(body, name, (n_shards, s // tm), in_specs, out_specs, out_shape,
                 [pltpu.VMEM((d, cols), BF16), pltpu.SemaphoreType.DMA(())], inputs, rider, prefetch=chips,
                 aliases=aliases, rider_refs=True)


def _kv_fwd(mem, g, w_kv):
    m, d = mem.shape

    def body(mem_ref, g_ref, w_ref, k_ref, v_ref):
        mv = mem_ref[...]
        r = lax.rsqrt(jnp.mean(mv * mv, axis=-1, keepdims=True) + EPS)
        mem_n = (mv * r * g_ref[...]).astype(BF16)
        kv = _dot(mem_n, w_ref[...], NN)
        k_ref[...] = kv[:, :WIDTH].astype(BF16)
        v_ref[...] = kv[:, WIDTH:].astype(BF16)

    return pl.pallas_call(
        body, name="kv_fwd",
        out_shape=(jax.ShapeDtypeStruct((m, WIDTH), BF16), jax.ShapeDtypeStruct((m, WIDTH), BF16)),
        compiler_params=_params(0),
    )(mem, g, w_kv)


def _kv_bwd(mem, g, w_kv, dk, dv):
    m, d = mem.shape
    n = w_kv.shape[1]
    col = 512

    def body(mem_ref, g_ref, w_ref, dk_ref, dv_ref, dw_ref, dg_ref):
        mv = mem_ref[...]
        r = lax.rsqrt(jnp.mean(mv * mv, axis=-1, keepdims=True) + EPS)
        mem_hat = mv * r
        mem_n = (mem_hat * g_ref[...]).astype(BF16)
        dkv = jnp.concatenate([dk_ref[...], dv_ref[...]], axis=1).astype(BF16)
        for j in range(n // col):
            dw_ref[:, j * col:(j + 1) * col] = _dot(mem_n, dkv[:, j * col:(j + 1) * col], TN).astype(BF16)
        dmem_n = _dot(dkv, w_ref[...], NT)
        dg_ref[...] = jnp.sum(dmem_n * mem_hat, axis=0, keepdims=True)

    return pl.pallas_call(
        body, name="kv_bwd",
        out_shape=(jax.ShapeDtypeStruct((d, n), BF16), jax.ShapeDtypeStruct((1, d), F32)),
        compiler_params=_params(0),
    )(mem, g, w_kv, dk, dv)


def _sigmoid(x):
    return 1.0 / (1.0 + jnp.exp(-x))


def _inv_counts(t0, t):
    pos = (t0 + lax.broadcasted_iota(jnp.int32, (t, 1), 0) + 1).astype(F32)
    return [1.0 / jnp.minimum(pos, float(w)) for w in POOL_WINDOWS]


def _window_sums(ext, t, backward):
    n = t + HALO
    parts = []
    for gi, w in enumerate(POOL_WINDOWS):
        s = ext[:, gi * GROUP:(gi + 1) * GROUP]
        k = 1
        while k < w:
            s = s + pltpu.roll(s, (n - k) if backward else k, axis=0)
            k *= 2
        parts.append(s[:t] if backward else s[HALO:])
    return parts


def _pool_fwd(xa, halo, inv, pool_w):
    t = xa.shape[0]
    sums = _window_sums(jnp.concatenate([halo, xa], axis=0), t, backward=False)
    d = jnp.concatenate([sums[gi] * inv[gi] - xa[:, gi * GROUP:(gi + 1) * GROUP] for gi in range(4)], axis=1)
    d = d.astype(BF16)
    y = jnp.concatenate([_dot(d[:, gi * GROUP:(gi + 1) * GROUP], pool_w[gi], NN) for gi in range(4)], axis=1)
    return d, y


def _layernorm_fwd(v):
    mu = jnp.mean(v, axis=-1, keepdims=True)
    xc = v - mu
    rstd = lax.rsqrt(jnp.mean(xc * xc, axis=-1, keepdims=True) + EPS)
    return xc * rstd, rstd


def _tril_mask(transposed):
    r = lax.broadcasted_iota(jnp.int32, (CHUNK, CHUNK), 0)
    c = lax.broadcasted_iota(jnp.int32, (CHUNK, CHUNK), 1)
    return (r <= c) if transposed else (r >= c)


def _sgu_mix(w_ref, vals, transposed):
    t = vals.shape[0]
    mask = _tril_mask(transposed)
    ws = [jnp.where(mask, w_ref[h], 0.0).astype(BF16) for h in range(N_SGU_HEADS)]
    rows = []
    for ci in range(t // CHUNK):
        blk = vals[ci * CHUNK:(ci + 1) * CHUNK]
        rows.append(jnp.concatenate(
            [_dot(ws[h], blk[:, h * CHUNK:(h + 1) * CHUNK], NN) for h in range(N_SGU_HEADS)], axis=1))
    return jnp.concatenate(rows, axis=0)


def _attn_fwd(q, k, v):
    ps, os_ = [], []
    for h in range(N_ATT_HEADS):
        sl = slice(h * ATT_DIM, (h + 1) * ATT_DIM)
        s = _dot(q[:, sl], k[:, sl], NT) * ATT_SCALE
        s = s - jnp.max(s, axis=-1, keepdims=True)
        e = jnp.exp(s)
        p = e / jnp.sum(e, axis=-1, keepdims=True)
        ps.append(p)
        os_.append(_dot(p.astype(BF16), v[:, sl], NN))
    return ps, jnp.concatenate(os_, axis=1)


def _rms_branch(y_pre):
    r = lax.rsqrt(jnp.mean(y_pre * y_pre, axis=-1, keepdims=True) + EPS)
    return y_pre * r, r


def _branch_specs(t, n_tiles, order):
    width_in = 7 * WIDTH
    tile = lambda i: order(i)
    per_halo = t // HALO
    const2 = lambda i: (0, 0)
    const3 = lambda i: (0, 0, 0)
    return [
        pl.BlockSpec((t, width_in), lambda i: (tile(i), 0)),
        pl.BlockSpec((HALO, WIDTH), lambda i: (jnp.maximum(tile(i) * per_halo - 1, 0), 0)),
        pl.BlockSpec((4, GROUP, GROUP), const3),
        pl.BlockSpec((1, WIDTH), const2),
        pl.BlockSpec((1, WIDTH), const2),
        pl.BlockSpec((1, WIDTH), const2),
        pl.BlockSpec((N_SGU_HEADS, CHUNK, CHUNK), const3),
        pl.BlockSpec((CHUNK, WIDTH), const2),
        pl.BlockSpec((MEM_ROWS, WIDTH), const2),
        pl.BlockSpec((MEM_ROWS, WIDTH), const2),
        pl.BlockSpec((1, 3 * WIDTH), const2),
    ]


MEM_ROWS = 256


def _branches_fwd(proj, pool_w, pool_scale, ln_g, ln_b, sgu_w, bias_full, k, v, branch_norm, t, rider=None):
    s = proj.shape[0]
    n_tiles = s // t

    def body(proj_ref, halo_ref, pw_ref, ps_ref, lg_ref, lb_ref, sw_ref, sb_ref, k_ref, v_ref, bn_ref, y_ref):
        i = pl.program_id(0)
        col = lambda j: proj_ref[:, j * WIDTH:(j + 1) * WIDTH]
        bn = bn_ref[...]
        halo = jnp.where(i > 0, halo_ref[...], 0.0)
        _, y_pool = _pool_fwd(col(0), halo, _inv_counts(i * t, t), pw_ref[...])
        ga = col(1)
        ya = y_pool * ps_ref[...] * (ga * _sigmoid(ga))
        y_ref[:, 0:WIDTH] = (_rms_branch(ya)[0] * bn[:, 0:WIDTH]).astype(BF16)
        vhat, _ = _layernorm_fwd(col(3))
        vn = (vhat * lg_ref[...] + lb_ref[...]).astype(BF16)
        z = _sgu_mix(sw_ref, vn, transposed=False) + jnp.tile(sb_ref[...], (t // CHUNK, 1))
        gb = col(4)
        yb = col(2) * z * (gb * _sigmoid(gb))
        y_ref[:, WIDTH:2 * WIDTH] = (_rms_branch(yb)[0] * bn[:, WIDTH:2 * WIDTH]).astype(BF16)
        _, o = _attn_fwd(col(5).astype(BF16), k_ref[...], v_ref[...])
        gc = col(6)
        yc = o * (gc * _sigmoid(gc))
        y_ref[:, 2 * WIDTH:] = (_rms_branch(yc)[0] * bn[:, 2 * WIDTH:]).astype(BF16)

    return _call(body, "branches_fwd", (n_tiles,), _branch_specs(t, n_tiles, lambda i: i),
                 [pl.BlockSpec((t, 3 * WIDTH), lambda i: (i, 0))], [jax.ShapeDtypeStruct((s, 3 * WIDTH), BF16)], [],
                 [proj, proj, pool_w, pool_scale, ln_g, ln_b, sgu_w, bias_full, k, v, branch_norm], rider)


def _branches_bwd(proj, dy, pool_w, pool_scale, ln_g, ln_b, sgu_w, sgu_wt, bias_full, k, v, branch_norm, t, rider=None):
    s = proj.shape[0]
    n_tiles = s // t
    n_chunks = t // CHUNK
    order = lambda i: n_tiles - 1 - i

    def body(proj_ref, halo_ref, pw_ref, ps_ref, lg_ref, lb_ref, sw_ref, sb_ref, k_ref, v_ref, bn_ref,
             swt_ref, dy_ref,
             dproj_ref, dpw_ref, dps_ref, dlg_ref, dlb_ref, dsw_ref, dsb_ref, dbn_ref, dk_ref, dv_ref,
             carry_ref, dbias_ref):
        step = pl.program_id(0)
        i = order(step)

        @pl.when(step == 0)
        def _():
            for ref in (dpw_ref, dps_ref, dlg_ref, dlb_ref, dsw_ref, dbn_ref, dk_ref, dv_ref, carry_ref, dbias_ref):
                ref[...] = jnp.zeros(ref.shape, ref.dtype)

        col = lambda j: proj_ref[:, j * WIDTH:(j + 1) * WIDTH]
        bn = bn_ref[...]

        def norm_bwd(y_pre, sl):
            yhat, r = _rms_branch(y_pre)
            dyv = dy_ref[:, sl].astype(F32)
            dbn_ref[:, sl] += jnp.sum(dyv * yhat, axis=0, keepdims=True)
            dyhat = dyv * bn[:, sl]
            return r * (dyhat - yhat * jnp.mean(dyhat * yhat, axis=-1, keepdims=True))

        def gate(gv):
            sg = _sigmoid(gv)
            return gv * sg, sg * (1.0 + gv * (1.0 - sg))

        inv = _inv_counts(i * t, t)
        halo = jnp.where(i > 0, halo_ref[...], 0.0)
        pw = pw_ref[...]
        d, y_pool = _pool_fwd(col(0), halo, inv, pw)
        scale = ps_ref[...]
        silu_a, dsilu_a = gate(col(1))
        pa = y_pool * scale
        dya = norm_bwd(pa * silu_a, slice(0, WIDTH))
        dproj_ref[:, WIDTH:2 * WIDTH] = (dya * pa * dsilu_a).astype(BF16)
        dpa = dya * silu_a
        dps_ref[...] += jnp.sum(dpa * y_pool, axis=0, keepdims=True)
        dy_pool = (dpa * scale).astype(BF16)
        dd_parts, ddc_parts = [], []
        for gi in range(4):
            sl = slice(gi * GROUP, (gi + 1) * GROUP)
            dpw_ref[gi] += _dot(d[:, sl], dy_pool[:, sl], TN)
            dd = _dot(dy_pool[:, sl], pw[gi], NT)
            dd_parts.append(dd)
            ddc_parts.append(dd * inv[gi])
        ddc = jnp.concatenate(ddc_parts, axis=1)
        sums = _window_sums(jnp.concatenate([ddc, carry_ref[...]], axis=0), t, backward=True)
        carry_ref[...] = ddc[:HALO]
        dproj_ref[:, 0:WIDTH] = jnp.concatenate([sums[gi] - dd_parts[gi] for gi in range(4)], axis=1).astype(BF16)

        vhat, rstd = _layernorm_fwd(col(3))
        lg = lg_ref[...]
        vn = (vhat * lg + lb_ref[...]).astype(BF16)
        z = _sgu_mix(sw_ref, vn, transposed=False) + jnp.tile(sb_ref[...], (n_chunks, 1))
        u = col(2)
        silu_b, dsilu_b = gate(col(4))
        uz = u * z
        dyb = norm_bwd(uz * silu_b, slice(WIDTH, 2 * WIDTH))
        dproj_ref[:, 4 * WIDTH:5 * WIDTH] = (dyb * uz * dsilu_b).astype(BF16)
        duz = dyb * silu_b
        dproj_ref[:, 2 * WIDTH:3 * WIDTH] = (duz * z).astype(BF16)
        dz = duz * u
        dz_b = dz.astype(BF16)
        for ci in range(n_chunks):
            rows = slice(ci * CHUNK, (ci + 1) * CHUNK)
            dbias_ref[...] += dz[rows]
            for h in range(N_SGU_HEADS):
                sl = slice(h * CHUNK, (h + 1) * CHUNK)
                dsw_ref[h] += _dot(dz_b[rows, sl], vn[rows, sl], NT)
        dvn = _sgu_mix(swt_ref, dz_b, transposed=True)
        dlg_ref[...] += jnp.sum(dvn * vhat, axis=0, keepdims=True)
        dlb_ref[...] += jnp.sum(dvn, axis=0, keepdims=True)
        dvhat = dvn * lg
        dvb = rstd * (dvhat - jnp.mean(dvhat, axis=-1, keepdims=True)
                      - vhat * jnp.mean(dvhat * vhat, axis=-1, keepdims=True))
        dproj_ref[:, 3 * WIDTH:4 * WIDTH] = dvb.astype(BF16)

        q = col(5).astype(BF16)
        kv_k, kv_v = k_ref[...], v_ref[...]
        ps, o = _attn_fwd(q, kv_k, kv_v)
        silu_c, dsilu_c = gate(col(6))
        dyc = norm_bwd(o * silu_c, slice(2 * WIDTH, 3 * WIDTH))
        dproj_ref[:, 6 * WIDTH:7 * WIDTH] = (dyc * o * dsilu_c).astype(BF16)
        do = (dyc * silu_c).astype(BF16)
        dq_parts = []
        for h in range(N_ATT_HEADS):
            sl = slice(h * ATT_DIM, (h + 1) * ATT_DIM)
            p = ps[h]
            dp = _dot(do[:, sl], kv_v[:, sl], NT)
            ds = (p * (dp - jnp.sum(p * dp, axis=-1, keepdims=True)) * ATT_SCALE).astype(BF16)
            dq_parts.append(_dot(ds, kv_k[:, sl], NN))
            dk_ref[:, sl] += _dot(ds, q[:, sl], TN)
            dv_ref[:, sl] += _dot(p.astype(BF16), do[:, sl], TN)
        dproj_ref[:, 5 * WIDTH:6 * WIDTH] = jnp.concatenate(dq_parts, axis=1).astype(BF16)

        @pl.when(step == n_tiles - 1)
        def _():
            keep = _tril_mask(transposed=False)
            for h in range(N_SGU_HEADS):
                dsw_ref[h] = jnp.where(keep, dsw_ref[h], 0.0)
            dsb_ref[...] = jnp.concatenate(
                [jnp.sum(dbias_ref[:, h * CHUNK:(h + 1) * CHUNK], axis=1, keepdims=True)
                 for h in range(N_SGU_HEADS)], axis=1)

    const2 = lambda i: (0, 0)
    const3 = lambda i: (0, 0, 0)
    out_shapes = (
        jax.ShapeDtypeStruct((s, 7 * WIDTH), BF16),
        jax.ShapeDtypeStruct((4, GROUP, GROUP), F32),
        jax.ShapeDtypeStruct((1, WIDTH), F32),
        jax.ShapeDtypeStruct((1, WIDTH), F32),
        jax.ShapeDtypeStruct((1, WIDTH), F32),
        jax.ShapeDtypeStruct((N_SGU_HEADS, CHUNK, CHUNK), F32),
        jax.ShapeDtypeStruct((CHUNK, N_SGU_HEADS), F32),
        jax.ShapeDtypeStruct((1, 3 * WIDTH), F32),
        jax.ShapeDtypeStruct((MEM_ROWS, WIDTH), F32),
        jax.ShapeDtypeStruct((MEM_ROWS, WIDTH), F32),
    )
    out_specs = (
        pl.BlockSpec((t, 7 * WIDTH), lambda i: (order(i), 0)),
        pl.BlockSpec((4, GROUP, GROUP), const3),
        pl.BlockSpec((1, WIDTH), const2),
        pl.BlockSpec((1, WIDTH), const2),
        pl.BlockSpec((1, WIDTH), const2),
        pl.BlockSpec((N_SGU_HEADS, CHUNK, CHUNK), const3),
        pl.BlockSpec((CHUNK, N_SGU_HEADS), const2),
        pl.BlockSpec((1, 3 * WIDTH), const2),
        pl.BlockSpec((MEM_ROWS, WIDTH), const2),
        pl.BlockSpec((MEM_ROWS, WIDTH), const2),
    )
    in_specs = _branch_specs(t, n_tiles, order) + [
        pl.BlockSpec((N_SGU_HEADS, CHUNK, CHUNK), const3),
        pl.BlockSpec((t, 3 * WIDTH), lambda i: (order(i), 0)),
    ]
    return _call(body, "branches_bwd", (n_tiles,), in_specs, out_specs, out_shapes,
                 [pltpu.VMEM((HALO, WIDTH), F32), pltpu.VMEM((CHUNK, WIDTH), F32)],
                 [proj, proj, pool_w, pool_scale, ln_g, ln_b, sgu_w, bias_full, k, v, branch_norm, sgu_wt, dy], rider)


def _out_loss(y, w_out, x, target, g_post, tm):
    s, d = x.shape
    e_w = y.shape[1]
    n_tiles = s // tm

    def body(y_ref, w_ref, x_ref, t_ref, g_ref, loss_ref, dz_ref, dout_ref, dy_ref, dg_ref, sq_ref):
        i = pl.program_id(0)

        @pl.when(i == 0)
        def _():
            sq_ref[...] = jnp.zeros(sq_ref.shape, F32)
            dg_ref[...] = jnp.zeros(dg_ref.shape, F32)

        w = w_ref[...]
        out = _dot(y_ref[...], w, NN)
        r = lax.rsqrt(jnp.mean(out * out, axis=-1, keepdims=True) + EPS)
        outn = out * r
        g = g_ref[...]
        err = (x_ref[...] + outn * g) - t_ref[...]
        sq_ref[...] += jnp.sum(err * err, axis=0, keepdims=True)
        dz = err * (1.0 / d)
        dz_ref[...] = dz
        dg_ref[...] += jnp.sum(dz * outn, axis=0, keepdims=True)
        doutn = dz * g
        dout = (r * (doutn - outn * jnp.mean(doutn * outn, axis=-1, keepdims=True))).astype(BF16)
        dout_ref[...] = dout
        dy_ref[...] = _dot(dout, w, NT).astype(BF16)

        @pl.when(i == n_tiles - 1)
        def _():
            loss_ref[...] = 0.5 * jnp.sum(sq_ref[...], axis=1, keepdims=True) * (1.0 / d)

    row = lambda i: (i, 0)
    const2 = lambda i: (0, 0)
    return pl.pallas_call(
        body, name="out_loss", grid=(n_tiles,),
        in_specs=[
            pl.BlockSpec((tm, e_w), row),
            pl.BlockSpec((e_w, d), const2, pipeline_mode=pl.Buffered(1)),
            pl.BlockSpec((tm, d), row),
            pl.BlockSpec((tm, d), row),
            pl.BlockSpec((1, d), const2),
        ],
        out_specs=(
            pl.BlockSpec((1, 1), const2),
            pl.BlockSpec((tm, d), row),
            pl.BlockSpec((tm, d), row),
            pl.BlockSpec((tm, e_w), row),
            pl.BlockSpec((1, d), const2),
        ),
        out_shape=(
            jax.ShapeDtypeStruct((1, 1), F32),
            jax.ShapeDtypeStruct((s, d), F32),
            jax.ShapeDtypeStruct((s, d), BF16),
            jax.ShapeDtypeStruct((s, e_w), BF16),
            jax.ShapeDtypeStruct((1, d), F32),
        ),
        scratch_shapes=[pltpu.VMEM((1, d), F32)],
        compiler_params=_params(1),
    )(y, w_out, x, target, g_post)


def _dx_call(dproj, w_in, x, dz, g_pre, tm, tk, rider=None):
    s, d = x.shape
    k_total = dproj.shape[1]
    nk = k_total // tk
    n_tiles = s // tm

    def body(dp_ref, w_ref, x_ref, dz_ref, g_ref, dx_ref, dg_ref, acc_ref):
        i, kk = pl.program_id(0), pl.program_id(1)
        part = lambda: _dot(dp_ref[...], w_ref[...], NT)

        @pl.when(kk == 0)
        def _():
            acc_ref[...] = part()

        @pl.when((kk > 0) & (kk < nk - 1))
        def _():
            acc_ref[...] += part()

        @pl.when((i == 0) & (kk == 0))
        def _():
            dg_ref[...] = jnp.zeros(dg_ref.shape, F32)

        @pl.when(kk == nk - 1)
        def _():
            dh = acc_ref[...] + part()
            xv = x_ref[...]
            r = lax.rsqrt(jnp.mean(xv * xv, axis=-1, keepdims=True) + EPS)
            xhat = xv * r
            dg_ref[...] += jnp.sum(dh * xhat, axis=0, keepdims=True)
            dxhat = dh * g_ref[...]
            dx_ref[...] = dz_ref[...] + r * (dxhat - xhat * jnp.mean(dxhat * xhat, axis=-1, keepdims=True))

    row = lambda i, kk: (i, 0)
    const2 = lambda i, kk: (0, 0)
    return _call(
        body, "dx", (n_tiles, nk),
        [
            pl.BlockSpec((tm, tk), lambda i, kk: (i, kk)),
            pl.BlockSpec((d, tk), lambda i, kk: (0, kk)),
            pl.BlockSpec((tm, d), row),
            pl.BlockSpec((tm, d), row),
            pl.BlockSpec((1, d), const2),
        ],
        [pl.BlockSpec((tm, d), row), pl.BlockSpec((1, d), const2)],
        [jax.ShapeDtypeStruct((s, d), F32), jax.ShapeDtypeStruct((1, d), F32)],
        [pltpu.VMEM((tm, d), F32)], [dproj, w_in, x, dz, g_pre], rider)


def _rows_tile(rows, cols, n_arrays, itemsize=4):
    budget = 24 * 1024 * 1024 // (2 * n_arrays * cols * itemsize)
    if rows <= budget:
        return rows
    best = None
    for cand in range(16, rows + 1, 16):
        if rows % cand == 0 and cand <= max(budget, 16):
            best = cand
    return best if best is not None else rows


def _elementwise(fn, inputs, out_dtypes, name):
    rows, cols = inputs[0].shape
    tr = _rows_tile(rows, cols, len(inputs) + len(out_dtypes))
    n_in = len(inputs)

    def body(*refs):
        outs = fn(*[r[...] for r in refs[:n_in]])
        for o_ref, o in zip(refs[n_in:], outs):
            o_ref[...] = o.astype(o_ref.dtype)

    spec = pl.BlockSpec((tr, cols), lambda i: (i, 0))
    return pl.pallas_call(
        body, name=name, grid=(rows // tr,),
        in_specs=[spec] * n_in, out_specs=tuple([spec] * len(out_dtypes)),
        out_shape=tuple(jax.ShapeDtypeStruct((rows, cols), dt) for dt in out_dtypes),
        compiler_params=_params(1),
    )(*inputs)


def _blockwise(fn, pos, inputs, in_specs, out_shape, out_spec, grid, name):
    n_in = len(inputs)

    def body(pos_ref, *refs):
        o_ref = refs[n_in]
        (out,) = fn(*[r[...].reshape(o_ref.shape) for r in refs[:n_in]])
        o_ref[...] = out.astype(o_ref.dtype)

    return pl.pallas_call(
        body, name=name,
        grid_spec=pltpu.PrefetchScalarGridSpec(num_scalar_prefetch=1, grid=grid, in_specs=in_specs,
                                               out_specs=out_spec),
        out_shape=out_shape,
        compiler_params=_params(len(grid)),
    )(pos, *inputs)


def _cast_copy(x):
    return (x,)


def _pair_sum(mine, theirs):
    return ((mine.astype(F32) + theirs.astype(F32)),)


def _four_sum(own, t0, t1, t2):
    return ((((own.astype(F32) + t0.astype(F32)) + t1.astype(F32)) + t2.astype(F32)),)


def _adamw(w, g, m, v):
    m = ADAM_B1 * m + (1.0 - ADAM_B1) * g
    v = ADAM_B2 * v + (1.0 - ADAM_B2) * jnp.square(g)
    m_hat = m / (1.0 - ADAM_B1 ** ADAM_STEP)
    v_hat = v / (1.0 - ADAM_B2 ** ADAM_STEP)
    delta = -ADAM_LR * (m_hat / (jnp.sqrt(v_hat) + ADAM_EPS) + ADAM_WD * w)
    return delta, m, v


def _place():
    x, y, c = lax.axis_index("x"), lax.axis_index("y"), lax.axis_index("c")
    chips = [(1 - x, y), (x, 1 - y), (1 - x, 1 - y)]
    return x, y, c, chips


def _remote(src, dst, send_sem, recv_sem, to):
    return pltpu.make_async_remote_copy(src_ref=src, dst_ref=dst, send_sem=send_sem, recv_sem=recv_sem,
                                        device_id=to, device_id_type=MESH)


def _split(ref, plan):
    views = [ref]
    for axis, parts in plan:
        size = ref.shape[axis] // parts
        assert size * parts == ref.shape[axis]
        views = [v.at[tuple(pl.ds(q * size, size) if i == axis else slice(None) for i in range(len(ref.shape)))]
                 for v in views for q in range(parts)]
    return views


def _remote_in_parts(src, dst, send_sem, recv_sem, to, plan):
    for s, d in zip(_split(src, plan), _split(dst, plan)):
        _remote(s, d, send_sem, recv_sem, to).start()
    return _remote(src, dst, send_sem, recv_sem, to)


def _local_in_parts(src, dst, sem, plan):
    for s, d in zip(_split(src, plan), _split(dst, plan)):
        pltpu.make_async_copy(s, d, sem).start()
    return pltpu.make_async_copy(src, dst, sem)


def _hbm_call(body, name, inputs, out_shapes, scratch, aliases=None):
    return pl.pallas_call(
        body, name=name,
        in_specs=[ANY] * len(inputs), out_specs=tuple([ANY] * len(out_shapes)), out_shape=tuple(out_shapes),
        scratch_shapes=scratch, input_output_aliases=aliases or {},
        compiler_params=pltpu.CompilerParams(has_side_effects=True),
    )(*inputs)


def _gather_rider(fulls, kinds, peers=(0, 1, 2)):
    n = len(fulls)

    def full_half(a, ref, chip, cc):
        if a == 0:
            rows, cols = ref.shape[0] // 2, ref.shape[1] // 4
            return ref.at[pl.ds(cc * rows, rows), pl.ds(pl.multiple_of(chip * cols, 128), cols)]
        if a == 3:
            rows = ref.shape[1] // 8
            return ref.at[:, pl.ds(pl.multiple_of((2 * chip + cc) * rows, 16), rows), :]
        rows = ref.shape[0] // 8
        return ref.at[pl.ds(pl.multiple_of((2 * chip + cc) * rows, 16), rows), :]

    def run(in_refs, full_refs, send_sems, recv_sems, start):
        x, y, c, chips = _place()
        me = 2 * x + y
        sibling = (x, y, 1 - c)
        plans = [[(0, 8)], [(0, 2)], [(0, 2)], []]
        chips = [(p, chips[p]) for p in peers]
        if start:
            for p, chip in chips:
                for a in range(n):
                    mine = full_half(kinds[a], full_refs[a], me, c)
                    _remote_in_parts(mine, mine, send_sems.at[6 * a + p], recv_sems.at[6 * a + p], (*chip, c),
                                     plans[kinds[a]])
            return
        passed_on = []
        for p, chip in chips:
            them = 2 * chip[0] + chip[1]
            for a in range(n):
                landed = full_half(kinds[a], full_refs[a], them, c)
                _remote(landed, landed, send_sems.at[6 * a + p], recv_sems.at[6 * a + p], (*chip, c)).wait_recv()
                passed_on.append(_remote_in_parts(landed, landed, send_sems.at[6 * a + 3 + p],
                                                  recv_sems.at[6 * a + 3 + p], sibling, plans[kinds[a]]))
        for p, chip in chips:
            them = 2 * chip[0] + chip[1]
            for a in range(n):
                passed = full_half(kinds[a], full_refs[a], them, 1 - c)
                _remote(passed, passed, send_sems.at[6 * a + 3 + p], recv_sems.at[6 * a + 3 + p], sibling).wait_recv()
                mine = full_half(kinds[a], full_refs[a], me, c)
                _remote(mine, mine, send_sems.at[6 * a + p], recv_sems.at[6 * a + p], (*chip, c)).wait_send()
        for cp in passed_on:
            cp.wait_send()

    return _Rider(fulls, [jax.ShapeDtypeStruct(f.shape, f.dtype) for f in fulls], 6 * n, run,
                  aliases={a: a for a in range(n)})


def _run_rider(rider, name):
    r_in = len(rider.inputs)

    def body(*refs):
        in_refs, out_refs = refs[:r_in], refs[r_in:r_in + len(rider.out_shapes)]
        send_sems, recv_sems = refs[r_in + len(rider.out_shapes):]
        rider.run(in_refs, out_refs, send_sems, recv_sems, True)
        rider.run(in_refs, out_refs, send_sems, recv_sems, False)

    return _hbm_call(body, name, rider.inputs, rider.out_shapes,
                     [pltpu.SemaphoreType.DMA((rider.n_sems,)), pltpu.SemaphoreType.DMA((rider.n_sems,))],
                     aliases=rider.aliases)


def _exchange_halves(grads, name):
    n = len(grads)
    arrays = [g for g, _, _ in grads]
    out_shapes = [jax.ShapeDtypeStruct(tuple(1 if i == ax else dim for i, dim in enumerate(g.shape)), g.dtype)
                  for g, ax, _ in grads]

    def half(ref, ax, cc):
        idx = tuple(pl.ds(cc, 1) if i == ax else slice(None) for i in range(len(ref.shape)))
        return ref.at[idx]

    def body(*refs):
        in_refs, out_refs = refs[:n], refs[n:2 * n]
        send_sems, recv_sems = refs[2 * n:]
        x, y, c, _ = _place()
        sibling = (x, y, 1 - c)
        copies = [_remote_in_parts(half(in_refs[a], grads[a][1], 1 - c), out_refs[a], send_sems.at[a],
                                   recv_sems.at[a], sibling, grads[a][2]) for a in range(n)]
        for rem in copies:
            rem.wait()

    return _hbm_call(body, name, arrays, out_shapes,
                     [pltpu.SemaphoreType.DMA((n,)), pltpu.SemaphoreType.DMA((n,))])


def _scatter_rider(parts):
    n = len(parts)
    arrays = [p for p, _, _ in parts]

    def block_shape(p, ax):
        if ax == len(p.shape) - 1:
            return p.shape[:-1] + (p.shape[-1] // 4,)
        return tuple(1 if i == ax else dim for i, dim in enumerate(p.shape))

    out_shapes = [jax.ShapeDtypeStruct((3,) + block_shape(p, ax), p.dtype) for p, ax, _ in parts]

    def block(ref, ax, chip):
        rank = len(ref.shape)
        if ax == rank - 1:
            cols = ref.shape[-1] // 4
            last = pl.ds(pl.multiple_of(chip * cols, 128), cols)
            return ref.at[tuple([slice(None)] * (rank - 1) + [last])]
        return ref.at[tuple(pl.ds(chip, 1) if i == ax else slice(None) for i in range(rank))]

    def run(in_refs, out_refs, send_sems, recv_sems, start):
        x, y, c, chips = _place()
        for a in range(n):
            ax, plan = parts[a][1], parts[a][2]
            for p, chip in enumerate(chips):
                src, dst = block(in_refs[a], ax, 2 * chip[0] + chip[1]), out_refs[a].at[p]
                sems = (send_sems.at[3 * a + p], recv_sems.at[3 * a + p])
                if start:
                    _remote_in_parts(src, dst, *sems, (*chip, c), plan)
                else:
                    _remote(src, dst, *sems, (*chip, c)).wait()

    return _Rider(arrays, out_shapes, 3 * n, run)


def _join_halves(joined):
    n = len(joined)
    arrays = [j for j, _, _ in joined]

    def body(*refs):
        out_refs = refs[n:2 * n]
        send_sems, recv_sems = refs[2 * n:]
        x, y, c, _ = _place()
        sibling = (x, y, 1 - c)

        def half(a, cc):
            rank = len(out_refs[a].shape)
            return out_refs[a].at[tuple(pl.ds(cc, 1) if i == joined[a][1] else slice(None) for i in range(rank))]

        sends = [_remote_in_parts(half(a, c), half(a, c), send_sems.at[a], recv_sems.at[a], sibling, joined[a][2])
                 for a in range(n)]
        for a, rem in enumerate(sends):
            rem.wait_send()
            _remote(half(a, 1 - c), half(a, 1 - c), send_sems.at[a], recv_sems.at[a], sibling).wait_recv()

    return _hbm_call(body, "join_halves", arrays, [jax.ShapeDtypeStruct(j.shape, j.dtype) for j in arrays],
                     [pltpu.SemaphoreType.DMA((n,)), pltpu.SemaphoreType.DMA((n,))],
                     aliases={a: a for a in range(n)})


def _allreduce_small(packed):
    rows, lanes = packed.shape
    half = rows // 2

    def body(in_ref, out_ref, pair_ref, gath_ref, send_sems, recv_sems):
        x, y, c, chips = _place()
        me = 2 * x + y
        sibling = (x, y, 1 - c)
        mine = pl.ds(pl.multiple_of(c * half, 8), half)
        theirs = pl.ds(pl.multiple_of((1 - c) * half, 8), half)
        to_sib = _remote(in_ref.at[theirs], pair_ref, send_sems.at[0], recv_sems.at[0], sibling)
        to_sib.start()
        to_sib.wait()
        gath_ref[me] = in_ref[mine] + pair_ref[...]
        sends = [_remote(gath_ref.at[me], gath_ref.at[me], send_sems.at[1 + p], recv_sems.at[1 + p], (*chip, c))
                 for p, chip in enumerate(chips)]
        for cp in sends:
            cp.start()
        for p, chip in enumerate(chips):
            slot = gath_ref.at[2 * chip[0] + chip[1]]
            _remote(slot, slot, send_sems.at[1 + p], recv_sems.at[1 + p], (*chip, c)).wait_recv()
        for cp in sends:
            cp.wait_send()
        out_ref[mine] = ((gath_ref[0] + gath_ref[1]) + gath_ref[2]) + gath_ref[3]
        back = _remote(out_ref.at[mine], out_ref.at[mine], send_sems.at[4], recv_sems.at[4], sibling)
        back.start()
        back.wait_send()
        _remote(out_ref.at[theirs], out_ref.at[theirs], send_sems.at[4], recv_sems.at[4], sibling).wait_recv()

    vmem = pl.BlockSpec(memory_space=pltpu.VMEM)
    return pl.pallas_call(
        body, name="allreduce_small",
        in_specs=[vmem], out_specs=vmem, out_shape=jax.ShapeDtypeStruct((rows, lanes), F32),
        scratch_shapes=[pltpu.VMEM((half, lanes), F32), pltpu.VMEM((4, half, lanes), F32),
                        pltpu.SemaphoreType.DMA((5,)), pltpu.SemaphoreType.DMA((5,))],
        compiler_params=pltpu.CompilerParams(has_side_effects=True, vmem_limit_bytes=32 * 1024 * 1024),
    )(packed)


SMALL = ("norm_pre", "pool_scale", "sgu_ln_g", "sgu_ln_b", "sgu_w", "sgu_b", "mem_norm", "branch_norm", "norm_post")
LARGE = ("w_in", "pool_w", "w_kv", "w_out")
ORDER = ("norm_pre", "w_in", "pool_w", "pool_scale", "sgu_ln_g", "sgu_ln_b", "sgu_w", "sgu_b", "mem_norm", "w_kv",
         "branch_norm", "w_out", "norm_post")


def _pack(arrays):
    rows = [a.reshape(-1, 128) for a in arrays]
    pad = -sum(r.shape[0] for r in rows) % 16
    return jnp.concatenate(rows + ([jnp.zeros((pad, 128), F32)] if pad else []), axis=0)


def _unpack(packed, like):
    out, row = [], 0
    for a in like:
        rows = a.size // 128
        out.append(packed[row:row + rows].reshape(a.shape))
        row += rows
    return out


def kernel(x, mem, norm_pre, w_in, pool_w, pool_scale, sgu_ln_g, sgu_ln_b, sgu_w, sgu_b, mem_norm, w_kv, branch_norm, w_out, norm_post, loss_target, m_norm_pre, m_w_in, m_pool_w, m_pool_scale, m_sgu_ln_g, m_sgu_ln_b, m_sgu_w, m_sgu_b, m_mem_norm, m_w_kv, m_branch_norm, m_w_out, m_norm_post, v_norm_pre, v_w_in, v_pool_w, v_pool_scale, v_sgu_ln_g, v_sgu_ln_b, v_sgu_w, v_sgu_b, v_mem_norm, v_w_kv, v_branch_norm, v_w_out, v_norm_post):
    weights = dict(norm_pre=norm_pre, w_in=w_in, pool_w=pool_w, pool_scale=pool_scale, sgu_ln_g=sgu_ln_g,
                   sgu_ln_b=sgu_ln_b, sgu_w=sgu_w, sgu_b=sgu_b, mem_norm=mem_norm, w_kv=w_kv, branch_norm=branch_norm,
                   w_out=w_out, norm_post=norm_post)
    mom1 = dict(norm_pre=m_norm_pre, w_in=m_w_in, pool_w=m_pool_w, pool_scale=m_pool_scale, sgu_ln_g=m_sgu_ln_g,
                sgu_ln_b=m_sgu_ln_b, sgu_w=m_sgu_w, sgu_b=m_sgu_b, mem_norm=m_mem_norm, w_kv=m_w_kv,
                branch_norm=m_branch_norm, w_out=m_w_out, norm_post=m_norm_post)
    mom2 = dict(norm_pre=v_norm_pre, w_in=v_w_in, pool_w=v_pool_w, pool_scale=v_pool_scale, sgu_ln_g=v_sgu_ln_g,
                sgu_ln_b=v_sgu_ln_b, sgu_w=v_sgu_w, sgu_b=v_sgu_b, mem_norm=v_mem_norm, w_kv=v_w_kv,
                branch_norm=v_branch_norm, w_out=v_w_out, norm_post=v_norm_post)

    s, d = x.shape[1], x.shape[2]
    x2, mem2, tgt2 = x[0], mem[0], loss_target[0]
    t_branch = min(256, s)
    tm = min(512, s)

    core = lax.axis_index("c")
    chip = 2 * lax.axis_index("x") + lax.axis_index("y")
    pos = jnp.stack([core, chip]).astype(jnp.int32)
    n_in, n_kv, n_out = 4 * w_in.shape[2], 4 * w_kv.shape[1], 4 * w_out.shape[1]
    wi_rows, kv_rows, wo_rows = d // 8, n_kv // 8, n_out // 8

    def placed(shard, full_shape, block, grid, in_map, out_map, name):
        return _blockwise(_cast_copy, pos, [shard], [pl.BlockSpec(block, in_map)],
                          jax.ShapeDtypeStruct(full_shape, BF16), pl.BlockSpec(block, out_map), grid, name)

    kv_cols, pw_rows = w_kv.shape[2], GROUP // 8
    wi_own = placed(w_in[0], (d, n_in), (wi_rows, n_in // 4), (8,), lambda i, p: (i, 0), lambda i, p: (i, p[1]),
                    "place_w_in")
    wkv_own = placed(w_kv[0], (n_kv, kv_cols), (kv_rows, kv_cols), (2,), lambda i, p: (i, 0),
                     lambda i, p: (2 * p[1] + i, 0), "place_w_kv")
    wo_own = placed(w_out[0], (n_out, d), (wo_rows, d), (2,), lambda i, p: (i, 0), lambda i, p: (2 * p[1] + i, 0),
                    "place_w_out")
    pw_own = placed(pool_w[0], (4, GROUP, GROUP), (4, GROUP // 4, GROUP), (1,), lambda i, p: (0, 0, 0),
                    lambda i, p: (0, p[1], 0), "place_pool_w")

    x_pos, y_pos = lax.axis_index("x"), lax.axis_index("y")
    chips = jnp.stack([chip, 2 * (1 - x_pos) + y_pos, 2 * x_pos + 1 - y_pos,
                       2 * (1 - x_pos) + 1 - y_pos]).astype(jnp.int32)
    mem_g = mem_norm.reshape(1, d)
    proj, h, wi_full = _proj_piece(chips, 0, 1, x2, norm_pre, None, None, n_in,
                                   _gather_rider([wi_own], [0], peers=(0, 1)), tm, "proj_own")
    proj, wi_full = _proj_piece(chips, 1, 2, h, None, None, proj, n_in,
                                _gather_rider([wi_full], [0], peers=(2,)), tm, "proj_neighbours")
    proj, wkv_full, pw_full = _proj_piece(chips, 3, 1, h, None, wi_full, proj, n_in,
                                          _gather_rider([wkv_own, pw_own], [1, 3]), tm, "proj_diagonal")
    k_m, v_m = _kv_fwd(mem2, mem_g, wkv_full)
    bias_full = jnp.repeat(sgu_b[0].T, CHUNK, axis=1)
    y, wo_full = _branches_fwd(proj, pw_full, pool_scale, sgu_ln_g, sgu_ln_b, sgu_w[0], bias_full, k_m, v_m,
                               branch_norm, t_branch, _gather_rider([wo_own], [2]))
    loss_local, dz, dout, dy, g_norm_post = _out_loss(y, wo_full, x2, tgt2, norm_post, min(256, s))

    def pair_sums(views):
        theirs = _exchange_halves([(v[0], v[1], v[2]) for v in views], "exchange_for_" + views[0][9])
        return [_blockwise(_pair_sum, pos, [v[0], th], [pl.BlockSpec(v[3], v[4][0]), pl.BlockSpec(v[3], v[4][1])],
                           jax.ShapeDtypeStruct(v[5], BF16), pl.BlockSpec(v[6], v[7]), v[8], v[9])
                for v, th in zip(views, theirs)]

    (g_wo,) = _matmul(y, dout, TN, BF16, 1536, 1024, min(1024, s), "grad_w_out")
    (ps_wo,) = pair_sums([
        (g_wo.reshape(4, 2, wo_rows, d), 1, [(0, 4), (2, 2)], (1, 1, wo_rows, d),
         (lambda i, p: (i, p[0], 0, 0), lambda i, p: (i, 0, 0, 0)), (4, wo_rows, d), (1, wo_rows, d),
         lambda i, p: (i, 0, 0), (4,), "pair_sum_w_out")])
    (dproj, g_pw, g_pool_scale, g_ln_g, g_ln_b, g_sgu_w, g_sgu_b_t, g_branch_norm, dk, dv, landed_wo) = _branches_bwd(
        proj, dy, pw_full, pool_scale, sgu_ln_g, sgu_ln_b, sgu_w[0], jnp.swapaxes(sgu_w[0], 1, 2), bias_full,
        k_m, v_m, branch_norm, t_branch, _scatter_rider([(ps_wo, 0, [(1, 2)])]))
    g_wkv, g_mem_norm = _kv_bwd(mem2, mem_g, wkv_full, dk, dv)
    ps_kv, ps_pw = pair_sums([
        (g_wkv.reshape(4, 2, kv_rows, kv_cols), 1, [(0, 4), (2, 2)], (1, 1, kv_rows, kv_cols),
         (lambda i, p: (i, p[0], 0, 0), lambda i, p: (i, 0, 0, 0)), (4, kv_rows, kv_cols), (1, kv_rows, kv_cols),
         lambda i, p: (i, 0, 0), (4,), "pair_sum_w_kv"),
        (g_pw.astype(BF16).reshape(4, 4, 2, pw_rows, GROUP), 2, [(0, 4)], (1, 4, 1, pw_rows, GROUP),
         (lambda i, p: (i, 0, p[0], 0, 0), lambda i, p: (i, 0, 0, 0, 0)), (4, 4, pw_rows, GROUP),
         (1, 4, pw_rows, GROUP), lambda i, p: (i, 0, 0, 0), (4,), "pair_sum_pool_w")])
    g_wi, landed_kv, landed_pw = _matmul(h, dproj, TN, BF16, d, 1024, min(1024, s), "grad_w_in",
                                         _scatter_rider([(ps_kv, 0, [(1, 2)]), (ps_pw, 1, [])]))
    (ps_wi,) = pair_sums([
        (g_wi.reshape(2, d // 2, n_in), 0, [(1, 16)], (1, 128, n_in),
         (lambda i, p: (p[0], i, 0), lambda i, p: (0, i, 0)), (d // 2, n_in), (128, n_in), lambda i, p: (i, 0),
         (d // 2 // 128,), "pair_sum_w_in")])
    grad_x, g_norm_pre, landed_wi = _dx_call(dproj, wi_full, x2, dz, norm_pre, tm, 1024,
                                             _scatter_rider([(ps_wi, 1, [(0, 4)])]))
    psum = [ps_wi, ps_kv, ps_wo, ps_pw]
    landed = [landed_wi, landed_kv, landed_wo, landed_pw]
    from_chip = lambda spec_shape, rank: [
        pl.BlockSpec(spec_shape, functools.partial(lambda i, p, q: (q, i) + (0,) * (rank - 2), q=q))
        for q in range(3)]
    joined = _join_halves([
        (_blockwise(_four_sum, pos, [psum[0]] + [landed[0]] * 3,
                    [pl.BlockSpec((256, n_in // 4), lambda i, p: (i, p[1]))] + from_chip((1, 256, n_in // 4), 3),
                    jax.ShapeDtypeStruct((2, d // 2, n_in // 4), F32),
                    pl.BlockSpec((1, 256, n_in // 4), lambda i, p: (p[0], i, 0)), (d // 2 // 256,), "chip_sum_w_in"),
         0, [(1, 8)]),
        (_blockwise(_four_sum, pos, [psum[1]] + [landed[1]] * 3,
                    [pl.BlockSpec((1, kv_rows, kv_cols), lambda i, p: (p[1], 0, 0))]
                    + from_chip((1, 1, kv_rows, kv_cols), 4),
                    jax.ShapeDtypeStruct((2, kv_rows, kv_cols), F32),
                    pl.BlockSpec((1, kv_rows, kv_cols), lambda i, p: (p[0], 0, 0)), (1,), "chip_sum_w_kv"),
         0, [(1, 2)]),
        (_blockwise(_four_sum, pos, [psum[2]] + [landed[2]] * 3,
                    [pl.BlockSpec((1, wo_rows, d), lambda i, p: (p[1], 0, 0))] + from_chip((1, 1, wo_rows, d), 4),
                    jax.ShapeDtypeStruct((2, wo_rows, d), F32),
                    pl.BlockSpec((1, wo_rows, d), lambda i, p: (p[0], 0, 0)), (1,), "chip_sum_w_out"),
         0, [(1, 2)]),
        (_blockwise(_four_sum, pos, [psum[3]] + [landed[3]] * 3,
                    [pl.BlockSpec((4, 1, pw_rows, GROUP), lambda i, p: (0, p[1], 0, 0))]
                    + from_chip((1, 4, 1, pw_rows, GROUP), 5),
                    jax.ShapeDtypeStruct((4, 2, pw_rows, GROUP), F32),
                    pl.BlockSpec((4, 1, pw_rows, GROUP), lambda i, p: (0, p[0], 0, 0)), (1,), "chip_sum_pool_w"),
         1, []),
    ])
    grads = {"w_in": joined[0].reshape(w_in.shape), "w_kv": joined[1].reshape(w_kv.shape),
             "w_out": joined[2].reshape(w_out.shape), "pool_w": joined[3].reshape(pool_w.shape)}

    small_local = dict(norm_pre=g_norm_pre, pool_scale=g_pool_scale, sgu_ln_g=g_ln_g, sgu_ln_b=g_ln_b,
                       sgu_w=g_sgu_w, sgu_b=g_sgu_b_t.T, mem_norm=g_mem_norm, branch_norm=g_branch_norm,
                       norm_post=g_norm_post)
    small_sum = _allreduce_small(_pack([small_local[n] for n in SMALL]))
    for n, g in zip(SMALL, _unpack(small_sum, [weights[n] for n in SMALL])):
        grads[n] = g

    delta, new_m, new_v = {}, {}, {}
    packed = [small_sum if src is grads else _pack([src[n] for n in SMALL]) for src in (weights, grads, mom1, mom2)]
    outs = _elementwise(_adamw, packed, [F32, F32, F32], "adamw_small")
    for dst, o in zip((delta, new_m, new_v), outs):
        for n, a in zip(SMALL, _unpack(o, [weights[n] for n in SMALL])):
            dst[n] = a
    for n in LARGE:
        cols = weights[n].shape[-1]
        outs = _elementwise(_adamw, [src[n].reshape(-1, cols) for src in (weights, grads, mom1, mom2)],
                            [F32, F32, F32], "adamw_" + n)
        for dst, o in zip((delta, new_m, new_v), outs):
            dst[n] = o.reshape(weights[n].shape)

    loss = lax.psum(loss_local[0, 0], ("x", "y", "c"))
    return (loss, grad_x[None], *[grads[n] for n in ORDER], *[delta[n] for n in ORDER],
            *[new_m[n] for n in ORDER], *[new_v[n] for n in ORDER])
```

```python
import functools

import jax
import jax.numpy as jnp
from jax import lax
from jax.experimental import pallas as pl
from jax.experimental.pallas import tpu as pltpu

F32 = jnp.float32
BF16 = jnp.bfloat16
EPS = 1e-6
MESH = pl.DeviceIdType.MESH
ANY = pl.BlockSpec(memory_space=pl.ANY)

POOL_WINDOWS = (2, 4, 8, 16)
GROUP = 256
HALO = 16
CHUNK = 128
N_SGU_HEADS = 8
N_ATT_HEADS = 4
ATT_DIM = 256
WIDTH = 1024
ATT_SCALE = 1.0 / 16.0

ADAM_LR = 0.001
ADAM_B1 = 0.9
ADAM_B2 = 0.999
ADAM_EPS = 1e-08
ADAM_WD = 0.01
ADAM_STEP = 10

VMEM_LIMIT = 60 * 1024 * 1024


def _params(n_grid_axes, vmem=VMEM_LIMIT):
    return pltpu.CompilerParams(dimension_semantics=("arbitrary",) * n_grid_axes, vmem_limit_bytes=vmem)


def _dot(a, b, dims):
    return lax.dot_general(a, b, (dims, ((), ())), preferred_element_type=F32)


NN = ((1,), (0,))
NT = ((1,), (1,))
TN = ((0,), (0,))


class _Rider:
    def __init__(self, inputs, out_shapes, n_sems, run, aliases=None):
        self.inputs, self.out_shapes, self.n_sems, self.run = list(inputs), list(out_shapes), n_sems, run
        self.aliases = aliases or {}


def _call(body, name, grid, in_specs, out_specs, out_shape, scratch_shapes, inputs, rider=None, prefetch=None,
          aliases=None, rider_refs=False):
    n_in, n_out, n_scr = len(in_specs), len(out_specs), len(scratch_shapes)
    r_in = len(rider.inputs) if rider else 0
    r_out = len(rider.out_shapes) if rider else 0
    n_pre = 0 if prefetch is None else 1

    def whole_body(*refs):
        pre, refs = refs[:n_pre], refs[n_pre:]
        ins, rider_ins = refs[:n_in], refs[n_in:n_in + r_in]
        refs = refs[n_in + r_in:]
        outs, rider_outs = refs[:n_out], refs[n_out:n_out + r_out]
        refs = refs[n_out + r_out:]
        scratch, sems = refs[:n_scr], refs[n_scr:]
        extra = {"rider_outs": rider_outs} if rider_refs else {}
        if rider is None:
            body(*pre, *ins, *outs, *scratch, **extra)
            return
        ids = [pl.program_id(ax) for ax in range(len(grid))]
        first = functools.reduce(lambda p, q: p & q, [i == 0 for i in ids])
        last = functools.reduce(lambda p, q: p & q, [i == g - 1 for i, g in zip(ids, grid)])

        @pl.when(first)
        def _():
            rider.run(rider_ins, rider_outs, *sems, True)

        body(*pre, *ins, *outs, *scratch, **extra)

        @pl.when(last)
        def _():
            rider.run(rider_ins, rider_outs, *sems, False)

    io_aliases = {n_pre + i: o for i, o in (aliases or {}).items()}
    scratch_all = list(scratch_shapes)
    if rider:
        io_aliases.update({n_pre + n_in + i: n_out + o for i, o in rider.aliases.items()})
        scratch_all += [pltpu.SemaphoreType.DMA((rider.n_sems,)), pltpu.SemaphoreType.DMA((rider.n_sems,))]
    specs = dict(grid=grid, in_specs=list(in_specs) + [ANY] * r_in, out_specs=tuple(out_specs) + (ANY,) * r_out,
                 scratch_shapes=scratch_all)
    if n_pre:
        specs = dict(grid_spec=pltpu.PrefetchScalarGridSpec(num_scalar_prefetch=1, **specs))
    outs = pl.pallas_call(
        whole_body, name=name, **specs,
        out_shape=tuple(out_shape) + tuple(rider.out_shapes if rider else ()),
        input_output_aliases=io_aliases, compiler_params=_params(len(grid)),
    )(*([prefetch] if n_pre else []), *inputs, *(rider.inputs if rider else []))
    return tuple(outs)


def _grad_rows(a, b, pos, col_of, m, tm, tn, tk, name, rider=None):
    k, n = a.shape[0], b.shape[1]
    nk = k // tk
    out_dtype, dims = BF16, TN
    a_spec = pl.BlockSpec((tk, tm), lambda i, j, kk, p: (kk, col_of(i, p)))
    b_spec = pl.BlockSpec((tk, tn), lambda i, j, kk, p: (kk, j))

    def body(pos_ref, a_ref, b_ref, o_ref, *acc):
        part = lambda: _dot(a_ref[...], b_ref[...], dims)
        if nk == 1:
            o_ref[...] = part().astype(out_dtype)
            return
        (acc_ref,) = acc
        kk = pl.program_id(2)

        @pl.when(kk == 0)
        def _():
            acc_ref[...] = part()

        @pl.when((kk > 0) & (kk < nk - 1))
        def _():
            acc_ref[...] += part()

        @pl.when(kk == nk - 1)
        def _():
            o_ref[...] = (acc_ref[...] + part()).astype(out_dtype)

    return _call(body, name, (m // tm, n // tn, nk), [a_spec, b_spec],
                 [pl.BlockSpec((tm, tn), lambda i, j, kk, p: (i, j))], [jax.ShapeDtypeStruct((m, n), out_dtype)],
                 [pltpu.VMEM((tm, tn), F32)] if nk > 1 else [], [a, b], rider, prefetch=pos)


def _proj_piece(chips, first, n_shards, src, g_pre, w_in, proj_in, n_cols, rider, tm, name):
    s, d = src.shape
    cols = n_cols // 4
    fused = g_pre is not None

    def body(chips_ref, *refs, rider_outs=()):
        refs = list(refs)
        src_ref = refs.pop(0)
        g_ref = refs.pop(0) if fused else None
        w_ref = refs.pop(0) if w_in is not None else rider_outs[0]
        if proj_in is not None:
            refs.pop(0)
        proj_ref = refs.pop(0)
        h_ref = refs.pop(0) if fused else None
        wbuf, sem = refs
        q, i = pl.program_id(0), pl.program_id(1)

        @pl.when(i == 0)
        def _():
            at = pl.multiple_of(chips_ref[first + q] * cols, 128)
            cp = pltpu.make_async_copy(w_ref.at[:, pl.ds(at, cols)], wbuf, sem)
            cp.start()
            cp.wait()

        if fused:
            xv = src_ref[...]
            r = lax.rsqrt(jnp.mean(xv * xv, axis=-1, keepdims=True) + EPS)
            h = (xv * r * g_ref[...]).astype(BF16)
            h_ref[...] = h
        else:
            h = src_ref[...]
        proj_ref[...] = _dot(h, wbuf[...], NN)

    row = lambda q, i, ch: (i, 0)
    inputs, in_specs = [src], [pl.BlockSpec((tm, d), row)]
    if fused:
        inputs.append(g_pre)
        in_specs.append(pl.BlockSpec((1, d), lambda q, i, ch: (0, 0)))
    if w_in is not None:
        inputs.append(w_in)
        in_specs.append(ANY)
    aliases = {}
    if proj_in is not None:
        aliases[len(inputs)] = 0
        inputs.append(proj_in)
        in_specs.append(ANY)
    out_specs = [pl.BlockSpec((tm, cols), lambda q, i, ch: (i, ch[first + q]))]
    out_shape = [jax.ShapeDtypeStruct((s, n_cols), F32)]
    if fused:
        assert n_shards == 1
        out_specs.append(pl.BlockSpec((tm, d), row))
        out_shape.append(jax.ShapeDtypeStruct((s, d), BF16))
    return _call(body, name, (n_shards, s // tm), in_specs, out_specs, out_shape,
                 [pltpu.VMEM((d, cols), BF16), pltpu.SemaphoreType.DMA(())], inputs, rider, prefetch=chips,
                 aliases=aliases, rider_refs=True)


def _kv_fwd(mem, g, w_kv):
    m, d = mem.shape

    def body(mem_ref, g_ref, w_ref, k_ref, v_ref):
        mv = mem_ref[...]
        r = lax.rsqrt(jnp.mean(mv * mv, axis=-1, keepdims=True) + EPS)
        mem_n = (mv * r * g_ref[...]).astype(BF16)
        kv = _dot(mem_n, w_ref[...], NN)
        k_ref[...] = kv[:, :WIDTH].astype(BF16)
        v_ref[...] = kv[:, WIDTH:].astype(BF16)

    return pl.pallas_call(
        body, name="kv_fwd",
        out_shape=(jax.ShapeDtypeStruct((m, WIDTH), BF16), jax.ShapeDtypeStruct((m, WIDTH), BF16)),
        compiler_params=_params(0),
    )(mem, g, w_kv)


def _kv_bwd(mem, g, w_kv, dk, dv):
    m, d = mem.shape
    n = w_kv.shape[1]
    col = 512

    def body(mem_ref, g_ref, w_ref, dk_ref, dv_ref, dw_ref, dg_ref):
        mv = mem_ref[...]
        r = lax.rsqrt(jnp.mean(mv * mv, axis=-1, keepdims=True) + EPS)
        mem_hat = mv * r
        mem_n = (mem_hat * g_ref[...]).astype(BF16)
        dkv = jnp.concatenate([dk_ref[...], dv_ref[...]], axis=1).astype(BF16)
        for j in range(n // col):
            dw_ref[:, j * col:(j + 1) * col] = _dot(mem_n, dkv[:, j * col:(j + 1) * col], TN).astype(BF16)
        dmem_n = _dot(dkv, w_ref[...], NT)
        dg_ref[...] = jnp.sum(dmem_n * mem_hat, axis=0, keepdims=True)

    return pl.pallas_call(
        body, name="kv_bwd",
        out_shape=(jax.ShapeDtypeStruct((d, n), BF16), jax.ShapeDtypeStruct((1, d), F32)),
        compiler_params=_params(0),
    )(mem, g, w_kv, dk, dv)


def _sigmoid(x):
    return 1.0 / (1.0 + jnp.exp(-x))


def _inv_counts(t0, t):
    pos = (t0 + lax.broadcasted_iota(jnp.int32, (t, 1), 0) + 1).astype(F32)
    return [1.0 / jnp.minimum(pos, float(w)) for w in POOL_WINDOWS]


def _window_sums(ext, t, backward):
    n = t + HALO
    parts = []
    for gi, w in enumerate(POOL_WINDOWS):
        s = ext[:, gi * GROUP:(gi + 1) * GROUP]
        k = 1
        while k < w:
            s = s + pltpu.roll(s, (n - k) if backward else k, axis=0)
            k *= 2
        parts.append(s[:t] if backward else s[HALO:])
    return parts


def _pool_fwd(xa, halo, inv, pool_w):
    t = xa.shape[0]
    sums = _window_sums(jnp.concatenate([halo, xa], axis=0), t, backward=False)
    d = jnp.concatenate([sums[gi] * inv[gi] - xa[:, gi * GROUP:(gi + 1) * GROUP] for gi in range(4)], axis=1)
    d = d.astype(BF16)
    y = jnp.concatenate([_dot(d[:, gi * GROUP:(gi + 1) * GROUP], pool_w[gi], NN) for gi in range(4)], axis=1)
    return d, y


def _layernorm_fwd(v):
    mu = jnp.mean(v, axis=-1, keepdims=True)
    xc = v - mu
    rstd = lax.rsqrt(jnp.mean(xc * xc, axis=-1, keepdims=True) + EPS)
    return xc * rstd, rstd


def _tril_mask(transposed):
    r = lax.broadcasted_iota(jnp.int32, (CHUNK, CHUNK), 0)
    c = lax.broadcasted_iota(jnp.int32, (CHUNK, CHUNK), 1)
    return (r <= c) if transposed else (r >= c)


def _sgu_mix(w_ref, vals, transposed):
    t = vals.shape[0]
    mask = _tril_mask(transposed)
    ws = [jnp.where(mask, w_ref[h], 0.0).astype(BF16) for h in range(N_SGU_HEADS)]
    rows = []
    for ci in range(t // CHUNK):
        blk = vals[ci * CHUNK:(ci + 1) * CHUNK]
        rows.append(jnp.concatenate(
            [_dot(ws[h], blk[:, h * CHUNK:(h + 1) * CHUNK], NN) for h in range(N_SGU_HEADS)], axis=1))
    return jnp.concatenate(rows, axis=0)


def _attn_fwd(q, k, v):
    ps, os_ = [], []
    for h in range(N_ATT_HEADS):
        sl = slice(h * ATT_DIM, (h + 1) * ATT_DIM)
        s = _dot(q[:, sl], k[:, sl], NT) * ATT_SCALE
        s = s - jnp.max(s, axis=-1, keepdims=True)
        e = jnp.exp(s)
        p = e * (1.0 / jnp.sum(e, axis=-1, keepdims=True))
        ps.append(p)
        os_.append(_dot(p.astype(BF16), v[:, sl], NN))
    return ps, jnp.concatenate(os_, axis=1)


def _rms_branch(y_pre):
    r = lax.rsqrt(jnp.mean(y_pre * y_pre, axis=-1, keepdims=True) + EPS)
    return y_pre * r, r


def _branch_specs(t, n_tiles, order):
    width_in = 7 * WIDTH
    tile = lambda i: order(i)
    per_halo = t // HALO
    const2 = lambda i: (0, 0)
    const3 = lambda i: (0, 0, 0)
    return [
        pl.BlockSpec((t, width_in), lambda i: (tile(i), 0)),
        pl.BlockSpec((HALO, WIDTH), lambda i: (jnp.maximum(tile(i) * per_halo - 1, 0), 0)),
        pl.BlockSpec((4, GROUP, GROUP), const3),
        pl.BlockSpec((1, WIDTH), const2),
        pl.BlockSpec((1, WIDTH), const2),
        pl.BlockSpec((1, WIDTH), const2),
        pl.BlockSpec((N_SGU_HEADS, CHUNK, CHUNK), const3),
        pl.BlockSpec((CHUNK, WIDTH), const2),
        pl.BlockSpec((MEM_ROWS, WIDTH), const2),
        pl.BlockSpec((MEM_ROWS, WIDTH), const2),
        pl.BlockSpec((1, 3 * WIDTH), const2),
    ]


MEM_ROWS = 256


def _branches_fwd(proj, pool_w, pool_scale, ln_g, ln_b, sgu_w, bias_full, k, v, branch_norm, t, rider=None):
    s = proj.shape[0]
    n_tiles = s // t

    def body(proj_ref, halo_ref, pw_ref, ps_ref, lg_ref, lb_ref, sw_ref, sb_ref, k_ref, v_ref, bn_ref, y_ref):
        i = pl.program_id(0)
        col = lambda j: proj_ref[:, j * WIDTH:(j + 1) * WIDTH]
        bn = bn_ref[...]
        halo = jnp.where(i > 0, halo_ref[...], 0.0)
        _, y_pool = _pool_fwd(col(0), halo, _inv_counts(i * t, t), pw_ref[...])
        ga = col(1)
        ya = y_pool * ps_ref[...] * (ga * _sigmoid(ga))
        y_ref[:, 0:WIDTH] = (_rms_branch(ya)[0] * bn[:, 0:WIDTH]).astype(BF16)
        vhat, _ = _layernorm_fwd(col(3))
        vn = (vhat * lg_ref[...] + lb_ref[...]).astype(BF16)
        z = _sgu_mix(sw_ref, vn, transposed=False) + jnp.tile(sb_ref[...], (t // CHUNK, 1))
        gb = col(4)
        yb = col(2) * z * (gb * _sigmoid(gb))
        y_ref[:, WIDTH:2 * WIDTH] = (_rms_branch(yb)[0] * bn[:, WIDTH:2 * WIDTH]).astype(BF16)
        _, o = _attn_fwd(col(5).astype(BF16), k_ref[...], v_ref[...])
        gc = col(6)
        yc = o * (gc * _sigmoid(gc))
        y_ref[:, 2 * WIDTH:] = (_rms_branch(yc)[0] * bn[:, 2 * WIDTH:]).astype(BF16)

    return _call(body, "branches_fwd", (n_tiles,), _branch_specs(t, n_tiles, lambda i: i),
                 [pl.BlockSpec((t, 3 * WIDTH), lambda i: (i, 0))], [jax.ShapeDtypeStruct((s, 3 * WIDTH), BF16)], [],
                 [proj, proj, pool_w, pool_scale, ln_g, ln_b, sgu_w, bias_full, k, v, branch_norm], rider)


def _branches_bwd(proj, dy, pool_w, pool_scale, ln_g, ln_b, sgu_w, sgu_wt, bias_full, k, v, branch_norm, t, rider=None):
    s = proj.shape[0]
    n_tiles = s // t
    n_chunks = t // CHUNK
    order = lambda i: n_tiles - 1 - i

    def body(proj_ref, halo_ref, pw_ref, ps_ref, lg_ref, lb_ref, sw_ref, sb_ref, k_ref, v_ref, bn_ref,
             swt_ref, dy_ref,
             dproj_ref, dpw_ref, dps_ref, dlg_ref, dlb_ref, dsw_ref, dsb_ref, dbn_ref, dk_ref, dv_ref,
             carry_ref, dbias_ref):
        step = pl.program_id(0)
        i = order(step)

        @pl.when(step == 0)
        def _():
            for ref in (dpw_ref, dps_ref, dlg_ref, dlb_ref, dsw_ref, dbn_ref, dk_ref, dv_ref, carry_ref, dbias_ref):
                ref[...] = jnp.zeros(ref.shape, ref.dtype)

        col = lambda j: proj_ref[:, j * WIDTH:(j + 1) * WIDTH]
        bn = bn_ref[...]

        def norm_bwd(y_pre, sl):
            yhat, r = _rms_branch(y_pre)
            dyv = dy_ref[:, sl].astype(F32)
            dbn_ref[:, sl] += jnp.sum(dyv * yhat, axis=0, keepdims=True)
            dyhat = dyv * bn[:, sl]
            return r * (dyhat - yhat * jnp.mean(dyhat * yhat, axis=-1, keepdims=True))

        def gate(gv):
            sg = _sigmoid(gv)
            return gv * sg, sg * (1.0 + gv * (1.0 - sg))

        inv = _inv_counts(i * t, t)
        halo = jnp.where(i > 0, halo_ref[...], 0.0)
        pw = pw_ref[...]
        d, y_pool = _pool_fwd(col(0), halo, inv, pw)
        scale = ps_ref[...]
        silu_a, dsilu_a = gate(col(1))
        pa = y_pool * scale
        dya = norm_bwd(pa * silu_a, slice(0, WIDTH))
        dproj_ref[:, WIDTH:2 * WIDTH] = (dya * pa * dsilu_a).astype(BF16)
        dpa = dya * silu_a
        dps_ref[...] += jnp.sum(dpa * y_pool, axis=0, keepdims=True)
        dy_pool = (dpa * scale).astype(BF16)
        dd_parts, ddc_parts = [], []
        for gi in range(4):
            sl = slice(gi * GROUP, (gi + 1) * GROUP)
            dpw_ref[gi] += _dot(d[:, sl], dy_pool[:, sl], TN)
            dd = _dot(dy_pool[:, sl], pw[gi], NT)
            dd_parts.append(dd)
            ddc_parts.append(dd * inv[gi])
        ddc = jnp.concatenate(ddc_parts, axis=1)
        sums = _window_sums(jnp.concatenate([ddc, carry_ref[...]], axis=0), t, backward=True)
        carry_ref[...] = ddc[:HALO]
        dproj_ref[:, 0:WIDTH] = jnp.concatenate([sums[gi] - dd_parts[gi] for gi in range(4)], axis=1).astype(BF16)

        vhat, rstd = _layernorm_fwd(col(3))
        lg = lg_ref[...]
        vn = (vhat * lg + lb_ref[...]).astype(BF16)
        z = _sgu_mix(sw_ref, vn, transposed=False) + jnp.tile(sb_ref[...], (n_chunks, 1))
        u = col(2)
        silu_b, dsilu_b = gate(col(4))
        uz = u * z
        dyb = norm_bwd(uz * silu_b, slice(WIDTH, 2 * WIDTH))
        dproj_ref[:, 4 * WIDTH:5 * WIDTH] = (dyb * uz * dsilu_b).astype(BF16)
        duz = dyb * silu_b
        dproj_ref[:, 2 * WIDTH:3 * WIDTH] = (duz * z).astype(BF16)
        dz = duz * u
        dz_b = dz.astype(BF16)
        for ci in range(n_chunks):
            rows = slice(ci * CHUNK, (ci + 1) * CHUNK)
            dbias_ref[...] += dz[rows]
            for h in range(N_SGU_HEADS):
                sl = slice(h * CHUNK, (h + 1) * CHUNK)
                dsw_ref[h] += _dot(dz_b[rows, sl], vn[rows, sl], NT)
        dvn = _sgu_mix(swt_ref, dz_b, transposed=True)
        dlg_ref[...] += jnp.sum(dvn * vhat, axis=0, keepdims=True)
        dlb_ref[...] += jnp.sum(dvn, axis=0, keepdims=True)
        dvhat = dvn * lg
        dvb = rstd * (dvhat - jnp.mean(dvhat, axis=-1, keepdims=True)
                      - vhat * jnp.mean(dvhat * vhat, axis=-1, keepdims=True))
        dproj_ref[:, 3 * WIDTH:4 * WIDTH] = dvb.astype(BF16)

        q = col(5).astype(BF16)
        kv_k, kv_v = k_ref[...], v_ref[...]
        ps, o = _attn_fwd(q, kv_k, kv_v)
        silu_c, dsilu_c = gate(col(6))
        dyc = norm_bwd(o * silu_c, slice(2 * WIDTH, 3 * WIDTH))
        dproj_ref[:, 6 * WIDTH:7 * WIDTH] = (dyc * o * dsilu_c).astype(BF16)
        do = (dyc * silu_c).astype(BF16)
        dq_parts = []
        for h in range(N_ATT_HEADS):
            sl = slice(h * ATT_DIM, (h + 1) * ATT_DIM)
            p = ps[h]
            dp = _dot(do[:, sl], kv_v[:, sl], NT)
            ds = (p * (dp - jnp.sum(p * dp, axis=-1, keepdims=True)) * ATT_SCALE).astype(BF16)
            dq_parts.append(_dot(ds, kv_k[:, sl], NN))
            dk_ref[:, sl] += _dot(ds, q[:, sl], TN)
            dv_ref[:, sl] += _dot(p.astype(BF16), do[:, sl], TN)
        dproj_ref[:, 5 * WIDTH:6 * WIDTH] = jnp.concatenate(dq_parts, axis=1).astype(BF16)

        @pl.when(step == n_tiles - 1)
        def _():
            keep = _tril_mask(transposed=False)
            for h in range(N_SGU_HEADS):
                dsw_ref[h] = jnp.where(keep, dsw_ref[h], 0.0)
            dsb_ref[...] = jnp.concatenate(
                [jnp.sum(dbias_ref[:, h * CHUNK:(h + 1) * CHUNK], axis=1, keepdims=True)
                 for h in range(N_SGU_HEADS)], axis=1)

    const2 = lambda i: (0, 0)
    const3 = lambda i: (0, 0, 0)
    out_shapes = (
        jax.ShapeDtypeStruct((s, 7 * WIDTH), BF16),
        jax.ShapeDtypeStruct((4, GROUP, GROUP), F32),
        jax.ShapeDtypeStruct((1, WIDTH), F32),
        jax.ShapeDtypeStruct((1, WIDTH), F32),
        jax.ShapeDtypeStruct((1, WIDTH), F32),
        jax.ShapeDtypeStruct((N_SGU_HEADS, CHUNK, CHUNK), F32),
        jax.ShapeDtypeStruct((CHUNK, N_SGU_HEADS), F32),
        jax.ShapeDtypeStruct((1, 3 * WIDTH), F32),
        jax.ShapeDtypeStruct((MEM_ROWS, WIDTH), F32),
        jax.ShapeDtypeStruct((MEM_ROWS, WIDTH), F32),
    )
    out_specs = (
        pl.BlockSpec((t, 7 * WIDTH), lambda i: (order(i), 0)),
        pl.BlockSpec((4, GROUP, GROUP), const3),
        pl.BlockSpec((1, WIDTH), const2),
        pl.BlockSpec((1, WIDTH), const2),
        pl.BlockSpec((1, WIDTH), const2),
        pl.BlockSpec((N_SGU_HEADS, CHUNK, CHUNK), const3),
        pl.BlockSpec((CHUNK, N_SGU_HEADS), const2),
        pl.BlockSpec((1, 3 * WIDTH), const2),
        pl.BlockSpec((MEM_ROWS, WIDTH), const2),
        pl.BlockSpec((MEM_ROWS, WIDTH), const2),
    )
    in_specs = _branch_specs(t, n_tiles, order) + [
        pl.BlockSpec((N_SGU_HEADS, CHUNK, CHUNK), const3),
        pl.BlockSpec((t, 3 * WIDTH), lambda i: (order(i), 0)),
    ]
    return _call(body, "branches_bwd", (n_tiles,), in_specs, out_specs, out_shapes,
                 [pltpu.VMEM((HALO, WIDTH), F32), pltpu.VMEM((CHUNK, WIDTH), F32)],
                 [proj, proj, pool_w, pool_scale, ln_g, ln_b, sgu_w, bias_full, k, v, branch_norm, sgu_wt, dy], rider)


def _out_loss(y, w_out, x, target, g_post, tm):
    s, d = x.shape
    e_w = y.shape[1]
    n_tiles = s // tm

    def body(y_ref, w_ref, x_ref, t_ref, g_ref, loss_ref, dz_ref, dout_ref, dy_ref, dg_ref, sq_ref):
        i = pl.program_id(0)

        @pl.when(i == 0)
        def _():
            sq_ref[...] = jnp.zeros(sq_ref.shape, F32)
            dg_ref[...] = jnp.zeros(dg_ref.shape, F32)

        w = w_ref[...]
        out = _dot(y_ref[...], w, NN)
        r = lax.rsqrt(jnp.mean(out * out, axis=-1, keepdims=True) + EPS)
        outn = out * r
        g = g_ref[...]
        err = (x_ref[...] + outn * g) - t_ref[...]
        sq_ref[...] += jnp.sum(err * err, axis=0, keepdims=True)
        dz = err * (1.0 / d)
        dz_ref[...] = dz
        dg_ref[...] += jnp.sum(dz * outn, axis=0, keepdims=True)
        doutn = dz * g
        dout = (r * (doutn - outn * jnp.mean(doutn * outn, axis=-1, keepdims=True))).astype(BF16)
        dout_ref[...] = dout
        dy_ref[...] = _dot(dout, w, NT).astype(BF16)

        @pl.when(i == n_tiles - 1)
        def _():
            loss_ref[...] = 0.5 * jnp.sum(sq_ref[...], axis=1, keepdims=True) * (1.0 / d)

    row = lambda i: (i, 0)
    const2 = lambda i: (0, 0)
    return pl.pallas_call(
        body, name="out_loss", grid=(n_tiles,),
        in_specs=[
            pl.BlockSpec((tm, e_w), row),
            pl.BlockSpec((e_w, d), const2, pipeline_mode=pl.Buffered(1)),
            pl.BlockSpec((tm, d), row),
            pl.BlockSpec((tm, d), row),
            pl.BlockSpec((1, d), const2),
        ],
        out_specs=(
            pl.BlockSpec((1, 1), const2),
            pl.BlockSpec((tm, d), row),
            pl.BlockSpec((tm, d), row),
            pl.BlockSpec((tm, e_w), row),
            pl.BlockSpec((1, d), const2),
        ),
        out_shape=(
            jax.ShapeDtypeStruct((1, 1), F32),
            jax.ShapeDtypeStruct((s, d), F32),
            jax.ShapeDtypeStruct((s, d), BF16),
            jax.ShapeDtypeStruct((s, e_w), BF16),
            jax.ShapeDtypeStruct((1, d), F32),
        ),
        scratch_shapes=[pltpu.VMEM((1, d), F32)],
        compiler_params=_params(1),
    )(y, w_out, x, target, g_post)


def _dx_call(dproj, w_in, x, dz, g_pre, tm, tk, rider=None):
    s, d = x.shape
    k_total = dproj.shape[1]
    nk = k_total // tk
    n_tiles = s // tm

    def body(dp_ref, w_ref, x_ref, dz_ref, g_ref, dx_ref, dg_ref, acc_ref):
        i, kk = pl.program_id(0), pl.program_id(1)
        part = lambda: _dot(dp_ref[...], w_ref[...], NT)

        @pl.when(kk == 0)
        def _():
            acc_ref[...] = part()

        @pl.when((kk > 0) & (kk < nk - 1))
        def _():
            acc_ref[...] += part()

        @pl.when((i == 0) & (kk == 0))
        def _():
            dg_ref[...] = jnp.zeros(dg_ref.shape, F32)

        @pl.when(kk == nk - 1)
        def _():
            dh = acc_ref[...] + part()
            xv = x_ref[...]
            r = lax.rsqrt(jnp.mean(xv * xv, axis=-1, keepdims=True) + EPS)
            xhat = xv * r
            dg_ref[...] += jnp.sum(dh * xhat, axis=0, keepdims=True)
            dxhat = dh * g_ref[...]
            dx_ref[...] = dz_ref[...] + r * (dxhat - xhat * jnp.mean(dxhat * xhat, axis=-1, keepdims=True))

    row = lambda i, kk: (i, 0)
    const2 = lambda i, kk: (0, 0)
    return _call(
        body, "dx", (n_tiles, nk),
        [
            pl.BlockSpec((tm, tk), lambda i, kk: (i, kk)),
            pl.BlockSpec((d, tk), lambda i, kk: (0, kk)),
            pl.BlockSpec((tm, d), row),
            pl.BlockSpec((tm, d), row),
            pl.BlockSpec((1, d), const2),
        ],
        [pl.BlockSpec((tm, d), row), pl.BlockSpec((1, d), const2)],
        [jax.ShapeDtypeStruct((s, d), F32), jax.ShapeDtypeStruct((1, d), F32)],
        [pltpu.VMEM((tm, d), F32)], [dproj, w_in, x, dz, g_pre], rider)


def _rows_tile(rows, cols, n_arrays, itemsize=4):
    budget = 24 * 1024 * 1024 // (2 * n_arrays * cols * itemsize)
    if rows <= budget:
        return rows
    best = None
    for cand in range(16, rows + 1, 16):
        if rows % cand == 0 and cand <= max(budget, 16):
            best = cand
    return best if best is not None else rows


def _elementwise(fn, inputs, out_dtypes, name):
    rows, cols = inputs[0].shape
    tr = _rows_tile(rows, cols, len(inputs) + len(out_dtypes))
    n_in = len(inputs)

    def body(*refs):
        outs = fn(*[r[...] for r in refs[:n_in]])
        for o_ref, o in zip(refs[n_in:], outs):
            o_ref[...] = o.astype(o_ref.dtype)

    spec = pl.BlockSpec((tr, cols), lambda i: (i, 0))
    return pl.pallas_call(
        body, name=name, grid=(rows // tr,),
        in_specs=[spec] * n_in, out_specs=tuple([spec] * len(out_dtypes)),
        out_shape=tuple(jax.ShapeDtypeStruct((rows, cols), dt) for dt in out_dtypes),
        compiler_params=_params(1),
    )(*inputs)


def _blockwise(fn, pos, inputs, in_specs, out_shape, out_spec, grid, name):
    n_in = len(inputs)

    def body(pos_ref, *refs):
        o_ref = refs[n_in]
        (out,) = fn(*[r[...].reshape(o_ref.shape) for r in refs[:n_in]])
        o_ref[...] = out.astype(o_ref.dtype)

    return pl.pallas_call(
        body, name=name,
        grid_spec=pltpu.PrefetchScalarGridSpec(num_scalar_prefetch=1, grid=grid, in_specs=in_specs,
                                               out_specs=out_spec),
        out_shape=out_shape,
        compiler_params=_params(len(grid)),
    )(pos, *inputs)


def _cast_copy(x):
    return (x,)


def _pair_sum(mine, theirs):
    return ((mine.astype(F32) + theirs.astype(F32)),)


def _four_sum(own, t0, t1, t2):
    return ((((own.astype(F32) + t0.astype(F32)) + t1.astype(F32)) + t2.astype(F32)),)


def _adamw(w, g, m, v):
    m = ADAM_B1 * m + (1.0 - ADAM_B1) * g
    v = ADAM_B2 * v + (1.0 - ADAM_B2) * jnp.square(g)
    m_hat = m / (1.0 - ADAM_B1 ** ADAM_STEP)
    v_hat = v / (1.0 - ADAM_B2 ** ADAM_STEP)
    delta = -ADAM_LR * (m_hat / (jnp.sqrt(v_hat) + ADAM_EPS) + ADAM_WD * w)
    return delta, m, v


def _place():
    x, y, c = lax.axis_index("x"), lax.axis_index("y"), lax.axis_index("c")
    chips = [(1 - x, y), (x, 1 - y), (1 - x, 1 - y)]
    return x, y, c, chips


def _remote(src, dst, send_sem, recv_sem, to):
    return pltpu.make_async_remote_copy(src_ref=src, dst_ref=dst, send_sem=send_sem, recv_sem=recv_sem,
                                        device_id=to, device_id_type=MESH)


def _split(ref, plan):
    views = [ref]
    for axis, parts in plan:
        size = ref.shape[axis] // parts
        assert size * parts == ref.shape[axis]
        views = [v.at[tuple(pl.ds(q * size, size) if i == axis else slice(None) for i in range(len(ref.shape)))]
                 for v in views for q in range(parts)]
    return views


def _remote_in_parts(src, dst, send_sem, recv_sem, to, plan):
    for s, d in zip(_split(src, plan), _split(dst, plan)):
        _remote(s, d, send_sem, recv_sem, to).start()
    return _remote(src, dst, send_sem, recv_sem, to)


def _local_in_parts(src, dst, sem, plan):
    for s, d in zip(_split(src, plan), _split(dst, plan)):
        pltpu.make_async_copy(s, d, sem).start()
    return pltpu.make_async_copy(src, dst, sem)


def _hbm_call(body, name, inputs, out_shapes, scratch, aliases=None):
    return pl.pallas_call(
        body, name=name,
        in_specs=[ANY] * len(inputs), out_specs=tuple([ANY] * len(out_shapes)), out_shape=tuple(out_shapes),
        scratch_shapes=scratch, input_output_aliases=aliases or {},
        compiler_params=pltpu.CompilerParams(has_side_effects=True),
    )(*inputs)


def _gather_rider(fulls, kinds, peers=(0, 1, 2)):
    n = len(fulls)

    def full_half(a, ref, chip, cc):
        if a == 0:
            rows, cols = ref.shape[0] // 2, ref.shape[1] // 4
            return ref.at[pl.ds(cc * rows, rows), pl.ds(pl.multiple_of(chip * cols, 128), cols)]
        if a == 3:
            rows = ref.shape[1] // 8
            return ref.at[:, pl.ds(pl.multiple_of((2 * chip + cc) * rows, 16), rows), :]
        rows = ref.shape[0] // 8
        return ref.at[pl.ds(pl.multiple_of((2 * chip + cc) * rows, 16), rows), :]

    def run(in_refs, full_refs, send_sems, recv_sems, start):
        x, y, c, chips = _place()
        me = 2 * x + y
        sibling = (x, y, 1 - c)
        plans = [[(0, 8)], [(0, 2)], [(0, 2)], []]
        chips = [(p, chips[p]) for p in peers]
        if start:
            for p, chip in chips:
                for a in range(n):
                    mine = full_half(kinds[a], full_refs[a], me, c)
                    _remote_in_parts(mine, mine, send_sems.at[6 * a + p], recv_sems.at[6 * a + p], (*chip, c),
                                     plans[kinds[a]])
            return
        passed_on = []
        for p, chip in chips:
            them = 2 * chip[0] + chip[1]
            for a in range(n):
                landed = full_half(kinds[a], full_refs[a], them, c)
                _remote(landed, landed, send_sems.at[6 * a + p], recv_sems.at[6 * a + p], (*chip, c)).wait_recv()
                passed_on.append(_remote_in_parts(landed, landed, send_sems.at[6 * a + 3 + p],
                                                  recv_sems.at[6 * a + 3 + p], sibling, plans[kinds[a]]))
        for p, chip in chips:
            them = 2 * chip[0] + chip[1]
            for a in range(n):
                passed = full_half(kinds[a], full_refs[a], them, 1 - c)
                _remote(passed, passed, send_sems.at[6 * a + 3 + p], recv_sems.at[6 * a + 3 + p], sibling).wait_recv()
                mine = full_half(kinds[a], full_refs[a], me, c)
                _remote(mine, mine, send_sems.at[6 * a + p], recv_sems.at[6 * a + p], (*chip, c)).wait_send()
        for cp in passed_on:
            cp.wait_send()

    return _Rider(fulls, [jax.ShapeDtypeStruct(f.shape, f.dtype) for f in fulls], 6 * n, run,
                  aliases={a: a for a in range(n)})


def _exchange_halves(grads, name):
    n = len(grads)
    arrays = [g for g, _, _ in grads]
    out_shapes = [jax.ShapeDtypeStruct(tuple(1 if i == ax else dim for i, dim in enumerate(g.shape)), g.dtype)
                  for g, ax, _ in grads]

    def half(ref, ax, cc):
        idx = tuple(pl.ds(cc, 1) if i == ax else slice(None) for i in range(len(ref.shape)))
        return ref.at[idx]

    def body(*refs):
        in_refs, out_refs = refs[:n], refs[n:2 * n]
        send_sems, recv_sems = refs[2 * n:]
        x, y, c, _ = _place()
        sibling = (x, y, 1 - c)
        copies = [_remote_in_parts(half(in_refs[a], grads[a][1], 1 - c), out_refs[a], send_sems.at[a],
                                   recv_sems.at[a], sibling, grads[a][2]) for a in range(n)]
        for rem in copies:
            rem.wait()

    return _hbm_call(body, name, arrays, out_shapes,
                     [pltpu.SemaphoreType.DMA((n,)), pltpu.SemaphoreType.DMA((n,))])


def _exchange_rider(arrays, plans):
    n = len(arrays)

    def run(in_refs, out_refs, send_sems, recv_sems, start):
        x, y, c, _ = _place()
        sibling = (x, y, 1 - c)
        for a in range(n):
            sems = (send_sems.at[a], recv_sems.at[a])
            if start:
                _remote_in_parts(in_refs[a], out_refs[a], *sems, sibling, plans[a])
            else:
                _remote(in_refs[a], out_refs[a], *sems, sibling).wait()

    return _Rider(arrays, [jax.ShapeDtypeStruct(a.shape, a.dtype) for a in arrays], n, run)


def _scatter_rider(parts):
    n = len(parts)
    arrays = [p for p, _, _ in parts]

    def block_shape(p, ax):
        if ax == len(p.shape) - 1:
            return p.shape[:-1] + (p.shape[-1] // 4,)
        return tuple(1 if i == ax else dim for i, dim in enumerate(p.shape))

    out_shapes = [jax.ShapeDtypeStruct((3,) + block_shape(p, ax), p.dtype) for p, ax, _ in parts]

    def block(ref, ax, chip):
        rank = len(ref.shape)
        if ax == rank - 1:
            cols = ref.shape[-1] // 4
            last = pl.ds(pl.multiple_of(chip * cols, 128), cols)
            return ref.at[tuple([slice(None)] * (rank - 1) + [last])]
        return ref.at[tuple(pl.ds(chip, 1) if i == ax else slice(None) for i in range(rank))]

    def run(in_refs, out_refs, send_sems, recv_sems, start):
        x, y, c, chips = _place()
        for a in range(n):
            ax, plan = parts[a][1], parts[a][2]
            for p, chip in enumerate(chips):
                src, dst = block(in_refs[a], ax, 2 * chip[0] + chip[1]), out_refs[a].at[p]
                sems = (send_sems.at[3 * a + p], recv_sems.at[3 * a + p])
                if start:
                    _remote_in_parts(src, dst, *sems, (*chip, c), plan)
                else:
                    _remote(src, dst, *sems, (*chip, c)).wait()

    return _Rider(arrays, out_shapes, 3 * n, run)


def _join_halves(joined):
    n = len(joined)
    arrays = [j for j, _, _ in joined]

    def body(*refs):
        out_refs = refs[n:2 * n]
        send_sems, recv_sems = refs[2 * n:]
        x, y, c, _ = _place()
        sibling = (x, y, 1 - c)

        def half(a, cc):
            rank = len(out_refs[a].shape)
            return out_refs[a].at[tuple(pl.ds(cc, 1) if i == joined[a][1] else slice(None) for i in range(rank))]

        sends = [_remote_in_parts(half(a, c), half(a, c), send_sems.at[a], recv_sems.at[a], sibling, joined[a][2])
                 for a in range(n)]
        for a, rem in enumerate(sends):
            rem.wait_send()
            _remote(half(a, 1 - c), half(a, 1 - c), send_sems.at[a], recv_sems.at[a], sibling).wait_recv()

    return _hbm_call(body, "join_halves", arrays, [jax.ShapeDtypeStruct(j.shape, j.dtype) for j in arrays],
                     [pltpu.SemaphoreType.DMA((n,)), pltpu.SemaphoreType.DMA((n,))],
                     aliases={a: a for a in range(n)})


def _allreduce_small(packed):
    rows, lanes = packed.shape
    half = rows // 2

    def body(in_ref, out_ref, pair_ref, gath_ref, send_sems, recv_sems):
        x, y, c, chips = _place()
        me = 2 * x + y
        sibling = (x, y, 1 - c)
        mine = pl.ds(pl.multiple_of(c * half, 8), half)
        theirs = pl.ds(pl.multiple_of((1 - c) * half, 8), half)
        to_sib = _remote(in_ref.at[theirs], pair_ref, send_sems.at[0], recv_sems.at[0], sibling)
        to_sib.start()
        to_sib.wait()
        gath_ref[me] = in_ref[mine] + pair_ref[...]
        sends = [_remote(gath_ref.at[me], gath_ref.at[me], send_sems.at[1 + p], recv_sems.at[1 + p], (*chip, c))
                 for p, chip in enumerate(chips)]
        for cp in sends:
            cp.start()
        for p, chip in enumerate(chips):
            slot = gath_ref.at[2 * chip[0] + chip[1]]
            _remote(slot, slot, send_sems.at[1 + p], recv_sems.at[1 + p], (*chip, c)).wait_recv()
        for cp in sends:
            cp.wait_send()
        out_ref[mine] = ((gath_ref[0] + gath_ref[1]) + gath_ref[2]) + gath_ref[3]
        back = _remote(out_ref.at[mine], out_ref.at[mine], send_sems.at[4], recv_sems.at[4], sibling)
        back.start()
        back.wait_send()
        _remote(out_ref.at[theirs], out_ref.at[theirs], send_sems.at[4], recv_sems.at[4], sibling).wait_recv()

    vmem = pl.BlockSpec(memory_space=pltpu.VMEM)
    return pl.pallas_call(
        body, name="allreduce_small",
        in_specs=[vmem], out_specs=vmem, out_shape=jax.ShapeDtypeStruct((rows, lanes), F32),
        scratch_shapes=[pltpu.VMEM((half, lanes), F32), pltpu.VMEM((4, half, lanes), F32),
                        pltpu.SemaphoreType.DMA((5,)), pltpu.SemaphoreType.DMA((5,))],
        compiler_params=pltpu.CompilerParams(has_side_effects=True, vmem_limit_bytes=32 * 1024 * 1024),
    )(packed)


SMALL = ("norm_pre", "pool_scale", "sgu_ln_g", "sgu_ln_b", "sgu_w", "sgu_b", "mem_norm", "branch_norm", "norm_post")
LARGE = ("w_in", "pool_w", "w_kv", "w_out")
ORDER = ("norm_pre", "w_in", "pool_w", "pool_scale", "sgu_ln_g", "sgu_ln_b", "sgu_w", "sgu_b", "mem_norm", "w_kv",
         "branch_norm", "w_out", "norm_post")


def _pack(arrays, extra=()):
    rows = [a.reshape(-1, 128) for a in arrays] + list(extra)
    pad = -sum(r.shape[0] for r in rows) % 16
    return jnp.concatenate(rows + ([jnp.zeros((pad, 128), F32)] if pad else []), axis=0)


def _unpack(packed, like):
    out, row = [], 0
    for a in like:
        rows = a.size // 128
        out.append(packed[row:row + rows].reshape(a.shape))
        row += rows
    return out


def kernel(x, mem, norm_pre, w_in, pool_w, pool_scale, sgu_ln_g, sgu_ln_b, sgu_w, sgu_b, mem_norm, w_kv, branch_norm, w_out, norm_post, loss_target, m_norm_pre, m_w_in, m_pool_w, m_pool_scale, m_sgu_ln_g, m_sgu_ln_b, m_sgu_w, m_sgu_b, m_mem_norm, m_w_kv, m_branch_norm, m_w_out, m_norm_post, v_norm_pre, v_w_in, v_pool_w, v_pool_scale, v_sgu_ln_g, v_sgu_ln_b, v_sgu_w, v_sgu_b, v_mem_norm, v_w_kv, v_branch_norm, v_w_out, v_norm_post):
    weights = dict(norm_pre=norm_pre, w_in=w_in, pool_w=pool_w, pool_scale=pool_scale, sgu_ln_g=sgu_ln_g,
                   sgu_ln_b=sgu_ln_b, sgu_w=sgu_w, sgu_b=sgu_b, mem_norm=mem_norm, w_kv=w_kv, branch_norm=branch_norm,
                   w_out=w_out, norm_post=norm_post)
    mom1 = dict(norm_pre=m_norm_pre, w_in=m_w_in, pool_w=m_pool_w, pool_scale=m_pool_scale, sgu_ln_g=m_sgu_ln_g,
                sgu_ln_b=m_sgu_ln_b, sgu_w=m_sgu_w, sgu_b=m_sgu_b, mem_norm=m_mem_norm, w_kv=m_w_kv,
                branch_norm=m_branch_norm, w_out=m_w_out, norm_post=m_norm_post)
    mom2 = dict(norm_pre=v_norm_pre, w_in=v_w_in, pool_w=v_pool_w, pool_scale=v_pool_scale, sgu_ln_g=v_sgu_ln_g,
                sgu_ln_b=v_sgu_ln_b, sgu_w=v_sgu_w, sgu_b=v_sgu_b, mem_norm=v_mem_norm, w_kv=v_w_kv,
                branch_norm=v_branch_norm, w_out=v_w_out, norm_post=v_norm_post)

    s, d = x.shape[1], x.shape[2]
    x2, mem2, tgt2 = x[0], mem[0], loss_target[0]
    t_branch = min(256, s)
    tm = min(512, s)

    core = lax.axis_index("c")
    chip = 2 * lax.axis_index("x") + lax.axis_index("y")
    pos = jnp.stack([core, chip]).astype(jnp.int32)
    n_in, n_kv, n_out = 4 * w_in.shape[2], 4 * w_kv.shape[1], 4 * w_out.shape[1]
    wi_rows, kv_rows, wo_rows = d // 8, n_kv // 8, n_out // 8

    def placed(shard, full_shape, block, grid, in_map, out_map, name):
        return _blockwise(_cast_copy, pos, [shard], [pl.BlockSpec(block, in_map)],
                          jax.ShapeDtypeStruct(full_shape, BF16), pl.BlockSpec(block, out_map), grid, name)

    kv_cols, pw_rows = w_kv.shape[2], GROUP // 8
    wi_own = placed(w_in[0], (d, n_in), (wi_rows, n_in // 4), (8,), lambda i, p: (i, 0), lambda i, p: (i, p[1]),
                    "place_w_in")
    wkv_own = placed(w_kv[0], (n_kv, kv_cols), (kv_rows, kv_cols), (2,), lambda i, p: (i, 0),
                     lambda i, p: (2 * p[1] + i, 0), "place_w_kv")
    wo_own = placed(w_out[0], (n_out, d), (wo_rows, d), (2,), lambda i, p: (i, 0), lambda i, p: (2 * p[1] + i, 0),
                    "place_w_out")
    pw_own = placed(pool_w[0], (4, GROUP, GROUP), (4, GROUP // 4, GROUP), (1,), lambda i, p: (0, 0, 0),
                    lambda i, p: (0, p[1], 0), "place_pool_w")

    x_pos, y_pos = lax.axis_index("x"), lax.axis_index("y")
    chips = jnp.stack([chip, 2 * (1 - x_pos) + y_pos, 2 * x_pos + 1 - y_pos,
                       2 * (1 - x_pos) + 1 - y_pos]).astype(jnp.int32)
    mem_g = mem_norm.reshape(1, d)
    proj, h, wi_full = _proj_piece(chips, 0, 1, x2, norm_pre, None, None, n_in,
                                   _gather_rider([wi_own], [0], peers=(0, 1)), tm, "proj_own")
    proj, wi_full = _proj_piece(chips, 1, 2, h, None, None, proj, n_in,
                                _gather_rider([wi_full], [0], peers=(2,)), tm, "proj_neighbours")
    proj, wkv_full, pw_full = _proj_piece(chips, 3, 1, h, None, wi_full, proj, n_in,
                                          _gather_rider([wkv_own, pw_own], [1, 3]), tm, "proj_diagonal")
    k_m, v_m = _kv_fwd(mem2, mem_g, wkv_full)
    bias_full = jnp.repeat(sgu_b[0].T, CHUNK, axis=1)
    y, wo_full = _branches_fwd(proj, pw_full, pool_scale, sgu_ln_g, sgu_ln_b, sgu_w[0], bias_full, k_m, v_m,
                               branch_norm, t_branch, _gather_rider([wo_own], [2]))
    loss_local, dz, dout, dy, g_norm_post = _out_loss(y, wo_full, x2, tgt2, norm_post, min(256, s))

    def pair_sums(views):
        theirs = _exchange_halves([(v[0], v[1], v[2]) for v in views], "exchange_for_" + views[0][9])
        return [_blockwise(_pair_sum, pos, [v[0], th], [pl.BlockSpec(v[3], v[4][0]), pl.BlockSpec(v[3], v[4][1])],
                           jax.ShapeDtypeStruct(v[5], BF16), pl.BlockSpec(v[6], v[7]), v[8], v[9])
                for v, th in zip(views, theirs)]

    tk = min(1024, s)
    (gwo_theirs,) = _grad_rows(y, dout, pos, lambda i, p: 2 * i + 1 - p[0], 4 * wo_rows, wo_rows, d, tk,
                               "grad_w_out_sibling_half")
    gwo_mine, gwo_from_sibling = _grad_rows(y, dout, pos, lambda i, p: 2 * i + p[0], 4 * wo_rows, wo_rows, d, tk,
                                            "grad_w_out_own_half", _exchange_rider([gwo_theirs], [[(0, 8)]]))
    ps_wo = _elementwise(_pair_sum, [gwo_mine, gwo_from_sibling], [BF16], "pair_sum_w_out")[0].reshape(4, wo_rows, d)
    (dproj, g_pw, g_pool_scale, g_ln_g, g_ln_b, g_sgu_w, g_sgu_b_t, g_branch_norm, dk, dv, landed_wo) = _branches_bwd(
        proj, dy, pw_full, pool_scale, sgu_ln_g, sgu_ln_b, sgu_w[0], jnp.swapaxes(sgu_w[0], 1, 2), bias_full,
        k_m, v_m, branch_norm, t_branch, _scatter_rider([(ps_wo, 0, [(1, 2)])]))
    g_wkv, g_mem_norm = _kv_bwd(mem2, mem_g, wkv_full, dk, dv)
    ps_kv, ps_pw = pair_sums([
        (g_wkv.reshape(4, 2, kv_rows, kv_cols), 1, [(0, 4), (2, 2)], (1, 1, kv_rows, kv_cols),
         (lambda i, p: (i, p[0], 0, 0), lambda i, p: (i, 0, 0, 0)), (4, kv_rows, kv_cols), (1, kv_rows, kv_cols),
         lambda i, p: (i, 0, 0), (4,), "pair_sum_w_kv"),
        (g_pw.astype(BF16).reshape(4, 4, 2, pw_rows, GROUP), 2, [(0, 4)], (1, 4, 1, pw_rows, GROUP),
         (lambda i, p: (i, 0, p[0], 0, 0), lambda i, p: (i, 0, 0, 0, 0)), (4, 4, pw_rows, GROUP),
         (1, 4, pw_rows, GROUP), lambda i, p: (i, 0, 0, 0), (4,), "pair_sum_pool_w")])
    gwi_theirs, landed_kv, landed_pw = _grad_rows(
        h, dproj, pos, lambda i, p: 1 - p[0], d // 2, d // 2, n_in // 4, tk, "grad_w_in_sibling_half",
        _scatter_rider([(ps_kv, 0, [(1, 2)]), (ps_pw, 1, [])]))
    gwi_mine, gwi_from_sibling = _grad_rows(h, dproj, pos, lambda i, p: p[0], d // 2, d // 2, n_in // 4, tk,
                                            "grad_w_in_own_half", _exchange_rider([gwi_theirs], [[(0, 16)]]))
    ps_wi = _elementwise(_pair_sum, [gwi_mine, gwi_from_sibling], [BF16], "pair_sum_w_in")[0]
    grad_x, g_norm_pre, landed_wi = _dx_call(dproj, wi_full, x2, dz, norm_pre, tm, 1024,
                                             _scatter_rider([(ps_wi, 1, [(0, 4)])]))
    psum = [ps_wi, ps_kv, ps_wo, ps_pw]
    landed = [landed_wi, landed_kv, landed_wo, landed_pw]
    from_chip = lambda spec_shape, rank: [
        pl.BlockSpec(spec_shape, functools.partial(lambda i, p, q: (q, i) + (0,) * (rank - 2), q=q))
        for q in range(3)]
    joined = _join_halves([
        (_blockwise(_four_sum, pos, [psum[0]] + [landed[0]] * 3,
                    [pl.BlockSpec((256, n_in // 4), lambda i, p: (i, p[1]))] + from_chip((1, 256, n_in // 4), 3),
                    jax.ShapeDtypeStruct((2, d // 2, n_in // 4), F32),
                    pl.BlockSpec((1, 256, n_in // 4), lambda i, p: (p[0], i, 0)), (d // 2 // 256,), "chip_sum_w_in"),
         0, [(1, 8)]),
        (_blockwise(_four_sum, pos, [psum[1]] + [landed[1]] * 3,
                    [pl.BlockSpec((1, kv_rows, kv_cols), lambda i, p: (p[1], 0, 0))]
                    + from_chip((1, 1, kv_rows, kv_cols), 4),
                    jax.ShapeDtypeStruct((2, kv_rows, kv_cols), F32),
                    pl.BlockSpec((1, kv_rows, kv_cols), lambda i, p: (p[0], 0, 0)), (1,), "chip_sum_w_kv"),
         0, [(1, 2)]),
        (_blockwise(_four_sum, pos, [psum[2]] + [landed[2]] * 3,
                    [pl.BlockSpec((1, wo_rows, d), lambda i, p: (p[1], 0, 0))] + from_chip((1, 1, wo_rows, d), 4),
                    jax.ShapeDtypeStruct((2, wo_rows, d), F32),
                    pl.BlockSpec((1, wo_rows, d), lambda i, p: (p[0], 0, 0)), (1,), "chip_sum_w_out"),
         0, [(1, 2)]),
        (_blockwise(_four_sum, pos, [psum[3]] + [landed[3]] * 3,
                    [pl.BlockSpec((4, 1, pw_rows, GROUP), lambda i, p: (0, p[1], 0, 0))]
                    + from_chip((1, 4, 1, pw_rows, GROUP), 5),
                    jax.ShapeDtypeStruct((4, 2, pw_rows, GROUP), F32),
                    pl.BlockSpec((4, 1, pw_rows, GROUP), lambda i, p: (0, p[0], 0, 0)), (1,), "chip_sum_pool_w"),
         1, []),
    ])
    grads = {"w_in": joined[0].reshape(w_in.shape), "w_kv": joined[1].reshape(w_kv.shape),
             "w_out": joined[2].reshape(w_out.shape), "pool_w": joined[3].reshape(pool_w.shape)}

    small_local = dict(norm_pre=g_norm_pre, pool_scale=g_pool_scale, sgu_ln_g=g_ln_g, sgu_ln_b=g_ln_b,
                       sgu_w=g_sgu_w, sgu_b=g_sgu_b_t.T, mem_norm=g_mem_norm, branch_norm=g_branch_norm,
                       norm_post=g_norm_post)
    small_rows = sum(weights[n].size for n in SMALL) // 128
    small_sum = _allreduce_small(_pack([small_local[n] for n in SMALL], [jnp.pad(loss_local, ((0, 7), (0, 127)))]))
    for n, g in zip(SMALL, _unpack(small_sum, [weights[n] for n in SMALL])):
        grads[n] = g
    loss = small_sum[small_rows, 0]

    delta, new_m, new_v = {}, {}, {}
    packed = [small_sum if src is grads else _pack([src[n] for n in SMALL]) for src in (weights, grads, mom1, mom2)]
    outs = _elementwise(_adamw, packed, [F32, F32, F32], "adamw_small")
    for dst, o in zip((delta, new_m, new_v), outs):
        for n, a in zip(SMALL, _unpack(o, [weights[n] for n in SMALL])):
            dst[n] = a
    for n in LARGE:
        cols = weights[n].shape[-1]
        outs = _elementwise(_adamw, [src[n].reshape(-1, cols) for src in (weights, grads, mom1, mom2)],
                            [F32, F32, F32], "adamw_" + n)
        for dst, o in zip((delta, new_m, new_v), outs):
            dst[n] = o.reshape(weights[n].shape)

    return (loss, grad_x[None], *[grads[n] for n in ORDER], *[delta[n] for n in ORDER],
            *[new_m[n] for n in ORDER], *[new_v[n] for n in ORDER])
```

```python
import functools

import jax
import jax.numpy as jnp
from jax import lax
from jax.experimental import pallas as pl
from jax.experimental.pallas import tpu as pltpu

F32 = jnp.float32
BF16 = jnp.bfloat16
EPS = 1e-6
MESH = pl.DeviceIdType.MESH
ANY = pl.BlockSpec(memory_space=pl.ANY)

POOL_WINDOWS = (2, 4, 8, 16)
GROUP = 256
HALO = 16
CHUNK = 128
N_SGU_HEADS = 8
N_ATT_HEADS = 4
ATT_DIM = 256
WIDTH = 1024
ATT_SCALE = 1.0 / 16.0

ADAM_LR = 0.001
ADAM_B1 = 0.9
ADAM_B2 = 0.999
ADAM_EPS = 1e-08
ADAM_WD = 0.01
ADAM_STEP = 10

VMEM_LIMIT = 60 * 1024 * 1024


def _params(n_grid_axes, vmem=VMEM_LIMIT):
    return pltpu.CompilerParams(dimension_semantics=("arbitrary",) * n_grid_axes, vmem_limit_bytes=vmem)


def _dot(a, b, dims):
    return lax.dot_general(a, b, (dims, ((), ())), preferred_element_type=F32)


NN = ((1,), (0,))
NT = ((1,), (1,))
TN = ((0,), (0,))


class _Rider:
    def __init__(self, inputs, out_shapes, n_sems, run, aliases=None):
        self.inputs, self.out_shapes, self.n_sems, self.run = list(inputs), list(out_shapes), n_sems, run
        self.aliases = aliases or {}


def _call(body, name, grid, in_specs, out_specs, out_shape, scratch_shapes, inputs, rider=None, prefetch=None,
          aliases=None, rider_refs=False):
    n_in, n_out, n_scr = len(in_specs), len(out_specs), len(scratch_shapes)
    r_in = len(rider.inputs) if rider else 0
    r_out = len(rider.out_shapes) if rider else 0
    n_pre = 0 if prefetch is None else 1

    def whole_body(*refs):
        pre, refs = refs[:n_pre], refs[n_pre:]
        ins, rider_ins = refs[:n_in], refs[n_in:n_in + r_in]
        refs = refs[n_in + r_in:]
        outs, rider_outs = refs[:n_out], refs[n_out:n_out + r_out]
        refs = refs[n_out + r_out:]
        scratch, sems = refs[:n_scr], refs[n_scr:]
        extra = {"rider_outs": rider_outs} if rider_refs else {}
        if rider is None:
            body(*pre, *ins, *outs, *scratch, **extra)
            return
        ids = [pl.program_id(ax) for ax in range(len(grid))]
        first = functools.reduce(lambda p, q: p & q, [i == 0 for i in ids])
        last = functools.reduce(lambda p, q: p & q, [i == g - 1 for i, g in zip(ids, grid)])

        @pl.when(first)
        def _():
            rider.run(rider_ins, rider_outs, *sems, True)

        body(*pre, *ins, *outs, *scratch, **extra)

        @pl.when(last)
        def _():
            rider.run(rider_ins, rider_outs, *sems, False)

    io_aliases = {n_pre + i: o for i, o in (aliases or {}).items()}
    scratch_all = list(scratch_shapes)
    if rider:
        io_aliases.update({n_pre + n_in + i: n_out + o for i, o in rider.aliases.items()})
        scratch_all += [pltpu.SemaphoreType.DMA((rider.n_sems,)), pltpu.SemaphoreType.DMA((rider.n_sems,))]
    specs = dict(grid=grid, in_specs=list(in_specs) + [ANY] * r_in, out_specs=tuple(out_specs) + (ANY,) * r_out,
                 scratch_shapes=scratch_all)
    if n_pre:
        specs = dict(grid_spec=pltpu.PrefetchScalarGridSpec(num_scalar_prefetch=1, **specs))
    outs = pl.pallas_call(
        whole_body, name=name, **specs,
        out_shape=tuple(out_shape) + tuple(rider.out_shapes if rider else ()),
        input_output_aliases=io_aliases, compiler_params=_params(len(grid)),
    )(*([prefetch] if n_pre else []), *inputs, *(rider.inputs if rider else []))
    return tuple(outs)


def _grad_rows(a_t, b, pos, row_of, m, tm, tn, tk, name, rider=None):
    k, n = a_t.shape[1], b.shape[1]
    nk = k // tk
    out_dtype, dims, a = BF16, NN, a_t
    a_spec = pl.BlockSpec((tm, tk), lambda i, j, kk, p: (row_of(i, p), kk))
    b_spec = pl.BlockSpec((tk, tn), lambda i, j, kk, p: (kk, j))

    def body(pos_ref, a_ref, b_ref, o_ref, *acc):
        part = lambda: _dot(a_ref[...], b_ref[...], dims)
        if nk == 1:
            o_ref[...] = part().astype(out_dtype)
            return
        (acc_ref,) = acc
        kk = pl.program_id(2)

        @pl.when(kk == 0)
        def _():
            acc_ref[...] = part()

        @pl.when((kk > 0) & (kk < nk - 1))
        def _():
            acc_ref[...] += part()

        @pl.when(kk == nk - 1)
        def _():
            o_ref[...] = (acc_ref[...] + part()).astype(out_dtype)

    return _call(body, name, (m // tm, n // tn, nk), [a_spec, b_spec],
                 [pl.BlockSpec((tm, tn), lambda i, j, kk, p: (i, j))], [jax.ShapeDtypeStruct((m, n), out_dtype)],
                 [pltpu.VMEM((tm, tn), F32)] if nk > 1 else [], [a, b], rider, prefetch=pos)


def _proj_piece(chips, first, n_shards, src, g_pre, w_in, proj_in, n_cols, rider, tm, name):
    s, d = src.shape
    cols = n_cols // 4
    fused = g_pre is not None

    def body(chips_ref, *refs, rider_outs=()):
        refs = list(refs)
        src_ref = refs.pop(0)
        g_ref = refs.pop(0) if fused else None
        w_ref = refs.pop(0) if w_in is not None else rider_outs[0]
        if proj_in is not None:
            refs.pop(0)
        proj_ref = refs.pop(0)
        h_ref, ht_ref = (refs.pop(0), refs.pop(0)) if fused else (None, None)
        wbuf, sem = refs
        q, i = pl.program_id(0), pl.program_id(1)

        @pl.when(i == 0)
        def _():
            at = pl.multiple_of(chips_ref[first + q] * cols, 128)
            cp = pltpu.make_async_copy(w_ref.at[:, pl.ds(at, cols)], wbuf, sem)
            cp.start()
            cp.wait()

        if fused:
            xv = src_ref[...]
            r = lax.rsqrt(jnp.mean(xv * xv, axis=-1, keepdims=True) + EPS)
            h = (xv * r * g_ref[...]).astype(BF16)
            h_ref[...] = h
            ht_ref[...] = h.T
        else:
            h = src_ref[...]
        proj_ref[...] = _dot(h, wbuf[...], NN)

    row = lambda q, i, ch: (i, 0)
    inputs, in_specs = [src], [pl.BlockSpec((tm, d), row)]
    if fused:
        inputs.append(g_pre)
        in_specs.append(pl.BlockSpec((1, d), lambda q, i, ch: (0, 0)))
    if w_in is not None:
        inputs.append(w_in)
        in_specs.append(ANY)
    aliases = {}
    if proj_in is not None:
        aliases[len(inputs)] = 0
        inputs.append(proj_in)
        in_specs.append(ANY)
    out_specs = [pl.BlockSpec((tm, cols), lambda q, i, ch: (i, ch[first + q]))]
    out_shape = [jax.ShapeDtypeStruct((s, n_cols), F32)]
    if fused:
        assert n_shards == 1
        out_specs += [pl.BlockSpec((tm, d), row), pl.BlockSpec((d, tm), lambda q, i, ch: (0, i))]
        out_shape += [jax.ShapeDtypeStruct((s, d), BF16), jax.ShapeDtypeStruct((d, s), BF16)]
    return _call(body, name, (n_shards, s // tm), in_specs, out_specs, out_shape,
                 [pltpu.VMEM((d, cols), BF16), pltpu.SemaphoreType.DMA(())], inputs, rider, prefetch=chips,
                 aliases=aliases, rider_refs=True)


def _kv_fwd(mem, g, w_kv):
    m, d = mem.shape

    def body(mem_ref, g_ref, w_ref, k_ref, v_ref):
        mv = mem_ref[...]
        r = lax.rsqrt(jnp.mean(mv * mv, axis=-1, keepdims=True) + EPS)
        mem_n = (mv * r * g_ref[...]).astype(BF16)
        kv = _dot(mem_n, w_ref[...], NN)
        k_ref[...] = kv[:, :WIDTH].astype(BF16)
        v_ref[...] = kv[:, WIDTH:].astype(BF16)

    return pl.pallas_call(
        body, name="kv_fwd",
        out_shape=(jax.ShapeDtypeStruct((m, WIDTH), BF16), jax.ShapeDtypeStruct((m, WIDTH), BF16)),
        compiler_params=_params(0),
    )(mem, g, w_kv)


def _kv_bwd(mem, g, w_kv, dk, dv):
    m, d = mem.shape
    n = w_kv.shape[1]
    col = 512

    def body(mem_ref, g_ref, w_ref, dk_ref, dv_ref, dw_ref, dg_ref):
        mv = mem_ref[...]
        r = lax.rsqrt(jnp.mean(mv * mv, axis=-1, keepdims=True) + EPS)
        mem_hat = mv * r
        mem_n = (mem_hat * g_ref[...]).astype(BF16)
        dkv = jnp.concatenate([dk_ref[...], dv_ref[...]], axis=1).astype(BF16)
        for j in range(n // col):
            dw_ref[:, j * col:(j + 1) * col] = _dot(mem_n, dkv[:, j * col:(j + 1) * col], TN).astype(BF16)
        dmem_n = _dot(dkv, w_ref[...], NT)
        dg_ref[...] = jnp.sum(dmem_n * mem_hat, axis=0, keepdims=True)

    return pl.pallas_call(
        body, name="kv_bwd",
        out_shape=(jax.ShapeDtypeStruct((d, n), BF16), jax.ShapeDtypeStruct((1, d), F32)),
        compiler_params=_params(0),
    )(mem, g, w_kv, dk, dv)


def _sigmoid(x):
    return 1.0 / (1.0 + jnp.exp(-x))


def _inv_counts(t0, t):
    pos = (t0 + lax.broadcasted_iota(jnp.int32, (t, 1), 0) + 1).astype(F32)
    return [1.0 / jnp.minimum(pos, float(w)) for w in POOL_WINDOWS]


def _window_sums(ext, t, backward):
    n = t + HALO
    parts = []
    for gi, w in enumerate(POOL_WINDOWS):
        s = ext[:, gi * GROUP:(gi + 1) * GROUP]
        k = 1
        while k < w:
            s = s + pltpu.roll(s, (n - k) if backward else k, axis=0)
            k *= 2
        parts.append(s[:t] if backward else s[HALO:])
    return parts


def _pool_fwd(xa, halo, inv, pool_w):
    t = xa.shape[0]
    sums = _window_sums(jnp.concatenate([halo, xa], axis=0), t, backward=False)
    d = jnp.concatenate([sums[gi] * inv[gi] - xa[:, gi * GROUP:(gi + 1) * GROUP] for gi in range(4)], axis=1)
    d = d.astype(BF16)
    y = jnp.concatenate([_dot(d[:, gi * GROUP:(gi + 1) * GROUP], pool_w[gi], NN) for gi in range(4)], axis=1)
    return d, y


def _layernorm_fwd(v):
    mu = jnp.mean(v, axis=-1, keepdims=True)
    xc = v - mu
    rstd = lax.rsqrt(jnp.mean(xc * xc, axis=-1, keepdims=True) + EPS)
    return xc * rstd, rstd


def _tril_mask(transposed):
    r = lax.broadcasted_iota(jnp.int32, (CHUNK, CHUNK), 0)
    c = lax.broadcasted_iota(jnp.int32, (CHUNK, CHUNK), 1)
    return (r <= c) if transposed else (r >= c)


def _sgu_mix(w_ref, vals, transposed):
    t = vals.shape[0]
    mask = _tril_mask(transposed)
    ws = [jnp.where(mask, w_ref[h], 0.0).astype(BF16) for h in range(N_SGU_HEADS)]
    rows = []
    for ci in range(t // CHUNK):
        blk = vals[ci * CHUNK:(ci + 1) * CHUNK]
        rows.append(jnp.concatenate(
            [_dot(ws[h], blk[:, h * CHUNK:(h + 1) * CHUNK], NN) for h in range(N_SGU_HEADS)], axis=1))
    return jnp.concatenate(rows, axis=0)


def _attn_fwd(q, k, v):
    ps, os_ = [], []
    for h in range(N_ATT_HEADS):
        sl = slice(h * ATT_DIM, (h + 1) * ATT_DIM)
        s = _dot(q[:, sl], k[:, sl], NT) * ATT_SCALE
        s = s - jnp.max(s, axis=-1, keepdims=True)
        e = jnp.exp(s)
        p = e * (1.0 / jnp.sum(e, axis=-1, keepdims=True))
        ps.append(p)
        os_.append(_dot(p.astype(BF16), v[:, sl], NN))
    return ps, jnp.concatenate(os_, axis=1)


def _rms_branch(y_pre):
    r = lax.rsqrt(jnp.mean(y_pre * y_pre, axis=-1, keepdims=True) + EPS)
    return y_pre * r, r


def _branch_specs(t, n_tiles, order):
    width_in = 7 * WIDTH
    tile = lambda i: order(i)
    per_halo = t // HALO
    const2 = lambda i: (0, 0)
    const3 = lambda i: (0, 0, 0)
    return [
        pl.BlockSpec((t, width_in), lambda i: (tile(i), 0)),
        pl.BlockSpec((HALO, WIDTH), lambda i: (jnp.maximum(tile(i) * per_halo - 1, 0), 0)),
        pl.BlockSpec((4, GROUP, GROUP), const3),
        pl.BlockSpec((1, WIDTH), const2),
        pl.BlockSpec((1, WIDTH), const2),
        pl.BlockSpec((1, WIDTH), const2),
        pl.BlockSpec((N_SGU_HEADS, CHUNK, CHUNK), const3),
        pl.BlockSpec((CHUNK, WIDTH), const2),
        pl.BlockSpec((MEM_ROWS, WIDTH), const2),
        pl.BlockSpec((MEM_ROWS, WIDTH), const2),
        pl.BlockSpec((1, 3 * WIDTH), const2),
    ]


MEM_ROWS = 256


def _branches_fwd(proj, pool_w, pool_scale, ln_g, ln_b, sgu_w, bias_full, k, v, branch_norm, t, rider=None):
    s = proj.shape[0]
    n_tiles = s // t

    def body(proj_ref, halo_ref, pw_ref, ps_ref, lg_ref, lb_ref, sw_ref, sb_ref, k_ref, v_ref, bn_ref, y_ref, yt_ref):
        i = pl.program_id(0)
        col = lambda j: proj_ref[:, j * WIDTH:(j + 1) * WIDTH]

        def put(branch, y_pre):
            sl = slice(branch * WIDTH, (branch + 1) * WIDTH)
            val = (_rms_branch(y_pre)[0] * bn[:, sl]).astype(BF16)
            y_ref[:, sl] = val
            yt_ref[sl, :] = val.T

        bn = bn_ref[...]
        halo = jnp.where(i > 0, halo_ref[...], 0.0)
        _, y_pool = _pool_fwd(col(0), halo, _inv_counts(i * t, t), pw_ref[...])
        ga = col(1)
        ya = y_pool * ps_ref[...] * (ga * _sigmoid(ga))
        put(0, ya)
        vhat, _ = _layernorm_fwd(col(3))
        vn = (vhat * lg_ref[...] + lb_ref[...]).astype(BF16)
        z = _sgu_mix(sw_ref, vn, transposed=False) + jnp.tile(sb_ref[...], (t // CHUNK, 1))
        gb = col(4)
        yb = col(2) * z * (gb * _sigmoid(gb))
        put(1, yb)
        _, o = _attn_fwd(col(5).astype(BF16), k_ref[...], v_ref[...])
        gc = col(6)
        yc = o * (gc * _sigmoid(gc))
        put(2, yc)

    return _call(body, "branches_fwd", (n_tiles,), _branch_specs(t, n_tiles, lambda i: i),
                 [pl.BlockSpec((t, 3 * WIDTH), lambda i: (i, 0)), pl.BlockSpec((3 * WIDTH, t), lambda i: (0, i))],
                 [jax.ShapeDtypeStruct((s, 3 * WIDTH), BF16), jax.ShapeDtypeStruct((3 * WIDTH, s), BF16)], [],
                 [proj, proj, pool_w, pool_scale, ln_g, ln_b, sgu_w, bias_full, k, v, branch_norm], rider)


def _branches_bwd(proj, dy, pool_w, pool_scale, ln_g, ln_b, sgu_w, sgu_wt, bias_full, k, v, branch_norm, t, rider=None):
    s = proj.shape[0]
    n_tiles = s // t
    n_chunks = t // CHUNK
    order = lambda i: n_tiles - 1 - i

    def body(proj_ref, halo_ref, pw_ref, ps_ref, lg_ref, lb_ref, sw_ref, sb_ref, k_ref, v_ref, bn_ref,
             swt_ref, dy_ref,
             dproj_ref, dpw_ref, dps_ref, dlg_ref, dlb_ref, dsw_ref, dsb_ref, dbn_ref, dk_ref, dv_ref,
             carry_ref, dbias_ref):
        step = pl.program_id(0)
        i = order(step)

        @pl.when(step == 0)
        def _():
            for ref in (dpw_ref, dps_ref, dlg_ref, dlb_ref, dsw_ref, dbn_ref, dk_ref, dv_ref, carry_ref, dbias_ref):
                ref[...] = jnp.zeros(ref.shape, ref.dtype)

        col = lambda j: proj_ref[:, j * WIDTH:(j + 1) * WIDTH]
        bn = bn_ref[...]

        def norm_bwd(y_pre, sl):
            yhat, r = _rms_branch(y_pre)
            dyv = dy_ref[:, sl].astype(F32)
            dbn_ref[:, sl] += jnp.sum(dyv * yhat, axis=0, keepdims=True)
            dyhat = dyv * bn[:, sl]
            return r * (dyhat - yhat * jnp.mean(dyhat * yhat, axis=-1, keepdims=True))

        def gate(gv):
            sg = _sigmoid(gv)
            return gv * sg, sg * (1.0 + gv * (1.0 - sg))

        inv = _inv_counts(i * t, t)
        halo = jnp.where(i > 0, halo_ref[...], 0.0)
        pw = pw_ref[...]
        d, y_pool = _pool_fwd(col(0), halo, inv, pw)
        scale = ps_ref[...]
        silu_a, dsilu_a = gate(col(1))
        pa = y_pool * scale
        dya = norm_bwd(pa * silu_a, slice(0, WIDTH))
        dproj_ref[:, WIDTH:2 * WIDTH] = (dya * pa * dsilu_a).astype(BF16)
        dpa = dya * silu_a
        dps_ref[...] += jnp.sum(dpa * y_pool, axis=0, keepdims=True)
        dy_pool = (dpa * scale).astype(BF16)
        dd_parts, ddc_parts = [], []
        for gi in range(4):
            sl = slice(gi * GROUP, (gi + 1) * GROUP)
            dpw_ref[gi] += _dot(d[:, sl], dy_pool[:, sl], TN)
            dd = _dot(dy_pool[:, sl], pw[gi], NT)
            dd_parts.append(dd)
            ddc_parts.append(dd * inv[gi])
        ddc = jnp.concatenate(ddc_parts, axis=1)
        sums = _window_sums(jnp.concatenate([ddc, carry_ref[...]], axis=0), t, backward=True)
        carry_ref[...] = ddc[:HALO]
        dproj_ref[:, 0:WIDTH] = jnp.concatenate([sums[gi] - dd_parts[gi] for gi in range(4)], axis=1).astype(BF16)

        vhat, rstd = _layernorm_fwd(col(3))
        lg = lg_ref[...]
        vn = (vhat * lg + lb_ref[...]).astype(BF16)
        z = _sgu_mix(sw_ref, vn, transposed=False) + jnp.tile(sb_ref[...], (n_chunks, 1))
        u = col(2)
        silu_b, dsilu_b = gate(col(4))
        uz = u * z
        dyb = norm_bwd(uz * silu_b, slice(WIDTH, 2 * WIDTH))
        dproj_ref[:, 4 * WIDTH:5 * WIDTH] = (dyb * uz * dsilu_b).astype(BF16)
        duz = dyb * silu_b
        dproj_ref[:, 2 * WIDTH:3 * WIDTH] = (duz * z).astype(BF16)
        dz = duz * u
        dz_b = dz.astype(BF16)
        for ci in range(n_chunks):
            rows = slice(ci * CHUNK, (ci + 1) * CHUNK)
            dbias_ref[...] += dz[rows]
            for h in range(N_SGU_HEADS):
                sl = slice(h * CHUNK, (h + 1) * CHUNK)
                dsw_ref[h] += _dot(dz_b[rows, sl], vn[rows, sl], NT)
        dvn = _sgu_mix(swt_ref, dz_b, transposed=True)
        dlg_ref[...] += jnp.sum(dvn * vhat, axis=0, keepdims=True)
        dlb_ref[...] += jnp.sum(dvn, axis=0, keepdims=True)
        dvhat = dvn * lg
        dvb = rstd * (dvhat - jnp.mean(dvhat, axis=-1, keepdims=True)
                      - vhat * jnp.mean(dvhat * vhat, axis=-1, keepdims=True))
        dproj_ref[:, 3 * WIDTH:4 * WIDTH] = dvb.astype(BF16)

        q = col(5).astype(BF16)
        kv_k, kv_v = k_ref[...], v_ref[...]
        ps, o = _attn_fwd(q, kv_k, kv_v)
        silu_c, dsilu_c = gate(col(6))
        dyc = norm_bwd(o * silu_c, slice(2 * WIDTH, 3 * WIDTH))
        dproj_ref[:, 6 * WIDTH:7 * WIDTH] = (dyc * o * dsilu_c).astype(BF16)
        do = (dyc * silu_c).astype(BF16)
        dq_parts = []
        for h in range(N_ATT_HEADS):
            sl = slice(h * ATT_DIM, (h + 1) * ATT_DIM)
            p = ps[h]
            dp = _dot(do[:, sl], kv_v[:, sl], NT)
            ds = (p * (dp - jnp.sum(p * dp, axis=-1, keepdims=True)) * ATT_SCALE).astype(BF16)
            dq_parts.append(_dot(ds, kv_k[:, sl], NN))
            dk_ref[:, sl] += _dot(ds, q[:, sl], TN)
            dv_ref[:, sl] += _dot(p.astype(BF16), do[:, sl], TN)
        dproj_ref[:, 5 * WIDTH:6 * WIDTH] = jnp.concatenate(dq_parts, axis=1).astype(BF16)

        @pl.when(step == n_tiles - 1)
        def _():
            keep = _tril_mask(transposed=False)
            for h in range(N_SGU_HEADS):
                dsw_ref[h] = jnp.where(keep, dsw_ref[h], 0.0)
            dsb_ref[...] = jnp.concatenate(
                [jnp.sum(dbias_ref[:, h * CHUNK:(h + 1) * CHUNK], axis=1, keepdims=True)
                 for h in range(N_SGU_HEADS)], axis=1)

    const2 = lambda i: (0, 0)
    const3 = lambda i: (0, 0, 0)
    out_shapes = (
        jax.ShapeDtypeStruct((s, 7 * WIDTH), BF16),
        jax.ShapeDtypeStruct((4, GROUP, GROUP), F32),
        jax.ShapeDtypeStruct((1, WIDTH), F32),
        jax.ShapeDtypeStruct((1, WIDTH), F32),
        jax.ShapeDtypeStruct((1, WIDTH), F32),
        jax.ShapeDtypeStruct((N_SGU_HEADS, CHUNK, CHUNK), F32),
        jax.ShapeDtypeStruct((CHUNK, N_SGU_HEADS), F32),
        jax.ShapeDtypeStruct((1, 3 * WIDTH), F32),
        jax.ShapeDtypeStruct((MEM_ROWS, WIDTH), F32),
        jax.ShapeDtypeStruct((MEM_ROWS, WIDTH), F32),
    )
    out_specs = (
        pl.BlockSpec((t, 7 * WIDTH), lambda i: (order(i), 0)),
        pl.BlockSpec((4, GROUP, GROUP), const3),
        pl.BlockSpec((1, WIDTH), const2),
        pl.BlockSpec((1, WIDTH), const2),
        pl.BlockSpec((1, WIDTH), const2),
        pl.BlockSpec((N_SGU_HEADS, CHUNK, CHUNK), const3),
        pl.BlockSpec((CHUNK, N_SGU_HEADS), const2),
        pl.BlockSpec((1, 3 * WIDTH), const2),
        pl.BlockSpec((MEM_ROWS, WIDTH), const2),
        pl.BlockSpec((MEM_ROWS, WIDTH), const2),
    )
    in_specs = _branch_specs(t, n_tiles, order) + [
        pl.BlockSpec((N_SGU_HEADS, CHUNK, CHUNK), const3),
        pl.BlockSpec((t, 3 * WIDTH), lambda i: (order(i), 0)),
    ]
    return _call(body, "branches_bwd", (n_tiles,), in_specs, out_specs, out_shapes,
                 [pltpu.VMEM((HALO, WIDTH), F32), pltpu.VMEM((CHUNK, WIDTH), F32)],
                 [proj, proj, pool_w, pool_scale, ln_g, ln_b, sgu_w, bias_full, k, v, branch_norm, sgu_wt, dy], rider)


def _out_loss(y, w_out, x, target, g_post, tm):
    s, d = x.shape
    e_w = y.shape[1]
    n_tiles = s // tm

    def body(y_ref, w_ref, x_ref, t_ref, g_ref, loss_ref, dz_ref, dout_ref, dy_ref, dg_ref, sq_ref):
        i = pl.program_id(0)

        @pl.when(i == 0)
        def _():
            sq_ref[...] = jnp.zeros(sq_ref.shape, F32)
            dg_ref[...] = jnp.zeros(dg_ref.shape, F32)

        w = w_ref[...]
        out = _dot(y_ref[...], w, NN)
        r = lax.rsqrt(jnp.mean(out * out, axis=-1, keepdims=True) + EPS)
        outn = out * r
        g = g_ref[...]
        err = (x_ref[...] + outn * g) - t_ref[...]
        sq_ref[...] += jnp.sum(err * err, axis=0, keepdims=True)
        dz = err * (1.0 / d)
        dz_ref[...] = dz
        dg_ref[...] += jnp.sum(dz * outn, axis=0, keepdims=True)
        doutn = dz * g
        dout = (r * (doutn - outn * jnp.mean(doutn * outn, axis=-1, keepdims=True))).astype(BF16)
        dout_ref[...] = dout
        dy_ref[...] = _dot(dout, w, NT).astype(BF16)

        @pl.when(i == n_tiles - 1)
        def _():
            loss_ref[...] = 0.5 * jnp.sum(sq_ref[...], axis=1, keepdims=True) * (1.0 / d)

    row = lambda i: (i, 0)
    const2 = lambda i: (0, 0)
    return pl.pallas_call(
        body, name="out_loss", grid=(n_tiles,),
        in_specs=[
            pl.BlockSpec((tm, e_w), row),
            pl.BlockSpec((e_w, d), const2, pipeline_mode=pl.Buffered(1)),
            pl.BlockSpec((tm, d), row),
            pl.BlockSpec((tm, d), row),
            pl.BlockSpec((1, d), const2),
        ],
        out_specs=(
            pl.BlockSpec((1, 1), const2),
            pl.BlockSpec((tm, d), row),
            pl.BlockSpec((tm, d), row),
            pl.BlockSpec((tm, e_w), row),
            pl.BlockSpec((1, d), const2),
        ),
        out_shape=(
            jax.ShapeDtypeStruct((1, 1), F32),
            jax.ShapeDtypeStruct((s, d), F32),
            jax.ShapeDtypeStruct((s, d), BF16),
            jax.ShapeDtypeStruct((s, e_w), BF16),
            jax.ShapeDtypeStruct((1, d), F32),
        ),
        scratch_shapes=[pltpu.VMEM((1, d), F32)],
        compiler_params=_params(1),
    )(y, w_out, x, target, g_post)


def _dx_call(dproj, w_in, x, dz, g_pre, tm, tk, rider=None):
    s, d = x.shape
    k_total = dproj.shape[1]
    nk = k_total // tk
    n_tiles = s // tm

    def body(dp_ref, w_ref, x_ref, dz_ref, g_ref, dx_ref, dg_ref, acc_ref):
        i, kk = pl.program_id(0), pl.program_id(1)
        part = lambda: _dot(dp_ref[...], w_ref[...], NT)

        @pl.when(kk == 0)
        def _():
            acc_ref[...] = part()

        @pl.when((kk > 0) & (kk < nk - 1))
        def _():
            acc_ref[...] += part()

        @pl.when((i == 0) & (kk == 0))
        def _():
            dg_ref[...] = jnp.zeros(dg_ref.shape, F32)

        @pl.when(kk == nk - 1)
        def _():
            dh = acc_ref[...] + part()
            xv = x_ref[...]
            r = lax.rsqrt(jnp.mean(xv * xv, axis=-1, keepdims=True) + EPS)
            xhat = xv * r
            dg_ref[...] += jnp.sum(dh * xhat, axis=0, keepdims=True)
            dxhat = dh * g_ref[...]
            dx_ref[...] = dz_ref[...] + r * (dxhat - xhat * jnp.mean(dxhat * xhat, axis=-1, keepdims=True))

    row = lambda i, kk: (i, 0)
    const2 = lambda i, kk: (0, 0)
    return _call(
        body, "dx", (n_tiles, nk),
        [
            pl.BlockSpec((tm, tk), lambda i, kk: (i, kk)),
            pl.BlockSpec((d, tk), lambda i, kk: (0, kk)),
            pl.BlockSpec((tm, d), row),
            pl.BlockSpec((tm, d), row),
            pl.BlockSpec((1, d), const2),
        ],
        [pl.BlockSpec((tm, d), row), pl.BlockSpec((1, d), const2)],
        [jax.ShapeDtypeStruct((s, d), F32), jax.ShapeDtypeStruct((1, d), F32)],
        [pltpu.VMEM((tm, d), F32)], [dproj, w_in, x, dz, g_pre], rider)


def _rows_tile(rows, cols, n_arrays, itemsize=4):
    budget = 24 * 1024 * 1024 // (2 * n_arrays * cols * itemsize)
    if rows <= budget:
        return rows
    best = None
    for cand in range(16, rows + 1, 16):
        if rows % cand == 0 and cand <= max(budget, 16):
            best = cand
    return best if best is not None else rows


def _elementwise(fn, inputs, out_dtypes, name):
    rows, cols = inputs[0].shape
    tr = _rows_tile(rows, cols, len(inputs) + len(out_dtypes))
    n_in = len(inputs)

    def body(*refs):
        outs = fn(*[r[...] for r in refs[:n_in]])
        for o_ref, o in zip(refs[n_in:], outs):
            o_ref[...] = o.astype(o_ref.dtype)

    spec = pl.BlockSpec((tr, cols), lambda i: (i, 0))
    return pl.pallas_call(
        body, name=name, grid=(rows // tr,),
        in_specs=[spec] * n_in, out_specs=tuple([spec] * len(out_dtypes)),
        out_shape=tuple(jax.ShapeDtypeStruct((rows, cols), dt) for dt in out_dtypes),
        compiler_params=_params(1),
    )(*inputs)


def _blockwise(fn, pos, inputs, in_specs, out_shape, out_spec, grid, name):
    n_in = len(inputs)

    def body(pos_ref, *refs):
        o_ref = refs[n_in]
        (out,) = fn(*[r[...].reshape(o_ref.shape) for r in refs[:n_in]])
        o_ref[...] = out.astype(o_ref.dtype)

    return pl.pallas_call(
        body, name=name,
        grid_spec=pltpu.PrefetchScalarGridSpec(num_scalar_prefetch=1, grid=grid, in_specs=in_specs,
                                               out_specs=out_spec),
        out_shape=out_shape,
        compiler_params=_params(len(grid)),
    )(pos, *inputs)


def _cast_copy(x):
    return (x,)


def _pair_sum(mine, theirs):
    return ((mine.astype(F32) + theirs.astype(F32)),)


def _four_sum(own, t0, t1, t2):
    return ((((own.astype(F32) + t0.astype(F32)) + t1.astype(F32)) + t2.astype(F32)),)


def _adamw(w, g, m, v):
    m = ADAM_B1 * m + (1.0 - ADAM_B1) * g
    v = ADAM_B2 * v + (1.0 - ADAM_B2) * jnp.square(g)
    m_hat = m / (1.0 - ADAM_B1 ** ADAM_STEP)
    v_hat = v / (1.0 - ADAM_B2 ** ADAM_STEP)
    delta = -ADAM_LR * (m_hat / (jnp.sqrt(v_hat) + ADAM_EPS) + ADAM_WD * w)
    return delta, m, v


def _place():
    x, y, c = lax.axis_index("x"), lax.axis_index("y"), lax.axis_index("c")
    chips = [(1 - x, y), (x, 1 - y), (1 - x, 1 - y)]
    return x, y, c, chips


def _remote(src, dst, send_sem, recv_sem, to):
    return pltpu.make_async_remote_copy(src_ref=src, dst_ref=dst, send_sem=send_sem, recv_sem=recv_sem,
                                        device_id=to, device_id_type=MESH)


def _split(ref, plan):
    views = [ref]
    for axis, parts in plan:
        size = ref.shape[axis] // parts
        assert size * parts == ref.shape[axis]
        views = [v.at[tuple(pl.ds(q * size, size) if i == axis else slice(None) for i in range(len(ref.shape)))]
                 for v in views for q in range(parts)]
    return views


def _remote_in_parts(src, dst, send_sem, recv_sem, to, plan):
    for s, d in zip(_split(src, plan), _split(dst, plan)):
        _remote(s, d, send_sem, recv_sem, to).start()
    return _remote(src, dst, send_sem, recv_sem, to)


def _local_in_parts(src, dst, sem, plan):
    for s, d in zip(_split(src, plan), _split(dst, plan)):
        pltpu.make_async_copy(s, d, sem).start()
    return pltpu.make_async_copy(src, dst, sem)


def _hbm_call(body, name, inputs, out_shapes, scratch, aliases=None):
    return pl.pallas_call(
        body, name=name,
        in_specs=[ANY] * len(inputs), out_specs=tuple([ANY] * len(out_shapes)), out_shape=tuple(out_shapes),
        scratch_shapes=scratch, input_output_aliases=aliases or {},
        compiler_params=pltpu.CompilerParams(has_side_effects=True),
    )(*inputs)


def _gather_rider(fulls, kinds, peers=(0, 1, 2)):
    n = len(fulls)

    def full_half(a, ref, chip, cc):
        if a == 0:
            rows, cols = ref.shape[0] // 2, ref.shape[1] // 4
            return ref.at[pl.ds(cc * rows, rows), pl.ds(pl.multiple_of(chip * cols, 128), cols)]
        if a == 3:
            rows = ref.shape[1] // 8
            return ref.at[:, pl.ds(pl.multiple_of((2 * chip + cc) * rows, 16), rows), :]
        rows = ref.shape[0] // 8
        return ref.at[pl.ds(pl.multiple_of((2 * chip + cc) * rows, 16), rows), :]

    def run(in_refs, full_refs, send_sems, recv_sems, start):
        x, y, c, chips = _place()
        me = 2 * x + y
        sibling = (x, y, 1 - c)
        plans = [[(0, 8)], [(0, 2)], [(0, 2)], []]
        chips = [(p, chips[p]) for p in peers]
        if start:
            for p, chip in chips:
                for a in range(n):
                    mine = full_half(kinds[a], full_refs[a], me, c)
                    _remote_in_parts(mine, mine, send_sems.at[6 * a + p], recv_sems.at[6 * a + p], (*chip, c),
                                     plans[kinds[a]])
            return
        passed_on = []
        for p, chip in chips:
            them = 2 * chip[0] + chip[1]
            for a in range(n):
                landed = full_half(kinds[a], full_refs[a], them, c)
                _remote(landed, landed, send_sems.at[6 * a + p], recv_sems.at[6 * a + p], (*chip, c)).wait_recv()
                passed_on.append(_remote_in_parts(landed, landed, send_sems.at[6 * a + 3 + p],
                                                  recv_sems.at[6 * a + 3 + p], sibling, plans[kinds[a]]))
        for p, chip in chips:
            them = 2 * chip[0] + chip[1]
            for a in range(n):
                passed = full_half(kinds[a], full_refs[a], them, 1 - c)
                _remote(passed, passed, send_sems.at[6 * a + 3 + p], recv_sems.at[6 * a + 3 + p], sibling).wait_recv()
                mine = full_half(kinds[a], full_refs[a], me, c)
                _remote(mine, mine, send_sems.at[6 * a + p], recv_sems.at[6 * a + p], (*chip, c)).wait_send()
        for cp in passed_on:
            cp.wait_send()

    return _Rider(fulls, [jax.ShapeDtypeStruct(f.shape, f.dtype) for f in fulls], 6 * n, run,
                  aliases={a: a for a in range(n)})


def _exchange_halves(grads, name):
    n = len(grads)
    arrays = [g for g, _, _ in grads]
    out_shapes = [jax.ShapeDtypeStruct(tuple(1 if i == ax else dim for i, dim in enumerate(g.shape)), g.dtype)
                  for g, ax, _ in grads]

    def half(ref, ax, cc):
        idx = tuple(pl.ds(cc, 1) if i == ax else slice(None) for i in range(len(ref.shape)))
        return ref.at[idx]

    def body(*refs):
        in_refs, out_refs = refs[:n], refs[n:2 * n]
        send_sems, recv_sems = refs[2 * n:]
        x, y, c, _ = _place()
        sibling = (x, y, 1 - c)
        copies = [_remote_in_parts(half(in_refs[a], grads[a][1], 1 - c), out_refs[a], send_sems.at[a],
                                   recv_sems.at[a], sibling, grads[a][2]) for a in range(n)]
        for rem in copies:
            rem.wait()

    return _hbm_call(body, name, arrays, out_shapes,
                     [pltpu.SemaphoreType.DMA((n,)), pltpu.SemaphoreType.DMA((n,))])


def _exchange_rider(arrays, plans):
    n = len(arrays)

    def run(in_refs, out_refs, send_sems, recv_sems, start):
        x, y, c, _ = _place()
        sibling = (x, y, 1 - c)
        for a in range(n):
            sems = (send_sems.at[a], recv_sems.at[a])
            if start:
                _remote_in_parts(in_refs[a], out_refs[a], *sems, sibling, plans[a])
            else:
                _remote(in_refs[a], out_refs[a], *sems, sibling).wait()

    return _Rider(arrays, [jax.ShapeDtypeStruct(a.shape, a.dtype) for a in arrays], n, run)


def _scatter_rider(parts):
    n = len(parts)
    arrays = [p for p, _, _ in parts]

    def block_shape(p, ax):
        if ax == len(p.shape) - 1:
            return p.shape[:-1] + (p.shape[-1] // 4,)
        return tuple(1 if i == ax else dim for i, dim in enumerate(p.shape))

    out_shapes = [jax.ShapeDtypeStruct((3,) + block_shape(p, ax), p.dtype) for p, ax, _ in parts]

    def block(ref, ax, chip):
        rank = len(ref.shape)
        if ax == rank - 1:
            cols = ref.shape[-1] // 4
            last = pl.ds(pl.multiple_of(chip * cols, 128), cols)
            return ref.at[tuple([slice(None)] * (rank - 1) + [last])]
        return ref.at[tuple(pl.ds(chip, 1) if i == ax else slice(None) for i in range(rank))]

    def run(in_refs, out_refs, send_sems, recv_sems, start):
        x, y, c, chips = _place()
        for a in range(n):
            ax, plan = parts[a][1], parts[a][2]
            for p, chip in enumerate(chips):
                src, dst = block(in_refs[a], ax, 2 * chip[0] + chip[1]), out_refs[a].at[p]
                sems = (send_sems.at[3 * a + p], recv_sems.at[3 * a + p])
                if start:
                    _remote_in_parts(src, dst, *sems, (*chip, c), plan)
                else:
                    _remote(src, dst, *sems, (*chip, c)).wait()

    return _Rider(arrays, out_shapes, 3 * n, run)


def _join_halves(joined):
    n = len(joined)
    arrays = [j for j, _, _ in joined]

    def body(*refs):
        out_refs = refs[n:2 * n]
        send_sems, recv_sems = refs[2 * n:]
        x, y, c, _ = _place()
        sibling = (x, y, 1 - c)

        def half(a, cc):
            rank = len(out_refs[a].shape)
            return out_refs[a].at[tuple(pl.ds(cc, 1) if i == joined[a][1] else slice(None) for i in range(rank))]

        sends = [_remote_in_parts(half(a, c), half(a, c), send_sems.at[a], recv_sems.at[a], sibling, joined[a][2])
                 for a in range(n)]
        for a, rem in enumerate(sends):
            rem.wait_send()
            _remote(half(a, 1 - c), half(a, 1 - c), send_sems.at[a], recv_sems.at[a], sibling).wait_recv()

    return _hbm_call(body, "join_halves", arrays, [jax.ShapeDtypeStruct(j.shape, j.dtype) for j in arrays],
                     [pltpu.SemaphoreType.DMA((n,)), pltpu.SemaphoreType.DMA((n,))],
                     aliases={a: a for a in range(n)})


def _allreduce_small(packed):
    rows, lanes = packed.shape
    half = rows // 2

    def body(in_ref, out_ref, pair_ref, gath_ref, send_sems, recv_sems):
        x, y, c, chips = _place()
        me = 2 * x + y
        sibling = (x, y, 1 - c)
        mine = pl.ds(pl.multiple_of(c * half, 8), half)
        theirs = pl.ds(pl.multiple_of((1 - c) * half, 8), half)
        to_sib = _remote(in_ref.at[theirs], pair_ref, send_sems.at[0], recv_sems.at[0], sibling)
        to_sib.start()
        to_sib.wait()
        gath_ref[me] = in_ref[mine] + pair_ref[...]
        sends = [_remote(gath_ref.at[me], gath_ref.at[me], send_sems.at[1 + p], recv_sems.at[1 + p], (*chip, c))
                 for p, chip in enumerate(chips)]
        for cp in sends:
            cp.start()
        for p, chip in enumerate(chips):
            slot = gath_ref.at[2 * chip[0] + chip[1]]
            _remote(slot, slot, send_sems.at[1 + p], recv_sems.at[1 + p], (*chip, c)).wait_recv()
        for cp in sends:
            cp.wait_send()
        out_ref[mine] = ((gath_ref[0] + gath_ref[1]) + gath_ref[2]) + gath_ref[3]
        back = _remote(out_ref.at[mine], out_ref.at[mine], send_sems.at[4], recv_sems.at[4], sibling)
        back.start()
        back.wait_send()
        _remote(out_ref.at[theirs], out_ref.at[theirs], send_sems.at[4], recv_sems.at[4], sibling).wait_recv()

    vmem = pl.BlockSpec(memory_space=pltpu.VMEM)
    return pl.pallas_call(
        body, name="allreduce_small",
        in_specs=[vmem], out_specs=vmem, out_shape=jax.ShapeDtypeStruct((rows, lanes), F32),
        scratch_shapes=[pltpu.VMEM((half, lanes), F32), pltpu.VMEM((4, half, lanes), F32),
                        pltpu.SemaphoreType.DMA((5,)), pltpu.SemaphoreType.DMA((5,))],
        compiler_params=pltpu.CompilerParams(has_side_effects=True, vmem_limit_bytes=32 * 1024 * 1024),
    )(packed)


SMALL = ("norm_pre", "pool_scale", "sgu_ln_g", "sgu_ln_b", "sgu_w", "sgu_b", "mem_norm", "branch_norm", "norm_post")
LARGE = ("w_in", "pool_w", "w_kv", "w_out")
ORDER = ("norm_pre", "w_in", "pool_w", "pool_scale", "sgu_ln_g", "sgu_ln_b", "sgu_w", "sgu_b", "mem_norm", "w_kv",
         "branch_norm", "w_out", "norm_post")


def _pack(arrays, extra=()):
    rows = [a.reshape(-1, 128) for a in arrays] + list(extra)
    pad = -sum(r.shape[0] for r in rows) % 16
    return jnp.concatenate(rows + ([jnp.zeros((pad, 128), F32)] if pad else []), axis=0)


def _unpack(packed, like):
    out, row = [], 0
    for a in like:
        rows = a.size // 128
        out.append(packed[row:row + rows].reshape(a.shape))
        row += rows
    return out


def kernel(x, mem, norm_pre, w_in, pool_w, pool_scale, sgu_ln_g, sgu_ln_b, sgu_w, sgu_b, mem_norm, w_kv, branch_norm, w_out, norm_post, loss_target, m_norm_pre, m_w_in, m_pool_w, m_pool_scale, m_sgu_ln_g, m_sgu_ln_b, m_sgu_w, m_sgu_b, m_mem_norm, m_w_kv, m_branch_norm, m_w_out, m_norm_post, v_norm_pre, v_w_in, v_pool_w, v_pool_scale, v_sgu_ln_g, v_sgu_ln_b, v_sgu_w, v_sgu_b, v_mem_norm, v_w_kv, v_branch_norm, v_w_out, v_norm_post):
    weights = dict(norm_pre=norm_pre, w_in=w_in, pool_w=pool_w, pool_scale=pool_scale, sgu_ln_g=sgu_ln_g,
                   sgu_ln_b=sgu_ln_b, sgu_w=sgu_w, sgu_b=sgu_b, mem_norm=mem_norm, w_kv=w_kv, branch_norm=branch_norm,
                   w_out=w_out, norm_post=norm_post)
    mom1 = dict(norm_pre=m_norm_pre, w_in=m_w_in, pool_w=m_pool_w, pool_scale=m_pool_scale, sgu_ln_g=m_sgu_ln_g,
                sgu_ln_b=m_sgu_ln_b, sgu_w=m_sgu_w, sgu_b=m_sgu_b, mem_norm=m_mem_norm, w_kv=m_w_kv,
                branch_norm=m_branch_norm, w_out=m_w_out, norm_post=m_norm_post)
    mom2 = dict(norm_pre=v_norm_pre, w_in=v_w_in, pool_w=v_pool_w, pool_scale=v_pool_scale, sgu_ln_g=v_sgu_ln_g,
                sgu_ln_b=v_sgu_ln_b, sgu_w=v_sgu_w, sgu_b=v_sgu_b, mem_norm=v_mem_norm, w_kv=v_w_kv,
                branch_norm=v_branch_norm, w_out=v_w_out, norm_post=v_norm_post)

    s, d = x.shape[1], x.shape[2]
    x2, mem2, tgt2 = x[0], mem[0], loss_target[0]
    t_branch = min(256, s)
    tm = min(512, s)

    core = lax.axis_index("c")
    chip = 2 * lax.axis_index("x") + lax.axis_index("y")
    pos = jnp.stack([core, chip]).astype(jnp.int32)
    n_in, n_kv, n_out = 4 * w_in.shape[2], 4 * w_kv.shape[1], 4 * w_out.shape[1]
    wi_rows, kv_rows, wo_rows = d // 8, n_kv // 8, n_out // 8

    def placed(shard, full_shape, block, grid, in_map, out_map, name):
        return _blockwise(_cast_copy, pos, [shard], [pl.BlockSpec(block, in_map)],
                          jax.ShapeDtypeStruct(full_shape, BF16), pl.BlockSpec(block, out_map), grid, name)

    kv_cols, pw_rows = w_kv.shape[2], GROUP // 8
    wi_own = placed(w_in[0], (d, n_in), (wi_rows, n_in // 4), (8,), lambda i, p: (i, 0), lambda i, p: (i, p[1]),
                    "place_w_in")
    wkv_own = placed(w_kv[0], (n_kv, kv_cols), (kv_rows, kv_cols), (2,), lambda i, p: (i, 0),
                     lambda i, p: (2 * p[1] + i, 0), "place_w_kv")
    wo_own = placed(w_out[0], (n_out, d), (wo_rows, d), (2,), lambda i, p: (i, 0), lambda i, p: (2 * p[1] + i, 0),
                    "place_w_out")
    pw_own = placed(pool_w[0], (4, GROUP, GROUP), (4, GROUP // 4, GROUP), (1,), lambda i, p: (0, 0, 0),
                    lambda i, p: (0, p[1], 0), "place_pool_w")

    x_pos, y_pos = lax.axis_index("x"), lax.axis_index("y")
    chips = jnp.stack([chip, 2 * (1 - x_pos) + y_pos, 2 * x_pos + 1 - y_pos,
                       2 * (1 - x_pos) + 1 - y_pos]).astype(jnp.int32)
    mem_g = mem_norm.reshape(1, d)
    proj, h, h_t, wi_full = _proj_piece(chips, 0, 1, x2, norm_pre, None, None, n_in,
                                   _gather_rider([wi_own], [0], peers=(0, 1)), tm, "proj_own")
    proj, wi_full = _proj_piece(chips, 1, 2, h, None, None, proj, n_in,
                                _gather_rider([wi_full], [0], peers=(2,)), tm, "proj_neighbours")
    proj, wkv_full, pw_full = _proj_piece(chips, 3, 1, h, None, wi_full, proj, n_in,
                                          _gather_rider([wkv_own, pw_own], [1, 3]), tm, "proj_diagonal")
    k_m, v_m = _kv_fwd(mem2, mem_g, wkv_full)
    bias_full = jnp.repeat(sgu_b[0].T, CHUNK, axis=1)
    y, y_t, wo_full = _branches_fwd(proj, pw_full, pool_scale, sgu_ln_g, sgu_ln_b, sgu_w[0], bias_full, k_m, v_m,
                               branch_norm, t_branch, _gather_rider([wo_own], [2]))
    loss_local, dz, dout, dy, g_norm_post = _out_loss(y, wo_full, x2, tgt2, norm_post, min(256, s))

    def pair_sums(views):
        theirs = _exchange_halves([(v[0], v[1], v[2]) for v in views], "exchange_for_" + views[0][9])
        return [_blockwise(_pair_sum, pos, [v[0], th], [pl.BlockSpec(v[3], v[4][0]), pl.BlockSpec(v[3], v[4][1])],
                           jax.ShapeDtypeStruct(v[5], BF16), pl.BlockSpec(v[6], v[7]), v[8], v[9])
                for v, th in zip(views, theirs)]

    tk = min(1024, s)
    (g_wo,) = _grad_rows(y_t, dout, pos, lambda i, p: i, n_out, n_out // 2, 1024, tk, "grad_w_out")
    (ps_wo,) = pair_sums([
        (g_wo.reshape(4, 2, wo_rows, d), 1, [(0, 4), (2, 2)], (1, 1, wo_rows, d),
         (lambda i, p: (i, p[0], 0, 0), lambda i, p: (i, 0, 0, 0)), (4, wo_rows, d), (1, wo_rows, d),
         lambda i, p: (i, 0, 0), (4,), "pair_sum_w_out")])
    (dproj, g_pw, g_pool_scale, g_ln_g, g_ln_b, g_sgu_w, g_sgu_b_t, g_branch_norm, dk, dv, landed_wo) = _branches_bwd(
        proj, dy, pw_full, pool_scale, sgu_ln_g, sgu_ln_b, sgu_w[0], jnp.swapaxes(sgu_w[0], 1, 2), bias_full,
        k_m, v_m, branch_norm, t_branch, _scatter_rider([(ps_wo, 0, [(1, 2)])]))
    g_wkv, g_mem_norm = _kv_bwd(mem2, mem_g, wkv_full, dk, dv)
    ps_kv, ps_pw = pair_sums([
        (g_wkv.reshape(4, 2, kv_rows, kv_cols), 1, [(0, 4), (2, 2)], (1, 1, kv_rows, kv_cols),
         (lambda i, p: (i, p[0], 0, 0), lambda i, p: (i, 0, 0, 0)), (4, kv_rows, kv_cols), (1, kv_rows, kv_cols),
         lambda i, p: (i, 0, 0), (4,), "pair_sum_w_kv"),
        (g_pw.astype(BF16).reshape(4, 4, 2, pw_rows, GROUP), 2, [(0, 4)], (1, 4, 1, pw_rows, GROUP),
         (lambda i, p: (i, 0, p[0], 0, 0), lambda i, p: (i, 0, 0, 0, 0)), (4, 4, pw_rows, GROUP),
         (1, 4, pw_rows, GROUP), lambda i, p: (i, 0, 0, 0), (4,), "pair_sum_pool_w")])
    gwi_theirs, landed_kv, landed_pw = _grad_rows(
        h_t, dproj, pos, lambda i, p: 1 - p[0], d // 2, d // 2, n_in // 4, tk, "grad_w_in_sibling_half",
        _scatter_rider([(ps_kv, 0, [(1, 2)]), (ps_pw, 1, [])]))
    gwi_mine, gwi_from_sibling = _grad_rows(h_t, dproj, pos, lambda i, p: p[0], d // 2, d // 2, n_in // 4, tk,
                                            "grad_w_in_own_half", _exchange_rider([gwi_theirs], [[(0, 16)]]))
    ps_wi = _elementwise(_pair_sum, [gwi_mine, gwi_from_sibling], [BF16], "pair_sum_w_in")[0]
    grad_x, g_norm_pre, landed_wi = _dx_call(dproj, wi_full, x2, dz, norm_pre, tm, 1024,
                                             _scatter_rider([(ps_wi, 1, [(0, 4)])]))
    psum = [ps_wi, ps_kv, ps_wo, ps_pw]
    landed = [landed_wi, landed_kv, landed_wo, landed_pw]
    from_chip = lambda spec_shape, rank: [
        pl.BlockSpec(spec_shape, functools.partial(lambda i, p, q: (q, i) + (0,) * (rank - 2), q=q))
        for q in range(3)]
    joined = _join_halves([
        (_blockwise(_four_sum, pos, [psum[0]] + [landed[0]] * 3,
                    [pl.BlockSpec((256, n_in // 4), lambda i, p: (i, p[1]))] + from_chip((1, 256, n_in // 4), 3),
                    jax.ShapeDtypeStruct((2, d // 2, n_in // 4), F32),
                    pl.BlockSpec((1, 256, n_in // 4), lambda i, p: (p[0], i, 0)), (d // 2 // 256,), "chip_sum_w_in"),
         0, [(1, 8)]),
        (_blockwise(_four_sum, pos, [psum[1]] + [landed[1]] * 3,
                    [pl.BlockSpec((1, kv_rows, kv_cols), lambda i, p: (p[1], 0, 0))]
                    + from_chip((1, 1, kv_rows, kv_cols), 4),
                    jax.ShapeDtypeStruct((2, kv_rows, kv_cols), F32),
                    pl.BlockSpec((1, kv_rows, kv_cols), lambda i, p: (p[0], 0, 0)), (1,), "chip_sum_w_kv"),
         0, [(1, 2)]),
        (_blockwise(_four_sum, pos, [psum[2]] + [landed[2]] * 3,
                    [pl.BlockSpec((1, wo_rows, d), lambda i, p: (p[1], 0, 0))] + from_chip((1, 1, wo_rows, d), 4),
                    jax.ShapeDtypeStruct((2, wo_rows, d), F32),
                    pl.BlockSpec((1, wo_rows, d), lambda i, p: (p[0], 0, 0)), (1,), "chip_sum_w_out"),
         0, [(1, 2)]),
        (_blockwise(_four_sum, pos, [psum[3]] + [landed[3]] * 3,
                    [pl.BlockSpec((4, 1, pw_rows, GROUP), lambda i, p: (0, p[1], 0, 0))]
                    + from_chip((1, 4, 1, pw_rows, GROUP), 5),
                    jax.ShapeDtypeStruct((4, 2, pw_rows, GROUP), F32),
                    pl.BlockSpec((4, 1, pw_rows, GROUP), lambda i, p: (0, p[0], 0, 0)), (1,), "chip_sum_pool_w"),
         1, []),
    ])
    grads = {"w_in": joined[0].reshape(w_in.shape), "w_kv": joined[1].reshape(w_kv.shape),
             "w_out": joined[2].reshape(w_out.shape), "pool_w": joined[3].reshape(pool_w.shape)}

    small_local = dict(norm_pre=g_norm_pre, pool_scale=g_pool_scale, sgu_ln_g=g_ln_g, sgu_ln_b=g_ln_b,
                       sgu_w=g_sgu_w, sgu_b=g_sgu_b_t.T, mem_norm=g_mem_norm, branch_norm=g_branch_norm,
                       norm_post=g_norm_post)
    small_rows = sum(weights[n].size for n in SMALL) // 128
    small_sum = _allreduce_small(_pack([small_local[n] for n in SMALL], [jnp.pad(loss_local, ((0, 7), (0, 127)))]))
    for n, g in zip(SMALL, _unpack(small_sum, [weights[n] for n in SMALL])):
        grads[n] = g
    loss = small_sum[small_rows, 0]

    delta, new_m, new_v = {}, {}, {}
    packed = [small_sum if src is grads else _pack([src[n] for n in SMALL]) for src in (weights, grads, mom1, mom2)]
    outs = _elementwise(_adamw, packed, [F32, F32, F32], "adamw_small")
    for dst, o in zip((delta, new_m, new_v), outs):
        for n, a in zip(SMALL, _unpack(o, [weights[n] for n in SMALL])):
            dst[n] = a
    for n in LARGE:
        cols = weights[n].shape[-1]
        outs = _elementwise(_adamw, [src[n].reshape(-1, cols) for src in (weights, grads, mom1, mom2)],
                            [F32, F32, F32], "adamw_" + n)
        for dst, o in zip((delta, new_m, new_v), outs):
            dst[n] = o.reshape(weights[n].shape)

    return (loss, grad_x[None], *[grads[n] for n in ORDER], *[delta[n] for n in ORDER],
            *[new_m[n] for n in ORDER], *[new_v[n] for n in ORDER])
```

```python
import functools

import jax
import jax.numpy as jnp
from jax import lax
from jax.experimental import pallas as pl
from jax.experimental.pallas import tpu as pltpu

F32 = jnp.float32
BF16 = jnp.bfloat16
EPS = 1e-6
MESH = pl.DeviceIdType.MESH
ANY = pl.BlockSpec(memory_space=pl.ANY)

POOL_WINDOWS = (2, 4, 8, 16)
GROUP = 256
HALO = 16
CHUNK = 128
N_SGU_HEADS = 8
N_ATT_HEADS = 4
ATT_DIM = 256
WIDTH = 1024
ATT_SCALE = 1.0 / 16.0

ADAM_LR = 0.001
ADAM_B1 = 0.9
ADAM_B2 = 0.999
ADAM_EPS = 1e-08
ADAM_WD = 0.01
ADAM_STEP = 10

VMEM_LIMIT = 60 * 1024 * 1024
MAX_PARTS = 4


def _params(n_grid_axes, vmem=VMEM_LIMIT):
    return pltpu.CompilerParams(dimension_semantics=("arbitrary",) * n_grid_axes, vmem_limit_bytes=vmem)


def _dot(a, b, dims):
    return lax.dot_general(a, b, (dims, ((), ())), preferred_element_type=F32)


NN = ((1,), (0,))
NT = ((1,), (1,))
TN = ((0,), (0,))


class _Rider:
    def __init__(self, inputs, out_shapes, n_sems, run, aliases=None):
        self.inputs, self.out_shapes, self.n_sems, self.run = list(inputs), list(out_shapes), n_sems, run
        self.aliases = aliases or {}


def _call(body, name, grid, in_specs, out_specs, out_shape, scratch_shapes, inputs, rider=None, prefetch=None,
          aliases=None, rider_refs=False):
    n_in, n_out, n_scr = len(in_specs), len(out_specs), len(scratch_shapes)
    r_in = len(rider.inputs) if rider else 0
    r_out = len(rider.out_shapes) if rider else 0
    n_pre = 0 if prefetch is None else 1

    def whole_body(*refs):
        pre, refs = refs[:n_pre], refs[n_pre:]
        ins, rider_ins = refs[:n_in], refs[n_in:n_in + r_in]
        refs = refs[n_in + r_in:]
        outs, rider_outs = refs[:n_out], refs[n_out:n_out + r_out]
        refs = refs[n_out + r_out:]
        scratch, sems = refs[:n_scr], refs[n_scr:]
        extra = {"rider_outs": rider_outs} if rider_refs else {}
        if rider is None:
            body(*pre, *ins, *outs, *scratch, **extra)
            return
        ids = [pl.program_id(ax) for ax in range(len(grid))]
        first = functools.reduce(lambda p, q: p & q, [i == 0 for i in ids])
        last = functools.reduce(lambda p, q: p & q, [i == g - 1 for i, g in zip(ids, grid)])

        @pl.when(first)
        def _():
            rider.run(rider_ins, rider_outs, *sems, True)

        body(*pre, *ins, *outs, *scratch, **extra)

        @pl.when(last)
        def _():
            rider.run(rider_ins, rider_outs, *sems, False)

    io_aliases = {n_pre + i: o for i, o in (aliases or {}).items()}
    scratch_all = list(scratch_shapes)
    if rider:
        io_aliases.update({n_pre + n_in + i: n_out + o for i, o in rider.aliases.items()})
        scratch_all += [pltpu.SemaphoreType.DMA((rider.n_sems,)), pltpu.SemaphoreType.DMA((rider.n_sems,))]
    specs = dict(grid=grid, in_specs=list(in_specs) + [ANY] * r_in, out_specs=tuple(out_specs) + (ANY,) * r_out,
                 scratch_shapes=scratch_all)
    if n_pre:
        specs = dict(grid_spec=pltpu.PrefetchScalarGridSpec(num_scalar_prefetch=1, **specs))
    outs = pl.pallas_call(
        whole_body, name=name, **specs,
        out_shape=tuple(out_shape) + tuple(rider.out_shapes if rider else ()),
        input_output_aliases=io_aliases, compiler_params=_params(len(grid)),
    )(*([prefetch] if n_pre else []), *inputs, *(rider.inputs if rider else []))
    return tuple(outs)


def _grad_rows(a_t, b, pos, row_of, m, tm, tn, tk, name, rider=None):
    k, n = a_t.shape[1], b.shape[1]
    nk = k // tk
    out_dtype, dims, a = BF16, NN, a_t
    a_spec = pl.BlockSpec((tm, tk), lambda i, j, kk, p: (row_of(i, p), kk))
    b_spec = pl.BlockSpec((tk, tn), lambda i, j, kk, p: (kk, j))

    def body(pos_ref, a_ref, b_ref, o_ref, *acc):
        part = lambda: _dot(a_ref[...], b_ref[...], dims)
        if nk == 1:
            o_ref[...] = part().astype(out_dtype)
            return
        (acc_ref,) = acc
        kk = pl.program_id(2)

        @pl.when(kk == 0)
        def _():
            acc_ref[...] = part()

        @pl.when((kk > 0) & (kk < nk - 1))
        def _():
            acc_ref[...] += part()

        @pl.when(kk == nk - 1)
        def _():
            o_ref[...] = (acc_ref[...] + part()).astype(out_dtype)

    return _call(body, name, (m // tm, n // tn, nk), [a_spec, b_spec],
                 [pl.BlockSpec((tm, tn), lambda i, j, kk, p: (i, j))], [jax.ShapeDtypeStruct((m, n), out_dtype)],
                 [pltpu.VMEM((tm, tn), F32)] if nk > 1 else [], [a, b], rider, prefetch=pos)


def _proj_piece(chips, first, n_shards, src, g_pre, w_in, proj_in, n_cols, rider, tm, name):
    s, d = src.shape
    cols = n_cols // 4
    fused = g_pre is not None

    def body(chips_ref, *refs, rider_outs=()):
        refs = list(refs)
        src_ref = refs.pop(0)
        g_ref = refs.pop(0) if fused else None
        w_ref = refs.pop(0) if w_in is not None else rider_outs[0]
        if proj_in is not None:
            refs.pop(0)
        proj_ref = refs.pop(0)
        h_ref, ht_ref = (refs.pop(0), refs.pop(0)) if fused else (None, None)
        wbuf, sem = refs
        q, i = pl.program_id(0), pl.program_id(1)

        @pl.when(i == 0)
        def _():
            at = pl.multiple_of(chips_ref[first + q] * cols, 128)
            cp = pltpu.make_async_copy(w_ref.at[:, pl.ds(at, cols)], wbuf, sem)
            cp.start()
            cp.wait()

        if fused:
            xv = src_ref[...]
            r = lax.rsqrt(jnp.mean(xv * xv, axis=-1, keepdims=True) + EPS)
            h = (xv * r * g_ref[...]).astype(BF16)
            h_ref[...] = h
            ht_ref[...] = h.T
        else:
            h = src_ref[...]
        proj_ref[...] = _dot(h, wbuf[...], NN)

    row = lambda q, i, ch: (i, 0)
    inputs, in_specs = [src], [pl.BlockSpec((tm, d), row)]
    if fused:
        inputs.append(g_pre)
        in_specs.append(pl.BlockSpec((1, d), lambda q, i, ch: (0, 0)))
    if w_in is not None:
        inputs.append(w_in)
        in_specs.append(ANY)
    aliases = {}
    if proj_in is not None:
        aliases[len(inputs)] = 0
        inputs.append(proj_in)
        in_specs.append(ANY)
    out_specs = [pl.BlockSpec((tm, cols), lambda q, i, ch: (i, ch[first + q]))]
    out_shape = [jax.ShapeDtypeStruct((s, n_cols), F32)]
    if fused:
        assert n_shards == 1
        out_specs += [pl.BlockSpec((tm, d), row), pl.BlockSpec((d, tm), lambda q, i, ch: (0, i))]
        out_shape += [jax.ShapeDtypeStruct((s, d), BF16), jax.ShapeDtypeStruct((d, s), BF16)]
    return _call(body, name, (n_shards, s // tm), in_specs, out_specs, out_shape,
                 [pltpu.VMEM((d, cols), BF16), pltpu.SemaphoreType.DMA(())], inputs, rider, prefetch=chips,
                 aliases=aliases, rider_refs=True)


def _kv_fwd(mem, g, w_kv):
    m, d = mem.shape

    def body(mem_ref, g_ref, w_ref, k_ref, v_ref):
        mv = mem_ref[...]
        r = lax.rsqrt(jnp.mean(mv * mv, axis=-1, keepdims=True) + EPS)
        mem_n = (mv * r * g_ref[...]).astype(BF16)
        kv = _dot(mem_n, w_ref[...], NN)
        k_ref[...] = kv[:, :WIDTH].astype(BF16)
        v_ref[...] = kv[:, WIDTH:].astype(BF16)

    return pl.pallas_call(
        body, name="kv_fwd",
        out_shape=(jax.ShapeDtypeStruct((m, WIDTH), BF16), jax.ShapeDtypeStruct((m, WIDTH), BF16)),
        compiler_params=_params(0),
    )(mem, g, w_kv)


def _kv_bwd(mem, g, w_kv, dk, dv):
    m, d = mem.shape
    n = w_kv.shape[1]
    col = 512

    def body(mem_ref, g_ref, w_ref, dk_ref, dv_ref, dw_ref, dg_ref):
        mv = mem_ref[...]
        r = lax.rsqrt(jnp.mean(mv * mv, axis=-1, keepdims=True) + EPS)
        mem_hat = mv * r
        mem_n = (mem_hat * g_ref[...]).astype(BF16)
        dkv = jnp.concatenate([dk_ref[...], dv_ref[...]], axis=1).astype(BF16)
        for j in range(n // col):
            dw_ref[:, j * col:(j + 1) * col] = _dot(mem_n, dkv[:, j * col:(j + 1) * col], TN).astype(BF16)
        dmem_n = _dot(dkv, w_ref[...], NT)
        dg_ref[...] = jnp.sum(dmem_n * mem_hat, axis=0, keepdims=True)

    return pl.pallas_call(
        body, name="kv_bwd",
        out_shape=(jax.ShapeDtypeStruct((d, n), BF16), jax.ShapeDtypeStruct((1, d), F32)),
        compiler_params=_params(0),
    )(mem, g, w_kv, dk, dv)


def _sigmoid(x):
    return 1.0 / (1.0 + jnp.exp(-x))


def _inv_counts(t0, t):
    pos = (t0 + lax.broadcasted_iota(jnp.int32, (t, 1), 0) + 1).astype(F32)
    return [1.0 / jnp.minimum(pos, float(w)) for w in POOL_WINDOWS]


def _window_sums(ext, t, backward):
    n = t + HALO
    parts = []
    for gi, w in enumerate(POOL_WINDOWS):
        s = ext[:, gi * GROUP:(gi + 1) * GROUP]
        k = 1
        while k < w:
            s = s + pltpu.roll(s, (n - k) if backward else k, axis=0)
            k *= 2
        parts.append(s[:t] if backward else s[HALO:])
    return parts


def _pool_fwd(xa, halo, inv, pool_w):
    t = xa.shape[0]
    sums = _window_sums(jnp.concatenate([halo, xa], axis=0), t, backward=False)
    d = jnp.concatenate([sums[gi] * inv[gi] - xa[:, gi * GROUP:(gi + 1) * GROUP] for gi in range(4)], axis=1)
    d = d.astype(BF16)
    y = jnp.concatenate([_dot(d[:, gi * GROUP:(gi + 1) * GROUP], pool_w[gi], NN) for gi in range(4)], axis=1)
    return d, y


def _layernorm_fwd(v):
    mu = jnp.mean(v, axis=-1, keepdims=True)
    xc = v - mu
    rstd = lax.rsqrt(jnp.mean(xc * xc, axis=-1, keepdims=True) + EPS)
    return xc * rstd, rstd


def _tril_mask(transposed):
    r = lax.broadcasted_iota(jnp.int32, (CHUNK, CHUNK), 0)
    c = lax.broadcasted_iota(jnp.int32, (CHUNK, CHUNK), 1)
    return (r <= c) if transposed else (r >= c)


def _sgu_mix(w_ref, vals, transposed):
    t = vals.shape[0]
    mask = _tril_mask(transposed)
    ws = [jnp.where(mask, w_ref[h], 0.0).astype(BF16) for h in range(N_SGU_HEADS)]
    rows = []
    for ci in range(t // CHUNK):
        blk = vals[ci * CHUNK:(ci + 1) * CHUNK]
        rows.append(jnp.concatenate(
            [_dot(ws[h], blk[:, h * CHUNK:(h + 1) * CHUNK], NN) for h in range(N_SGU_HEADS)], axis=1))
    return jnp.concatenate(rows, axis=0)


def _attn_fwd(q, k, v):
    ps, os_ = [], []
    for h in range(N_ATT_HEADS):
        sl = slice(h * ATT_DIM, (h + 1) * ATT_DIM)
        s = _dot(q[:, sl], k[:, sl], NT) * ATT_SCALE
        s = s - jnp.max(s, axis=-1, keepdims=True)
        e = jnp.exp(s)
        p = e * (1.0 / jnp.sum(e, axis=-1, keepdims=True))
        ps.append(p)
        os_.append(_dot(p.astype(BF16), v[:, sl], NN))
    return ps, jnp.concatenate(os_, axis=1)


def _rms_branch(y_pre):
    r = lax.rsqrt(jnp.mean(y_pre * y_pre, axis=-1, keepdims=True) + EPS)
    return y_pre * r, r


def _branch_specs(t, n_tiles, order):
    width_in = 7 * WIDTH
    tile = lambda i: order(i)
    per_halo = t // HALO
    const2 = lambda i: (0, 0)
    const3 = lambda i: (0, 0, 0)
    return [
        pl.BlockSpec((t, width_in), lambda i: (tile(i), 0)),
        pl.BlockSpec((HALO, WIDTH), lambda i: (jnp.maximum(tile(i) * per_halo - 1, 0), 0)),
        pl.BlockSpec((4, GROUP, GROUP), const3),
        pl.BlockSpec((1, WIDTH), const2),
        pl.BlockSpec((1, WIDTH), const2),
        pl.BlockSpec((1, WIDTH), const2),
        pl.BlockSpec((N_SGU_HEADS, CHUNK, CHUNK), const3),
        pl.BlockSpec((CHUNK, WIDTH), const2),
        pl.BlockSpec((MEM_ROWS, WIDTH), const2),
        pl.BlockSpec((MEM_ROWS, WIDTH), const2),
        pl.BlockSpec((1, 3 * WIDTH), const2),
    ]


MEM_ROWS = 256


def _branches_fwd(proj, pool_w, pool_scale, ln_g, ln_b, sgu_w, bias_full, k, v, branch_norm, t, rider=None):
    s = proj.shape[0]
    n_tiles = s // t

    def body(proj_ref, halo_ref, pw_ref, ps_ref, lg_ref, lb_ref, sw_ref, sb_ref, k_ref, v_ref, bn_ref, y_ref, yt_ref):
        i = pl.program_id(0)
        col = lambda j: proj_ref[:, j * WIDTH:(j + 1) * WIDTH]

        def put(branch, y_pre):
            sl = slice(branch * WIDTH, (branch + 1) * WIDTH)
            val = (_rms_branch(y_pre)[0] * bn[:, sl]).astype(BF16)
            y_ref[:, sl] = val
            yt_ref[sl, :] = val.T

        bn = bn_ref[...]
        halo = jnp.where(i > 0, halo_ref[...], 0.0)
        _, y_pool = _pool_fwd(col(0), halo, _inv_counts(i * t, t), pw_ref[...])
        ga = col(1)
        ya = y_pool * ps_ref[...] * (ga * _sigmoid(ga))
        put(0, ya)
        vhat, _ = _layernorm_fwd(col(3))
        vn = (vhat * lg_ref[...] + lb_ref[...]).astype(BF16)
        z = _sgu_mix(sw_ref, vn, transposed=False) + jnp.tile(sb_ref[...], (t // CHUNK, 1))
        gb = col(4)
        yb = col(2) * z * (gb * _sigmoid(gb))
        put(1, yb)
        _, o = _attn_fwd(col(5).astype(BF16), k_ref[...], v_ref[...])
        gc = col(6)
        yc = o * (gc * _sigmoid(gc))
        put(2, yc)

    return _call(body, "branches_fwd", (n_tiles,), _branch_specs(t, n_tiles, lambda i: i),
                 [pl.BlockSpec((t, 3 * WIDTH), lambda i: (i, 0)), pl.BlockSpec((3 * WIDTH, t), lambda i: (0, i))],
                 [jax.ShapeDtypeStruct((s, 3 * WIDTH), BF16), jax.ShapeDtypeStruct((3 * WIDTH, s), BF16)], [],
                 [proj, proj, pool_w, pool_scale, ln_g, ln_b, sgu_w, bias_full, k, v, branch_norm], rider)


def _branches_bwd(proj, dy, pool_w, pool_scale, ln_g, ln_b, sgu_w, sgu_wt, bias_full, k, v, branch_norm, t, rider=None):
    s = proj.shape[0]
    n_tiles = s // t
    n_chunks = t // CHUNK
    order = lambda i: n_tiles - 1 - i

    def body(proj_ref, halo_ref, pw_ref, ps_ref, lg_ref, lb_ref, sw_ref, sb_ref, k_ref, v_ref, bn_ref,
             swt_ref, dy_ref,
             dproj_ref, dpw_ref, dps_ref, dlg_ref, dlb_ref, dsw_ref, dsb_ref, dbn_ref, dk_ref, dv_ref,
             carry_ref, dbias_ref):
        step = pl.program_id(0)
        i = order(step)

        @pl.when(step == 0)
        def _():
            for ref in (dpw_ref, dps_ref, dlg_ref, dlb_ref, dsw_ref, dbn_ref, dk_ref, dv_ref, carry_ref, dbias_ref):
                ref[...] = jnp.zeros(ref.shape, ref.dtype)

        col = lambda j: proj_ref[:, j * WIDTH:(j + 1) * WIDTH]
        bn = bn_ref[...]

        def norm_bwd(y_pre, sl):
            yhat, r = _rms_branch(y_pre)
            dyv = dy_ref[:, sl].astype(F32)
            dbn_ref[:, sl] += jnp.sum(dyv * yhat, axis=0, keepdims=True)
            dyhat = dyv * bn[:, sl]
            return r * (dyhat - yhat * jnp.mean(dyhat * yhat, axis=-1, keepdims=True))

        def gate(gv):
            sg = _sigmoid(gv)
            return gv * sg, sg * (1.0 + gv * (1.0 - sg))

        inv = _inv_counts(i * t, t)
        halo = jnp.where(i > 0, halo_ref[...], 0.0)
        pw = pw_ref[...]
        d, y_pool = _pool_fwd(col(0), halo, inv, pw)
        scale = ps_ref[...]
        silu_a, dsilu_a = gate(col(1))
        pa = y_pool * scale
        dya = norm_bwd(pa * silu_a, slice(0, WIDTH))
        dproj_ref[:, WIDTH:2 * WIDTH] = (dya * pa * dsilu_a).astype(BF16)
        dpa = dya * silu_a
        dps_ref[...] += jnp.sum(dpa * y_pool, axis=0, keepdims=True)
        dy_pool = (dpa * scale).astype(BF16)
        dd_parts, ddc_parts = [], []
        for gi in range(4):
            sl = slice(gi * GROUP, (gi + 1) * GROUP)
            dpw_ref[gi] += _dot(d[:, sl], dy_pool[:, sl], TN)
            dd = _dot(dy_pool[:, sl], pw[gi], NT)
            dd_parts.append(dd)
            ddc_parts.append(dd * inv[gi])
        ddc = jnp.concatenate(ddc_parts, axis=1)
        sums = _window_sums(jnp.concatenate([ddc, carry_ref[...]], axis=0), t, backward=True)
        carry_ref[...] = ddc[:HALO]
        dproj_ref[:, 0:WIDTH] = jnp.concatenate([sums[gi] - dd_parts[gi] for gi in range(4)], axis=1).astype(BF16)

        vhat, rstd = _layernorm_fwd(col(3))
        lg = lg_ref[...]
        vn = (vhat * lg + lb_ref[...]).astype(BF16)
        z = _sgu_mix(sw_ref, vn, transposed=False) + jnp.tile(sb_ref[...], (n_chunks, 1))
        u = col(2)
        silu_b, dsilu_b = gate(col(4))
        uz = u * z
        dyb = norm_bwd(uz * silu_b, slice(WIDTH, 2 * WIDTH))
        dproj_ref[:, 4 * WIDTH:5 * WIDTH] = (dyb * uz * dsilu_b).astype(BF16)
        duz = dyb * silu_b
        dproj_ref[:, 2 * WIDTH:3 * WIDTH] = (duz * z).astype(BF16)
        dz = duz * u
        dz_b = dz.astype(BF16)
        for ci in range(n_chunks):
            rows = slice(ci * CHUNK, (ci + 1) * CHUNK)
            dbias_ref[...] += dz[rows]
            for h in range(N_SGU_HEADS):
                sl = slice(h * CHUNK, (h + 1) * CHUNK)
                dsw_ref[h] += _dot(dz_b[rows, sl], vn[rows, sl], NT)
        dvn = _sgu_mix(swt_ref, dz_b, transposed=True)
        dlg_ref[...] += jnp.sum(dvn * vhat, axis=0, keepdims=True)
        dlb_ref[...] += jnp.sum(dvn, axis=0, keepdims=True)
        dvhat = dvn * lg
        dvb = rstd * (dvhat - jnp.mean(dvhat, axis=-1, keepdims=True)
                      - vhat * jnp.mean(dvhat * vhat, axis=-1, keepdims=True))
        dproj_ref[:, 3 * WIDTH:4 * WIDTH] = dvb.astype(BF16)

        q = col(5).astype(BF16)
        kv_k, kv_v = k_ref[...], v_ref[...]
        ps, o = _attn_fwd(q, kv_k, kv_v)
        silu_c, dsilu_c = gate(col(6))
        dyc = norm_bwd(o * silu_c, slice(2 * WIDTH, 3 * WIDTH))
        dproj_ref[:, 6 * WIDTH:7 * WIDTH] = (dyc * o * dsilu_c).astype(BF16)
        do = (dyc * silu_c).astype(BF16)
        dq_parts = []
        for h in range(N_ATT_HEADS):
            sl = slice(h * ATT_DIM, (h + 1) * ATT_DIM)
            p = ps[h]
            dp = _dot(do[:, sl], kv_v[:, sl], NT)
            ds = (p * (dp - jnp.sum(p * dp, axis=-1, keepdims=True)) * ATT_SCALE).astype(BF16)
            dq_parts.append(_dot(ds, kv_k[:, sl], NN))
            dk_ref[:, sl] += _dot(ds, q[:, sl], TN)
            dv_ref[:, sl] += _dot(p.astype(BF16), do[:, sl], TN)
        dproj_ref[:, 5 * WIDTH:6 * WIDTH] = jnp.concatenate(dq_parts, axis=1).astype(BF16)

        @pl.when(step == n_tiles - 1)
        def _():
            keep = _tril_mask(transposed=False)
            for h in range(N_SGU_HEADS):
                dsw_ref[h] = jnp.where(keep, dsw_ref[h], 0.0)
            dsb_ref[...] = jnp.concatenate(
                [jnp.sum(dbias_ref[:, h * CHUNK:(h + 1) * CHUNK], axis=1, keepdims=True)
                 for h in range(N_SGU_HEADS)], axis=1)

    const2 = lambda i: (0, 0)
    const3 = lambda i: (0, 0, 0)
    out_shapes = (
        jax.ShapeDtypeStruct((s, 7 * WIDTH), BF16),
        jax.ShapeDtypeStruct((4, GROUP, GROUP), F32),
        jax.ShapeDtypeStruct((1, WIDTH), F32),
        jax.ShapeDtypeStruct((1, WIDTH), F32),
        jax.ShapeDtypeStruct((1, WIDTH), F32),
        jax.ShapeDtypeStruct((N_SGU_HEADS, CHUNK, CHUNK), F32),
        jax.ShapeDtypeStruct((CHUNK, N_SGU_HEADS), F32),
        jax.ShapeDtypeStruct((1, 3 * WIDTH), F32),
        jax.ShapeDtypeStruct((MEM_ROWS, WIDTH), F32),
        jax.ShapeDtypeStruct((MEM_ROWS, WIDTH), F32),
    )
    out_specs = (
        pl.BlockSpec((t, 7 * WIDTH), lambda i: (order(i), 0)),
        pl.BlockSpec((4, GROUP, GROUP), const3),
        pl.BlockSpec((1, WIDTH), const2),
        pl.BlockSpec((1, WIDTH), const2),
        pl.BlockSpec((1, WIDTH), const2),
        pl.BlockSpec((N_SGU_HEADS, CHUNK, CHUNK), const3),
        pl.BlockSpec((CHUNK, N_SGU_HEADS), const2),
        pl.BlockSpec((1, 3 * WIDTH), const2),
        pl.BlockSpec((MEM_ROWS, WIDTH), const2),
        pl.BlockSpec((MEM_ROWS, WIDTH), const2),
    )
    in_specs = _branch_specs(t, n_tiles, order) + [
        pl.BlockSpec((N_SGU_HEADS, CHUNK, CHUNK), const3),
        pl.BlockSpec((t, 3 * WIDTH), lambda i: (order(i), 0)),
    ]
    return _call(body, "branches_bwd", (n_tiles,), in_specs, out_specs, out_shapes,
                 [pltpu.VMEM((HALO, WIDTH), F32), pltpu.VMEM((CHUNK, WIDTH), F32)],
                 [proj, proj, pool_w, pool_scale, ln_g, ln_b, sgu_w, bias_full, k, v, branch_norm, sgu_wt, dy], rider)


def _out_loss(y, w_out, x, target, g_post, tm):
    s, d = x.shape
    e_w = y.shape[1]
    n_tiles = s // tm

    def body(y_ref, w_ref, x_ref, t_ref, g_ref, loss_ref, dz_ref, dout_ref, dy_ref, dg_ref, sq_ref):
        i = pl.program_id(0)

        @pl.when(i == 0)
        def _():
            sq_ref[...] = jnp.zeros(sq_ref.shape, F32)
            dg_ref[...] = jnp.zeros(dg_ref.shape, F32)

        w = w_ref[...]
        out = _dot(y_ref[...], w, NN)
        r = lax.rsqrt(jnp.mean(out * out, axis=-1, keepdims=True) + EPS)
        outn = out * r
        g = g_ref[...]
        err = (x_ref[...] + outn * g) - t_ref[...]
        sq_ref[...] += jnp.sum(err * err, axis=0, keepdims=True)
        dz = err * (1.0 / d)
        dz_ref[...] = dz
        dg_ref[...] += jnp.sum(dz * outn, axis=0, keepdims=True)
        doutn = dz * g
        dout = (r * (doutn - outn * jnp.mean(doutn * outn, axis=-1, keepdims=True))).astype(BF16)
        dout_ref[...] = dout
        dy_ref[...] = _dot(dout, w, NT).astype(BF16)

        @pl.when(i == n_tiles - 1)
        def _():
            loss_ref[...] = 0.5 * jnp.sum(sq_ref[...], axis=1, keepdims=True) * (1.0 / d)

    row = lambda i: (i, 0)
    const2 = lambda i: (0, 0)
    return pl.pallas_call(
        body, name="out_loss", grid=(n_tiles,),
        in_specs=[
            pl.BlockSpec((tm, e_w), row),
            pl.BlockSpec((e_w, d), const2, pipeline_mode=pl.Buffered(1)),
            pl.BlockSpec((tm, d), row),
            pl.BlockSpec((tm, d), row),
            pl.BlockSpec((1, d), const2),
        ],
        out_specs=(
            pl.BlockSpec((1, 1), const2),
            pl.BlockSpec((tm, d), row),
            pl.BlockSpec((tm, d), row),
            pl.BlockSpec((tm, e_w), row),
            pl.BlockSpec((1, d), const2),
        ),
        out_shape=(
            jax.ShapeDtypeStruct((1, 1), F32),
            jax.ShapeDtypeStruct((s, d), F32),
            jax.ShapeDtypeStruct((s, d), BF16),
            jax.ShapeDtypeStruct((s, e_w), BF16),
            jax.ShapeDtypeStruct((1, d), F32),
        ),
        scratch_shapes=[pltpu.VMEM((1, d), F32)],
        compiler_params=_params(1),
    )(y, w_out, x, target, g_post)


def _dx_call(dproj, w_in, x, dz, g_pre, tm, tk, rider=None):
    s, d = x.shape
    k_total = dproj.shape[1]
    nk = k_total // tk
    n_tiles = s // tm

    def body(dp_ref, w_ref, x_ref, dz_ref, g_ref, dx_ref, dg_ref, acc_ref):
        i, kk = pl.program_id(0), pl.program_id(1)
        part = lambda: _dot(dp_ref[...], w_ref[...], NT)

        @pl.when(kk == 0)
        def _():
            acc_ref[...] = part()

        @pl.when((kk > 0) & (kk < nk - 1))
        def _():
            acc_ref[...] += part()

        @pl.when((i == 0) & (kk == 0))
        def _():
            dg_ref[...] = jnp.zeros(dg_ref.shape, F32)

        @pl.when(kk == nk - 1)
        def _():
            dh = acc_ref[...] + part()
            xv = x_ref[...]
            r = lax.rsqrt(jnp.mean(xv * xv, axis=-1, keepdims=True) + EPS)
            xhat = xv * r
            dg_ref[...] += jnp.sum(dh * xhat, axis=0, keepdims=True)
            dxhat = dh * g_ref[...]
            dx_ref[...] = dz_ref[...] + r * (dxhat - xhat * jnp.mean(dxhat * xhat, axis=-1, keepdims=True))

    row = lambda i, kk: (i, 0)
    const2 = lambda i, kk: (0, 0)
    return _call(
        body, "dx", (n_tiles, nk),
        [
            pl.BlockSpec((tm, tk), lambda i, kk: (i, kk)),
            pl.BlockSpec((d, tk), lambda i, kk: (0, kk)),
            pl.BlockSpec((tm, d), row),
            pl.BlockSpec((tm, d), row),
            pl.BlockSpec((1, d), const2),
        ],
        [pl.BlockSpec((tm, d), row), pl.BlockSpec((1, d), const2)],
        [jax.ShapeDtypeStruct((s, d), F32), jax.ShapeDtypeStruct((1, d), F32)],
        [pltpu.VMEM((tm, d), F32)], [dproj, w_in, x, dz, g_pre], rider)


def _rows_tile(rows, cols, n_arrays, itemsize=4):
    budget = 24 * 1024 * 1024 // (2 * n_arrays * cols * itemsize)
    if rows <= budget:
        return rows
    best = None
    for cand in range(16, rows + 1, 16):
        if rows % cand == 0 and cand <= max(budget, 16):
            best = cand
    return best if best is not None else rows


def _elementwise(fn, inputs, out_dtypes, name):
    rows, cols = inputs[0].shape
    tr = _rows_tile(rows, cols, len(inputs) + len(out_dtypes))
    n_in = len(inputs)

    def body(*refs):
        outs = fn(*[r[...] for r in refs[:n_in]])
        for o_ref, o in zip(refs[n_in:], outs):
            o_ref[...] = o.astype(o_ref.dtype)

    spec = pl.BlockSpec((tr, cols), lambda i: (i, 0))
    return pl.pallas_call(
        body, name=name, grid=(rows // tr,),
        in_specs=[spec] * n_in, out_specs=tuple([spec] * len(out_dtypes)),
        out_shape=tuple(jax.ShapeDtypeStruct((rows, cols), dt) for dt in out_dtypes),
        compiler_params=_params(1),
    )(*inputs)


def _blockwise(fn, pos, inputs, in_specs, out_shape, out_spec, grid, name):
    n_in = len(inputs)

    def body(pos_ref, *refs):
        o_ref = refs[n_in]
        (out,) = fn(*[r[...].reshape(o_ref.shape) for r in refs[:n_in]])
        o_ref[...] = out.astype(o_ref.dtype)

    return pl.pallas_call(
        body, name=name,
        grid_spec=pltpu.PrefetchScalarGridSpec(num_scalar_prefetch=1, grid=grid, in_specs=in_specs,
                                               out_specs=out_spec),
        out_shape=out_shape,
        compiler_params=_params(len(grid)),
    )(pos, *inputs)


def _cast_copy(x):
    return (x,)


def _pair_sum(mine, theirs):
    return ((mine.astype(F32) + theirs.astype(F32)),)


def _four_sum(own, t0, t1, t2):
    return ((((own.astype(F32) + t0.astype(F32)) + t1.astype(F32)) + t2.astype(F32)),)


def _adamw(w, g, m, v):
    m = ADAM_B1 * m + (1.0 - ADAM_B1) * g
    v = ADAM_B2 * v + (1.0 - ADAM_B2) * jnp.square(g)
    m_hat = m / (1.0 - ADAM_B1 ** ADAM_STEP)
    v_hat = v / (1.0 - ADAM_B2 ** ADAM_STEP)
    delta = -ADAM_LR * (m_hat / (jnp.sqrt(v_hat) + ADAM_EPS) + ADAM_WD * w)
    return delta, m, v


def _place():
    x, y, c = lax.axis_index("x"), lax.axis_index("y"), lax.axis_index("c")
    chips = [(1 - x, y), (x, 1 - y), (1 - x, 1 - y)]
    return x, y, c, chips


def _remote(src, dst, send_sem, recv_sem, to):
    return pltpu.make_async_remote_copy(src_ref=src, dst_ref=dst, send_sem=send_sem, recv_sem=recv_sem,
                                        device_id=to, device_id_type=MESH)


def _split(ref, plan):
    views = [ref]
    for axis, parts in plan:
        size = ref.shape[axis] // parts
        assert size * parts == ref.shape[axis]
        views = [v.at[tuple(pl.ds(q * size, size) if i == axis else slice(None) for i in range(len(ref.shape)))]
                 for v in views for q in range(parts)]
    return views


def _remote_in_parts(src, dst, send_sem, recv_sem, to, plan):
    for s, d in zip(_split(src, plan), _split(dst, plan)):
        _remote(s, d, send_sem, recv_sem, to).start()
    return _remote(src, dst, send_sem, recv_sem, to)


def _hbm_call(body, name, inputs, out_shapes, scratch, aliases=None):
    return pl.pallas_call(
        body, name=name,
        in_specs=[ANY] * len(inputs), out_specs=tuple([ANY] * len(out_shapes)), out_shape=tuple(out_shapes),
        scratch_shapes=scratch, input_output_aliases=aliases or {},
        compiler_params=pltpu.CompilerParams(has_side_effects=True),
    )(*inputs)


def _gather_rider(fulls, kinds, peers=(0, 1, 2)):
    n = len(fulls)

    def full_half(a, ref, chip, cc):
        if a == 0:
            rows, cols = ref.shape[0] // 2, ref.shape[1] // 4
            return ref.at[pl.ds(cc * rows, rows), pl.ds(pl.multiple_of(chip * cols, 128), cols)]
        if a == 3:
            rows = ref.shape[1] // 8
            return ref.at[:, pl.ds(pl.multiple_of((2 * chip + cc) * rows, 16), rows), :]
        rows = ref.shape[0] // 8
        return ref.at[pl.ds(pl.multiple_of((2 * chip + cc) * rows, 16), rows), :]

    def run(in_refs, full_refs, send_sems, recv_sems, start):
        x, y, c, chips = _place()
        me = 2 * x + y
        sibling = (x, y, 1 - c)
        plans = [[(0, MAX_PARTS)], [(0, 2)], [(0, 2)], []]
        chips = [(p, chips[p]) for p in peers]
        across = lambda a, p, k: (3 * a + p) * MAX_PARTS + k
        onward = lambda a, p: 3 * n * MAX_PARTS + 3 * a + p

        def parts(a, chip_no, cc):
            return _split(full_half(kinds[a], full_refs[a], chip_no, cc), plans[kinds[a]])

        if start:
            for p, chip in chips:
                for a in range(n):
                    for k, mine in enumerate(parts(a, me, c)):
                        _remote(mine, mine, send_sems.at[across(a, p, k)], recv_sems.at[across(a, p, k)],
                                (*chip, c)).start()
            return
        for k in range(MAX_PARTS):
            for p, chip in chips:
                for a in range(n):
                    landed = parts(a, 2 * chip[0] + chip[1], c)
                    if k < len(landed):
                        _remote(landed[k], landed[k], send_sems.at[across(a, p, k)], recv_sems.at[across(a, p, k)],
                                (*chip, c)).wait_recv()
                        _remote(landed[k], landed[k], send_sems.at[onward(a, p)], recv_sems.at[onward(a, p)],
                                sibling).start()
        for p, chip in chips:
            them = 2 * chip[0] + chip[1]
            for a in range(n):
                passed = full_half(kinds[a], full_refs[a], them, 1 - c)
                _remote(passed, passed, send_sems.at[onward(a, p)], recv_sems.at[onward(a, p)], sibling).wait_recv()
                landed = full_half(kinds[a], full_refs[a], them, c)
                _remote(landed, landed, send_sems.at[onward(a, p)], recv_sems.at[onward(a, p)], sibling).wait_send()
                for k, mine in enumerate(parts(a, me, c)):
                    _remote(mine, mine, send_sems.at[across(a, p, k)], recv_sems.at[across(a, p, k)],
                            (*chip, c)).wait_send()

    return _Rider(fulls, [jax.ShapeDtypeStruct(f.shape, f.dtype) for f in fulls], 3 * n * (MAX_PARTS + 1), run,
                  aliases={a: a for a in range(n)})


def _exchange_halves(grads, name):
    n = len(grads)
    arrays = [g for g, _, _ in grads]
    out_shapes = [jax.ShapeDtypeStruct(tuple(1 if i == ax else dim for i, dim in enumerate(g.shape)), g.dtype)
                  for g, ax, _ in grads]

    def half(ref, ax, cc):
        idx = tuple(pl.ds(cc, 1) if i == ax else slice(None) for i in range(len(ref.shape)))
        return ref.at[idx]

    def body(*refs):
        in_refs, out_refs = refs[:n], refs[n:2 * n]
        send_sems, recv_sems = refs[2 * n:]
        x, y, c, _ = _place()
        sibling = (x, y, 1 - c)
        copies = [_remote_in_parts(half(in_refs[a], grads[a][1], 1 - c), out_refs[a], send_sems.at[a],
                                   recv_sems.at[a], sibling, grads[a][2]) for a in range(n)]
        for rem in copies:
            rem.wait()

    return _hbm_call(body, name, arrays, out_shapes,
                     [pltpu.SemaphoreType.DMA((n,)), pltpu.SemaphoreType.DMA((n,))])


def _exchange_rider(arrays, plans):
    n = len(arrays)

    def run(in_refs, out_refs, send_sems, recv_sems, start):
        x, y, c, _ = _place()
        sibling = (x, y, 1 - c)
        for a in range(n):
            sems = (send_sems.at[a], recv_sems.at[a])
            if start:
                _remote_in_parts(in_refs[a], out_refs[a], *sems, sibling, plans[a])
            else:
                _remote(in_refs[a], out_refs[a], *sems, sibling).wait()

    return _Rider(arrays, [jax.ShapeDtypeStruct(a.shape, a.dtype) for a in arrays], n, run)


def _scatter_rider(parts):
    n = len(parts)
    arrays = [p for p, _, _ in parts]

    def block_shape(p, ax):
        if ax == len(p.shape) - 1:
            return p.shape[:-1] + (p.shape[-1] // 4,)
        return tuple(1 if i == ax else dim for i, dim in enumerate(p.shape))

    out_shapes = [jax.ShapeDtypeStruct((3,) + block_shape(p, ax), p.dtype) for p, ax, _ in parts]

    def block(ref, ax, chip):
        rank = len(ref.shape)
        if ax == rank - 1:
            cols = ref.shape[-1] // 4
            last = pl.ds(pl.multiple_of(chip * cols, 128), cols)
            return ref.at[tuple([slice(None)] * (rank - 1) + [last])]
        return ref.at[tuple(pl.ds(chip, 1) if i == ax else slice(None) for i in range(rank))]

    def run(in_refs, out_refs, send_sems, recv_sems, start):
        x, y, c, chips = _place()
        for a in range(n):
            ax, plan = parts[a][1], parts[a][2]
            for p, chip in enumerate(chips):
                src, dst = block(in_refs[a], ax, 2 * chip[0] + chip[1]), out_refs[a].at[p]
                sems = (send_sems.at[3 * a + p], recv_sems.at[3 * a + p])
                if start:
                    _remote_in_parts(src, dst, *sems, (*chip, c), plan)
                else:
                    _remote(src, dst, *sems, (*chip, c)).wait()

    return _Rider(arrays, out_shapes, 3 * n, run)


def _join_halves(joined):
    n = len(joined)
    arrays = [j for j, _, _ in joined]

    def body(*refs):
        out_refs = refs[n:2 * n]
        send_sems, recv_sems = refs[2 * n:]
        x, y, c, _ = _place()
        sibling = (x, y, 1 - c)

        def half(a, cc):
            rank = len(out_refs[a].shape)
            return out_refs[a].at[tuple(pl.ds(cc, 1) if i == joined[a][1] else slice(None) for i in range(rank))]

        sends = [_remote_in_parts(half(a, c), half(a, c), send_sems.at[a], recv_sems.at[a], sibling, joined[a][2])
                 for a in range(n)]
        for a, rem in enumerate(sends):
            rem.wait_send()
            _remote(half(a, 1 - c), half(a, 1 - c), send_sems.at[a], recv_sems.at[a], sibling).wait_recv()

    return _hbm_call(body, "join_halves", arrays, [jax.ShapeDtypeStruct(j.shape, j.dtype) for j in arrays],
                     [pltpu.SemaphoreType.DMA((n,)), pltpu.SemaphoreType.DMA((n,))],
                     aliases={a: a for a in range(n)})


def _allreduce_small(packed):
    rows, lanes = packed.shape
    half = rows // 2

    def body(in_ref, out_ref, pair_ref, gath_ref, send_sems, recv_sems):
        x, y, c, chips = _place()
        me = 2 * x + y
        sibling = (x, y, 1 - c)
        mine = pl.ds(pl.multiple_of(c * half, 8), half)
        theirs = pl.ds(pl.multiple_of((1 - c) * half, 8), half)
        to_sib = _remote(in_ref.at[theirs], pair_ref, send_sems.at[0], recv_sems.at[0], sibling)
        to_sib.start()
        to_sib.wait()
        gath_ref[me] = in_ref[mine] + pair_ref[...]
        sends = [_remote(gath_ref.at[me], gath_ref.at[me], send_sems.at[1 + p], recv_sems.at[1 + p], (*chip, c))
                 for p, chip in enumerate(chips)]
        for cp in sends:
            cp.start()
        for p, chip in enumerate(chips):
            slot = gath_ref.at[2 * chip[0] + chip[1]]
            _remote(slot, slot, send_sems.at[1 + p], recv_sems.at[1 + p], (*chip, c)).wait_recv()
        for cp in sends:
            cp.wait_send()
        out_ref[mine] = ((gath_ref[0] + gath_ref[1]) + gath_ref[2]) + gath_ref[3]
        back = _remote(out_ref.at[mine], out_ref.at[mine], send_sems.at[4], recv_sems.at[4], sibling)
        back.start()
        back.wait_send()
        _remote(out_ref.at[theirs], out_ref.at[theirs], send_sems.at[4], recv_sems.at[4], sibling).wait_recv()

    vmem = pl.BlockSpec(memory_space=pltpu.VMEM)
    return pl.pallas_call(
        body, name="allreduce_small",
        in_specs=[vmem], out_specs=vmem, out_shape=jax.ShapeDtypeStruct((rows, lanes), F32),
        scratch_shapes=[pltpu.VMEM((half, lanes), F32), pltpu.VMEM((4, half, lanes), F32),
                        pltpu.SemaphoreType.DMA((5,)), pltpu.SemaphoreType.DMA((5,))],
        compiler_params=pltpu.CompilerParams(has_side_effects=True, vmem_limit_bytes=32 * 1024 * 1024),
    )(packed)


SMALL = ("norm_pre", "pool_scale", "sgu_ln_g", "sgu_ln_b", "sgu_w", "sgu_b", "mem_norm", "branch_norm", "norm_post")
LARGE = ("w_in", "pool_w", "w_kv", "w_out")
ORDER = ("norm_pre", "w_in", "pool_w", "pool_scale", "sgu_ln_g", "sgu_ln_b", "sgu_w", "sgu_b", "mem_norm", "w_kv",
         "branch_norm", "w_out", "norm_post")


def _pack(arrays, extra=()):
    rows = [a.reshape(-1, 128) for a in arrays] + list(extra)
    pad = -sum(r.shape[0] for r in rows) % 16
    return jnp.concatenate(rows + ([jnp.zeros((pad, 128), F32)] if pad else []), axis=0)


def _unpack(packed, like):
    out, row = [], 0
    for a in like:
        rows = a.size // 128
        out.append(packed[row:row + rows].reshape(a.shape))
        row += rows
    return out


def kernel(x, mem, norm_pre, w_in, pool_w, pool_scale, sgu_ln_g, sgu_ln_b, sgu_w, sgu_b, mem_norm, w_kv, branch_norm, w_out, norm_post, loss_target, m_norm_pre, m_w_in, m_pool_w, m_pool_scale, m_sgu_ln_g, m_sgu_ln_b, m_sgu_w, m_sgu_b, m_mem_norm, m_w_kv, m_branch_norm, m_w_out, m_norm_post, v_norm_pre, v_w_in, v_pool_w, v_pool_scale, v_sgu_ln_g, v_sgu_ln_b, v_sgu_w, v_sgu_b, v_mem_norm, v_w_kv, v_branch_norm, v_w_out, v_norm_post):
    weights = dict(norm_pre=norm_pre, w_in=w_in, pool_w=pool_w, pool_scale=pool_scale, sgu_ln_g=sgu_ln_g,
                   sgu_ln_b=sgu_ln_b, sgu_w=sgu_w, sgu_b=sgu_b, mem_norm=mem_norm, w_kv=w_kv, branch_norm=branch_norm,
                   w_out=w_out, norm_post=norm_post)
    mom1 = dict(norm_pre=m_norm_pre, w_in=m_w_in, pool_w=m_pool_w, pool_scale=m_pool_scale, sgu_ln_g=m_sgu_ln_g,
                sgu_ln_b=m_sgu_ln_b, sgu_w=m_sgu_w, sgu_b=m_sgu_b, mem_norm=m_mem_norm, w_kv=m_w_kv,
                branch_norm=m_branch_norm, w_out=m_w_out, norm_post=m_norm_post)
    mom2 = dict(norm_pre=v_norm_pre, w_in=v_w_in, pool_w=v_pool_w, pool_scale=v_pool_scale, sgu_ln_g=v_sgu_ln_g,
                sgu_ln_b=v_sgu_ln_b, sgu_w=v_sgu_w, sgu_b=v_sgu_b, mem_norm=v_mem_norm, w_kv=v_w_kv,
                branch_norm=v_branch_norm, w_out=v_w_out, norm_post=v_norm_post)

    s, d = x.shape[1], x.shape[2]
    x2, mem2, tgt2 = x[0], mem[0], loss_target[0]
    t_branch = min(256, s)
    tm = min(512, s)

    core = lax.axis_index("c")
    chip = 2 * lax.axis_index("x") + lax.axis_index("y")
    pos = jnp.stack([core, chip]).astype(jnp.int32)
    n_in, n_kv, n_out = 4 * w_in.shape[2], 4 * w_kv.shape[1], 4 * w_out.shape[1]
    wi_rows, kv_rows, wo_rows = d // 8, n_kv // 8, n_out // 8

    def placed(shard, full_shape, block, grid, in_map, out_map, name):
        return _blockwise(_cast_copy, pos, [shard], [pl.BlockSpec(block, in_map)],
                          jax.ShapeDtypeStruct(full_shape, BF16), pl.BlockSpec(block, out_map), grid, name)

    kv_cols, pw_rows = w_kv.shape[2], GROUP // 8
    wi_own = placed(w_in[0], (d, n_in), (wi_rows, n_in // 4), (8,), lambda i, p: (i, 0), lambda i, p: (i, p[1]),
                    "place_w_in")
    wkv_own = placed(w_kv[0], (n_kv, kv_cols), (kv_rows, kv_cols), (2,), lambda i, p: (i, 0),
                     lambda i, p: (2 * p[1] + i, 0), "place_w_kv")
    wo_own = placed(w_out[0], (n_out, d), (wo_rows, d), (2,), lambda i, p: (i, 0), lambda i, p: (2 * p[1] + i, 0),
                    "place_w_out")
    pw_own = placed(pool_w[0], (4, GROUP, GROUP), (4, GROUP // 4, GROUP), (1,), lambda i, p: (0, 0, 0),
                    lambda i, p: (0, p[1], 0), "place_pool_w")

    x_pos, y_pos = lax.axis_index("x"), lax.axis_index("y")
    chips = jnp.stack([chip, 2 * (1 - x_pos) + y_pos, 2 * x_pos + 1 - y_pos,
                       2 * (1 - x_pos) + 1 - y_pos]).astype(jnp.int32)
    mem_g = mem_norm.reshape(1, d)
    proj, h, h_t, wi_full = _proj_piece(chips, 0, 1, x2, norm_pre, None, None, n_in,
                                   _gather_rider([wi_own], [0], peers=(0, 1)), tm, "proj_own")
    proj, wi_full = _proj_piece(chips, 1, 2, h, None, None, proj, n_in,
                                _gather_rider([wi_full], [0], peers=(2,)), tm, "proj_neighbours")
    proj, wkv_full, pw_full = _proj_piece(chips, 3, 1, h, None, wi_full, proj, n_in,
                                          _gather_rider([wkv_own, pw_own], [1, 3]), tm, "proj_diagonal")
    k_m, v_m = _kv_fwd(mem2, mem_g, wkv_full)
    bias_full = jnp.repeat(sgu_b[0].T, CHUNK, axis=1)
    y, y_t, wo_full = _branches_fwd(proj, pw_full, pool_scale, sgu_ln_g, sgu_ln_b, sgu_w[0], bias_full, k_m, v_m,
                               branch_norm, t_branch, _gather_rider([wo_own], [2]))
    loss_local, dz, dout, dy, g_norm_post = _out_loss(y, wo_full, x2, tgt2, norm_post, min(256, s))

    def pair_sums(views):
        theirs = _exchange_halves([(v[0], v[1], v[2]) for v in views], "exchange_for_" + views[0][9])
        return [_blockwise(_pair_sum, pos, [v[0], th], [pl.BlockSpec(v[3], v[4][0]), pl.BlockSpec(v[3], v[4][1])],
                           jax.ShapeDtypeStruct(v[5], BF16), pl.BlockSpec(v[6], v[7]), v[8], v[9])
                for v, th in zip(views, theirs)]

    tk = min(1024, s)
    (g_wo,) = _grad_rows(y_t, dout, pos, lambda i, p: i, n_out, n_out // 2, 1024, tk, "grad_w_out")
    (ps_wo,) = pair_sums([
        (g_wo.reshape(4, 2, wo_rows, d), 1, [(0, 4), (2, 2)], (1, 1, wo_rows, d),
         (lambda i, p: (i, p[0], 0, 0), lambda i, p: (i, 0, 0, 0)), (4, wo_rows, d), (1, wo_rows, d),
         lambda i, p: (i, 0, 0), (4,), "pair_sum_w_out")])
    (dproj, g_pw, g_pool_scale, g_ln_g, g_ln_b, g_sgu_w, g_sgu_b_t, g_branch_norm, dk, dv, landed_wo) = _branches_bwd(
        proj, dy, pw_full, pool_scale, sgu_ln_g, sgu_ln_b, sgu_w[0], jnp.swapaxes(sgu_w[0], 1, 2), bias_full,
        k_m, v_m, branch_norm, t_branch, _scatter_rider([(ps_wo, 0, [(1, 2)])]))
    g_wkv, g_mem_norm = _kv_bwd(mem2, mem_g, wkv_full, dk, dv)
    ps_kv, ps_pw = pair_sums([
        (g_wkv.reshape(4, 2, kv_rows, kv_cols), 1, [(0, 4), (2, 2)], (1, 1, kv_rows, kv_cols),
         (lambda i, p: (i, p[0], 0, 0), lambda i, p: (i, 0, 0, 0)), (4, kv_rows, kv_cols), (1, kv_rows, kv_cols),
         lambda i, p: (i, 0, 0), (4,), "pair_sum_w_kv"),
        (g_pw.astype(BF16).reshape(4, 4, 2, pw_rows, GROUP), 2, [(0, 4)], (1, 4, 1, pw_rows, GROUP),
         (lambda i, p: (i, 0, p[0], 0, 0), lambda i, p: (i, 0, 0, 0, 0)), (4, 4, pw_rows, GROUP),
         (1, 4, pw_rows, GROUP), lambda i, p: (i, 0, 0, 0), (4,), "pair_sum_pool_w")])
    gwi_theirs, landed_kv, landed_pw = _grad_rows(
        h_t, dproj, pos, lambda i, p: 1 - p[0], d // 2, d // 2, n_in // 4, tk, "grad_w_in_sibling_half",
        _scatter_rider([(ps_kv, 0, [(1, 2)]), (ps_pw, 1, [])]))
    gwi_mine, gwi_from_sibling = _grad_rows(h_t, dproj, pos, lambda i, p: p[0], d // 2, d // 2, n_in // 4, tk,
                                            "grad_w_in_own_half", _exchange_rider([gwi_theirs], [[(0, 16)]]))
    ps_wi = _elementwise(_pair_sum, [gwi_mine, gwi_from_sibling], [BF16], "pair_sum_w_in")[0]
    grad_x, g_norm_pre, landed_wi = _dx_call(dproj, wi_full, x2, dz, norm_pre, tm, 1024,
                                             _scatter_rider([(ps_wi, 1, [(0, 4)])]))
    psum = [ps_wi, ps_kv, ps_wo, ps_pw]
    landed = [landed_wi, landed_kv, landed_wo, landed_pw]
    from_chip = lambda spec_shape, rank: [
        pl.BlockSpec(spec_shape, functools.partial(lambda i, p, q: (q, i) + (0,) * (rank - 2), q=q))
        for q in range(3)]
    joined = _join_halves([
        (_blockwise(_four_sum, pos, [psum[0]] + [landed[0]] * 3,
                    [pl.BlockSpec((256, n_in // 4), lambda i, p: (i, p[1]))] + from_chip((1, 256, n_in // 4), 3),
                    jax.ShapeDtypeStruct((2, d // 2, n_in // 4), F32),
                    pl.BlockSpec((1, 256, n_in // 4), lambda i, p: (p[0], i, 0)), (d // 2 // 256,), "chip_sum_w_in"),
         0, [(1, 8)]),
        (_blockwise(_four_sum, pos, [psum[1]] + [landed[1]] * 3,
                    [pl.BlockSpec((1, kv_rows, kv_cols), lambda i, p: (p[1], 0, 0))]
                    + from_chip((1, 1, kv_rows, kv_cols), 4),
                    jax.ShapeDtypeStruct((2, kv_rows, kv_cols), F32),
                    pl.BlockSpec((1, kv_rows, kv_cols), lambda i, p: (p[0], 0, 0)), (1,), "chip_sum_w_kv"),
         0, [(1, 2)]),
        (_blockwise(_four_sum, pos, [psum[2]] + [landed[2]] * 3,
                    [pl.BlockSpec((1, wo_rows, d), lambda i, p: (p[1], 0, 0))] + from_chip((1, 1, wo_rows, d), 4),
                    jax.ShapeDtypeStruct((2, wo_rows, d), F32),
                    pl.BlockSpec((1, wo_rows, d), lambda i, p: (p[0], 0, 0)), (1,), "chip_sum_w_out"),
         0, [(1, 2)]),
        (_blockwise(_four_sum, pos, [psum[3]] + [landed[3]] * 3,
                    [pl.BlockSpec((4, 1, pw_rows, GROUP), lambda i, p: (0, p[1], 0, 0))]
                    + from_chip((1, 4, 1, pw_rows, GROUP), 5),
                    jax.ShapeDtypeStruct((4, 2, pw_rows, GROUP), F32),
                    pl.BlockSpec((4, 1, pw_rows, GROUP), lambda i, p: (0, p[0], 0, 0)), (1,), "chip_sum_pool_w"),
         1, []),
    ])
    grads = {"w_in": joined[0].reshape(w_in.shape), "w_kv": joined[1].reshape(w_kv.shape),
             "w_out": joined[2].reshape(w_out.shape), "pool_w": joined[3].reshape(pool_w.shape)}

    small_local = dict(norm_pre=g_norm_pre, pool_scale=g_pool_scale, sgu_ln_g=g_ln_g, sgu_ln_b=g_ln_b,
                       sgu_w=g_sgu_w, sgu_b=g_sgu_b_t.T, mem_norm=g_mem_norm, branch_norm=g_branch_norm,
                       norm_post=g_norm_post)
    small_rows = sum(weights[n].size for n in SMALL) // 128
    small_sum = _allreduce_small(_pack([small_local[n] for n in SMALL], [jnp.pad(loss_local, ((0, 7), (0, 127)))]))
    for n, g in zip(SMALL, _unpack(small_sum, [weights[n] for n in SMALL])):
        grads[n] = g
    loss = small_sum[small_rows, 0]

    delta, new_m, new_v = {}, {}, {}
    packed = [small_sum if src is grads else _pack([src[n] for n in SMALL]) for src in (weights, grads, mom1, mom2)]
    outs = _elementwise(_adamw, packed, [F32, F32, F32], "adamw_small")
    for dst, o in zip((delta, new_m, new_v), outs):
        for n, a in zip(SMALL, _unpack(o, [weights[n] for n in SMALL])):
            dst[n] = a
    for n in LARGE:
        cols = weights[n].shape[-1]
        outs = _elementwise(_adamw, [src[n].reshape(-1, cols) for src in (weights, grads, mom1, mom2)],
                            [F32, F32, F32], "adamw_" + n)
        for dst, o in zip((delta, new_m, new_v), outs):
            dst[n] = o.reshape(weights[n].shape)

    return (loss, grad_x[None], *[grads[n] for n in ORDER], *[delta[n] for n in ORDER],
            *[new_m[n] for n in ORDER], *[new_v[n] for n in ORDER])
```

```python
import functools

import jax
import jax.numpy as jnp
from jax import lax
from jax.experimental import pallas as pl
from jax.experimental.pallas import tpu as pltpu

F32 = jnp.float32
BF16 = jnp.bfloat16
EPS = 1e-6
MESH = pl.DeviceIdType.MESH
ANY = pl.BlockSpec(memory_space=pl.ANY)

POOL_WINDOWS = (2, 4, 8, 16)
GROUP = 256
HALO = 16
CHUNK = 128
N_SGU_HEADS = 8
N_ATT_HEADS = 4
ATT_DIM = 256
WIDTH = 1024
ATT_SCALE = 1.0 / 16.0

ADAM_LR = 0.001
ADAM_B1 = 0.9
ADAM_B2 = 0.999
ADAM_EPS = 1e-08
ADAM_WD = 0.01
ADAM_STEP = 10

VMEM_LIMIT = 60 * 1024 * 1024
MAX_PARTS = 4


def _params(n_grid_axes, vmem=VMEM_LIMIT):
    return pltpu.CompilerParams(dimension_semantics=("arbitrary",) * n_grid_axes, vmem_limit_bytes=vmem)


def _dot(a, b, dims):
    return lax.dot_general(a, b, (dims, ((), ())), preferred_element_type=F32)


NN = ((1,), (0,))
NT = ((1,), (1,))
TN = ((0,), (0,))


class _Rider:
    def __init__(self, inputs, out_shapes, n_sems, run, aliases=None):
        self.inputs, self.out_shapes, self.n_sems, self.run = list(inputs), list(out_shapes), n_sems, run
        self.aliases = aliases or {}


def _call(body, name, grid, in_specs, out_specs, out_shape, scratch_shapes, inputs, rider=None, prefetch=None,
          aliases=None, rider_refs=False):
    n_in, n_out, n_scr = len(in_specs), len(out_specs), len(scratch_shapes)
    r_in = len(rider.inputs) if rider else 0
    r_out = len(rider.out_shapes) if rider else 0
    n_pre = 0 if prefetch is None else 1

    def whole_body(*refs):
        pre, refs = refs[:n_pre], refs[n_pre:]
        ins, rider_ins = refs[:n_in], refs[n_in:n_in + r_in]
        refs = refs[n_in + r_in:]
        outs, rider_outs = refs[:n_out], refs[n_out:n_out + r_out]
        refs = refs[n_out + r_out:]
        scratch, sems = refs[:n_scr], refs[n_scr:]
        extra = {"rider_outs": rider_outs} if rider_refs else {}
        if rider is None:
            body(*pre, *ins, *outs, *scratch, **extra)
            return
        ids = [pl.program_id(ax) for ax in range(len(grid))]
        first = functools.reduce(lambda p, q: p & q, [i == 0 for i in ids])
        last = functools.reduce(lambda p, q: p & q, [i == g - 1 for i, g in zip(ids, grid)])

        @pl.when(first)
        def _():
            rider.run(rider_ins, rider_outs, *sems, True)

        body(*pre, *ins, *outs, *scratch, **extra)

        @pl.when(last)
        def _():
            rider.run(rider_ins, rider_outs, *sems, False)

    io_aliases = {n_pre + i: o for i, o in (aliases or {}).items()}
    scratch_all = list(scratch_shapes)
    if rider:
        io_aliases.update({n_pre + n_in + i: n_out + o for i, o in rider.aliases.items()})
        scratch_all += [pltpu.SemaphoreType.DMA((rider.n_sems,)), pltpu.SemaphoreType.DMA((rider.n_sems,))]
    specs = dict(grid=grid, in_specs=list(in_specs) + [ANY] * r_in, out_specs=tuple(out_specs) + (ANY,) * r_out,
                 scratch_shapes=scratch_all)
    if n_pre:
        specs = dict(grid_spec=pltpu.PrefetchScalarGridSpec(num_scalar_prefetch=1, **specs))
    outs = pl.pallas_call(
        whole_body, name=name, **specs,
        out_shape=tuple(out_shape) + tuple(rider.out_shapes if rider else ()),
        input_output_aliases=io_aliases, compiler_params=_params(len(grid)),
    )(*([prefetch] if n_pre else []), *inputs, *(rider.inputs if rider else []))
    return tuple(outs)


def _grad_rows(a_t, b, pos, row_of, m, tm, tn, tk, name, rider=None):
    k, n = a_t.shape[1], b.shape[1]
    nk = k // tk
    out_dtype, dims, a = BF16, NN, a_t
    a_spec = pl.BlockSpec((tm, tk), lambda i, j, kk, p: (row_of(i, p), kk))
    b_spec = pl.BlockSpec((tk, tn), lambda i, j, kk, p: (kk, j))

    def body(pos_ref, a_ref, b_ref, o_ref, *acc):
        part = lambda: _dot(a_ref[...], b_ref[...], dims)
        if nk == 1:
            o_ref[...] = part().astype(out_dtype)
            return
        (acc_ref,) = acc
        kk = pl.program_id(2)

        @pl.when(kk == 0)
        def _():
            acc_ref[...] = part()

        @pl.when((kk > 0) & (kk < nk - 1))
        def _():
            acc_ref[...] += part()

        @pl.when(kk == nk - 1)
        def _():
            o_ref[...] = (acc_ref[...] + part()).astype(out_dtype)

    return _call(body, name, (m // tm, n // tn, nk), [a_spec, b_spec],
                 [pl.BlockSpec((tm, tn), lambda i, j, kk, p: (i, j))], [jax.ShapeDtypeStruct((m, n), out_dtype)],
                 [pltpu.VMEM((tm, tn), F32)] if nk > 1 else [], [a, b], rider, prefetch=pos)


def _proj_piece(chips, first, n_shards, src, g_pre, w_in, proj_in, n_cols, rider, tm, name):
    s, d = src.shape
    cols = n_cols // 4
    fused = g_pre is not None

    def body(chips_ref, *refs, rider_outs=()):
        refs = list(refs)
        src_ref = refs.pop(0)
        g_ref = refs.pop(0) if fused else None
        w_ref = refs.pop(0) if w_in is not None else rider_outs[0]
        if proj_in is not None:
            refs.pop(0)
        proj_ref = refs.pop(0)
        h_ref, ht_ref = (refs.pop(0), refs.pop(0)) if fused else (None, None)
        wbuf, sem = refs
        q, i = pl.program_id(0), pl.program_id(1)

        @pl.when(i == 0)
        def _():
            at = pl.multiple_of(chips_ref[first + q] * cols, 128)
            cp = pltpu.make_async_copy(w_ref.at[:, pl.ds(at, cols)], wbuf, sem)
            cp.start()
            cp.wait()

        if fused:
            xv = src_ref[...]
            r = lax.rsqrt(jnp.mean(xv * xv, axis=-1, keepdims=True) + EPS)
            h = (xv * r * g_ref[...]).astype(BF16)
            h_ref[...] = h
            ht_ref[...] = h.T
        else:
            h = src_ref[...]
        proj_ref[...] = _dot(h, wbuf[...], NN)

    row = lambda q, i, ch: (i, 0)
    inputs, in_specs = [src], [pl.BlockSpec((tm, d), row)]
    if fused:
        inputs.append(g_pre)
        in_specs.append(pl.BlockSpec((1, d), lambda q, i, ch: (0, 0)))
    if w_in is not None:
        inputs.append(w_in)
        in_specs.append(ANY)
    aliases = {}
    if proj_in is not None:
        aliases[len(inputs)] = 0
        inputs.append(proj_in)
        in_specs.append(ANY)
    out_specs = [pl.BlockSpec((tm, cols), lambda q, i, ch: (i, ch[first + q]))]
    out_shape = [jax.ShapeDtypeStruct((s, n_cols), F32)]
    if fused:
        assert n_shards == 1
        out_specs += [pl.BlockSpec((tm, d), row), pl.BlockSpec((d, tm), lambda q, i, ch: (0, i))]
        out_shape += [jax.ShapeDtypeStruct((s, d), BF16), jax.ShapeDtypeStruct((d, s), BF16)]
    return _call(body, name, (n_shards, s // tm), in_specs, out_specs, out_shape,
                 [pltpu.VMEM((d, cols), BF16), pltpu.SemaphoreType.DMA(())], inputs, rider, prefetch=chips,
                 aliases=aliases, rider_refs=True)


def _kv_fwd(mem, g, w_kv):
    m, d = mem.shape

    def body(mem_ref, g_ref, w_ref, k_ref, v_ref):
        mv = mem_ref[...]
        r = lax.rsqrt(jnp.mean(mv * mv, axis=-1, keepdims=True) + EPS)
        mem_n = (mv * r * g_ref[...]).astype(BF16)
        kv = _dot(mem_n, w_ref[...], NN)
        k_ref[...] = kv[:, :WIDTH].astype(BF16)
        v_ref[...] = kv[:, WIDTH:].astype(BF16)

    return pl.pallas_call(
        body, name="kv_fwd",
        out_shape=(jax.ShapeDtypeStruct((m, WIDTH), BF16), jax.ShapeDtypeStruct((m, WIDTH), BF16)),
        compiler_params=_params(0),
    )(mem, g, w_kv)


def _kv_bwd(mem, g, w_kv, dk, dv):
    m, d = mem.shape
    n = w_kv.shape[1]
    col = 512

    def body(mem_ref, g_ref, w_ref, dk_ref, dv_ref, dw_ref, dg_ref):
        mv = mem_ref[...]
        r = lax.rsqrt(jnp.mean(mv * mv, axis=-1, keepdims=True) + EPS)
        mem_hat = mv * r
        mem_n = (mem_hat * g_ref[...]).astype(BF16)
        dkv = jnp.concatenate([dk_ref[...], dv_ref[...]], axis=1).astype(BF16)
        for j in range(n // col):
            dw_ref[:, j * col:(j + 1) * col] = _dot(mem_n, dkv[:, j * col:(j + 1) * col], TN).astype(BF16)
        dmem_n = _dot(dkv, w_ref[...], NT)
        dg_ref[...] = jnp.sum(dmem_n * mem_hat, axis=0, keepdims=True)

    return pl.pallas_call(
        body, name="kv_bwd",
        out_shape=(jax.ShapeDtypeStruct((d, n), BF16), jax.ShapeDtypeStruct((1, d), F32)),
        compiler_params=_params(0),
    )(mem, g, w_kv, dk, dv)


def _sigmoid(x):
    return 1.0 / (1.0 + jnp.exp(-x))


def _inv_counts(t0, t):
    pos = (t0 + lax.broadcasted_iota(jnp.int32, (t, 1), 0) + 1).astype(F32)
    return [1.0 / jnp.minimum(pos, float(w)) for w in POOL_WINDOWS]


def _window_sums(ext, t, backward):
    n = t + HALO
    parts = []
    for gi, w in enumerate(POOL_WINDOWS):
        s = ext[:, gi * GROUP:(gi + 1) * GROUP]
        k = 1
        while k < w:
            s = s + pltpu.roll(s, (n - k) if backward else k, axis=0)
            k *= 2
        parts.append(s[:t] if backward else s[HALO:])
    return parts


def _pool_fwd(xa, halo, inv, pool_w):
    t = xa.shape[0]
    sums = _window_sums(jnp.concatenate([halo, xa], axis=0), t, backward=False)
    d = jnp.concatenate([sums[gi] * inv[gi] - xa[:, gi * GROUP:(gi + 1) * GROUP] for gi in range(4)], axis=1)
    d = d.astype(BF16)
    y = jnp.concatenate([_dot(d[:, gi * GROUP:(gi + 1) * GROUP], pool_w[gi], NN) for gi in range(4)], axis=1)
    return d, y


def _layernorm_fwd(v):
    mu = jnp.mean(v, axis=-1, keepdims=True)
    xc = v - mu
    rstd = lax.rsqrt(jnp.mean(xc * xc, axis=-1, keepdims=True) + EPS)
    return xc * rstd, rstd


def _tril_mask(transposed):
    r = lax.broadcasted_iota(jnp.int32, (CHUNK, CHUNK), 0)
    c = lax.broadcasted_iota(jnp.int32, (CHUNK, CHUNK), 1)
    return (r <= c) if transposed else (r >= c)


def _sgu_mix(w_ref, vals, transposed):
    t = vals.shape[0]
    mask = _tril_mask(transposed)
    ws = [jnp.where(mask, w_ref[h], 0.0).astype(BF16) for h in range(N_SGU_HEADS)]
    rows = []
    for ci in range(t // CHUNK):
        blk = vals[ci * CHUNK:(ci + 1) * CHUNK]
        rows.append(jnp.concatenate(
            [_dot(ws[h], blk[:, h * CHUNK:(h + 1) * CHUNK], NN) for h in range(N_SGU_HEADS)], axis=1))
    return jnp.concatenate(rows, axis=0)


def _attn_fwd(q, k, v):
    ps, os_ = [], []
    for h in range(N_ATT_HEADS):
        sl = slice(h * ATT_DIM, (h + 1) * ATT_DIM)
        s = _dot(q[:, sl], k[:, sl], NT) * ATT_SCALE
        s = s - jnp.max(s, axis=-1, keepdims=True)
        e = jnp.exp(s)
        p = e * (1.0 / jnp.sum(e, axis=-1, keepdims=True))
        ps.append(p)
        os_.append(_dot(p.astype(BF16), v[:, sl], NN))
    return ps, jnp.concatenate(os_, axis=1)


def _rms_branch(y_pre):
    r = lax.rsqrt(jnp.mean(y_pre * y_pre, axis=-1, keepdims=True) + EPS)
    return y_pre * r, r


def _branch_specs(t, n_tiles, order):
    width_in = 7 * WIDTH
    tile = lambda i: order(i)
    per_halo = t // HALO
    const2 = lambda i: (0, 0)
    const3 = lambda i: (0, 0, 0)
    return [
        pl.BlockSpec((t, width_in), lambda i: (tile(i), 0)),
        pl.BlockSpec((HALO, WIDTH), lambda i: (jnp.maximum(tile(i) * per_halo - 1, 0), 0)),
        pl.BlockSpec((4, GROUP, GROUP), const3),
        pl.BlockSpec((1, WIDTH), const2),
        pl.BlockSpec((1, WIDTH), const2),
        pl.BlockSpec((1, WIDTH), const2),
        pl.BlockSpec((N_SGU_HEADS, CHUNK, CHUNK), const3),
        pl.BlockSpec((CHUNK, WIDTH), const2),
        pl.BlockSpec((MEM_ROWS, WIDTH), const2),
        pl.BlockSpec((MEM_ROWS, WIDTH), const2),
        pl.BlockSpec((1, 3 * WIDTH), const2),
    ]


MEM_ROWS = 256


def _branches_fwd(proj, pool_w, pool_scale, ln_g, ln_b, sgu_w, bias_full, k, v, branch_norm, t, rider=None):
    s = proj.shape[0]
    n_tiles = s // t

    def body(proj_ref, halo_ref, pw_ref, ps_ref, lg_ref, lb_ref, sw_ref, sb_ref, k_ref, v_ref, bn_ref, y_ref, yt_ref):
        i = pl.program_id(0)
        col = lambda j: proj_ref[:, j * WIDTH:(j + 1) * WIDTH]

        def put(branch, y_pre):
            sl = slice(branch * WIDTH, (branch + 1) * WIDTH)
            val = (_rms_branch(y_pre)[0] * bn[:, sl]).astype(BF16)
            y_ref[:, sl] = val
            yt_ref[sl, :] = val.T

        bn = bn_ref[...]
        halo = jnp.where(i > 0, halo_ref[...], 0.0)
        _, y_pool = _pool_fwd(col(0), halo, _inv_counts(i * t, t), pw_ref[...])
        ga = col(1)
        ya = y_pool * ps_ref[...] * (ga * _sigmoid(ga))
        put(0, ya)
        vhat, _ = _layernorm_fwd(col(3))
        vn = (vhat * lg_ref[...] + lb_ref[...]).astype(BF16)
        z = _sgu_mix(sw_ref, vn, transposed=False) + jnp.tile(sb_ref[...], (t // CHUNK, 1))
        gb = col(4)
        yb = col(2) * z * (gb * _sigmoid(gb))
        put(1, yb)
        _, o = _attn_fwd(col(5).astype(BF16), k_ref[...], v_ref[...])
        gc = col(6)
        yc = o * (gc * _sigmoid(gc))
        put(2, yc)

    return _call(body, "branches_fwd", (n_tiles,), _branch_specs(t, n_tiles, lambda i: i),
                 [pl.BlockSpec((t, 3 * WIDTH), lambda i: (i, 0)), pl.BlockSpec((3 * WIDTH, t), lambda i: (0, i))],
                 [jax.ShapeDtypeStruct((s, 3 * WIDTH), BF16), jax.ShapeDtypeStruct((3 * WIDTH, s), BF16)], [],
                 [proj, proj, pool_w, pool_scale, ln_g, ln_b, sgu_w, bias_full, k, v, branch_norm], rider)


def _branches_bwd(proj, dy, pool_w, pool_scale, ln_g, ln_b, sgu_w, sgu_wt, bias_full, k, v, branch_norm, t, rider=None):
    s = proj.shape[0]
    n_tiles = s // t
    n_chunks = t // CHUNK
    order = lambda i: n_tiles - 1 - i

    def body(proj_ref, halo_ref, pw_ref, ps_ref, lg_ref, lb_ref, sw_ref, sb_ref, k_ref, v_ref, bn_ref,
             swt_ref, dy_ref,
             dproj_ref, dpw_ref, dps_ref, dlg_ref, dlb_ref, dsw_ref, dsb_ref, dbn_ref, dk_ref, dv_ref,
             carry_ref, dbias_ref):
        step = pl.program_id(0)
        i = order(step)

        @pl.when(step == 0)
        def _():
            for ref in (dpw_ref, dps_ref, dlg_ref, dlb_ref, dsw_ref, dbn_ref, dk_ref, dv_ref, carry_ref, dbias_ref):
                ref[...] = jnp.zeros(ref.shape, ref.dtype)

        col = lambda j: proj_ref[:, j * WIDTH:(j + 1) * WIDTH]
        bn = bn_ref[...]

        def norm_bwd(y_pre, sl):
            yhat, r = _rms_branch(y_pre)
            dyv = dy_ref[:, sl].astype(F32)
            dbn_ref[:, sl] += jnp.sum(dyv * yhat, axis=0, keepdims=True)
            dyhat = dyv * bn[:, sl]
            return r * (dyhat - yhat * jnp.mean(dyhat * yhat, axis=-1, keepdims=True))

        def gate(gv):
            sg = _sigmoid(gv)
            return gv * sg, sg * (1.0 + gv * (1.0 - sg))

        inv = _inv_counts(i * t, t)
        halo = jnp.where(i > 0, halo_ref[...], 0.0)
        pw = pw_ref[...]
        d, y_pool = _pool_fwd(col(0), halo, inv, pw)
        scale = ps_ref[...]
        silu_a, dsilu_a = gate(col(1))
        pa = y_pool * scale
        dya = norm_bwd(pa * silu_a, slice(0, WIDTH))
        dproj_ref[:, WIDTH:2 * WIDTH] = (dya * pa * dsilu_a).astype(BF16)
        dpa = dya * silu_a
        dps_ref[...] += jnp.sum(dpa * y_pool, axis=0, keepdims=True)
        dy_pool = (dpa * scale).astype(BF16)
        dd_parts, ddc_parts = [], []
        for gi in range(4):
            sl = slice(gi * GROUP, (gi + 1) * GROUP)
            dpw_ref[gi] += _dot(d[:, sl], dy_pool[:, sl], TN)
            dd = _dot(dy_pool[:, sl], pw[gi], NT)
            dd_parts.append(dd)
            ddc_parts.append(dd * inv[gi])
        ddc = jnp.concatenate(ddc_parts, axis=1)
        sums = _window_sums(jnp.concatenate([ddc, carry_ref[...]], axis=0), t, backward=True)
        carry_ref[...] = ddc[:HALO]
        dproj_ref[:, 0:WIDTH] = jnp.concatenate([sums[gi] - dd_parts[gi] for gi in range(4)], axis=1).astype(BF16)

        vhat, rstd = _layernorm_fwd(col(3))
        lg = lg_ref[...]
        vn = (vhat * lg + lb_ref[...]).astype(BF16)
        z = _sgu_mix(sw_ref, vn, transposed=False) + jnp.tile(sb_ref[...], (n_chunks, 1))
        u = col(2)
        silu_b, dsilu_b = gate(col(4))
        uz = u * z
        dyb = norm_bwd(uz * silu_b, slice(WIDTH, 2 * WIDTH))
        dproj_ref[:, 4 * WIDTH:5 * WIDTH] = (dyb * uz * dsilu_b).astype(BF16)
        duz = dyb * silu_b
        dproj_ref[:, 2 * WIDTH:3 * WIDTH] = (duz * z).astype(BF16)
        dz = duz * u
        dz_b = dz.astype(BF16)
        for ci in range(n_chunks):
            rows = slice(ci * CHUNK, (ci + 1) * CHUNK)
            dbias_ref[...] += dz[rows]
            for h in range(N_SGU_HEADS):
                sl = slice(h * CHUNK, (h + 1) * CHUNK)
                dsw_ref[h] += _dot(dz_b[rows, sl], vn[rows, sl], NT)
        dvn = _sgu_mix(swt_ref, dz_b, transposed=True)
        dlg_ref[...] += jnp.sum(dvn * vhat, axis=0, keepdims=True)
        dlb_ref[...] += jnp.sum(dvn, axis=0, keepdims=True)
        dvhat = dvn * lg
        dvb = rstd * (dvhat - jnp.mean(dvhat, axis=-1, keepdims=True)
                      - vhat * jnp.mean(dvhat * vhat, axis=-1, keepdims=True))
        dproj_ref[:, 3 * WIDTH:4 * WIDTH] = dvb.astype(BF16)

        q = col(5).astype(BF16)
        kv_k, kv_v = k_ref[...], v_ref[...]
        ps, o = _attn_fwd(q, kv_k, kv_v)
        silu_c, dsilu_c = gate(col(6))
        dyc = norm_bwd(o * silu_c, slice(2 * WIDTH, 3 * WIDTH))
        dproj_ref[:, 6 * WIDTH:7 * WIDTH] = (dyc * o * dsilu_c).astype(BF16)
        do = (dyc * silu_c).astype(BF16)
        dq_parts = []
        for h in range(N_ATT_HEADS):
            sl = slice(h * ATT_DIM, (h + 1) * ATT_DIM)
            p = ps[h]
            dp = _dot(do[:, sl], kv_v[:, sl], NT)
            ds = (p * (dp - jnp.sum(p * dp, axis=-1, keepdims=True)) * ATT_SCALE).astype(BF16)
            dq_parts.append(_dot(ds, kv_k[:, sl], NN))
            dk_ref[:, sl] += _dot(ds, q[:, sl], TN)
            dv_ref[:, sl] += _dot(p.astype(BF16), do[:, sl], TN)
        dproj_ref[:, 5 * WIDTH:6 * WIDTH] = jnp.concatenate(dq_parts, axis=1).astype(BF16)

        @pl.when(step == n_tiles - 1)
        def _():
            keep = _tril_mask(transposed=False)
            for h in range(N_SGU_HEADS):
                dsw_ref[h] = jnp.where(keep, dsw_ref[h], 0.0)
            dsb_ref[...] = jnp.concatenate(
                [jnp.sum(dbias_ref[:, h * CHUNK:(h + 1) * CHUNK], axis=1, keepdims=True)
                 for h in range(N_SGU_HEADS)], axis=1)

    const2 = lambda i: (0, 0)
    const3 = lambda i: (0, 0, 0)
    out_shapes = (
        jax.ShapeDtypeStruct((s, 7 * WIDTH), BF16),
        jax.ShapeDtypeStruct((4, GROUP, GROUP), F32),
        jax.ShapeDtypeStruct((1, WIDTH), F32),
        jax.ShapeDtypeStruct((1, WIDTH), F32),
        jax.ShapeDtypeStruct((1, WIDTH), F32),
        jax.ShapeDtypeStruct((N_SGU_HEADS, CHUNK, CHUNK), F32),
        jax.ShapeDtypeStruct((CHUNK, N_SGU_HEADS), F32),
        jax.ShapeDtypeStruct((1, 3 * WIDTH), F32),
        jax.ShapeDtypeStruct((MEM_ROWS, WIDTH), F32),
        jax.ShapeDtypeStruct((MEM_ROWS, WIDTH), F32),
    )
    out_specs = (
        pl.BlockSpec((t, 7 * WIDTH), lambda i: (order(i), 0)),
        pl.BlockSpec((4, GROUP, GROUP), const3),
        pl.BlockSpec((1, WIDTH), const2),
        pl.BlockSpec((1, WIDTH), const2),
        pl.BlockSpec((1, WIDTH), const2),
        pl.BlockSpec((N_SGU_HEADS, CHUNK, CHUNK), const3),
        pl.BlockSpec((CHUNK, N_SGU_HEADS), const2),
        pl.BlockSpec((1, 3 * WIDTH), const2),
        pl.BlockSpec((MEM_ROWS, WIDTH), const2),
        pl.BlockSpec((MEM_ROWS, WIDTH), const2),
    )
    in_specs = _branch_specs(t, n_tiles, order) + [
        pl.BlockSpec((N_SGU_HEADS, CHUNK, CHUNK), const3),
        pl.BlockSpec((t, 3 * WIDTH), lambda i: (order(i), 0)),
    ]
    return _call(body, "branches_bwd", (n_tiles,), in_specs, out_specs, out_shapes,
                 [pltpu.VMEM((HALO, WIDTH), F32), pltpu.VMEM((CHUNK, WIDTH), F32)],
                 [proj, proj, pool_w, pool_scale, ln_g, ln_b, sgu_w, bias_full, k, v, branch_norm, sgu_wt, dy], rider)


def _out_loss(y, w_out, x, target, g_post, tm):
    s, d = x.shape
    e_w = y.shape[1]
    n_tiles = s // tm

    def body(y_ref, w_ref, x_ref, t_ref, g_ref, loss_ref, dz_ref, dout_ref, dy_ref, dg_ref, sq_ref):
        i = pl.program_id(0)

        @pl.when(i == 0)
        def _():
            sq_ref[...] = jnp.zeros(sq_ref.shape, F32)
            dg_ref[...] = jnp.zeros(dg_ref.shape, F32)

        w = w_ref[...]
        out = _dot(y_ref[...], w, NN)
        r = lax.rsqrt(jnp.mean(out * out, axis=-1, keepdims=True) + EPS)
        outn = out * r
        g = g_ref[...]
        err = (x_ref[...] + outn * g) - t_ref[...]
        sq_ref[...] += jnp.sum(err * err, axis=0, keepdims=True)
        dz = err * (1.0 / d)
        dz_ref[...] = dz
        dg_ref[...] += jnp.sum(dz * outn, axis=0, keepdims=True)
        doutn = dz * g
        dout = (r * (doutn - outn * jnp.mean(doutn * outn, axis=-1, keepdims=True))).astype(BF16)
        dout_ref[...] = dout
        dy_ref[...] = _dot(dout, w, NT).astype(BF16)

        @pl.when(i == n_tiles - 1)
        def _():
            loss_ref[...] = 0.5 * jnp.sum(sq_ref[...], axis=1, keepdims=True) * (1.0 / d)

    row = lambda i: (i, 0)
    const2 = lambda i: (0, 0)
    return pl.pallas_call(
        body, name="out_loss", grid=(n_tiles,),
        in_specs=[
            pl.BlockSpec((tm, e_w), row),
            pl.BlockSpec((e_w, d), const2, pipeline_mode=pl.Buffered(1)),
            pl.BlockSpec((tm, d), row),
            pl.BlockSpec((tm, d), row),
            pl.BlockSpec((1, d), const2),
        ],
        out_specs=(
            pl.BlockSpec((1, 1), const2),
            pl.BlockSpec((tm, d), row),
            pl.BlockSpec((tm, d), row),
            pl.BlockSpec((tm, e_w), row),
            pl.BlockSpec((1, d), const2),
        ),
        out_shape=(
            jax.ShapeDtypeStruct((1, 1), F32),
            jax.ShapeDtypeStruct((s, d), F32),
            jax.ShapeDtypeStruct((s, d), BF16),
            jax.ShapeDtypeStruct((s, e_w), BF16),
            jax.ShapeDtypeStruct((1, d), F32),
        ),
        scratch_shapes=[pltpu.VMEM((1, d), F32)],
        compiler_params=_params(1),
    )(y, w_out, x, target, g_post)


def _dx_call(dproj, w_in, x, dz, g_pre, tm, tk, rider=None):
    s, d = x.shape
    k_total = dproj.shape[1]
    nk = k_total // tk
    n_tiles = s // tm

    def body(dp_ref, w_ref, x_ref, dz_ref, g_ref, dx_ref, dg_ref, acc_ref):
        i, kk = pl.program_id(0), pl.program_id(1)
        part = lambda: _dot(dp_ref[...], w_ref[...], NT)

        @pl.when(kk == 0)
        def _():
            acc_ref[...] = part()

        @pl.when((kk > 0) & (kk < nk - 1))
        def _():
            acc_ref[...] += part()

        @pl.when((i == 0) & (kk == 0))
        def _():
            dg_ref[...] = jnp.zeros(dg_ref.shape, F32)

        @pl.when(kk == nk - 1)
        def _():
            dh = acc_ref[...] + part()
            xv = x_ref[...]
            r = lax.rsqrt(jnp.mean(xv * xv, axis=-1, keepdims=True) + EPS)
            xhat = xv * r
            dg_ref[...] += jnp.sum(dh * xhat, axis=0, keepdims=True)
            dxhat = dh * g_ref[...]
            dx_ref[...] = dz_ref[...] + r * (dxhat - xhat * jnp.mean(dxhat * xhat, axis=-1, keepdims=True))

    row = lambda i, kk: (i, 0)
    const2 = lambda i, kk: (0, 0)
    return _call(
        body, "dx", (n_tiles, nk),
        [
            pl.BlockSpec((tm, tk), lambda i, kk: (i, kk)),
            pl.BlockSpec((d, tk), lambda i, kk: (0, kk)),
            pl.BlockSpec((tm, d), row),
            pl.BlockSpec((tm, d), row),
            pl.BlockSpec((1, d), const2),
        ],
        [pl.BlockSpec((tm, d), row), pl.BlockSpec((1, d), const2)],
        [jax.ShapeDtypeStruct((s, d), F32), jax.ShapeDtypeStruct((1, d), F32)],
        [pltpu.VMEM((tm, d), F32)], [dproj, w_in, x, dz, g_pre], rider)


def _rows_tile(rows, cols, n_arrays, itemsize=4):
    budget = 24 * 1024 * 1024 // (2 * n_arrays * cols * itemsize)
    if rows <= budget:
        return rows
    best = None
    for cand in range(16, rows + 1, 16):
        if rows % cand == 0 and cand <= max(budget, 16):
            best = cand
    return best if best is not None else rows


def _elementwise(fn, inputs, out_dtypes, name):
    rows, cols = inputs[0].shape
    tr = _rows_tile(rows, cols, len(inputs) + len(out_dtypes))
    n_in = len(inputs)

    def body(*refs):
        outs = fn(*[r[...] for r in refs[:n_in]])
        for o_ref, o in zip(refs[n_in:], outs):
            o_ref[...] = o.astype(o_ref.dtype)

    spec = pl.BlockSpec((tr, cols), lambda i: (i, 0))
    return pl.pallas_call(
        body, name=name, grid=(rows // tr,),
        in_specs=[spec] * n_in, out_specs=tuple([spec] * len(out_dtypes)),
        out_shape=tuple(jax.ShapeDtypeStruct((rows, cols), dt) for dt in out_dtypes),
        compiler_params=_params(1),
    )(*inputs)


def _blockwise(fn, pos, inputs, in_specs, out_shape, out_spec, grid, name):
    n_in = len(inputs)

    def body(pos_ref, *refs):
        o_ref = refs[n_in]
        (out,) = fn(*[r[...].reshape(o_ref.shape) for r in refs[:n_in]])
        o_ref[...] = out.astype(o_ref.dtype)

    return pl.pallas_call(
        body, name=name,
        grid_spec=pltpu.PrefetchScalarGridSpec(num_scalar_prefetch=1, grid=grid, in_specs=in_specs,
                                               out_specs=out_spec),
        out_shape=out_shape,
        compiler_params=_params(len(grid)),
    )(pos, *inputs)


def _cast_copy(x):
    return (x,)


def _pair_sum(mine, theirs):
    return ((mine.astype(F32) + theirs.astype(F32)),)


def _four_sum(own, t0, t1, t2):
    return ((((own.astype(F32) + t0.astype(F32)) + t1.astype(F32)) + t2.astype(F32)),)


def _adamw(w, g, m, v):
    m = ADAM_B1 * m + (1.0 - ADAM_B1) * g
    v = ADAM_B2 * v + (1.0 - ADAM_B2) * jnp.square(g)
    m_hat = m / (1.0 - ADAM_B1 ** ADAM_STEP)
    v_hat = v / (1.0 - ADAM_B2 ** ADAM_STEP)
    delta = -ADAM_LR * (m_hat / (jnp.sqrt(v_hat) + ADAM_EPS) + ADAM_WD * w)
    return delta, m, v


def _place():
    x, y, c = lax.axis_index("x"), lax.axis_index("y"), lax.axis_index("c")
    chips = [(1 - x, y), (x, 1 - y), (1 - x, 1 - y)]
    return x, y, c, chips


def _remote(src, dst, send_sem, recv_sem, to):
    return pltpu.make_async_remote_copy(src_ref=src, dst_ref=dst, send_sem=send_sem, recv_sem=recv_sem,
                                        device_id=to, device_id_type=MESH)


def _split(ref, plan):
    views = [ref]
    for axis, parts in plan:
        size = ref.shape[axis] // parts
        assert size * parts == ref.shape[axis]
        views = [v.at[tuple(pl.ds(q * size, size) if i == axis else slice(None) for i in range(len(ref.shape)))]
                 for v in views for q in range(parts)]
    return views


def _remote_in_parts(src, dst, send_sem, recv_sem, to, plan):
    for s, d in zip(_split(src, plan), _split(dst, plan)):
        _remote(s, d, send_sem, recv_sem, to).start()
    return _remote(src, dst, send_sem, recv_sem, to)


def _hbm_call(body, name, inputs, out_shapes, scratch, aliases=None):
    return pl.pallas_call(
        body, name=name,
        in_specs=[ANY] * len(inputs), out_specs=tuple([ANY] * len(out_shapes)), out_shape=tuple(out_shapes),
        scratch_shapes=scratch, input_output_aliases=aliases or {},
        compiler_params=pltpu.CompilerParams(has_side_effects=True),
    )(*inputs)


def _shard_half(kind, ref, chip, cc):
    if kind == 0:
        rows, cols = ref.shape[0] // 2, ref.shape[1] // 4
        return ref.at[pl.ds(cc * rows, rows), pl.ds(pl.multiple_of(chip * cols, 128), cols)]
    if kind == 3:
        rows = ref.shape[1] // 8
        return ref.at[:, pl.ds(pl.multiple_of((2 * chip + cc) * rows, 16), rows), :]
    rows = ref.shape[0] // 8
    return ref.at[pl.ds(pl.multiple_of((2 * chip + cc) * rows, 16), rows), :]


def _relay_rider(full):
    def quarter(ref, chip_no, cc, q):
        return _split(_shard_half(0, ref, chip_no, cc), [(0, 2)])[q]

    def run(in_refs, full_refs, send_sems, recv_sems, start):
        (ref,) = full_refs
        x, y, c, chips = _place()
        sibling = (x, y, 1 - c)
        chip_no = [2 * ch[0] + ch[1] for ch in chips]
        if start:
            for p in (0, 1):
                held = quarter(ref, chip_no[1 - p], c, p)
                _remote(held, held, send_sems.at[p], recv_sems.at[p], (*chips[p], c)).start()
            return
        for p in (0, 1):
            landed = quarter(ref, chip_no[2], c, p)
            _remote(landed, landed, send_sems.at[p], recv_sems.at[p], (*chips[p], c)).wait_recv()
            _remote(landed, landed, send_sems.at[2], recv_sems.at[2], sibling).start()
        mine, theirs = _shard_half(0, ref, chip_no[2], c), _shard_half(0, ref, chip_no[2], 1 - c)
        _remote(mine, mine, send_sems.at[2], recv_sems.at[2], sibling).wait_send()
        _remote(theirs, theirs, send_sems.at[2], recv_sems.at[2], sibling).wait_recv()
        for p in (0, 1):
            held = quarter(ref, chip_no[1 - p], c, p)
            _remote(held, held, send_sems.at[p], recv_sems.at[p], (*chips[p], c)).wait_send()

    return _Rider([full], [jax.ShapeDtypeStruct(full.shape, full.dtype)], 3, run, aliases={0: 0})


def _riders(riders):
    def bounds(counts):
        ends = [sum(counts[:i + 1]) for i in range(len(counts))]
        return list(zip([0] + ends[:-1], ends))

    ins = bounds([len(r.inputs) for r in riders])
    outs = bounds([len(r.out_shapes) for r in riders])
    sems = bounds([r.n_sems for r in riders])

    class From:
        def __init__(self, sem_refs, base):
            self.sem_refs, self.base, self.at = sem_refs, base, self

        def __getitem__(self, k):
            return self.sem_refs.at[self.base + k]

    def run(in_refs, out_refs, send_sems, recv_sems, start):
        for r, (i0, i1), (o0, o1), (s0, _) in zip(riders, ins, outs, sems):
            r.run(in_refs[i0:i1], out_refs[o0:o1], From(send_sems, s0), From(recv_sems, s0), start)

    aliases = {}
    for r, (i0, _), (o0, _) in zip(riders, ins, outs):
        aliases.update({i0 + i: o0 + o for i, o in r.aliases.items()})
    return _Rider([a for r in riders for a in r.inputs], [o for r in riders for o in r.out_shapes],
                  sems[-1][1], run, aliases)


def _gather_rider(fulls, kinds, peers=(0, 1, 2)):
    n = len(fulls)
    full_half = _shard_half

    def run(in_refs, full_refs, send_sems, recv_sems, start):
        x, y, c, chips = _place()
        me = 2 * x + y
        sibling = (x, y, 1 - c)
        plans = [[(0, MAX_PARTS)], [(0, 2)], [(0, 2)], []]
        chips = [(p, chips[p]) for p in peers]
        across = lambda a, p, k: (3 * a + p) * MAX_PARTS + k
        onward = lambda a, p: 3 * n * MAX_PARTS + 3 * a + p

        def parts(a, chip_no, cc):
            return _split(full_half(kinds[a], full_refs[a], chip_no, cc), plans[kinds[a]])

        if start:
            for p, chip in chips:
                for a in range(n):
                    for k, mine in enumerate(parts(a, me, c)):
                        _remote(mine, mine, send_sems.at[across(a, p, k)], recv_sems.at[across(a, p, k)],
                                (*chip, c)).start()
            return
        for k in range(MAX_PARTS):
            for p, chip in chips:
                for a in range(n):
                    landed = parts(a, 2 * chip[0] + chip[1], c)
                    if k < len(landed):
                        _remote(landed[k], landed[k], send_sems.at[across(a, p, k)], recv_sems.at[across(a, p, k)],
                                (*chip, c)).wait_recv()
                        _remote(landed[k], landed[k], send_sems.at[onward(a, p)], recv_sems.at[onward(a, p)],
                                sibling).start()
        for p, chip in chips:
            them = 2 * chip[0] + chip[1]
            for a in range(n):
                passed = full_half(kinds[a], full_refs[a], them, 1 - c)
                _remote(passed, passed, send_sems.at[onward(a, p)], recv_sems.at[onward(a, p)], sibling).wait_recv()
                landed = full_half(kinds[a], full_refs[a], them, c)
                _remote(landed, landed, send_sems.at[onward(a, p)], recv_sems.at[onward(a, p)], sibling).wait_send()
                for k, mine in enumerate(parts(a, me, c)):
                    _remote(mine, mine, send_sems.at[across(a, p, k)], recv_sems.at[across(a, p, k)],
                            (*chip, c)).wait_send()

    return _Rider(fulls, [jax.ShapeDtypeStruct(f.shape, f.dtype) for f in fulls], 3 * n * (MAX_PARTS + 1), run,
                  aliases={a: a for a in range(n)})


def _exchange_halves(grads, name):
    n = len(grads)
    arrays = [g for g, _, _ in grads]
    out_shapes = [jax.ShapeDtypeStruct(tuple(1 if i == ax else dim for i, dim in enumerate(g.shape)), g.dtype)
                  for g, ax, _ in grads]

    def half(ref, ax, cc):
        idx = tuple(pl.ds(cc, 1) if i == ax else slice(None) for i in range(len(ref.shape)))
        return ref.at[idx]

    def body(*refs):
        in_refs, out_refs = refs[:n], refs[n:2 * n]
        send_sems, recv_sems = refs[2 * n:]
        x, y, c, _ = _place()
        sibling = (x, y, 1 - c)
        copies = [_remote_in_parts(half(in_refs[a], grads[a][1], 1 - c), out_refs[a], send_sems.at[a],
                                   recv_sems.at[a], sibling, grads[a][2]) for a in range(n)]
        for rem in copies:
            rem.wait()

    return _hbm_call(body, name, arrays, out_shapes,
                     [pltpu.SemaphoreType.DMA((n,)), pltpu.SemaphoreType.DMA((n,))])


def _exchange_rider(arrays, plans):
    n = len(arrays)

    def run(in_refs, out_refs, send_sems, recv_sems, start):
        x, y, c, _ = _place()
        sibling = (x, y, 1 - c)
        for a in range(n):
            sems = (send_sems.at[a], recv_sems.at[a])
            if start:
                _remote_in_parts(in_refs[a], out_refs[a], *sems, sibling, plans[a])
            else:
                _remote(in_refs[a], out_refs[a], *sems, sibling).wait()

    return _Rider(arrays, [jax.ShapeDtypeStruct(a.shape, a.dtype) for a in arrays], n, run)


def _scatter_rider(parts):
    n = len(parts)
    arrays = [p for p, _, _ in parts]

    def block_shape(p, ax):
        if ax == len(p.shape) - 1:
            return p.shape[:-1] + (p.shape[-1] // 4,)
        return tuple(1 if i == ax else dim for i, dim in enumerate(p.shape))

    out_shapes = [jax.ShapeDtypeStruct((3,) + block_shape(p, ax), p.dtype) for p, ax, _ in parts]

    def block(ref, ax, chip):
        rank = len(ref.shape)
        if ax == rank - 1:
            cols = ref.shape[-1] // 4
            last = pl.ds(pl.multiple_of(chip * cols, 128), cols)
            return ref.at[tuple([slice(None)] * (rank - 1) + [last])]
        return ref.at[tuple(pl.ds(chip, 1) if i == ax else slice(None) for i in range(rank))]

    def run(in_refs, out_refs, send_sems, recv_sems, start):
        x, y, c, chips = _place()
        for a in range(n):
            ax, plan = parts[a][1], parts[a][2]
            for p, chip in enumerate(chips):
                src, dst = block(in_refs[a], ax, 2 * chip[0] + chip[1]), out_refs[a].at[p]
                sems = (send_sems.at[3 * a + p], recv_sems.at[3 * a + p])
                if start:
                    _remote_in_parts(src, dst, *sems, (*chip, c), plan)
                else:
                    _remote(src, dst, *sems, (*chip, c)).wait()

    return _Rider(arrays, out_shapes, 3 * n, run)


def _join_halves(joined):
    n = len(joined)
    arrays = [j for j, _, _ in joined]

    def body(*refs):
        out_refs = refs[n:2 * n]
        send_sems, recv_sems = refs[2 * n:]
        x, y, c, _ = _place()
        sibling = (x, y, 1 - c)

        def half(a, cc):
            rank = len(out_refs[a].shape)
            return out_refs[a].at[tuple(pl.ds(cc, 1) if i == joined[a][1] else slice(None) for i in range(rank))]

        sends = [_remote_in_parts(half(a, c), half(a, c), send_sems.at[a], recv_sems.at[a], sibling, joined[a][2])
                 for a in range(n)]
        for a, rem in enumerate(sends):
            rem.wait_send()
            _remote(half(a, 1 - c), half(a, 1 - c), send_sems.at[a], recv_sems.at[a], sibling).wait_recv()

    return _hbm_call(body, "join_halves", arrays, [jax.ShapeDtypeStruct(j.shape, j.dtype) for j in arrays],
                     [pltpu.SemaphoreType.DMA((n,)), pltpu.SemaphoreType.DMA((n,))],
                     aliases={a: a for a in range(n)})


def _allreduce_small(packed):
    rows, lanes = packed.shape
    half = rows // 2

    def body(in_ref, out_ref, pair_ref, gath_ref, send_sems, recv_sems):
        x, y, c, chips = _place()
        me = 2 * x + y
        sibling = (x, y, 1 - c)
        mine = pl.ds(pl.multiple_of(c * half, 8), half)
        theirs = pl.ds(pl.multiple_of((1 - c) * half, 8), half)
        to_sib = _remote(in_ref.at[theirs], pair_ref, send_sems.at[0], recv_sems.at[0], sibling)
        to_sib.start()
        to_sib.wait()
        gath_ref[me] = in_ref[mine] + pair_ref[...]
        sends = [_remote(gath_ref.at[me], gath_ref.at[me], send_sems.at[1 + p], recv_sems.at[1 + p], (*chip, c))
                 for p, chip in enumerate(chips)]
        for cp in sends:
            cp.start()
        for p, chip in enumerate(chips):
            slot = gath_ref.at[2 * chip[0] + chip[1]]
            _remote(slot, slot, send_sems.at[1 + p], recv_sems.at[1 + p], (*chip, c)).wait_recv()
        for cp in sends:
            cp.wait_send()
        out_ref[mine] = ((gath_ref[0] + gath_ref[1]) + gath_ref[2]) + gath_ref[3]
        back = _remote(out_ref.at[mine], out_ref.at[mine], send_sems.at[4], recv_sems.at[4], sibling)
        back.start()
        back.wait_send()
        _remote(out_ref.at[theirs], out_ref.at[theirs], send_sems.at[4], recv_sems.at[4], sibling).wait_recv()

    vmem = pl.BlockSpec(memory_space=pltpu.VMEM)
    return pl.pallas_call(
        body, name="allreduce_small",
        in_specs=[vmem], out_specs=vmem, out_shape=jax.ShapeDtypeStruct((rows, lanes), F32),
        scratch_shapes=[pltpu.VMEM((half, lanes), F32), pltpu.VMEM((4, half, lanes), F32),
                        pltpu.SemaphoreType.DMA((5,)), pltpu.SemaphoreType.DMA((5,))],
        compiler_params=pltpu.CompilerParams(has_side_effects=True, vmem_limit_bytes=32 * 1024 * 1024),
    )(packed)


SMALL = ("norm_pre", "pool_scale", "sgu_ln_g", "sgu_ln_b", "sgu_w", "sgu_b", "mem_norm", "branch_norm", "norm_post")
LARGE = ("w_in", "pool_w", "w_kv", "w_out")
ORDER = ("norm_pre", "w_in", "pool_w", "pool_scale", "sgu_ln_g", "sgu_ln_b", "sgu_w", "sgu_b", "mem_norm", "w_kv",
         "branch_norm", "w_out", "norm_post")


def _pack(arrays, extra=()):
    rows = [a.reshape(-1, 128) for a in arrays] + list(extra)
    pad = -sum(r.shape[0] for r in rows) % 16
    return jnp.concatenate(rows + ([jnp.zeros((pad, 128), F32)] if pad else []), axis=0)


def _unpack(packed, like):
    out, row = [], 0
    for a in like:
        rows = a.size // 128
        out.append(packed[row:row + rows].reshape(a.shape))
        row += rows
    return out


def kernel(x, mem, norm_pre, w_in, pool_w, pool_scale, sgu_ln_g, sgu_ln_b, sgu_w, sgu_b, mem_norm, w_kv, branch_norm, w_out, norm_post, loss_target, m_norm_pre, m_w_in, m_pool_w, m_pool_scale, m_sgu_ln_g, m_sgu_ln_b, m_sgu_w, m_sgu_b, m_mem_norm, m_w_kv, m_branch_norm, m_w_out, m_norm_post, v_norm_pre, v_w_in, v_pool_w, v_pool_scale, v_sgu_ln_g, v_sgu_ln_b, v_sgu_w, v_sgu_b, v_mem_norm, v_w_kv, v_branch_norm, v_w_out, v_norm_post):
    weights = dict(norm_pre=norm_pre, w_in=w_in, pool_w=pool_w, pool_scale=pool_scale, sgu_ln_g=sgu_ln_g,
                   sgu_ln_b=sgu_ln_b, sgu_w=sgu_w, sgu_b=sgu_b, mem_norm=mem_norm, w_kv=w_kv, branch_norm=branch_norm,
                   w_out=w_out, norm_post=norm_post)
    mom1 = dict(norm_pre=m_norm_pre, w_in=m_w_in, pool_w=m_pool_w, pool_scale=m_pool_scale, sgu_ln_g=m_sgu_ln_g,
                sgu_ln_b=m_sgu_ln_b, sgu_w=m_sgu_w, sgu_b=m_sgu_b, mem_norm=m_mem_norm, w_kv=m_w_kv,
                branch_norm=m_branch_norm, w_out=m_w_out, norm_post=m_norm_post)
    mom2 = dict(norm_pre=v_norm_pre, w_in=v_w_in, pool_w=v_pool_w, pool_scale=v_pool_scale, sgu_ln_g=v_sgu_ln_g,
                sgu_ln_b=v_sgu_ln_b, sgu_w=v_sgu_w, sgu_b=v_sgu_b, mem_norm=v_mem_norm, w_kv=v_w_kv,
                branch_norm=v_branch_norm, w_out=v_w_out, norm_post=v_norm_post)

    s, d = x.shape[1], x.shape[2]
    x2, mem2, tgt2 = x[0], mem[0], loss_target[0]
    t_branch = min(256, s)
    tm = min(512, s)

    core = lax.axis_index("c")
    chip = 2 * lax.axis_index("x") + lax.axis_index("y")
    pos = jnp.stack([core, chip]).astype(jnp.int32)
    n_in, n_kv, n_out = 4 * w_in.shape[2], 4 * w_kv.shape[1], 4 * w_out.shape[1]
    wi_rows, kv_rows, wo_rows = d // 8, n_kv // 8, n_out // 8

    def placed(shard, full_shape, block, grid, in_map, out_map, name):
        return _blockwise(_cast_copy, pos, [shard], [pl.BlockSpec(block, in_map)],
                          jax.ShapeDtypeStruct(full_shape, BF16), pl.BlockSpec(block, out_map), grid, name)

    kv_cols, pw_rows = w_kv.shape[2], GROUP // 8
    wi_own = placed(w_in[0], (d, n_in), (wi_rows, n_in // 4), (8,), lambda i, p: (i, 0), lambda i, p: (i, p[1]),
                    "place_w_in")
    wkv_own = placed(w_kv[0], (n_kv, kv_cols), (kv_rows, kv_cols), (2,), lambda i, p: (i, 0),
                     lambda i, p: (2 * p[1] + i, 0), "place_w_kv")
    wo_own = placed(w_out[0], (n_out, d), (wo_rows, d), (2,), lambda i, p: (i, 0), lambda i, p: (2 * p[1] + i, 0),
                    "place_w_out")
    pw_own = placed(pool_w[0], (4, GROUP, GROUP), (4, GROUP // 4, GROUP), (1,), lambda i, p: (0, 0, 0),
                    lambda i, p: (0, p[1], 0), "place_pool_w")

    x_pos, y_pos = lax.axis_index("x"), lax.axis_index("y")
    chips = jnp.stack([chip, 2 * (1 - x_pos) + y_pos, 2 * x_pos + 1 - y_pos,
                       2 * (1 - x_pos) + 1 - y_pos]).astype(jnp.int32)
    mem_g = mem_norm.reshape(1, d)
    proj, h, h_t, wi_full = _proj_piece(chips, 0, 1, x2, norm_pre, None, None, n_in,
                                   _gather_rider([wi_own], [0], peers=(0, 1)), tm, "proj_own")
    proj, wi_full, wkv_full, pw_full = _proj_piece(
        chips, 1, 2, h, None, None, proj, n_in,
        _riders([_relay_rider(wi_full), _gather_rider([wkv_own, pw_own], [1, 3])]), tm, "proj_neighbours")
    proj, wo_part = _proj_piece(chips, 3, 1, h, None, wi_full, proj, n_in,
                                _gather_rider([wo_own], [2], peers=(0, 1)), tm, "proj_diagonal")
    k_m, v_m = _kv_fwd(mem2, mem_g, wkv_full)
    bias_full = jnp.repeat(sgu_b[0].T, CHUNK, axis=1)
    y, y_t, wo_full = _branches_fwd(proj, pw_full, pool_scale, sgu_ln_g, sgu_ln_b, sgu_w[0], bias_full, k_m, v_m,
                                    branch_norm, t_branch, _gather_rider([wo_part], [2], peers=(2,)))
    loss_local, dz, dout, dy, g_norm_post = _out_loss(y, wo_full, x2, tgt2, norm_post, min(256, s))

    def pair_sums(views):
        theirs = _exchange_halves([(v[0], v[1], v[2]) for v in views], "exchange_for_" + views[0][9])
        return [_blockwise(_pair_sum, pos, [v[0], th], [pl.BlockSpec(v[3], v[4][0]), pl.BlockSpec(v[3], v[4][1])],
                           jax.ShapeDtypeStruct(v[5], BF16), pl.BlockSpec(v[6], v[7]), v[8], v[9])
                for v, th in zip(views, theirs)]

    tk = min(1024, s)
    (g_wo,) = _grad_rows(y_t, dout, pos, lambda i, p: i, n_out, n_out // 2, 1024, tk, "grad_w_out")
    (ps_wo,) = pair_sums([
        (g_wo.reshape(4, 2, wo_rows, d), 1, [(0, 4), (2, 2)], (1, 1, wo_rows, d),
         (lambda i, p: (i, p[0], 0, 0), lambda i, p: (i, 0, 0, 0)), (4, wo_rows, d), (1, wo_rows, d),
         lambda i, p: (i, 0, 0), (4,), "pair_sum_w_out")])
    (dproj, g_pw, g_pool_scale, g_ln_g, g_ln_b, g_sgu_w, g_sgu_b_t, g_branch_norm, dk, dv, landed_wo) = _branches_bwd(
        proj, dy, pw_full, pool_scale, sgu_ln_g, sgu_ln_b, sgu_w[0], jnp.swapaxes(sgu_w[0], 1, 2), bias_full,
        k_m, v_m, branch_norm, t_branch, _scatter_rider([(ps_wo, 0, [(1, 2)])]))
    g_wkv, g_mem_norm = _kv_bwd(mem2, mem_g, wkv_full, dk, dv)
    ps_kv, ps_pw = pair_sums([
        (g_wkv.reshape(4, 2, kv_rows, kv_cols), 1, [(0, 4), (2, 2)], (1, 1, kv_rows, kv_cols),
         (lambda i, p: (i, p[0], 0, 0), lambda i, p: (i, 0, 0, 0)), (4, kv_rows, kv_cols), (1, kv_rows, kv_cols),
         lambda i, p: (i, 0, 0), (4,), "pair_sum_w_kv"),
        (g_pw.astype(BF16).reshape(4, 4, 2, pw_rows, GROUP), 2, [(0, 4)], (1, 4, 1, pw_rows, GROUP),
         (lambda i, p: (i, 0, p[0], 0, 0), lambda i, p: (i, 0, 0, 0, 0)), (4, 4, pw_rows, GROUP),
         (1, 4, pw_rows, GROUP), lambda i, p: (i, 0, 0, 0), (4,), "pair_sum_pool_w")])
    gwi_theirs, landed_kv, landed_pw = _grad_rows(
        h_t, dproj, pos, lambda i, p: 1 - p[0], d // 2, d // 2, n_in // 4, tk, "grad_w_in_sibling_half",
        _scatter_rider([(ps_kv, 0, [(1, 2)]), (ps_pw, 1, [])]))
    gwi_mine, gwi_from_sibling = _grad_rows(h_t, dproj, pos, lambda i, p: p[0], d // 2, d // 2, n_in // 4, tk,
                                            "grad_w_in_own_half", _exchange_rider([gwi_theirs], [[(0, 16)]]))
    ps_wi = _elementwise(_pair_sum, [gwi_mine, gwi_from_sibling], [BF16], "pair_sum_w_in")[0]
    grad_x, g_norm_pre, landed_wi = _dx_call(dproj, wi_full, x2, dz, norm_pre, tm, 1024,
                                             _scatter_rider([(ps_wi, 1, [(0, 4)])]))
    psum = [ps_wi, ps_kv, ps_wo, ps_pw]
    landed = [landed_wi, landed_kv, landed_wo, landed_pw]
    from_chip = lambda spec_shape, rank: [
        pl.BlockSpec(spec_shape, functools.partial(lambda i, p, q: (q, i) + (0,) * (rank - 2), q=q))
        for q in range(3)]
    joined = _join_halves([
        (_blockwise(_four_sum, pos, [psum[0]] + [landed[0]] * 3,
                    [pl.BlockSpec((256, n_in // 4), lambda i, p: (i, p[1]))] + from_chip((1, 256, n_in // 4), 3),
                    jax.ShapeDtypeStruct((2, d // 2, n_in // 4), F32),
                    pl.BlockSpec((1, 256, n_in // 4), lambda i, p: (p[0], i, 0)), (d // 2 // 256,), "chip_sum_w_in"),
         0, [(1, 8)]),
        (_blockwise(_four_sum, pos, [psum[1]] + [landed[1]] * 3,
                    [pl.BlockSpec((1, kv_rows, kv_cols), lambda i, p: (p[1], 0, 0))]
                    + from_chip((1, 1, kv_rows, kv_cols), 4),
                    jax.ShapeDtypeStruct((2, kv_rows, kv_cols), F32),
                    pl.BlockSpec((1, kv_rows, kv_cols), lambda i, p: (p[0], 0, 0)), (1,), "chip_sum_w_kv"),
         0, [(1, 2)]),
        (_blockwise(_four_sum, pos, [psum[2]] + [landed[2]] * 3,
                    [pl.BlockSpec((1, wo_rows, d), lambda i, p: (p[1], 0, 0))] + from_chip((1, 1, wo_rows, d), 4),
                    jax.ShapeDtypeStruct((2, wo_rows, d), F32),
                    pl.BlockSpec((1, wo_rows, d), lambda i, p: (p[0], 0, 0)), (1,), "chip_sum_w_out"),
         0, [(1, 2)]),
        (_blockwise(_four_sum, pos, [psum[3]] + [landed[3]] * 3,
                    [pl.BlockSpec((4, 1, pw_rows, GROUP), lambda i, p: (0, p[1], 0, 0))]
                    + from_chip((1, 4, 1, pw_rows, GROUP), 5),
                    jax.ShapeDtypeStruct((4, 2, pw_rows, GROUP), F32),
                    pl.BlockSpec((4, 1, pw_rows, GROUP), lambda i, p: (0, p[0], 0, 0)), (1,), "chip_sum_pool_w"),
         1, []),
    ])
    grads = {"w_in": joined[0].reshape(w_in.shape), "w_kv": joined[1].reshape(w_kv.shape),
             "w_out": joined[2].reshape(w_out.shape), "pool_w": joined[3].reshape(pool_w.shape)}

    small_local = dict(norm_pre=g_norm_pre, pool_scale=g_pool_scale, sgu_ln_g=g_ln_g, sgu_ln_b=g_ln_b,
                       sgu_w=g_sgu_w, sgu_b=g_sgu_b_t.T, mem_norm=g_mem_norm, branch_norm=g_branch_norm,
                       norm_post=g_norm_post)
    small_rows = sum(weights[n].size for n in SMALL) // 128
    small_sum = _allreduce_small(_pack([small_local[n] for n in SMALL], [jnp.pad(loss_local, ((0, 7), (0, 127)))]))
    for n, g in zip(SMALL, _unpack(small_sum, [weights[n] for n in SMALL])):
        grads[n] = g
    loss = small_sum[small_rows, 0]

    delta, new_m, new_v = {}, {}, {}
    packed = [small_sum if src is grads else _pack([src[n] for n in SMALL]) for src in (weights, grads, mom1, mom2)]
    outs = _elementwise(_adamw, packed, [F32, F32, F32], "adamw_small")
    for dst, o in zip((delta, new_m, new_v), outs):
        for n, a in zip(SMALL, _unpack(o, [weights[n] for n in SMALL])):
            dst[n] = a
    for n in LARGE:
        cols = weights[n].shape[-1]
        outs = _elementwise(_adamw, [src[n].reshape(-1, cols) for src in (weights, grads, mom1, mom2)],
                            [F32, F32, F32], "adamw_" + n)
        for dst, o in zip((delta, new_m, new_v), outs):
            dst[n] = o.reshape(weights[n].shape)

    return (loss, grad_x[None], *[grads[n] for n in ORDER], *[delta[n] for n in ORDER],
            *[new_m[n] for n in ORDER], *[new_v[n] for n in ORDER])
```

```python
import functools

import jax
import jax.numpy as jnp
from jax import lax
from jax.experimental import pallas as pl
from jax.experimental.pallas import tpu as pltpu

F32 = jnp.float32
BF16 = jnp.bfloat16
EPS = 1e-6
MESH = pl.DeviceIdType.MESH
ANY = pl.BlockSpec(memory_space=pl.ANY)

POOL_WINDOWS = (2, 4, 8, 16)
GROUP = 256
HALO = 16
CHUNK = 128
N_SGU_HEADS = 8
N_ATT_HEADS = 4
ATT_DIM = 256
WIDTH = 1024
ATT_SCALE = 1.0 / 16.0

ADAM_LR = 0.001
ADAM_B1 = 0.9
ADAM_B2 = 0.999
ADAM_EPS = 1e-08
ADAM_WD = 0.01
ADAM_STEP = 10

VMEM_LIMIT = 60 * 1024 * 1024
MAX_PARTS = 4


def _params(n_grid_axes, vmem=VMEM_LIMIT):
    return pltpu.CompilerParams(dimension_semantics=("arbitrary",) * n_grid_axes, vmem_limit_bytes=vmem)


def _dot(a, b, dims):
    return lax.dot_general(a, b, (dims, ((), ())), preferred_element_type=F32)


NN = ((1,), (0,))
NT = ((1,), (1,))
TN = ((0,), (0,))


class _Rider:
    def __init__(self, inputs, out_shapes, n_sems, run, aliases=None):
        self.inputs, self.out_shapes, self.n_sems, self.run = list(inputs), list(out_shapes), n_sems, run
        self.aliases = aliases or {}


def _call(body, name, grid, in_specs, out_specs, out_shape, scratch_shapes, inputs, rider=None, prefetch=None,
          aliases=None, rider_refs=False):
    n_in, n_out, n_scr = len(in_specs), len(out_specs), len(scratch_shapes)
    r_in = len(rider.inputs) if rider else 0
    r_out = len(rider.out_shapes) if rider else 0
    n_pre = 0 if prefetch is None else 1

    def whole_body(*refs):
        pre, refs = refs[:n_pre], refs[n_pre:]
        ins, rider_ins = refs[:n_in], refs[n_in:n_in + r_in]
        refs = refs[n_in + r_in:]
        outs, rider_outs = refs[:n_out], refs[n_out:n_out + r_out]
        refs = refs[n_out + r_out:]
        scratch, sems = refs[:n_scr], refs[n_scr:]
        extra = {"rider_outs": rider_outs} if rider_refs else {}
        if rider is None:
            body(*pre, *ins, *outs, *scratch, **extra)
            return
        ids = [pl.program_id(ax) for ax in range(len(grid))]
        first = functools.reduce(lambda p, q: p & q, [i == 0 for i in ids])
        last = functools.reduce(lambda p, q: p & q, [i == g - 1 for i, g in zip(ids, grid)])

        @pl.when(first)
        def _():
            rider.run(rider_ins, rider_outs, *sems, True)

        body(*pre, *ins, *outs, *scratch, **extra)

        @pl.when(last)
        def _():
            rider.run(rider_ins, rider_outs, *sems, False)

    io_aliases = {n_pre + i: o for i, o in (aliases or {}).items()}
    scratch_all = list(scratch_shapes)
    if rider:
        io_aliases.update({n_pre + n_in + i: n_out + o for i, o in rider.aliases.items()})
        scratch_all += [pltpu.SemaphoreType.DMA((rider.n_sems,)), pltpu.SemaphoreType.DMA((rider.n_sems,))]
    specs = dict(grid=grid, in_specs=list(in_specs) + [ANY] * r_in, out_specs=tuple(out_specs) + (ANY,) * r_out,
                 scratch_shapes=scratch_all)
    if n_pre:
        specs = dict(grid_spec=pltpu.PrefetchScalarGridSpec(num_scalar_prefetch=1, **specs))
    outs = pl.pallas_call(
        whole_body, name=name, **specs,
        out_shape=tuple(out_shape) + tuple(rider.out_shapes if rider else ()),
        input_output_aliases=io_aliases, compiler_params=_params(len(grid)),
    )(*([prefetch] if n_pre else []), *inputs, *(rider.inputs if rider else []))
    return tuple(outs)


def _grad_rows(a_t, b, pos, row_of, m, tm, tn, tk, name, rider=None):
    k, n = a_t.shape[1], b.shape[1]
    nk = k // tk
    out_dtype, dims, a = BF16, NN, a_t
    a_spec = pl.BlockSpec((tm, tk), lambda i, j, kk, p: (row_of(i, p), kk))
    b_spec = pl.BlockSpec((tk, tn), lambda i, j, kk, p: (kk, j))

    def body(pos_ref, a_ref, b_ref, o_ref, *acc):
        part = lambda: _dot(a_ref[...], b_ref[...], dims)
        if nk == 1:
            o_ref[...] = part().astype(out_dtype)
            return
        (acc_ref,) = acc
        kk = pl.program_id(2)

        @pl.when(kk == 0)
        def _():
            acc_ref[...] = part()

        @pl.when((kk > 0) & (kk < nk - 1))
        def _():
            acc_ref[...] += part()

        @pl.when(kk == nk - 1)
        def _():
            o_ref[...] = (acc_ref[...] + part()).astype(out_dtype)

    return _call(body, name, (m // tm, n // tn, nk), [a_spec, b_spec],
                 [pl.BlockSpec((tm, tn), lambda i, j, kk, p: (i, j))], [jax.ShapeDtypeStruct((m, n), out_dtype)],
                 [pltpu.VMEM((tm, tn), F32)] if nk > 1 else [], [a, b], rider, prefetch=pos)


def _proj_piece(chips, first, n_shards, src, g_pre, w_in, proj_in, n_cols, rider, tm, name):
    s, d = src.shape
    cols = n_cols // 4
    fused = g_pre is not None

    def body(chips_ref, *refs, rider_outs=()):
        refs = list(refs)
        src_ref = refs.pop(0)
        g_ref = refs.pop(0) if fused else None
        w_ref = refs.pop(0) if w_in is not None else rider_outs[0]
        if proj_in is not None:
            refs.pop(0)
        proj_ref = refs.pop(0)
        h_ref, ht_ref = (refs.pop(0), refs.pop(0)) if fused else (None, None)
        wbuf, sem = refs
        q, i = pl.program_id(0), pl.program_id(1)

        @pl.when(i == 0)
        def _():
            at = pl.multiple_of(chips_ref[first + q] * cols, 128)
            cp = pltpu.make_async_copy(w_ref.at[:, pl.ds(at, cols)], wbuf, sem)
            cp.start()
            cp.wait()

        if fused:
            xv = src_ref[...]
            r = lax.rsqrt(jnp.mean(xv * xv, axis=-1, keepdims=True) + EPS)
            h = (xv * r * g_ref[...]).astype(BF16)
            h_ref[...] = h
            ht_ref[...] = h.T
        else:
            h = src_ref[...]
        proj_ref[...] = _dot(h, wbuf[...], NN)

    row = lambda q, i, ch: (i, 0)
    inputs, in_specs = [src], [pl.BlockSpec((tm, d), row)]
    if fused:
        inputs.append(g_pre)
        in_specs.append(pl.BlockSpec((1, d), lambda q, i, ch: (0, 0)))
    if w_in is not None:
        inputs.append(w_in)
        in_specs.append(ANY)
    aliases = {}
    if proj_in is not None:
        aliases[len(inputs)] = 0
        inputs.append(proj_in)
        in_specs.append(ANY)
    out_specs = [pl.BlockSpec((tm, cols), lambda q, i, ch: (i, ch[first + q]))]
    out_shape = [jax.ShapeDtypeStruct((s, n_cols), F32)]
    if fused:
        assert n_shards == 1
        out_specs += [pl.BlockSpec((tm, d), row), pl.BlockSpec((d, tm), lambda q, i, ch: (0, i))]
        out_shape += [jax.ShapeDtypeStruct((s, d), BF16), jax.ShapeDtypeStruct((d, s), BF16)]
    return _call(body, name, (n_shards, s // tm), in_specs, out_specs, out_shape,
                 [pltpu.VMEM((d, cols), BF16), pltpu.SemaphoreType.DMA(())], inputs, rider, prefetch=chips,
                 aliases=aliases, rider_refs=True)


def _kv_fwd(mem, g, w_kv):
    m, d = mem.shape

    def body(mem_ref, g_ref, w_ref, k_ref, v_ref):
        mv = mem_ref[...]
        r = lax.rsqrt(jnp.mean(mv * mv, axis=-1, keepdims=True) + EPS)
        mem_n = (mv * r * g_ref[...]).astype(BF16)
        kv = _dot(mem_n, w_ref[...], NN)
        k_ref[...] = kv[:, :WIDTH].astype(BF16)
        v_ref[...] = kv[:, WIDTH:].astype(BF16)

    return pl.pallas_call(
        body, name="kv_fwd",
        out_shape=(jax.ShapeDtypeStruct((m, WIDTH), BF16), jax.ShapeDtypeStruct((m, WIDTH), BF16)),
        compiler_params=_params(0),
    )(mem, g, w_kv)


def _kv_bwd(mem, g, w_kv, dk, dv):
    m, d = mem.shape
    n = w_kv.shape[1]
    col = 512

    def body(mem_ref, g_ref, w_ref, dk_ref, dv_ref, dw_ref, dg_ref):
        mv = mem_ref[...]
        r = lax.rsqrt(jnp.mean(mv * mv, axis=-1, keepdims=True) + EPS)
        mem_hat = mv * r
        mem_n = (mem_hat * g_ref[...]).astype(BF16)
        dkv = jnp.concatenate([dk_ref[...], dv_ref[...]], axis=1).astype(BF16)
        for j in range(n // col):
            dw_ref[:, j * col:(j + 1) * col] = _dot(mem_n, dkv[:, j * col:(j + 1) * col], TN).astype(BF16)
        dmem_n = _dot(dkv, w_ref[...], NT)
        dg_ref[...] = jnp.sum(dmem_n * mem_hat, axis=0, keepdims=True)

    return pl.pallas_call(
        body, name="kv_bwd",
        out_shape=(jax.ShapeDtypeStruct((d, n), BF16), jax.ShapeDtypeStruct((1, d), F32)),
        compiler_params=_params(0),
    )(mem, g, w_kv, dk, dv)


def _sigmoid(x):
    return 1.0 / (1.0 + jnp.exp(-x))


def _inv_counts(t0, t):
    pos = (t0 + lax.broadcasted_iota(jnp.int32, (t, 1), 0) + 1).astype(F32)
    return [1.0 / jnp.minimum(pos, float(w)) for w in POOL_WINDOWS]


def _window_sums(ext, t, backward):
    n = t + HALO
    parts = []
    for gi, w in enumerate(POOL_WINDOWS):
        s = ext[:, gi * GROUP:(gi + 1) * GROUP]
        k = 1
        while k < w:
            s = s + pltpu.roll(s, (n - k) if backward else k, axis=0)
            k *= 2
        parts.append(s[:t] if backward else s[HALO:])
    return parts


def _pool_fwd(xa, halo, inv, pool_w):
    t = xa.shape[0]
    sums = _window_sums(jnp.concatenate([halo, xa], axis=0), t, backward=False)
    d = jnp.concatenate([sums[gi] * inv[gi] - xa[:, gi * GROUP:(gi + 1) * GROUP] for gi in range(4)], axis=1)
    d = d.astype(BF16)
    y = jnp.concatenate([_dot(d[:, gi * GROUP:(gi + 1) * GROUP], pool_w[gi], NN) for gi in range(4)], axis=1)
    return d, y


def _layernorm_fwd(v):
    mu = jnp.mean(v, axis=-1, keepdims=True)
    xc = v - mu
    rstd = lax.rsqrt(jnp.mean(xc * xc, axis=-1, keepdims=True) + EPS)
    return xc * rstd, rstd


def _tril_mask(transposed):
    r = lax.broadcasted_iota(jnp.int32, (CHUNK, CHUNK), 0)
    c = lax.broadcasted_iota(jnp.int32, (CHUNK, CHUNK), 1)
    return (r <= c) if transposed else (r >= c)


def _sgu_mix(w_ref, vals, transposed):
    t = vals.shape[0]
    mask = _tril_mask(transposed)
    ws = [jnp.where(mask, w_ref[h], 0.0).astype(BF16) for h in range(N_SGU_HEADS)]
    rows = []
    for ci in range(t // CHUNK):
        blk = vals[ci * CHUNK:(ci + 1) * CHUNK]
        rows.append(jnp.concatenate(
            [_dot(ws[h], blk[:, h * CHUNK:(h + 1) * CHUNK], NN) for h in range(N_SGU_HEADS)], axis=1))
    return jnp.concatenate(rows, axis=0)


def _attn_fwd(q, k, v):
    ps, os_ = [], []
    for h in range(N_ATT_HEADS):
        sl = slice(h * ATT_DIM, (h + 1) * ATT_DIM)
        s = _dot(q[:, sl], k[:, sl], NT) * ATT_SCALE
        s = s - jnp.max(s, axis=-1, keepdims=True)
        e = jnp.exp(s)
        p = e * (1.0 / jnp.sum(e, axis=-1, keepdims=True))
        ps.append(p)
        os_.append(_dot(p.astype(BF16), v[:, sl], NN))
    return ps, jnp.concatenate(os_, axis=1)


def _rms_branch(y_pre):
    r = lax.rsqrt(jnp.mean(y_pre * y_pre, axis=-1, keepdims=True) + EPS)
    return y_pre * r, r


def _branch_specs(t, n_tiles, order):
    width_in = 7 * WIDTH
    tile = lambda i: order(i)
    per_halo = t // HALO
    const2 = lambda i: (0, 0)
    const3 = lambda i: (0, 0, 0)
    return [
        pl.BlockSpec((t, width_in), lambda i: (tile(i), 0)),
        pl.BlockSpec((HALO, WIDTH), lambda i: (jnp.maximum(tile(i) * per_halo - 1, 0), 0)),
        pl.BlockSpec((4, GROUP, GROUP), const3),
        pl.BlockSpec((1, WIDTH), const2),
        pl.BlockSpec((1, WIDTH), const2),
        pl.BlockSpec((1, WIDTH), const2),
        pl.BlockSpec((N_SGU_HEADS, CHUNK, CHUNK), const3),
        pl.BlockSpec((CHUNK, WIDTH), const2),
        pl.BlockSpec((MEM_ROWS, WIDTH), const2),
        pl.BlockSpec((MEM_ROWS, WIDTH), const2),
        pl.BlockSpec((1, 3 * WIDTH), const2),
    ]


MEM_ROWS = 256


def _branches_fwd(proj, pool_w, pool_scale, ln_g, ln_b, sgu_w, bias_full, k, v, branch_norm, t, rider=None):
    s = proj.shape[0]
    n_tiles = s // t

    def body(proj_ref, halo_ref, pw_ref, ps_ref, lg_ref, lb_ref, sw_ref, sb_ref, k_ref, v_ref, bn_ref, y_ref, yt_ref):
        i = pl.program_id(0)
        col = lambda j: proj_ref[:, j * WIDTH:(j + 1) * WIDTH]

        def put(branch, y_pre):
            sl = slice(branch * WIDTH, (branch + 1) * WIDTH)
            val = (_rms_branch(y_pre)[0] * bn[:, sl]).astype(BF16)
            y_ref[:, sl] = val
            yt_ref[sl, :] = val.T

        bn = bn_ref[...]
        halo = jnp.where(i > 0, halo_ref[...], 0.0)
        _, y_pool = _pool_fwd(col(0), halo, _inv_counts(i * t, t), pw_ref[...])
        ga = col(1)
        ya = y_pool * ps_ref[...] * (ga * _sigmoid(ga))
        put(0, ya)
        vhat, _ = _layernorm_fwd(col(3))
        vn = (vhat * lg_ref[...] + lb_ref[...]).astype(BF16)
        z = _sgu_mix(sw_ref, vn, transposed=False) + jnp.tile(sb_ref[...], (t // CHUNK, 1))
        gb = col(4)
        yb = col(2) * z * (gb * _sigmoid(gb))
        put(1, yb)
        _, o = _attn_fwd(col(5).astype(BF16), k_ref[...], v_ref[...])
        gc = col(6)
        yc = o * (gc * _sigmoid(gc))
        put(2, yc)

    return _call(body, "branches_fwd", (n_tiles,), _branch_specs(t, n_tiles, lambda i: i),
                 [pl.BlockSpec((t, 3 * WIDTH), lambda i: (i, 0)), pl.BlockSpec((3 * WIDTH, t), lambda i: (0, i))],
                 [jax.ShapeDtypeStruct((s, 3 * WIDTH), BF16), jax.ShapeDtypeStruct((3 * WIDTH, s), BF16)], [],
                 [proj, proj, pool_w, pool_scale, ln_g, ln_b, sgu_w, bias_full, k, v, branch_norm], rider)


def _branches_bwd(proj, dy, pool_w, pool_scale, ln_g, ln_b, sgu_w, sgu_wt, bias_full, k, v, branch_norm, t, rider=None):
    s = proj.shape[0]
    n_tiles = s // t
    n_chunks = t // CHUNK
    order = lambda i: n_tiles - 1 - i

    def body(proj_ref, halo_ref, pw_ref, ps_ref, lg_ref, lb_ref, sw_ref, sb_ref, k_ref, v_ref, bn_ref,
             swt_ref, dy_ref,
             dproj_ref, dpw_ref, dps_ref, dlg_ref, dlb_ref, dsw_ref, dsb_ref, dbn_ref, dk_ref, dv_ref,
             carry_ref, dbias_ref):
        step = pl.program_id(0)
        i = order(step)

        @pl.when(step == 0)
        def _():
            for ref in (dpw_ref, dps_ref, dlg_ref, dlb_ref, dsw_ref, dbn_ref, dk_ref, dv_ref, carry_ref, dbias_ref):
                ref[...] = jnp.zeros(ref.shape, ref.dtype)

        col = lambda j: proj_ref[:, j * WIDTH:(j + 1) * WIDTH]
        bn = bn_ref[...]

        def norm_bwd(y_pre, sl):
            yhat, r = _rms_branch(y_pre)
            dyv = dy_ref[:, sl].astype(F32)
            dbn_ref[:, sl] += jnp.sum(dyv * yhat, axis=0, keepdims=True)
            dyhat = dyv * bn[:, sl]
            return r * (dyhat - yhat * jnp.mean(dyhat * yhat, axis=-1, keepdims=True))

        def gate(gv):
            sg = _sigmoid(gv)
            return gv * sg, sg * (1.0 + gv * (1.0 - sg))

        inv = _inv_counts(i * t, t)
        halo = jnp.where(i > 0, halo_ref[...], 0.0)
        pw = pw_ref[...]
        d, y_pool = _pool_fwd(col(0), halo, inv, pw)
        scale = ps_ref[...]
        silu_a, dsilu_a = gate(col(1))
        pa = y_pool * scale
        dya = norm_bwd(pa * silu_a, slice(0, WIDTH))
        dproj_ref[:, WIDTH:2 * WIDTH] = (dya * pa * dsilu_a).astype(BF16)
        dpa = dya * silu_a
        dps_ref[...] += jnp.sum(dpa * y_pool, axis=0, keepdims=True)
        dy_pool = (dpa * scale).astype(BF16)
        dd_parts, ddc_parts = [], []
        for gi in range(4):
            sl = slice(gi * GROUP, (gi + 1) * GROUP)
            dpw_ref[gi] += _dot(d[:, sl], dy_pool[:, sl], TN)
            dd = _dot(dy_pool[:, sl], pw[gi], NT)
            dd_parts.append(dd)
            ddc_parts.append(dd * inv[gi])
        ddc = jnp.concatenate(ddc_parts, axis=1)
        sums = _window_sums(jnp.concatenate([ddc, carry_ref[...]], axis=0), t, backward=True)
        carry_ref[...] = ddc[:HALO]
        dproj_ref[:, 0:WIDTH] = jnp.concatenate([sums[gi] - dd_parts[gi] for gi in range(4)], axis=1).astype(BF16)

        vhat, rstd = _layernorm_fwd(col(3))
        lg = lg_ref[...]
        vn = (vhat * lg + lb_ref[...]).astype(BF16)
        z = _sgu_mix(sw_ref, vn, transposed=False) + jnp.tile(sb_ref[...], (n_chunks, 1))
        u = col(2)
        silu_b, dsilu_b = gate(col(4))
        uz = u * z
        dyb = norm_bwd(uz * silu_b, slice(WIDTH, 2 * WIDTH))
        dproj_ref[:, 4 * WIDTH:5 * WIDTH] = (dyb * uz * dsilu_b).astype(BF16)
        duz = dyb * silu_b
        dproj_ref[:, 2 * WIDTH:3 * WIDTH] = (duz * z).astype(BF16)
        dz = duz * u
        dz_b = dz.astype(BF16)
        for ci in range(n_chunks):
            rows = slice(ci * CHUNK, (ci + 1) * CHUNK)
            dbias_ref[...] += dz[rows]
            for h in range(N_SGU_HEADS):
                sl = slice(h * CHUNK, (h + 1) * CHUNK)
                dsw_ref[h] += _dot(dz_b[rows, sl], vn[rows, sl], NT)
        dvn = _sgu_mix(swt_ref, dz_b, transposed=True)
        dlg_ref[...] += jnp.sum(dvn * vhat, axis=0, keepdims=True)
        dlb_ref[...] += jnp.sum(dvn, axis=0, keepdims=True)
        dvhat = dvn * lg
        dvb = rstd * (dvhat - jnp.mean(dvhat, axis=-1, keepdims=True)
                      - vhat * jnp.mean(dvhat * vhat, axis=-1, keepdims=True))
        dproj_ref[:, 3 * WIDTH:4 * WIDTH] = dvb.astype(BF16)

        q = col(5).astype(BF16)
        kv_k, kv_v = k_ref[...], v_ref[...]
        ps, o = _attn_fwd(q, kv_k, kv_v)
        silu_c, dsilu_c = gate(col(6))
        dyc = norm_bwd(o * silu_c, slice(2 * WIDTH, 3 * WIDTH))
        dproj_ref[:, 6 * WIDTH:7 * WIDTH] = (dyc * o * dsilu_c).astype(BF16)
        do = (dyc * silu_c).astype(BF16)
        dq_parts = []
        for h in range(N_ATT_HEADS):
            sl = slice(h * ATT_DIM, (h + 1) * ATT_DIM)
            p = ps[h]
            dp = _dot(do[:, sl], kv_v[:, sl], NT)
            ds = (p * (dp - jnp.sum(p * dp, axis=-1, keepdims=True)) * ATT_SCALE).astype(BF16)
            dq_parts.append(_dot(ds, kv_k[:, sl], NN))
            dk_ref[:, sl] += _dot(ds, q[:, sl], TN)
            dv_ref[:, sl] += _dot(p.astype(BF16), do[:, sl], TN)
        dproj_ref[:, 5 * WIDTH:6 * WIDTH] = jnp.concatenate(dq_parts, axis=1).astype(BF16)

        @pl.when(step == n_tiles - 1)
        def _():
            keep = _tril_mask(transposed=False)
            for h in range(N_SGU_HEADS):
                dsw_ref[h] = jnp.where(keep, dsw_ref[h], 0.0)
            dsb_ref[...] = jnp.concatenate(
                [jnp.sum(dbias_ref[:, h * CHUNK:(h + 1) * CHUNK], axis=1, keepdims=True)
                 for h in range(N_SGU_HEADS)], axis=1)

    const2 = lambda i: (0, 0)
    const3 = lambda i: (0, 0, 0)
    out_shapes = (
        jax.ShapeDtypeStruct((s, 7 * WIDTH), BF16),
        jax.ShapeDtypeStruct((4, GROUP, GROUP), F32),
        jax.ShapeDtypeStruct((1, WIDTH), F32),
        jax.ShapeDtypeStruct((1, WIDTH), F32),
        jax.ShapeDtypeStruct((1, WIDTH), F32),
        jax.ShapeDtypeStruct((N_SGU_HEADS, CHUNK, CHUNK), F32),
        jax.ShapeDtypeStruct((CHUNK, N_SGU_HEADS), F32),
        jax.ShapeDtypeStruct((1, 3 * WIDTH), F32),
        jax.ShapeDtypeStruct((MEM_ROWS, WIDTH), F32),
        jax.ShapeDtypeStruct((MEM_ROWS, WIDTH), F32),
    )
    out_specs = (
        pl.BlockSpec((t, 7 * WIDTH), lambda i: (order(i), 0)),
        pl.BlockSpec((4, GROUP, GROUP), const3),
        pl.BlockSpec((1, WIDTH), const2),
        pl.BlockSpec((1, WIDTH), const2),
        pl.BlockSpec((1, WIDTH), const2),
        pl.BlockSpec((N_SGU_HEADS, CHUNK, CHUNK), const3),
        pl.BlockSpec((CHUNK, N_SGU_HEADS), const2),
        pl.BlockSpec((1, 3 * WIDTH), const2),
        pl.BlockSpec((MEM_ROWS, WIDTH), const2),
        pl.BlockSpec((MEM_ROWS, WIDTH), const2),
    )
    in_specs = _branch_specs(t, n_tiles, order) + [
        pl.BlockSpec((N_SGU_HEADS, CHUNK, CHUNK), const3),
        pl.BlockSpec((t, 3 * WIDTH), lambda i: (order(i), 0)),
    ]
    return _call(body, "branches_bwd", (n_tiles,), in_specs, out_specs, out_shapes,
                 [pltpu.VMEM((HALO, WIDTH), F32), pltpu.VMEM((CHUNK, WIDTH), F32)],
                 [proj, proj, pool_w, pool_scale, ln_g, ln_b, sgu_w, bias_full, k, v, branch_norm, sgu_wt, dy], rider)


def _out_loss(y, w_out, x, target, g_post, tm):
    s, d = x.shape
    e_w = y.shape[1]
    n_tiles = s // tm

    def body(y_ref, w_ref, x_ref, t_ref, g_ref, loss_ref, dz_ref, dout_ref, dy_ref, dg_ref, sq_ref):
        i = pl.program_id(0)

        @pl.when(i == 0)
        def _():
            sq_ref[...] = jnp.zeros(sq_ref.shape, F32)
            dg_ref[...] = jnp.zeros(dg_ref.shape, F32)

        w = w_ref[...]
        out = _dot(y_ref[...], w, NN)
        r = lax.rsqrt(jnp.mean(out * out, axis=-1, keepdims=True) + EPS)
        outn = out * r
        g = g_ref[...]
        err = (x_ref[...] + outn * g) - t_ref[...]
        sq_ref[...] += jnp.sum(err * err, axis=0, keepdims=True)
        dz = err * (1.0 / d)
        dz_ref[...] = dz
        dg_ref[...] += jnp.sum(dz * outn, axis=0, keepdims=True)
        doutn = dz * g
        dout = (r * (doutn - outn * jnp.mean(doutn * outn, axis=-1, keepdims=True))).astype(BF16)
        dout_ref[...] = dout
        dy_ref[...] = _dot(dout, w, NT).astype(BF16)

        @pl.when(i == n_tiles - 1)
        def _():
            loss_ref[...] = 0.5 * jnp.sum(sq_ref[...], axis=1, keepdims=True) * (1.0 / d)

    row = lambda i: (i, 0)
    const2 = lambda i: (0, 0)
    return pl.pallas_call(
        body, name="out_loss", grid=(n_tiles,),
        in_specs=[
            pl.BlockSpec((tm, e_w), row),
            pl.BlockSpec((e_w, d), const2, pipeline_mode=pl.Buffered(1)),
            pl.BlockSpec((tm, d), row),
            pl.BlockSpec((tm, d), row),
            pl.BlockSpec((1, d), const2),
        ],
        out_specs=(
            pl.BlockSpec((1, 1), const2),
            pl.BlockSpec((tm, d), row),
            pl.BlockSpec((tm, d), row),
            pl.BlockSpec((tm, e_w), row),
            pl.BlockSpec((1, d), const2),
        ),
        out_shape=(
            jax.ShapeDtypeStruct((1, 1), F32),
            jax.ShapeDtypeStruct((s, d), F32),
            jax.ShapeDtypeStruct((s, d), BF16),
            jax.ShapeDtypeStruct((s, e_w), BF16),
            jax.ShapeDtypeStruct((1, d), F32),
        ),
        scratch_shapes=[pltpu.VMEM((1, d), F32)],
        compiler_params=_params(1),
    )(y, w_out, x, target, g_post)


def _dx_call(dproj, w_in, x, dz, g_pre, tm, tk, rider=None):
    s, d = x.shape
    k_total = dproj.shape[1]
    nk = k_total // tk
    n_tiles = s // tm

    def body(dp_ref, w_ref, x_ref, dz_ref, g_ref, dx_ref, dg_ref, acc_ref):
        i, kk = pl.program_id(0), pl.program_id(1)
        part = lambda: _dot(dp_ref[...], w_ref[...], NT)

        @pl.when(kk == 0)
        def _():
            acc_ref[...] = part()

        @pl.when((kk > 0) & (kk < nk - 1))
        def _():
            acc_ref[...] += part()

        @pl.when((i == 0) & (kk == 0))
        def _():
            dg_ref[...] = jnp.zeros(dg_ref.shape, F32)

        @pl.when(kk == nk - 1)
        def _():
            dh = acc_ref[...] + part()
            xv = x_ref[...]
            r = lax.rsqrt(jnp.mean(xv * xv, axis=-1, keepdims=True) + EPS)
            xhat = xv * r
            dg_ref[...] += jnp.sum(dh * xhat, axis=0, keepdims=True)
            dxhat = dh * g_ref[...]
            dx_ref[...] = dz_ref[...] + r * (dxhat - xhat * jnp.mean(dxhat * xhat, axis=-1, keepdims=True))

    row = lambda i, kk: (i, 0)
    const2 = lambda i, kk: (0, 0)
    return _call(
        body, "dx", (n_tiles, nk),
        [
            pl.BlockSpec((tm, tk), lambda i, kk: (i, kk)),
            pl.BlockSpec((d, tk), lambda i, kk: (0, kk)),
            pl.BlockSpec((tm, d), row),
            pl.BlockSpec((tm, d), row),
            pl.BlockSpec((1, d), const2),
        ],
        [pl.BlockSpec((tm, d), row), pl.BlockSpec((1, d), const2)],
        [jax.ShapeDtypeStruct((s, d), F32), jax.ShapeDtypeStruct((1, d), F32)],
        [pltpu.VMEM((tm, d), F32)], [dproj, w_in, x, dz, g_pre], rider)


def _rows_tile(rows, cols, n_arrays, itemsize=4):
    budget = 24 * 1024 * 1024 // (2 * n_arrays * cols * itemsize)
    if rows <= budget:
        return rows
    best = None
    for cand in range(16, rows + 1, 16):
        if rows % cand == 0 and cand <= max(budget, 16):
            best = cand
    return best if best is not None else rows


def _elementwise(fn, inputs, out_dtypes, name):
    rows, cols = inputs[0].shape
    tr = _rows_tile(rows, cols, len(inputs) + len(out_dtypes))
    n_in = len(inputs)

    def body(*refs):
        outs = fn(*[r[...] for r in refs[:n_in]])
        for o_ref, o in zip(refs[n_in:], outs):
            o_ref[...] = o.astype(o_ref.dtype)

    spec = pl.BlockSpec((tr, cols), lambda i: (i, 0))
    return pl.pallas_call(
        body, name=name, grid=(rows // tr,),
        in_specs=[spec] * n_in, out_specs=tuple([spec] * len(out_dtypes)),
        out_shape=tuple(jax.ShapeDtypeStruct((rows, cols), dt) for dt in out_dtypes),
        compiler_params=_params(1),
    )(*inputs)


def _blockwise(fn, pos, inputs, in_specs, out_shape, out_spec, grid, name):
    n_in = len(inputs)

    def body(pos_ref, *refs):
        o_ref = refs[n_in]
        (out,) = fn(*[r[...].reshape(o_ref.shape) for r in refs[:n_in]])
        o_ref[...] = out.astype(o_ref.dtype)

    return pl.pallas_call(
        body, name=name,
        grid_spec=pltpu.PrefetchScalarGridSpec(num_scalar_prefetch=1, grid=grid, in_specs=in_specs,
                                               out_specs=out_spec),
        out_shape=out_shape,
        compiler_params=_params(len(grid)),
    )(pos, *inputs)


def _cast_copy(x):
    return (x,)


def _pair_sum(mine, theirs):
    return ((mine.astype(F32) + theirs.astype(F32)),)


def _four_sum(own, t0, t1, t2):
    return ((((own.astype(F32) + t0.astype(F32)) + t1.astype(F32)) + t2.astype(F32)),)


def _adamw(w, g, m, v):
    m = ADAM_B1 * m + (1.0 - ADAM_B1) * g
    v = ADAM_B2 * v + (1.0 - ADAM_B2) * jnp.square(g)
    m_hat = m / (1.0 - ADAM_B1 ** ADAM_STEP)
    v_hat = v / (1.0 - ADAM_B2 ** ADAM_STEP)
    delta = -ADAM_LR * (m_hat / (jnp.sqrt(v_hat) + ADAM_EPS) + ADAM_WD * w)
    return delta, m, v


def _place():
    x, y, c = lax.axis_index("x"), lax.axis_index("y"), lax.axis_index("c")
    chips = [(1 - x, y), (x, 1 - y), (1 - x, 1 - y)]
    return x, y, c, chips


def _remote(src, dst, send_sem, recv_sem, to):
    return pltpu.make_async_remote_copy(src_ref=src, dst_ref=dst, send_sem=send_sem, recv_sem=recv_sem,
                                        device_id=to, device_id_type=MESH)


def _split(ref, plan):
    views = [ref]
    for axis, parts in plan:
        size = ref.shape[axis] // parts
        assert size * parts == ref.shape[axis]
        views = [v.at[tuple(pl.ds(q * size, size) if i == axis else slice(None) for i in range(len(ref.shape)))]
                 for v in views for q in range(parts)]
    return views


def _remote_in_parts(src, dst, send_sem, recv_sem, to, plan):
    for s, d in zip(_split(src, plan), _split(dst, plan)):
        _remote(s, d, send_sem, recv_sem, to).start()
    return _remote(src, dst, send_sem, recv_sem, to)


def _hbm_call(body, name, inputs, out_shapes, scratch, aliases=None):
    return pl.pallas_call(
        body, name=name,
        in_specs=[ANY] * len(inputs), out_specs=tuple([ANY] * len(out_shapes)), out_shape=tuple(out_shapes),
        scratch_shapes=scratch, input_output_aliases=aliases or {},
        compiler_params=pltpu.CompilerParams(has_side_effects=True),
    )(*inputs)


def _shard_half(kind, ref, chip, cc):
    if kind == 0:
        rows, cols = ref.shape[0] // 2, ref.shape[1] // 4
        return ref.at[pl.ds(cc * rows, rows), pl.ds(pl.multiple_of(chip * cols, 128), cols)]
    if kind == 3:
        rows = ref.shape[1] // 8
        return ref.at[:, pl.ds(pl.multiple_of((2 * chip + cc) * rows, 16), rows), :]
    rows = ref.shape[0] // 8
    return ref.at[pl.ds(pl.multiple_of((2 * chip + cc) * rows, 16), rows), :]


def _relay_rider(full, kind):
    def quarter(ref, chip_no, cc, q):
        return _split(_shard_half(kind, ref, chip_no, cc), [(0, 2)])[q]

    def run(in_refs, full_refs, send_sems, recv_sems, start):
        (ref,) = full_refs
        x, y, c, chips = _place()
        sibling = (x, y, 1 - c)
        chip_no = [2 * ch[0] + ch[1] for ch in chips]
        if start:
            for p in (0, 1):
                held = quarter(ref, chip_no[1 - p], c, p)
                _remote(held, held, send_sems.at[p], recv_sems.at[p], (*chips[p], c)).start()
            return
        for p in (0, 1):
            landed = quarter(ref, chip_no[2], c, p)
            _remote(landed, landed, send_sems.at[p], recv_sems.at[p], (*chips[p], c)).wait_recv()
            _remote(landed, landed, send_sems.at[2], recv_sems.at[2], sibling).start()
        mine, theirs = _shard_half(kind, ref, chip_no[2], c), _shard_half(kind, ref, chip_no[2], 1 - c)
        _remote(mine, mine, send_sems.at[2], recv_sems.at[2], sibling).wait_send()
        _remote(theirs, theirs, send_sems.at[2], recv_sems.at[2], sibling).wait_recv()
        for p in (0, 1):
            held = quarter(ref, chip_no[1 - p], c, p)
            _remote(held, held, send_sems.at[p], recv_sems.at[p], (*chips[p], c)).wait_send()

    return _Rider([full], [jax.ShapeDtypeStruct(full.shape, full.dtype)], 3, run, aliases={0: 0})


def _riders(riders):
    def bounds(counts):
        ends = [sum(counts[:i + 1]) for i in range(len(counts))]
        return list(zip([0] + ends[:-1], ends))

    ins = bounds([len(r.inputs) for r in riders])
    outs = bounds([len(r.out_shapes) for r in riders])
    sems = bounds([r.n_sems for r in riders])

    class From:
        def __init__(self, sem_refs, base):
            self.sem_refs, self.base, self.at = sem_refs, base, self

        def __getitem__(self, k):
            return self.sem_refs.at[self.base + k]

    def run(in_refs, out_refs, send_sems, recv_sems, start):
        for r, (i0, i1), (o0, o1), (s0, _) in zip(riders, ins, outs, sems):
            r.run(in_refs[i0:i1], out_refs[o0:o1], From(send_sems, s0), From(recv_sems, s0), start)

    aliases = {}
    for r, (i0, _), (o0, _) in zip(riders, ins, outs):
        aliases.update({i0 + i: o0 + o for i, o in r.aliases.items()})
    return _Rider([a for r in riders for a in r.inputs], [o for r in riders for o in r.out_shapes],
                  sems[-1][1], run, aliases)


def _gather_rider(fulls, kinds, peers=(0, 1, 2)):
    n = len(fulls)
    full_half = _shard_half

    def run(in_refs, full_refs, send_sems, recv_sems, start):
        x, y, c, chips = _place()
        me = 2 * x + y
        sibling = (x, y, 1 - c)
        plans = [[(0, MAX_PARTS)], [(0, 2)], [(0, 2)], []]
        chips = [(p, chips[p]) for p in peers]
        across = lambda a, p, k: (3 * a + p) * MAX_PARTS + k
        onward = lambda a, p: 3 * n * MAX_PARTS + 3 * a + p

        def parts(a, chip_no, cc):
            return _split(full_half(kinds[a], full_refs[a], chip_no, cc), plans[kinds[a]])

        if start:
            for p, chip in chips:
                for a in range(n):
                    for k, mine in enumerate(parts(a, me, c)):
                        _remote(mine, mine, send_sems.at[across(a, p, k)], recv_sems.at[across(a, p, k)],
                                (*chip, c)).start()
            return
        for k in range(MAX_PARTS):
            for p, chip in chips:
                for a in range(n):
                    landed = parts(a, 2 * chip[0] + chip[1], c)
                    if k < len(landed):
                        _remote(landed[k], landed[k], send_sems.at[across(a, p, k)], recv_sems.at[across(a, p, k)],
                                (*chip, c)).wait_recv()
                        _remote(landed[k], landed[k], send_sems.at[onward(a, p)], recv_sems.at[onward(a, p)],
                                sibling).start()
        for p, chip in chips:
            them = 2 * chip[0] + chip[1]
            for a in range(n):
                passed = full_half(kinds[a], full_refs[a], them, 1 - c)
                _remote(passed, passed, send_sems.at[onward(a, p)], recv_sems.at[onward(a, p)], sibling).wait_recv()
                landed = full_half(kinds[a], full_refs[a], them, c)
                _remote(landed, landed, send_sems.at[onward(a, p)], recv_sems.at[onward(a, p)], sibling).wait_send()
                for k, mine in enumerate(parts(a, me, c)):
                    _remote(mine, mine, send_sems.at[across(a, p, k)], recv_sems.at[across(a, p, k)],
                            (*chip, c)).wait_send()

    return _Rider(fulls, [jax.ShapeDtypeStruct(f.shape, f.dtype) for f in fulls], 3 * n * (MAX_PARTS + 1), run,
                  aliases={a: a for a in range(n)})


def _exchange_halves(grads, name):
    n = len(grads)
    arrays = [g for g, _, _ in grads]
    out_shapes = [jax.ShapeDtypeStruct(tuple(1 if i == ax else dim for i, dim in enumerate(g.shape)), g.dtype)
                  for g, ax, _ in grads]

    def half(ref, ax, cc):
        idx = tuple(pl.ds(cc, 1) if i == ax else slice(None) for i in range(len(ref.shape)))
        return ref.at[idx]

    def body(*refs):
        in_refs, out_refs = refs[:n], refs[n:2 * n]
        send_sems, recv_sems = refs[2 * n:]
        x, y, c, _ = _place()
        sibling = (x, y, 1 - c)
        copies = [_remote_in_parts(half(in_refs[a], grads[a][1], 1 - c), out_refs[a], send_sems.at[a],
                                   recv_sems.at[a], sibling, grads[a][2]) for a in range(n)]
        for rem in copies:
            rem.wait()

    return _hbm_call(body, name, arrays, out_shapes,
                     [pltpu.SemaphoreType.DMA((n,)), pltpu.SemaphoreType.DMA((n,))])


def _exchange_rider(arrays, plans):
    n = len(arrays)

    def run(in_refs, out_refs, send_sems, recv_sems, start):
        x, y, c, _ = _place()
        sibling = (x, y, 1 - c)
        for a in range(n):
            sems = (send_sems.at[a], recv_sems.at[a])
            if start:
                _remote_in_parts(in_refs[a], out_refs[a], *sems, sibling, plans[a])
            else:
                _remote(in_refs[a], out_refs[a], *sems, sibling).wait()

    return _Rider(arrays, [jax.ShapeDtypeStruct(a.shape, a.dtype) for a in arrays], n, run)


def _scatter_rider(parts):
    n = len(parts)
    arrays = [p for p, _, _ in parts]

    def block_shape(p, ax):
        if ax == len(p.shape) - 1:
            return p.shape[:-1] + (p.shape[-1] // 4,)
        return tuple(1 if i == ax else dim for i, dim in enumerate(p.shape))

    out_shapes = [jax.ShapeDtypeStruct((3,) + block_shape(p, ax), p.dtype) for p, ax, _ in parts]

    def block(ref, ax, chip):
        rank = len(ref.shape)
        if ax == rank - 1:
            cols = ref.shape[-1] // 4
            last = pl.ds(pl.multiple_of(chip * cols, 128), cols)
            return ref.at[tuple([slice(None)] * (rank - 1) + [last])]
        return ref.at[tuple(pl.ds(chip, 1) if i == ax else slice(None) for i in range(rank))]

    def run(in_refs, out_refs, send_sems, recv_sems, start):
        x, y, c, chips = _place()
        for a in range(n):
            ax, plan = parts[a][1], parts[a][2]
            for p, chip in enumerate(chips):
                src, dst = block(in_refs[a], ax, 2 * chip[0] + chip[1]), out_refs[a].at[p]
                sems = (send_sems.at[3 * a + p], recv_sems.at[3 * a + p])
                if start:
                    _remote_in_parts(src, dst, *sems, (*chip, c), plan)
                else:
                    _remote(src, dst, *sems, (*chip, c)).wait()

    return _Rider(arrays, out_shapes, 3 * n, run)


def _join_halves(joined):
    n = len(joined)
    arrays = [j for j, _, _ in joined]

    def body(*refs):
        out_refs = refs[n:2 * n]
        send_sems, recv_sems = refs[2 * n:]
        x, y, c, _ = _place()
        sibling = (x, y, 1 - c)

        def half(a, cc):
            rank = len(out_refs[a].shape)
            return out_refs[a].at[tuple(pl.ds(cc, 1) if i == joined[a][1] else slice(None) for i in range(rank))]

        sends = [_remote_in_parts(half(a, c), half(a, c), send_sems.at[a], recv_sems.at[a], sibling, joined[a][2])
                 for a in range(n)]
        for a, rem in enumerate(sends):
            rem.wait_send()
            _remote(half(a, 1 - c), half(a, 1 - c), send_sems.at[a], recv_sems.at[a], sibling).wait_recv()

    return _hbm_call(body, "join_halves", arrays, [jax.ShapeDtypeStruct(j.shape, j.dtype) for j in arrays],
                     [pltpu.SemaphoreType.DMA((n,)), pltpu.SemaphoreType.DMA((n,))],
                     aliases={a: a for a in range(n)})


def _allreduce_small(packed):
    rows, lanes = packed.shape
    half = rows // 2

    def body(in_ref, out_ref, pair_ref, gath_ref, send_sems, recv_sems):
        x, y, c, chips = _place()
        me = 2 * x + y
        sibling = (x, y, 1 - c)
        mine = pl.ds(pl.multiple_of(c * half, 8), half)
        theirs = pl.ds(pl.multiple_of((1 - c) * half, 8), half)
        to_sib = _remote(in_ref.at[theirs], pair_ref, send_sems.at[0], recv_sems.at[0], sibling)
        to_sib.start()
        to_sib.wait()
        gath_ref[me] = in_ref[mine] + pair_ref[...]
        sends = [_remote(gath_ref.at[me], gath_ref.at[me], send_sems.at[1 + p], recv_sems.at[1 + p], (*chip, c))
                 for p, chip in enumerate(chips)]
        for cp in sends:
            cp.start()
        for p, chip in enumerate(chips):
            slot = gath_ref.at[2 * chip[0] + chip[1]]
            _remote(slot, slot, send_sems.at[1 + p], recv_sems.at[1 + p], (*chip, c)).wait_recv()
        for cp in sends:
            cp.wait_send()
        out_ref[mine] = ((gath_ref[0] + gath_ref[1]) + gath_ref[2]) + gath_ref[3]
        back = _remote(out_ref.at[mine], out_ref.at[mine], send_sems.at[4], recv_sems.at[4], sibling)
        back.start()
        back.wait_send()
        _remote(out_ref.at[theirs], out_ref.at[theirs], send_sems.at[4], recv_sems.at[4], sibling).wait_recv()

    vmem = pl.BlockSpec(memory_space=pltpu.VMEM)
    return pl.pallas_call(
        body, name="allreduce_small",
        in_specs=[vmem], out_specs=vmem, out_shape=jax.ShapeDtypeStruct((rows, lanes), F32),
        scratch_shapes=[pltpu.VMEM((half, lanes), F32), pltpu.VMEM((4, half, lanes), F32),
                        pltpu.SemaphoreType.DMA((5,)), pltpu.SemaphoreType.DMA((5,))],
        compiler_params=pltpu.CompilerParams(has_side_effects=True, vmem_limit_bytes=32 * 1024 * 1024),
    )(packed)


SMALL = ("norm_pre", "pool_scale", "sgu_ln_g", "sgu_ln_b", "sgu_w", "sgu_b", "mem_norm", "branch_norm", "norm_post")
LARGE = ("w_in", "pool_w", "w_kv", "w_out")
ORDER = ("norm_pre", "w_in", "pool_w", "pool_scale", "sgu_ln_g", "sgu_ln_b", "sgu_w", "sgu_b", "mem_norm", "w_kv",
         "branch_norm", "w_out", "norm_post")


def _pack(arrays, extra=()):
    rows = [a.reshape(-1, 128) for a in arrays] + list(extra)
    pad = -sum(r.shape[0] for r in rows) % 16
    return jnp.concatenate(rows + ([jnp.zeros((pad, 128), F32)] if pad else []), axis=0)


def _unpack(packed, like):
    out, row = [], 0
    for a in like:
        rows = a.size // 128
        out.append(packed[row:row + rows].reshape(a.shape))
        row += rows
    return out


def kernel(x, mem, norm_pre, w_in, pool_w, pool_scale, sgu_ln_g, sgu_ln_b, sgu_w, sgu_b, mem_norm, w_kv, branch_norm, w_out, norm_post, loss_target, m_norm_pre, m_w_in, m_pool_w, m_pool_scale, m_sgu_ln_g, m_sgu_ln_b, m_sgu_w, m_sgu_b, m_mem_norm, m_w_kv, m_branch_norm, m_w_out, m_norm_post, v_norm_pre, v_w_in, v_pool_w, v_pool_scale, v_sgu_ln_g, v_sgu_ln_b, v_sgu_w, v_sgu_b, v_mem_norm, v_w_kv, v_branch_norm, v_w_out, v_norm_post):
    weights = dict(norm_pre=norm_pre, w_in=w_in, pool_w=pool_w, pool_scale=pool_scale, sgu_ln_g=sgu_ln_g,
                   sgu_ln_b=sgu_ln_b, sgu_w=sgu_w, sgu_b=sgu_b, mem_norm=mem_norm, w_kv=w_kv, branch_norm=branch_norm,
                   w_out=w_out, norm_post=norm_post)
    mom1 = dict(norm_pre=m_norm_pre, w_in=m_w_in, pool_w=m_pool_w, pool_scale=m_pool_scale, sgu_ln_g=m_sgu_ln_g,
                sgu_ln_b=m_sgu_ln_b, sgu_w=m_sgu_w, sgu_b=m_sgu_b, mem_norm=m_mem_norm, w_kv=m_w_kv,
                branch_norm=m_branch_norm, w_out=m_w_out, norm_post=m_norm_post)
    mom2 = dict(norm_pre=v_norm_pre, w_in=v_w_in, pool_w=v_pool_w, pool_scale=v_pool_scale, sgu_ln_g=v_sgu_ln_g,
                sgu_ln_b=v_sgu_ln_b, sgu_w=v_sgu_w, sgu_b=v_sgu_b, mem_norm=v_mem_norm, w_kv=v_w_kv,
                branch_norm=v_branch_norm, w_out=v_w_out, norm_post=v_norm_post)

    s, d = x.shape[1], x.shape[2]
    x2, mem2, tgt2 = x[0], mem[0], loss_target[0]
    t_branch = min(256, s)
    tm = min(512, s)

    core = lax.axis_index("c")
    chip = 2 * lax.axis_index("x") + lax.axis_index("y")
    pos = jnp.stack([core, chip]).astype(jnp.int32)
    n_in, n_kv, n_out = 4 * w_in.shape[2], 4 * w_kv.shape[1], 4 * w_out.shape[1]
    wi_rows, kv_rows, wo_rows = d // 8, n_kv // 8, n_out // 8

    def placed(shard, full_shape, block, grid, in_map, out_map, name):
        return _blockwise(_cast_copy, pos, [shard], [pl.BlockSpec(block, in_map)],
                          jax.ShapeDtypeStruct(full_shape, BF16), pl.BlockSpec(block, out_map), grid, name)

    kv_cols, pw_rows = w_kv.shape[2], GROUP // 8
    wi_own = placed(w_in[0], (d, n_in), (wi_rows, n_in // 4), (8,), lambda i, p: (i, 0), lambda i, p: (i, p[1]),
                    "place_w_in")
    wkv_own = placed(w_kv[0], (n_kv, kv_cols), (kv_rows, kv_cols), (2,), lambda i, p: (i, 0),
                     lambda i, p: (2 * p[1] + i, 0), "place_w_kv")
    wo_own = placed(w_out[0], (n_out, d), (wo_rows, d), (2,), lambda i, p: (i, 0), lambda i, p: (2 * p[1] + i, 0),
                    "place_w_out")
    pw_own = placed(pool_w[0], (4, GROUP, GROUP), (4, GROUP // 4, GROUP), (1,), lambda i, p: (0, 0, 0),
                    lambda i, p: (0, p[1], 0), "place_pool_w")

    x_pos, y_pos = lax.axis_index("x"), lax.axis_index("y")
    chips = jnp.stack([chip, 2 * (1 - x_pos) + y_pos, 2 * x_pos + 1 - y_pos,
                       2 * (1 - x_pos) + 1 - y_pos]).astype(jnp.int32)
    mem_g = mem_norm.reshape(1, d)
    proj, h, h_t, wi_full = _proj_piece(chips, 0, 1, x2, norm_pre, None, None, n_in,
                                   _gather_rider([wi_own], [0], peers=(0, 1)), tm, "proj_own")
    proj, wi_full, wkv_part, pw_full = _proj_piece(
        chips, 1, 2, h, None, None, proj, n_in,
        _riders([_relay_rider(wi_full, 0), _gather_rider([wkv_own], [1], peers=(0, 1)),
                 _gather_rider([pw_own], [3])]), tm, "proj_neighbours")
    proj, wkv_full, wo_part = _proj_piece(
        chips, 3, 1, h, None, wi_full, proj, n_in,
        _riders([_relay_rider(wkv_part, 1), _gather_rider([wo_own], [2], peers=(0, 1))]), tm, "proj_diagonal")
    k_m, v_m = _kv_fwd(mem2, mem_g, wkv_full)
    bias_full = jnp.repeat(sgu_b[0].T, CHUNK, axis=1)
    y, y_t, wo_full = _branches_fwd(proj, pw_full, pool_scale, sgu_ln_g, sgu_ln_b, sgu_w[0], bias_full, k_m, v_m,
                                    branch_norm, t_branch, _relay_rider(wo_part, 2))
    loss_local, dz, dout, dy, g_norm_post = _out_loss(y, wo_full, x2, tgt2, norm_post, min(256, s))

    def pair_sums(views):
        theirs = _exchange_halves([(v[0], v[1], v[2]) for v in views], "exchange_for_" + views[0][9])
        return [_blockwise(_pair_sum, pos, [v[0], th], [pl.BlockSpec(v[3], v[4][0]), pl.BlockSpec(v[3], v[4][1])],
                           jax.ShapeDtypeStruct(v[5], BF16), pl.BlockSpec(v[6], v[7]), v[8], v[9])
                for v, th in zip(views, theirs)]

    tk = min(1024, s)
    (g_wo,) = _grad_rows(y_t, dout, pos, lambda i, p: i, n_out, n_out // 2, 1024, tk, "grad_w_out")
    (ps_wo,) = pair_sums([
        (g_wo.reshape(4, 2, wo_rows, d), 1, [(0, 4), (2, 2)], (1, 1, wo_rows, d),
         (lambda i, p: (i, p[0], 0, 0), lambda i, p: (i, 0, 0, 0)), (4, wo_rows, d), (1, wo_rows, d),
         lambda i, p: (i, 0, 0), (4,), "pair_sum_w_out")])
    (dproj, g_pw, g_pool_scale, g_ln_g, g_ln_b, g_sgu_w, g_sgu_b_t, g_branch_norm, dk, dv, landed_wo) = _branches_bwd(
        proj, dy, pw_full, pool_scale, sgu_ln_g, sgu_ln_b, sgu_w[0], jnp.swapaxes(sgu_w[0], 1, 2), bias_full,
        k_m, v_m, branch_norm, t_branch, _scatter_rider([(ps_wo, 0, [(1, 2)])]))
    g_wkv, g_mem_norm = _kv_bwd(mem2, mem_g, wkv_full, dk, dv)
    ps_kv, ps_pw = pair_sums([
        (g_wkv.reshape(4, 2, kv_rows, kv_cols), 1, [(0, 4), (2, 2)], (1, 1, kv_rows, kv_cols),
         (lambda i, p: (i, p[0], 0, 0), lambda i, p: (i, 0, 0, 0)), (4, kv_rows, kv_cols), (1, kv_rows, kv_cols),
         lambda i, p: (i, 0, 0), (4,), "pair_sum_w_kv"),
        (g_pw.astype(BF16).reshape(4, 4, 2, pw_rows, GROUP), 2, [(0, 4)], (1, 4, 1, pw_rows, GROUP),
         (lambda i, p: (i, 0, p[0], 0, 0), lambda i, p: (i, 0, 0, 0, 0)), (4, 4, pw_rows, GROUP),
         (1, 4, pw_rows, GROUP), lambda i, p: (i, 0, 0, 0), (4,), "pair_sum_pool_w")])
    gwi_theirs, landed_kv, landed_pw = _grad_rows(
        h_t, dproj, pos, lambda i, p: 1 - p[0], d // 2, d // 2, n_in // 4, tk, "grad_w_in_sibling_half",
        _scatter_rider([(ps_kv, 0, [(1, 2)]), (ps_pw, 1, [])]))
    gwi_mine, gwi_from_sibling = _grad_rows(h_t, dproj, pos, lambda i, p: p[0], d // 2, d // 2, n_in // 4, tk,
                                            "grad_w_in_own_half", _exchange_rider([gwi_theirs], [[(0, 16)]]))
    ps_wi = _elementwise(_pair_sum, [gwi_mine, gwi_from_sibling], [BF16], "pair_sum_w_in")[0]
    grad_x, g_norm_pre, landed_wi = _dx_call(dproj, wi_full, x2, dz, norm_pre, tm, 1024,
                                             _scatter_rider([(ps_wi, 1, [(0, 4)])]))
    psum = [ps_wi, ps_kv, ps_wo, ps_pw]
    landed = [landed_wi, landed_kv, landed_wo, landed_pw]
    from_chip = lambda spec_shape, rank: [
        pl.BlockSpec(spec_shape, functools.partial(lambda i, p, q: (q, i) + (0,) * (rank - 2), q=q))
        for q in range(3)]
    joined = _join_halves([
        (_blockwise(_four_sum, pos, [psum[0]] + [landed[0]] * 3,
                    [pl.BlockSpec((256, n_in // 4), lambda i, p: (i, p[1]))] + from_chip((1, 256, n_in // 4), 3),
                    jax.ShapeDtypeStruct((2, d // 2, n_in // 4), F32),
                    pl.BlockSpec((1, 256, n_in // 4), lambda i, p: (p[0], i, 0)), (d // 2 // 256,), "chip_sum_w_in"),
         0, [(1, 8)]),
        (_blockwise(_four_sum, pos, [psum[1]] + [landed[1]] * 3,
                    [pl.BlockSpec((1, kv_rows, kv_cols), lambda i, p: (p[1], 0, 0))]
                    + from_chip((1, 1, kv_rows, kv_cols), 4),
                    jax.ShapeDtypeStruct((2, kv_rows, kv_cols), F32),
                    pl.BlockSpec((1, kv_rows, kv_cols), lambda i, p: (p[0], 0, 0)), (1,), "chip_sum_w_kv"),
         0, [(1, 2)]),
        (_blockwise(_four_sum, pos, [psum[2]] + [landed[2]] * 3,
                    [pl.BlockSpec((1, wo_rows, d), lambda i, p: (p[1], 0, 0))] + from_chip((1, 1, wo_rows, d), 4),
                    jax.ShapeDtypeStruct((2, wo_rows, d), F32),
                    pl.BlockSpec((1, wo_rows, d), lambda i, p: (p[0], 0, 0)), (1,), "chip_sum_w_out"),
         0, [(1, 2)]),
        (_blockwise(_four_sum, pos, [psum[3]] + [landed[3]] * 3,
                    [pl.BlockSpec((4, 1, pw_rows, GROUP), lambda i, p: (0, p[1], 0, 0))]
                    + from_chip((1, 4, 1, pw_rows, GROUP), 5),
                    jax.ShapeDtypeStruct((4, 2, pw_rows, GROUP), F32),
                    pl.BlockSpec((4, 1, pw_rows, GROUP), lambda i, p: (0, p[0], 0, 0)), (1,), "chip_sum_pool_w"),
         1, []),
    ])
    grads = {"w_in": joined[0].reshape(w_in.shape), "w_kv": joined[1].reshape(w_kv.shape),
             "w_out": joined[2].reshape(w_out.shape), "pool_w": joined[3].reshape(pool_w.shape)}

    small_local = dict(norm_pre=g_norm_pre, pool_scale=g_pool_scale, sgu_ln_g=g_ln_g, sgu_ln_b=g_ln_b,
                       sgu_w=g_sgu_w, sgu_b=g_sgu_b_t.T, mem_norm=g_mem_norm, branch_norm=g_branch_norm,
                       norm_post=g_norm_post)
    small_rows = sum(weights[n].size for n in SMALL) // 128
    small_sum = _allreduce_small(_pack([small_local[n] for n in SMALL], [jnp.pad(loss_local, ((0, 7), (0, 127)))]))
    for n, g in zip(SMALL, _unpack(small_sum, [weights[n] for n in SMALL])):
        grads[n] = g
    loss = small_sum[small_rows, 0]

    delta, new_m, new_v = {}, {}, {}
    packed = [small_sum if src is grads else _pack([src[n] for n in SMALL]) for src in (weights, grads, mom1, mom2)]
    outs = _elementwise(_adamw, packed, [F32, F32, F32], "adamw_small")
    for dst, o in zip((delta, new_m, new_v), outs):
        for n, a in zip(SMALL, _unpack(o, [weights[n] for n in SMALL])):
            dst[n] = a
    for n in LARGE:
        cols = weights[n].shape[-1]
        outs = _elementwise(_adamw, [src[n].reshape(-1, cols) for src in (weights, grads, mom1, mom2)],
                            [F32, F32, F32], "adamw_" + n)
        for dst, o in zip((delta, new_m, new_v), outs):
            dst[n] = o.reshape(weights[n].shape)

    return (loss, grad_x[None], *[grads[n] for n in ORDER], *[delta[n] for n in ORDER],
            *[new_m[n] for n in ORDER], *[new_v[n] for n in ORDER])
```

```python
import functools

import jax
import jax.numpy as jnp
from jax import lax
from jax.experimental import pallas as pl
from jax.experimental.pallas import tpu as pltpu

F32 = jnp.float32
BF16 = jnp.bfloat16
EPS = 1e-6
MESH = pl.DeviceIdType.MESH
ANY = pl.BlockSpec(memory_space=pl.ANY)

POOL_WINDOWS = (2, 4, 8, 16)
GROUP = 256
HALO = 16
CHUNK = 128
N_SGU_HEADS = 8
N_ATT_HEADS = 4
ATT_DIM = 256
WIDTH = 1024
ATT_SCALE = 1.0 / 16.0

ADAM_LR = 0.001
ADAM_B1 = 0.9
ADAM_B2 = 0.999
ADAM_EPS = 1e-08
ADAM_WD = 0.01
ADAM_STEP = 10

VMEM_LIMIT = 60 * 1024 * 1024
ELEMENTWISE_VMEM = 24 * 1024 * 1024
LANES = 128
BF16_ROWS = 16
MAX_PARTS = 4


def _params(n_grid_axes, vmem=VMEM_LIMIT):
    return pltpu.CompilerParams(dimension_semantics=("arbitrary",) * n_grid_axes, vmem_limit_bytes=vmem)


def _dot(a, b, dims):
    return lax.dot_general(a, b, (dims, ((), ())), preferred_element_type=F32)


NN = ((1,), (0,))
NT = ((1,), (1,))
TN = ((0,), (0,))


class _Rider:
    def __init__(self, inputs, out_shapes, n_sems, run, aliases=None):
        self.inputs, self.out_shapes, self.n_sems, self.run = list(inputs), list(out_shapes), n_sems, run
        self.aliases = aliases or {}


def _call(body, name, grid, in_specs, out_specs, out_shape, scratch_shapes, inputs, rider=None, prefetch=None,
          aliases=None, rider_refs=False):
    n_in, n_out, n_scr = len(in_specs), len(out_specs), len(scratch_shapes)
    r_in = len(rider.inputs) if rider else 0
    r_out = len(rider.out_shapes) if rider else 0
    n_pre = 0 if prefetch is None else 1

    def whole_body(*refs):
        pre, refs = refs[:n_pre], refs[n_pre:]
        ins, rider_ins = refs[:n_in], refs[n_in:n_in + r_in]
        refs = refs[n_in + r_in:]
        outs, rider_outs = refs[:n_out], refs[n_out:n_out + r_out]
        refs = refs[n_out + r_out:]
        scratch, sems = refs[:n_scr], refs[n_scr:]
        extra = {"rider_outs": rider_outs} if rider_refs else {}
        if rider is None:
            body(*pre, *ins, *outs, *scratch, **extra)
            return
        ids = [pl.program_id(ax) for ax in range(len(grid))]
        first = functools.reduce(lambda p, q: p & q, [i == 0 for i in ids])
        last = functools.reduce(lambda p, q: p & q, [i == g - 1 for i, g in zip(ids, grid)])

        @pl.when(first)
        def _():
            rider.run(rider_ins, rider_outs, *sems, True)

        body(*pre, *ins, *outs, *scratch, **extra)

        @pl.when(last)
        def _():
            rider.run(rider_ins, rider_outs, *sems, False)

    io_aliases = {n_pre + i: o for i, o in (aliases or {}).items()}
    scratch_all = list(scratch_shapes)
    if rider:
        io_aliases.update({n_pre + n_in + i: n_out + o for i, o in rider.aliases.items()})
        scratch_all += [pltpu.SemaphoreType.DMA((rider.n_sems,)), pltpu.SemaphoreType.DMA((rider.n_sems,))]
    specs = dict(grid=grid, in_specs=list(in_specs) + [ANY] * r_in, out_specs=tuple(out_specs) + (ANY,) * r_out,
                 scratch_shapes=scratch_all)
    if n_pre:
        specs = dict(grid_spec=pltpu.PrefetchScalarGridSpec(num_scalar_prefetch=1, **specs))
    outs = pl.pallas_call(
        whole_body, name=name, **specs,
        out_shape=tuple(out_shape) + tuple(rider.out_shapes if rider else ()),
        input_output_aliases=io_aliases, compiler_params=_params(len(grid)),
    )(*([prefetch] if n_pre else []), *inputs, *(rider.inputs if rider else []))
    return tuple(outs)


def _grad_rows(a_t, b, pos, row_of, m, tm, tn, tk, name, rider=None):
    k, n = a_t.shape[1], b.shape[1]
    nk = k // tk
    out_dtype, dims, a = BF16, NN, a_t
    a_spec = pl.BlockSpec((tm, tk), lambda i, j, kk, p: (row_of(i, p), kk))
    b_spec = pl.BlockSpec((tk, tn), lambda i, j, kk, p: (kk, j))

    def body(pos_ref, a_ref, b_ref, o_ref, *acc):
        part = lambda: _dot(a_ref[...], b_ref[...], dims)
        if nk == 1:
            o_ref[...] = part().astype(out_dtype)
            return
        (acc_ref,) = acc
        kk = pl.program_id(2)

        @pl.when(kk == 0)
        def _():
            acc_ref[...] = part()

        @pl.when((kk > 0) & (kk < nk - 1))
        def _():
            acc_ref[...] += part()

        @pl.when(kk == nk - 1)
        def _():
            o_ref[...] = (acc_ref[...] + part()).astype(out_dtype)

    return _call(body, name, (m // tm, n // tn, nk), [a_spec, b_spec],
                 [pl.BlockSpec((tm, tn), lambda i, j, kk, p: (i, j))], [jax.ShapeDtypeStruct((m, n), out_dtype)],
                 [pltpu.VMEM((tm, tn), F32)] if nk > 1 else [], [a, b], rider, prefetch=pos)


def _proj_piece(chips, first, n_shards, src, g_pre, w_in, proj_in, n_cols, rider, tm, name):
    s, d = src.shape
    cols = n_cols // 4
    fused = g_pre is not None

    def body(chips_ref, *refs, rider_outs=()):
        refs = list(refs)
        src_ref = refs.pop(0)
        g_ref = refs.pop(0) if fused else None
        w_ref = refs.pop(0) if w_in is not None else rider_outs[0]
        if proj_in is not None:
            refs.pop(0)
        proj_ref = refs.pop(0)
        h_ref, ht_ref = (refs.pop(0), refs.pop(0)) if fused else (None, None)
        wbuf, sem = refs
        q, i = pl.program_id(0), pl.program_id(1)

        @pl.when(i == 0)
        def _():
            at = pl.multiple_of(chips_ref[first + q] * cols, LANES)
            cp = pltpu.make_async_copy(w_ref.at[:, pl.ds(at, cols)], wbuf, sem)
            cp.start()
            cp.wait()

        if fused:
            xv = src_ref[...]
            r = lax.rsqrt(jnp.mean(xv * xv, axis=-1, keepdims=True) + EPS)
            h = (xv * r * g_ref[...]).astype(BF16)
            h_ref[...] = h
            ht_ref[...] = h.T
        else:
            h = src_ref[...]
        proj_ref[...] = _dot(h, wbuf[...], NN)

    row = lambda q, i, ch: (i, 0)
    inputs, in_specs = [src], [pl.BlockSpec((tm, d), row)]
    if fused:
        inputs.append(g_pre)
        in_specs.append(pl.BlockSpec((1, d), lambda q, i, ch: (0, 0)))
    if w_in is not None:
        inputs.append(w_in)
        in_specs.append(ANY)
    aliases = {}
    if proj_in is not None:
        aliases[len(inputs)] = 0
        inputs.append(proj_in)
        in_specs.append(ANY)
    out_specs = [pl.BlockSpec((tm, cols), lambda q, i, ch: (i, ch[first + q]))]
    out_shape = [jax.ShapeDtypeStruct((s, n_cols), F32)]
    if fused:
        assert n_shards == 1
        out_specs += [pl.BlockSpec((tm, d), row), pl.BlockSpec((d, tm), lambda q, i, ch: (0, i))]
        out_shape += [jax.ShapeDtypeStruct((s, d), BF16), jax.ShapeDtypeStruct((d, s), BF16)]
    return _call(body, name, (n_shards, s // tm), in_specs, out_specs, out_shape,
                 [pltpu.VMEM((d, cols), BF16), pltpu.SemaphoreType.DMA(())], inputs, rider, prefetch=chips,
                 aliases=aliases, rider_refs=True)


def _kv_fwd(mem, g, w_kv):
    m, d = mem.shape

    def body(mem_ref, g_ref, w_ref, k_ref, v_ref):
        mv = mem_ref[...]
        r = lax.rsqrt(jnp.mean(mv * mv, axis=-1, keepdims=True) + EPS)
        mem_n = (mv * r * g_ref[...]).astype(BF16)
        kv = _dot(mem_n, w_ref[...], NN)
        k_ref[...] = kv[:, :WIDTH].astype(BF16)
        v_ref[...] = kv[:, WIDTH:].astype(BF16)

    return pl.pallas_call(
        body, name="kv_fwd",
        out_shape=(jax.ShapeDtypeStruct((m, WIDTH), BF16), jax.ShapeDtypeStruct((m, WIDTH), BF16)),
        compiler_params=_params(0),
    )(mem, g, w_kv)


def _kv_bwd(mem, g, w_kv, dk, dv):
    m, d = mem.shape
    n = w_kv.shape[1]
    col = 512

    def body(mem_ref, g_ref, w_ref, dk_ref, dv_ref, dw_ref, dg_ref):
        mv = mem_ref[...]
        r = lax.rsqrt(jnp.mean(mv * mv, axis=-1, keepdims=True) + EPS)
        mem_hat = mv * r
        mem_n = (mem_hat * g_ref[...]).astype(BF16)
        dkv = jnp.concatenate([dk_ref[...], dv_ref[...]], axis=1).astype(BF16)
        for j in range(n // col):
            dw_ref[:, j * col:(j + 1) * col] = _dot(mem_n, dkv[:, j * col:(j + 1) * col], TN).astype(BF16)
        dmem_n = _dot(dkv, w_ref[...], NT)
        dg_ref[...] = jnp.sum(dmem_n * mem_hat, axis=0, keepdims=True)

    return pl.pallas_call(
        body, name="kv_bwd",
        out_shape=(jax.ShapeDtypeStruct((d, n), BF16), jax.ShapeDtypeStruct((1, d), F32)),
        compiler_params=_params(0),
    )(mem, g, w_kv, dk, dv)


def _sigmoid(x):
    return 1.0 / (1.0 + jnp.exp(-x))


def _inv_counts(t0, t):
    pos = (t0 + lax.broadcasted_iota(jnp.int32, (t, 1), 0) + 1).astype(F32)
    return [1.0 / jnp.minimum(pos, float(w)) for w in POOL_WINDOWS]


def _window_sums(ext, t, backward):
    n = t + HALO
    parts = []
    for gi, w in enumerate(POOL_WINDOWS):
        s = ext[:, gi * GROUP:(gi + 1) * GROUP]
        k = 1
        while k < w:
            s = s + pltpu.roll(s, (n - k) if backward else k, axis=0)
            k *= 2
        parts.append(s[:t] if backward else s[HALO:])
    return parts


def _pool_fwd(xa, halo, inv, pool_w):
    t = xa.shape[0]
    sums = _window_sums(jnp.concatenate([halo, xa], axis=0), t, backward=False)
    d = jnp.concatenate([sums[gi] * inv[gi] - xa[:, gi * GROUP:(gi + 1) * GROUP] for gi in range(4)], axis=1)
    d = d.astype(BF16)
    y = jnp.concatenate([_dot(d[:, gi * GROUP:(gi + 1) * GROUP], pool_w[gi], NN) for gi in range(4)], axis=1)
    return d, y


def _layernorm_fwd(v):
    mu = jnp.mean(v, axis=-1, keepdims=True)
    xc = v - mu
    rstd = lax.rsqrt(jnp.mean(xc * xc, axis=-1, keepdims=True) + EPS)
    return xc * rstd, rstd


def _tril_mask(transposed):
    r = lax.broadcasted_iota(jnp.int32, (CHUNK, CHUNK), 0)
    c = lax.broadcasted_iota(jnp.int32, (CHUNK, CHUNK), 1)
    return (r <= c) if transposed else (r >= c)


def _sgu_mix(w_ref, vals, transposed):
    t = vals.shape[0]
    mask = _tril_mask(transposed)
    ws = [jnp.where(mask, w_ref[h], 0.0).astype(BF16) for h in range(N_SGU_HEADS)]
    rows = []
    for ci in range(t // CHUNK):
        blk = vals[ci * CHUNK:(ci + 1) * CHUNK]
        rows.append(jnp.concatenate(
            [_dot(ws[h], blk[:, h * CHUNK:(h + 1) * CHUNK], NN) for h in range(N_SGU_HEADS)], axis=1))
    return jnp.concatenate(rows, axis=0)


def _attn_fwd(q, k, v):
    ps, os_ = [], []
    for h in range(N_ATT_HEADS):
        sl = slice(h * ATT_DIM, (h + 1) * ATT_DIM)
        s = _dot(q[:, sl], k[:, sl], NT) * ATT_SCALE
        s = s - jnp.max(s, axis=-1, keepdims=True)
        e = jnp.exp(s)
        p = e * (1.0 / jnp.sum(e, axis=-1, keepdims=True))
        ps.append(p)
        os_.append(_dot(p.astype(BF16), v[:, sl], NN))
    return ps, jnp.concatenate(os_, axis=1)


def _rms_branch(y_pre):
    r = lax.rsqrt(jnp.mean(y_pre * y_pre, axis=-1, keepdims=True) + EPS)
    return y_pre * r, r


def _branch_specs(t, n_tiles, order):
    width_in = 7 * WIDTH
    tile = lambda i: order(i)
    per_halo = t // HALO
    const2 = lambda i: (0, 0)
    const3 = lambda i: (0, 0, 0)
    return [
        pl.BlockSpec((t, width_in), lambda i: (tile(i), 0)),
        pl.BlockSpec((HALO, WIDTH), lambda i: (jnp.maximum(tile(i) * per_halo - 1, 0), 0)),
        pl.BlockSpec((4, GROUP, GROUP), const3),
        pl.BlockSpec((1, WIDTH), const2),
        pl.BlockSpec((1, WIDTH), const2),
        pl.BlockSpec((1, WIDTH), const2),
        pl.BlockSpec((N_SGU_HEADS, CHUNK, CHUNK), const3),
        pl.BlockSpec((CHUNK, WIDTH), const2),
        pl.BlockSpec((MEM_ROWS, WIDTH), const2),
        pl.BlockSpec((MEM_ROWS, WIDTH), const2),
        pl.BlockSpec((1, 3 * WIDTH), const2),
    ]


MEM_ROWS = 256


def _branches_fwd(proj, pool_w, pool_scale, ln_g, ln_b, sgu_w, bias_full, k, v, branch_norm, t, rider=None):
    s = proj.shape[0]
    n_tiles = s // t

    def body(proj_ref, halo_ref, pw_ref, ps_ref, lg_ref, lb_ref, sw_ref, sb_ref, k_ref, v_ref, bn_ref, y_ref, yt_ref):
        i = pl.program_id(0)
        col = lambda j: proj_ref[:, j * WIDTH:(j + 1) * WIDTH]

        def put(branch, y_pre):
            sl = slice(branch * WIDTH, (branch + 1) * WIDTH)
            val = (_rms_branch(y_pre)[0] * bn[:, sl]).astype(BF16)
            y_ref[:, sl] = val
            yt_ref[sl, :] = val.T

        bn = bn_ref[...]
        halo = jnp.where(i > 0, halo_ref[...], 0.0)
        _, y_pool = _pool_fwd(col(0), halo, _inv_counts(i * t, t), pw_ref[...])
        ga = col(1)
        ya = y_pool * ps_ref[...] * (ga * _sigmoid(ga))
        put(0, ya)
        vhat, _ = _layernorm_fwd(col(3))
        vn = (vhat * lg_ref[...] + lb_ref[...]).astype(BF16)
        z = _sgu_mix(sw_ref, vn, transposed=False) + jnp.tile(sb_ref[...], (t // CHUNK, 1))
        gb = col(4)
        yb = col(2) * z * (gb * _sigmoid(gb))
        put(1, yb)
        _, o = _attn_fwd(col(5).astype(BF16), k_ref[...], v_ref[...])
        gc = col(6)
        yc = o * (gc * _sigmoid(gc))
        put(2, yc)

    return _call(body, "branches_fwd", (n_tiles,), _branch_specs(t, n_tiles, lambda i: i),
                 [pl.BlockSpec((t, 3 * WIDTH), lambda i: (i, 0)), pl.BlockSpec((3 * WIDTH, t), lambda i: (0, i))],
                 [jax.ShapeDtypeStruct((s, 3 * WIDTH), BF16), jax.ShapeDtypeStruct((3 * WIDTH, s), BF16)], [],
                 [proj, proj, pool_w, pool_scale, ln_g, ln_b, sgu_w, bias_full, k, v, branch_norm], rider)


def _branches_bwd(proj, dy, pool_w, pool_scale, ln_g, ln_b, sgu_w, sgu_wt, bias_full, k, v, branch_norm, t, rider=None):
    s = proj.shape[0]
    n_tiles = s // t
    n_chunks = t // CHUNK
    order = lambda i: n_tiles - 1 - i

    def body(proj_ref, halo_ref, pw_ref, ps_ref, lg_ref, lb_ref, sw_ref, sb_ref, k_ref, v_ref, bn_ref,
             swt_ref, dy_ref,
             dproj_ref, dpw_ref, dps_ref, dlg_ref, dlb_ref, dsw_ref, dsb_ref, dbn_ref, dk_ref, dv_ref,
             carry_ref, dbias_ref):
        step = pl.program_id(0)
        i = order(step)

        @pl.when(step == 0)
        def _():
            for ref in (dpw_ref, dps_ref, dlg_ref, dlb_ref, dsw_ref, dbn_ref, dk_ref, dv_ref, carry_ref, dbias_ref):
                ref[...] = jnp.zeros(ref.shape, ref.dtype)

        col = lambda j: proj_ref[:, j * WIDTH:(j + 1) * WIDTH]
        bn = bn_ref[...]

        def norm_bwd(y_pre, sl):
            yhat, r = _rms_branch(y_pre)
            dyv = dy_ref[:, sl].astype(F32)
            dbn_ref[:, sl] += jnp.sum(dyv * yhat, axis=0, keepdims=True)
            dyhat = dyv * bn[:, sl]
            return r * (dyhat - yhat * jnp.mean(dyhat * yhat, axis=-1, keepdims=True))

        def gate(gv):
            sg = _sigmoid(gv)
            return gv * sg, sg * (1.0 + gv * (1.0 - sg))

        inv = _inv_counts(i * t, t)
        halo = jnp.where(i > 0, halo_ref[...], 0.0)
        pw = pw_ref[...]
        d, y_pool = _pool_fwd(col(0), halo, inv, pw)
        scale = ps_ref[...]
        silu_a, dsilu_a = gate(col(1))
        pa = y_pool * scale
        dya = norm_bwd(pa * silu_a, slice(0, WIDTH))
        dproj_ref[:, WIDTH:2 * WIDTH] = (dya * pa * dsilu_a).astype(BF16)
        dpa = dya * silu_a
        dps_ref[...] += jnp.sum(dpa * y_pool, axis=0, keepdims=True)
        dy_pool = (dpa * scale).astype(BF16)
        dd_parts, ddc_parts = [], []
        for gi in range(4):
            sl = slice(gi * GROUP, (gi + 1) * GROUP)
            dpw_ref[gi] += _dot(d[:, sl], dy_pool[:, sl], TN)
            dd = _dot(dy_pool[:, sl], pw[gi], NT)
            dd_parts.append(dd)
            ddc_parts.append(dd * inv[gi])
        ddc = jnp.concatenate(ddc_parts, axis=1)
        sums = _window_sums(jnp.concatenate([ddc, carry_ref[...]], axis=0), t, backward=True)
        carry_ref[...] = ddc[:HALO]
        dproj_ref[:, 0:WIDTH] = jnp.concatenate([sums[gi] - dd_parts[gi] for gi in range(4)], axis=1).astype(BF16)

        vhat, rstd = _layernorm_fwd(col(3))
        lg = lg_ref[...]
        vn = (vhat * lg + lb_ref[...]).astype(BF16)
        z = _sgu_mix(sw_ref, vn, transposed=False) + jnp.tile(sb_ref[...], (n_chunks, 1))
        u = col(2)
        silu_b, dsilu_b = gate(col(4))
        uz = u * z
        dyb = norm_bwd(uz * silu_b, slice(WIDTH, 2 * WIDTH))
        dproj_ref[:, 4 * WIDTH:5 * WIDTH] = (dyb * uz * dsilu_b).astype(BF16)
        duz = dyb * silu_b
        dproj_ref[:, 2 * WIDTH:3 * WIDTH] = (duz * z).astype(BF16)
        dz = duz * u
        dz_b = dz.astype(BF16)
        for ci in range(n_chunks):
            rows = slice(ci * CHUNK, (ci + 1) * CHUNK)
            dbias_ref[...] += dz[rows]
            for h in range(N_SGU_HEADS):
                sl = slice(h * CHUNK, (h + 1) * CHUNK)
                dsw_ref[h] += _dot(dz_b[rows, sl], vn[rows, sl], NT)
        dvn = _sgu_mix(swt_ref, dz_b, transposed=True)
        dlg_ref[...] += jnp.sum(dvn * vhat, axis=0, keepdims=True)
        dlb_ref[...] += jnp.sum(dvn, axis=0, keepdims=True)
        dvhat = dvn * lg
        dvb = rstd * (dvhat - jnp.mean(dvhat, axis=-1, keepdims=True)
                      - vhat * jnp.mean(dvhat * vhat, axis=-1, keepdims=True))
        dproj_ref[:, 3 * WIDTH:4 * WIDTH] = dvb.astype(BF16)

        q = col(5).astype(BF16)
        kv_k, kv_v = k_ref[...], v_ref[...]
        ps, o = _attn_fwd(q, kv_k, kv_v)
        silu_c, dsilu_c = gate(col(6))
        dyc = norm_bwd(o * silu_c, slice(2 * WIDTH, 3 * WIDTH))
        dproj_ref[:, 6 * WIDTH:7 * WIDTH] = (dyc * o * dsilu_c).astype(BF16)
        do = (dyc * silu_c).astype(BF16)
        dq_parts = []
        for h in range(N_ATT_HEADS):
            sl = slice(h * ATT_DIM, (h + 1) * ATT_DIM)
            p = ps[h]
            dp = _dot(do[:, sl], kv_v[:, sl], NT)
            ds = (p * (dp - jnp.sum(p * dp, axis=-1, keepdims=True)) * ATT_SCALE).astype(BF16)
            dq_parts.append(_dot(ds, kv_k[:, sl], NN))
            dk_ref[:, sl] += _dot(ds, q[:, sl], TN)
            dv_ref[:, sl] += _dot(p.astype(BF16), do[:, sl], TN)
        dproj_ref[:, 5 * WIDTH:6 * WIDTH] = jnp.concatenate(dq_parts, axis=1).astype(BF16)

        @pl.when(step == n_tiles - 1)
        def _():
            keep = _tril_mask(transposed=False)
            for h in range(N_SGU_HEADS):
                dsw_ref[h] = jnp.where(keep, dsw_ref[h], 0.0)
            dsb_ref[...] = jnp.concatenate(
                [jnp.sum(dbias_ref[:, h * CHUNK:(h + 1) * CHUNK], axis=1, keepdims=True)
                 for h in range(N_SGU_HEADS)], axis=1)

    const2 = lambda i: (0, 0)
    const3 = lambda i: (0, 0, 0)
    out_shapes = (
        jax.ShapeDtypeStruct((s, 7 * WIDTH), BF16),
        jax.ShapeDtypeStruct((4, GROUP, GROUP), F32),
        jax.ShapeDtypeStruct((1, WIDTH), F32),
        jax.ShapeDtypeStruct((1, WIDTH), F32),
        jax.ShapeDtypeStruct((1, WIDTH), F32),
        jax.ShapeDtypeStruct((N_SGU_HEADS, CHUNK, CHUNK), F32),
        jax.ShapeDtypeStruct((CHUNK, N_SGU_HEADS), F32),
        jax.ShapeDtypeStruct((1, 3 * WIDTH), F32),
        jax.ShapeDtypeStruct((MEM_ROWS, WIDTH), F32),
        jax.ShapeDtypeStruct((MEM_ROWS, WIDTH), F32),
    )
    out_specs = (
        pl.BlockSpec((t, 7 * WIDTH), lambda i: (order(i), 0)),
        pl.BlockSpec((4, GROUP, GROUP), const3),
        pl.BlockSpec((1, WIDTH), const2),
        pl.BlockSpec((1, WIDTH), const2),
        pl.BlockSpec((1, WIDTH), const2),
        pl.BlockSpec((N_SGU_HEADS, CHUNK, CHUNK), const3),
        pl.BlockSpec((CHUNK, N_SGU_HEADS), const2),
        pl.BlockSpec((1, 3 * WIDTH), const2),
        pl.BlockSpec((MEM_ROWS, WIDTH), const2),
        pl.BlockSpec((MEM_ROWS, WIDTH), const2),
    )
    in_specs = _branch_specs(t, n_tiles, order) + [
        pl.BlockSpec((N_SGU_HEADS, CHUNK, CHUNK), const3),
        pl.BlockSpec((t, 3 * WIDTH), lambda i: (order(i), 0)),
    ]
    return _call(body, "branches_bwd", (n_tiles,), in_specs, out_specs, out_shapes,
                 [pltpu.VMEM((HALO, WIDTH), F32), pltpu.VMEM((CHUNK, WIDTH), F32)],
                 [proj, proj, pool_w, pool_scale, ln_g, ln_b, sgu_w, bias_full, k, v, branch_norm, sgu_wt, dy], rider)


def _out_loss(y, w_out, x, target, g_post, tm):
    s, d = x.shape
    e_w = y.shape[1]
    n_tiles = s // tm

    def body(y_ref, w_ref, x_ref, t_ref, g_ref, loss_ref, dz_ref, dout_ref, dy_ref, dg_ref, sq_ref):
        i = pl.program_id(0)

        @pl.when(i == 0)
        def _():
            sq_ref[...] = jnp.zeros(sq_ref.shape, F32)
            dg_ref[...] = jnp.zeros(dg_ref.shape, F32)

        w = w_ref[...]
        out = _dot(y_ref[...], w, NN)
        r = lax.rsqrt(jnp.mean(out * out, axis=-1, keepdims=True) + EPS)
        outn = out * r
        g = g_ref[...]
        err = (x_ref[...] + outn * g) - t_ref[...]
        sq_ref[...] += jnp.sum(err * err, axis=0, keepdims=True)
        dz = err * (1.0 / d)
        dz_ref[...] = dz
        dg_ref[...] += jnp.sum(dz * outn, axis=0, keepdims=True)
        doutn = dz * g
        dout = (r * (doutn - outn * jnp.mean(doutn * outn, axis=-1, keepdims=True))).astype(BF16)
        dout_ref[...] = dout
        dy_ref[...] = _dot(dout, w, NT).astype(BF16)

        @pl.when(i == n_tiles - 1)
        def _():
            loss_ref[...] = 0.5 * jnp.sum(sq_ref[...], axis=1, keepdims=True) * (1.0 / d)

    row = lambda i: (i, 0)
    const2 = lambda i: (0, 0)
    return pl.pallas_call(
        body, name="out_loss", grid=(n_tiles,),
        in_specs=[
            pl.BlockSpec((tm, e_w), row),
            pl.BlockSpec((e_w, d), const2, pipeline_mode=pl.Buffered(1)),
            pl.BlockSpec((tm, d), row),
            pl.BlockSpec((tm, d), row),
            pl.BlockSpec((1, d), const2),
        ],
        out_specs=(
            pl.BlockSpec((1, 1), const2),
            pl.BlockSpec((tm, d), row),
            pl.BlockSpec((tm, d), row),
            pl.BlockSpec((tm, e_w), row),
            pl.BlockSpec((1, d), const2),
        ),
        out_shape=(
            jax.ShapeDtypeStruct((1, 1), F32),
            jax.ShapeDtypeStruct((s, d), F32),
            jax.ShapeDtypeStruct((s, d), BF16),
            jax.ShapeDtypeStruct((s, e_w), BF16),
            jax.ShapeDtypeStruct((1, d), F32),
        ),
        scratch_shapes=[pltpu.VMEM((1, d), F32)],
        compiler_params=_params(1),
    )(y, w_out, x, target, g_post)


def _dx_call(dproj, w_in, x, dz, g_pre, tm, tk, rider=None):
    s, d = x.shape
    k_total = dproj.shape[1]
    nk = k_total // tk
    n_tiles = s // tm

    def body(dp_ref, w_ref, x_ref, dz_ref, g_ref, dx_ref, dg_ref, acc_ref):
        i, kk = pl.program_id(0), pl.program_id(1)
        part = lambda: _dot(dp_ref[...], w_ref[...], NT)

        @pl.when(kk == 0)
        def _():
            acc_ref[...] = part()

        @pl.when((kk > 0) & (kk < nk - 1))
        def _():
            acc_ref[...] += part()

        @pl.when((i == 0) & (kk == 0))
        def _():
            dg_ref[...] = jnp.zeros(dg_ref.shape, F32)

        @pl.when(kk == nk - 1)
        def _():
            dh = acc_ref[...] + part()
            xv = x_ref[...]
            r = lax.rsqrt(jnp.mean(xv * xv, axis=-1, keepdims=True) + EPS)
            xhat = xv * r
            dg_ref[...] += jnp.sum(dh * xhat, axis=0, keepdims=True)
            dxhat = dh * g_ref[...]
            dx_ref[...] = dz_ref[...] + r * (dxhat - xhat * jnp.mean(dxhat * xhat, axis=-1, keepdims=True))

    row = lambda i, kk: (i, 0)
    const2 = lambda i, kk: (0, 0)
    return _call(
        body, "dx", (n_tiles, nk),
        [
            pl.BlockSpec((tm, tk), lambda i, kk: (i, kk)),
            pl.BlockSpec((d, tk), lambda i, kk: (0, kk)),
            pl.BlockSpec((tm, d), row),
            pl.BlockSpec((tm, d), row),
            pl.BlockSpec((1, d), const2),
        ],
        [pl.BlockSpec((tm, d), row), pl.BlockSpec((1, d), const2)],
        [jax.ShapeDtypeStruct((s, d), F32), jax.ShapeDtypeStruct((1, d), F32)],
        [pltpu.VMEM((tm, d), F32)], [dproj, w_in, x, dz, g_pre], rider)


def _rows_tile(rows, cols, n_arrays, itemsize=4):
    budget = ELEMENTWISE_VMEM // (2 * n_arrays * cols * itemsize)
    if rows <= budget:
        return rows
    best = None
    for cand in range(16, rows + 1, 16):
        if rows % cand == 0 and cand <= max(budget, 16):
            best = cand
    return best if best is not None else rows


def _elementwise(fn, inputs, out_dtypes, name):
    rows, cols = inputs[0].shape
    tr = _rows_tile(rows, cols, len(inputs) + len(out_dtypes))
    n_in = len(inputs)

    def body(*refs):
        outs = fn(*[r[...] for r in refs[:n_in]])
        for o_ref, o in zip(refs[n_in:], outs):
            o_ref[...] = o.astype(o_ref.dtype)

    spec = pl.BlockSpec((tr, cols), lambda i: (i, 0))
    return pl.pallas_call(
        body, name=name, grid=(rows // tr,),
        in_specs=[spec] * n_in, out_specs=tuple([spec] * len(out_dtypes)),
        out_shape=tuple(jax.ShapeDtypeStruct((rows, cols), dt) for dt in out_dtypes),
        compiler_params=_params(1),
    )(*inputs)


def _blockwise(fn, pos, inputs, in_specs, out_shape, out_spec, grid, name):
    n_in = len(inputs)

    def body(pos_ref, *refs):
        o_ref = refs[n_in]
        (out,) = fn(*[r[...].reshape(o_ref.shape) for r in refs[:n_in]])
        o_ref[...] = out.astype(o_ref.dtype)

    return pl.pallas_call(
        body, name=name,
        grid_spec=pltpu.PrefetchScalarGridSpec(num_scalar_prefetch=1, grid=grid, in_specs=in_specs,
                                               out_specs=out_spec),
        out_shape=out_shape,
        compiler_params=_params(len(grid)),
    )(pos, *inputs)


def _cast_copy(x):
    return (x,)


def _pair_sum(mine, theirs):
    return ((mine.astype(F32) + theirs.astype(F32)),)


def _four_sum(own, t0, t1, t2):
    return ((((own.astype(F32) + t0.astype(F32)) + t1.astype(F32)) + t2.astype(F32)),)


def _adamw(w, g, m, v):
    m = ADAM_B1 * m + (1.0 - ADAM_B1) * g
    v = ADAM_B2 * v + (1.0 - ADAM_B2) * jnp.square(g)
    m_hat = m / (1.0 - ADAM_B1 ** ADAM_STEP)
    v_hat = v / (1.0 - ADAM_B2 ** ADAM_STEP)
    delta = -ADAM_LR * (m_hat / (jnp.sqrt(v_hat) + ADAM_EPS) + ADAM_WD * w)
    return delta, m, v


def _place():
    x, y, c = lax.axis_index("x"), lax.axis_index("y"), lax.axis_index("c")
    chips = [(1 - x, y), (x, 1 - y), (1 - x, 1 - y)]
    return x, y, c, chips


def _remote(src, dst, send_sem, recv_sem, to):
    return pltpu.make_async_remote_copy(src_ref=src, dst_ref=dst, send_sem=send_sem, recv_sem=recv_sem,
                                        device_id=to, device_id_type=MESH)


def _split(ref, plan):
    views = [ref]
    for axis, parts in plan:
        size = ref.shape[axis] // parts
        assert size * parts == ref.shape[axis]
        views = [v.at[tuple(pl.ds(q * size, size) if i == axis else slice(None) for i in range(len(ref.shape)))]
                 for v in views for q in range(parts)]
    return views


def _started(src, dst, send_sem, recv_sem, to):
    copy = _remote(src, dst, send_sem, recv_sem, to)
    copy.start()
    return copy


def _hbm_call(body, name, inputs, out_shapes, scratch, aliases=None):
    return pl.pallas_call(
        body, name=name,
        in_specs=[ANY] * len(inputs), out_specs=tuple([ANY] * len(out_shapes)), out_shape=tuple(out_shapes),
        scratch_shapes=scratch, input_output_aliases=aliases or {},
        compiler_params=pltpu.CompilerParams(has_side_effects=True),
    )(*inputs)


def _shard_half(kind, ref, chip, cc):
    if kind == 0:
        rows, cols = ref.shape[0] // 2, ref.shape[1] // 4
        return ref.at[pl.ds(cc * rows, rows), pl.ds(pl.multiple_of(chip * cols, LANES), cols)]
    if kind == 3:
        rows = ref.shape[1] // 8
        return ref.at[:, pl.ds(pl.multiple_of((2 * chip + cc) * rows, BF16_ROWS), rows), :]
    rows = ref.shape[0] // 8
    return ref.at[pl.ds(pl.multiple_of((2 * chip + cc) * rows, BF16_ROWS), rows), :]


def _relay_rider(full):
    kind = 0

    def quarter(ref, chip_no, cc, q):
        return _split(_shard_half(kind, ref, chip_no, cc), [(0, 2)])[q]

    def run(in_refs, full_refs, send_sems, recv_sems, start):
        (ref,) = full_refs
        x, y, c, chips = _place()
        sibling = (x, y, 1 - c)
        chip_no = [2 * ch[0] + ch[1] for ch in chips]
        if start:
            for p in (0, 1):
                held = quarter(ref, chip_no[1 - p], c, p)
                _remote(held, held, send_sems.at[p], recv_sems.at[p], (*chips[p], c)).start()
            return
        for p in (0, 1):
            landed = quarter(ref, chip_no[2], c, p)
            _remote(landed, landed, send_sems.at[p], recv_sems.at[p], (*chips[p], c)).wait_recv()
            _remote(landed, landed, send_sems.at[2], recv_sems.at[2], sibling).start()
        mine, theirs = _shard_half(kind, ref, chip_no[2], c), _shard_half(kind, ref, chip_no[2], 1 - c)
        _remote(mine, mine, send_sems.at[2], recv_sems.at[2], sibling).wait_send()
        _remote(theirs, theirs, send_sems.at[2], recv_sems.at[2], sibling).wait_recv()
        for p in (0, 1):
            held = quarter(ref, chip_no[1 - p], c, p)
            _remote(held, held, send_sems.at[p], recv_sems.at[p], (*chips[p], c)).wait_send()

    return _Rider([full], [jax.ShapeDtypeStruct(full.shape, full.dtype)], 3, run, aliases={0: 0})


def _riders(riders):
    def bounds(counts):
        ends = [sum(counts[:i + 1]) for i in range(len(counts))]
        return list(zip([0] + ends[:-1], ends))

    ins = bounds([len(r.inputs) for r in riders])
    outs = bounds([len(r.out_shapes) for r in riders])
    sems = bounds([r.n_sems for r in riders])

    class From:
        def __init__(self, sem_refs, base):
            self.sem_refs, self.base, self.at = sem_refs, base, self

        def __getitem__(self, k):
            return self.sem_refs.at[self.base + k]

    def run(in_refs, out_refs, send_sems, recv_sems, start):
        for r, (i0, i1), (o0, o1), (s0, _) in zip(riders, ins, outs, sems):
            r.run(in_refs[i0:i1], out_refs[o0:o1], From(send_sems, s0), From(recv_sems, s0), start)

    aliases = {}
    for r, (i0, _), (o0, _) in zip(riders, ins, outs):
        aliases.update({i0 + i: o0 + o for i, o in r.aliases.items()})
    return _Rider([a for r in riders for a in r.inputs], [o for r in riders for o in r.out_shapes],
                  sems[-1][1], run, aliases)


def _gather_rider(fulls, kinds, peers=(0, 1, 2)):
    n = len(fulls)
    full_half = _shard_half

    def run(in_refs, full_refs, send_sems, recv_sems, start):
        x, y, c, chips = _place()
        me = 2 * x + y
        sibling = (x, y, 1 - c)
        plans = [[(0, MAX_PARTS)], [(0, 2)], [(0, 2)], []]
        chips = [(p, chips[p]) for p in peers]
        across = lambda a, p, k: (3 * a + p) * MAX_PARTS + k
        onward = lambda a, p: 3 * n * MAX_PARTS + 3 * a + p

        def parts(a, chip_no, cc):
            return _split(full_half(kinds[a], full_refs[a], chip_no, cc), plans[kinds[a]])

        if start:
            for p, chip in chips:
                for a in range(n):
                    for k, mine in enumerate(parts(a, me, c)):
                        _remote(mine, mine, send_sems.at[across(a, p, k)], recv_sems.at[across(a, p, k)],
                                (*chip, c)).start()
            return
        for k in range(MAX_PARTS):
            for p, chip in chips:
                for a in range(n):
                    landed = parts(a, 2 * chip[0] + chip[1], c)
                    if k < len(landed):
                        _remote(landed[k], landed[k], send_sems.at[across(a, p, k)], recv_sems.at[across(a, p, k)],
                                (*chip, c)).wait_recv()
                        _remote(landed[k], landed[k], send_sems.at[onward(a, p)], recv_sems.at[onward(a, p)],
                                sibling).start()
        for p, chip in chips:
            them = 2 * chip[0] + chip[1]
            for a in range(n):
                passed = full_half(kinds[a], full_refs[a], them, 1 - c)
                _remote(passed, passed, send_sems.at[onward(a, p)], recv_sems.at[onward(a, p)], sibling).wait_recv()
                landed = full_half(kinds[a], full_refs[a], them, c)
                _remote(landed, landed, send_sems.at[onward(a, p)], recv_sems.at[onward(a, p)], sibling).wait_send()
                for k, mine in enumerate(parts(a, me, c)):
                    _remote(mine, mine, send_sems.at[across(a, p, k)], recv_sems.at[across(a, p, k)],
                            (*chip, c)).wait_send()

    return _Rider(fulls, [jax.ShapeDtypeStruct(f.shape, f.dtype) for f in fulls], 3 * n * (MAX_PARTS + 1), run,
                  aliases={a: a for a in range(n)})


def _exchange_halves(grads, name):
    n = len(grads)
    arrays = [g for g, _ in grads]
    out_shapes = [jax.ShapeDtypeStruct(tuple(1 if i == ax else dim for i, dim in enumerate(g.shape)), g.dtype)
                  for g, ax in grads]

    def half(ref, ax, cc):
        idx = tuple(pl.ds(cc, 1) if i == ax else slice(None) for i in range(len(ref.shape)))
        return ref.at[idx]

    def body(*refs):
        in_refs, out_refs = refs[:n], refs[n:2 * n]
        send_sems, recv_sems = refs[2 * n:]
        x, y, c, _ = _place()
        sibling = (x, y, 1 - c)
        copies = [_started(half(in_refs[a], grads[a][1], 1 - c), out_refs[a], send_sems.at[a], recv_sems.at[a],
                           sibling) for a in range(n)]
        for rem in copies:
            rem.wait()

    return _hbm_call(body, name, arrays, out_shapes,
                     [pltpu.SemaphoreType.DMA((n,)), pltpu.SemaphoreType.DMA((n,))])


def _exchange_rider(arrays):
    n = len(arrays)

    def run(in_refs, out_refs, send_sems, recv_sems, start):
        x, y, c, _ = _place()
        sibling = (x, y, 1 - c)
        for a in range(n):
            sems = (send_sems.at[a], recv_sems.at[a])
            if start:
                _started(in_refs[a], out_refs[a], *sems, sibling)
            else:
                _remote(in_refs[a], out_refs[a], *sems, sibling).wait()

    return _Rider(arrays, [jax.ShapeDtypeStruct(a.shape, a.dtype) for a in arrays], n, run)


def _scatter_rider(parts):
    n = len(parts)
    arrays = [p for p, _ in parts]

    def block_shape(p, ax):
        if ax == len(p.shape) - 1:
            return p.shape[:-1] + (p.shape[-1] // 4,)
        return tuple(1 if i == ax else dim for i, dim in enumerate(p.shape))

    out_shapes = [jax.ShapeDtypeStruct((3,) + block_shape(p, ax), p.dtype) for p, ax in parts]

    def block(ref, ax, chip):
        rank = len(ref.shape)
        if ax == rank - 1:
            cols = ref.shape[-1] // 4
            last = pl.ds(pl.multiple_of(chip * cols, LANES), cols)
            return ref.at[tuple([slice(None)] * (rank - 1) + [last])]
        return ref.at[tuple(pl.ds(chip, 1) if i == ax else slice(None) for i in range(rank))]

    def run(in_refs, out_refs, send_sems, recv_sems, start):
        x, y, c, chips = _place()
        for a in range(n):
            ax = parts[a][1]
            for p, chip in enumerate(chips):
                src, dst = block(in_refs[a], ax, 2 * chip[0] + chip[1]), out_refs[a].at[p]
                sems = (send_sems.at[3 * a + p], recv_sems.at[3 * a + p])
                if start:
                    _started(src, dst, *sems, (*chip, c))
                else:
                    _remote(src, dst, *sems, (*chip, c)).wait()

    return _Rider(arrays, out_shapes, 3 * n, run)


def _join_halves(joined):
    n = len(joined)
    arrays = [j for j, _ in joined]

    def body(*refs):
        out_refs = refs[n:2 * n]
        send_sems, recv_sems = refs[2 * n:]
        x, y, c, _ = _place()
        sibling = (x, y, 1 - c)

        def half(a, cc):
            rank = len(out_refs[a].shape)
            return out_refs[a].at[tuple(pl.ds(cc, 1) if i == joined[a][1] else slice(None) for i in range(rank))]

        sends = [_started(half(a, c), half(a, c), send_sems.at[a], recv_sems.at[a], sibling) for a in range(n)]
        for a, rem in enumerate(sends):
            rem.wait_send()
            _remote(half(a, 1 - c), half(a, 1 - c), send_sems.at[a], recv_sems.at[a], sibling).wait_recv()

    return _hbm_call(body, "join_halves", arrays, [jax.ShapeDtypeStruct(j.shape, j.dtype) for j in arrays],
                     [pltpu.SemaphoreType.DMA((n,)), pltpu.SemaphoreType.DMA((n,))],
                     aliases={a: a for a in range(n)})


def _allreduce_small(packed):
    rows, lanes = packed.shape
    half = rows // 2

    def body(in_ref, out_ref, pair_ref, gath_ref, send_sems, recv_sems):
        x, y, c, chips = _place()
        me = 2 * x + y
        sibling = (x, y, 1 - c)
        mine = pl.ds(pl.multiple_of(c * half, 8), half)
        theirs = pl.ds(pl.multiple_of((1 - c) * half, 8), half)
        to_sib = _remote(in_ref.at[theirs], pair_ref, send_sems.at[0], recv_sems.at[0], sibling)
        to_sib.start()
        to_sib.wait()
        gath_ref[me] = in_ref[mine] + pair_ref[...]
        sends = [_remote(gath_ref.at[me], gath_ref.at[me], send_sems.at[1 + p], recv_sems.at[1 + p], (*chip, c))
                 for p, chip in enumerate(chips)]
        for cp in sends:
            cp.start()
        for p, chip in enumerate(chips):
            slot = gath_ref.at[2 * chip[0] + chip[1]]
            _remote(slot, slot, send_sems.at[1 + p], recv_sems.at[1 + p], (*chip, c)).wait_recv()
        for cp in sends:
            cp.wait_send()
        out_ref[mine] = ((gath_ref[0] + gath_ref[1]) + gath_ref[2]) + gath_ref[3]
        back = _remote(out_ref.at[mine], out_ref.at[mine], send_sems.at[4], recv_sems.at[4], sibling)
        back.start()
        back.wait_send()
        _remote(out_ref.at[theirs], out_ref.at[theirs], send_sems.at[4], recv_sems.at[4], sibling).wait_recv()

    vmem = pl.BlockSpec(memory_space=pltpu.VMEM)
    return pl.pallas_call(
        body, name="allreduce_small",
        in_specs=[vmem], out_specs=vmem, out_shape=jax.ShapeDtypeStruct((rows, lanes), F32),
        scratch_shapes=[pltpu.VMEM((half, lanes), F32), pltpu.VMEM((4, half, lanes), F32),
                        pltpu.SemaphoreType.DMA((5,)), pltpu.SemaphoreType.DMA((5,))],
        compiler_params=pltpu.CompilerParams(has_side_effects=True, vmem_limit_bytes=32 * 1024 * 1024),
    )(packed)


SMALL = ("norm_pre", "pool_scale", "sgu_ln_g", "sgu_ln_b", "sgu_w", "sgu_b", "mem_norm", "branch_norm", "norm_post")
LARGE = ("w_in", "pool_w", "w_kv", "w_out")
ORDER = ("norm_pre", "w_in", "pool_w", "pool_scale", "sgu_ln_g", "sgu_ln_b", "sgu_w", "sgu_b", "mem_norm", "w_kv",
         "branch_norm", "w_out", "norm_post")


def _pack(arrays, extra=()):
    rows = [a.reshape(-1, 128) for a in arrays] + list(extra)
    pad = -sum(r.shape[0] for r in rows) % 16
    return jnp.concatenate(rows + ([jnp.zeros((pad, 128), F32)] if pad else []), axis=0)


def _unpack(packed, like):
    out, row = [], 0
    for a in like:
        rows = a.size // 128
        out.append(packed[row:row + rows].reshape(a.shape))
        row += rows
    return out


def kernel(x, mem, norm_pre, w_in, pool_w, pool_scale, sgu_ln_g, sgu_ln_b, sgu_w, sgu_b, mem_norm, w_kv, branch_norm, w_out, norm_post, loss_target, m_norm_pre, m_w_in, m_pool_w, m_pool_scale, m_sgu_ln_g, m_sgu_ln_b, m_sgu_w, m_sgu_b, m_mem_norm, m_w_kv, m_branch_norm, m_w_out, m_norm_post, v_norm_pre, v_w_in, v_pool_w, v_pool_scale, v_sgu_ln_g, v_sgu_ln_b, v_sgu_w, v_sgu_b, v_mem_norm, v_w_kv, v_branch_norm, v_w_out, v_norm_post):
    weights = dict(norm_pre=norm_pre, w_in=w_in, pool_w=pool_w, pool_scale=pool_scale, sgu_ln_g=sgu_ln_g,
                   sgu_ln_b=sgu_ln_b, sgu_w=sgu_w, sgu_b=sgu_b, mem_norm=mem_norm, w_kv=w_kv, branch_norm=branch_norm,
                   w_out=w_out, norm_post=norm_post)
    mom1 = dict(norm_pre=m_norm_pre, w_in=m_w_in, pool_w=m_pool_w, pool_scale=m_pool_scale, sgu_ln_g=m_sgu_ln_g,
                sgu_ln_b=m_sgu_ln_b, sgu_w=m_sgu_w, sgu_b=m_sgu_b, mem_norm=m_mem_norm, w_kv=m_w_kv,
                branch_norm=m_branch_norm, w_out=m_w_out, norm_post=m_norm_post)
    mom2 = dict(norm_pre=v_norm_pre, w_in=v_w_in, pool_w=v_pool_w, pool_scale=v_pool_scale, sgu_ln_g=v_sgu_ln_g,
                sgu_ln_b=v_sgu_ln_b, sgu_w=v_sgu_w, sgu_b=v_sgu_b, mem_norm=v_mem_norm, w_kv=v_w_kv,
                branch_norm=v_branch_norm, w_out=v_w_out, norm_post=v_norm_post)

    s, d = x.shape[1], x.shape[2]
    x2, mem2, tgt2 = x[0], mem[0], loss_target[0]
    t_branch = min(256, s)
    tm = min(512, s)

    core = lax.axis_index("c")
    chip = 2 * lax.axis_index("x") + lax.axis_index("y")
    pos = jnp.stack([core, chip]).astype(jnp.int32)
    n_in, n_kv, n_out = 4 * w_in.shape[2], 4 * w_kv.shape[1], 4 * w_out.shape[1]
    wi_rows, kv_rows, wo_rows = d // 8, n_kv // 8, n_out // 8

    def placed(shard, full_shape, block, grid, in_map, out_map, name):
        return _blockwise(_cast_copy, pos, [shard], [pl.BlockSpec(block, in_map)],
                          jax.ShapeDtypeStruct(full_shape, BF16), pl.BlockSpec(block, out_map), grid, name)

    kv_cols, pw_rows = w_kv.shape[2], GROUP // 8
    wi_own = placed(w_in[0], (d, n_in), (wi_rows, n_in // 4), (8,), lambda i, p: (i, 0), lambda i, p: (i, p[1]),
                    "place_w_in")
    wkv_own = placed(w_kv[0], (n_kv, kv_cols), (kv_rows, kv_cols), (2,), lambda i, p: (i, 0),
                     lambda i, p: (2 * p[1] + i, 0), "place_w_kv")
    wo_own = placed(w_out[0], (n_out, d), (wo_rows, d), (2,), lambda i, p: (i, 0), lambda i, p: (2 * p[1] + i, 0),
                    "place_w_out")
    pw_own = placed(pool_w[0], (4, GROUP, GROUP), (4, GROUP // 4, GROUP), (1,), lambda i, p: (0, 0, 0),
                    lambda i, p: (0, p[1], 0), "place_pool_w")

    x_pos, y_pos = lax.axis_index("x"), lax.axis_index("y")
    chips = jnp.stack([chip, 2 * (1 - x_pos) + y_pos, 2 * x_pos + 1 - y_pos,
                       2 * (1 - x_pos) + 1 - y_pos]).astype(jnp.int32)
    mem_g = mem_norm.reshape(1, d)
    proj, h, h_t, wi_full = _proj_piece(chips, 0, 1, x2, norm_pre, None, None, n_in,
                                   _gather_rider([wi_own], [0], peers=(0, 1)), tm, "proj_own")
    proj, wi_full, wkv_full, pw_full = _proj_piece(
        chips, 1, 2, h, None, None, proj, n_in,
        _riders([_relay_rider(wi_full), _gather_rider([wkv_own, pw_own], [1, 3])]), tm, "proj_neighbours")
    proj, wo_part = _proj_piece(chips, 3, 1, h, None, wi_full, proj, n_in,
                                _gather_rider([wo_own], [2], peers=(0, 1)), tm, "proj_diagonal")
    k_m, v_m = _kv_fwd(mem2, mem_g, wkv_full)
    bias_full = jnp.repeat(sgu_b[0].T, CHUNK, axis=1)
    y, y_t, wo_full = _branches_fwd(proj, pw_full, pool_scale, sgu_ln_g, sgu_ln_b, sgu_w[0], bias_full, k_m, v_m,
                                    branch_norm, t_branch, _gather_rider([wo_part], [2], peers=(2,)))
    loss_local, dz, dout, dy, g_norm_post = _out_loss(y, wo_full, x2, tgt2, norm_post, min(256, s))

    def pair_sums(views):
        theirs = _exchange_halves([(v[0], v[1]) for v in views], "exchange_for_" + views[0][8])
        return [_blockwise(_pair_sum, pos, [v[0], th], [pl.BlockSpec(v[2], v[3][0]), pl.BlockSpec(v[2], v[3][1])],
                           jax.ShapeDtypeStruct(v[4], BF16), pl.BlockSpec(v[5], v[6]), v[7], v[8])
                for v, th in zip(views, theirs)]

    tk = min(1024, s)
    (g_wo,) = _grad_rows(y_t, dout, pos, lambda i, p: i, n_out, n_out // 2, 1024, tk, "grad_w_out")
    (ps_wo,) = pair_sums([
        (g_wo.reshape(4, 2, wo_rows, d), 1, (1, 1, wo_rows, d),
         (lambda i, p: (i, p[0], 0, 0), lambda i, p: (i, 0, 0, 0)), (4, wo_rows, d), (1, wo_rows, d),
         lambda i, p: (i, 0, 0), (4,), "pair_sum_w_out")])
    (dproj, g_pw, g_pool_scale, g_ln_g, g_ln_b, g_sgu_w, g_sgu_b_t, g_branch_norm, dk, dv, landed_wo) = _branches_bwd(
        proj, dy, pw_full, pool_scale, sgu_ln_g, sgu_ln_b, sgu_w[0], jnp.swapaxes(sgu_w[0], 1, 2), bias_full,
        k_m, v_m, branch_norm, t_branch, _scatter_rider([(ps_wo, 0)]))
    g_wkv, g_mem_norm = _kv_bwd(mem2, mem_g, wkv_full, dk, dv)
    ps_kv, ps_pw = pair_sums([
        (g_wkv.reshape(4, 2, kv_rows, kv_cols), 1, (1, 1, kv_rows, kv_cols),
         (lambda i, p: (i, p[0], 0, 0), lambda i, p: (i, 0, 0, 0)), (4, kv_rows, kv_cols), (1, kv_rows, kv_cols),
         lambda i, p: (i, 0, 0), (4,), "pair_sum_w_kv"),
        (g_pw.astype(BF16).reshape(4, 4, 2, pw_rows, GROUP), 2, (1, 4, 1, pw_rows, GROUP),
         (lambda i, p: (i, 0, p[0], 0, 0), lambda i, p: (i, 0, 0, 0, 0)), (4, 4, pw_rows, GROUP),
         (1, 4, pw_rows, GROUP), lambda i, p: (i, 0, 0, 0), (4,), "pair_sum_pool_w")])
    gwi_theirs, landed_kv, landed_pw = _grad_rows(
        h_t, dproj, pos, lambda i, p: 1 - p[0], d // 2, d // 2, n_in // 4, tk, "grad_w_in_sibling_half",
        _scatter_rider([(ps_kv, 0), (ps_pw, 1)]))
    gwi_mine, gwi_from_sibling = _grad_rows(h_t, dproj, pos, lambda i, p: p[0], d // 2, d // 2, n_in // 4, tk,
                                            "grad_w_in_own_half", _exchange_rider([gwi_theirs]))
    ps_wi = _elementwise(_pair_sum, [gwi_mine, gwi_from_sibling], [BF16], "pair_sum_w_in")[0]
    grad_x, g_norm_pre, landed_wi = _dx_call(dproj, wi_full, x2, dz, norm_pre, tm, 1024,
                                             _scatter_rider([(ps_wi, 1)]))
    psum = [ps_wi, ps_kv, ps_wo, ps_pw]
    landed = [landed_wi, landed_kv, landed_wo, landed_pw]
    from_chip = lambda spec_shape, rank: [
        pl.BlockSpec(spec_shape, functools.partial(lambda i, p, q: (q, i) + (0,) * (rank - 2), q=q))
        for q in range(3)]
    joined = _join_halves([
        (_blockwise(_four_sum, pos, [psum[0]] + [landed[0]] * 3,
                    [pl.BlockSpec((256, n_in // 4), lambda i, p: (i, p[1]))] + from_chip((1, 256, n_in // 4), 3),
                    jax.ShapeDtypeStruct((2, d // 2, n_in // 4), F32),
                    pl.BlockSpec((1, 256, n_in // 4), lambda i, p: (p[0], i, 0)), (d // 2 // 256,), "chip_sum_w_in"),
         0),
        (_blockwise(_four_sum, pos, [psum[1]] + [landed[1]] * 3,
                    [pl.BlockSpec((1, kv_rows, kv_cols), lambda i, p: (p[1], 0, 0))]
                    + from_chip((1, 1, kv_rows, kv_cols), 4),
                    jax.ShapeDtypeStruct((2, kv_rows, kv_cols), F32),
                    pl.BlockSpec((1, kv_rows, kv_cols), lambda i, p: (p[0], 0, 0)), (1,), "chip_sum_w_kv"),
         0),
        (_blockwise(_four_sum, pos, [psum[2]] + [landed[2]] * 3,
                    [pl.BlockSpec((1, wo_rows, d), lambda i, p: (p[1], 0, 0))] + from_chip((1, 1, wo_rows, d), 4),
                    jax.ShapeDtypeStruct((2, wo_rows, d), F32),
                    pl.BlockSpec((1, wo_rows, d), lambda i, p: (p[0], 0, 0)), (1,), "chip_sum_w_out"),
         0),
        (_blockwise(_four_sum, pos, [psum[3]] + [landed[3]] * 3,
                    [pl.BlockSpec((4, 1, pw_rows, GROUP), lambda i, p: (0, p[1], 0, 0))]
                    + from_chip((1, 4, 1, pw_rows, GROUP), 5),
                    jax.ShapeDtypeStruct((4, 2, pw_rows, GROUP), F32),
                    pl.BlockSpec((4, 1, pw_rows, GROUP), lambda i, p: (0, p[0], 0, 0)), (1,), "chip_sum_pool_w"),
         1),
    ])
    grads = {"w_in": joined[0].reshape(w_in.shape), "w_kv": joined[1].reshape(w_kv.shape),
             "w_out": joined[2].reshape(w_out.shape), "pool_w": joined[3].reshape(pool_w.shape)}

    small_local = dict(norm_pre=g_norm_pre, pool_scale=g_pool_scale, sgu_ln_g=g_ln_g, sgu_ln_b=g_ln_b,
                       sgu_w=g_sgu_w, sgu_b=g_sgu_b_t.T, mem_norm=g_mem_norm, branch_norm=g_branch_norm,
                       norm_post=g_norm_post)
    small_rows = sum(weights[n].size for n in SMALL) // 128
    small_sum = _allreduce_small(_pack([small_local[n] for n in SMALL], [jnp.pad(loss_local, ((0, 7), (0, 127)))]))
    for n, g in zip(SMALL, _unpack(small_sum, [weights[n] for n in SMALL])):
        grads[n] = g
    loss = small_sum[small_rows, 0]

    delta, new_m, new_v = {}, {}, {}
    packed = [small_sum if src is grads else _pack([src[n] for n in SMALL]) for src in (weights, grads, mom1, mom2)]
    outs = _elementwise(_adamw, packed, [F32, F32, F32], "adamw_small")
    for dst, o in zip((delta, new_m, new_v), outs):
        for n, a in zip(SMALL, _unpack(o, [weights[n] for n in SMALL])):
            dst[n] = a
    for n in LARGE:
        cols = weights[n].shape[-1]
        outs = _elementwise(_adamw, [src[n].reshape(-1, cols) for src in (weights, grads, mom1, mom2)],
                            [F32, F32, F32], "adamw_" + n)
        for dst, o in zip((delta, new_m, new_v), outs):
            dst[n] = o.reshape(weights[n].shape)

    return (loss, grad_x[None], *[grads[n] for n in ORDER], *[delta[n] for n in ORDER],
            *[new_m[n] for n in ORDER], *[new_v[n] for n in ORDER])
```

```python
import functools

import jax
import jax.numpy as jnp
from jax import lax
from jax.experimental import pallas as pl
from jax.experimental.pallas import tpu as pltpu

F32 = jnp.float32
BF16 = jnp.bfloat16
EPS = 1e-6
MESH = pl.DeviceIdType.MESH
ANY = pl.BlockSpec(memory_space=pl.ANY)

POOL_WINDOWS = (2, 4, 8, 16)
GROUP = 256
HALO = 16
CHUNK = 128
N_SGU_HEADS = 8
N_ATT_HEADS = 4
ATT_DIM = 256
WIDTH = 1024
ATT_SCALE = 1.0 / 16.0

ADAM_LR = 0.001
ADAM_B1 = 0.9
ADAM_B2 = 0.999
ADAM_EPS = 1e-08
ADAM_WD = 0.01
ADAM_STEP = 10

VMEM_LIMIT = 60 * 1024 * 1024
ELEMENTWISE_VMEM = 24 * 1024 * 1024
LANES = 128
BF16_ROWS = 16
MAX_PARTS = 4


def _params(n_grid_axes, vmem=VMEM_LIMIT):
    return pltpu.CompilerParams(dimension_semantics=("arbitrary",) * n_grid_axes, vmem_limit_bytes=vmem)


def _dot(a, b, dims):
    return lax.dot_general(a, b, (dims, ((), ())), preferred_element_type=F32)


NN = ((1,), (0,))
NT = ((1,), (1,))
TN = ((0,), (0,))


class _Rider:
    def __init__(self, inputs, out_shapes, n_sems, run, aliases=None):
        self.inputs, self.out_shapes, self.n_sems, self.run = list(inputs), list(out_shapes), n_sems, run
        self.aliases = aliases or {}


def _call(body, name, grid, in_specs, out_specs, out_shape, scratch_shapes, inputs, rider=None, prefetch=None,
          aliases=None, rider_refs=False):
    n_in, n_out, n_scr = len(in_specs), len(out_specs), len(scratch_shapes)
    r_in = len(rider.inputs) if rider else 0
    r_out = len(rider.out_shapes) if rider else 0
    n_pre = 0 if prefetch is None else 1

    def whole_body(*refs):
        pre, refs = refs[:n_pre], refs[n_pre:]
        ins, rider_ins = refs[:n_in], refs[n_in:n_in + r_in]
        refs = refs[n_in + r_in:]
        outs, rider_outs = refs[:n_out], refs[n_out:n_out + r_out]
        refs = refs[n_out + r_out:]
        scratch, sems = refs[:n_scr], refs[n_scr:]
        extra = {"rider_outs": rider_outs} if rider_refs else {}
        if rider is None:
            body(*pre, *ins, *outs, *scratch, **extra)
            return
        ids = [pl.program_id(ax) for ax in range(len(grid))]
        first = functools.reduce(lambda p, q: p & q, [i == 0 for i in ids])
        last = functools.reduce(lambda p, q: p & q, [i == g - 1 for i, g in zip(ids, grid)])

        @pl.when(first)
        def _():
            rider.run(rider_ins, rider_outs, *sems, True)

        body(*pre, *ins, *outs, *scratch, **extra)

        @pl.when(last)
        def _():
            rider.run(rider_ins, rider_outs, *sems, False)

    io_aliases = {n_pre + i: o for i, o in (aliases or {}).items()}
    scratch_all = list(scratch_shapes)
    if rider:
        io_aliases.update({n_pre + n_in + i: n_out + o for i, o in rider.aliases.items()})
        scratch_all += [pltpu.SemaphoreType.DMA((rider.n_sems,)), pltpu.SemaphoreType.DMA((rider.n_sems,))]
    specs = dict(grid=grid, in_specs=list(in_specs) + [ANY] * r_in, out_specs=tuple(out_specs) + (ANY,) * r_out,
                 scratch_shapes=scratch_all)
    if n_pre:
        specs = dict(grid_spec=pltpu.PrefetchScalarGridSpec(num_scalar_prefetch=1, **specs))
    outs = pl.pallas_call(
        whole_body, name=name, **specs,
        out_shape=tuple(out_shape) + tuple(rider.out_shapes if rider else ()),
        input_output_aliases=io_aliases, compiler_params=_params(len(grid)),
    )(*([prefetch] if n_pre else []), *inputs, *(rider.inputs if rider else []))
    return tuple(outs)


def _grad_rows(a_t, b, pos, row_of, m, tm, tn, tk, name, rider=None):
    k, n = a_t.shape[1], b.shape[1]
    nk = k // tk
    out_dtype, dims, a = BF16, NN, a_t
    a_spec = pl.BlockSpec((tm, tk), lambda i, j, kk, p: (row_of(i, p), kk))
    b_spec = pl.BlockSpec((tk, tn), lambda i, j, kk, p: (kk, j))

    def body(pos_ref, a_ref, b_ref, o_ref, *acc):
        part = lambda: _dot(a_ref[...], b_ref[...], dims)
        if nk == 1:
            o_ref[...] = part().astype(out_dtype)
            return
        (acc_ref,) = acc
        kk = pl.program_id(2)

        @pl.when(kk == 0)
        def _():
            acc_ref[...] = part()

        @pl.when((kk > 0) & (kk < nk - 1))
        def _():
            acc_ref[...] += part()

        @pl.when(kk == nk - 1)
        def _():
            o_ref[...] = (acc_ref[...] + part()).astype(out_dtype)

    return _call(body, name, (m // tm, n // tn, nk), [a_spec, b_spec],
                 [pl.BlockSpec((tm, tn), lambda i, j, kk, p: (i, j))], [jax.ShapeDtypeStruct((m, n), out_dtype)],
                 [pltpu.VMEM((tm, tn), F32)] if nk > 1 else [], [a, b], rider, prefetch=pos)


def _proj_piece(chips, first, n_shards, src, g_pre, w_in, proj_in, n_cols, rider, tm, name, casts=()):
    s, d = src.shape
    cols = n_cols // 4
    fused = g_pre is not None
    n_steps = s // tm

    def body(chips_ref, *refs, rider_outs=()):
        refs = list(refs)
        src_ref = refs.pop(0)
        g_ref = refs.pop(0) if fused else None
        w_ref = refs.pop(0) if w_in is not None else rider_outs[0]
        if proj_in is not None:
            refs.pop(0)
        shard_refs = [refs.pop(0) for _ in casts]
        proj_ref = refs.pop(0)
        h_ref, ht_ref = (refs.pop(0), refs.pop(0)) if fused else (None, None)
        for shard_ref in shard_refs:
            refs.pop(0)[...] = shard_ref[...].astype(BF16)
        wbuf, sem = refs
        q, i = pl.program_id(0), pl.program_id(1)

        @pl.when(i == 0)
        def _():
            at = pl.multiple_of(chips_ref[first + q] * cols, LANES)
            cp = pltpu.make_async_copy(w_ref.at[:, pl.ds(at, cols)], wbuf, sem)
            cp.start()
            cp.wait()

        if fused:
            xv = src_ref[...]
            r = lax.rsqrt(jnp.mean(xv * xv, axis=-1, keepdims=True) + EPS)
            h = (xv * r * g_ref[...]).astype(BF16)
            h_ref[...] = h
            ht_ref[...] = h.T
        else:
            h = src_ref[...]
        proj_ref[...] = _dot(h, wbuf[...], NN)

    row = lambda q, i, ch: (i, 0)
    inputs, in_specs = [src], [pl.BlockSpec((tm, d), row)]
    if fused:
        inputs.append(g_pre)
        in_specs.append(pl.BlockSpec((1, d), lambda q, i, ch: (0, 0)))
    if w_in is not None:
        inputs.append(w_in)
        in_specs.append(ANY)
    aliases = {}
    if proj_in is not None:
        aliases[len(inputs)] = 0
        inputs.append(proj_in)
        in_specs.append(ANY)
    out_specs = [pl.BlockSpec((tm, cols), lambda q, i, ch: (i, ch[first + q]))]
    out_shape = [jax.ShapeDtypeStruct((s, n_cols), F32)]
    if fused:
        assert n_shards == 1
        out_specs += [pl.BlockSpec((tm, d), row), pl.BlockSpec((d, tm), lambda q, i, ch: (0, i))]
        out_shape += [jax.ShapeDtypeStruct((s, d), BF16), jax.ShapeDtypeStruct((d, s), BF16)]
    for shard, kind in casts:
        assert n_shards == 1
        inputs.append(shard)
        if kind == 3:
            in_specs.append(pl.BlockSpec(shard.shape, lambda q, i, ch: (0, 0, 0)))
            out_specs.append(pl.BlockSpec(shard.shape, lambda q, i, ch: (0, ch[0], 0)))
            out_shape.append(jax.ShapeDtypeStruct((shard.shape[0], 4 * shard.shape[1], shard.shape[2]), BF16))
        else:
            block = (shard.shape[0] // n_steps, shard.shape[1])
            in_specs.append(pl.BlockSpec(block, row))
            out_specs.append(pl.BlockSpec(block, lambda q, i, ch: (ch[0] * n_steps + i, 0)))
            out_shape.append(jax.ShapeDtypeStruct((4 * shard.shape[0], shard.shape[1]), BF16))
    return _call(body, name, (n_shards, s // tm), in_specs, out_specs, out_shape,
                 [pltpu.VMEM((d, cols), BF16), pltpu.SemaphoreType.DMA(())], inputs, rider, prefetch=chips,
                 aliases=aliases, rider_refs=True)


def _kv_fwd(mem, g, w_kv):
    m, d = mem.shape

    def body(mem_ref, g_ref, w_ref, k_ref, v_ref):
        mv = mem_ref[...]
        r = lax.rsqrt(jnp.mean(mv * mv, axis=-1, keepdims=True) + EPS)
        mem_n = (mv * r * g_ref[...]).astype(BF16)
        kv = _dot(mem_n, w_ref[...], NN)
        k_ref[...] = kv[:, :WIDTH].astype(BF16)
        v_ref[...] = kv[:, WIDTH:].astype(BF16)

    return pl.pallas_call(
        body, name="kv_fwd",
        out_shape=(jax.ShapeDtypeStruct((m, WIDTH), BF16), jax.ShapeDtypeStruct((m, WIDTH), BF16)),
        compiler_params=_params(0),
    )(mem, g, w_kv)


def _kv_bwd(mem, g, w_kv, dk, dv):
    m, d = mem.shape
    n = w_kv.shape[1]
    col = 512

    def body(mem_ref, g_ref, w_ref, dk_ref, dv_ref, dw_ref, dg_ref):
        mv = mem_ref[...]
        r = lax.rsqrt(jnp.mean(mv * mv, axis=-1, keepdims=True) + EPS)
        mem_hat = mv * r
        mem_n = (mem_hat * g_ref[...]).astype(BF16)
        dkv = jnp.concatenate([dk_ref[...], dv_ref[...]], axis=1).astype(BF16)
        for j in range(n // col):
            dw_ref[:, j * col:(j + 1) * col] = _dot(mem_n, dkv[:, j * col:(j + 1) * col], TN).astype(BF16)
        dmem_n = _dot(dkv, w_ref[...], NT)
        dg_ref[...] = jnp.sum(dmem_n * mem_hat, axis=0, keepdims=True)

    return pl.pallas_call(
        body, name="kv_bwd",
        out_shape=(jax.ShapeDtypeStruct((d, n), BF16), jax.ShapeDtypeStruct((1, d), F32)),
        compiler_params=_params(0),
    )(mem, g, w_kv, dk, dv)


def _sigmoid(x):
    return 1.0 / (1.0 + jnp.exp(-x))


def _inv_counts(t0, t):
    pos = (t0 + lax.broadcasted_iota(jnp.int32, (t, 1), 0) + 1).astype(F32)
    return [1.0 / jnp.minimum(pos, float(w)) for w in POOL_WINDOWS]


def _window_sums(ext, t, backward):
    n = t + HALO
    parts = []
    for gi, w in enumerate(POOL_WINDOWS):
        s = ext[:, gi * GROUP:(gi + 1) * GROUP]
        k = 1
        while k < w:
            s = s + pltpu.roll(s, (n - k) if backward else k, axis=0)
            k *= 2
        parts.append(s[:t] if backward else s[HALO:])
    return parts


def _pool_fwd(xa, halo, inv, pool_w):
    t = xa.shape[0]
    sums = _window_sums(jnp.concatenate([halo, xa], axis=0), t, backward=False)
    d = jnp.concatenate([sums[gi] * inv[gi] - xa[:, gi * GROUP:(gi + 1) * GROUP] for gi in range(4)], axis=1)
    d = d.astype(BF16)
    y = jnp.concatenate([_dot(d[:, gi * GROUP:(gi + 1) * GROUP], pool_w[gi], NN) for gi in range(4)], axis=1)
    return d, y


def _layernorm_fwd(v):
    mu = jnp.mean(v, axis=-1, keepdims=True)
    xc = v - mu
    rstd = lax.rsqrt(jnp.mean(xc * xc, axis=-1, keepdims=True) + EPS)
    return xc * rstd, rstd


def _tril_mask(transposed):
    r = lax.broadcasted_iota(jnp.int32, (CHUNK, CHUNK), 0)
    c = lax.broadcasted_iota(jnp.int32, (CHUNK, CHUNK), 1)
    return (r <= c) if transposed else (r >= c)


def _sgu_mix(w_ref, vals, transposed):
    t = vals.shape[0]
    mask = _tril_mask(transposed)
    ws = [jnp.where(mask, w_ref[h], 0.0).astype(BF16) for h in range(N_SGU_HEADS)]
    rows = []
    for ci in range(t // CHUNK):
        blk = vals[ci * CHUNK:(ci + 1) * CHUNK]
        rows.append(jnp.concatenate(
            [_dot(ws[h], blk[:, h * CHUNK:(h + 1) * CHUNK], NN) for h in range(N_SGU_HEADS)], axis=1))
    return jnp.concatenate(rows, axis=0)


def _attn_fwd(q, k, v):
    ps, os_ = [], []
    for h in range(N_ATT_HEADS):
        sl = slice(h * ATT_DIM, (h + 1) * ATT_DIM)
        s = _dot(q[:, sl], k[:, sl], NT) * ATT_SCALE
        s = s - jnp.max(s, axis=-1, keepdims=True)
        e = jnp.exp(s)
        p = e * (1.0 / jnp.sum(e, axis=-1, keepdims=True))
        ps.append(p)
        os_.append(_dot(p.astype(BF16), v[:, sl], NN))
    return ps, jnp.concatenate(os_, axis=1)


def _rms_branch(y_pre):
    r = lax.rsqrt(jnp.mean(y_pre * y_pre, axis=-1, keepdims=True) + EPS)
    return y_pre * r, r


def _branch_specs(t, n_tiles, order):
    width_in = 7 * WIDTH
    tile = lambda i: order(i)
    per_halo = t // HALO
    const2 = lambda i: (0, 0)
    const3 = lambda i: (0, 0, 0)
    return [
        pl.BlockSpec((t, width_in), lambda i: (tile(i), 0)),
        pl.BlockSpec((HALO, WIDTH), lambda i: (jnp.maximum(tile(i) * per_halo - 1, 0), 0)),
        pl.BlockSpec((4, GROUP, GROUP), const3),
        pl.BlockSpec((1, WIDTH), const2),
        pl.BlockSpec((1, WIDTH), const2),
        pl.BlockSpec((1, WIDTH), const2),
        pl.BlockSpec((N_SGU_HEADS, CHUNK, CHUNK), const3),
        pl.BlockSpec((CHUNK, WIDTH), const2),
        pl.BlockSpec((MEM_ROWS, WIDTH), const2),
        pl.BlockSpec((MEM_ROWS, WIDTH), const2),
        pl.BlockSpec((1, 3 * WIDTH), const2),
    ]


MEM_ROWS = 256


def _branches_fwd(proj, pool_w, pool_scale, ln_g, ln_b, sgu_w, bias_full, k, v, branch_norm, t, rider=None):
    s = proj.shape[0]
    n_tiles = s // t

    def body(proj_ref, halo_ref, pw_ref, ps_ref, lg_ref, lb_ref, sw_ref, sb_ref, k_ref, v_ref, bn_ref, y_ref, yt_ref):
        i = pl.program_id(0)
        col = lambda j: proj_ref[:, j * WIDTH:(j + 1) * WIDTH]

        def put(branch, y_pre):
            sl = slice(branch * WIDTH, (branch + 1) * WIDTH)
            val = (_rms_branch(y_pre)[0] * bn[:, sl]).astype(BF16)
            y_ref[:, sl] = val
            yt_ref[sl, :] = val.T

        bn = bn_ref[...]
        halo = jnp.where(i > 0, halo_ref[...], 0.0)
        _, y_pool = _pool_fwd(col(0), halo, _inv_counts(i * t, t), pw_ref[...])
        ga = col(1)
        ya = y_pool * ps_ref[...] * (ga * _sigmoid(ga))
        put(0, ya)
        vhat, _ = _layernorm_fwd(col(3))
        vn = (vhat * lg_ref[...] + lb_ref[...]).astype(BF16)
        z = _sgu_mix(sw_ref, vn, transposed=False) + jnp.tile(sb_ref[...], (t // CHUNK, 1))
        gb = col(4)
        yb = col(2) * z * (gb * _sigmoid(gb))
        put(1, yb)
        _, o = _attn_fwd(col(5).astype(BF16), k_ref[...], v_ref[...])
        gc = col(6)
        yc = o * (gc * _sigmoid(gc))
        put(2, yc)

    return _call(body, "branches_fwd", (n_tiles,), _branch_specs(t, n_tiles, lambda i: i),
                 [pl.BlockSpec((t, 3 * WIDTH), lambda i: (i, 0)), pl.BlockSpec((3 * WIDTH, t), lambda i: (0, i))],
                 [jax.ShapeDtypeStruct((s, 3 * WIDTH), BF16), jax.ShapeDtypeStruct((3 * WIDTH, s), BF16)], [],
                 [proj, proj, pool_w, pool_scale, ln_g, ln_b, sgu_w, bias_full, k, v, branch_norm], rider)


def _branches_bwd(proj, dy, pool_w, pool_scale, ln_g, ln_b, sgu_w, sgu_wt, bias_full, k, v, branch_norm, t, rider=None):
    s = proj.shape[0]
    n_tiles = s // t
    n_chunks = t // CHUNK
    order = lambda i: n_tiles - 1 - i

    def body(proj_ref, halo_ref, pw_ref, ps_ref, lg_ref, lb_ref, sw_ref, sb_ref, k_ref, v_ref, bn_ref,
             swt_ref, dy_ref,
             dproj_ref, dpw_ref, dps_ref, dlg_ref, dlb_ref, dsw_ref, dsb_ref, dbn_ref, dk_ref, dv_ref,
             carry_ref, dbias_ref):
        step = pl.program_id(0)
        i = order(step)

        @pl.when(step == 0)
        def _():
            for ref in (dpw_ref, dps_ref, dlg_ref, dlb_ref, dsw_ref, dbn_ref, dk_ref, dv_ref, carry_ref, dbias_ref):
                ref[...] = jnp.zeros(ref.shape, ref.dtype)

        col = lambda j: proj_ref[:, j * WIDTH:(j + 1) * WIDTH]
        bn = bn_ref[...]

        def norm_bwd(y_pre, sl):
            yhat, r = _rms_branch(y_pre)
            dyv = dy_ref[:, sl].astype(F32)
            dbn_ref[:, sl] += jnp.sum(dyv * yhat, axis=0, keepdims=True)
            dyhat = dyv * bn[:, sl]
            return r * (dyhat - yhat * jnp.mean(dyhat * yhat, axis=-1, keepdims=True))

        def gate(gv):
            sg = _sigmoid(gv)
            return gv * sg, sg * (1.0 + gv * (1.0 - sg))

        inv = _inv_counts(i * t, t)
        halo = jnp.where(i > 0, halo_ref[...], 0.0)
        pw = pw_ref[...]
        d, y_pool = _pool_fwd(col(0), halo, inv, pw)
        scale = ps_ref[...]
        silu_a, dsilu_a = gate(col(1))
        pa = y_pool * scale
        dya = norm_bwd(pa * silu_a, slice(0, WIDTH))
        dproj_ref[:, WIDTH:2 * WIDTH] = (dya * pa * dsilu_a).astype(BF16)
        dpa = dya * silu_a
        dps_ref[...] += jnp.sum(dpa * y_pool, axis=0, keepdims=True)
        dy_pool = (dpa * scale).astype(BF16)
        dd_parts, ddc_parts = [], []
        for gi in range(4):
            sl = slice(gi * GROUP, (gi + 1) * GROUP)
            dpw_ref[gi] += _dot(d[:, sl], dy_pool[:, sl], TN)
            dd = _dot(dy_pool[:, sl], pw[gi], NT)
            dd_parts.append(dd)
            ddc_parts.append(dd * inv[gi])
        ddc = jnp.concatenate(ddc_parts, axis=1)
        sums = _window_sums(jnp.concatenate([ddc, carry_ref[...]], axis=0), t, backward=True)
        carry_ref[...] = ddc[:HALO]
        dproj_ref[:, 0:WIDTH] = jnp.concatenate([sums[gi] - dd_parts[gi] for gi in range(4)], axis=1).astype(BF16)

        vhat, rstd = _layernorm_fwd(col(3))
        lg = lg_ref[...]
        vn = (vhat * lg + lb_ref[...]).astype(BF16)
        z = _sgu_mix(sw_ref, vn, transposed=False) + jnp.tile(sb_ref[...], (n_chunks, 1))
        u = col(2)
        silu_b, dsilu_b = gate(col(4))
        uz = u * z
        dyb = norm_bwd(uz * silu_b, slice(WIDTH, 2 * WIDTH))
        dproj_ref[:, 4 * WIDTH:5 * WIDTH] = (dyb * uz * dsilu_b).astype(BF16)
        duz = dyb * silu_b
        dproj_ref[:, 2 * WIDTH:3 * WIDTH] = (duz * z).astype(BF16)
        dz = duz * u
        dz_b = dz.astype(BF16)
        for ci in range(n_chunks):
            rows = slice(ci * CHUNK, (ci + 1) * CHUNK)
            dbias_ref[...] += dz[rows]
            for h in range(N_SGU_HEADS):
                sl = slice(h * CHUNK, (h + 1) * CHUNK)
                dsw_ref[h] += _dot(dz_b[rows, sl], vn[rows, sl], NT)
        dvn = _sgu_mix(swt_ref, dz_b, transposed=True)
        dlg_ref[...] += jnp.sum(dvn * vhat, axis=0, keepdims=True)
        dlb_ref[...] += jnp.sum(dvn, axis=0, keepdims=True)
        dvhat = dvn * lg
        dvb = rstd * (dvhat - jnp.mean(dvhat, axis=-1, keepdims=True)
                      - vhat * jnp.mean(dvhat * vhat, axis=-1, keepdims=True))
        dproj_ref[:, 3 * WIDTH:4 * WIDTH] = dvb.astype(BF16)

        q = col(5).astype(BF16)
        kv_k, kv_v = k_ref[...], v_ref[...]
        ps, o = _attn_fwd(q, kv_k, kv_v)
        silu_c, dsilu_c = gate(col(6))
        dyc = norm_bwd(o * silu_c, slice(2 * WIDTH, 3 * WIDTH))
        dproj_ref[:, 6 * WIDTH:7 * WIDTH] = (dyc * o * dsilu_c).astype(BF16)
        do = (dyc * silu_c).astype(BF16)
        dq_parts = []
        for h in range(N_ATT_HEADS):
            sl = slice(h * ATT_DIM, (h + 1) * ATT_DIM)
            p = ps[h]
            dp = _dot(do[:, sl], kv_v[:, sl], NT)
            ds = (p * (dp - jnp.sum(p * dp, axis=-1, keepdims=True)) * ATT_SCALE).astype(BF16)
            dq_parts.append(_dot(ds, kv_k[:, sl], NN))
            dk_ref[:, sl] += _dot(ds, q[:, sl], TN)
            dv_ref[:, sl] += _dot(p.astype(BF16), do[:, sl], TN)
        dproj_ref[:, 5 * WIDTH:6 * WIDTH] = jnp.concatenate(dq_parts, axis=1).astype(BF16)

        @pl.when(step == n_tiles - 1)
        def _():
            keep = _tril_mask(transposed=False)
            for h in range(N_SGU_HEADS):
                dsw_ref[h] = jnp.where(keep, dsw_ref[h], 0.0)
            dsb_ref[...] = jnp.concatenate(
                [jnp.sum(dbias_ref[:, h * CHUNK:(h + 1) * CHUNK], axis=1, keepdims=True)
                 for h in range(N_SGU_HEADS)], axis=1)

    const2 = lambda i: (0, 0)
    const3 = lambda i: (0, 0, 0)
    out_shapes = (
        jax.ShapeDtypeStruct((s, 7 * WIDTH), BF16),
        jax.ShapeDtypeStruct((4, GROUP, GROUP), F32),
        jax.ShapeDtypeStruct((1, WIDTH), F32),
        jax.ShapeDtypeStruct((1, WIDTH), F32),
        jax.ShapeDtypeStruct((1, WIDTH), F32),
        jax.ShapeDtypeStruct((N_SGU_HEADS, CHUNK, CHUNK), F32),
        jax.ShapeDtypeStruct((CHUNK, N_SGU_HEADS), F32),
        jax.ShapeDtypeStruct((1, 3 * WIDTH), F32),
        jax.ShapeDtypeStruct((MEM_ROWS, WIDTH), F32),
        jax.ShapeDtypeStruct((MEM_ROWS, WIDTH), F32),
    )
    out_specs = (
        pl.BlockSpec((t, 7 * WIDTH), lambda i: (order(i), 0)),
        pl.BlockSpec((4, GROUP, GROUP), const3),
        pl.BlockSpec((1, WIDTH), const2),
        pl.BlockSpec((1, WIDTH), const2),
        pl.BlockSpec((1, WIDTH), const2),
        pl.BlockSpec((N_SGU_HEADS, CHUNK, CHUNK), const3),
        pl.BlockSpec((CHUNK, N_SGU_HEADS), const2),
        pl.BlockSpec((1, 3 * WIDTH), const2),
        pl.BlockSpec((MEM_ROWS, WIDTH), const2),
        pl.BlockSpec((MEM_ROWS, WIDTH), const2),
    )
    in_specs = _branch_specs(t, n_tiles, order) + [
        pl.BlockSpec((N_SGU_HEADS, CHUNK, CHUNK), const3),
        pl.BlockSpec((t, 3 * WIDTH), lambda i: (order(i), 0)),
    ]
    return _call(body, "branches_bwd", (n_tiles,), in_specs, out_specs, out_shapes,
                 [pltpu.VMEM((HALO, WIDTH), F32), pltpu.VMEM((CHUNK, WIDTH), F32)],
                 [proj, proj, pool_w, pool_scale, ln_g, ln_b, sgu_w, bias_full, k, v, branch_norm, sgu_wt, dy], rider)


def _out_loss(y, w_out, x, target, g_post, tm):
    s, d = x.shape
    e_w = y.shape[1]
    n_tiles = s // tm

    def body(y_ref, w_ref, x_ref, t_ref, g_ref, loss_ref, dz_ref, dout_ref, dy_ref, dg_ref, sq_ref):
        i = pl.program_id(0)

        @pl.when(i == 0)
        def _():
            sq_ref[...] = jnp.zeros(sq_ref.shape, F32)
            dg_ref[...] = jnp.zeros(dg_ref.shape, F32)

        w = w_ref[...]
        out = _dot(y_ref[...], w, NN)
        r = lax.rsqrt(jnp.mean(out * out, axis=-1, keepdims=True) + EPS)
        outn = out * r
        g = g_ref[...]
        err = (x_ref[...] + outn * g) - t_ref[...]
        sq_ref[...] += jnp.sum(err * err, axis=0, keepdims=True)
        dz = err * (1.0 / d)
        dz_ref[...] = dz
        dg_ref[...] += jnp.sum(dz * outn, axis=0, keepdims=True)
        doutn = dz * g
        dout = (r * (doutn - outn * jnp.mean(doutn * outn, axis=-1, keepdims=True))).astype(BF16)
        dout_ref[...] = dout
        dy_ref[...] = _dot(dout, w, NT).astype(BF16)

        @pl.when(i == n_tiles - 1)
        def _():
            loss_ref[...] = 0.5 * jnp.sum(sq_ref[...], axis=1, keepdims=True) * (1.0 / d)

    row = lambda i: (i, 0)
    const2 = lambda i: (0, 0)
    return pl.pallas_call(
        body, name="out_loss", grid=(n_tiles,),
        in_specs=[
            pl.BlockSpec((tm, e_w), row),
            pl.BlockSpec((e_w, d), const2, pipeline_mode=pl.Buffered(1)),
            pl.BlockSpec((tm, d), row),
            pl.BlockSpec((tm, d), row),
            pl.BlockSpec((1, d), const2),
        ],
        out_specs=(
            pl.BlockSpec((1, 1), const2),
            pl.BlockSpec((tm, d), row),
            pl.BlockSpec((tm, d), row),
            pl.BlockSpec((tm, e_w), row),
            pl.BlockSpec((1, d), const2),
        ),
        out_shape=(
            jax.ShapeDtypeStruct((1, 1), F32),
            jax.ShapeDtypeStruct((s, d), F32),
            jax.ShapeDtypeStruct((s, d), BF16),
            jax.ShapeDtypeStruct((s, e_w), BF16),
            jax.ShapeDtypeStruct((1, d), F32),
        ),
        scratch_shapes=[pltpu.VMEM((1, d), F32)],
        compiler_params=_params(1),
    )(y, w_out, x, target, g_post)


def _dx_call(dproj, w_in, x, dz, g_pre, tm, tk, rider=None):
    s, d = x.shape
    k_total = dproj.shape[1]
    nk = k_total // tk
    n_tiles = s // tm

    def body(dp_ref, w_ref, x_ref, dz_ref, g_ref, dx_ref, dg_ref, acc_ref):
        i, kk = pl.program_id(0), pl.program_id(1)
        part = lambda: _dot(dp_ref[...], w_ref[...], NT)

        @pl.when(kk == 0)
        def _():
            acc_ref[...] = part()

        @pl.when((kk > 0) & (kk < nk - 1))
        def _():
            acc_ref[...] += part()

        @pl.when((i == 0) & (kk == 0))
        def _():
            dg_ref[...] = jnp.zeros(dg_ref.shape, F32)

        @pl.when(kk == nk - 1)
        def _():
            dh = acc_ref[...] + part()
            xv = x_ref[...]
            r = lax.rsqrt(jnp.mean(xv * xv, axis=-1, keepdims=True) + EPS)
            xhat = xv * r
            dg_ref[...] += jnp.sum(dh * xhat, axis=0, keepdims=True)
            dxhat = dh * g_ref[...]
            dx_ref[...] = dz_ref[...] + r * (dxhat - xhat * jnp.mean(dxhat * xhat, axis=-1, keepdims=True))

    row = lambda i, kk: (i, 0)
    const2 = lambda i, kk: (0, 0)
    return _call(
        body, "dx", (n_tiles, nk),
        [
            pl.BlockSpec((tm, tk), lambda i, kk: (i, kk)),
            pl.BlockSpec((d, tk), lambda i, kk: (0, kk)),
            pl.BlockSpec((tm, d), row),
            pl.BlockSpec((tm, d), row),
            pl.BlockSpec((1, d), const2),
        ],
        [pl.BlockSpec((tm, d), row), pl.BlockSpec((1, d), const2)],
        [jax.ShapeDtypeStruct((s, d), F32), jax.ShapeDtypeStruct((1, d), F32)],
        [pltpu.VMEM((tm, d), F32)], [dproj, w_in, x, dz, g_pre], rider)


def _rows_tile(rows, cols, n_arrays, itemsize=4):
    budget = ELEMENTWISE_VMEM // (2 * n_arrays * cols * itemsize)
    if rows <= budget:
        return rows
    best = None
    for cand in range(16, rows + 1, 16):
        if rows % cand == 0 and cand <= max(budget, 16):
            best = cand
    return best if best is not None else rows


def _elementwise(fn, inputs, out_dtypes, name):
    rows, cols = inputs[0].shape
    tr = _rows_tile(rows, cols, len(inputs) + len(out_dtypes))
    n_in = len(inputs)

    def body(*refs):
        outs = fn(*[r[...] for r in refs[:n_in]])
        for o_ref, o in zip(refs[n_in:], outs):
            o_ref[...] = o.astype(o_ref.dtype)

    spec = pl.BlockSpec((tr, cols), lambda i: (i, 0))
    return pl.pallas_call(
        body, name=name, grid=(rows // tr,),
        in_specs=[spec] * n_in, out_specs=tuple([spec] * len(out_dtypes)),
        out_shape=tuple(jax.ShapeDtypeStruct((rows, cols), dt) for dt in out_dtypes),
        compiler_params=_params(1),
    )(*inputs)


def _blockwise(fn, pos, inputs, in_specs, out_shape, out_spec, grid, name):
    n_in = len(inputs)

    def body(pos_ref, *refs):
        o_ref = refs[n_in]
        (out,) = fn(*[r[...].reshape(o_ref.shape) for r in refs[:n_in]])
        o_ref[...] = out.astype(o_ref.dtype)

    return pl.pallas_call(
        body, name=name,
        grid_spec=pltpu.PrefetchScalarGridSpec(num_scalar_prefetch=1, grid=grid, in_specs=in_specs,
                                               out_specs=out_spec),
        out_shape=out_shape,
        compiler_params=_params(len(grid)),
    )(pos, *inputs)


def _cast_copy(x):
    return (x,)


def _pair_sum(mine, theirs):
    return ((mine.astype(F32) + theirs.astype(F32)),)


def _four_sum(own, t0, t1, t2):
    return ((((own.astype(F32) + t0.astype(F32)) + t1.astype(F32)) + t2.astype(F32)),)


def _adamw(w, g, m, v):
    m = ADAM_B1 * m + (1.0 - ADAM_B1) * g
    v = ADAM_B2 * v + (1.0 - ADAM_B2) * jnp.square(g)
    m_hat = m / (1.0 - ADAM_B1 ** ADAM_STEP)
    v_hat = v / (1.0 - ADAM_B2 ** ADAM_STEP)
    delta = -ADAM_LR * (m_hat / (jnp.sqrt(v_hat) + ADAM_EPS) + ADAM_WD * w)
    return delta, m, v


def _place():
    x, y, c = lax.axis_index("x"), lax.axis_index("y"), lax.axis_index("c")
    chips = [(1 - x, y), (x, 1 - y), (1 - x, 1 - y)]
    return x, y, c, chips


def _remote(src, dst, send_sem, recv_sem, to):
    return pltpu.make_async_remote_copy(src_ref=src, dst_ref=dst, send_sem=send_sem, recv_sem=recv_sem,
                                        device_id=to, device_id_type=MESH)


def _split(ref, plan):
    views = [ref]
    for axis, parts in plan:
        size = ref.shape[axis] // parts
        assert size * parts == ref.shape[axis]
        views = [v.at[tuple(pl.ds(q * size, size) if i == axis else slice(None) for i in range(len(ref.shape)))]
                 for v in views for q in range(parts)]
    return views


def _started(src, dst, send_sem, recv_sem, to):
    copy = _remote(src, dst, send_sem, recv_sem, to)
    copy.start()
    return copy


def _hbm_call(body, name, inputs, out_shapes, scratch, aliases=None):
    return pl.pallas_call(
        body, name=name,
        in_specs=[ANY] * len(inputs), out_specs=tuple([ANY] * len(out_shapes)), out_shape=tuple(out_shapes),
        scratch_shapes=scratch, input_output_aliases=aliases or {},
        compiler_params=pltpu.CompilerParams(has_side_effects=True),
    )(*inputs)


def _shard_half(kind, ref, chip, cc):
    if kind == 0:
        rows, cols = ref.shape[0] // 2, ref.shape[1] // 4
        return ref.at[pl.ds(cc * rows, rows), pl.ds(pl.multiple_of(chip * cols, LANES), cols)]
    if kind == 3:
        rows = ref.shape[1] // 8
        return ref.at[:, pl.ds(pl.multiple_of((2 * chip + cc) * rows, BF16_ROWS), rows), :]
    rows = ref.shape[0] // 8
    return ref.at[pl.ds(pl.multiple_of((2 * chip + cc) * rows, BF16_ROWS), rows), :]


def _relay_rider(full):
    kind = 0

    def quarter(ref, chip_no, cc, q):
        return _split(_shard_half(kind, ref, chip_no, cc), [(0, 2)])[q]

    def run(in_refs, full_refs, send_sems, recv_sems, start):
        (ref,) = full_refs
        x, y, c, chips = _place()
        sibling = (x, y, 1 - c)
        chip_no = [2 * ch[0] + ch[1] for ch in chips]
        if start:
            for p in (0, 1):
                held = quarter(ref, chip_no[1 - p], c, p)
                _remote(held, held, send_sems.at[p], recv_sems.at[p], (*chips[p], c)).start()
            return
        for p in (0, 1):
            landed = quarter(ref, chip_no[2], c, p)
            _remote(landed, landed, send_sems.at[p], recv_sems.at[p], (*chips[p], c)).wait_recv()
            _remote(landed, landed, send_sems.at[2], recv_sems.at[2], sibling).start()
        mine, theirs = _shard_half(kind, ref, chip_no[2], c), _shard_half(kind, ref, chip_no[2], 1 - c)
        _remote(mine, mine, send_sems.at[2], recv_sems.at[2], sibling).wait_send()
        _remote(theirs, theirs, send_sems.at[2], recv_sems.at[2], sibling).wait_recv()
        for p in (0, 1):
            held = quarter(ref, chip_no[1 - p], c, p)
            _remote(held, held, send_sems.at[p], recv_sems.at[p], (*chips[p], c)).wait_send()

    return _Rider([full], [jax.ShapeDtypeStruct(full.shape, full.dtype)], 3, run, aliases={0: 0})


def _riders(riders):
    def bounds(counts):
        ends = [sum(counts[:i + 1]) for i in range(len(counts))]
        return list(zip([0] + ends[:-1], ends))

    ins = bounds([len(r.inputs) for r in riders])
    outs = bounds([len(r.out_shapes) for r in riders])
    sems = bounds([r.n_sems for r in riders])

    class From:
        def __init__(self, sem_refs, base):
            self.sem_refs, self.base, self.at = sem_refs, base, self

        def __getitem__(self, k):
            return self.sem_refs.at[self.base + k]

    def run(in_refs, out_refs, send_sems, recv_sems, start):
        for r, (i0, i1), (o0, o1), (s0, _) in zip(riders, ins, outs, sems):
            r.run(in_refs[i0:i1], out_refs[o0:o1], From(send_sems, s0), From(recv_sems, s0), start)

    aliases = {}
    for r, (i0, _), (o0, _) in zip(riders, ins, outs):
        aliases.update({i0 + i: o0 + o for i, o in r.aliases.items()})
    return _Rider([a for r in riders for a in r.inputs], [o for r in riders for o in r.out_shapes],
                  sems[-1][1], run, aliases)


def _gather_rider(fulls, kinds, peers=(0, 1, 2)):
    n = len(fulls)
    full_half = _shard_half

    def run(in_refs, full_refs, send_sems, recv_sems, start):
        x, y, c, chips = _place()
        me = 2 * x + y
        sibling = (x, y, 1 - c)
        plans = [[(0, MAX_PARTS)], [(0, 2)], [(0, 2)], []]
        chips = [(p, chips[p]) for p in peers]
        across = lambda a, p, k: (3 * a + p) * MAX_PARTS + k
        onward = lambda a, p: 3 * n * MAX_PARTS + 3 * a + p

        def parts(a, chip_no, cc):
            return _split(full_half(kinds[a], full_refs[a], chip_no, cc), plans[kinds[a]])

        if start:
            for p, chip in chips:
                for a in range(n):
                    for k, mine in enumerate(parts(a, me, c)):
                        _remote(mine, mine, send_sems.at[across(a, p, k)], recv_sems.at[across(a, p, k)],
                                (*chip, c)).start()
            return
        for k in range(MAX_PARTS):
            for p, chip in chips:
                for a in range(n):
                    landed = parts(a, 2 * chip[0] + chip[1], c)
                    if k < len(landed):
                        _remote(landed[k], landed[k], send_sems.at[across(a, p, k)], recv_sems.at[across(a, p, k)],
                                (*chip, c)).wait_recv()
                        _remote(landed[k], landed[k], send_sems.at[onward(a, p)], recv_sems.at[onward(a, p)],
                                sibling).start()
        for p, chip in chips:
            them = 2 * chip[0] + chip[1]
            for a in range(n):
                passed = full_half(kinds[a], full_refs[a], them, 1 - c)
                _remote(passed, passed, send_sems.at[onward(a, p)], recv_sems.at[onward(a, p)], sibling).wait_recv()
                landed = full_half(kinds[a], full_refs[a], them, c)
                _remote(landed, landed, send_sems.at[onward(a, p)], recv_sems.at[onward(a, p)], sibling).wait_send()
                for k, mine in enumerate(parts(a, me, c)):
                    _remote(mine, mine, send_sems.at[across(a, p, k)], recv_sems.at[across(a, p, k)],
                            (*chip, c)).wait_send()

    return _Rider(fulls, [jax.ShapeDtypeStruct(f.shape, f.dtype) for f in fulls], 3 * n * (MAX_PARTS + 1), run,
                  aliases={a: a for a in range(n)})


def _exchange_halves(grads, name):
    n = len(grads)
    arrays = [g for g, _ in grads]
    out_shapes = [jax.ShapeDtypeStruct(tuple(1 if i == ax else dim for i, dim in enumerate(g.shape)), g.dtype)
                  for g, ax in grads]

    def half(ref, ax, cc):
        idx = tuple(pl.ds(cc, 1) if i == ax else slice(None) for i in range(len(ref.shape)))
        return ref.at[idx]

    def body(*refs):
        in_refs, out_refs = refs[:n], refs[n:2 * n]
        send_sems, recv_sems = refs[2 * n:]
        x, y, c, _ = _place()
        sibling = (x, y, 1 - c)
        copies = [_started(half(in_refs[a], grads[a][1], 1 - c), out_refs[a], send_sems.at[a], recv_sems.at[a],
                           sibling) for a in range(n)]
        for rem in copies:
            rem.wait()

    return _hbm_call(body, name, arrays, out_shapes,
                     [pltpu.SemaphoreType.DMA((n,)), pltpu.SemaphoreType.DMA((n,))])


def _exchange_rider(arrays):
    n = len(arrays)

    def run(in_refs, out_refs, send_sems, recv_sems, start):
        x, y, c, _ = _place()
        sibling = (x, y, 1 - c)
        for a in range(n):
            sems = (send_sems.at[a], recv_sems.at[a])
            if start:
                _started(in_refs[a], out_refs[a], *sems, sibling)
            else:
                _remote(in_refs[a], out_refs[a], *sems, sibling).wait()

    return _Rider(arrays, [jax.ShapeDtypeStruct(a.shape, a.dtype) for a in arrays], n, run)


def _scatter_rider(parts):
    n = len(parts)
    arrays = [p for p, _ in parts]

    def block_shape(p, ax):
        if ax == len(p.shape) - 1:
            return p.shape[:-1] + (p.shape[-1] // 4,)
        return tuple(1 if i == ax else dim for i, dim in enumerate(p.shape))

    out_shapes = [jax.ShapeDtypeStruct((3,) + block_shape(p, ax), p.dtype) for p, ax in parts]

    def block(ref, ax, chip):
        rank = len(ref.shape)
        if ax == rank - 1:
            cols = ref.shape[-1] // 4
            last = pl.ds(pl.multiple_of(chip * cols, LANES), cols)
            return ref.at[tuple([slice(None)] * (rank - 1) + [last])]
        return ref.at[tuple(pl.ds(chip, 1) if i == ax else slice(None) for i in range(rank))]

    def run(in_refs, out_refs, send_sems, recv_sems, start):
        x, y, c, chips = _place()
        for a in range(n):
            ax = parts[a][1]
            for p, chip in enumerate(chips):
                src, dst = block(in_refs[a], ax, 2 * chip[0] + chip[1]), out_refs[a].at[p]
                sems = (send_sems.at[3 * a + p], recv_sems.at[3 * a + p])
                if start:
                    _started(src, dst, *sems, (*chip, c))
                else:
                    _remote(src, dst, *sems, (*chip, c)).wait()

    return _Rider(arrays, out_shapes, 3 * n, run)


def _join_rider(joined):
    n = len(joined)
    arrays = [j for j, _ in joined]

    def run(in_refs, out_refs, send_sems, recv_sems, start):
        x, y, c, _ = _place()
        sibling = (x, y, 1 - c)

        def half(a, cc):
            rank = len(out_refs[a].shape)
            return out_refs[a].at[tuple(pl.ds(cc, 1) if i == joined[a][1] else slice(None) for i in range(rank))]

        for a in range(n):
            sems = (send_sems.at[a], recv_sems.at[a])
            if start:
                _started(half(a, c), half(a, c), *sems, sibling)
            else:
                _remote(half(a, c), half(a, c), *sems, sibling).wait_send()
                _remote(half(a, 1 - c), half(a, 1 - c), *sems, sibling).wait_recv()

    return _Rider(arrays, [jax.ShapeDtypeStruct(j.shape, j.dtype) for j in arrays], n, run,
                  aliases={a: a for a in range(n)})


def _allreduce_small(packed, rider):
    rows, lanes = packed.shape
    half = rows // 2
    r_in, r_out = len(rider.inputs), len(rider.out_shapes)

    def body(in_ref, *refs):
        rider_ins, out_ref, rider_outs = refs[:r_in], refs[r_in], refs[r_in + 1:r_in + 1 + r_out]
        pair_ref, gath_ref, send_sems, recv_sems, rider_send, rider_recv = refs[r_in + 1 + r_out:]
        rider.run(rider_ins, rider_outs, rider_send, rider_recv, True)
        x, y, c, chips = _place()
        me = 2 * x + y
        sibling = (x, y, 1 - c)
        mine = pl.ds(pl.multiple_of(c * half, 8), half)
        theirs = pl.ds(pl.multiple_of((1 - c) * half, 8), half)
        to_sib = _remote(in_ref.at[theirs], pair_ref, send_sems.at[0], recv_sems.at[0], sibling)
        to_sib.start()
        to_sib.wait()
        gath_ref[me] = in_ref[mine] + pair_ref[...]
        sends = [_remote(gath_ref.at[me], gath_ref.at[me], send_sems.at[1 + p], recv_sems.at[1 + p], (*chip, c))
                 for p, chip in enumerate(chips)]
        for cp in sends:
            cp.start()
        for p, chip in enumerate(chips):
            slot = gath_ref.at[2 * chip[0] + chip[1]]
            _remote(slot, slot, send_sems.at[1 + p], recv_sems.at[1 + p], (*chip, c)).wait_recv()
        for cp in sends:
            cp.wait_send()
        out_ref[mine] = ((gath_ref[0] + gath_ref[1]) + gath_ref[2]) + gath_ref[3]
        back = _remote(out_ref.at[mine], out_ref.at[mine], send_sems.at[4], recv_sems.at[4], sibling)
        back.start()
        back.wait_send()
        _remote(out_ref.at[theirs], out_ref.at[theirs], send_sems.at[4], recv_sems.at[4], sibling).wait_recv()
        rider.run(rider_ins, rider_outs, rider_send, rider_recv, False)

    vmem = pl.BlockSpec(memory_space=pltpu.VMEM)
    return pl.pallas_call(
        body, name="allreduce_small",
        in_specs=[vmem] + [ANY] * r_in, out_specs=(vmem,) + (ANY,) * r_out,
        out_shape=(jax.ShapeDtypeStruct((rows, lanes), F32),) + tuple(rider.out_shapes),
        scratch_shapes=[pltpu.VMEM((half, lanes), F32), pltpu.VMEM((4, half, lanes), F32),
                        pltpu.SemaphoreType.DMA((5,)), pltpu.SemaphoreType.DMA((5,)),
                        pltpu.SemaphoreType.DMA((rider.n_sems,)), pltpu.SemaphoreType.DMA((rider.n_sems,))],
        input_output_aliases={1 + i: 1 + o for i, o in rider.aliases.items()},
        compiler_params=pltpu.CompilerParams(has_side_effects=True, vmem_limit_bytes=32 * 1024 * 1024),
    )(packed, *rider.inputs)


SMALL = ("norm_pre", "pool_scale", "sgu_ln_g", "sgu_ln_b", "sgu_w", "sgu_b", "mem_norm", "branch_norm", "norm_post")
LARGE = ("w_in", "pool_w", "w_kv", "w_out")
ORDER = ("norm_pre", "w_in", "pool_w", "pool_scale", "sgu_ln_g", "sgu_ln_b", "sgu_w", "sgu_b", "mem_norm", "w_kv",
         "branch_norm", "w_out", "norm_post")


def _pack(arrays, extra=()):
    rows = [a.reshape(-1, 128) for a in arrays] + list(extra)
    pad = -sum(r.shape[0] for r in rows) % 16
    return jnp.concatenate(rows + ([jnp.zeros((pad, 128), F32)] if pad else []), axis=0)


def _unpack(packed, like):
    out, row = [], 0
    for a in like:
        rows = a.size // 128
        out.append(packed[row:row + rows].reshape(a.shape))
        row += rows
    return out


def kernel(x, mem, norm_pre, w_in, pool_w, pool_scale, sgu_ln_g, sgu_ln_b, sgu_w, sgu_b, mem_norm, w_kv, branch_norm, w_out, norm_post, loss_target, m_norm_pre, m_w_in, m_pool_w, m_pool_scale, m_sgu_ln_g, m_sgu_ln_b, m_sgu_w, m_sgu_b, m_mem_norm, m_w_kv, m_branch_norm, m_w_out, m_norm_post, v_norm_pre, v_w_in, v_pool_w, v_pool_scale, v_sgu_ln_g, v_sgu_ln_b, v_sgu_w, v_sgu_b, v_mem_norm, v_w_kv, v_branch_norm, v_w_out, v_norm_post):
    weights = dict(norm_pre=norm_pre, w_in=w_in, pool_w=pool_w, pool_scale=pool_scale, sgu_ln_g=sgu_ln_g,
                   sgu_ln_b=sgu_ln_b, sgu_w=sgu_w, sgu_b=sgu_b, mem_norm=mem_norm, w_kv=w_kv, branch_norm=branch_norm,
                   w_out=w_out, norm_post=norm_post)
    mom1 = dict(norm_pre=m_norm_pre, w_in=m_w_in, pool_w=m_pool_w, pool_scale=m_pool_scale, sgu_ln_g=m_sgu_ln_g,
                sgu_ln_b=m_sgu_ln_b, sgu_w=m_sgu_w, sgu_b=m_sgu_b, mem_norm=m_mem_norm, w_kv=m_w_kv,
                branch_norm=m_branch_norm, w_out=m_w_out, norm_post=m_norm_post)
    mom2 = dict(norm_pre=v_norm_pre, w_in=v_w_in, pool_w=v_pool_w, pool_scale=v_pool_scale, sgu_ln_g=v_sgu_ln_g,
                sgu_ln_b=v_sgu_ln_b, sgu_w=v_sgu_w, sgu_b=v_sgu_b, mem_norm=v_mem_norm, w_kv=v_w_kv,
                branch_norm=v_branch_norm, w_out=v_w_out, norm_post=v_norm_post)

    s, d = x.shape[1], x.shape[2]
    x2, mem2, tgt2 = x[0], mem[0], loss_target[0]
    t_branch = min(256, s)
    tm = min(512, s)

    core = lax.axis_index("c")
    chip = 2 * lax.axis_index("x") + lax.axis_index("y")
    pos = jnp.stack([core, chip]).astype(jnp.int32)
    n_in, n_kv, n_out = 4 * w_in.shape[2], 4 * w_kv.shape[1], 4 * w_out.shape[1]
    wi_rows, kv_rows, wo_rows = d // 8, n_kv // 8, n_out // 8

    kv_cols, pw_rows = w_kv.shape[2], GROUP // 8
    wi_own = _blockwise(_cast_copy, pos, [w_in[0]], [pl.BlockSpec((wi_rows, n_in // 4), lambda i, p: (i, 0))],
                        jax.ShapeDtypeStruct((d, n_in), BF16),
                        pl.BlockSpec((wi_rows, n_in // 4), lambda i, p: (i, p[1])), (8,), "place_w_in")

    x_pos, y_pos = lax.axis_index("x"), lax.axis_index("y")
    chips = jnp.stack([chip, 2 * (1 - x_pos) + y_pos, 2 * x_pos + 1 - y_pos,
                       2 * (1 - x_pos) + 1 - y_pos]).astype(jnp.int32)
    mem_g = mem_norm.reshape(1, d)
    proj, h, h_t, wkv_own, wo_own, pw_own, wi_full = _proj_piece(
        chips, 0, 1, x2, norm_pre, None, None, n_in, _gather_rider([wi_own], [0], peers=(0, 1)), tm, "proj_own",
        casts=[(w_kv[0], 1), (w_out[0], 2), (pool_w[0], 3)])
    proj, wi_full, wkv_full, pw_full = _proj_piece(
        chips, 1, 2, h, None, None, proj, n_in,
        _riders([_relay_rider(wi_full), _gather_rider([wkv_own, pw_own], [1, 3])]), tm, "proj_neighbours")
    proj, wo_part = _proj_piece(chips, 3, 1, h, None, wi_full, proj, n_in,
                                _gather_rider([wo_own], [2], peers=(0, 1)), tm, "proj_diagonal")
    k_m, v_m = _kv_fwd(mem2, mem_g, wkv_full)
    bias_full = jnp.repeat(sgu_b[0].T, CHUNK, axis=1)
    y, y_t, wo_full = _branches_fwd(proj, pw_full, pool_scale, sgu_ln_g, sgu_ln_b, sgu_w[0], bias_full, k_m, v_m,
                                    branch_norm, t_branch, _gather_rider([wo_part], [2], peers=(2,)))
    loss_local, dz, dout, dy, g_norm_post = _out_loss(y, wo_full, x2, tgt2, norm_post, min(256, s))

    def pair_sums(views):
        theirs = _exchange_halves([(v[0], v[1]) for v in views], "exchange_for_" + views[0][8])
        return [_blockwise(_pair_sum, pos, [v[0], th], [pl.BlockSpec(v[2], v[3][0]), pl.BlockSpec(v[2], v[3][1])],
                           jax.ShapeDtypeStruct(v[4], BF16), pl.BlockSpec(v[5], v[6]), v[7], v[8])
                for v, th in zip(views, theirs)]

    tk = min(1024, s)
    (g_wo,) = _grad_rows(y_t, dout, pos, lambda i, p: i, n_out, n_out // 2, 1024, tk, "grad_w_out")
    (ps_wo,) = pair_sums([
        (g_wo.reshape(4, 2, wo_rows, d), 1, (1, 1, wo_rows, d),
         (lambda i, p: (i, p[0], 0, 0), lambda i, p: (i, 0, 0, 0)), (4, wo_rows, d), (1, wo_rows, d),
         lambda i, p: (i, 0, 0), (4,), "pair_sum_w_out")])
    (dproj, g_pw, g_pool_scale, g_ln_g, g_ln_b, g_sgu_w, g_sgu_b_t, g_branch_norm, dk, dv, landed_wo) = _branches_bwd(
        proj, dy, pw_full, pool_scale, sgu_ln_g, sgu_ln_b, sgu_w[0], jnp.swapaxes(sgu_w[0], 1, 2), bias_full,
        k_m, v_m, branch_norm, t_branch, _scatter_rider([(ps_wo, 0)]))
    g_wkv, g_mem_norm = _kv_bwd(mem2, mem_g, wkv_full, dk, dv)
    ps_kv, ps_pw = pair_sums([
        (g_wkv.reshape(4, 2, kv_rows, kv_cols), 1, (1, 1, kv_rows, kv_cols),
         (lambda i, p: (i, p[0], 0, 0), lambda i, p: (i, 0, 0, 0)), (4, kv_rows, kv_cols), (1, kv_rows, kv_cols),
         lambda i, p: (i, 0, 0), (4,), "pair_sum_w_kv"),
        (g_pw.astype(BF16).reshape(4, 4, 2, pw_rows, GROUP), 2, (1, 4, 1, pw_rows, GROUP),
         (lambda i, p: (i, 0, p[0], 0, 0), lambda i, p: (i, 0, 0, 0, 0)), (4, 4, pw_rows, GROUP),
         (1, 4, pw_rows, GROUP), lambda i, p: (i, 0, 0, 0), (4,), "pair_sum_pool_w")])
    gwi_theirs, landed_kv, landed_pw = _grad_rows(
        h_t, dproj, pos, lambda i, p: 1 - p[0], d // 2, d // 2, n_in // 4, tk, "grad_w_in_sibling_half",
        _scatter_rider([(ps_kv, 0), (ps_pw, 1)]))
    gwi_mine, gwi_from_sibling = _grad_rows(h_t, dproj, pos, lambda i, p: p[0], d // 2, d // 2, n_in // 4, tk,
                                            "grad_w_in_own_half", _exchange_rider([gwi_theirs]))
    ps_wi = _elementwise(_pair_sum, [gwi_mine, gwi_from_sibling], [BF16], "pair_sum_w_in")[0]
    grad_x, g_norm_pre, landed_wi = _dx_call(dproj, wi_full, x2, dz, norm_pre, tm, 1024,
                                             _scatter_rider([(ps_wi, 1)]))
    psum = [ps_wi, ps_kv, ps_wo, ps_pw]
    landed = [landed_wi, landed_kv, landed_wo, landed_pw]
    from_chip = lambda spec_shape, rank: [
        pl.BlockSpec(spec_shape, functools.partial(lambda i, p, q: (q, i) + (0,) * (rank - 2), q=q))
        for q in range(3)]
    join_rider = _join_rider([
        (_blockwise(_four_sum, pos, [psum[0]] + [landed[0]] * 3,
                    [pl.BlockSpec((256, n_in // 4), lambda i, p: (i, p[1]))] + from_chip((1, 256, n_in // 4), 3),
                    jax.ShapeDtypeStruct((2, d // 2, n_in // 4), F32),
                    pl.BlockSpec((1, 256, n_in // 4), lambda i, p: (p[0], i, 0)), (d // 2 // 256,), "chip_sum_w_in"),
         0),
        (_blockwise(_four_sum, pos, [psum[1]] + [landed[1]] * 3,
                    [pl.BlockSpec((1, kv_rows, kv_cols), lambda i, p: (p[1], 0, 0))]
                    + from_chip((1, 1, kv_rows, kv_cols), 4),
                    jax.ShapeDtypeStruct((2, kv_rows, kv_cols), F32),
                    pl.BlockSpec((1, kv_rows, kv_cols), lambda i, p: (p[0], 0, 0)), (1,), "chip_sum_w_kv"),
         0),
        (_blockwise(_four_sum, pos, [psum[2]] + [landed[2]] * 3,
                    [pl.BlockSpec((1, wo_rows, d), lambda i, p: (p[1], 0, 0))] + from_chip((1, 1, wo_rows, d), 4),
                    jax.ShapeDtypeStruct((2, wo_rows, d), F32),
                    pl.BlockSpec((1, wo_rows, d), lambda i, p: (p[0], 0, 0)), (1,), "chip_sum_w_out"),
         0),
        (_blockwise(_four_sum, pos, [psum[3]] + [landed[3]] * 3,
                    [pl.BlockSpec((4, 1, pw_rows, GROUP), lambda i, p: (0, p[1], 0, 0))]
                    + from_chip((1, 4, 1, pw_rows, GROUP), 5),
                    jax.ShapeDtypeStruct((4, 2, pw_rows, GROUP), F32),
                    pl.BlockSpec((4, 1, pw_rows, GROUP), lambda i, p: (0, p[0], 0, 0)), (1,), "chip_sum_pool_w"),
         1),
    ])

    small_local = dict(norm_pre=g_norm_pre, pool_scale=g_pool_scale, sgu_ln_g=g_ln_g, sgu_ln_b=g_ln_b,
                       sgu_w=g_sgu_w, sgu_b=g_sgu_b_t.T, mem_norm=g_mem_norm, branch_norm=g_branch_norm,
                       norm_post=g_norm_post)
    small_rows = sum(weights[n].size for n in SMALL) // 128
    small_sum, *joined = _allreduce_small(
        _pack([small_local[n] for n in SMALL], [jnp.pad(loss_local, ((0, 7), (0, 127)))]), join_rider)
    grads = {"w_in": joined[0].reshape(w_in.shape), "w_kv": joined[1].reshape(w_kv.shape),
             "w_out": joined[2].reshape(w_out.shape), "pool_w": joined[3].reshape(pool_w.shape)}
    for n, g in zip(SMALL, _unpack(small_sum, [weights[n] for n in SMALL])):
        grads[n] = g
    loss = small_sum[small_rows, 0]

    delta, new_m, new_v = {}, {}, {}
    packed = [small_sum if src is grads else _pack([src[n] for n in SMALL]) for src in (weights, grads, mom1, mom2)]
    outs = _elementwise(_adamw, packed, [F32, F32, F32], "adamw_small")
    for dst, o in zip((delta, new_m, new_v), outs):
        for n, a in zip(SMALL, _unpack(o, [weights[n] for n in SMALL])):
            dst[n] = a
    for n in LARGE:
        cols = weights[n].shape[-1]
        outs = _elementwise(_adamw, [src[n].reshape(-1, cols) for src in (weights, grads, mom1, mom2)],
                            [F32, F32, F32], "adamw_" + n)
        for dst, o in zip((delta, new_m, new_v), outs):
            dst[n] = o.reshape(weights[n].shape)

    return (loss, grad_x[None], *[grads[n] for n in ORDER], *[delta[n] for n in ORDER],
            *[new_m[n] for n in ORDER], *[new_v[n] for n in ORDER])
```

```python
import functools

import jax
import jax.numpy as jnp
from jax import lax
from jax.experimental import pallas as pl
from jax.experimental.pallas import tpu as pltpu

F32 = jnp.float32
BF16 = jnp.bfloat16
EPS = 1e-6
MESH = pl.DeviceIdType.MESH
ANY = pl.BlockSpec(memory_space=pl.ANY)

POOL_WINDOWS = (2, 4, 8, 16)
GROUP = 256
HALO = 16
CHUNK = 128
N_SGU_HEADS = 8
N_ATT_HEADS = 4
ATT_DIM = 256
WIDTH = 1024
ATT_SCALE = 1.0 / 16.0

ADAM_LR = 0.001
ADAM_B1 = 0.9
ADAM_B2 = 0.999
ADAM_EPS = 1e-08
ADAM_WD = 0.01
ADAM_STEP = 10

VMEM_LIMIT = 60 * 1024 * 1024
ELEMENTWISE_VMEM = 24 * 1024 * 1024
LANES = 128
BF16_ROWS = 16
MAX_PARTS = 4


def _params(n_grid_axes, vmem=VMEM_LIMIT):
    return pltpu.CompilerParams(dimension_semantics=("arbitrary",) * n_grid_axes, vmem_limit_bytes=vmem)


def _dot(a, b, dims):
    return lax.dot_general(a, b, (dims, ((), ())), preferred_element_type=F32)


NN = ((1,), (0,))
NT = ((1,), (1,))
TN = ((0,), (0,))


class _Rider:
    def __init__(self, inputs, out_shapes, n_sems, run, aliases=None):
        self.inputs, self.out_shapes, self.n_sems, self.run = list(inputs), list(out_shapes), n_sems, run
        self.aliases = aliases or {}


def _call(body, name, grid, in_specs, out_specs, out_shape, scratch_shapes, inputs, rider=None, prefetch=None,
          aliases=None, rider_refs=False):
    n_in, n_out, n_scr = len(in_specs), len(out_specs), len(scratch_shapes)
    r_in = len(rider.inputs) if rider else 0
    r_out = len(rider.out_shapes) if rider else 0
    n_pre = 0 if prefetch is None else 1

    def whole_body(*refs):
        pre, refs = refs[:n_pre], refs[n_pre:]
        ins, rider_ins = refs[:n_in], refs[n_in:n_in + r_in]
        refs = refs[n_in + r_in:]
        outs, rider_outs = refs[:n_out], refs[n_out:n_out + r_out]
        refs = refs[n_out + r_out:]
        scratch, sems = refs[:n_scr], refs[n_scr:]
        extra = {"rider_outs": rider_outs} if rider_refs else {}
        if rider is None:
            body(*pre, *ins, *outs, *scratch, **extra)
            return
        ids = [pl.program_id(ax) for ax in range(len(grid))]
        first = functools.reduce(lambda p, q: p & q, [i == 0 for i in ids])
        last = functools.reduce(lambda p, q: p & q, [i == g - 1 for i, g in zip(ids, grid)])

        @pl.when(first)
        def _():
            rider.run(rider_ins, rider_outs, *sems, True)

        body(*pre, *ins, *outs, *scratch, **extra)

        @pl.when(last)
        def _():
            rider.run(rider_ins, rider_outs, *sems, False)

    io_aliases = {n_pre + i: o for i, o in (aliases or {}).items()}
    scratch_all = list(scratch_shapes)
    if rider:
        io_aliases.update({n_pre + n_in + i: n_out + o for i, o in rider.aliases.items()})
        scratch_all += [pltpu.SemaphoreType.DMA((rider.n_sems,)), pltpu.SemaphoreType.DMA((rider.n_sems,))]
    specs = dict(grid=grid, in_specs=list(in_specs) + [ANY] * r_in, out_specs=tuple(out_specs) + (ANY,) * r_out,
                 scratch_shapes=scratch_all)
    if n_pre:
        specs = dict(grid_spec=pltpu.PrefetchScalarGridSpec(num_scalar_prefetch=1, **specs))
    outs = pl.pallas_call(
        whole_body, name=name, **specs,
        out_shape=tuple(out_shape) + tuple(rider.out_shapes if rider else ()),
        input_output_aliases=io_aliases, compiler_params=_params(len(grid)),
    )(*([prefetch] if n_pre else []), *inputs, *(rider.inputs if rider else []))
    return tuple(outs)


def _grad_rows(a_t, b, pos, row_of, m, tm, tn, tk, name, rider=None):
    k, n = a_t.shape[1], b.shape[1]
    nk = k // tk
    out_dtype, dims, a = BF16, NN, a_t
    a_spec = pl.BlockSpec((tm, tk), lambda i, j, kk, p: (row_of(i, p), kk))
    b_spec = pl.BlockSpec((tk, tn), lambda i, j, kk, p: (kk, j))

    def body(pos_ref, a_ref, b_ref, o_ref, *acc):
        part = lambda: _dot(a_ref[...], b_ref[...], dims)
        if nk == 1:
            o_ref[...] = part().astype(out_dtype)
            return
        (acc_ref,) = acc
        kk = pl.program_id(2)

        @pl.when(kk == 0)
        def _():
            acc_ref[...] = part()

        @pl.when((kk > 0) & (kk < nk - 1))
        def _():
            acc_ref[...] += part()

        @pl.when(kk == nk - 1)
        def _():
            o_ref[...] = (acc_ref[...] + part()).astype(out_dtype)

    return _call(body, name, (m // tm, n // tn, nk), [a_spec, b_spec],
                 [pl.BlockSpec((tm, tn), lambda i, j, kk, p: (i, j))], [jax.ShapeDtypeStruct((m, n), out_dtype)],
                 [pltpu.VMEM((tm, tn), F32)] if nk > 1 else [], [a, b], rider, prefetch=pos)


def _proj_piece(chips, first, n_shards, src, g_pre, w_in, proj_in, n_cols, rider, tm, name, casts=()):
    s, d = src.shape
    cols = n_cols // 4
    fused = g_pre is not None
    n_steps = s // tm

    def body(chips_ref, *refs, rider_outs=()):
        refs = list(refs)
        src_ref = refs.pop(0)
        g_ref = refs.pop(0) if fused else None
        w_ref = refs.pop(0) if w_in is not None else rider_outs[0]
        if proj_in is not None:
            refs.pop(0)
        shard_refs = [refs.pop(0) for _ in casts]
        proj_ref = refs.pop(0)
        h_ref, ht_ref = (refs.pop(0), refs.pop(0)) if fused else (None, None)
        for shard_ref in shard_refs:
            refs.pop(0)[...] = shard_ref[...].astype(BF16)
        wbuf, sem = refs
        q, i = pl.program_id(0), pl.program_id(1)

        @pl.when(i == 0)
        def _():
            at = pl.multiple_of(chips_ref[first + q] * cols, LANES)
            cp = pltpu.make_async_copy(w_ref.at[:, pl.ds(at, cols)], wbuf, sem)
            cp.start()
            cp.wait()

        if fused:
            xv = src_ref[...]
            r = lax.rsqrt(jnp.mean(xv * xv, axis=-1, keepdims=True) + EPS)
            h = (xv * r * g_ref[...]).astype(BF16)
            h_ref[...] = h
            ht_ref[...] = h.T
        else:
            h = src_ref[...]
        proj_ref[...] = _dot(h, wbuf[...], NN)

    row = lambda q, i, ch: (i, 0)
    inputs, in_specs = [src], [pl.BlockSpec((tm, d), row)]
    if fused:
        inputs.append(g_pre)
        in_specs.append(pl.BlockSpec((1, d), lambda q, i, ch: (0, 0)))
    if w_in is not None:
        inputs.append(w_in)
        in_specs.append(ANY)
    aliases = {}
    if proj_in is not None:
        aliases[len(inputs)] = 0
        inputs.append(proj_in)
        in_specs.append(ANY)
    out_specs = [pl.BlockSpec((tm, cols), lambda q, i, ch: (i, ch[first + q]))]
    out_shape = [jax.ShapeDtypeStruct((s, n_cols), F32)]
    if fused:
        assert n_shards == 1
        out_specs += [pl.BlockSpec((tm, d), row), pl.BlockSpec((d, tm), lambda q, i, ch: (0, i))]
        out_shape += [jax.ShapeDtypeStruct((s, d), BF16), jax.ShapeDtypeStruct((d, s), BF16)]
    for shard, kind in casts:
        assert n_shards == 1
        inputs.append(shard)
        if kind == 3:
            in_specs.append(pl.BlockSpec(shard.shape, lambda q, i, ch: (0, 0, 0)))
            out_specs.append(pl.BlockSpec(shard.shape, lambda q, i, ch: (0, ch[0], 0)))
            out_shape.append(jax.ShapeDtypeStruct((shard.shape[0], 4 * shard.shape[1], shard.shape[2]), BF16))
        else:
            block = (shard.shape[0] // n_steps, shard.shape[1])
            in_specs.append(pl.BlockSpec(block, row))
            out_specs.append(pl.BlockSpec(block, lambda q, i, ch: (ch[0] * n_steps + i, 0)))
            out_shape.append(jax.ShapeDtypeStruct((4 * shard.shape[0], shard.shape[1]), BF16))
    return _call(body, name, (n_shards, s // tm), in_specs, out_specs, out_shape,
                 [pltpu.VMEM((d, cols), BF16), pltpu.SemaphoreType.DMA(())], inputs, rider, prefetch=chips,
                 aliases=aliases, rider_refs=True)


def _kv_fwd(mem, g, w_kv):
    m, d = mem.shape

    def body(mem_ref, g_ref, w_ref, k_ref, v_ref):
        mv = mem_ref[...]
        r = lax.rsqrt(jnp.mean(mv * mv, axis=-1, keepdims=True) + EPS)
        mem_n = (mv * r * g_ref[...]).astype(BF16)
        kv = _dot(mem_n, w_ref[...], NN)
        k_ref[...] = kv[:, :WIDTH].astype(BF16)
        v_ref[...] = kv[:, WIDTH:].astype(BF16)

    return pl.pallas_call(
        body, name="kv_fwd",
        out_shape=(jax.ShapeDtypeStruct((m, WIDTH), BF16), jax.ShapeDtypeStruct((m, WIDTH), BF16)),
        compiler_params=_params(0),
    )(mem, g, w_kv)


def _kv_bwd(mem, g, w_kv, dk, dv):
    m, d = mem.shape
    n = w_kv.shape[1]
    col = 512

    def body(mem_ref, g_ref, w_ref, dk_ref, dv_ref, dw_ref, dg_ref):
        mv = mem_ref[...]
        r = lax.rsqrt(jnp.mean(mv * mv, axis=-1, keepdims=True) + EPS)
        mem_hat = mv * r
        mem_n = (mem_hat * g_ref[...]).astype(BF16)
        dkv = jnp.concatenate([dk_ref[...], dv_ref[...]], axis=1).astype(BF16)
        for j in range(n // col):
            dw_ref[:, j * col:(j + 1) * col] = _dot(mem_n, dkv[:, j * col:(j + 1) * col], TN).astype(BF16)
        dmem_n = _dot(dkv, w_ref[...], NT)
        dg_ref[...] = jnp.sum(dmem_n * mem_hat, axis=0, keepdims=True)

    return pl.pallas_call(
        body, name="kv_bwd",
        out_shape=(jax.ShapeDtypeStruct((d, n), BF16), jax.ShapeDtypeStruct((1, d), F32)),
        compiler_params=_params(0),
    )(mem, g, w_kv, dk, dv)


def _sigmoid(x):
    return 1.0 / (1.0 + jnp.exp(-x))


def _inv_counts(t0, t):
    pos = (t0 + lax.broadcasted_iota(jnp.int32, (t, 1), 0) + 1).astype(F32)
    return [1.0 / jnp.minimum(pos, float(w)) for w in POOL_WINDOWS]


def _window_sums(ext, t, backward):
    n = t + HALO
    parts = []
    for gi, w in enumerate(POOL_WINDOWS):
        s = ext[:, gi * GROUP:(gi + 1) * GROUP]
        k = 1
        while k < w:
            s = s + pltpu.roll(s, (n - k) if backward else k, axis=0)
            k *= 2
        parts.append(s[:t] if backward else s[HALO:])
    return parts


def _pool_fwd(xa, halo, inv, pool_w):
    t = xa.shape[0]
    sums = _window_sums(jnp.concatenate([halo, xa], axis=0), t, backward=False)
    d = jnp.concatenate([sums[gi] * inv[gi] - xa[:, gi * GROUP:(gi + 1) * GROUP] for gi in range(4)], axis=1)
    d = d.astype(BF16)
    y = jnp.concatenate([_dot(d[:, gi * GROUP:(gi + 1) * GROUP], pool_w[gi], NN) for gi in range(4)], axis=1)
    return d, y


def _layernorm_fwd(v):
    mu = jnp.mean(v, axis=-1, keepdims=True)
    xc = v - mu
    rstd = lax.rsqrt(jnp.mean(xc * xc, axis=-1, keepdims=True) + EPS)
    return xc * rstd, rstd


def _tril_mask(transposed):
    r = lax.broadcasted_iota(jnp.int32, (CHUNK, CHUNK), 0)
    c = lax.broadcasted_iota(jnp.int32, (CHUNK, CHUNK), 1)
    return (r <= c) if transposed else (r >= c)


def _sgu_mix(w_ref, vals, transposed):
    t = vals.shape[0]
    mask = _tril_mask(transposed)
    ws = [jnp.where(mask, w_ref[h], 0.0).astype(BF16) for h in range(N_SGU_HEADS)]
    rows = []
    for ci in range(t // CHUNK):
        blk = vals[ci * CHUNK:(ci + 1) * CHUNK]
        rows.append(jnp.concatenate(
            [_dot(ws[h], blk[:, h * CHUNK:(h + 1) * CHUNK], NN) for h in range(N_SGU_HEADS)], axis=1))
    return jnp.concatenate(rows, axis=0)


def _attn_fwd(q, k, v):
    ps, os_ = [], []
    for h in range(N_ATT_HEADS):
        sl = slice(h * ATT_DIM, (h + 1) * ATT_DIM)
        s = _dot(q[:, sl], k[:, sl], NT) * ATT_SCALE
        s = s - jnp.max(s, axis=-1, keepdims=True)
        e = jnp.exp(s)
        p = e * (1.0 / jnp.sum(e, axis=-1, keepdims=True))
        ps.append(p)
        os_.append(_dot(p.astype(BF16), v[:, sl], NN))
    return ps, jnp.concatenate(os_, axis=1)


def _rms_branch(y_pre):
    r = lax.rsqrt(jnp.mean(y_pre * y_pre, axis=-1, keepdims=True) + EPS)
    return y_pre * r, r


def _branch_specs(t, n_tiles, order):
    width_in = 7 * WIDTH
    tile = lambda i: order(i)
    per_halo = t // HALO
    const2 = lambda i: (0, 0)
    const3 = lambda i: (0, 0, 0)
    return [
        pl.BlockSpec((t, width_in), lambda i: (tile(i), 0)),
        pl.BlockSpec((HALO, WIDTH), lambda i: (jnp.maximum(tile(i) * per_halo - 1, 0), 0)),
        pl.BlockSpec((4, GROUP, GROUP), const3),
        pl.BlockSpec((1, WIDTH), const2),
        pl.BlockSpec((1, WIDTH), const2),
        pl.BlockSpec((1, WIDTH), const2),
        pl.BlockSpec((N_SGU_HEADS, CHUNK, CHUNK), const3),
        pl.BlockSpec((CHUNK, WIDTH), const2),
        pl.BlockSpec((MEM_ROWS, WIDTH), const2),
        pl.BlockSpec((MEM_ROWS, WIDTH), const2),
        pl.BlockSpec((1, 3 * WIDTH), const2),
    ]


MEM_ROWS = 256


def _branches_fwd(proj, pool_w, pool_scale, ln_g, ln_b, sgu_w, bias_full, k, v, branch_norm, t, rider=None):
    s = proj.shape[0]
    n_tiles = s // t

    def body(proj_ref, halo_ref, pw_ref, ps_ref, lg_ref, lb_ref, sw_ref, sb_ref, k_ref, v_ref, bn_ref, y_ref, yt_ref):
        i = pl.program_id(0)
        col = lambda j: proj_ref[:, j * WIDTH:(j + 1) * WIDTH]

        def put(branch, y_pre):
            sl = slice(branch * WIDTH, (branch + 1) * WIDTH)
            val = (_rms_branch(y_pre)[0] * bn[:, sl]).astype(BF16)
            y_ref[:, sl] = val
            yt_ref[sl, :] = val.T

        bn = bn_ref[...]
        halo = jnp.where(i > 0, halo_ref[...], 0.0)
        _, y_pool = _pool_fwd(col(0), halo, _inv_counts(i * t, t), pw_ref[...])
        ga = col(1)
        ya = y_pool * ps_ref[...] * (ga * _sigmoid(ga))
        put(0, ya)
        vhat, _ = _layernorm_fwd(col(3))
        vn = (vhat * lg_ref[...] + lb_ref[...]).astype(BF16)
        z = _sgu_mix(sw_ref, vn, transposed=False) + jnp.tile(sb_ref[...], (t // CHUNK, 1))
        gb = col(4)
        yb = col(2) * z * (gb * _sigmoid(gb))
        put(1, yb)
        _, o = _attn_fwd(col(5).astype(BF16), k_ref[...], v_ref[...])
        gc = col(6)
        yc = o * (gc * _sigmoid(gc))
        put(2, yc)

    return _call(body, "branches_fwd", (n_tiles,), _branch_specs(t, n_tiles, lambda i: i),
                 [pl.BlockSpec((t, 3 * WIDTH), lambda i: (i, 0)), pl.BlockSpec((3 * WIDTH, t), lambda i: (0, i))],
                 [jax.ShapeDtypeStruct((s, 3 * WIDTH), BF16), jax.ShapeDtypeStruct((3 * WIDTH, s), BF16)], [],
                 [proj, proj, pool_w, pool_scale, ln_g, ln_b, sgu_w, bias_full, k, v, branch_norm], rider)


def _branches_bwd(proj, dy, pool_w, pool_scale, ln_g, ln_b, sgu_w, sgu_wt, bias_full, k, v, branch_norm, t, rider=None):
    s = proj.shape[0]
    n_tiles = s // t
    n_chunks = t // CHUNK
    order = lambda i: n_tiles - 1 - i

    def body(proj_ref, halo_ref, pw_ref, ps_ref, lg_ref, lb_ref, sw_ref, sb_ref, k_ref, v_ref, bn_ref,
             swt_ref, dy_ref,
             dproj_ref, dpw_ref, dps_ref, dlg_ref, dlb_ref, dsw_ref, dsb_ref, dbn_ref, dk_ref, dv_ref,
             carry_ref, dbias_ref):
        step = pl.program_id(0)
        i = order(step)

        @pl.when(step == 0)
        def _():
            for ref in (dpw_ref, dps_ref, dlg_ref, dlb_ref, dsw_ref, dbn_ref, dk_ref, dv_ref, carry_ref, dbias_ref):
                ref[...] = jnp.zeros(ref.shape, ref.dtype)

        col = lambda j: proj_ref[:, j * WIDTH:(j + 1) * WIDTH]
        bn = bn_ref[...]

        def norm_bwd(y_pre, sl):
            yhat, r = _rms_branch(y_pre)
            dyv = dy_ref[:, sl].astype(F32)
            dbn_ref[:, sl] += jnp.sum(dyv * yhat, axis=0, keepdims=True)
            dyhat = dyv * bn[:, sl]
            return r * (dyhat - yhat * jnp.mean(dyhat * yhat, axis=-1, keepdims=True))

        def gate(gv):
            sg = _sigmoid(gv)
            return gv * sg, sg * (1.0 + gv * (1.0 - sg))

        inv = _inv_counts(i * t, t)
        halo = jnp.where(i > 0, halo_ref[...], 0.0)
        pw = pw_ref[...]
        d, y_pool = _pool_fwd(col(0), halo, inv, pw)
        scale = ps_ref[...]
        silu_a, dsilu_a = gate(col(1))
        pa = y_pool * scale
        dya = norm_bwd(pa * silu_a, slice(0, WIDTH))
        dproj_ref[:, WIDTH:2 * WIDTH] = (dya * pa * dsilu_a).astype(BF16)
        dpa = dya * silu_a
        dps_ref[...] += jnp.sum(dpa * y_pool, axis=0, keepdims=True)
        dy_pool = (dpa * scale).astype(BF16)
        dd_parts, ddc_parts = [], []
        for gi in range(4):
            sl = slice(gi * GROUP, (gi + 1) * GROUP)
            dpw_ref[gi] += _dot(d[:, sl], dy_pool[:, sl], TN)
            dd = _dot(dy_pool[:, sl], pw[gi], NT)
            dd_parts.append(dd)
            ddc_parts.append(dd * inv[gi])
        ddc = jnp.concatenate(ddc_parts, axis=1)
        sums = _window_sums(jnp.concatenate([ddc, carry_ref[...]], axis=0), t, backward=True)
        carry_ref[...] = ddc[:HALO]
        dproj_ref[:, 0:WIDTH] = jnp.concatenate([sums[gi] - dd_parts[gi] for gi in range(4)], axis=1).astype(BF16)

        vhat, rstd = _layernorm_fwd(col(3))
        lg = lg_ref[...]
        vn = (vhat * lg + lb_ref[...]).astype(BF16)
        z = _sgu_mix(sw_ref, vn, transposed=False) + jnp.tile(sb_ref[...], (n_chunks, 1))
        u = col(2)
        silu_b, dsilu_b = gate(col(4))
        uz = u * z
        dyb = norm_bwd(uz * silu_b, slice(WIDTH, 2 * WIDTH))
        dproj_ref[:, 4 * WIDTH:5 * WIDTH] = (dyb * uz * dsilu_b).astype(BF16)
        duz = dyb * silu_b
        dproj_ref[:, 2 * WIDTH:3 * WIDTH] = (duz * z).astype(BF16)
        dz = duz * u
        dz_b = dz.astype(BF16)
        for ci in range(n_chunks):
            rows = slice(ci * CHUNK, (ci + 1) * CHUNK)
            dbias_ref[...] += dz[rows]
            for h in range(N_SGU_HEADS):
                sl = slice(h * CHUNK, (h + 1) * CHUNK)
                dsw_ref[h] += _dot(dz_b[rows, sl], vn[rows, sl], NT)
        dvn = _sgu_mix(swt_ref, dz_b, transposed=True)
        dlg_ref[...] += jnp.sum(dvn * vhat, axis=0, keepdims=True)
        dlb_ref[...] += jnp.sum(dvn, axis=0, keepdims=True)
        dvhat = dvn * lg
        dvb = rstd * (dvhat - jnp.mean(dvhat, axis=-1, keepdims=True)
                      - vhat * jnp.mean(dvhat * vhat, axis=-1, keepdims=True))
        dproj_ref[:, 3 * WIDTH:4 * WIDTH] = dvb.astype(BF16)

        q = col(5).astype(BF16)
        kv_k, kv_v = k_ref[...], v_ref[...]
        ps, o = _attn_fwd(q, kv_k, kv_v)
        silu_c, dsilu_c = gate(col(6))
        dyc = norm_bwd(o * silu_c, slice(2 * WIDTH, 3 * WIDTH))
        dproj_ref[:, 6 * WIDTH:7 * WIDTH] = (dyc * o * dsilu_c).astype(BF16)
        do = (dyc * silu_c).astype(BF16)
        dq_parts = []
        for h in range(N_ATT_HEADS):
            sl = slice(h * ATT_DIM, (h + 1) * ATT_DIM)
            p = ps[h]
            dp = _dot(do[:, sl], kv_v[:, sl], NT)
            ds = (p * (dp - jnp.sum(p * dp, axis=-1, keepdims=True)) * ATT_SCALE).astype(BF16)
            dq_parts.append(_dot(ds, kv_k[:, sl], NN))
            dk_ref[:, sl] += _dot(ds, q[:, sl], TN)
            dv_ref[:, sl] += _dot(p.astype(BF16), do[:, sl], TN)
        dproj_ref[:, 5 * WIDTH:6 * WIDTH] = jnp.concatenate(dq_parts, axis=1).astype(BF16)

        @pl.when(step == n_tiles - 1)
        def _():
            keep = _tril_mask(transposed=False)
            for h in range(N_SGU_HEADS):
                dsw_ref[h] = jnp.where(keep, dsw_ref[h], 0.0)
            dsb_ref[...] = jnp.concatenate(
                [jnp.sum(dbias_ref[:, h * CHUNK:(h + 1) * CHUNK], axis=1, keepdims=True)
                 for h in range(N_SGU_HEADS)], axis=1)

    const2 = lambda i: (0, 0)
    const3 = lambda i: (0, 0, 0)
    out_shapes = (
        jax.ShapeDtypeStruct((s, 7 * WIDTH), BF16),
        jax.ShapeDtypeStruct((4, GROUP, GROUP), F32),
        jax.ShapeDtypeStruct((1, WIDTH), F32),
        jax.ShapeDtypeStruct((1, WIDTH), F32),
        jax.ShapeDtypeStruct((1, WIDTH), F32),
        jax.ShapeDtypeStruct((N_SGU_HEADS, CHUNK, CHUNK), F32),
        jax.ShapeDtypeStruct((CHUNK, N_SGU_HEADS), F32),
        jax.ShapeDtypeStruct((1, 3 * WIDTH), F32),
        jax.ShapeDtypeStruct((MEM_ROWS, WIDTH), F32),
        jax.ShapeDtypeStruct((MEM_ROWS, WIDTH), F32),
    )
    out_specs = (
        pl.BlockSpec((t, 7 * WIDTH), lambda i: (order(i), 0)),
        pl.BlockSpec((4, GROUP, GROUP), const3),
        pl.BlockSpec((1, WIDTH), const2),
        pl.BlockSpec((1, WIDTH), const2),
        pl.BlockSpec((1, WIDTH), const2),
        pl.BlockSpec((N_SGU_HEADS, CHUNK, CHUNK), const3),
        pl.BlockSpec((CHUNK, N_SGU_HEADS), const2),
        pl.BlockSpec((1, 3 * WIDTH), const2),
        pl.BlockSpec((MEM_ROWS, WIDTH), const2),
        pl.BlockSpec((MEM_ROWS, WIDTH), const2),
    )
    in_specs = _branch_specs(t, n_tiles, order) + [
        pl.BlockSpec((N_SGU_HEADS, CHUNK, CHUNK), const3),
        pl.BlockSpec((t, 3 * WIDTH), lambda i: (order(i), 0)),
    ]
    return _call(body, "branches_bwd", (n_tiles,), in_specs, out_specs, out_shapes,
                 [pltpu.VMEM((HALO, WIDTH), F32), pltpu.VMEM((CHUNK, WIDTH), F32)],
                 [proj, proj, pool_w, pool_scale, ln_g, ln_b, sgu_w, bias_full, k, v, branch_norm, sgu_wt, dy], rider)


def _out_loss(y, w_out, x, target, g_post, tm):
    s, d = x.shape
    e_w = y.shape[1]
    n_tiles = s // tm

    def body(y_ref, w_ref, x_ref, t_ref, g_ref, loss_ref, dz_ref, dout_ref, dy_ref, dg_ref, sq_ref):
        i = pl.program_id(0)

        @pl.when(i == 0)
        def _():
            sq_ref[...] = jnp.zeros(sq_ref.shape, F32)
            dg_ref[...] = jnp.zeros(dg_ref.shape, F32)

        w = w_ref[...]
        out = _dot(y_ref[...], w, NN)
        r = lax.rsqrt(jnp.mean(out * out, axis=-1, keepdims=True) + EPS)
        outn = out * r
        g = g_ref[...]
        err = (x_ref[...] + outn * g) - t_ref[...]
        sq_ref[...] += jnp.sum(err * err, axis=0, keepdims=True)
        dz = err * (1.0 / d)
        dz_ref[...] = dz
        dg_ref[...] += jnp.sum(dz * outn, axis=0, keepdims=True)
        doutn = dz * g
        dout = (r * (doutn - outn * jnp.mean(doutn * outn, axis=-1, keepdims=True))).astype(BF16)
        dout_ref[...] = dout
        dy_ref[...] = _dot(dout, w, NT).astype(BF16)

        @pl.when(i == n_tiles - 1)
        def _():
            loss_ref[...] = 0.5 * jnp.sum(sq_ref[...], axis=1, keepdims=True) * (1.0 / d)

    row = lambda i: (i, 0)
    const2 = lambda i: (0, 0)
    return pl.pallas_call(
        body, name="out_loss", grid=(n_tiles,),
        in_specs=[
            pl.BlockSpec((tm, e_w), row),
            pl.BlockSpec((e_w, d), const2, pipeline_mode=pl.Buffered(1)),
            pl.BlockSpec((tm, d), row),
            pl.BlockSpec((tm, d), row),
            pl.BlockSpec((1, d), const2),
        ],
        out_specs=(
            pl.BlockSpec((1, 1), const2),
            pl.BlockSpec((tm, d), row),
            pl.BlockSpec((tm, d), row),
            pl.BlockSpec((tm, e_w), row),
            pl.BlockSpec((1, d), const2),
        ),
        out_shape=(
            jax.ShapeDtypeStruct((1, 1), F32),
            jax.ShapeDtypeStruct((s, d), F32),
            jax.ShapeDtypeStruct((s, d), BF16),
            jax.ShapeDtypeStruct((s, e_w), BF16),
            jax.ShapeDtypeStruct((1, d), F32),
        ),
        scratch_shapes=[pltpu.VMEM((1, d), F32)],
        compiler_params=_params(1),
    )(y, w_out, x, target, g_post)


def _dx_call(dproj, w_in, x, dz, g_pre, tm, tk, rider=None):
    s, d = x.shape
    k_total = dproj.shape[1]
    nk = k_total // tk
    n_tiles = s // tm

    def body(dp_ref, w_ref, x_ref, dz_ref, g_ref, dx_ref, dg_ref, acc_ref):
        i, kk = pl.program_id(0), pl.program_id(1)
        part = lambda: _dot(dp_ref[...], w_ref[...], NT)

        @pl.when(kk == 0)
        def _():
            acc_ref[...] = part()

        @pl.when((kk > 0) & (kk < nk - 1))
        def _():
            acc_ref[...] += part()

        @pl.when((i == 0) & (kk == 0))
        def _():
            dg_ref[...] = jnp.zeros(dg_ref.shape, F32)

        @pl.when(kk == nk - 1)
        def _():
            dh = acc_ref[...] + part()
            xv = x_ref[...]
            r = lax.rsqrt(jnp.mean(xv * xv, axis=-1, keepdims=True) + EPS)
            xhat = xv * r
            dg_ref[...] += jnp.sum(dh * xhat, axis=0, keepdims=True)
            dxhat = dh * g_ref[...]
            dx_ref[...] = dz_ref[...] + r * (dxhat - xhat * jnp.mean(dxhat * xhat, axis=-1, keepdims=True))

    row = lambda i, kk: (i, 0)
    const2 = lambda i, kk: (0, 0)
    return _call(
        body, "dx", (n_tiles, nk),
        [
            pl.BlockSpec((tm, tk), lambda i, kk: (i, kk)),
            pl.BlockSpec((d, tk), lambda i, kk: (0, kk)),
            pl.BlockSpec((tm, d), row),
            pl.BlockSpec((tm, d), row),
            pl.BlockSpec((1, d), const2),
        ],
        [pl.BlockSpec((tm, d), row), pl.BlockSpec((1, d), const2)],
        [jax.ShapeDtypeStruct((s, d), F32), jax.ShapeDtypeStruct((1, d), F32)],
        [pltpu.VMEM((tm, d), F32)], [dproj, w_in, x, dz, g_pre], rider)


def _rows_tile(rows, cols, n_arrays, itemsize=4):
    budget = ELEMENTWISE_VMEM // (2 * n_arrays * cols * itemsize)
    if rows <= budget:
        return rows
    best = None
    for cand in range(16, rows + 1, 16):
        if rows % cand == 0 and cand <= max(budget, 16):
            best = cand
    return best if best is not None else rows


def _elementwise(fn, inputs, out_dtypes, name):
    rows, cols = inputs[0].shape
    tr = _rows_tile(rows, cols, len(inputs) + len(out_dtypes))
    n_in = len(inputs)

    def body(*refs):
        outs = fn(*[r[...] for r in refs[:n_in]])
        for o_ref, o in zip(refs[n_in:], outs):
            o_ref[...] = o.astype(o_ref.dtype)

    spec = pl.BlockSpec((tr, cols), lambda i: (i, 0))
    return pl.pallas_call(
        body, name=name, grid=(rows // tr,),
        in_specs=[spec] * n_in, out_specs=tuple([spec] * len(out_dtypes)),
        out_shape=tuple(jax.ShapeDtypeStruct((rows, cols), dt) for dt in out_dtypes),
        compiler_params=_params(1),
    )(*inputs)


def _blockwise(fn, pos, inputs, in_specs, out_shape, out_spec, grid, name):
    n_in = len(inputs)

    def body(pos_ref, *refs):
        o_ref = refs[n_in]
        (out,) = fn(*[r[...].reshape(o_ref.shape) for r in refs[:n_in]])
        o_ref[...] = out.astype(o_ref.dtype)

    return pl.pallas_call(
        body, name=name,
        grid_spec=pltpu.PrefetchScalarGridSpec(num_scalar_prefetch=1, grid=grid, in_specs=in_specs,
                                               out_specs=out_spec),
        out_shape=out_shape,
        compiler_params=_params(len(grid)),
    )(pos, *inputs)


def _cast_copy(x):
    return (x,)


def _pair_sum(mine, theirs):
    return ((mine.astype(F32) + theirs.astype(F32)),)


def _four_sum(own, t0, t1, t2):
    return ((((own.astype(F32) + t0.astype(F32)) + t1.astype(F32)) + t2.astype(F32)),)


def _adamw(w, g, m, v):
    m = ADAM_B1 * m + (1.0 - ADAM_B1) * g
    v = ADAM_B2 * v + (1.0 - ADAM_B2) * jnp.square(g)
    m_hat = m / (1.0 - ADAM_B1 ** ADAM_STEP)
    v_hat = v / (1.0 - ADAM_B2 ** ADAM_STEP)
    delta = -ADAM_LR * (m_hat / (jnp.sqrt(v_hat) + ADAM_EPS) + ADAM_WD * w)
    return delta, m, v


def _place():
    x, y, c = lax.axis_index("x"), lax.axis_index("y"), lax.axis_index("c")
    chips = [(1 - x, y), (x, 1 - y), (1 - x, 1 - y)]
    return x, y, c, chips


def _remote(src, dst, send_sem, recv_sem, to):
    return pltpu.make_async_remote_copy(src_ref=src, dst_ref=dst, send_sem=send_sem, recv_sem=recv_sem,
                                        device_id=to, device_id_type=MESH)


def _split(ref, plan):
    views = [ref]
    for axis, parts in plan:
        size = ref.shape[axis] // parts
        assert size * parts == ref.shape[axis]
        views = [v.at[tuple(pl.ds(q * size, size) if i == axis else slice(None) for i in range(len(ref.shape)))]
                 for v in views for q in range(parts)]
    return views


def _started(src, dst, send_sem, recv_sem, to):
    copy = _remote(src, dst, send_sem, recv_sem, to)
    copy.start()
    return copy


def _hbm_call(body, name, inputs, out_shapes, scratch, aliases=None):
    return pl.pallas_call(
        body, name=name,
        in_specs=[ANY] * len(inputs), out_specs=tuple([ANY] * len(out_shapes)), out_shape=tuple(out_shapes),
        scratch_shapes=scratch, input_output_aliases=aliases or {},
        compiler_params=pltpu.CompilerParams(has_side_effects=True),
    )(*inputs)


def _shard_half(kind, ref, chip, cc):
    if kind == 0:
        rows, cols = ref.shape[0] // 2, ref.shape[1] // 4
        return ref.at[pl.ds(cc * rows, rows), pl.ds(pl.multiple_of(chip * cols, LANES), cols)]
    if kind == 3:
        rows = ref.shape[1] // 8
        return ref.at[:, pl.ds(pl.multiple_of((2 * chip + cc) * rows, BF16_ROWS), rows), :]
    rows = ref.shape[0] // 8
    return ref.at[pl.ds(pl.multiple_of((2 * chip + cc) * rows, BF16_ROWS), rows), :]


def _relay_rider(full):
    kind = 0

    def quarter(ref, chip_no, cc, q):
        return _split(_shard_half(kind, ref, chip_no, cc), [(0, 2)])[q]

    def run(in_refs, full_refs, send_sems, recv_sems, start):
        (ref,) = full_refs
        x, y, c, chips = _place()
        sibling = (x, y, 1 - c)
        chip_no = [2 * ch[0] + ch[1] for ch in chips]
        if start:
            for p in (0, 1):
                held = quarter(ref, chip_no[1 - p], c, p)
                _remote(held, held, send_sems.at[p], recv_sems.at[p], (*chips[p], c)).start()
            return
        for p in (0, 1):
            landed = quarter(ref, chip_no[2], c, p)
            _remote(landed, landed, send_sems.at[p], recv_sems.at[p], (*chips[p], c)).wait_recv()
            _remote(landed, landed, send_sems.at[2], recv_sems.at[2], sibling).start()
        mine, theirs = _shard_half(kind, ref, chip_no[2], c), _shard_half(kind, ref, chip_no[2], 1 - c)
        _remote(mine, mine, send_sems.at[2], recv_sems.at[2], sibling).wait_send()
        _remote(theirs, theirs, send_sems.at[2], recv_sems.at[2], sibling).wait_recv()
        for p in (0, 1):
            held = quarter(ref, chip_no[1 - p], c, p)
            _remote(held, held, send_sems.at[p], recv_sems.at[p], (*chips[p], c)).wait_send()

    return _Rider([full], [jax.ShapeDtypeStruct(full.shape, full.dtype)], 3, run, aliases={0: 0})


def _riders(riders):
    def bounds(counts):
        ends = [sum(counts[:i + 1]) for i in range(len(counts))]
        return list(zip([0] + ends[:-1], ends))

    ins = bounds([len(r.inputs) for r in riders])
    outs = bounds([len(r.out_shapes) for r in riders])
    sems = bounds([r.n_sems for r in riders])

    class From:
        def __init__(self, sem_refs, base):
            self.sem_refs, self.base, self.at = sem_refs, base, self

        def __getitem__(self, k):
            return self.sem_refs.at[self.base + k]

    def run(in_refs, out_refs, send_sems, recv_sems, start):
        for r, (i0, i1), (o0, o1), (s0, _) in zip(riders, ins, outs, sems):
            r.run(in_refs[i0:i1], out_refs[o0:o1], From(send_sems, s0), From(recv_sems, s0), start)

    aliases = {}
    for r, (i0, _), (o0, _) in zip(riders, ins, outs):
        aliases.update({i0 + i: o0 + o for i, o in r.aliases.items()})
    return _Rider([a for r in riders for a in r.inputs], [o for r in riders for o in r.out_shapes],
                  sems[-1][1], run, aliases)


def _gather_rider(fulls, kinds, peers=(0, 1, 2)):
    n = len(fulls)
    full_half = _shard_half

    def run(in_refs, full_refs, send_sems, recv_sems, start):
        x, y, c, chips = _place()
        me = 2 * x + y
        sibling = (x, y, 1 - c)
        plans = [[(0, MAX_PARTS)], [(0, 2)], [(0, 2)], []]
        chips = [(p, chips[p]) for p in peers]
        across = lambda a, p, k: (3 * a + p) * MAX_PARTS + k
        onward = lambda a, p: 3 * n * MAX_PARTS + 3 * a + p

        def parts(a, chip_no, cc):
            return _split(full_half(kinds[a], full_refs[a], chip_no, cc), plans[kinds[a]])

        if start:
            for p, chip in chips:
                for a in range(n):
                    for k, mine in enumerate(parts(a, me, c)):
                        _remote(mine, mine, send_sems.at[across(a, p, k)], recv_sems.at[across(a, p, k)],
                                (*chip, c)).start()
            return
        for k in range(MAX_PARTS):
            for p, chip in chips:
                for a in range(n):
                    landed = parts(a, 2 * chip[0] + chip[1], c)
                    if k < len(landed):
                        _remote(landed[k], landed[k], send_sems.at[across(a, p, k)], recv_sems.at[across(a, p, k)],
                                (*chip, c)).wait_recv()
                        _remote(landed[k], landed[k], send_sems.at[onward(a, p)], recv_sems.at[onward(a, p)],
                                sibling).start()
        for p, chip in chips:
            them = 2 * chip[0] + chip[1]
            for a in range(n):
                passed = full_half(kinds[a], full_refs[a], them, 1 - c)
                _remote(passed, passed, send_sems.at[onward(a, p)], recv_sems.at[onward(a, p)], sibling).wait_recv()
                landed = full_half(kinds[a], full_refs[a], them, c)
                _remote(landed, landed, send_sems.at[onward(a, p)], recv_sems.at[onward(a, p)], sibling).wait_send()
                for k, mine in enumerate(parts(a, me, c)):
                    _remote(mine, mine, send_sems.at[across(a, p, k)], recv_sems.at[across(a, p, k)],
                            (*chip, c)).wait_send()

    return _Rider(fulls, [jax.ShapeDtypeStruct(f.shape, f.dtype) for f in fulls], 3 * n * (MAX_PARTS + 1), run,
                  aliases={a: a for a in range(n)})


def _exchange_halves(grads, name):
    n = len(grads)
    arrays = [g for g, _ in grads]
    out_shapes = [jax.ShapeDtypeStruct(tuple(1 if i == ax else dim for i, dim in enumerate(g.shape)), g.dtype)
                  for g, ax in grads]

    def half(ref, ax, cc):
        idx = tuple(pl.ds(cc, 1) if i == ax else slice(None) for i in range(len(ref.shape)))
        return ref.at[idx]

    def body(*refs):
        in_refs, out_refs = refs[:n], refs[n:2 * n]
        send_sems, recv_sems = refs[2 * n:]
        x, y, c, _ = _place()
        sibling = (x, y, 1 - c)
        copies = [_started(half(in_refs[a], grads[a][1], 1 - c), out_refs[a], send_sems.at[a], recv_sems.at[a],
                           sibling) for a in range(n)]
        for rem in copies:
            rem.wait()

    return _hbm_call(body, name, arrays, out_shapes,
                     [pltpu.SemaphoreType.DMA((n,)), pltpu.SemaphoreType.DMA((n,))])


def _exchange_rider(arrays, half_axes=None):
    n = len(arrays)
    half_axes = half_axes or [None] * n
    out_shapes = [jax.ShapeDtypeStruct(tuple(1 if i == ax else dim for i, dim in enumerate(g.shape)), g.dtype)
                  for g, ax in zip(arrays, half_axes)]

    def run(in_refs, out_refs, send_sems, recv_sems, start):
        x, y, c, _ = _place()
        sibling = (x, y, 1 - c)
        for a in range(n):
            src, ax = in_refs[a], half_axes[a]
            if ax is not None:
                src = src.at[tuple(pl.ds(1 - c, 1) if i == ax else slice(None) for i in range(len(src.shape)))]
            sems = (send_sems.at[a], recv_sems.at[a])
            if start:
                _started(src, out_refs[a], *sems, sibling)
            else:
                _remote(src, out_refs[a], *sems, sibling).wait()

    return _Rider(arrays, out_shapes, n, run)


def _scatter_rider(parts):
    n = len(parts)
    arrays = [p for p, _ in parts]

    def block_shape(p, ax):
        if ax == len(p.shape) - 1:
            return p.shape[:-1] + (p.shape[-1] // 4,)
        return tuple(1 if i == ax else dim for i, dim in enumerate(p.shape))

    out_shapes = [jax.ShapeDtypeStruct((3,) + block_shape(p, ax), p.dtype) for p, ax in parts]

    def block(ref, ax, chip):
        rank = len(ref.shape)
        if ax == rank - 1:
            cols = ref.shape[-1] // 4
            last = pl.ds(pl.multiple_of(chip * cols, LANES), cols)
            return ref.at[tuple([slice(None)] * (rank - 1) + [last])]
        return ref.at[tuple(pl.ds(chip, 1) if i == ax else slice(None) for i in range(rank))]

    def run(in_refs, out_refs, send_sems, recv_sems, start):
        x, y, c, chips = _place()
        for a in range(n):
            ax = parts[a][1]
            for p, chip in enumerate(chips):
                src, dst = block(in_refs[a], ax, 2 * chip[0] + chip[1]), out_refs[a].at[p]
                sems = (send_sems.at[3 * a + p], recv_sems.at[3 * a + p])
                if start:
                    _started(src, dst, *sems, (*chip, c))
                else:
                    _remote(src, dst, *sems, (*chip, c)).wait()

    return _Rider(arrays, out_shapes, 3 * n, run)


def _join_rider(joined):
    n = len(joined)
    arrays = [j for j, _ in joined]

    def run(in_refs, out_refs, send_sems, recv_sems, start):
        x, y, c, _ = _place()
        sibling = (x, y, 1 - c)

        def half(a, cc):
            rank = len(out_refs[a].shape)
            return out_refs[a].at[tuple(pl.ds(cc, 1) if i == joined[a][1] else slice(None) for i in range(rank))]

        for a in range(n):
            sems = (send_sems.at[a], recv_sems.at[a])
            if start:
                _started(half(a, c), half(a, c), *sems, sibling)
            else:
                _remote(half(a, c), half(a, c), *sems, sibling).wait_send()
                _remote(half(a, 1 - c), half(a, 1 - c), *sems, sibling).wait_recv()

    return _Rider(arrays, [jax.ShapeDtypeStruct(j.shape, j.dtype) for j in arrays], n, run,
                  aliases={a: a for a in range(n)})


def _allreduce_small(packed, rider):
    rows, lanes = packed.shape
    half = rows // 2
    r_in, r_out = len(rider.inputs), len(rider.out_shapes)

    def body(in_ref, *refs):
        rider_ins, out_ref, rider_outs = refs[:r_in], refs[r_in], refs[r_in + 1:r_in + 1 + r_out]
        pair_ref, gath_ref, send_sems, recv_sems, rider_send, rider_recv = refs[r_in + 1 + r_out:]
        rider.run(rider_ins, rider_outs, rider_send, rider_recv, True)
        x, y, c, chips = _place()
        me = 2 * x + y
        sibling = (x, y, 1 - c)
        mine = pl.ds(pl.multiple_of(c * half, 8), half)
        theirs = pl.ds(pl.multiple_of((1 - c) * half, 8), half)
        to_sib = _remote(in_ref.at[theirs], pair_ref, send_sems.at[0], recv_sems.at[0], sibling)
        to_sib.start()
        to_sib.wait()
        gath_ref[me] = in_ref[mine] + pair_ref[...]
        sends = [_remote(gath_ref.at[me], gath_ref.at[me], send_sems.at[1 + p], recv_sems.at[1 + p], (*chip, c))
                 for p, chip in enumerate(chips)]
        for cp in sends:
            cp.start()
        for p, chip in enumerate(chips):
            slot = gath_ref.at[2 * chip[0] + chip[1]]
            _remote(slot, slot, send_sems.at[1 + p], recv_sems.at[1 + p], (*chip, c)).wait_recv()
        for cp in sends:
            cp.wait_send()
        out_ref[mine] = ((gath_ref[0] + gath_ref[1]) + gath_ref[2]) + gath_ref[3]
        back = _remote(out_ref.at[mine], out_ref.at[mine], send_sems.at[4], recv_sems.at[4], sibling)
        back.start()
        back.wait_send()
        _remote(out_ref.at[theirs], out_ref.at[theirs], send_sems.at[4], recv_sems.at[4], sibling).wait_recv()
        rider.run(rider_ins, rider_outs, rider_send, rider_recv, False)

    vmem = pl.BlockSpec(memory_space=pltpu.VMEM)
    return pl.pallas_call(
        body, name="allreduce_small",
        in_specs=[vmem] + [ANY] * r_in, out_specs=(vmem,) + (ANY,) * r_out,
        out_shape=(jax.ShapeDtypeStruct((rows, lanes), F32),) + tuple(rider.out_shapes),
        scratch_shapes=[pltpu.VMEM((half, lanes), F32), pltpu.VMEM((4, half, lanes), F32),
                        pltpu.SemaphoreType.DMA((5,)), pltpu.SemaphoreType.DMA((5,)),
                        pltpu.SemaphoreType.DMA((rider.n_sems,)), pltpu.SemaphoreType.DMA((rider.n_sems,))],
        input_output_aliases={1 + i: 1 + o for i, o in rider.aliases.items()},
        compiler_params=pltpu.CompilerParams(has_side_effects=True, vmem_limit_bytes=32 * 1024 * 1024),
    )(packed, *rider.inputs)


SMALL = ("norm_pre", "pool_scale", "sgu_ln_g", "sgu_ln_b", "sgu_w", "sgu_b", "mem_norm", "branch_norm", "norm_post")
LARGE = ("w_in", "pool_w", "w_kv", "w_out")
ORDER = ("norm_pre", "w_in", "pool_w", "pool_scale", "sgu_ln_g", "sgu_ln_b", "sgu_w", "sgu_b", "mem_norm", "w_kv",
         "branch_norm", "w_out", "norm_post")


def _pack(arrays, extra=()):
    rows = [a.reshape(-1, 128) for a in arrays] + list(extra)
    pad = -sum(r.shape[0] for r in rows) % 16
    return jnp.concatenate(rows + ([jnp.zeros((pad, 128), F32)] if pad else []), axis=0)


def _unpack(packed, like):
    out, row = [], 0
    for a in like:
        rows = a.size // 128
        out.append(packed[row:row + rows].reshape(a.shape))
        row += rows
    return out


def kernel(x, mem, norm_pre, w_in, pool_w, pool_scale, sgu_ln_g, sgu_ln_b, sgu_w, sgu_b, mem_norm, w_kv, branch_norm, w_out, norm_post, loss_target, m_norm_pre, m_w_in, m_pool_w, m_pool_scale, m_sgu_ln_g, m_sgu_ln_b, m_sgu_w, m_sgu_b, m_mem_norm, m_w_kv, m_branch_norm, m_w_out, m_norm_post, v_norm_pre, v_w_in, v_pool_w, v_pool_scale, v_sgu_ln_g, v_sgu_ln_b, v_sgu_w, v_sgu_b, v_mem_norm, v_w_kv, v_branch_norm, v_w_out, v_norm_post):
    weights = dict(norm_pre=norm_pre, w_in=w_in, pool_w=pool_w, pool_scale=pool_scale, sgu_ln_g=sgu_ln_g,
                   sgu_ln_b=sgu_ln_b, sgu_w=sgu_w, sgu_b=sgu_b, mem_norm=mem_norm, w_kv=w_kv, branch_norm=branch_norm,
                   w_out=w_out, norm_post=norm_post)
    mom1 = dict(norm_pre=m_norm_pre, w_in=m_w_in, pool_w=m_pool_w, pool_scale=m_pool_scale, sgu_ln_g=m_sgu_ln_g,
                sgu_ln_b=m_sgu_ln_b, sgu_w=m_sgu_w, sgu_b=m_sgu_b, mem_norm=m_mem_norm, w_kv=m_w_kv,
                branch_norm=m_branch_norm, w_out=m_w_out, norm_post=m_norm_post)
    mom2 = dict(norm_pre=v_norm_pre, w_in=v_w_in, pool_w=v_pool_w, pool_scale=v_pool_scale, sgu_ln_g=v_sgu_ln_g,
                sgu_ln_b=v_sgu_ln_b, sgu_w=v_sgu_w, sgu_b=v_sgu_b, mem_norm=v_mem_norm, w_kv=v_w_kv,
                branch_norm=v_branch_norm, w_out=v_w_out, norm_post=v_norm_post)

    s, d = x.shape[1], x.shape[2]
    x2, mem2, tgt2 = x[0], mem[0], loss_target[0]
    t_branch = min(256, s)
    tm = min(512, s)

    core = lax.axis_index("c")
    chip = 2 * lax.axis_index("x") + lax.axis_index("y")
    pos = jnp.stack([core, chip]).astype(jnp.int32)
    n_in, n_kv, n_out = 4 * w_in.shape[2], 4 * w_kv.shape[1], 4 * w_out.shape[1]
    wi_rows, kv_rows, wo_rows = d // 8, n_kv // 8, n_out // 8

    kv_cols, pw_rows = w_kv.shape[2], GROUP // 8
    wi_own = _blockwise(_cast_copy, pos, [w_in[0]], [pl.BlockSpec((wi_rows, n_in // 4), lambda i, p: (i, 0))],
                        jax.ShapeDtypeStruct((d, n_in), BF16),
                        pl.BlockSpec((wi_rows, n_in // 4), lambda i, p: (i, p[1])), (8,), "place_w_in")

    x_pos, y_pos = lax.axis_index("x"), lax.axis_index("y")
    chips = jnp.stack([chip, 2 * (1 - x_pos) + y_pos, 2 * x_pos + 1 - y_pos,
                       2 * (1 - x_pos) + 1 - y_pos]).astype(jnp.int32)
    mem_g = mem_norm.reshape(1, d)
    proj, h, h_t, wkv_own, wo_own, pw_own, wi_full = _proj_piece(
        chips, 0, 1, x2, norm_pre, None, None, n_in, _gather_rider([wi_own], [0], peers=(0, 1)), tm, "proj_own",
        casts=[(w_kv[0], 1), (w_out[0], 2), (pool_w[0], 3)])
    proj, wi_full, wkv_full, pw_full = _proj_piece(
        chips, 1, 2, h, None, None, proj, n_in,
        _riders([_relay_rider(wi_full), _gather_rider([wkv_own, pw_own], [1, 3])]), tm, "proj_neighbours")
    proj, wo_part = _proj_piece(chips, 3, 1, h, None, wi_full, proj, n_in,
                                _gather_rider([wo_own], [2], peers=(0, 1)), tm, "proj_diagonal")
    k_m, v_m = _kv_fwd(mem2, mem_g, wkv_full)
    bias_full = jnp.repeat(sgu_b[0].T, CHUNK, axis=1)
    y, y_t, wo_full = _branches_fwd(proj, pw_full, pool_scale, sgu_ln_g, sgu_ln_b, sgu_w[0], bias_full, k_m, v_m,
                                    branch_norm, t_branch, _gather_rider([wo_part], [2], peers=(2,)))
    loss_local, dz, dout, dy, g_norm_post = _out_loss(y, wo_full, x2, tgt2, norm_post, min(256, s))

    def pair_sums(views):
        theirs = _exchange_halves([(v[0], v[1]) for v in views], "exchange_for_" + views[0][8])
        return [_blockwise(_pair_sum, pos, [v[0], th], [pl.BlockSpec(v[2], v[3][0]), pl.BlockSpec(v[2], v[3][1])],
                           jax.ShapeDtypeStruct(v[4], BF16), pl.BlockSpec(v[5], v[6]), v[7], v[8])
                for v, th in zip(views, theirs)]

    tk = min(1024, s)
    (g_wo,) = _grad_rows(y_t, dout, pos, lambda i, p: i, n_out, n_out // 2, 1024, tk, "grad_w_out")
    g_wo = g_wo.reshape(4, 2, wo_rows, d)
    (dproj, g_pw, g_pool_scale, g_ln_g, g_ln_b, g_sgu_w, g_sgu_b_t, g_branch_norm, dk, dv) = _branches_bwd(
        proj, dy, pw_full, pool_scale, sgu_ln_g, sgu_ln_b, sgu_w[0], jnp.swapaxes(sgu_w[0], 1, 2), bias_full,
        k_m, v_m, branch_norm, t_branch)
    g_wkv, g_mem_norm = _kv_bwd(mem2, mem_g, wkv_full, dk, dv)
    ps_kv, ps_pw = pair_sums([
        (g_wkv.reshape(4, 2, kv_rows, kv_cols), 1, (1, 1, kv_rows, kv_cols),
         (lambda i, p: (i, p[0], 0, 0), lambda i, p: (i, 0, 0, 0)), (4, kv_rows, kv_cols), (1, kv_rows, kv_cols),
         lambda i, p: (i, 0, 0), (4,), "pair_sum_w_kv"),
        (g_pw.astype(BF16).reshape(4, 4, 2, pw_rows, GROUP), 2, (1, 4, 1, pw_rows, GROUP),
         (lambda i, p: (i, 0, p[0], 0, 0), lambda i, p: (i, 0, 0, 0, 0)), (4, 4, pw_rows, GROUP),
         (1, 4, pw_rows, GROUP), lambda i, p: (i, 0, 0, 0), (4,), "pair_sum_pool_w")])
    gwi_theirs, landed_kv, landed_pw, gwo_from_sibling = _grad_rows(
        h_t, dproj, pos, lambda i, p: 1 - p[0], d // 2, d // 2, n_in // 4, tk, "grad_w_in_sibling_half",
        _riders([_scatter_rider([(ps_kv, 0), (ps_pw, 1)]), _exchange_rider([g_wo], [1])]))
    ps_wo = _blockwise(_pair_sum, pos, [g_wo, gwo_from_sibling],
                       [pl.BlockSpec((1, 1, wo_rows, d), lambda i, p: (i, p[0], 0, 0)),
                        pl.BlockSpec((1, 1, wo_rows, d), lambda i, p: (i, 0, 0, 0))],
                       jax.ShapeDtypeStruct((4, wo_rows, d), BF16),
                       pl.BlockSpec((1, wo_rows, d), lambda i, p: (i, 0, 0)), (4,), "pair_sum_w_out")
    gwi_mine, gwi_from_sibling, landed_wo = _grad_rows(
        h_t, dproj, pos, lambda i, p: p[0], d // 2, d // 2, n_in // 4, tk, "grad_w_in_own_half",
        _riders([_exchange_rider([gwi_theirs]), _scatter_rider([(ps_wo, 0)])]))
    ps_wi = _elementwise(_pair_sum, [gwi_mine, gwi_from_sibling], [BF16], "pair_sum_w_in")[0]
    grad_x, g_norm_pre, landed_wi = _dx_call(dproj, wi_full, x2, dz, norm_pre, tm, 1024,
                                             _scatter_rider([(ps_wi, 1)]))
    psum = [ps_wi, ps_kv, ps_wo, ps_pw]
    landed = [landed_wi, landed_kv, landed_wo, landed_pw]
    from_chip = lambda spec_shape, rank: [
        pl.BlockSpec(spec_shape, functools.partial(lambda i, p, q: (q, i) + (0,) * (rank - 2), q=q))
        for q in range(3)]
    join_rider = _join_rider([
        (_blockwise(_four_sum, pos, [psum[0]] + [landed[0]] * 3,
                    [pl.BlockSpec((256, n_in // 4), lambda i, p: (i, p[1]))] + from_chip((1, 256, n_in // 4), 3),
                    jax.ShapeDtypeStruct((2, d // 2, n_in // 4), F32),
                    pl.BlockSpec((1, 256, n_in // 4), lambda i, p: (p[0], i, 0)), (d // 2 // 256,), "chip_sum_w_in"),
         0),
        (_blockwise(_four_sum, pos, [psum[1]] + [landed[1]] * 3,
                    [pl.BlockSpec((1, kv_rows, kv_cols), lambda i, p: (p[1], 0, 0))]
                    + from_chip((1, 1, kv_rows, kv_cols), 4),
                    jax.ShapeDtypeStruct((2, kv_rows, kv_cols), F32),
                    pl.BlockSpec((1, kv_rows, kv_cols), lambda i, p: (p[0], 0, 0)), (1,), "chip_sum_w_kv"),
         0),
        (_blockwise(_four_sum, pos, [psum[2]] + [landed[2]] * 3,
                    [pl.BlockSpec((1, wo_rows, d), lambda i, p: (p[1], 0, 0))] + from_chip((1, 1, wo_rows, d), 4),
                    jax.ShapeDtypeStruct((2, wo_rows, d), F32),
                    pl.BlockSpec((1, wo_rows, d), lambda i, p: (p[0], 0, 0)), (1,), "chip_sum_w_out"),
         0),
        (_blockwise(_four_sum, pos, [psum[3]] + [landed[3]] * 3,
                    [pl.BlockSpec((4, 1, pw_rows, GROUP), lambda i, p: (0, p[1], 0, 0))]
                    + from_chip((1, 4, 1, pw_rows, GROUP), 5),
                    jax.ShapeDtypeStruct((4, 2, pw_rows, GROUP), F32),
                    pl.BlockSpec((4, 1, pw_rows, GROUP), lambda i, p: (0, p[0], 0, 0)), (1,), "chip_sum_pool_w"),
         1),
    ])

    small_local = dict(norm_pre=g_norm_pre, pool_scale=g_pool_scale, sgu_ln_g=g_ln_g, sgu_ln_b=g_ln_b,
                       sgu_w=g_sgu_w, sgu_b=g_sgu_b_t.T, mem_norm=g_mem_norm, branch_norm=g_branch_norm,
                       norm_post=g_norm_post)
    small_rows = sum(weights[n].size for n in SMALL) // 128
    small_sum, *joined = _allreduce_small(
        _pack([small_local[n] for n in SMALL], [jnp.pad(loss_local, ((0, 7), (0, 127)))]), join_rider)
    grads = {"w_in": joined[0].reshape(w_in.shape), "w_kv": joined[1].reshape(w_kv.shape),
             "w_out": joined[2].reshape(w_out.shape), "pool_w": joined[3].reshape(pool_w.shape)}
    for n, g in zip(SMALL, _unpack(small_sum, [weights[n] for n in SMALL])):
        grads[n] = g
    loss = small_sum[small_rows, 0]

    delta, new_m, new_v = {}, {}, {}
    packed = [small_sum if src is grads else _pack([src[n] for n in SMALL]) for src in (weights, grads, mom1, mom2)]
    outs = _elementwise(_adamw, packed, [F32, F32, F32], "adamw_small")
    for dst, o in zip((delta, new_m, new_v), outs):
        for n, a in zip(SMALL, _unpack(o, [weights[n] for n in SMALL])):
            dst[n] = a
    for n in LARGE:
        cols = weights[n].shape[-1]
        outs = _elementwise(_adamw, [src[n].reshape(-1, cols) for src in (weights, grads, mom1, mom2)],
                            [F32, F32, F32], "adamw_" + n)
        for dst, o in zip((delta, new_m, new_v), outs):
            dst[n] = o.reshape(weights[n].shape)

    return (loss, grad_x[None], *[grads[n] for n in ORDER], *[delta[n] for n in ORDER],
            *[new_m[n] for n in ORDER], *[new_v[n] for n in ORDER])
```

```python
import functools

import jax
import jax.numpy as jnp
from jax import lax
from jax.experimental import pallas as pl
from jax.experimental.pallas import tpu as pltpu

F32 = jnp.float32
BF16 = jnp.bfloat16
EPS = 1e-6
MESH = pl.DeviceIdType.MESH
ANY = pl.BlockSpec(memory_space=pl.ANY)

POOL_WINDOWS = (2, 4, 8, 16)
GROUP = 256
HALO = 16
CHUNK = 128
N_SGU_HEADS = 8
N_ATT_HEADS = 4
ATT_DIM = 256
WIDTH = 1024
ATT_SCALE = 1.0 / 16.0

ADAM_LR = 0.001
ADAM_B1 = 0.9
ADAM_B2 = 0.999
ADAM_EPS = 1e-08
ADAM_WD = 0.01
ADAM_STEP = 10

VMEM_LIMIT = 60 * 1024 * 1024
ELEMENTWISE_VMEM = 24 * 1024 * 1024
LANES = 128
BF16_ROWS = 16
MAX_PARTS = 4


def _params(n_grid_axes, vmem=VMEM_LIMIT):
    return pltpu.CompilerParams(dimension_semantics=("arbitrary",) * n_grid_axes, vmem_limit_bytes=vmem)


def _dot(a, b, dims):
    return lax.dot_general(a, b, (dims, ((), ())), preferred_element_type=F32)


NN = ((1,), (0,))
NT = ((1,), (1,))
TN = ((0,), (0,))


class _Rider:
    def __init__(self, inputs, out_shapes, n_sems, run, aliases=None):
        self.inputs, self.out_shapes, self.n_sems, self.run = list(inputs), list(out_shapes), n_sems, run
        self.aliases = aliases or {}


def _call(body, name, grid, in_specs, out_specs, out_shape, scratch_shapes, inputs, rider=None, prefetch=None,
          aliases=None, rider_refs=False):
    n_in, n_out, n_scr = len(in_specs), len(out_specs), len(scratch_shapes)
    r_in = len(rider.inputs) if rider else 0
    r_out = len(rider.out_shapes) if rider else 0
    n_pre = 0 if prefetch is None else 1

    def whole_body(*refs):
        pre, refs = refs[:n_pre], refs[n_pre:]
        ins, rider_ins = refs[:n_in], refs[n_in:n_in + r_in]
        refs = refs[n_in + r_in:]
        outs, rider_outs = refs[:n_out], refs[n_out:n_out + r_out]
        refs = refs[n_out + r_out:]
        scratch, sems = refs[:n_scr], refs[n_scr:]
        extra = {"rider_outs": rider_outs} if rider_refs else {}
        if rider is None:
            body(*pre, *ins, *outs, *scratch, **extra)
            return
        ids = [pl.program_id(ax) for ax in range(len(grid))]
        first = functools.reduce(lambda p, q: p & q, [i == 0 for i in ids])
        last = functools.reduce(lambda p, q: p & q, [i == g - 1 for i, g in zip(ids, grid)])

        @pl.when(first)
        def _():
            rider.run(rider_ins, rider_outs, *sems, True)

        body(*pre, *ins, *outs, *scratch, **extra)

        @pl.when(last)
        def _():
            rider.run(rider_ins, rider_outs, *sems, False)

    io_aliases = {n_pre + i: o for i, o in (aliases or {}).items()}
    scratch_all = list(scratch_shapes)
    if rider:
        io_aliases.update({n_pre + n_in + i: n_out + o for i, o in rider.aliases.items()})
        scratch_all += [pltpu.SemaphoreType.DMA((rider.n_sems,)), pltpu.SemaphoreType.DMA((rider.n_sems,))]
    specs = dict(grid=grid, in_specs=list(in_specs) + [ANY] * r_in, out_specs=tuple(out_specs) + (ANY,) * r_out,
                 scratch_shapes=scratch_all)
    if n_pre:
        specs = dict(grid_spec=pltpu.PrefetchScalarGridSpec(num_scalar_prefetch=1, **specs))
    outs = pl.pallas_call(
        whole_body, name=name, **specs,
        out_shape=tuple(out_shape) + tuple(rider.out_shapes if rider else ()),
        input_output_aliases=io_aliases, compiler_params=_params(len(grid)),
    )(*([prefetch] if n_pre else []), *inputs, *(rider.inputs if rider else []))
    return tuple(outs)


def _grad_rows(a_t, b, pos, row_of, m, tm, tn, tk, name, rider=None):
    k, n = a_t.shape[1], b.shape[1]
    nk = k // tk
    out_dtype, dims, a = BF16, NN, a_t
    a_spec = pl.BlockSpec((tm, tk), lambda i, j, kk, p: (row_of(i, p), kk))
    b_spec = pl.BlockSpec((tk, tn), lambda i, j, kk, p: (kk, j))

    def body(pos_ref, a_ref, b_ref, o_ref, *acc):
        part = lambda: _dot(a_ref[...], b_ref[...], dims)
        if nk == 1:
            o_ref[...] = part().astype(out_dtype)
            return
        (acc_ref,) = acc
        kk = pl.program_id(2)

        @pl.when(kk == 0)
        def _():
            acc_ref[...] = part()

        @pl.when((kk > 0) & (kk < nk - 1))
        def _():
            acc_ref[...] += part()

        @pl.when(kk == nk - 1)
        def _():
            o_ref[...] = (acc_ref[...] + part()).astype(out_dtype)

    return _call(body, name, (m // tm, n // tn, nk), [a_spec, b_spec],
                 [pl.BlockSpec((tm, tn), lambda i, j, kk, p: (i, j))], [jax.ShapeDtypeStruct((m, n), out_dtype)],
                 [pltpu.VMEM((tm, tn), F32)] if nk > 1 else [], [a, b], rider, prefetch=pos)


def _proj_piece(chips, first, n_shards, src, g_pre, w_in, proj_in, n_cols, rider, tm, name, casts=()):
    s, d = src.shape
    cols = n_cols // 4
    fused = g_pre is not None
    n_steps = s // tm

    def body(chips_ref, *refs, rider_outs=()):
        refs = list(refs)
        src_ref = refs.pop(0)
        g_ref = refs.pop(0) if fused else None
        w_ref = refs.pop(0) if w_in is not None else rider_outs[0]
        if proj_in is not None:
            refs.pop(0)
        shard_refs = [refs.pop(0) for _ in casts]
        proj_ref = refs.pop(0)
        h_ref, ht_ref = (refs.pop(0), refs.pop(0)) if fused else (None, None)
        for shard_ref in shard_refs:
            refs.pop(0)[...] = shard_ref[...].astype(BF16)
        wbuf, sem = refs
        q, i = pl.program_id(0), pl.program_id(1)

        @pl.when(i == 0)
        def _():
            at = pl.multiple_of(chips_ref[first + q] * cols, LANES)
            cp = pltpu.make_async_copy(w_ref.at[:, pl.ds(at, cols)], wbuf, sem)
            cp.start()
            cp.wait()

        if fused:
            xv = src_ref[...]
            r = lax.rsqrt(jnp.mean(xv * xv, axis=-1, keepdims=True) + EPS)
            h = (xv * r * g_ref[...]).astype(BF16)
            h_ref[...] = h
            ht_ref[...] = h.T
        else:
            h = src_ref[...]
        proj_ref[...] = _dot(h, wbuf[...], NN)

    row = lambda q, i, ch: (i, 0)
    inputs, in_specs = [src], [pl.BlockSpec((tm, d), row)]
    if fused:
        inputs.append(g_pre)
        in_specs.append(pl.BlockSpec((1, d), lambda q, i, ch: (0, 0)))
    if w_in is not None:
        inputs.append(w_in)
        in_specs.append(ANY)
    aliases = {}
    if proj_in is not None:
        aliases[len(inputs)] = 0
        inputs.append(proj_in)
        in_specs.append(ANY)
    out_specs = [pl.BlockSpec((tm, cols), lambda q, i, ch: (i, ch[first + q]))]
    out_shape = [jax.ShapeDtypeStruct((s, n_cols), F32)]
    if fused:
        assert n_shards == 1
        out_specs += [pl.BlockSpec((tm, d), row), pl.BlockSpec((d, tm), lambda q, i, ch: (0, i))]
        out_shape += [jax.ShapeDtypeStruct((s, d), BF16), jax.ShapeDtypeStruct((d, s), BF16)]
    for shard, kind in casts:
        assert n_shards == 1
        inputs.append(shard)
        if kind == 3:
            in_specs.append(pl.BlockSpec(shard.shape, lambda q, i, ch: (0, 0, 0)))
            out_specs.append(pl.BlockSpec(shard.shape, lambda q, i, ch: (0, ch[0], 0)))
            out_shape.append(jax.ShapeDtypeStruct((shard.shape[0], 4 * shard.shape[1], shard.shape[2]), BF16))
        else:
            block = (shard.shape[0] // n_steps, shard.shape[1])
            in_specs.append(pl.BlockSpec(block, row))
            out_specs.append(pl.BlockSpec(block, lambda q, i, ch: (ch[0] * n_steps + i, 0)))
            out_shape.append(jax.ShapeDtypeStruct((4 * shard.shape[0], shard.shape[1]), BF16))
    return _call(body, name, (n_shards, s // tm), in_specs, out_specs, out_shape,
                 [pltpu.VMEM((d, cols), BF16), pltpu.SemaphoreType.DMA(())], inputs, rider, prefetch=chips,
                 aliases=aliases, rider_refs=True)


def _kv_fwd(mem, g, w_kv):
    m, d = mem.shape

    def body(mem_ref, g_ref, w_ref, k_ref, v_ref):
        mv = mem_ref[...]
        r = lax.rsqrt(jnp.mean(mv * mv, axis=-1, keepdims=True) + EPS)
        mem_n = (mv * r * g_ref[...]).astype(BF16)
        kv = _dot(mem_n, w_ref[...], NN)
        k_ref[...] = kv[:, :WIDTH].astype(BF16)
        v_ref[...] = kv[:, WIDTH:].astype(BF16)

    return pl.pallas_call(
        body, name="kv_fwd",
        out_shape=(jax.ShapeDtypeStruct((m, WIDTH), BF16), jax.ShapeDtypeStruct((m, WIDTH), BF16)),
        compiler_params=_params(0),
    )(mem, g, w_kv)


def _kv_bwd(mem, g, w_kv, dk, dv):
    m, d = mem.shape
    n = w_kv.shape[1]
    col = 512

    def body(mem_ref, g_ref, w_ref, dk_ref, dv_ref, dw_ref, dg_ref):
        mv = mem_ref[...]
        r = lax.rsqrt(jnp.mean(mv * mv, axis=-1, keepdims=True) + EPS)
        mem_hat = mv * r
        mem_n = (mem_hat * g_ref[...]).astype(BF16)
        dkv = jnp.concatenate([dk_ref[...], dv_ref[...]], axis=1).astype(BF16)
        for j in range(n // col):
            dw_ref[:, j * col:(j + 1) * col] = _dot(mem_n, dkv[:, j * col:(j + 1) * col], TN).astype(BF16)
        dmem_n = _dot(dkv, w_ref[...], NT)
        dg_ref[...] = jnp.sum(dmem_n * mem_hat, axis=0, keepdims=True)

    return pl.pallas_call(
        body, name="kv_bwd",
        out_shape=(jax.ShapeDtypeStruct((d, n), BF16), jax.ShapeDtypeStruct((1, d), F32)),
        compiler_params=_params(0),
    )(mem, g, w_kv, dk, dv)


def _sigmoid(x):
    return 1.0 / (1.0 + jnp.exp(-x))


def _inv_counts(t0, t):
    pos = (t0 + lax.broadcasted_iota(jnp.int32, (t, 1), 0) + 1).astype(F32)
    return [1.0 / jnp.minimum(pos, float(w)) for w in POOL_WINDOWS]


def _window_sums(ext, t, backward):
    n = t + HALO
    parts = []
    for gi, w in enumerate(POOL_WINDOWS):
        s = ext[:, gi * GROUP:(gi + 1) * GROUP]
        k = 1
        while k < w:
            s = s + pltpu.roll(s, (n - k) if backward else k, axis=0)
            k *= 2
        parts.append(s[:t] if backward else s[HALO:])
    return parts


def _pool_fwd(xa, halo, inv, pool_w):
    t = xa.shape[0]
    sums = _window_sums(jnp.concatenate([halo, xa], axis=0), t, backward=False)
    d = jnp.concatenate([sums[gi] * inv[gi] - xa[:, gi * GROUP:(gi + 1) * GROUP] for gi in range(4)], axis=1)
    d = d.astype(BF16)
    y = jnp.concatenate([_dot(d[:, gi * GROUP:(gi + 1) * GROUP], pool_w[gi], NN) for gi in range(4)], axis=1)
    return d, y


def _layernorm_fwd(v):
    mu = jnp.mean(v, axis=-1, keepdims=True)
    xc = v - mu
    rstd = lax.rsqrt(jnp.mean(xc * xc, axis=-1, keepdims=True) + EPS)
    return xc * rstd, rstd


def _tril_mask(transposed):
    r = lax.broadcasted_iota(jnp.int32, (CHUNK, CHUNK), 0)
    c = lax.broadcasted_iota(jnp.int32, (CHUNK, CHUNK), 1)
    return (r <= c) if transposed else (r >= c)


def _sgu_mix(w_ref, vals, transposed):
    t = vals.shape[0]
    mask = _tril_mask(transposed)
    ws = [jnp.where(mask, w_ref[h], 0.0).astype(BF16) for h in range(N_SGU_HEADS)]
    rows = []
    for ci in range(t // CHUNK):
        blk = vals[ci * CHUNK:(ci + 1) * CHUNK]
        rows.append(jnp.concatenate(
            [_dot(ws[h], blk[:, h * CHUNK:(h + 1) * CHUNK], NN) for h in range(N_SGU_HEADS)], axis=1))
    return jnp.concatenate(rows, axis=0)


def _attn_fwd(q, k, v):
    ps, os_ = [], []
    for h in range(N_ATT_HEADS):
        sl = slice(h * ATT_DIM, (h + 1) * ATT_DIM)
        s = _dot(q[:, sl], k[:, sl], NT) * ATT_SCALE
        s = s - jnp.max(s, axis=-1, keepdims=True)
        e = jnp.exp(s)
        p = e * (1.0 / jnp.sum(e, axis=-1, keepdims=True))
        ps.append(p)
        os_.append(_dot(p.astype(BF16), v[:, sl], NN))
    return ps, jnp.concatenate(os_, axis=1)


def _rms_branch(y_pre):
    r = lax.rsqrt(jnp.mean(y_pre * y_pre, axis=-1, keepdims=True) + EPS)
    return y_pre * r, r


def _branch_specs(t, n_tiles, order):
    width_in = 7 * WIDTH
    tile = lambda i: order(i)
    per_halo = t // HALO
    const2 = lambda i: (0, 0)
    const3 = lambda i: (0, 0, 0)
    return [
        pl.BlockSpec((t, width_in), lambda i: (tile(i), 0)),
        pl.BlockSpec((HALO, WIDTH), lambda i: (jnp.maximum(tile(i) * per_halo - 1, 0), 0)),
        pl.BlockSpec((4, GROUP, GROUP), const3),
        pl.BlockSpec((1, WIDTH), const2),
        pl.BlockSpec((1, WIDTH), const2),
        pl.BlockSpec((1, WIDTH), const2),
        pl.BlockSpec((N_SGU_HEADS, CHUNK, CHUNK), const3),
        pl.BlockSpec((CHUNK, WIDTH), const2),
        pl.BlockSpec((MEM_ROWS, WIDTH), const2),
        pl.BlockSpec((MEM_ROWS, WIDTH), const2),
        pl.BlockSpec((1, 3 * WIDTH), const2),
    ]


MEM_ROWS = 256


def _branches_fwd(proj, pool_w, pool_scale, ln_g, ln_b, sgu_w, bias_full, k, v, branch_norm, t, rider=None):
    s = proj.shape[0]
    n_tiles = s // t

    def body(proj_ref, halo_ref, pw_ref, ps_ref, lg_ref, lb_ref, sw_ref, sb_ref, k_ref, v_ref, bn_ref, y_ref, yt_ref):
        i = pl.program_id(0)
        col = lambda j: proj_ref[:, j * WIDTH:(j + 1) * WIDTH]

        def put(branch, y_pre):
            sl = slice(branch * WIDTH, (branch + 1) * WIDTH)
            val = (_rms_branch(y_pre)[0] * bn[:, sl]).astype(BF16)
            y_ref[:, sl] = val
            yt_ref[sl, :] = val.T

        bn = bn_ref[...]
        halo = jnp.where(i > 0, halo_ref[...], 0.0)
        _, y_pool = _pool_fwd(col(0), halo, _inv_counts(i * t, t), pw_ref[...])
        ga = col(1)
        ya = y_pool * ps_ref[...] * (ga * _sigmoid(ga))
        put(0, ya)
        vhat, _ = _layernorm_fwd(col(3))
        vn = (vhat * lg_ref[...] + lb_ref[...]).astype(BF16)
        z = _sgu_mix(sw_ref, vn, transposed=False) + jnp.tile(sb_ref[...], (t // CHUNK, 1))
        gb = col(4)
        yb = col(2) * z * (gb * _sigmoid(gb))
        put(1, yb)
        _, o = _attn_fwd(col(5).astype(BF16), k_ref[...], v_ref[...])
        gc = col(6)
        yc = o * (gc * _sigmoid(gc))
        put(2, yc)

    return _call(body, "branches_fwd", (n_tiles,), _branch_specs(t, n_tiles, lambda i: i),
                 [pl.BlockSpec((t, 3 * WIDTH), lambda i: (i, 0)), pl.BlockSpec((3 * WIDTH, t), lambda i: (0, i))],
                 [jax.ShapeDtypeStruct((s, 3 * WIDTH), BF16), jax.ShapeDtypeStruct((3 * WIDTH, s), BF16)], [],
                 [proj, proj, pool_w, pool_scale, ln_g, ln_b, sgu_w, bias_full, k, v, branch_norm], rider)


def _branches_bwd(proj, dy, pool_w, pool_scale, ln_g, ln_b, sgu_w, sgu_wt, bias_full, k, v, branch_norm, t, rider=None):
    s = proj.shape[0]
    n_tiles = s // t
    n_chunks = t // CHUNK
    order = lambda i: n_tiles - 1 - i

    def body(proj_ref, halo_ref, pw_ref, ps_ref, lg_ref, lb_ref, sw_ref, sb_ref, k_ref, v_ref, bn_ref,
             swt_ref, dy_ref,
             dproj_ref, dpw_ref, dps_ref, dlg_ref, dlb_ref, dsw_ref, dsb_ref, dbn_ref, dk_ref, dv_ref,
             carry_ref, dbias_ref):
        step = pl.program_id(0)
        i = order(step)

        @pl.when(step == 0)
        def _():
            for ref in (dpw_ref, dps_ref, dlg_ref, dlb_ref, dsw_ref, dbn_ref, dk_ref, dv_ref, carry_ref, dbias_ref):
                ref[...] = jnp.zeros(ref.shape, ref.dtype)

        col = lambda j: proj_ref[:, j * WIDTH:(j + 1) * WIDTH]
        bn = bn_ref[...]

        def norm_bwd(y_pre, sl):
            yhat, r = _rms_branch(y_pre)
            dyv = dy_ref[:, sl].astype(F32)
            dbn_ref[:, sl] += jnp.sum(dyv * yhat, axis=0, keepdims=True)
            dyhat = dyv * bn[:, sl]
            return r * (dyhat - yhat * jnp.mean(dyhat * yhat, axis=-1, keepdims=True))

        def gate(gv):
            sg = _sigmoid(gv)
            return gv * sg, sg * (1.0 + gv * (1.0 - sg))

        inv = _inv_counts(i * t, t)
        halo = jnp.where(i > 0, halo_ref[...], 0.0)
        pw = pw_ref[...]
        d, y_pool = _pool_fwd(col(0), halo, inv, pw)
        scale = ps_ref[...]
        silu_a, dsilu_a = gate(col(1))
        pa = y_pool * scale
        dya = norm_bwd(pa * silu_a, slice(0, WIDTH))
        dproj_ref[:, WIDTH:2 * WIDTH] = (dya * pa * dsilu_a).astype(BF16)
        dpa = dya * silu_a
        dps_ref[...] += jnp.sum(dpa * y_pool, axis=0, keepdims=True)
        dy_pool = (dpa * scale).astype(BF16)
        dd_parts, ddc_parts = [], []
        for gi in range(4):
            sl = slice(gi * GROUP, (gi + 1) * GROUP)
            dpw_ref[gi] += _dot(d[:, sl], dy_pool[:, sl], TN)
            dd = _dot(dy_pool[:, sl], pw[gi], NT)
            dd_parts.append(dd)
            ddc_parts.append(dd * inv[gi])
        ddc = jnp.concatenate(ddc_parts, axis=1)
        sums = _window_sums(jnp.concatenate([ddc, carry_ref[...]], axis=0), t, backward=True)
        carry_ref[...] = ddc[:HALO]
        dproj_ref[:, 0:WIDTH] = jnp.concatenate([sums[gi] - dd_parts[gi] for gi in range(4)], axis=1).astype(BF16)

        vhat, rstd = _layernorm_fwd(col(3))
        lg = lg_ref[...]
        vn = (vhat * lg + lb_ref[...]).astype(BF16)
        z = _sgu_mix(sw_ref, vn, transposed=False) + jnp.tile(sb_ref[...], (n_chunks, 1))
        u = col(2)
        silu_b, dsilu_b = gate(col(4))
        uz = u * z
        dyb = norm_bwd(uz * silu_b, slice(WIDTH, 2 * WIDTH))
        dproj_ref[:, 4 * WIDTH:5 * WIDTH] = (dyb * uz * dsilu_b).astype(BF16)
        duz = dyb * silu_b
        dproj_ref[:, 2 * WIDTH:3 * WIDTH] = (duz * z).astype(BF16)
        dz = duz * u
        dz_b = dz.astype(BF16)
        for ci in range(n_chunks):
            rows = slice(ci * CHUNK, (ci + 1) * CHUNK)
            dbias_ref[...] += dz[rows]
            for h in range(N_SGU_HEADS):
                sl = slice(h * CHUNK, (h + 1) * CHUNK)
                dsw_ref[h] += _dot(dz_b[rows, sl], vn[rows, sl], NT)
        dvn = _sgu_mix(swt_ref, dz_b, transposed=True)
        dlg_ref[...] += jnp.sum(dvn * vhat, axis=0, keepdims=True)
        dlb_ref[...] += jnp.sum(dvn, axis=0, keepdims=True)
        dvhat = dvn * lg
        dvb = rstd * (dvhat - jnp.mean(dvhat, axis=-1, keepdims=True)
                      - vhat * jnp.mean(dvhat * vhat, axis=-1, keepdims=True))
        dproj_ref[:, 3 * WIDTH:4 * WIDTH] = dvb.astype(BF16)

        q = col(5).astype(BF16)
        kv_k, kv_v = k_ref[...], v_ref[...]
        ps, o = _attn_fwd(q, kv_k, kv_v)
        silu_c, dsilu_c = gate(col(6))
        dyc = norm_bwd(o * silu_c, slice(2 * WIDTH, 3 * WIDTH))
        dproj_ref[:, 6 * WIDTH:7 * WIDTH] = (dyc * o * dsilu_c).astype(BF16)
        do = (dyc * silu_c).astype(BF16)
        dq_parts = []
        for h in range(N_ATT_HEADS):
            sl = slice(h * ATT_DIM, (h + 1) * ATT_DIM)
            p = ps[h]
            dp = _dot(do[:, sl], kv_v[:, sl], NT)
            ds = (p * (dp - jnp.sum(p * dp, axis=-1, keepdims=True)) * ATT_SCALE).astype(BF16)
            dq_parts.append(_dot(ds, kv_k[:, sl], NN))
            dk_ref[:, sl] += _dot(ds, q[:, sl], TN)
            dv_ref[:, sl] += _dot(p.astype(BF16), do[:, sl], TN)
        dproj_ref[:, 5 * WIDTH:6 * WIDTH] = jnp.concatenate(dq_parts, axis=1).astype(BF16)

        @pl.when(step == n_tiles - 1)
        def _():
            keep = _tril_mask(transposed=False)
            for h in range(N_SGU_HEADS):
                dsw_ref[h] = jnp.where(keep, dsw_ref[h], 0.0)
            dsb_ref[...] = jnp.concatenate(
                [jnp.sum(dbias_ref[:, h * CHUNK:(h + 1) * CHUNK], axis=1, keepdims=True)
                 for h in range(N_SGU_HEADS)], axis=1)

    const2 = lambda i: (0, 0)
    const3 = lambda i: (0, 0, 0)
    out_shapes = (
        jax.ShapeDtypeStruct((s, 7 * WIDTH), BF16),
        jax.ShapeDtypeStruct((4, GROUP, GROUP), F32),
        jax.ShapeDtypeStruct((1, WIDTH), F32),
        jax.ShapeDtypeStruct((1, WIDTH), F32),
        jax.ShapeDtypeStruct((1, WIDTH), F32),
        jax.ShapeDtypeStruct((N_SGU_HEADS, CHUNK, CHUNK), F32),
        jax.ShapeDtypeStruct((CHUNK, N_SGU_HEADS), F32),
        jax.ShapeDtypeStruct((1, 3 * WIDTH), F32),
        jax.ShapeDtypeStruct((MEM_ROWS, WIDTH), F32),
        jax.ShapeDtypeStruct((MEM_ROWS, WIDTH), F32),
    )
    out_specs = (
        pl.BlockSpec((t, 7 * WIDTH), lambda i: (order(i), 0)),
        pl.BlockSpec((4, GROUP, GROUP), const3),
        pl.BlockSpec((1, WIDTH), const2),
        pl.BlockSpec((1, WIDTH), const2),
        pl.BlockSpec((1, WIDTH), const2),
        pl.BlockSpec((N_SGU_HEADS, CHUNK, CHUNK), const3),
        pl.BlockSpec((CHUNK, N_SGU_HEADS), const2),
        pl.BlockSpec((1, 3 * WIDTH), const2),
        pl.BlockSpec((MEM_ROWS, WIDTH), const2),
        pl.BlockSpec((MEM_ROWS, WIDTH), const2),
    )
    in_specs = _branch_specs(t, n_tiles, order) + [
        pl.BlockSpec((N_SGU_HEADS, CHUNK, CHUNK), const3),
        pl.BlockSpec((t, 3 * WIDTH), lambda i: (order(i), 0)),
    ]
    return _call(body, "branches_bwd", (n_tiles,), in_specs, out_specs, out_shapes,
                 [pltpu.VMEM((HALO, WIDTH), F32), pltpu.VMEM((CHUNK, WIDTH), F32)],
                 [proj, proj, pool_w, pool_scale, ln_g, ln_b, sgu_w, bias_full, k, v, branch_norm, sgu_wt, dy], rider)


def _out_loss(y, w_out, x, target, g_post, tm):
    s, d = x.shape
    e_w = y.shape[1]
    n_tiles = s // tm

    def body(y_ref, w_ref, x_ref, t_ref, g_ref, loss_ref, dz_ref, dout_ref, dy_ref, dg_ref, sq_ref):
        i = pl.program_id(0)

        @pl.when(i == 0)
        def _():
            sq_ref[...] = jnp.zeros(sq_ref.shape, F32)
            dg_ref[...] = jnp.zeros(dg_ref.shape, F32)

        w = w_ref[...]
        out = _dot(y_ref[...], w, NN)
        r = lax.rsqrt(jnp.mean(out * out, axis=-1, keepdims=True) + EPS)
        outn = out * r
        g = g_ref[...]
        err = (x_ref[...] + outn * g) - t_ref[...]
        sq_ref[...] += jnp.sum(err * err, axis=0, keepdims=True)
        dz = err * (1.0 / d)
        dz_ref[...] = dz
        dg_ref[...] += jnp.sum(dz * outn, axis=0, keepdims=True)
        doutn = dz * g
        dout = (r * (doutn - outn * jnp.mean(doutn * outn, axis=-1, keepdims=True))).astype(BF16)
        dout_ref[...] = dout
        dy_ref[...] = _dot(dout, w, NT).astype(BF16)

        @pl.when(i == n_tiles - 1)
        def _():
            loss_ref[...] = 0.5 * jnp.sum(sq_ref[...], axis=1, keepdims=True) * (1.0 / d)

    row = lambda i: (i, 0)
    const2 = lambda i: (0, 0)
    return pl.pallas_call(
        body, name="out_loss", grid=(n_tiles,),
        in_specs=[
            pl.BlockSpec((tm, e_w), row),
            pl.BlockSpec((e_w, d), const2, pipeline_mode=pl.Buffered(1)),
            pl.BlockSpec((tm, d), row),
            pl.BlockSpec((tm, d), row),
            pl.BlockSpec((1, d), const2),
        ],
        out_specs=(
            pl.BlockSpec((1, 1), const2),
            pl.BlockSpec((tm, d), row),
            pl.BlockSpec((tm, d), row),
            pl.BlockSpec((tm, e_w), row),
            pl.BlockSpec((1, d), const2),
        ),
        out_shape=(
            jax.ShapeDtypeStruct((1, 1), F32),
            jax.ShapeDtypeStruct((s, d), F32),
            jax.ShapeDtypeStruct((s, d), BF16),
            jax.ShapeDtypeStruct((s, e_w), BF16),
            jax.ShapeDtypeStruct((1, d), F32),
        ),
        scratch_shapes=[pltpu.VMEM((1, d), F32)],
        compiler_params=_params(1),
    )(y, w_out, x, target, g_post)


def _dx_call(dproj, w_in, x, dz, g_pre, tm, tk, rider=None):
    s, d = x.shape
    k_total = dproj.shape[1]
    nk = k_total // tk
    n_tiles = s // tm

    def body(dp_ref, w_ref, x_ref, dz_ref, g_ref, dx_ref, dg_ref, acc_ref):
        i, kk = pl.program_id(0), pl.program_id(1)
        part = lambda: _dot(dp_ref[...], w_ref[...], NT)

        @pl.when(kk == 0)
        def _():
            acc_ref[...] = part()

        @pl.when((kk > 0) & (kk < nk - 1))
        def _():
            acc_ref[...] += part()

        @pl.when((i == 0) & (kk == 0))
        def _():
            dg_ref[...] = jnp.zeros(dg_ref.shape, F32)

        @pl.when(kk == nk - 1)
        def _():
            dh = acc_ref[...] + part()
            xv = x_ref[...]
            r = lax.rsqrt(jnp.mean(xv * xv, axis=-1, keepdims=True) + EPS)
            xhat = xv * r
            dg_ref[...] += jnp.sum(dh * xhat, axis=0, keepdims=True)
            dxhat = dh * g_ref[...]
            dx_ref[...] = dz_ref[...] + r * (dxhat - xhat * jnp.mean(dxhat * xhat, axis=-1, keepdims=True))

    row = lambda i, kk: (i, 0)
    const2 = lambda i, kk: (0, 0)
    return _call(
        body, "dx", (n_tiles, nk),
        [
            pl.BlockSpec((tm, tk), lambda i, kk: (i, kk)),
            pl.BlockSpec((d, tk), lambda i, kk: (0, kk)),
            pl.BlockSpec((tm, d), row),
            pl.BlockSpec((tm, d), row),
            pl.BlockSpec((1, d), const2),
        ],
        [pl.BlockSpec((tm, d), row), pl.BlockSpec((1, d), const2)],
        [jax.ShapeDtypeStruct((s, d), F32), jax.ShapeDtypeStruct((1, d), F32)],
        [pltpu.VMEM((tm, d), F32)], [dproj, w_in, x, dz, g_pre], rider)


def _rows_tile(rows, cols, n_arrays, itemsize=4):
    budget = ELEMENTWISE_VMEM // (2 * n_arrays * cols * itemsize)
    if rows <= budget:
        return rows
    best = None
    for cand in range(16, rows + 1, 16):
        if rows % cand == 0 and cand <= max(budget, 16):
            best = cand
    return best if best is not None else rows


def _elementwise(fn, inputs, out_dtypes, name):
    rows, cols = inputs[0].shape
    tr = _rows_tile(rows, cols, len(inputs) + len(out_dtypes))
    n_in = len(inputs)

    def body(*refs):
        outs = fn(*[r[...] for r in refs[:n_in]])
        for o_ref, o in zip(refs[n_in:], outs):
            o_ref[...] = o.astype(o_ref.dtype)

    spec = pl.BlockSpec((tr, cols), lambda i: (i, 0))
    return pl.pallas_call(
        body, name=name, grid=(rows // tr,),
        in_specs=[spec] * n_in, out_specs=tuple([spec] * len(out_dtypes)),
        out_shape=tuple(jax.ShapeDtypeStruct((rows, cols), dt) for dt in out_dtypes),
        compiler_params=_params(1),
    )(*inputs)


def _blockwise(fn, pos, inputs, in_specs, out_shape, out_spec, grid, name):
    n_in = len(inputs)

    def body(pos_ref, *refs):
        o_ref = refs[n_in]
        (out,) = fn(*[r[...].reshape(o_ref.shape) for r in refs[:n_in]])
        o_ref[...] = out.astype(o_ref.dtype)

    return pl.pallas_call(
        body, name=name,
        grid_spec=pltpu.PrefetchScalarGridSpec(num_scalar_prefetch=1, grid=grid, in_specs=in_specs,
                                               out_specs=out_spec),
        out_shape=out_shape,
        compiler_params=_params(len(grid)),
    )(pos, *inputs)


def _cast_copy(x):
    return (x,)


def _pair_sum(mine, theirs):
    return ((mine.astype(F32) + theirs.astype(F32)),)


def _four_sum(own, t0, t1, t2):
    return ((((own.astype(F32) + t0.astype(F32)) + t1.astype(F32)) + t2.astype(F32)),)


def _adamw(w, g, m, v):
    m = ADAM_B1 * m + (1.0 - ADAM_B1) * g
    v = ADAM_B2 * v + (1.0 - ADAM_B2) * jnp.square(g)
    m_hat = m / (1.0 - ADAM_B1 ** ADAM_STEP)
    v_hat = v / (1.0 - ADAM_B2 ** ADAM_STEP)
    delta = -ADAM_LR * (m_hat / (jnp.sqrt(v_hat) + ADAM_EPS) + ADAM_WD * w)
    return delta, m, v


def _place():
    x, y, c = lax.axis_index("x"), lax.axis_index("y"), lax.axis_index("c")
    chips = [(1 - x, y), (x, 1 - y), (1 - x, 1 - y)]
    return x, y, c, chips


def _remote(src, dst, send_sem, recv_sem, to):
    return pltpu.make_async_remote_copy(src_ref=src, dst_ref=dst, send_sem=send_sem, recv_sem=recv_sem,
                                        device_id=to, device_id_type=MESH)


def _split(ref, plan):
    views = [ref]
    for axis, parts in plan:
        size = ref.shape[axis] // parts
        assert size * parts == ref.shape[axis]
        views = [v.at[tuple(pl.ds(q * size, size) if i == axis else slice(None) for i in range(len(ref.shape)))]
                 for v in views for q in range(parts)]
    return views


def _started(src, dst, send_sem, recv_sem, to):
    copy = _remote(src, dst, send_sem, recv_sem, to)
    copy.start()
    return copy


def _shard_half(kind, ref, chip, cc):
    if kind == 0:
        rows, cols = ref.shape[0] // 2, ref.shape[1] // 4
        return ref.at[pl.ds(cc * rows, rows), pl.ds(pl.multiple_of(chip * cols, LANES), cols)]
    if kind == 3:
        rows = ref.shape[1] // 8
        return ref.at[:, pl.ds(pl.multiple_of((2 * chip + cc) * rows, BF16_ROWS), rows), :]
    rows = ref.shape[0] // 8
    return ref.at[pl.ds(pl.multiple_of((2 * chip + cc) * rows, BF16_ROWS), rows), :]


def _relay_rider(full):
    kind = 0

    def quarter(ref, chip_no, cc, q):
        return _split(_shard_half(kind, ref, chip_no, cc), [(0, 2)])[q]

    def run(in_refs, full_refs, send_sems, recv_sems, start):
        (ref,) = full_refs
        x, y, c, chips = _place()
        sibling = (x, y, 1 - c)
        chip_no = [2 * ch[0] + ch[1] for ch in chips]
        if start:
            for p in (0, 1):
                held = quarter(ref, chip_no[1 - p], c, p)
                _remote(held, held, send_sems.at[p], recv_sems.at[p], (*chips[p], c)).start()
            return
        for p in (0, 1):
            landed = quarter(ref, chip_no[2], c, p)
            _remote(landed, landed, send_sems.at[p], recv_sems.at[p], (*chips[p], c)).wait_recv()
            _remote(landed, landed, send_sems.at[2], recv_sems.at[2], sibling).start()
        mine, theirs = _shard_half(kind, ref, chip_no[2], c), _shard_half(kind, ref, chip_no[2], 1 - c)
        _remote(mine, mine, send_sems.at[2], recv_sems.at[2], sibling).wait_send()
        _remote(theirs, theirs, send_sems.at[2], recv_sems.at[2], sibling).wait_recv()
        for p in (0, 1):
            held = quarter(ref, chip_no[1 - p], c, p)
            _remote(held, held, send_sems.at[p], recv_sems.at[p], (*chips[p], c)).wait_send()

    return _Rider([full], [jax.ShapeDtypeStruct(full.shape, full.dtype)], 3, run, aliases={0: 0})


def _riders(riders):
    def bounds(counts):
        ends = [sum(counts[:i + 1]) for i in range(len(counts))]
        return list(zip([0] + ends[:-1], ends))

    ins = bounds([len(r.inputs) for r in riders])
    outs = bounds([len(r.out_shapes) for r in riders])
    sems = bounds([r.n_sems for r in riders])

    class From:
        def __init__(self, sem_refs, base):
            self.sem_refs, self.base, self.at = sem_refs, base, self

        def __getitem__(self, k):
            return self.sem_refs.at[self.base + k]

    def run(in_refs, out_refs, send_sems, recv_sems, start):
        for r, (i0, i1), (o0, o1), (s0, _) in zip(riders, ins, outs, sems):
            r.run(in_refs[i0:i1], out_refs[o0:o1], From(send_sems, s0), From(recv_sems, s0), start)

    aliases = {}
    for r, (i0, _), (o0, _) in zip(riders, ins, outs):
        aliases.update({i0 + i: o0 + o for i, o in r.aliases.items()})
    return _Rider([a for r in riders for a in r.inputs], [o for r in riders for o in r.out_shapes],
                  sems[-1][1], run, aliases)


def _gather_rider(fulls, kinds, peers=(0, 1, 2)):
    n = len(fulls)
    full_half = _shard_half

    def run(in_refs, full_refs, send_sems, recv_sems, start):
        x, y, c, chips = _place()
        me = 2 * x + y
        sibling = (x, y, 1 - c)
        plans = [[(0, MAX_PARTS)], [(0, 2)], [(0, 2)], []]
        chips = [(p, chips[p]) for p in peers]
        across = lambda a, p, k: (3 * a + p) * MAX_PARTS + k
        onward = lambda a, p: 3 * n * MAX_PARTS + 3 * a + p

        def parts(a, chip_no, cc):
            return _split(full_half(kinds[a], full_refs[a], chip_no, cc), plans[kinds[a]])

        if start:
            for p, chip in chips:
                for a in range(n):
                    for k, mine in enumerate(parts(a, me, c)):
                        _remote(mine, mine, send_sems.at[across(a, p, k)], recv_sems.at[across(a, p, k)],
                                (*chip, c)).start()
            return
        for k in range(MAX_PARTS):
            for p, chip in chips:
                for a in range(n):
                    landed = parts(a, 2 * chip[0] + chip[1], c)
                    if k < len(landed):
                        _remote(landed[k], landed[k], send_sems.at[across(a, p, k)], recv_sems.at[across(a, p, k)],
                                (*chip, c)).wait_recv()
                        _remote(landed[k], landed[k], send_sems.at[onward(a, p)], recv_sems.at[onward(a, p)],
                                sibling).start()
        for p, chip in chips:
            them = 2 * chip[0] + chip[1]
            for a in range(n):
                passed = full_half(kinds[a], full_refs[a], them, 1 - c)
                _remote(passed, passed, send_sems.at[onward(a, p)], recv_sems.at[onward(a, p)], sibling).wait_recv()
                landed = full_half(kinds[a], full_refs[a], them, c)
                _remote(landed, landed, send_sems.at[onward(a, p)], recv_sems.at[onward(a, p)], sibling).wait_send()
                for k, mine in enumerate(parts(a, me, c)):
                    _remote(mine, mine, send_sems.at[across(a, p, k)], recv_sems.at[across(a, p, k)],
                            (*chip, c)).wait_send()

    return _Rider(fulls, [jax.ShapeDtypeStruct(f.shape, f.dtype) for f in fulls], 3 * n * (MAX_PARTS + 1), run,
                  aliases={a: a for a in range(n)})


def _exchange_rider(arrays, half_axes=None):
    n = len(arrays)
    half_axes = half_axes or [None] * n
    out_shapes = [jax.ShapeDtypeStruct(tuple(1 if i == ax else dim for i, dim in enumerate(g.shape)), g.dtype)
                  for g, ax in zip(arrays, half_axes)]

    def run(in_refs, out_refs, send_sems, recv_sems, start):
        x, y, c, _ = _place()
        sibling = (x, y, 1 - c)
        for a in range(n):
            src, ax = in_refs[a], half_axes[a]
            if ax is not None:
                src = src.at[tuple(pl.ds(1 - c, 1) if i == ax else slice(None) for i in range(len(src.shape)))]
            sems = (send_sems.at[a], recv_sems.at[a])
            if start:
                _started(src, out_refs[a], *sems, sibling)
            else:
                _remote(src, out_refs[a], *sems, sibling).wait()

    return _Rider(arrays, out_shapes, n, run)


def _scatter_rider(parts):
    n = len(parts)
    arrays = [p for p, _ in parts]

    def block_shape(p, ax):
        if ax == len(p.shape) - 1:
            return p.shape[:-1] + (p.shape[-1] // 4,)
        return tuple(1 if i == ax else dim for i, dim in enumerate(p.shape))

    out_shapes = [jax.ShapeDtypeStruct((3,) + block_shape(p, ax), p.dtype) for p, ax in parts]

    def block(ref, ax, chip):
        rank = len(ref.shape)
        if ax == rank - 1:
            cols = ref.shape[-1] // 4
            last = pl.ds(pl.multiple_of(chip * cols, LANES), cols)
            return ref.at[tuple([slice(None)] * (rank - 1) + [last])]
        return ref.at[tuple(pl.ds(chip, 1) if i == ax else slice(None) for i in range(rank))]

    def run(in_refs, out_refs, send_sems, recv_sems, start):
        x, y, c, chips = _place()
        for a in range(n):
            ax = parts[a][1]
            for p, chip in enumerate(chips):
                src, dst = block(in_refs[a], ax, 2 * chip[0] + chip[1]), out_refs[a].at[p]
                sems = (send_sems.at[3 * a + p], recv_sems.at[3 * a + p])
                if start:
                    _started(src, dst, *sems, (*chip, c))
                else:
                    _remote(src, dst, *sems, (*chip, c)).wait()

    return _Rider(arrays, out_shapes, 3 * n, run)


def _join_rider(joined):
    n = len(joined)
    arrays = [j for j, _ in joined]

    def run(in_refs, out_refs, send_sems, recv_sems, start):
        x, y, c, _ = _place()
        sibling = (x, y, 1 - c)

        def half(a, cc):
            rank = len(out_refs[a].shape)
            return out_refs[a].at[tuple(pl.ds(cc, 1) if i == joined[a][1] else slice(None) for i in range(rank))]

        for a in range(n):
            sems = (send_sems.at[a], recv_sems.at[a])
            if start:
                _started(half(a, c), half(a, c), *sems, sibling)
            else:
                _remote(half(a, c), half(a, c), *sems, sibling).wait_send()
                _remote(half(a, 1 - c), half(a, 1 - c), *sems, sibling).wait_recv()

    return _Rider(arrays, [jax.ShapeDtypeStruct(j.shape, j.dtype) for j in arrays], n, run,
                  aliases={a: a for a in range(n)})


def _allreduce_small(packed, rider):
    rows, lanes = packed.shape
    half = rows // 2
    r_in, r_out = len(rider.inputs), len(rider.out_shapes)

    def body(in_ref, *refs):
        rider_ins, out_ref, rider_outs = refs[:r_in], refs[r_in], refs[r_in + 1:r_in + 1 + r_out]
        pair_ref, gath_ref, send_sems, recv_sems, rider_send, rider_recv = refs[r_in + 1 + r_out:]
        rider.run(rider_ins, rider_outs, rider_send, rider_recv, True)
        x, y, c, chips = _place()
        me = 2 * x + y
        sibling = (x, y, 1 - c)
        mine = pl.ds(pl.multiple_of(c * half, 8), half)
        theirs = pl.ds(pl.multiple_of((1 - c) * half, 8), half)
        to_sib = _remote(in_ref.at[theirs], pair_ref, send_sems.at[0], recv_sems.at[0], sibling)
        to_sib.start()
        to_sib.wait()
        gath_ref[me] = in_ref[mine] + pair_ref[...]
        sends = [_remote(gath_ref.at[me], gath_ref.at[me], send_sems.at[1 + p], recv_sems.at[1 + p], (*chip, c))
                 for p, chip in enumerate(chips)]
        for cp in sends:
            cp.start()
        for p, chip in enumerate(chips):
            slot = gath_ref.at[2 * chip[0] + chip[1]]
            _remote(slot, slot, send_sems.at[1 + p], recv_sems.at[1 + p], (*chip, c)).wait_recv()
        for cp in sends:
            cp.wait_send()
        out_ref[mine] = ((gath_ref[0] + gath_ref[1]) + gath_ref[2]) + gath_ref[3]
        back = _remote(out_ref.at[mine], out_ref.at[mine], send_sems.at[4], recv_sems.at[4], sibling)
        back.start()
        back.wait_send()
        _remote(out_ref.at[theirs], out_ref.at[theirs], send_sems.at[4], recv_sems.at[4], sibling).wait_recv()
        rider.run(rider_ins, rider_outs, rider_send, rider_recv, False)

    vmem = pl.BlockSpec(memory_space=pltpu.VMEM)
    return pl.pallas_call(
        body, name="allreduce_small",
        in_specs=[vmem] + [ANY] * r_in, out_specs=(vmem,) + (ANY,) * r_out,
        out_shape=(jax.ShapeDtypeStruct((rows, lanes), F32),) + tuple(rider.out_shapes),
        scratch_shapes=[pltpu.VMEM((half, lanes), F32), pltpu.VMEM((4, half, lanes), F32),
                        pltpu.SemaphoreType.DMA((5,)), pltpu.SemaphoreType.DMA((5,)),
                        pltpu.SemaphoreType.DMA((rider.n_sems,)), pltpu.SemaphoreType.DMA((rider.n_sems,))],
        input_output_aliases={1 + i: 1 + o for i, o in rider.aliases.items()},
        compiler_params=pltpu.CompilerParams(has_side_effects=True, vmem_limit_bytes=32 * 1024 * 1024),
    )(packed, *rider.inputs)


SMALL = ("norm_pre", "pool_scale", "sgu_ln_g", "sgu_ln_b", "sgu_w", "sgu_b", "mem_norm", "branch_norm", "norm_post")
LARGE = ("w_in", "pool_w", "w_kv", "w_out")
ORDER = ("norm_pre", "w_in", "pool_w", "pool_scale", "sgu_ln_g", "sgu_ln_b", "sgu_w", "sgu_b", "mem_norm", "w_kv",
         "branch_norm", "w_out", "norm_post")


def _pack(arrays, extra=()):
    rows = [a.reshape(-1, 128) for a in arrays] + list(extra)
    pad = -sum(r.shape[0] for r in rows) % 16
    return jnp.concatenate(rows + ([jnp.zeros((pad, 128), F32)] if pad else []), axis=0)


def _unpack(packed, like):
    out, row = [], 0
    for a in like:
        rows = a.size // 128
        out.append(packed[row:row + rows].reshape(a.shape))
        row += rows
    return out


def kernel(x, mem, norm_pre, w_in, pool_w, pool_scale, sgu_ln_g, sgu_ln_b, sgu_w, sgu_b, mem_norm, w_kv, branch_norm, w_out, norm_post, loss_target, m_norm_pre, m_w_in, m_pool_w, m_pool_scale, m_sgu_ln_g, m_sgu_ln_b, m_sgu_w, m_sgu_b, m_mem_norm, m_w_kv, m_branch_norm, m_w_out, m_norm_post, v_norm_pre, v_w_in, v_pool_w, v_pool_scale, v_sgu_ln_g, v_sgu_ln_b, v_sgu_w, v_sgu_b, v_mem_norm, v_w_kv, v_branch_norm, v_w_out, v_norm_post):
    weights = dict(norm_pre=norm_pre, w_in=w_in, pool_w=pool_w, pool_scale=pool_scale, sgu_ln_g=sgu_ln_g,
                   sgu_ln_b=sgu_ln_b, sgu_w=sgu_w, sgu_b=sgu_b, mem_norm=mem_norm, w_kv=w_kv, branch_norm=branch_norm,
                   w_out=w_out, norm_post=norm_post)
    mom1 = dict(norm_pre=m_norm_pre, w_in=m_w_in, pool_w=m_pool_w, pool_scale=m_pool_scale, sgu_ln_g=m_sgu_ln_g,
                sgu_ln_b=m_sgu_ln_b, sgu_w=m_sgu_w, sgu_b=m_sgu_b, mem_norm=m_mem_norm, w_kv=m_w_kv,
                branch_norm=m_branch_norm, w_out=m_w_out, norm_post=m_norm_post)
    mom2 = dict(norm_pre=v_norm_pre, w_in=v_w_in, pool_w=v_pool_w, pool_scale=v_pool_scale, sgu_ln_g=v_sgu_ln_g,
                sgu_ln_b=v_sgu_ln_b, sgu_w=v_sgu_w, sgu_b=v_sgu_b, mem_norm=v_mem_norm, w_kv=v_w_kv,
                branch_norm=v_branch_norm, w_out=v_w_out, norm_post=v_norm_post)

    s, d = x.shape[1], x.shape[2]
    x2, mem2, tgt2 = x[0], mem[0], loss_target[0]
    t_branch = min(256, s)
    tm = min(512, s)

    core = lax.axis_index("c")
    chip = 2 * lax.axis_index("x") + lax.axis_index("y")
    pos = jnp.stack([core, chip]).astype(jnp.int32)
    n_in, n_kv, n_out = 4 * w_in.shape[2], 4 * w_kv.shape[1], 4 * w_out.shape[1]
    wi_rows, kv_rows, wo_rows = d // 8, n_kv // 8, n_out // 8

    kv_cols, pw_rows = w_kv.shape[2], GROUP // 8
    wi_own = _blockwise(_cast_copy, pos, [w_in[0]], [pl.BlockSpec((wi_rows, n_in // 4), lambda i, p: (i, 0))],
                        jax.ShapeDtypeStruct((d, n_in), BF16),
                        pl.BlockSpec((wi_rows, n_in // 4), lambda i, p: (i, p[1])), (8,), "place_w_in")

    x_pos, y_pos = lax.axis_index("x"), lax.axis_index("y")
    chips = jnp.stack([chip, 2 * (1 - x_pos) + y_pos, 2 * x_pos + 1 - y_pos,
                       2 * (1 - x_pos) + 1 - y_pos]).astype(jnp.int32)
    mem_g = mem_norm.reshape(1, d)
    proj, h, h_t, wkv_own, wo_own, pw_own, wi_full = _proj_piece(
        chips, 0, 1, x2, norm_pre, None, None, n_in, _gather_rider([wi_own], [0], peers=(0, 1)), tm, "proj_own",
        casts=[(w_kv[0], 1), (w_out[0], 2), (pool_w[0], 3)])
    proj, wi_full, wkv_full, pw_full = _proj_piece(
        chips, 1, 2, h, None, None, proj, n_in,
        _riders([_relay_rider(wi_full), _gather_rider([wkv_own, pw_own], [1, 3])]), tm, "proj_neighbours")
    proj, wo_part = _proj_piece(chips, 3, 1, h, None, wi_full, proj, n_in,
                                _gather_rider([wo_own], [2], peers=(0, 1)), tm, "proj_diagonal")
    k_m, v_m = _kv_fwd(mem2, mem_g, wkv_full)
    bias_full = jnp.repeat(sgu_b[0].T, CHUNK, axis=1)
    y, y_t, wo_full = _branches_fwd(proj, pw_full, pool_scale, sgu_ln_g, sgu_ln_b, sgu_w[0], bias_full, k_m, v_m,
                                    branch_norm, t_branch, _gather_rider([wo_part], [2], peers=(2,)))
    loss_local, dz, dout, dy, g_norm_post = _out_loss(y, wo_full, x2, tgt2, norm_post, min(256, s))

    tk = min(1024, s)
    (dproj, g_pw, g_pool_scale, g_ln_g, g_ln_b, g_sgu_w, g_sgu_b_t, g_branch_norm, dk, dv) = _branches_bwd(
        proj, dy, pw_full, pool_scale, sgu_ln_g, sgu_ln_b, sgu_w[0], jnp.swapaxes(sgu_w[0], 1, 2), bias_full,
        k_m, v_m, branch_norm, t_branch)
    g_wkv, g_mem_norm = _kv_bwd(mem2, mem_g, wkv_full, dk, dv)
    g_wkv = g_wkv.reshape(4, 2, kv_rows, kv_cols)
    g_pw = g_pw.astype(BF16).reshape(4, 4, 2, pw_rows, GROUP)
    g_wo, gkv_from_sibling, gpw_from_sibling = _grad_rows(y_t, dout, pos, lambda i, p: i, n_out, n_out // 2, 1024, tk,
                                                          "grad_w_out", _exchange_rider([g_wkv, g_pw], [1, 2]))
    g_wo = g_wo.reshape(4, 2, wo_rows, d)
    ps_kv = _blockwise(_pair_sum, pos, [g_wkv, gkv_from_sibling],
                       [pl.BlockSpec((1, 1, kv_rows, kv_cols), lambda i, p: (i, p[0], 0, 0)),
                        pl.BlockSpec((1, 1, kv_rows, kv_cols), lambda i, p: (i, 0, 0, 0))],
                       jax.ShapeDtypeStruct((4, kv_rows, kv_cols), BF16),
                       pl.BlockSpec((1, kv_rows, kv_cols), lambda i, p: (i, 0, 0)), (4,), "pair_sum_w_kv")
    ps_pw = _blockwise(_pair_sum, pos, [g_pw, gpw_from_sibling],
                       [pl.BlockSpec((1, 4, 1, pw_rows, GROUP), lambda i, p: (i, 0, p[0], 0, 0)),
                        pl.BlockSpec((1, 4, 1, pw_rows, GROUP), lambda i, p: (i, 0, 0, 0, 0))],
                       jax.ShapeDtypeStruct((4, 4, pw_rows, GROUP), BF16),
                       pl.BlockSpec((1, 4, pw_rows, GROUP), lambda i, p: (i, 0, 0, 0)), (4,), "pair_sum_pool_w")
    gwi_theirs, landed_kv, landed_pw, gwo_from_sibling = _grad_rows(
        h_t, dproj, pos, lambda i, p: 1 - p[0], d // 2, d // 2, n_in // 4, tk, "grad_w_in_sibling_half",
        _riders([_scatter_rider([(ps_kv, 0), (ps_pw, 1)]), _exchange_rider([g_wo], [1])]))
    ps_wo = _blockwise(_pair_sum, pos, [g_wo, gwo_from_sibling],
                       [pl.BlockSpec((1, 1, wo_rows, d), lambda i, p: (i, p[0], 0, 0)),
                        pl.BlockSpec((1, 1, wo_rows, d), lambda i, p: (i, 0, 0, 0))],
                       jax.ShapeDtypeStruct((4, wo_rows, d), BF16),
                       pl.BlockSpec((1, wo_rows, d), lambda i, p: (i, 0, 0)), (4,), "pair_sum_w_out")
    gwi_mine, gwi_from_sibling, landed_wo = _grad_rows(
        h_t, dproj, pos, lambda i, p: p[0], d // 2, d // 2, n_in // 4, tk, "grad_w_in_own_half",
        _riders([_exchange_rider([gwi_theirs]), _scatter_rider([(ps_wo, 0)])]))
    ps_wi = _elementwise(_pair_sum, [gwi_mine, gwi_from_sibling], [BF16], "pair_sum_w_in")[0]
    grad_x, g_norm_pre, landed_wi = _dx_call(dproj, wi_full, x2, dz, norm_pre, tm, 1024,
                                             _scatter_rider([(ps_wi, 1)]))
    psum = [ps_wi, ps_kv, ps_wo, ps_pw]
    landed = [landed_wi, landed_kv, landed_wo, landed_pw]
    from_chip = lambda spec_shape, rank: [
        pl.BlockSpec(spec_shape, functools.partial(lambda i, p, q: (q, i) + (0,) * (rank - 2), q=q))
        for q in range(3)]
    join_rider = _join_rider([
        (_blockwise(_four_sum, pos, [psum[0]] + [landed[0]] * 3,
                    [pl.BlockSpec((256, n_in // 4), lambda i, p: (i, p[1]))] + from_chip((1, 256, n_in // 4), 3),
                    jax.ShapeDtypeStruct((2, d // 2, n_in // 4), F32),
                    pl.BlockSpec((1, 256, n_in // 4), lambda i, p: (p[0], i, 0)), (d // 2 // 256,), "chip_sum_w_in"),
         0),
        (_blockwise(_four_sum, pos, [psum[1]] + [landed[1]] * 3,
                    [pl.BlockSpec((1, kv_rows, kv_cols), lambda i, p: (p[1], 0, 0))]
                    + from_chip((1, 1, kv_rows, kv_cols), 4),
                    jax.ShapeDtypeStruct((2, kv_rows, kv_cols), F32),
                    pl.BlockSpec((1, kv_rows, kv_cols), lambda i, p: (p[0], 0, 0)), (1,), "chip_sum_w_kv"),
         0),
        (_blockwise(_four_sum, pos, [psum[2]] + [landed[2]] * 3,
                    [pl.BlockSpec((1, wo_rows, d), lambda i, p: (p[1], 0, 0))] + from_chip((1, 1, wo_rows, d), 4),
                    jax.ShapeDtypeStruct((2, wo_rows, d), F32),
                    pl.BlockSpec((1, wo_rows, d), lambda i, p: (p[0], 0, 0)), (1,), "chip_sum_w_out"),
         0),
        (_blockwise(_four_sum, pos, [psum[3]] + [landed[3]] * 3,
                    [pl.BlockSpec((4, 1, pw_rows, GROUP), lambda i, p: (0, p[1], 0, 0))]
                    + from_chip((1, 4, 1, pw_rows, GROUP), 5),
                    jax.ShapeDtypeStruct((4, 2, pw_rows, GROUP), F32),
                    pl.BlockSpec((4, 1, pw_rows, GROUP), lambda i, p: (0, p[0], 0, 0)), (1,), "chip_sum_pool_w"),
         1),
    ])

    small_local = dict(norm_pre=g_norm_pre, pool_scale=g_pool_scale, sgu_ln_g=g_ln_g, sgu_ln_b=g_ln_b,
                       sgu_w=g_sgu_w, sgu_b=g_sgu_b_t.T, mem_norm=g_mem_norm, branch_norm=g_branch_norm,
                       norm_post=g_norm_post)
    small_rows = sum(weights[n].size for n in SMALL) // 128
    small_sum, *joined = _allreduce_small(
        _pack([small_local[n] for n in SMALL], [jnp.pad(loss_local, ((0, 7), (0, 127)))]), join_rider)
    grads = {"w_in": joined[0].reshape(w_in.shape), "w_kv": joined[1].reshape(w_kv.shape),
             "w_out": joined[2].reshape(w_out.shape), "pool_w": joined[3].reshape(pool_w.shape)}
    for n, g in zip(SMALL, _unpack(small_sum, [weights[n] for n in SMALL])):
        grads[n] = g
    loss = small_sum[small_rows, 0]

    delta, new_m, new_v = {}, {}, {}
    packed = [small_sum if src is grads else _pack([src[n] for n in SMALL]) for src in (weights, grads, mom1, mom2)]
    outs = _elementwise(_adamw, packed, [F32, F32, F32], "adamw_small")
    for dst, o in zip((delta, new_m, new_v), outs):
        for n, a in zip(SMALL, _unpack(o, [weights[n] for n in SMALL])):
            dst[n] = a
    for n in LARGE:
        cols = weights[n].shape[-1]
        outs = _elementwise(_adamw, [src[n].reshape(-1, cols) for src in (weights, grads, mom1, mom2)],
                            [F32, F32, F32], "adamw_" + n)
        for dst, o in zip((delta, new_m, new_v), outs):
            dst[n] = o.reshape(weights[n].shape)

    return (loss, grad_x[None], *[grads[n] for n in ORDER], *[delta[n] for n in ORDER],
            *[new_m[n] for n in ORDER], *[new_v[n] for n in ORDER])
```

```python
import functools

import jax
import jax.numpy as jnp
from jax import lax
from jax.experimental import pallas as pl
from jax.experimental.pallas import tpu as pltpu

F32 = jnp.float32
BF16 = jnp.bfloat16
EPS = 1e-6
MESH = pl.DeviceIdType.MESH
ANY = pl.BlockSpec(memory_space=pl.ANY)

POOL_WINDOWS = (2, 4, 8, 16)
GROUP = 256
HALO = 16
CHUNK = 128
N_SGU_HEADS = 8
N_ATT_HEADS = 4
ATT_DIM = 256
WIDTH = 1024
ATT_SCALE = 1.0 / 16.0

ADAM_LR = 0.001
ADAM_B1 = 0.9
ADAM_B2 = 0.999
ADAM_EPS = 1e-08
ADAM_WD = 0.01
ADAM_STEP = 10

VMEM_LIMIT = 60 * 1024 * 1024
ELEMENTWISE_VMEM = 24 * 1024 * 1024
LANES = 128
BF16_ROWS = 16
MAX_PARTS = 4


def _params(n_grid_axes, vmem=VMEM_LIMIT):
    return pltpu.CompilerParams(dimension_semantics=("arbitrary",) * n_grid_axes, vmem_limit_bytes=vmem)


def _dot(a, b, dims):
    return lax.dot_general(a, b, (dims, ((), ())), preferred_element_type=F32)


NN = ((1,), (0,))
NT = ((1,), (1,))
TN = ((0,), (0,))


class _Rider:
    def __init__(self, inputs, out_shapes, n_sems, run, aliases=None):
        self.inputs, self.out_shapes, self.n_sems, self.run = list(inputs), list(out_shapes), n_sems, run
        self.aliases = aliases or {}


def _call(body, name, grid, in_specs, out_specs, out_shape, scratch_shapes, inputs, rider=None, prefetch=None,
          aliases=None, rider_refs=False):
    n_in, n_out, n_scr = len(in_specs), len(out_specs), len(scratch_shapes)
    r_in = len(rider.inputs) if rider else 0
    r_out = len(rider.out_shapes) if rider else 0
    n_pre = 0 if prefetch is None else 1

    def whole_body(*refs):
        pre, refs = refs[:n_pre], refs[n_pre:]
        ins, rider_ins = refs[:n_in], refs[n_in:n_in + r_in]
        refs = refs[n_in + r_in:]
        outs, rider_outs = refs[:n_out], refs[n_out:n_out + r_out]
        refs = refs[n_out + r_out:]
        scratch, sems = refs[:n_scr], refs[n_scr:]
        extra = {"rider_outs": rider_outs} if rider_refs else {}
        if rider is None:
            body(*pre, *ins, *outs, *scratch, **extra)
            return
        ids = [pl.program_id(ax) for ax in range(len(grid))]
        first = functools.reduce(lambda p, q: p & q, [i == 0 for i in ids])
        last = functools.reduce(lambda p, q: p & q, [i == g - 1 for i, g in zip(ids, grid)])

        @pl.when(first)
        def _():
            rider.run(rider_ins, rider_outs, *sems, True)

        body(*pre, *ins, *outs, *scratch, **extra)

        @pl.when(last)
        def _():
            rider.run(rider_ins, rider_outs, *sems, False)

    io_aliases = {n_pre + i: o for i, o in (aliases or {}).items()}
    scratch_all = list(scratch_shapes)
    if rider:
        io_aliases.update({n_pre + n_in + i: n_out + o for i, o in rider.aliases.items()})
        scratch_all += [pltpu.SemaphoreType.DMA((rider.n_sems,)), pltpu.SemaphoreType.DMA((rider.n_sems,))]
    specs = dict(grid=grid, in_specs=list(in_specs) + [ANY] * r_in, out_specs=tuple(out_specs) + (ANY,) * r_out,
                 scratch_shapes=scratch_all)
    if n_pre:
        specs = dict(grid_spec=pltpu.PrefetchScalarGridSpec(num_scalar_prefetch=1, **specs))
    outs = pl.pallas_call(
        whole_body, name=name, **specs,
        out_shape=tuple(out_shape) + tuple(rider.out_shapes if rider else ()),
        input_output_aliases=io_aliases, compiler_params=_params(len(grid)),
    )(*([prefetch] if n_pre else []), *inputs, *(rider.inputs if rider else []))
    return tuple(outs)


def _grad_rows(a_t, b, pos, row_of, m, tm, tn, tk, name, rider=None):
    k, n = a_t.shape[1], b.shape[1]
    nk = k // tk
    out_dtype, dims, a = BF16, NN, a_t
    a_spec = pl.BlockSpec((tm, tk), lambda i, j, kk, p: (row_of(i, p), kk))
    b_spec = pl.BlockSpec((tk, tn), lambda i, j, kk, p: (kk, j))

    def body(pos_ref, a_ref, b_ref, o_ref, *acc):
        part = lambda: _dot(a_ref[...], b_ref[...], dims)
        if nk == 1:
            o_ref[...] = part().astype(out_dtype)
            return
        (acc_ref,) = acc
        kk = pl.program_id(2)

        @pl.when(kk == 0)
        def _():
            acc_ref[...] = part()

        @pl.when((kk > 0) & (kk < nk - 1))
        def _():
            acc_ref[...] += part()

        @pl.when(kk == nk - 1)
        def _():
            o_ref[...] = (acc_ref[...] + part()).astype(out_dtype)

    return _call(body, name, (m // tm, n // tn, nk), [a_spec, b_spec],
                 [pl.BlockSpec((tm, tn), lambda i, j, kk, p: (i, j))], [jax.ShapeDtypeStruct((m, n), out_dtype)],
                 [pltpu.VMEM((tm, tn), F32)] if nk > 1 else [], [a, b], rider, prefetch=pos)


def _proj_piece(chips, first, n_shards, src, g_pre, w_in, proj_in, n_cols, rider, tm, name, casts=()):
    s, d = src.shape
    cols = n_cols // 4
    fused = g_pre is not None
    n_steps = s // tm

    def body(chips_ref, *refs, rider_outs=()):
        refs = list(refs)
        src_ref = refs.pop(0)
        g_ref = refs.pop(0) if fused else None
        w_ref = refs.pop(0) if w_in is not None else rider_outs[0]
        if proj_in is not None:
            refs.pop(0)
        shard_refs = [refs.pop(0) for _ in casts]
        proj_ref = refs.pop(0)
        h_ref, ht_ref = (refs.pop(0), refs.pop(0)) if fused else (None, None)
        for shard_ref in shard_refs:
            refs.pop(0)[...] = shard_ref[...].astype(BF16)
        wbuf, sem = refs
        q, i = pl.program_id(0), pl.program_id(1)

        @pl.when(i == 0)
        def _():
            at = pl.multiple_of(chips_ref[first + q] * cols, LANES)
            cp = pltpu.make_async_copy(w_ref.at[:, pl.ds(at, cols)], wbuf, sem)
            cp.start()
            cp.wait()

        if fused:
            xv = src_ref[...]
            r = lax.rsqrt(jnp.mean(xv * xv, axis=-1, keepdims=True) + EPS)
            h = (xv * r * g_ref[...]).astype(BF16)
            h_ref[...] = h
            ht_ref[...] = h.T
        else:
            h = src_ref[...]
        proj_ref[...] = _dot(h, wbuf[...], NN)

    row = lambda q, i, ch: (i, 0)
    inputs, in_specs = [src], [pl.BlockSpec((tm, d), row)]
    if fused:
        inputs.append(g_pre)
        in_specs.append(pl.BlockSpec((1, d), lambda q, i, ch: (0, 0)))
    if w_in is not None:
        inputs.append(w_in)
        in_specs.append(ANY)
    aliases = {}
    if proj_in is not None:
        aliases[len(inputs)] = 0
        inputs.append(proj_in)
        in_specs.append(ANY)
    out_specs = [pl.BlockSpec((tm, cols), lambda q, i, ch: (i, ch[first + q]))]
    out_shape = [jax.ShapeDtypeStruct((s, n_cols), F32)]
    if fused:
        assert n_shards == 1
        out_specs += [pl.BlockSpec((tm, d), row), pl.BlockSpec((d, tm), lambda q, i, ch: (0, i))]
        out_shape += [jax.ShapeDtypeStruct((s, d), BF16), jax.ShapeDtypeStruct((d, s), BF16)]
    for shard, kind in casts:
        assert n_shards == 1
        inputs.append(shard)
        if kind == 3:
            in_specs.append(pl.BlockSpec(shard.shape, lambda q, i, ch: (0, 0, 0)))
            out_specs.append(pl.BlockSpec(shard.shape, lambda q, i, ch: (0, ch[0], 0)))
            out_shape.append(jax.ShapeDtypeStruct((shard.shape[0], 4 * shard.shape[1], shard.shape[2]), BF16))
        else:
            block = (shard.shape[0] // n_steps, shard.shape[1])
            in_specs.append(pl.BlockSpec(block, row))
            out_specs.append(pl.BlockSpec(block, lambda q, i, ch: (ch[0] * n_steps + i, 0)))
            out_shape.append(jax.ShapeDtypeStruct((4 * shard.shape[0], shard.shape[1]), BF16))
    return _call(body, name, (n_shards, s // tm), in_specs, out_specs, out_shape,
                 [pltpu.VMEM((d, cols), BF16), pltpu.SemaphoreType.DMA(())], inputs, rider, prefetch=chips,
                 aliases=aliases, rider_refs=True)


def _kv_fwd(mem, g, w_kv):
    m, d = mem.shape

    def body(mem_ref, g_ref, w_ref, k_ref, v_ref):
        mv = mem_ref[...]
        r = lax.rsqrt(jnp.mean(mv * mv, axis=-1, keepdims=True) + EPS)
        mem_n = (mv * r * g_ref[...]).astype(BF16)
        kv = _dot(mem_n, w_ref[...], NN)
        k_ref[...] = kv[:, :WIDTH].astype(BF16)
        v_ref[...] = kv[:, WIDTH:].astype(BF16)

    return pl.pallas_call(
        body, name="kv_fwd",
        out_shape=(jax.ShapeDtypeStruct((m, WIDTH), BF16), jax.ShapeDtypeStruct((m, WIDTH), BF16)),
        compiler_params=_params(0),
    )(mem, g, w_kv)


def _kv_bwd(mem, g, w_kv, dk, dv):
    m, d = mem.shape
    n = w_kv.shape[1]
    col = 512

    def body(mem_ref, g_ref, w_ref, dk_ref, dv_ref, dw_ref, dg_ref):
        mv = mem_ref[...]
        r = lax.rsqrt(jnp.mean(mv * mv, axis=-1, keepdims=True) + EPS)
        mem_hat = mv * r
        mem_n = (mem_hat * g_ref[...]).astype(BF16)
        dkv = jnp.concatenate([dk_ref[...], dv_ref[...]], axis=1).astype(BF16)
        for j in range(n // col):
            dw_ref[:, j * col:(j + 1) * col] = _dot(mem_n, dkv[:, j * col:(j + 1) * col], TN).astype(BF16)
        dmem_n = _dot(dkv, w_ref[...], NT)
        dg_ref[...] = jnp.sum(dmem_n * mem_hat, axis=0, keepdims=True)

    return pl.pallas_call(
        body, name="kv_bwd",
        out_shape=(jax.ShapeDtypeStruct((d, n), BF16), jax.ShapeDtypeStruct((1, d), F32)),
        compiler_params=_params(0),
    )(mem, g, w_kv, dk, dv)


def _sigmoid(x):
    return 1.0 / (1.0 + jnp.exp(-x))


def _inv_counts(t0, t):
    pos = (t0 + lax.broadcasted_iota(jnp.int32, (t, 1), 0) + 1).astype(F32)
    return [1.0 / jnp.minimum(pos, float(w)) for w in POOL_WINDOWS]


def _window_sums(ext, t, backward):
    n = t + HALO
    parts = []
    for gi, w in enumerate(POOL_WINDOWS):
        s = ext[:, gi * GROUP:(gi + 1) * GROUP]
        k = 1
        while k < w:
            s = s + pltpu.roll(s, (n - k) if backward else k, axis=0)
            k *= 2
        parts.append(s[:t] if backward else s[HALO:])
    return parts


def _pool_fwd(xa, halo, inv, pool_w):
    t = xa.shape[0]
    sums = _window_sums(jnp.concatenate([halo, xa], axis=0), t, backward=False)
    d = jnp.concatenate([sums[gi] * inv[gi] - xa[:, gi * GROUP:(gi + 1) * GROUP] for gi in range(4)], axis=1)
    d = d.astype(BF16)
    y = jnp.concatenate([_dot(d[:, gi * GROUP:(gi + 1) * GROUP], pool_w[gi], NN) for gi in range(4)], axis=1)
    return d, y


def _layernorm_fwd(v):
    mu = jnp.mean(v, axis=-1, keepdims=True)
    xc = v - mu
    rstd = lax.rsqrt(jnp.mean(xc * xc, axis=-1, keepdims=True) + EPS)
    return xc * rstd, rstd


def _tril_mask(transposed):
    r = lax.broadcasted_iota(jnp.int32, (CHUNK, CHUNK), 0)
    c = lax.broadcasted_iota(jnp.int32, (CHUNK, CHUNK), 1)
    return (r <= c) if transposed else (r >= c)


def _sgu_mix(w_ref, vals, transposed):
    t = vals.shape[0]
    mask = _tril_mask(transposed)
    ws = [jnp.where(mask, w_ref[h], 0.0).astype(BF16) for h in range(N_SGU_HEADS)]
    rows = []
    for ci in range(t // CHUNK):
        blk = vals[ci * CHUNK:(ci + 1) * CHUNK]
        rows.append(jnp.concatenate(
            [_dot(ws[h], blk[:, h * CHUNK:(h + 1) * CHUNK], NN) for h in range(N_SGU_HEADS)], axis=1))
    return jnp.concatenate(rows, axis=0)


def _attn_fwd(q, k, v):
    ps, os_ = [], []
    for h in range(N_ATT_HEADS):
        sl = slice(h * ATT_DIM, (h + 1) * ATT_DIM)
        s = _dot(q[:, sl], k[:, sl], NT) * ATT_SCALE
        s = s - jnp.max(s, axis=-1, keepdims=True)
        e = jnp.exp(s)
        p = e * (1.0 / jnp.sum(e, axis=-1, keepdims=True))
        ps.append(p)
        os_.append(_dot(p.astype(BF16), v[:, sl], NN))
    return ps, jnp.concatenate(os_, axis=1)


def _rms_branch(y_pre):
    r = lax.rsqrt(jnp.mean(y_pre * y_pre, axis=-1, keepdims=True) + EPS)
    return y_pre * r, r


def _branch_specs(t, n_tiles, order):
    width_in = 7 * WIDTH
    tile = lambda i: order(i)
    per_halo = t // HALO
    const2 = lambda i: (0, 0)
    const3 = lambda i: (0, 0, 0)
    return [
        pl.BlockSpec((t, width_in), lambda i: (tile(i), 0)),
        pl.BlockSpec((HALO, WIDTH), lambda i: (jnp.maximum(tile(i) * per_halo - 1, 0), 0)),
        pl.BlockSpec((4, GROUP, GROUP), const3),
        pl.BlockSpec((1, WIDTH), const2),
        pl.BlockSpec((1, WIDTH), const2),
        pl.BlockSpec((1, WIDTH), const2),
        pl.BlockSpec((N_SGU_HEADS, CHUNK, CHUNK), const3),
        pl.BlockSpec((CHUNK, WIDTH), const2),
        pl.BlockSpec((MEM_ROWS, WIDTH), const2),
        pl.BlockSpec((MEM_ROWS, WIDTH), const2),
        pl.BlockSpec((1, 3 * WIDTH), const2),
    ]


MEM_ROWS = 256


def _branches_fwd(proj, pool_w, pool_scale, ln_g, ln_b, sgu_w, bias_full, k, v, branch_norm, t, rider=None):
    s = proj.shape[0]
    n_tiles = s // t

    def body(proj_ref, halo_ref, pw_ref, ps_ref, lg_ref, lb_ref, sw_ref, sb_ref, k_ref, v_ref, bn_ref, y_ref, yt_ref):
        i = pl.program_id(0)
        col = lambda j: proj_ref[:, j * WIDTH:(j + 1) * WIDTH]

        def put(branch, y_pre):
            sl = slice(branch * WIDTH, (branch + 1) * WIDTH)
            val = (_rms_branch(y_pre)[0] * bn[:, sl]).astype(BF16)
            y_ref[:, sl] = val
            yt_ref[sl, :] = val.T

        bn = bn_ref[...]
        halo = jnp.where(i > 0, halo_ref[...], 0.0)
        _, y_pool = _pool_fwd(col(0), halo, _inv_counts(i * t, t), pw_ref[...])
        ga = col(1)
        ya = y_pool * ps_ref[...] * (ga * _sigmoid(ga))
        put(0, ya)
        vhat, _ = _layernorm_fwd(col(3))
        vn = (vhat * lg_ref[...] + lb_ref[...]).astype(BF16)
        z = _sgu_mix(sw_ref, vn, transposed=False) + jnp.tile(sb_ref[...], (t // CHUNK, 1))
        gb = col(4)
        yb = col(2) * z * (gb * _sigmoid(gb))
        put(1, yb)
        _, o = _attn_fwd(col(5).astype(BF16), k_ref[...], v_ref[...])
        gc = col(6)
        yc = o * (gc * _sigmoid(gc))
        put(2, yc)

    return _call(body, "branches_fwd", (n_tiles,), _branch_specs(t, n_tiles, lambda i: i),
                 [pl.BlockSpec((t, 3 * WIDTH), lambda i: (i, 0)), pl.BlockSpec((3 * WIDTH, t), lambda i: (0, i))],
                 [jax.ShapeDtypeStruct((s, 3 * WIDTH), BF16), jax.ShapeDtypeStruct((3 * WIDTH, s), BF16)], [],
                 [proj, proj, pool_w, pool_scale, ln_g, ln_b, sgu_w, bias_full, k, v, branch_norm], rider)


def _branches_bwd(proj, dy, pool_w, pool_scale, ln_g, ln_b, sgu_w, sgu_wt, bias_full, k, v, branch_norm, t, rider=None):
    s = proj.shape[0]
    n_tiles = s // t
    n_chunks = t // CHUNK
    order = lambda i: n_tiles - 1 - i

    def body(proj_ref, halo_ref, pw_ref, ps_ref, lg_ref, lb_ref, sw_ref, sb_ref, k_ref, v_ref, bn_ref,
             swt_ref, dy_ref,
             dproj_ref, dpw_ref, dps_ref, dlg_ref, dlb_ref, dsw_ref, dsb_ref, dbn_ref, dk_ref, dv_ref,
             carry_ref, dbias_ref):
        step = pl.program_id(0)
        i = order(step)

        @pl.when(step == 0)
        def _():
            for ref in (dpw_ref, dps_ref, dlg_ref, dlb_ref, dsw_ref, dbn_ref, dk_ref, dv_ref, carry_ref, dbias_ref):
                ref[...] = jnp.zeros(ref.shape, ref.dtype)

        col = lambda j: proj_ref[:, j * WIDTH:(j + 1) * WIDTH]
        bn = bn_ref[...]

        def norm_bwd(y_pre, sl):
            yhat, r = _rms_branch(y_pre)
            dyv = dy_ref[:, sl].astype(F32)
            dbn_ref[:, sl] += jnp.sum(dyv * yhat, axis=0, keepdims=True)
            dyhat = dyv * bn[:, sl]
            return r * (dyhat - yhat * jnp.mean(dyhat * yhat, axis=-1, keepdims=True))

        def gate(gv):
            sg = _sigmoid(gv)
            return gv * sg, sg * (1.0 + gv * (1.0 - sg))

        inv = _inv_counts(i * t, t)
        halo = jnp.where(i > 0, halo_ref[...], 0.0)
        pw = pw_ref[...]
        d, y_pool = _pool_fwd(col(0), halo, inv, pw)
        scale = ps_ref[...]
        silu_a, dsilu_a = gate(col(1))
        pa = y_pool * scale
        dya = norm_bwd(pa * silu_a, slice(0, WIDTH))
        dproj_ref[:, WIDTH:2 * WIDTH] = (dya * pa * dsilu_a).astype(BF16)
        dpa = dya * silu_a
        dps_ref[...] += jnp.sum(dpa * y_pool, axis=0, keepdims=True)
        dy_pool = (dpa * scale).astype(BF16)
        dd_parts, ddc_parts = [], []
        for gi in range(4):
            sl = slice(gi * GROUP, (gi + 1) * GROUP)
            dpw_ref[gi] += _dot(d[:, sl], dy_pool[:, sl], TN)
            dd = _dot(dy_pool[:, sl], pw[gi], NT)
            dd_parts.append(dd)
            ddc_parts.append(dd * inv[gi])
        ddc = jnp.concatenate(ddc_parts, axis=1)
        sums = _window_sums(jnp.concatenate([ddc, carry_ref[...]], axis=0), t, backward=True)
        carry_ref[...] = ddc[:HALO]
        dproj_ref[:, 0:WIDTH] = jnp.concatenate([sums[gi] - dd_parts[gi] for gi in range(4)], axis=1).astype(BF16)

        vhat, rstd = _layernorm_fwd(col(3))
        lg = lg_ref[...]
        vn = (vhat * lg + lb_ref[...]).astype(BF16)
        z = _sgu_mix(sw_ref, vn, transposed=False) + jnp.tile(sb_ref[...], (n_chunks, 1))
        u = col(2)
        silu_b, dsilu_b = gate(col(4))
        uz = u * z
        dyb = norm_bwd(uz * silu_b, slice(WIDTH, 2 * WIDTH))
        dproj_ref[:, 4 * WIDTH:5 * WIDTH] = (dyb * uz * dsilu_b).astype(BF16)
        duz = dyb * silu_b
        dproj_ref[:, 2 * WIDTH:3 * WIDTH] = (duz * z).astype(BF16)
        dz = duz * u
        dz_b = dz.astype(BF16)
        for ci in range(n_chunks):
            rows = slice(ci * CHUNK, (ci + 1) * CHUNK)
            dbias_ref[...] += dz[rows]
            for h in range(N_SGU_HEADS):
                sl = slice(h * CHUNK, (h + 1) * CHUNK)
                dsw_ref[h] += _dot(dz_b[rows, sl], vn[rows, sl], NT)
        dvn = _sgu_mix(swt_ref, dz_b, transposed=True)
        dlg_ref[...] += jnp.sum(dvn * vhat, axis=0, keepdims=True)
        dlb_ref[...] += jnp.sum(dvn, axis=0, keepdims=True)
        dvhat = dvn * lg
        dvb = rstd * (dvhat - jnp.mean(dvhat, axis=-1, keepdims=True)
                      - vhat * jnp.mean(dvhat * vhat, axis=-1, keepdims=True))
        dproj_ref[:, 3 * WIDTH:4 * WIDTH] = dvb.astype(BF16)

        q = col(5).astype(BF16)
        kv_k, kv_v = k_ref[...], v_ref[...]
        ps, o = _attn_fwd(q, kv_k, kv_v)
        silu_c, dsilu_c = gate(col(6))
        dyc = norm_bwd(o * silu_c, slice(2 * WIDTH, 3 * WIDTH))
        dproj_ref[:, 6 * WIDTH:7 * WIDTH] = (dyc * o * dsilu_c).astype(BF16)
        do = (dyc * silu_c).astype(BF16)
        dq_parts = []
        for h in range(N_ATT_HEADS):
            sl = slice(h * ATT_DIM, (h + 1) * ATT_DIM)
            p = ps[h]
            dp = _dot(do[:, sl], kv_v[:, sl], NT)
            ds = (p * (dp - jnp.sum(p * dp, axis=-1, keepdims=True)) * ATT_SCALE).astype(BF16)
            dq_parts.append(_dot(ds, kv_k[:, sl], NN))
            dk_ref[:, sl] += _dot(ds, q[:, sl], TN)
            dv_ref[:, sl] += _dot(p.astype(BF16), do[:, sl], TN)
        dproj_ref[:, 5 * WIDTH:6 * WIDTH] = jnp.concatenate(dq_parts, axis=1).astype(BF16)

        @pl.when(step == n_tiles - 1)
        def _():
            keep = _tril_mask(transposed=False)
            for h in range(N_SGU_HEADS):
                dsw_ref[h] = jnp.where(keep, dsw_ref[h], 0.0)
            dsb_ref[...] = jnp.concatenate(
                [jnp.sum(dbias_ref[:, h * CHUNK:(h + 1) * CHUNK], axis=1, keepdims=True)
                 for h in range(N_SGU_HEADS)], axis=1)

    const2 = lambda i: (0, 0)
    const3 = lambda i: (0, 0, 0)
    out_shapes = (
        jax.ShapeDtypeStruct((s, 7 * WIDTH), BF16),
        jax.ShapeDtypeStruct((4, GROUP, GROUP), F32),
        jax.ShapeDtypeStruct((1, WIDTH), F32),
        jax.ShapeDtypeStruct((1, WIDTH), F32),
        jax.ShapeDtypeStruct((1, WIDTH), F32),
        jax.ShapeDtypeStruct((N_SGU_HEADS, CHUNK, CHUNK), F32),
        jax.ShapeDtypeStruct((CHUNK, N_SGU_HEADS), F32),
        jax.ShapeDtypeStruct((1, 3 * WIDTH), F32),
        jax.ShapeDtypeStruct((MEM_ROWS, WIDTH), F32),
        jax.ShapeDtypeStruct((MEM_ROWS, WIDTH), F32),
    )
    out_specs = (
        pl.BlockSpec((t, 7 * WIDTH), lambda i: (order(i), 0)),
        pl.BlockSpec((4, GROUP, GROUP), const3),
        pl.BlockSpec((1, WIDTH), const2),
        pl.BlockSpec((1, WIDTH), const2),
        pl.BlockSpec((1, WIDTH), const2),
        pl.BlockSpec((N_SGU_HEADS, CHUNK, CHUNK), const3),
        pl.BlockSpec((CHUNK, N_SGU_HEADS), const2),
        pl.BlockSpec((1, 3 * WIDTH), const2),
        pl.BlockSpec((MEM_ROWS, WIDTH), const2),
        pl.BlockSpec((MEM_ROWS, WIDTH), const2),
    )
    in_specs = _branch_specs(t, n_tiles, order) + [
        pl.BlockSpec((N_SGU_HEADS, CHUNK, CHUNK), const3),
        pl.BlockSpec((t, 3 * WIDTH), lambda i: (order(i), 0)),
    ]
    return _call(body, "branches_bwd", (n_tiles,), in_specs, out_specs, out_shapes,
                 [pltpu.VMEM((HALO, WIDTH), F32), pltpu.VMEM((CHUNK, WIDTH), F32)],
                 [proj, proj, pool_w, pool_scale, ln_g, ln_b, sgu_w, bias_full, k, v, branch_norm, sgu_wt, dy], rider)


def _out_loss(y, w_out, x, target, g_post, tm):
    s, d = x.shape
    e_w = y.shape[1]
    n_tiles = s // tm

    def body(y_ref, w_ref, x_ref, t_ref, g_ref, loss_ref, dz_ref, dout_ref, dy_ref, dg_ref, sq_ref):
        i = pl.program_id(0)

        @pl.when(i == 0)
        def _():
            sq_ref[...] = jnp.zeros(sq_ref.shape, F32)
            dg_ref[...] = jnp.zeros(dg_ref.shape, F32)

        w = w_ref[...]
        out = _dot(y_ref[...], w, NN)
        r = lax.rsqrt(jnp.mean(out * out, axis=-1, keepdims=True) + EPS)
        outn = out * r
        g = g_ref[...]
        err = (x_ref[...] + outn * g) - t_ref[...]
        sq_ref[...] += jnp.sum(err * err, axis=0, keepdims=True)
        dz = err * (1.0 / d)
        dz_ref[...] = dz
        dg_ref[...] += jnp.sum(dz * outn, axis=0, keepdims=True)
        doutn = dz * g
        dout = (r * (doutn - outn * jnp.mean(doutn * outn, axis=-1, keepdims=True))).astype(BF16)
        dout_ref[...] = dout
        dy_ref[...] = _dot(dout, w, NT).astype(BF16)

        @pl.when(i == n_tiles - 1)
        def _():
            loss_ref[...] = 0.5 * jnp.sum(sq_ref[...], axis=1, keepdims=True) * (1.0 / d)

    row = lambda i: (i, 0)
    const2 = lambda i: (0, 0)
    return pl.pallas_call(
        body, name="out_loss", grid=(n_tiles,),
        in_specs=[
            pl.BlockSpec((tm, e_w), row),
            pl.BlockSpec((e_w, d), const2, pipeline_mode=pl.Buffered(1)),
            pl.BlockSpec((tm, d), row),
            pl.BlockSpec((tm, d), row),
            pl.BlockSpec((1, d), const2),
        ],
        out_specs=(
            pl.BlockSpec((1, 1), const2),
            pl.BlockSpec((tm, d), row),
            pl.BlockSpec((tm, d), row),
            pl.BlockSpec((tm, e_w), row),
            pl.BlockSpec((1, d), const2),
        ),
        out_shape=(
            jax.ShapeDtypeStruct((1, 1), F32),
            jax.ShapeDtypeStruct((s, d), F32),
            jax.ShapeDtypeStruct((s, d), BF16),
            jax.ShapeDtypeStruct((s, e_w), BF16),
            jax.ShapeDtypeStruct((1, d), F32),
        ),
        scratch_shapes=[pltpu.VMEM((1, d), F32)],
        compiler_params=_params(1),
    )(y, w_out, x, target, g_post)


def _dx_call(dproj, w_in, x, dz, g_pre, tm, tk, rider=None):
    s, d = x.shape
    k_total = dproj.shape[1]
    nk = k_total // tk
    n_tiles = s // tm

    def body(dp_ref, w_ref, x_ref, dz_ref, g_ref, dx_ref, dg_ref, acc_ref):
        i, kk = pl.program_id(0), pl.program_id(1)
        part = lambda: _dot(dp_ref[...], w_ref[...], NT)

        @pl.when(kk == 0)
        def _():
            acc_ref[...] = part()

        @pl.when((kk > 0) & (kk < nk - 1))
        def _():
            acc_ref[...] += part()

        @pl.when((i == 0) & (kk == 0))
        def _():
            dg_ref[...] = jnp.zeros(dg_ref.shape, F32)

        @pl.when(kk == nk - 1)
        def _():
            dh = acc_ref[...] + part()
            xv = x_ref[...]
            r = lax.rsqrt(jnp.mean(xv * xv, axis=-1, keepdims=True) + EPS)
            xhat = xv * r
            dg_ref[...] += jnp.sum(dh * xhat, axis=0, keepdims=True)
            dxhat = dh * g_ref[...]
            dx_ref[...] = dz_ref[...] + r * (dxhat - xhat * jnp.mean(dxhat * xhat, axis=-1, keepdims=True))

    row = lambda i, kk: (i, 0)
    const2 = lambda i, kk: (0, 0)
    return _call(
        body, "dx", (n_tiles, nk),
        [
            pl.BlockSpec((tm, tk), lambda i, kk: (i, kk)),
            pl.BlockSpec((d, tk), lambda i, kk: (0, kk)),
            pl.BlockSpec((tm, d), row),
            pl.BlockSpec((tm, d), row),
            pl.BlockSpec((1, d), const2),
        ],
        [pl.BlockSpec((tm, d), row), pl.BlockSpec((1, d), const2)],
        [jax.ShapeDtypeStruct((s, d), F32), jax.ShapeDtypeStruct((1, d), F32)],
        [pltpu.VMEM((tm, d), F32)], [dproj, w_in, x, dz, g_pre], rider)


def _rows_tile(rows, cols, n_arrays, itemsize=4):
    budget = ELEMENTWISE_VMEM // (2 * n_arrays * cols * itemsize)
    if rows <= budget:
        return rows
    best = None
    for cand in range(16, rows + 1, 16):
        if rows % cand == 0 and cand <= max(budget, 16):
            best = cand
    return best if best is not None else rows


def _elementwise(fn, inputs, out_dtypes, name):
    rows, cols = inputs[0].shape
    tr = _rows_tile(rows, cols, len(inputs) + len(out_dtypes))
    n_in = len(inputs)

    def body(*refs):
        outs = fn(*[r[...] for r in refs[:n_in]])
        for o_ref, o in zip(refs[n_in:], outs):
            o_ref[...] = o.astype(o_ref.dtype)

    spec = pl.BlockSpec((tr, cols), lambda i: (i, 0))
    return pl.pallas_call(
        body, name=name, grid=(rows // tr,),
        in_specs=[spec] * n_in, out_specs=tuple([spec] * len(out_dtypes)),
        out_shape=tuple(jax.ShapeDtypeStruct((rows, cols), dt) for dt in out_dtypes),
        compiler_params=_params(1),
    )(*inputs)


def _blockwise(fn, pos, inputs, in_specs, out_shape, out_spec, grid, name):
    n_in = len(inputs)

    def body(pos_ref, *refs):
        o_ref = refs[n_in]
        (out,) = fn(*[r[...].reshape(o_ref.shape) for r in refs[:n_in]])
        o_ref[...] = out.astype(o_ref.dtype)

    return pl.pallas_call(
        body, name=name,
        grid_spec=pltpu.PrefetchScalarGridSpec(num_scalar_prefetch=1, grid=grid, in_specs=in_specs,
                                               out_specs=out_spec),
        out_shape=out_shape,
        compiler_params=_params(len(grid)),
    )(pos, *inputs)


def _cast_copy(x):
    return (x,)


def _pair_sum(mine, theirs):
    return ((mine.astype(F32) + theirs.astype(F32)),)


def _four_sum(own, t0, t1, t2):
    return ((((own.astype(F32) + t0.astype(F32)) + t1.astype(F32)) + t2.astype(F32)),)


def _adamw(w, g, m, v):
    m = ADAM_B1 * m + (1.0 - ADAM_B1) * g
    v = ADAM_B2 * v + (1.0 - ADAM_B2) * jnp.square(g)
    m_hat = m / (1.0 - ADAM_B1 ** ADAM_STEP)
    v_hat = v / (1.0 - ADAM_B2 ** ADAM_STEP)
    delta = -ADAM_LR * (m_hat / (jnp.sqrt(v_hat) + ADAM_EPS) + ADAM_WD * w)
    return delta, m, v


def _place():
    x, y, c = lax.axis_index("x"), lax.axis_index("y"), lax.axis_index("c")
    chips = [(1 - x, y), (x, 1 - y), (1 - x, 1 - y)]
    return x, y, c, chips


def _remote(src, dst, send_sem, recv_sem, to):
    return pltpu.make_async_remote_copy(src_ref=src, dst_ref=dst, send_sem=send_sem, recv_sem=recv_sem,
                                        device_id=to, device_id_type=MESH)


def _split(ref, plan):
    views = [ref]
    for axis, parts in plan:
        size = ref.shape[axis] // parts
        assert size * parts == ref.shape[axis]
        views = [v.at[tuple(pl.ds(q * size, size) if i == axis else slice(None) for i in range(len(ref.shape)))]
                 for v in views for q in range(parts)]
    return views


def _started(src, dst, send_sem, recv_sem, to):
    copy = _remote(src, dst, send_sem, recv_sem, to)
    copy.start()
    return copy


def _shard_half(kind, ref, chip, cc):
    if kind == 0:
        rows, cols = ref.shape[0] // 2, ref.shape[1] // 4
        return ref.at[pl.ds(cc * rows, rows), pl.ds(pl.multiple_of(chip * cols, LANES), cols)]
    if kind == 3:
        rows = ref.shape[1] // 8
        return ref.at[:, pl.ds(pl.multiple_of((2 * chip + cc) * rows, BF16_ROWS), rows), :]
    rows = ref.shape[0] // 8
    return ref.at[pl.ds(pl.multiple_of((2 * chip + cc) * rows, BF16_ROWS), rows), :]


def _relay_rider(full):
    kind = 0

    def quarter(ref, chip_no, cc, q):
        return _split(_shard_half(kind, ref, chip_no, cc), [(0, 2)])[q]

    def run(in_refs, full_refs, send_sems, recv_sems, start):
        (ref,) = full_refs
        x, y, c, chips = _place()
        sibling = (x, y, 1 - c)
        chip_no = [2 * ch[0] + ch[1] for ch in chips]
        if start:
            for p in (0, 1):
                held = quarter(ref, chip_no[1 - p], c, p)
                _remote(held, held, send_sems.at[p], recv_sems.at[p], (*chips[p], c)).start()
            return
        for p in (0, 1):
            landed = quarter(ref, chip_no[2], c, p)
            _remote(landed, landed, send_sems.at[p], recv_sems.at[p], (*chips[p], c)).wait_recv()
            _remote(landed, landed, send_sems.at[2], recv_sems.at[2], sibling).start()
        mine, theirs = _shard_half(kind, ref, chip_no[2], c), _shard_half(kind, ref, chip_no[2], 1 - c)
        _remote(mine, mine, send_sems.at[2], recv_sems.at[2], sibling).wait_send()
        _remote(theirs, theirs, send_sems.at[2], recv_sems.at[2], sibling).wait_recv()
        for p in (0, 1):
            held = quarter(ref, chip_no[1 - p], c, p)
            _remote(held, held, send_sems.at[p], recv_sems.at[p], (*chips[p], c)).wait_send()

    return _Rider([full], [jax.ShapeDtypeStruct(full.shape, full.dtype)], 3, run, aliases={0: 0})


def _riders(riders):
    def bounds(counts):
        ends = [sum(counts[:i + 1]) for i in range(len(counts))]
        return list(zip([0] + ends[:-1], ends))

    ins = bounds([len(r.inputs) for r in riders])
    outs = bounds([len(r.out_shapes) for r in riders])
    sems = bounds([r.n_sems for r in riders])

    class From:
        def __init__(self, sem_refs, base):
            self.sem_refs, self.base, self.at = sem_refs, base, self

        def __getitem__(self, k):
            return self.sem_refs.at[self.base + k]

    def run(in_refs, out_refs, send_sems, recv_sems, start):
        for r, (i0, i1), (o0, o1), (s0, _) in zip(riders, ins, outs, sems):
            r.run(in_refs[i0:i1], out_refs[o0:o1], From(send_sems, s0), From(recv_sems, s0), start)

    aliases = {}
    for r, (i0, _), (o0, _) in zip(riders, ins, outs):
        aliases.update({i0 + i: o0 + o for i, o in r.aliases.items()})
    return _Rider([a for r in riders for a in r.inputs], [o for r in riders for o in r.out_shapes],
                  sems[-1][1], run, aliases)


def _gather_rider(fulls, kinds, peers=(0, 1, 2)):
    n = len(fulls)
    full_half = _shard_half

    def run(in_refs, full_refs, send_sems, recv_sems, start):
        x, y, c, chips = _place()
        me = 2 * x + y
        sibling = (x, y, 1 - c)
        plans = [[(0, MAX_PARTS)], [(0, 2)], [(0, 2)], []]
        chips = [(p, chips[p]) for p in peers]
        across = lambda a, p, k: (3 * a + p) * MAX_PARTS + k
        onward = lambda a, p: 3 * n * MAX_PARTS + 3 * a + p

        def parts(a, chip_no, cc):
            return _split(full_half(kinds[a], full_refs[a], chip_no, cc), plans[kinds[a]])

        if start:
            for p, chip in chips:
                for a in range(n):
                    for k, mine in enumerate(parts(a, me, c)):
                        _remote(mine, mine, send_sems.at[across(a, p, k)], recv_sems.at[across(a, p, k)],
                                (*chip, c)).start()
            return
        for k in range(MAX_PARTS):
            for p, chip in chips:
                for a in range(n):
                    landed = parts(a, 2 * chip[0] + chip[1], c)
                    if k < len(landed):
                        _remote(landed[k], landed[k], send_sems.at[across(a, p, k)], recv_sems.at[across(a, p, k)],
                                (*chip, c)).wait_recv()
                        _remote(landed[k], landed[k], send_sems.at[onward(a, p)], recv_sems.at[onward(a, p)],
                                sibling).start()
        for p, chip in chips:
            them = 2 * chip[0] + chip[1]
            for a in range(n):
                passed = full_half(kinds[a], full_refs[a], them, 1 - c)
                _remote(passed, passed, send_sems.at[onward(a, p)], recv_sems.at[onward(a, p)], sibling).wait_recv()
                landed = full_half(kinds[a], full_refs[a], them, c)
                _remote(landed, landed, send_sems.at[onward(a, p)], recv_sems.at[onward(a, p)], sibling).wait_send()
                for k, mine in enumerate(parts(a, me, c)):
                    _remote(mine, mine, send_sems.at[across(a, p, k)], recv_sems.at[across(a, p, k)],
                            (*chip, c)).wait_send()

    return _Rider(fulls, [jax.ShapeDtypeStruct(f.shape, f.dtype) for f in fulls], 3 * n * (MAX_PARTS + 1), run,
                  aliases={a: a for a in range(n)})


def _exchange_rider(arrays, half_axes=None):
    n = len(arrays)
    half_axes = half_axes or [None] * n
    out_shapes = [jax.ShapeDtypeStruct(tuple(1 if i == ax else dim for i, dim in enumerate(g.shape)), g.dtype)
                  for g, ax in zip(arrays, half_axes)]

    def run(in_refs, out_refs, send_sems, recv_sems, start):
        x, y, c, _ = _place()
        sibling = (x, y, 1 - c)
        for a in range(n):
            src, ax = in_refs[a], half_axes[a]
            if ax is not None:
                src = src.at[tuple(pl.ds(1 - c, 1) if i == ax else slice(None) for i in range(len(src.shape)))]
            sems = (send_sems.at[a], recv_sems.at[a])
            if start:
                _started(src, out_refs[a], *sems, sibling)
            else:
                _remote(src, out_refs[a], *sems, sibling).wait()

    return _Rider(arrays, out_shapes, n, run)


def _scatter_rider(parts):
    n = len(parts)
    arrays = [p for p, _ in parts]

    def block_shape(p, ax):
        if ax == len(p.shape) - 1:
            return p.shape[:-1] + (p.shape[-1] // 4,)
        return tuple(1 if i == ax else dim for i, dim in enumerate(p.shape))

    out_shapes = [jax.ShapeDtypeStruct((3,) + block_shape(p, ax), p.dtype) for p, ax in parts]

    def block(ref, ax, chip):
        rank = len(ref.shape)
        if ax == rank - 1:
            cols = ref.shape[-1] // 4
            last = pl.ds(pl.multiple_of(chip * cols, LANES), cols)
            return ref.at[tuple([slice(None)] * (rank - 1) + [last])]
        return ref.at[tuple(pl.ds(chip, 1) if i == ax else slice(None) for i in range(rank))]

    def run(in_refs, out_refs, send_sems, recv_sems, start):
        x, y, c, chips = _place()
        for a in range(n):
            ax = parts[a][1]
            for p, chip in enumerate(chips):
                src, dst = block(in_refs[a], ax, 2 * chip[0] + chip[1]), out_refs[a].at[p]
                sems = (send_sems.at[3 * a + p], recv_sems.at[3 * a + p])
                if start:
                    _started(src, dst, *sems, (*chip, c))
                else:
                    _remote(src, dst, *sems, (*chip, c)).wait()

    return _Rider(arrays, out_shapes, 3 * n, run)


def _join_rider(joined):
    n = len(joined)
    arrays = [j for j, _ in joined]

    def run(in_refs, out_refs, send_sems, recv_sems, start):
        x, y, c, _ = _place()
        sibling = (x, y, 1 - c)

        def half(a, cc):
            rank = len(out_refs[a].shape)
            return out_refs[a].at[tuple(pl.ds(cc, 1) if i == joined[a][1] else slice(None) for i in range(rank))]

        for a in range(n):
            sems = (send_sems.at[a], recv_sems.at[a])
            if start:
                _started(half(a, c), half(a, c), *sems, sibling)
            else:
                _remote(half(a, c), half(a, c), *sems, sibling).wait_send()
                _remote(half(a, 1 - c), half(a, 1 - c), *sems, sibling).wait_recv()

    return _Rider(arrays, [jax.ShapeDtypeStruct(j.shape, j.dtype) for j in arrays], n, run,
                  aliases={a: a for a in range(n)})


def _allreduce_small(packed, rider):
    rows, lanes = packed.shape
    half = rows // 2
    r_in, r_out = len(rider.inputs), len(rider.out_shapes)

    def body(in_ref, *refs):
        rider_ins, out_ref, rider_outs = refs[:r_in], refs[r_in], refs[r_in + 1:r_in + 1 + r_out]
        pair_ref, gath_ref, send_sems, recv_sems, rider_send, rider_recv = refs[r_in + 1 + r_out:]
        rider.run(rider_ins, rider_outs, rider_send, rider_recv, True)
        x, y, c, chips = _place()
        me = 2 * x + y
        sibling = (x, y, 1 - c)
        mine = pl.ds(pl.multiple_of(c * half, 8), half)
        theirs = pl.ds(pl.multiple_of((1 - c) * half, 8), half)
        to_sib = _remote(in_ref.at[theirs], pair_ref, send_sems.at[0], recv_sems.at[0], sibling)
        to_sib.start()
        to_sib.wait()
        gath_ref[me] = in_ref[mine] + pair_ref[...]
        sends = [_remote(gath_ref.at[me], gath_ref.at[me], send_sems.at[1 + p], recv_sems.at[1 + p], (*chip, c))
                 for p, chip in enumerate(chips)]
        for cp in sends:
            cp.start()
        for p, chip in enumerate(chips):
            slot = gath_ref.at[2 * chip[0] + chip[1]]
            _remote(slot, slot, send_sems.at[1 + p], recv_sems.at[1 + p], (*chip, c)).wait_recv()
        for cp in sends:
            cp.wait_send()
        out_ref[mine] = ((gath_ref[0] + gath_ref[1]) + gath_ref[2]) + gath_ref[3]
        back = _remote(out_ref.at[mine], out_ref.at[mine], send_sems.at[4], recv_sems.at[4], sibling)
        back.start()
        back.wait_send()
        _remote(out_ref.at[theirs], out_ref.at[theirs], send_sems.at[4], recv_sems.at[4], sibling).wait_recv()
        rider.run(rider_ins, rider_outs, rider_send, rider_recv, False)

    vmem = pl.BlockSpec(memory_space=pltpu.VMEM)
    return pl.pallas_call(
        body, name="allreduce_small",
        in_specs=[vmem] + [ANY] * r_in, out_specs=(vmem,) + (ANY,) * r_out,
        out_shape=(jax.ShapeDtypeStruct((rows, lanes), F32),) + tuple(rider.out_shapes),
        scratch_shapes=[pltpu.VMEM((half, lanes), F32), pltpu.VMEM((4, half, lanes), F32),
                        pltpu.SemaphoreType.DMA((5,)), pltpu.SemaphoreType.DMA((5,)),
                        pltpu.SemaphoreType.DMA((rider.n_sems,)), pltpu.SemaphoreType.DMA((rider.n_sems,))],
        input_output_aliases={1 + i: 1 + o for i, o in rider.aliases.items()},
        compiler_params=pltpu.CompilerParams(has_side_effects=True, vmem_limit_bytes=32 * 1024 * 1024),
    )(packed, *rider.inputs)


SMALL = ("norm_pre", "pool_scale", "sgu_ln_g", "sgu_ln_b", "sgu_w", "sgu_b", "mem_norm", "branch_norm", "norm_post")
LARGE = ("w_in", "pool_w", "w_kv", "w_out")
ORDER = ("norm_pre", "w_in", "pool_w", "pool_scale", "sgu_ln_g", "sgu_ln_b", "sgu_w", "sgu_b", "mem_norm", "w_kv",
         "branch_norm", "w_out", "norm_post")


def _pack(arrays, extra=()):
    rows = [a.reshape(-1, 128) for a in arrays] + list(extra)
    pad = -sum(r.shape[0] for r in rows) % 16
    return jnp.concatenate(rows + ([jnp.zeros((pad, 128), F32)] if pad else []), axis=0)


def _unpack(packed, like):
    out, row = [], 0
    for a in like:
        rows = a.size // 128
        out.append(packed[row:row + rows].reshape(a.shape))
        row += rows
    return out


def kernel(x, mem, norm_pre, w_in, pool_w, pool_scale, sgu_ln_g, sgu_ln_b, sgu_w, sgu_b, mem_norm, w_kv, branch_norm, w_out, norm_post, loss_target, m_norm_pre, m_w_in, m_pool_w, m_pool_scale, m_sgu_ln_g, m_sgu_ln_b, m_sgu_w, m_sgu_b, m_mem_norm, m_w_kv, m_branch_norm, m_w_out, m_norm_post, v_norm_pre, v_w_in, v_pool_w, v_pool_scale, v_sgu_ln_g, v_sgu_ln_b, v_sgu_w, v_sgu_b, v_mem_norm, v_w_kv, v_branch_norm, v_w_out, v_norm_post):
    weights = dict(norm_pre=norm_pre, w_in=w_in, pool_w=pool_w, pool_scale=pool_scale, sgu_ln_g=sgu_ln_g,
                   sgu_ln_b=sgu_ln_b, sgu_w=sgu_w, sgu_b=sgu_b, mem_norm=mem_norm, w_kv=w_kv, branch_norm=branch_norm,
                   w_out=w_out, norm_post=norm_post)
    mom1 = dict(norm_pre=m_norm_pre, w_in=m_w_in, pool_w=m_pool_w, pool_scale=m_pool_scale, sgu_ln_g=m_sgu_ln_g,
                sgu_ln_b=m_sgu_ln_b, sgu_w=m_sgu_w, sgu_b=m_sgu_b, mem_norm=m_mem_norm, w_kv=m_w_kv,
                branch_norm=m_branch_norm, w_out=m_w_out, norm_post=m_norm_post)
    mom2 = dict(norm_pre=v_norm_pre, w_in=v_w_in, pool_w=v_pool_w, pool_scale=v_pool_scale, sgu_ln_g=v_sgu_ln_g,
                sgu_ln_b=v_sgu_ln_b, sgu_w=v_sgu_w, sgu_b=v_sgu_b, mem_norm=v_mem_norm, w_kv=v_w_kv,
                branch_norm=v_branch_norm, w_out=v_w_out, norm_post=v_norm_post)

    s, d = x.shape[1], x.shape[2]
    x2, mem2, tgt2 = x[0], mem[0], loss_target[0]
    t_branch = min(256, s)
    tm = min(512, s)

    core = lax.axis_index("c")
    chip = 2 * lax.axis_index("x") + lax.axis_index("y")
    pos = jnp.stack([core, chip]).astype(jnp.int32)
    n_in, n_kv, n_out = 4 * w_in.shape[2], 4 * w_kv.shape[1], 4 * w_out.shape[1]
    wi_rows, kv_rows, wo_rows = d // 8, n_kv // 8, n_out // 8

    kv_cols, pw_rows = w_kv.shape[2], GROUP // 8
    wi_own = _blockwise(_cast_copy, pos, [w_in[0]], [pl.BlockSpec((wi_rows, n_in // 4), lambda i, p: (i, 0))],
                        jax.ShapeDtypeStruct((d, n_in), BF16),
                        pl.BlockSpec((wi_rows, n_in // 4), lambda i, p: (i, p[1])), (8,), "place_w_in")

    x_pos, y_pos = lax.axis_index("x"), lax.axis_index("y")
    chips = jnp.stack([chip, 2 * (1 - x_pos) + y_pos, 2 * x_pos + 1 - y_pos,
                       2 * (1 - x_pos) + 1 - y_pos]).astype(jnp.int32)
    mem_g = mem_norm.reshape(1, d)
    proj, h, h_t, wkv_own, wo_own, pw_own, wi_full = _proj_piece(
        chips, 0, 1, x2, norm_pre, None, None, n_in, _gather_rider([wi_own], [0], peers=(0, 1)), tm, "proj_own",
        casts=[(w_kv[0], 1), (w_out[0], 2), (pool_w[0], 3)])
    proj, wi_full, wkv_full, pw_full = _proj_piece(
        chips, 1, 2, h, None, None, proj, n_in,
        _riders([_relay_rider(wi_full), _gather_rider([wkv_own, pw_own], [1, 3])]), tm, "proj_neighbours")
    proj, wo_part = _proj_piece(chips, 3, 1, h, None, wi_full, proj, n_in,
                                _gather_rider([wo_own], [2], peers=(0, 1)), tm, "proj_diagonal")
    k_m, v_m = _kv_fwd(mem2, mem_g, wkv_full)
    bias_full = jnp.repeat(sgu_b[0].T, CHUNK, axis=1)
    y, y_t, wo_full = _branches_fwd(proj, pw_full, pool_scale, sgu_ln_g, sgu_ln_b, sgu_w[0], bias_full, k_m, v_m,
                                    branch_norm, t_branch, _gather_rider([wo_part], [2], peers=(2,)))
    loss_local, dz, dout, dy, g_norm_post = _out_loss(y, wo_full, x2, tgt2, norm_post, min(256, s))

    tk = min(1024, s)
    (dproj, g_pw, g_pool_scale, g_ln_g, g_ln_b, g_sgu_w, g_sgu_b_t, g_branch_norm, dk, dv) = _branches_bwd(
        proj, dy, pw_full, pool_scale, sgu_ln_g, sgu_ln_b, sgu_w[0], jnp.swapaxes(sgu_w[0], 1, 2), bias_full,
        k_m, v_m, branch_norm, t_branch)
    g_wkv, g_mem_norm = _kv_bwd(mem2, mem_g, wkv_full, dk, dv)
    g_wkv = g_wkv.reshape(4, 2, kv_rows, kv_cols)
    g_pw = g_pw.astype(BF16).reshape(4, 4, 2, pw_rows, GROUP)
    g_wo, gkv_from_sibling, gpw_from_sibling = _grad_rows(y_t, dout, pos, lambda i, p: i, n_out, n_out // 2, 1024, tk,
                                                          "grad_w_out", _exchange_rider([g_wkv, g_pw], [1, 2]))
    g_wo = g_wo.reshape(4, 2, wo_rows, d)
    ps_kv = _blockwise(_pair_sum, pos, [g_wkv, gkv_from_sibling],
                       [pl.BlockSpec((1, 1, kv_rows, kv_cols), lambda i, p: (i, p[0], 0, 0)),
                        pl.BlockSpec((1, 1, kv_rows, kv_cols), lambda i, p: (i, 0, 0, 0))],
                       jax.ShapeDtypeStruct((4, kv_rows, kv_cols), BF16),
                       pl.BlockSpec((1, kv_rows, kv_cols), lambda i, p: (i, 0, 0)), (4,), "pair_sum_w_kv")
    ps_pw = _blockwise(_pair_sum, pos, [g_pw, gpw_from_sibling],
                       [pl.BlockSpec((1, 4, 1, pw_rows, GROUP), lambda i, p: (i, 0, p[0], 0, 0)),
                        pl.BlockSpec((1, 4, 1, pw_rows, GROUP), lambda i, p: (i, 0, 0, 0, 0))],
                       jax.ShapeDtypeStruct((4, 4, pw_rows, GROUP), BF16),
                       pl.BlockSpec((1, 4, pw_rows, GROUP), lambda i, p: (i, 0, 0, 0)), (4,), "pair_sum_pool_w")
    gwi_theirs, landed_kv, landed_pw, gwo_from_sibling = _grad_rows(
        h_t, dproj, pos, lambda i, p: 1 - p[0], d // 2, d // 2, n_in // 4, tk, "grad_w_in_sibling_half",
        _riders([_scatter_rider([(ps_kv, 0), (ps_pw, 1)]), _exchange_rider([g_wo], [1])]))
    ps_wo = _blockwise(_pair_sum, pos, [g_wo, gwo_from_sibling],
                       [pl.BlockSpec((1, 1, wo_rows, d), lambda i, p: (i, p[0], 0, 0)),
                        pl.BlockSpec((1, 1, wo_rows, d), lambda i, p: (i, 0, 0, 0))],
                       jax.ShapeDtypeStruct((4, wo_rows, d), BF16),
                       pl.BlockSpec((1, wo_rows, d), lambda i, p: (i, 0, 0)), (4,), "pair_sum_w_out")
    gwi_mine, gwi_from_sibling, landed_wo = _grad_rows(
        h_t, dproj, pos, lambda i, p: p[0], d // 2, d // 2, n_in // 4, tk, "grad_w_in_own_half",
        _riders([_exchange_rider([gwi_theirs]), _scatter_rider([(ps_wo, 0)])]))
    ps_wi = _elementwise(_pair_sum, [gwi_mine, gwi_from_sibling], [BF16], "pair_sum_w_in")[0]
    grad_x, g_norm_pre, landed_wi = _dx_call(dproj, wi_full, x2, dz, norm_pre, tm, 1024,
                                             _scatter_rider([(ps_wi, 1)]))
    psum = [ps_wi, ps_kv, ps_wo, ps_pw]
    landed = [landed_wi, landed_kv, landed_wo, landed_pw]
    from_chip = lambda spec_shape, rank: [
        pl.BlockSpec(spec_shape, functools.partial(lambda i, p, q: (q, i) + (0,) * (rank - 2), q=q))
        for q in range(3)]
    join_rider = _join_rider([
        (_blockwise(_four_sum, pos, [psum[0]] + [landed[0]] * 3,
                    [pl.BlockSpec((256, n_in // 4), lambda i, p: (i, p[1]))] + from_chip((1, 256, n_in // 4), 3),
                    jax.ShapeDtypeStruct((2, d // 2, n_in // 4), F32),
                    pl.BlockSpec((1, 256, n_in // 4), lambda i, p: (p[0], i, 0)), (d // 2 // 256,), "chip_sum_w_in"),
         0),
        (_blockwise(_four_sum, pos, [psum[1]] + [landed[1]] * 3,
                    [pl.BlockSpec((1, kv_rows, kv_cols), lambda i, p: (p[1], 0, 0))]
                    + from_chip((1, 1, kv_rows, kv_cols), 4),
                    jax.ShapeDtypeStruct((2, kv_rows, kv_cols), F32),
                    pl.BlockSpec((1, kv_rows, kv_cols), lambda i, p: (p[0], 0, 0)), (1,), "chip_sum_w_kv"),
         0),
        (_blockwise(_four_sum, pos, [psum[2]] + [landed[2]] * 3,
                    [pl.BlockSpec((1, wo_rows, d), lambda i, p: (p[1], 0, 0))] + from_chip((1, 1, wo_rows, d), 4),
                    jax.ShapeDtypeStruct((2, wo_rows, d), F32),
                    pl.BlockSpec((1, wo_rows, d), lambda i, p: (p[0], 0, 0)), (1,), "chip_sum_w_out"),
         0),
        (_blockwise(_four_sum, pos, [psum[3]] + [landed[3]] * 3,
                    [pl.BlockSpec((4, 1, pw_rows, GROUP), lambda i, p: (0, p[1], 0, 0))]
                    + from_chip((1, 4, 1, pw_rows, GROUP), 5),
                    jax.ShapeDtypeStruct((4, 2, pw_rows, GROUP), F32),
                    pl.BlockSpec((4, 1, pw_rows, GROUP), lambda i, p: (0, p[0], 0, 0)), (1,), "chip_sum_pool_w"),
         1),
    ])

    small_local = dict(norm_pre=g_norm_pre, pool_scale=g_pool_scale, sgu_ln_g=g_ln_g, sgu_ln_b=g_ln_b,
                       sgu_w=g_sgu_w, sgu_b=g_sgu_b_t.T, mem_norm=g_mem_norm, branch_norm=g_branch_norm,
                       norm_post=g_norm_post)
    small_rows = sum(weights[n].size for n in SMALL) // 128
    small_sum, *joined = _allreduce_small(
        _pack([small_local[n] for n in SMALL], [jnp.pad(loss_local, ((0, 7), (0, 127)))]), join_rider)
    grads = {"w_in": joined[0].reshape(w_in.shape), "w_kv": joined[1].reshape(w_kv.shape),
             "w_out": joined[2].reshape(w_out.shape), "pool_w": joined[3].reshape(pool_w.shape)}
    for n, g in zip(SMALL, _unpack(small_sum, [weights[n] for n in SMALL])):
        grads[n] = g
    loss = small_sum[small_rows, 0]

    delta, new_m, new_v = {}, {}, {}
    packed = [small_sum if src is grads else _pack([src[n] for n in SMALL]) for src in (weights, grads, mom1, mom2)]
    outs = _elementwise(_adamw, packed, [F32, F32, F32], "adamw_small")
    for dst, o in zip((delta, new_m, new_v), outs):
        for n, a in zip(SMALL, _unpack(o, [weights[n] for n in SMALL])):
            dst[n] = a
    for n in LARGE:
        cols = weights[n].shape[-1]
        outs = _elementwise(lambda w, g, m, v: _adamw(w, g, m, v) + (g,),
                            [src[n].reshape(-1, cols) for src in (weights, grads, mom1, mom2)],
                            [F32, F32, F32, F32], "adamw_" + n)
        for dst, o in zip((delta, new_m, new_v, grads), outs):
            dst[n] = o.reshape(weights[n].shape)

    return (loss, grad_x[None], *[grads[n] for n in ORDER], *[delta[n] for n in ORDER],
            *[new_m[n] for n in ORDER], *[new_v[n] for n in ORDER])
```

```python
import functools

import jax
import jax.numpy as jnp
from jax import lax
from jax.experimental import pallas as pl
from jax.experimental.pallas import tpu as pltpu

F32 = jnp.float32
BF16 = jnp.bfloat16
EPS = 1e-6
MESH = pl.DeviceIdType.MESH
ANY = pl.BlockSpec(memory_space=pl.ANY)

POOL_WINDOWS = (2, 4, 8, 16)
GROUP = 256
HALO = 16
CHUNK = 128
N_SGU_HEADS = 8
N_ATT_HEADS = 4
ATT_DIM = 256
WIDTH = 1024
ATT_SCALE = 1.0 / 16.0

ADAM_LR = 0.001
ADAM_B1 = 0.9
ADAM_B2 = 0.999
ADAM_EPS = 1e-08
ADAM_WD = 0.01
ADAM_STEP = 10

VMEM_LIMIT = 60 * 1024 * 1024
ELEMENTWISE_VMEM = 24 * 1024 * 1024
LANES = 128
BF16_ROWS = 16
MAX_PARTS = 4


def _params(n_grid_axes, vmem=VMEM_LIMIT):
    return pltpu.CompilerParams(dimension_semantics=("arbitrary",) * n_grid_axes, vmem_limit_bytes=vmem)


def _dot(a, b, dims):
    return lax.dot_general(a, b, (dims, ((), ())), preferred_element_type=F32)


NN = ((1,), (0,))
NT = ((1,), (1,))
TN = ((0,), (0,))


class _Rider:
    def __init__(self, inputs, out_shapes, n_sems, run, aliases=None):
        self.inputs, self.out_shapes, self.n_sems, self.run = list(inputs), list(out_shapes), n_sems, run
        self.aliases = aliases or {}


def _call(body, name, grid, in_specs, out_specs, out_shape, scratch_shapes, inputs, rider=None, prefetch=None,
          aliases=None, rider_refs=False):
    n_in, n_out, n_scr = len(in_specs), len(out_specs), len(scratch_shapes)
    r_in = len(rider.inputs) if rider else 0
    r_out = len(rider.out_shapes) if rider else 0
    n_pre = 0 if prefetch is None else 1

    def whole_body(*refs):
        pre, refs = refs[:n_pre], refs[n_pre:]
        ins, rider_ins = refs[:n_in], refs[n_in:n_in + r_in]
        refs = refs[n_in + r_in:]
        outs, rider_outs = refs[:n_out], refs[n_out:n_out + r_out]
        refs = refs[n_out + r_out:]
        scratch, sems = refs[:n_scr], refs[n_scr:]
        extra = {"rider_outs": rider_outs} if rider_refs else {}
        if rider is None:
            body(*pre, *ins, *outs, *scratch, **extra)
            return
        ids = [pl.program_id(ax) for ax in range(len(grid))]
        first = functools.reduce(lambda p, q: p & q, [i == 0 for i in ids])
        last = functools.reduce(lambda p, q: p & q, [i == g - 1 for i, g in zip(ids, grid)])

        @pl.when(first)
        def _():
            rider.run(rider_ins, rider_outs, *sems, True)

        body(*pre, *ins, *outs, *scratch, **extra)

        @pl.when(last)
        def _():
            rider.run(rider_ins, rider_outs, *sems, False)

    io_aliases = {n_pre + i: o for i, o in (aliases or {}).items()}
    scratch_all = list(scratch_shapes)
    if rider:
        io_aliases.update({n_pre + n_in + i: n_out + o for i, o in rider.aliases.items()})
        scratch_all += [pltpu.SemaphoreType.DMA((rider.n_sems,)), pltpu.SemaphoreType.DMA((rider.n_sems,))]
    specs = dict(grid=grid, in_specs=list(in_specs) + [ANY] * r_in, out_specs=tuple(out_specs) + (ANY,) * r_out,
                 scratch_shapes=scratch_all)
    if n_pre:
        specs = dict(grid_spec=pltpu.PrefetchScalarGridSpec(num_scalar_prefetch=1, **specs))
    outs = pl.pallas_call(
        whole_body, name=name, **specs,
        out_shape=tuple(out_shape) + tuple(rider.out_shapes if rider else ()),
        input_output_aliases=io_aliases, compiler_params=_params(len(grid)),
    )(*([prefetch] if n_pre else []), *inputs, *(rider.inputs if rider else []))
    return tuple(outs)


def _grad_rows(a_t, b, pos, row_of, m, tm, tn, tk, name, rider=None):
    k, n = a_t.shape[1], b.shape[1]
    nk = k // tk
    out_dtype, dims, a = BF16, NN, a_t
    a_spec = pl.BlockSpec((tm, tk), lambda i, j, kk, p: (row_of(i, p), kk))
    b_spec = pl.BlockSpec((tk, tn), lambda i, j, kk, p: (kk, j))

    def body(pos_ref, a_ref, b_ref, o_ref, *acc):
        part = lambda: _dot(a_ref[...], b_ref[...], dims)
        if nk == 1:
            o_ref[...] = part().astype(out_dtype)
            return
        (acc_ref,) = acc
        kk = pl.program_id(2)

        @pl.when(kk == 0)
        def _():
            acc_ref[...] = part()

        @pl.when((kk > 0) & (kk < nk - 1))
        def _():
            acc_ref[...] += part()

        @pl.when(kk == nk - 1)
        def _():
            o_ref[...] = (acc_ref[...] + part()).astype(out_dtype)

    return _call(body, name, (m // tm, n // tn, nk), [a_spec, b_spec],
                 [pl.BlockSpec((tm, tn), lambda i, j, kk, p: (i, j))], [jax.ShapeDtypeStruct((m, n), out_dtype)],
                 [pltpu.VMEM((tm, tn), F32)] if nk > 1 else [], [a, b], rider, prefetch=pos)


def _proj_piece(chips, first, n_shards, src, g_pre, w_in, proj_in, n_cols, rider, tm, name, casts=(), memory=()):
    s, d = src.shape
    cols = n_cols // 4
    fused = g_pre is not None
    n_steps = s // tm

    def body(chips_ref, *refs, rider_outs=()):
        refs = list(refs)
        src_ref = refs.pop(0)
        g_ref = refs.pop(0) if fused else None
        w_ref = refs.pop(0) if w_in is not None else rider_outs[0]
        if proj_in is not None:
            refs.pop(0)
        shard_refs = [refs.pop(0) for _ in casts]
        memory_refs = [refs.pop(0) for _ in memory]
        proj_ref = refs.pop(0)
        h_ref, ht_ref = (refs.pop(0), refs.pop(0)) if fused else (None, None)
        for shard_ref in shard_refs:
            refs.pop(0)[...] = shard_ref[...].astype(BF16)
        kv_refs = [refs.pop(0) for _ in memory[:2]]
        wbuf, sem = refs
        q, i = pl.program_id(0), pl.program_id(1)

        @pl.when(i == 0)
        def _():
            at = pl.multiple_of(chips_ref[first + q] * cols, LANES)
            cp = pltpu.make_async_copy(w_ref.at[:, pl.ds(at, cols)], wbuf, sem)
            cp.start()
            cp.wait()

        if memory:
            @pl.when((q == 0) & (i == 0))
            def _():
                _kv_body(*memory_refs, *kv_refs)

        if fused:
            xv = src_ref[...]
            r = lax.rsqrt(jnp.mean(xv * xv, axis=-1, keepdims=True) + EPS)
            h = (xv * r * g_ref[...]).astype(BF16)
            h_ref[...] = h
            ht_ref[...] = h.T
        else:
            h = src_ref[...]
        proj_ref[...] = _dot(h, wbuf[...], NN)

    row = lambda q, i, ch: (i, 0)
    inputs, in_specs = [src], [pl.BlockSpec((tm, d), row)]
    if fused:
        inputs.append(g_pre)
        in_specs.append(pl.BlockSpec((1, d), lambda q, i, ch: (0, 0)))
    if w_in is not None:
        inputs.append(w_in)
        in_specs.append(ANY)
    aliases = {}
    if proj_in is not None:
        aliases[len(inputs)] = 0
        inputs.append(proj_in)
        in_specs.append(ANY)
    out_specs = [pl.BlockSpec((tm, cols), lambda q, i, ch: (i, ch[first + q]))]
    out_shape = [jax.ShapeDtypeStruct((s, n_cols), F32)]
    if fused:
        assert n_shards == 1
        out_specs += [pl.BlockSpec((tm, d), row), pl.BlockSpec((d, tm), lambda q, i, ch: (0, i))]
        out_shape += [jax.ShapeDtypeStruct((s, d), BF16), jax.ShapeDtypeStruct((d, s), BF16)]
    for shard, kind in casts:
        assert n_shards == 1
        inputs.append(shard)
        if kind == 3:
            in_specs.append(pl.BlockSpec(shard.shape, lambda q, i, ch: (0, 0, 0)))
            out_specs.append(pl.BlockSpec(shard.shape, lambda q, i, ch: (0, ch[0], 0)))
            out_shape.append(jax.ShapeDtypeStruct((shard.shape[0], 4 * shard.shape[1], shard.shape[2]), BF16))
        else:
            block = (shard.shape[0] // n_steps, shard.shape[1])
            in_specs.append(pl.BlockSpec(block, row))
            out_specs.append(pl.BlockSpec(block, lambda q, i, ch: (ch[0] * n_steps + i, 0)))
            out_shape.append(jax.ShapeDtypeStruct((4 * shard.shape[0], shard.shape[1]), BF16))
    if memory:
        mem = memory[0]
        whole = lambda q, i, ch: (0, 0)
        inputs += list(memory)
        in_specs += [pl.BlockSpec(mem.shape, whole), pl.BlockSpec(memory[1].shape, whole),
                     pl.BlockSpec(memory[2].shape, whole, pipeline_mode=pl.Buffered(1))]
        out_specs += [pl.BlockSpec((mem.shape[0], WIDTH), whole)] * 2
        out_shape += [jax.ShapeDtypeStruct((mem.shape[0], WIDTH), BF16)] * 2
    return _call(body, name, (n_shards, s // tm), in_specs, out_specs, out_shape,
                 [pltpu.VMEM((d, cols), BF16), pltpu.SemaphoreType.DMA(())], inputs, rider, prefetch=chips,
                 aliases=aliases, rider_refs=True)


def _kv_body(mem_ref, g_ref, w_ref, k_ref, v_ref):
    mv = mem_ref[...]
    r = lax.rsqrt(jnp.mean(mv * mv, axis=-1, keepdims=True) + EPS)
    mem_n = (mv * r * g_ref[...]).astype(BF16)
    kv = _dot(mem_n, w_ref[...], NN)
    k_ref[...] = kv[:, :WIDTH].astype(BF16)
    v_ref[...] = kv[:, WIDTH:].astype(BF16)


def _kv_bwd(mem, g, w_kv, dk, dv):
    m, d = mem.shape
    n = w_kv.shape[1]
    col = 512

    def body(mem_ref, g_ref, w_ref, dk_ref, dv_ref, dw_ref, dg_ref):
        mv = mem_ref[...]
        r = lax.rsqrt(jnp.mean(mv * mv, axis=-1, keepdims=True) + EPS)
        mem_hat = mv * r
        mem_n = (mem_hat * g_ref[...]).astype(BF16)
        dkv = jnp.concatenate([dk_ref[...], dv_ref[...]], axis=1).astype(BF16)
        for j in range(n // col):
            dw_ref[:, j * col:(j + 1) * col] = _dot(mem_n, dkv[:, j * col:(j + 1) * col], TN).astype(BF16)
        dmem_n = _dot(dkv, w_ref[...], NT)
        dg_ref[...] = jnp.sum(dmem_n * mem_hat, axis=0, keepdims=True)

    return pl.pallas_call(
        body, name="kv_bwd",
        out_shape=(jax.ShapeDtypeStruct((d, n), BF16), jax.ShapeDtypeStruct((1, d), F32)),
        compiler_params=_params(0),
    )(mem, g, w_kv, dk, dv)


def _sigmoid(x):
    return 1.0 / (1.0 + jnp.exp(-x))


def _inv_counts(t0, t):
    pos = (t0 + lax.broadcasted_iota(jnp.int32, (t, 1), 0) + 1).astype(F32)
    return [1.0 / jnp.minimum(pos, float(w)) for w in POOL_WINDOWS]


def _window_sums(ext, t, backward):
    n = t + HALO
    parts = []
    for gi, w in enumerate(POOL_WINDOWS):
        s = ext[:, gi * GROUP:(gi + 1) * GROUP]
        k = 1
        while k < w:
            s = s + pltpu.roll(s, (n - k) if backward else k, axis=0)
            k *= 2
        parts.append(s[:t] if backward else s[HALO:])
    return parts


def _pool_fwd(xa, halo, inv, pool_w):
    t = xa.shape[0]
    sums = _window_sums(jnp.concatenate([halo, xa], axis=0), t, backward=False)
    d = jnp.concatenate([sums[gi] * inv[gi] - xa[:, gi * GROUP:(gi + 1) * GROUP] for gi in range(4)], axis=1)
    d = d.astype(BF16)
    y = jnp.concatenate([_dot(d[:, gi * GROUP:(gi + 1) * GROUP], pool_w[gi], NN) for gi in range(4)], axis=1)
    return d, y


def _layernorm_fwd(v):
    mu = jnp.mean(v, axis=-1, keepdims=True)
    xc = v - mu
    rstd = lax.rsqrt(jnp.mean(xc * xc, axis=-1, keepdims=True) + EPS)
    return xc * rstd, rstd


def _tril_mask(transposed):
    r = lax.broadcasted_iota(jnp.int32, (CHUNK, CHUNK), 0)
    c = lax.broadcasted_iota(jnp.int32, (CHUNK, CHUNK), 1)
    return (r <= c) if transposed else (r >= c)


def _sgu_mix(w_ref, vals, transposed):
    t = vals.shape[0]
    mask = _tril_mask(transposed)
    ws = [jnp.where(mask, w_ref[h], 0.0).astype(BF16) for h in range(N_SGU_HEADS)]
    rows = []
    for ci in range(t // CHUNK):
        blk = vals[ci * CHUNK:(ci + 1) * CHUNK]
        rows.append(jnp.concatenate(
            [_dot(ws[h], blk[:, h * CHUNK:(h + 1) * CHUNK], NN) for h in range(N_SGU_HEADS)], axis=1))
    return jnp.concatenate(rows, axis=0)


def _attn_fwd(q, k, v):
    ps, os_ = [], []
    for h in range(N_ATT_HEADS):
        sl = slice(h * ATT_DIM, (h + 1) * ATT_DIM)
        s = _dot(q[:, sl], k[:, sl], NT) * ATT_SCALE
        s = s - jnp.max(s, axis=-1, keepdims=True)
        e = jnp.exp(s)
        p = e * (1.0 / jnp.sum(e, axis=-1, keepdims=True))
        ps.append(p)
        os_.append(_dot(p.astype(BF16), v[:, sl], NN))
    return ps, jnp.concatenate(os_, axis=1)


def _rms_branch(y_pre):
    r = lax.rsqrt(jnp.mean(y_pre * y_pre, axis=-1, keepdims=True) + EPS)
    return y_pre * r, r


def _branch_specs(t, n_tiles, order):
    width_in = 7 * WIDTH
    tile = lambda i: order(i)
    per_halo = t // HALO
    const2 = lambda i: (0, 0)
    const3 = lambda i: (0, 0, 0)
    return [
        pl.BlockSpec((t, width_in), lambda i: (tile(i), 0)),
        pl.BlockSpec((HALO, WIDTH), lambda i: (jnp.maximum(tile(i) * per_halo - 1, 0), 0)),
        pl.BlockSpec((4, GROUP, GROUP), const3),
        pl.BlockSpec((1, WIDTH), const2),
        pl.BlockSpec((1, WIDTH), const2),
        pl.BlockSpec((1, WIDTH), const2),
        pl.BlockSpec((N_SGU_HEADS, CHUNK, CHUNK), const3),
        pl.BlockSpec((CHUNK, WIDTH), const2),
        pl.BlockSpec((MEM_ROWS, WIDTH), const2),
        pl.BlockSpec((MEM_ROWS, WIDTH), const2),
        pl.BlockSpec((1, 3 * WIDTH), const2),
    ]


MEM_ROWS = 256


def _branches_fwd(proj, pool_w, pool_scale, ln_g, ln_b, sgu_w, bias_full, k, v, branch_norm, t, rider=None):
    s = proj.shape[0]
    n_tiles = s // t

    def body(proj_ref, halo_ref, pw_ref, ps_ref, lg_ref, lb_ref, sw_ref, sb_ref, k_ref, v_ref, bn_ref, y_ref, yt_ref):
        i = pl.program_id(0)
        col = lambda j: proj_ref[:, j * WIDTH:(j + 1) * WIDTH]

        def put(branch, y_pre):
            sl = slice(branch * WIDTH, (branch + 1) * WIDTH)
            val = (_rms_branch(y_pre)[0] * bn[:, sl]).astype(BF16)
            y_ref[:, sl] = val
            yt_ref[sl, :] = val.T

        bn = bn_ref[...]
        halo = jnp.where(i > 0, halo_ref[...], 0.0)
        _, y_pool = _pool_fwd(col(0), halo, _inv_counts(i * t, t), pw_ref[...])
        ga = col(1)
        ya = y_pool * ps_ref[...] * (ga * _sigmoid(ga))
        put(0, ya)
        vhat, _ = _layernorm_fwd(col(3))
        vn = (vhat * lg_ref[...] + lb_ref[...]).astype(BF16)
        z = _sgu_mix(sw_ref, vn, transposed=False) + jnp.tile(sb_ref[...], (t // CHUNK, 1))
        gb = col(4)
        yb = col(2) * z * (gb * _sigmoid(gb))
        put(1, yb)
        _, o = _attn_fwd(col(5).astype(BF16), k_ref[...], v_ref[...])
        gc = col(6)
        yc = o * (gc * _sigmoid(gc))
        put(2, yc)

    return _call(body, "branches_fwd", (n_tiles,), _branch_specs(t, n_tiles, lambda i: i),
                 [pl.BlockSpec((t, 3 * WIDTH), lambda i: (i, 0)), pl.BlockSpec((3 * WIDTH, t), lambda i: (0, i))],
                 [jax.ShapeDtypeStruct((s, 3 * WIDTH), BF16), jax.ShapeDtypeStruct((3 * WIDTH, s), BF16)], [],
                 [proj, proj, pool_w, pool_scale, ln_g, ln_b, sgu_w, bias_full, k, v, branch_norm], rider)


def _branches_bwd(proj, dy, pool_w, pool_scale, ln_g, ln_b, sgu_w, sgu_wt, bias_full, k, v, branch_norm, t, rider=None):
    s = proj.shape[0]
    n_tiles = s // t
    n_chunks = t // CHUNK
    order = lambda i: n_tiles - 1 - i

    def body(proj_ref, halo_ref, pw_ref, ps_ref, lg_ref, lb_ref, sw_ref, sb_ref, k_ref, v_ref, bn_ref,
             swt_ref, dy_ref,
             dproj_ref, dpw_ref, dps_ref, dlg_ref, dlb_ref, dsw_ref, dsb_ref, dbn_ref, dk_ref, dv_ref,
             carry_ref, dbias_ref):
        step = pl.program_id(0)
        i = order(step)

        @pl.when(step == 0)
        def _():
            for ref in (dpw_ref, dps_ref, dlg_ref, dlb_ref, dsw_ref, dbn_ref, dk_ref, dv_ref, carry_ref, dbias_ref):
                ref[...] = jnp.zeros(ref.shape, ref.dtype)

        col = lambda j: proj_ref[:, j * WIDTH:(j + 1) * WIDTH]
        bn = bn_ref[...]

        def norm_bwd(y_pre, sl):
            yhat, r = _rms_branch(y_pre)
            dyv = dy_ref[:, sl].astype(F32)
            dbn_ref[:, sl] += jnp.sum(dyv * yhat, axis=0, keepdims=True)
            dyhat = dyv * bn[:, sl]
            return r * (dyhat - yhat * jnp.mean(dyhat * yhat, axis=-1, keepdims=True))

        def gate(gv):
            sg = _sigmoid(gv)
            return gv * sg, sg * (1.0 + gv * (1.0 - sg))

        inv = _inv_counts(i * t, t)
        halo = jnp.where(i > 0, halo_ref[...], 0.0)
        pw = pw_ref[...]
        d, y_pool = _pool_fwd(col(0), halo, inv, pw)
        scale = ps_ref[...]
        silu_a, dsilu_a = gate(col(1))
        pa = y_pool * scale
        dya = norm_bwd(pa * silu_a, slice(0, WIDTH))
        dproj_ref[:, WIDTH:2 * WIDTH] = (dya * pa * dsilu_a).astype(BF16)
        dpa = dya * silu_a
        dps_ref[...] += jnp.sum(dpa * y_pool, axis=0, keepdims=True)
        dy_pool = (dpa * scale).astype(BF16)
        dd_parts, ddc_parts = [], []
        for gi in range(4):
            sl = slice(gi * GROUP, (gi + 1) * GROUP)
            dpw_ref[gi] += _dot(d[:, sl], dy_pool[:, sl], TN)
            dd = _dot(dy_pool[:, sl], pw[gi], NT)
            dd_parts.append(dd)
            ddc_parts.append(dd * inv[gi])
        ddc = jnp.concatenate(ddc_parts, axis=1)
        sums = _window_sums(jnp.concatenate([ddc, carry_ref[...]], axis=0), t, backward=True)
        carry_ref[...] = ddc[:HALO]
        dproj_ref[:, 0:WIDTH] = jnp.concatenate([sums[gi] - dd_parts[gi] for gi in range(4)], axis=1).astype(BF16)

        vhat, rstd = _layernorm_fwd(col(3))
        lg = lg_ref[...]
        vn = (vhat * lg + lb_ref[...]).astype(BF16)
        z = _sgu_mix(sw_ref, vn, transposed=False) + jnp.tile(sb_ref[...], (n_chunks, 1))
        u = col(2)
        silu_b, dsilu_b = gate(col(4))
        uz = u * z
        dyb = norm_bwd(uz * silu_b, slice(WIDTH, 2 * WIDTH))
        dproj_ref[:, 4 * WIDTH:5 * WIDTH] = (dyb * uz * dsilu_b).astype(BF16)
        duz = dyb * silu_b
        dproj_ref[:, 2 * WIDTH:3 * WIDTH] = (duz * z).astype(BF16)
        dz = duz * u
        dz_b = dz.astype(BF16)
        for ci in range(n_chunks):
            rows = slice(ci * CHUNK, (ci + 1) * CHUNK)
            dbias_ref[...] += dz[rows]
            for h in range(N_SGU_HEADS):
                sl = slice(h * CHUNK, (h + 1) * CHUNK)
                dsw_ref[h] += _dot(dz_b[rows, sl], vn[rows, sl], NT)
        dvn = _sgu_mix(swt_ref, dz_b, transposed=True)
        dlg_ref[...] += jnp.sum(dvn * vhat, axis=0, keepdims=True)
        dlb_ref[...] += jnp.sum(dvn, axis=0, keepdims=True)
        dvhat = dvn * lg
        dvb = rstd * (dvhat - jnp.mean(dvhat, axis=-1, keepdims=True)
                      - vhat * jnp.mean(dvhat * vhat, axis=-1, keepdims=True))
        dproj_ref[:, 3 * WIDTH:4 * WIDTH] = dvb.astype(BF16)

        q = col(5).astype(BF16)
        kv_k, kv_v = k_ref[...], v_ref[...]
        ps, o = _attn_fwd(q, kv_k, kv_v)
        silu_c, dsilu_c = gate(col(6))
        dyc = norm_bwd(o * silu_c, slice(2 * WIDTH, 3 * WIDTH))
        dproj_ref[:, 6 * WIDTH:7 * WIDTH] = (dyc * o * dsilu_c).astype(BF16)
        do = (dyc * silu_c).astype(BF16)
        dq_parts = []
        for h in range(N_ATT_HEADS):
            sl = slice(h * ATT_DIM, (h + 1) * ATT_DIM)
            p = ps[h]
            dp = _dot(do[:, sl], kv_v[:, sl], NT)
            ds = (p * (dp - jnp.sum(p * dp, axis=-1, keepdims=True)) * ATT_SCALE).astype(BF16)
            dq_parts.append(_dot(ds, kv_k[:, sl], NN))
            dk_ref[:, sl] += _dot(ds, q[:, sl], TN)
            dv_ref[:, sl] += _dot(p.astype(BF16), do[:, sl], TN)
        dproj_ref[:, 5 * WIDTH:6 * WIDTH] = jnp.concatenate(dq_parts, axis=1).astype(BF16)

        @pl.when(step == n_tiles - 1)
        def _():
            keep = _tril_mask(transposed=False)
            for h in range(N_SGU_HEADS):
                dsw_ref[h] = jnp.where(keep, dsw_ref[h], 0.0)
            dsb_ref[...] = jnp.concatenate(
                [jnp.sum(dbias_ref[:, h * CHUNK:(h + 1) * CHUNK], axis=1, keepdims=True)
                 for h in range(N_SGU_HEADS)], axis=1)

    const2 = lambda i: (0, 0)
    const3 = lambda i: (0, 0, 0)
    out_shapes = (
        jax.ShapeDtypeStruct((s, 7 * WIDTH), BF16),
        jax.ShapeDtypeStruct((4, GROUP, GROUP), F32),
        jax.ShapeDtypeStruct((1, WIDTH), F32),
        jax.ShapeDtypeStruct((1, WIDTH), F32),
        jax.ShapeDtypeStruct((1, WIDTH), F32),
        jax.ShapeDtypeStruct((N_SGU_HEADS, CHUNK, CHUNK), F32),
        jax.ShapeDtypeStruct((CHUNK, N_SGU_HEADS), F32),
        jax.ShapeDtypeStruct((1, 3 * WIDTH), F32),
        jax.ShapeDtypeStruct((MEM_ROWS, WIDTH), F32),
        jax.ShapeDtypeStruct((MEM_ROWS, WIDTH), F32),
    )
    out_specs = (
        pl.BlockSpec((t, 7 * WIDTH), lambda i: (order(i), 0)),
        pl.BlockSpec((4, GROUP, GROUP), const3),
        pl.BlockSpec((1, WIDTH), const2),
        pl.BlockSpec((1, WIDTH), const2),
        pl.BlockSpec((1, WIDTH), const2),
        pl.BlockSpec((N_SGU_HEADS, CHUNK, CHUNK), const3),
        pl.BlockSpec((CHUNK, N_SGU_HEADS), const2),
        pl.BlockSpec((1, 3 * WIDTH), const2),
        pl.BlockSpec((MEM_ROWS, WIDTH), const2),
        pl.BlockSpec((MEM_ROWS, WIDTH), const2),
    )
    in_specs = _branch_specs(t, n_tiles, order) + [
        pl.BlockSpec((N_SGU_HEADS, CHUNK, CHUNK), const3),
        pl.BlockSpec((t, 3 * WIDTH), lambda i: (order(i), 0)),
    ]
    return _call(body, "branches_bwd", (n_tiles,), in_specs, out_specs, out_shapes,
                 [pltpu.VMEM((HALO, WIDTH), F32), pltpu.VMEM((CHUNK, WIDTH), F32)],
                 [proj, proj, pool_w, pool_scale, ln_g, ln_b, sgu_w, bias_full, k, v, branch_norm, sgu_wt, dy], rider)


def _out_loss(y, w_out, x, target, g_post, tm):
    s, d = x.shape
    e_w = y.shape[1]
    n_tiles = s // tm

    def body(y_ref, w_ref, x_ref, t_ref, g_ref, loss_ref, dz_ref, dout_ref, dy_ref, dg_ref, sq_ref):
        i = pl.program_id(0)

        @pl.when(i == 0)
        def _():
            sq_ref[...] = jnp.zeros(sq_ref.shape, F32)
            dg_ref[...] = jnp.zeros(dg_ref.shape, F32)

        w = w_ref[...]
        out = _dot(y_ref[...], w, NN)
        r = lax.rsqrt(jnp.mean(out * out, axis=-1, keepdims=True) + EPS)
        outn = out * r
        g = g_ref[...]
        err = (x_ref[...] + outn * g) - t_ref[...]
        sq_ref[...] += jnp.sum(err * err, axis=0, keepdims=True)
        dz = err * (1.0 / d)
        dz_ref[...] = dz
        dg_ref[...] += jnp.sum(dz * outn, axis=0, keepdims=True)
        doutn = dz * g
        dout = (r * (doutn - outn * jnp.mean(doutn * outn, axis=-1, keepdims=True))).astype(BF16)
        dout_ref[...] = dout
        dy_ref[...] = _dot(dout, w, NT).astype(BF16)

        @pl.when(i == n_tiles - 1)
        def _():
            loss_ref[...] = 0.5 * jnp.sum(sq_ref[...], axis=1, keepdims=True) * (1.0 / d)

    row = lambda i: (i, 0)
    const2 = lambda i: (0, 0)
    return pl.pallas_call(
        body, name="out_loss", grid=(n_tiles,),
        in_specs=[
            pl.BlockSpec((tm, e_w), row),
            pl.BlockSpec((e_w, d), const2, pipeline_mode=pl.Buffered(1)),
            pl.BlockSpec((tm, d), row),
            pl.BlockSpec((tm, d), row),
            pl.BlockSpec((1, d), const2),
        ],
        out_specs=(
            pl.BlockSpec((1, 1), const2),
            pl.BlockSpec((tm, d), row),
            pl.BlockSpec((tm, d), row),
            pl.BlockSpec((tm, e_w), row),
            pl.BlockSpec((1, d), const2),
        ),
        out_shape=(
            jax.ShapeDtypeStruct((1, 1), F32),
            jax.ShapeDtypeStruct((s, d), F32),
            jax.ShapeDtypeStruct((s, d), BF16),
            jax.ShapeDtypeStruct((s, e_w), BF16),
            jax.ShapeDtypeStruct((1, d), F32),
        ),
        scratch_shapes=[pltpu.VMEM((1, d), F32)],
        compiler_params=_params(1),
    )(y, w_out, x, target, g_post)


def _dx_call(dproj, w_in, x, dz, g_pre, tm, tk, rider=None):
    s, d = x.shape
    k_total = dproj.shape[1]
    nk = k_total // tk
    n_tiles = s // tm

    def body(dp_ref, w_ref, x_ref, dz_ref, g_ref, dx_ref, dg_ref, acc_ref):
        i, kk = pl.program_id(0), pl.program_id(1)
        part = lambda: _dot(dp_ref[...], w_ref[...], NT)

        @pl.when(kk == 0)
        def _():
            acc_ref[...] = part()

        @pl.when((kk > 0) & (kk < nk - 1))
        def _():
            acc_ref[...] += part()

        @pl.when((i == 0) & (kk == 0))
        def _():
            dg_ref[...] = jnp.zeros(dg_ref.shape, F32)

        @pl.when(kk == nk - 1)
        def _():
            dh = acc_ref[...] + part()
            xv = x_ref[...]
            r = lax.rsqrt(jnp.mean(xv * xv, axis=-1, keepdims=True) + EPS)
            xhat = xv * r
            dg_ref[...] += jnp.sum(dh * xhat, axis=0, keepdims=True)
            dxhat = dh * g_ref[...]
            dx_ref[...] = dz_ref[...] + r * (dxhat - xhat * jnp.mean(dxhat * xhat, axis=-1, keepdims=True))

    row = lambda i, kk: (i, 0)
    const2 = lambda i, kk: (0, 0)
    return _call(
        body, "dx", (n_tiles, nk),
        [
            pl.BlockSpec((tm, tk), lambda i, kk: (i, kk)),
            pl.BlockSpec((d, tk), lambda i, kk: (0, kk)),
            pl.BlockSpec((tm, d), row),
            pl.BlockSpec((tm, d), row),
            pl.BlockSpec((1, d), const2),
        ],
        [pl.BlockSpec((tm, d), row), pl.BlockSpec((1, d), const2)],
        [jax.ShapeDtypeStruct((s, d), F32), jax.ShapeDtypeStruct((1, d), F32)],
        [pltpu.VMEM((tm, d), F32)], [dproj, w_in, x, dz, g_pre], rider)


def _rows_tile(rows, cols, n_arrays, itemsize=4):
    budget = ELEMENTWISE_VMEM // (2 * n_arrays * cols * itemsize)
    if rows <= budget:
        return rows
    best = None
    for cand in range(16, rows + 1, 16):
        if rows % cand == 0 and cand <= max(budget, 16):
            best = cand
    return best if best is not None else rows


def _elementwise(fn, inputs, out_dtypes, name):
    rows, cols = inputs[0].shape
    tr = _rows_tile(rows, cols, len(inputs) + len(out_dtypes))
    n_in = len(inputs)

    def body(*refs):
        outs = fn(*[r[...] for r in refs[:n_in]])
        for o_ref, o in zip(refs[n_in:], outs):
            o_ref[...] = o.astype(o_ref.dtype)

    spec = pl.BlockSpec((tr, cols), lambda i: (i, 0))
    return pl.pallas_call(
        body, name=name, grid=(rows // tr,),
        in_specs=[spec] * n_in, out_specs=tuple([spec] * len(out_dtypes)),
        out_shape=tuple(jax.ShapeDtypeStruct((rows, cols), dt) for dt in out_dtypes),
        compiler_params=_params(1),
    )(*inputs)


def _blockwise(fn, pos, inputs, in_specs, out_shape, out_spec, grid, name):
    n_in = len(inputs)

    def body(pos_ref, *refs):
        o_ref = refs[n_in]
        (out,) = fn(*[r[...].reshape(o_ref.shape) for r in refs[:n_in]])
        o_ref[...] = out.astype(o_ref.dtype)

    return pl.pallas_call(
        body, name=name,
        grid_spec=pltpu.PrefetchScalarGridSpec(num_scalar_prefetch=1, grid=grid, in_specs=in_specs,
                                               out_specs=out_spec),
        out_shape=out_shape,
        compiler_params=_params(len(grid)),
    )(pos, *inputs)


def _cast_copy(x):
    return (x,)


def _pair_sum(mine, theirs):
    return ((mine.astype(F32) + theirs.astype(F32)),)


def _four_sum(own, t0, t1, t2):
    return ((((own.astype(F32) + t0.astype(F32)) + t1.astype(F32)) + t2.astype(F32)),)


def _adamw(w, g, m, v):
    m = ADAM_B1 * m + (1.0 - ADAM_B1) * g
    v = ADAM_B2 * v + (1.0 - ADAM_B2) * jnp.square(g)
    m_hat = m / (1.0 - ADAM_B1 ** ADAM_STEP)
    v_hat = v / (1.0 - ADAM_B2 ** ADAM_STEP)
    delta = -ADAM_LR * (m_hat / (jnp.sqrt(v_hat) + ADAM_EPS) + ADAM_WD * w)
    return delta, m, v


def _place():
    x, y, c = lax.axis_index("x"), lax.axis_index("y"), lax.axis_index("c")
    chips = [(1 - x, y), (x, 1 - y), (1 - x, 1 - y)]
    return x, y, c, chips


def _remote(src, dst, send_sem, recv_sem, to):
    return pltpu.make_async_remote_copy(src_ref=src, dst_ref=dst, send_sem=send_sem, recv_sem=recv_sem,
                                        device_id=to, device_id_type=MESH)


def _split(ref, plan):
    views = [ref]
    for axis, parts in plan:
        size = ref.shape[axis] // parts
        assert size * parts == ref.shape[axis]
        views = [v.at[tuple(pl.ds(q * size, size) if i == axis else slice(None) for i in range(len(ref.shape)))]
                 for v in views for q in range(parts)]
    return views


def _started(src, dst, send_sem, recv_sem, to):
    copy = _remote(src, dst, send_sem, recv_sem, to)
    copy.start()
    return copy


def _shard_half(kind, ref, chip, cc):
    if kind == 0:
        rows, cols = ref.shape[0] // 2, ref.shape[1] // 4
        return ref.at[pl.ds(cc * rows, rows), pl.ds(pl.multiple_of(chip * cols, LANES), cols)]
    if kind == 3:
        rows = ref.shape[1] // 8
        return ref.at[:, pl.ds(pl.multiple_of((2 * chip + cc) * rows, BF16_ROWS), rows), :]
    rows = ref.shape[0] // 8
    return ref.at[pl.ds(pl.multiple_of((2 * chip + cc) * rows, BF16_ROWS), rows), :]


def _relay_rider(full):
    kind = 0

    def quarter(ref, chip_no, cc, q):
        return _split(_shard_half(kind, ref, chip_no, cc), [(0, 2)])[q]

    def run(in_refs, full_refs, send_sems, recv_sems, start):
        (ref,) = full_refs
        x, y, c, chips = _place()
        sibling = (x, y, 1 - c)
        chip_no = [2 * ch[0] + ch[1] for ch in chips]
        if start:
            for p in (0, 1):
                held = quarter(ref, chip_no[1 - p], c, p)
                _remote(held, held, send_sems.at[p], recv_sems.at[p], (*chips[p], c)).start()
            return
        for p in (0, 1):
            landed = quarter(ref, chip_no[2], c, p)
            _remote(landed, landed, send_sems.at[p], recv_sems.at[p], (*chips[p], c)).wait_recv()
            _remote(landed, landed, send_sems.at[2], recv_sems.at[2], sibling).start()
        mine, theirs = _shard_half(kind, ref, chip_no[2], c), _shard_half(kind, ref, chip_no[2], 1 - c)
        _remote(mine, mine, send_sems.at[2], recv_sems.at[2], sibling).wait_send()
        _remote(theirs, theirs, send_sems.at[2], recv_sems.at[2], sibling).wait_recv()
        for p in (0, 1):
            held = quarter(ref, chip_no[1 - p], c, p)
            _remote(held, held, send_sems.at[p], recv_sems.at[p], (*chips[p], c)).wait_send()

    return _Rider([full], [jax.ShapeDtypeStruct(full.shape, full.dtype)], 3, run, aliases={0: 0})


def _riders(riders):
    def bounds(counts):
        ends = [sum(counts[:i + 1]) for i in range(len(counts))]
        return list(zip([0] + ends[:-1], ends))

    ins = bounds([len(r.inputs) for r in riders])
    outs = bounds([len(r.out_shapes) for r in riders])
    sems = bounds([r.n_sems for r in riders])

    class From:
        def __init__(self, sem_refs, base):
            self.sem_refs, self.base, self.at = sem_refs, base, self

        def __getitem__(self, k):
            return self.sem_refs.at[self.base + k]

    def run(in_refs, out_refs, send_sems, recv_sems, start):
        for r, (i0, i1), (o0, o1), (s0, _) in zip(riders, ins, outs, sems):
            r.run(in_refs[i0:i1], out_refs[o0:o1], From(send_sems, s0), From(recv_sems, s0), start)

    aliases = {}
    for r, (i0, _), (o0, _) in zip(riders, ins, outs):
        aliases.update({i0 + i: o0 + o for i, o in r.aliases.items()})
    return _Rider([a for r in riders for a in r.inputs], [o for r in riders for o in r.out_shapes],
                  sems[-1][1], run, aliases)


def _gather_rider(fulls, kinds, peers=(0, 1, 2)):
    n = len(fulls)
    full_half = _shard_half

    def run(in_refs, full_refs, send_sems, recv_sems, start):
        x, y, c, chips = _place()
        me = 2 * x + y
        sibling = (x, y, 1 - c)
        plans = [[(0, MAX_PARTS)], [(0, 2)], [(0, 2)], []]
        chips = [(p, chips[p]) for p in peers]
        across = lambda a, p, k: (3 * a + p) * MAX_PARTS + k
        onward = lambda a, p: 3 * n * MAX_PARTS + 3 * a + p

        def parts(a, chip_no, cc):
            return _split(full_half(kinds[a], full_refs[a], chip_no, cc), plans[kinds[a]])

        if start:
            for p, chip in chips:
                for a in range(n):
                    for k, mine in enumerate(parts(a, me, c)):
                        _remote(mine, mine, send_sems.at[across(a, p, k)], recv_sems.at[across(a, p, k)],
                                (*chip, c)).start()
            return
        for k in range(MAX_PARTS):
            for p, chip in chips:
                for a in range(n):
                    landed = parts(a, 2 * chip[0] + chip[1], c)
                    if k < len(landed):
                        _remote(landed[k], landed[k], send_sems.at[across(a, p, k)], recv_sems.at[across(a, p, k)],
                                (*chip, c)).wait_recv()
                        _remote(landed[k], landed[k], send_sems.at[onward(a, p)], recv_sems.at[onward(a, p)],
                                sibling).start()
        for p, chip in chips:
            them = 2 * chip[0] + chip[1]
            for a in range(n):
                passed = full_half(kinds[a], full_refs[a], them, 1 - c)
                _remote(passed, passed, send_sems.at[onward(a, p)], recv_sems.at[onward(a, p)], sibling).wait_recv()
                landed = full_half(kinds[a], full_refs[a], them, c)
                _remote(landed, landed, send_sems.at[onward(a, p)], recv_sems.at[onward(a, p)], sibling).wait_send()
                for k, mine in enumerate(parts(a, me, c)):
                    _remote(mine, mine, send_sems.at[across(a, p, k)], recv_sems.at[across(a, p, k)],
                            (*chip, c)).wait_send()

    return _Rider(fulls, [jax.ShapeDtypeStruct(f.shape, f.dtype) for f in fulls], 3 * n * (MAX_PARTS + 1), run,
                  aliases={a: a for a in range(n)})


def _exchange_rider(arrays, half_axes=None):
    n = len(arrays)
    half_axes = half_axes or [None] * n
    out_shapes = [jax.ShapeDtypeStruct(tuple(1 if i == ax else dim for i, dim in enumerate(g.shape)), g.dtype)
                  for g, ax in zip(arrays, half_axes)]

    def run(in_refs, out_refs, send_sems, recv_sems, start):
        x, y, c, _ = _place()
        sibling = (x, y, 1 - c)
        for a in range(n):
            src, ax = in_refs[a], half_axes[a]
            if ax is not None:
                src = src.at[tuple(pl.ds(1 - c, 1) if i == ax else slice(None) for i in range(len(src.shape)))]
            sems = (send_sems.at[a], recv_sems.at[a])
            if start:
                _started(src, out_refs[a], *sems, sibling)
            else:
                _remote(src, out_refs[a], *sems, sibling).wait()

    return _Rider(arrays, out_shapes, n, run)


def _scatter_rider(parts):
    n = len(parts)
    arrays = [p for p, _ in parts]

    def block_shape(p, ax):
        if ax == len(p.shape) - 1:
            return p.shape[:-1] + (p.shape[-1] // 4,)
        return tuple(1 if i == ax else dim for i, dim in enumerate(p.shape))

    out_shapes = [jax.ShapeDtypeStruct((3,) + block_shape(p, ax), p.dtype) for p, ax in parts]

    def block(ref, ax, chip):
        rank = len(ref.shape)
        if ax == rank - 1:
            cols = ref.shape[-1] // 4
            last = pl.ds(pl.multiple_of(chip * cols, LANES), cols)
            return ref.at[tuple([slice(None)] * (rank - 1) + [last])]
        return ref.at[tuple(pl.ds(chip, 1) if i == ax else slice(None) for i in range(rank))]

    def run(in_refs, out_refs, send_sems, recv_sems, start):
        x, y, c, chips = _place()
        for a in range(n):
            ax = parts[a][1]
            for p, chip in enumerate(chips):
                src, dst = block(in_refs[a], ax, 2 * chip[0] + chip[1]), out_refs[a].at[p]
                sems = (send_sems.at[3 * a + p], recv_sems.at[3 * a + p])
                if start:
                    _started(src, dst, *sems, (*chip, c))
                else:
                    _remote(src, dst, *sems, (*chip, c)).wait()

    return _Rider(arrays, out_shapes, 3 * n, run)


def _join_rider(joined):
    n = len(joined)
    arrays = [j for j, _ in joined]

    def run(in_refs, out_refs, send_sems, recv_sems, start):
        x, y, c, _ = _place()
        sibling = (x, y, 1 - c)

        def half(a, cc):
            rank = len(out_refs[a].shape)
            return out_refs[a].at[tuple(pl.ds(cc, 1) if i == joined[a][1] else slice(None) for i in range(rank))]

        for a in range(n):
            sems = (send_sems.at[a], recv_sems.at[a])
            if start:
                _started(half(a, c), half(a, c), *sems, sibling)
            else:
                _remote(half(a, c), half(a, c), *sems, sibling).wait_send()
                _remote(half(a, 1 - c), half(a, 1 - c), *sems, sibling).wait_recv()

    return _Rider(arrays, [jax.ShapeDtypeStruct(j.shape, j.dtype) for j in arrays], n, run,
                  aliases={a: a for a in range(n)})


def _allreduce_small(packed, rider):
    rows, lanes = packed.shape
    half = rows // 2
    r_in, r_out = len(rider.inputs), len(rider.out_shapes)

    def body(in_ref, *refs):
        rider_ins, out_ref, rider_outs = refs[:r_in], refs[r_in], refs[r_in + 1:r_in + 1 + r_out]
        pair_ref, gath_ref, send_sems, recv_sems, rider_send, rider_recv = refs[r_in + 1 + r_out:]
        rider.run(rider_ins, rider_outs, rider_send, rider_recv, True)
        x, y, c, chips = _place()
        me = 2 * x + y
        sibling = (x, y, 1 - c)
        mine = pl.ds(pl.multiple_of(c * half, 8), half)
        theirs = pl.ds(pl.multiple_of((1 - c) * half, 8), half)
        to_sib = _remote(in_ref.at[theirs], pair_ref, send_sems.at[0], recv_sems.at[0], sibling)
        to_sib.start()
        to_sib.wait()
        gath_ref[me] = in_ref[mine] + pair_ref[...]
        sends = [_remote(gath_ref.at[me], gath_ref.at[me], send_sems.at[1 + p], recv_sems.at[1 + p], (*chip, c))
                 for p, chip in enumerate(chips)]
        for cp in sends:
            cp.start()
        for p, chip in enumerate(chips):
            slot = gath_ref.at[2 * chip[0] + chip[1]]
            _remote(slot, slot, send_sems.at[1 + p], recv_sems.at[1 + p], (*chip, c)).wait_recv()
        for cp in sends:
            cp.wait_send()
        out_ref[mine] = ((gath_ref[0] + gath_ref[1]) + gath_ref[2]) + gath_ref[3]
        back = _remote(out_ref.at[mine], out_ref.at[mine], send_sems.at[4], recv_sems.at[4], sibling)
        back.start()
        back.wait_send()
        _remote(out_ref.at[theirs], out_ref.at[theirs], send_sems.at[4], recv_sems.at[4], sibling).wait_recv()
        rider.run(rider_ins, rider_outs, rider_send, rider_recv, False)

    vmem = pl.BlockSpec(memory_space=pltpu.VMEM)
    return pl.pallas_call(
        body, name="allreduce_small",
        in_specs=[vmem] + [ANY] * r_in, out_specs=(vmem,) + (ANY,) * r_out,
        out_shape=(jax.ShapeDtypeStruct((rows, lanes), F32),) + tuple(rider.out_shapes),
        scratch_shapes=[pltpu.VMEM((half, lanes), F32), pltpu.VMEM((4, half, lanes), F32),
                        pltpu.SemaphoreType.DMA((5,)), pltpu.SemaphoreType.DMA((5,)),
                        pltpu.SemaphoreType.DMA((rider.n_sems,)), pltpu.SemaphoreType.DMA((rider.n_sems,))],
        input_output_aliases={1 + i: 1 + o for i, o in rider.aliases.items()},
        compiler_params=pltpu.CompilerParams(has_side_effects=True, vmem_limit_bytes=32 * 1024 * 1024),
    )(packed, *rider.inputs)


SMALL = ("norm_pre", "pool_scale", "sgu_ln_g", "sgu_ln_b", "sgu_w", "sgu_b", "mem_norm", "branch_norm", "norm_post")
LARGE = ("w_in", "pool_w", "w_kv", "w_out")
ORDER = ("norm_pre", "w_in", "pool_w", "pool_scale", "sgu_ln_g", "sgu_ln_b", "sgu_w", "sgu_b", "mem_norm", "w_kv",
         "branch_norm", "w_out", "norm_post")


def _pack(arrays, extra=()):
    rows = [a.reshape(-1, 128) for a in arrays] + list(extra)
    pad = -sum(r.shape[0] for r in rows) % 16
    return jnp.concatenate(rows + ([jnp.zeros((pad, 128), F32)] if pad else []), axis=0)


def _unpack(packed, like):
    out, row = [], 0
    for a in like:
        rows = a.size // 128
        out.append(packed[row:row + rows].reshape(a.shape))
        row += rows
    return out


def kernel(x, mem, norm_pre, w_in, pool_w, pool_scale, sgu_ln_g, sgu_ln_b, sgu_w, sgu_b, mem_norm, w_kv, branch_norm, w_out, norm_post, loss_target, m_norm_pre, m_w_in, m_pool_w, m_pool_scale, m_sgu_ln_g, m_sgu_ln_b, m_sgu_w, m_sgu_b, m_mem_norm, m_w_kv, m_branch_norm, m_w_out, m_norm_post, v_norm_pre, v_w_in, v_pool_w, v_pool_scale, v_sgu_ln_g, v_sgu_ln_b, v_sgu_w, v_sgu_b, v_mem_norm, v_w_kv, v_branch_norm, v_w_out, v_norm_post):
    weights = dict(norm_pre=norm_pre, w_in=w_in, pool_w=pool_w, pool_scale=pool_scale, sgu_ln_g=sgu_ln_g,
                   sgu_ln_b=sgu_ln_b, sgu_w=sgu_w, sgu_b=sgu_b, mem_norm=mem_norm, w_kv=w_kv, branch_norm=branch_norm,
                   w_out=w_out, norm_post=norm_post)
    mom1 = dict(norm_pre=m_norm_pre, w_in=m_w_in, pool_w=m_pool_w, pool_scale=m_pool_scale, sgu_ln_g=m_sgu_ln_g,
                sgu_ln_b=m_sgu_ln_b, sgu_w=m_sgu_w, sgu_b=m_sgu_b, mem_norm=m_mem_norm, w_kv=m_w_kv,
                branch_norm=m_branch_norm, w_out=m_w_out, norm_post=m_norm_post)
    mom2 = dict(norm_pre=v_norm_pre, w_in=v_w_in, pool_w=v_pool_w, pool_scale=v_pool_scale, sgu_ln_g=v_sgu_ln_g,
                sgu_ln_b=v_sgu_ln_b, sgu_w=v_sgu_w, sgu_b=v_sgu_b, mem_norm=v_mem_norm, w_kv=v_w_kv,
                branch_norm=v_branch_norm, w_out=v_w_out, norm_post=v_norm_post)

    s, d = x.shape[1], x.shape[2]
    x2, mem2, tgt2 = x[0], mem[0], loss_target[0]
    t_branch = min(256, s)
    tm = min(512, s)

    core = lax.axis_index("c")
    chip = 2 * lax.axis_index("x") + lax.axis_index("y")
    pos = jnp.stack([core, chip]).astype(jnp.int32)
    n_in, n_kv, n_out = 4 * w_in.shape[2], 4 * w_kv.shape[1], 4 * w_out.shape[1]
    wi_rows, kv_rows, wo_rows = d // 8, n_kv // 8, n_out // 8

    kv_cols, pw_rows = w_kv.shape[2], GROUP // 8
    wi_own = _blockwise(_cast_copy, pos, [w_in[0]], [pl.BlockSpec((wi_rows, n_in // 4), lambda i, p: (i, 0))],
                        jax.ShapeDtypeStruct((d, n_in), BF16),
                        pl.BlockSpec((wi_rows, n_in // 4), lambda i, p: (i, p[1])), (8,), "place_w_in")

    x_pos, y_pos = lax.axis_index("x"), lax.axis_index("y")
    chips = jnp.stack([chip, 2 * (1 - x_pos) + y_pos, 2 * x_pos + 1 - y_pos,
                       2 * (1 - x_pos) + 1 - y_pos]).astype(jnp.int32)
    mem_g = mem_norm.reshape(1, d)
    proj, h, h_t, wkv_own, wo_own, pw_own, wi_full = _proj_piece(
        chips, 0, 1, x2, norm_pre, None, None, n_in, _gather_rider([wi_own], [0], peers=(0, 1)), tm, "proj_own",
        casts=[(w_kv[0], 1), (w_out[0], 2), (pool_w[0], 3)])
    proj, wi_full, wkv_full, pw_full = _proj_piece(
        chips, 1, 2, h, None, None, proj, n_in,
        _riders([_relay_rider(wi_full), _gather_rider([wkv_own, pw_own], [1, 3])]), tm, "proj_neighbours")
    proj, k_m, v_m, wo_part = _proj_piece(chips, 3, 1, h, None, wi_full, proj, n_in,
                                          _gather_rider([wo_own], [2], peers=(0, 1)), tm, "proj_diagonal",
                                          memory=(mem2, mem_g, wkv_full))
    bias_full = jnp.repeat(sgu_b[0].T, CHUNK, axis=1)
    y, y_t, wo_full = _branches_fwd(proj, pw_full, pool_scale, sgu_ln_g, sgu_ln_b, sgu_w[0], bias_full, k_m, v_m,
                                    branch_norm, t_branch, _gather_rider([wo_part], [2], peers=(2,)))
    loss_local, dz, dout, dy, g_norm_post = _out_loss(y, wo_full, x2, tgt2, norm_post, min(256, s))

    tk = min(1024, s)
    (dproj, g_pw, g_pool_scale, g_ln_g, g_ln_b, g_sgu_w, g_sgu_b_t, g_branch_norm, dk, dv) = _branches_bwd(
        proj, dy, pw_full, pool_scale, sgu_ln_g, sgu_ln_b, sgu_w[0], jnp.swapaxes(sgu_w[0], 1, 2), bias_full,
        k_m, v_m, branch_norm, t_branch)
    g_wkv, g_mem_norm = _kv_bwd(mem2, mem_g, wkv_full, dk, dv)
    g_wkv = g_wkv.reshape(4, 2, kv_rows, kv_cols)
    g_pw = g_pw.astype(BF16).reshape(4, 4, 2, pw_rows, GROUP)
    g_wo, gkv_from_sibling, gpw_from_sibling = _grad_rows(y_t, dout, pos, lambda i, p: i, n_out, n_out // 2, 1024, tk,
                                                          "grad_w_out", _exchange_rider([g_wkv, g_pw], [1, 2]))
    g_wo = g_wo.reshape(4, 2, wo_rows, d)
    ps_kv = _blockwise(_pair_sum, pos, [g_wkv, gkv_from_sibling],
                       [pl.BlockSpec((1, 1, kv_rows, kv_cols), lambda i, p: (i, p[0], 0, 0)),
                        pl.BlockSpec((1, 1, kv_rows, kv_cols), lambda i, p: (i, 0, 0, 0))],
                       jax.ShapeDtypeStruct((4, kv_rows, kv_cols), BF16),
                       pl.BlockSpec((1, kv_rows, kv_cols), lambda i, p: (i, 0, 0)), (4,), "pair_sum_w_kv")
    ps_pw = _blockwise(_pair_sum, pos, [g_pw, gpw_from_sibling],
                       [pl.BlockSpec((1, 4, 1, pw_rows, GROUP), lambda i, p: (i, 0, p[0], 0, 0)),
                        pl.BlockSpec((1, 4, 1, pw_rows, GROUP), lambda i, p: (i, 0, 0, 0, 0))],
                       jax.ShapeDtypeStruct((4, 4, pw_rows, GROUP), BF16),
                       pl.BlockSpec((1, 4, pw_rows, GROUP), lambda i, p: (i, 0, 0, 0)), (4,), "pair_sum_pool_w")
    gwi_theirs, landed_kv, landed_pw, gwo_from_sibling = _grad_rows(
        h_t, dproj, pos, lambda i, p: 1 - p[0], d // 2, d // 2, n_in // 4, tk, "grad_w_in_sibling_half",
        _riders([_scatter_rider([(ps_kv, 0), (ps_pw, 1)]), _exchange_rider([g_wo], [1])]))
    ps_wo = _blockwise(_pair_sum, pos, [g_wo, gwo_from_sibling],
                       [pl.BlockSpec((1, 1, wo_rows, d), lambda i, p: (i, p[0], 0, 0)),
                        pl.BlockSpec((1, 1, wo_rows, d), lambda i, p: (i, 0, 0, 0))],
                       jax.ShapeDtypeStruct((4, wo_rows, d), BF16),
                       pl.BlockSpec((1, wo_rows, d), lambda i, p: (i, 0, 0)), (4,), "pair_sum_w_out")
    gwi_mine, gwi_from_sibling, landed_wo = _grad_rows(
        h_t, dproj, pos, lambda i, p: p[0], d // 2, d // 2, n_in // 4, tk, "grad_w_in_own_half",
        _riders([_exchange_rider([gwi_theirs]), _scatter_rider([(ps_wo, 0)])]))
    ps_wi = _elementwise(_pair_sum, [gwi_mine, gwi_from_sibling], [BF16], "pair_sum_w_in")[0]
    grad_x, g_norm_pre, landed_wi = _dx_call(dproj, wi_full, x2, dz, norm_pre, tm, 1024,
                                             _scatter_rider([(ps_wi, 1)]))
    psum = [ps_wi, ps_kv, ps_wo, ps_pw]
    landed = [landed_wi, landed_kv, landed_wo, landed_pw]
    from_chip = lambda spec_shape, rank: [
        pl.BlockSpec(spec_shape, functools.partial(lambda i, p, q: (q, i) + (0,) * (rank - 2), q=q))
        for q in range(3)]
    join_rider = _join_rider([
        (_blockwise(_four_sum, pos, [psum[0]] + [landed[0]] * 3,
                    [pl.BlockSpec((256, n_in // 4), lambda i, p: (i, p[1]))] + from_chip((1, 256, n_in // 4), 3),
                    jax.ShapeDtypeStruct((2, d // 2, n_in // 4), F32),
                    pl.BlockSpec((1, 256, n_in // 4), lambda i, p: (p[0], i, 0)), (d // 2 // 256,), "chip_sum_w_in"),
         0),
        (_blockwise(_four_sum, pos, [psum[1]] + [landed[1]] * 3,
                    [pl.BlockSpec((1, kv_rows, kv_cols), lambda i, p: (p[1], 0, 0))]
                    + from_chip((1, 1, kv_rows, kv_cols), 4),
                    jax.ShapeDtypeStruct((2, kv_rows, kv_cols), F32),
                    pl.BlockSpec((1, kv_rows, kv_cols), lambda i, p: (p[0], 0, 0)), (1,), "chip_sum_w_kv"),
         0),
        (_blockwise(_four_sum, pos, [psum[2]] + [landed[2]] * 3,
                    [pl.BlockSpec((1, wo_rows, d), lambda i, p: (p[1], 0, 0))] + from_chip((1, 1, wo_rows, d), 4),
                    jax.ShapeDtypeStruct((2, wo_rows, d), F32),
                    pl.BlockSpec((1, wo_rows, d), lambda i, p: (p[0], 0, 0)), (1,), "chip_sum_w_out"),
         0),
        (_blockwise(_four_sum, pos, [psum[3]] + [landed[3]] * 3,
                    [pl.BlockSpec((4, 1, pw_rows, GROUP), lambda i, p: (0, p[1], 0, 0))]
                    + from_chip((1, 4, 1, pw_rows, GROUP), 5),
                    jax.ShapeDtypeStruct((4, 2, pw_rows, GROUP), F32),
                    pl.BlockSpec((4, 1, pw_rows, GROUP), lambda i, p: (0, p[0], 0, 0)), (1,), "chip_sum_pool_w"),
         1),
    ])

    small_local = dict(norm_pre=g_norm_pre, pool_scale=g_pool_scale, sgu_ln_g=g_ln_g, sgu_ln_b=g_ln_b,
                       sgu_w=g_sgu_w, sgu_b=g_sgu_b_t.T, mem_norm=g_mem_norm, branch_norm=g_branch_norm,
                       norm_post=g_norm_post)
    small_rows = sum(weights[n].size for n in SMALL) // 128
    small_sum, *joined = _allreduce_small(
        _pack([small_local[n] for n in SMALL], [jnp.pad(loss_local, ((0, 7), (0, 127)))]), join_rider)
    grads = {"w_in": joined[0].reshape(w_in.shape), "w_kv": joined[1].reshape(w_kv.shape),
             "w_out": joined[2].reshape(w_out.shape), "pool_w": joined[3].reshape(pool_w.shape)}
    for n, g in zip(SMALL, _unpack(small_sum, [weights[n] for n in SMALL])):
        grads[n] = g
    loss = small_sum[small_rows, 0]

    delta, new_m, new_v = {}, {}, {}
    packed = [small_sum if src is grads else _pack([src[n] for n in SMALL]) for src in (weights, grads, mom1, mom2)]
    outs = _elementwise(_adamw, packed, [F32, F32, F32], "adamw_small")
    for dst, o in zip((delta, new_m, new_v), outs):
        for n, a in zip(SMALL, _unpack(o, [weights[n] for n in SMALL])):
            dst[n] = a
    for n in LARGE:
        cols = weights[n].shape[-1]
        outs = _elementwise(lambda w, g, m, v: _adamw(w, g, m, v) + (g,),
                            [src[n].reshape(-1, cols) for src in (weights, grads, mom1, mom2)],
                            [F32, F32, F32, F32], "adamw_" + n)
        for dst, o in zip((delta, new_m, new_v, grads), outs):
            dst[n] = o.reshape(weights[n].shape)

    return (loss, grad_x[None], *[grads[n] for n in ORDER], *[delta[n] for n in ORDER],
            *[new_m[n] for n in ORDER], *[new_v[n] for n in ORDER])
```

```python
import functools

import jax
import jax.numpy as jnp
from jax import lax
from jax.experimental import pallas as pl
from jax.experimental.pallas import tpu as pltpu

F32 = jnp.float32
BF16 = jnp.bfloat16
EPS = 1e-6
MESH = pl.DeviceIdType.MESH
ANY = pl.BlockSpec(memory_space=pl.ANY)

POOL_WINDOWS = (2, 4, 8, 16)
GROUP = 256
HALO = 16
CHUNK = 128
N_SGU_HEADS = 8
N_ATT_HEADS = 4
ATT_DIM = 256
WIDTH = 1024
ATT_SCALE = 1.0 / 16.0

ADAM_LR = 0.001
ADAM_B1 = 0.9
ADAM_B2 = 0.999
ADAM_EPS = 1e-08
ADAM_WD = 0.01
ADAM_STEP = 10

VMEM_LIMIT = 60 * 1024 * 1024
ELEMENTWISE_VMEM = 24 * 1024 * 1024
LANES = 128
BF16_ROWS = 16
MAX_PARTS = 4


def _params(n_grid_axes, vmem=VMEM_LIMIT):
    return pltpu.CompilerParams(dimension_semantics=("arbitrary",) * n_grid_axes, vmem_limit_bytes=vmem)


def _dot(a, b, dims):
    return lax.dot_general(a, b, (dims, ((), ())), preferred_element_type=F32)


NN = ((1,), (0,))
NT = ((1,), (1,))
TN = ((0,), (0,))


class _Rider:
    def __init__(self, inputs, out_shapes, n_sems, run, aliases=None):
        self.inputs, self.out_shapes, self.n_sems, self.run = list(inputs), list(out_shapes), n_sems, run
        self.aliases = aliases or {}


def _call(body, name, grid, in_specs, out_specs, out_shape, scratch_shapes, inputs, rider=None, prefetch=None,
          aliases=None, rider_refs=False):
    n_in, n_out, n_scr = len(in_specs), len(out_specs), len(scratch_shapes)
    r_in = len(rider.inputs) if rider else 0
    r_out = len(rider.out_shapes) if rider else 0
    n_pre = 0 if prefetch is None else 1

    def whole_body(*refs):
        pre, refs = refs[:n_pre], refs[n_pre:]
        ins, rider_ins = refs[:n_in], refs[n_in:n_in + r_in]
        refs = refs[n_in + r_in:]
        outs, rider_outs = refs[:n_out], refs[n_out:n_out + r_out]
        refs = refs[n_out + r_out:]
        scratch, sems = refs[:n_scr], refs[n_scr:]
        extra = {"rider_outs": rider_outs} if rider_refs else {}
        if rider is None:
            body(*pre, *ins, *outs, *scratch, **extra)
            return
        ids = [pl.program_id(ax) for ax in range(len(grid))]
        first = functools.reduce(lambda p, q: p & q, [i == 0 for i in ids])
        last = functools.reduce(lambda p, q: p & q, [i == g - 1 for i, g in zip(ids, grid)])

        @pl.when(first)
        def _():
            rider.run(rider_ins, rider_outs, *sems, True)

        body(*pre, *ins, *outs, *scratch, **extra)

        @pl.when(last)
        def _():
            rider.run(rider_ins, rider_outs, *sems, False)

    io_aliases = {n_pre + i: o for i, o in (aliases or {}).items()}
    scratch_all = list(scratch_shapes)
    if rider:
        io_aliases.update({n_pre + n_in + i: n_out + o for i, o in rider.aliases.items()})
        scratch_all += [pltpu.SemaphoreType.DMA((rider.n_sems,)), pltpu.SemaphoreType.DMA((rider.n_sems,))]
    specs = dict(grid=grid, in_specs=list(in_specs) + [ANY] * r_in, out_specs=tuple(out_specs) + (ANY,) * r_out,
                 scratch_shapes=scratch_all)
    if n_pre:
        specs = dict(grid_spec=pltpu.PrefetchScalarGridSpec(num_scalar_prefetch=1, **specs))
    outs = pl.pallas_call(
        whole_body, name=name, **specs,
        out_shape=tuple(out_shape) + tuple(rider.out_shapes if rider else ()),
        input_output_aliases=io_aliases, compiler_params=_params(len(grid)),
    )(*([prefetch] if n_pre else []), *inputs, *(rider.inputs if rider else []))
    return tuple(outs)


def _grad_rows(a_t, b, pos, row_of, m, tm, tn, tk, name, rider=None):
    k, n = a_t.shape[1], b.shape[1]
    nk = k // tk
    out_dtype, dims, a = BF16, NN, a_t
    a_spec = pl.BlockSpec((tm, tk), lambda i, j, kk, p: (row_of(i, p), kk))
    b_spec = pl.BlockSpec((tk, tn), lambda i, j, kk, p: (kk, j))

    def body(pos_ref, a_ref, b_ref, o_ref, *acc):
        part = lambda: _dot(a_ref[...], b_ref[...], dims)
        if nk == 1:
            o_ref[...] = part().astype(out_dtype)
            return
        (acc_ref,) = acc
        kk = pl.program_id(2)

        @pl.when(kk == 0)
        def _():
            acc_ref[...] = part()

        @pl.when((kk > 0) & (kk < nk - 1))
        def _():
            acc_ref[...] += part()

        @pl.when(kk == nk - 1)
        def _():
            o_ref[...] = (acc_ref[...] + part()).astype(out_dtype)

    return _call(body, name, (m // tm, n // tn, nk), [a_spec, b_spec],
                 [pl.BlockSpec((tm, tn), lambda i, j, kk, p: (i, j))], [jax.ShapeDtypeStruct((m, n), out_dtype)],
                 [pltpu.VMEM((tm, tn), F32)] if nk > 1 else [], [a, b], rider, prefetch=pos)


def _proj_piece(chips, first, n_shards, src, g_pre, w_in, proj_in, n_cols, rider, tm, name, casts=(), memory=()):
    s, d = src.shape
    cols = n_cols // 4
    fused = g_pre is not None
    n_steps = s // tm

    def body(chips_ref, *refs, rider_outs=()):
        refs = list(refs)
        src_ref = refs.pop(0)
        g_ref = refs.pop(0) if fused else None
        w_ref = refs.pop(0) if w_in is not None else rider_outs[0]
        if proj_in is not None:
            refs.pop(0)
        shard_refs = [refs.pop(0) for _ in casts]
        memory_refs = [refs.pop(0) for _ in memory]
        proj_ref = refs.pop(0)
        h_ref, ht_ref = (refs.pop(0), refs.pop(0)) if fused else (None, None)
        for shard_ref in shard_refs:
            refs.pop(0)[...] = shard_ref[...].astype(BF16)
        kv_refs = [refs.pop(0) for _ in memory[:2]]
        wbuf, sem = refs
        q, i = pl.program_id(0), pl.program_id(1)

        @pl.when(i == 0)
        def _():
            at = pl.multiple_of(chips_ref[first + q] * cols, LANES)
            cp = pltpu.make_async_copy(w_ref.at[:, pl.ds(at, cols)], wbuf, sem)
            cp.start()
            cp.wait()

        if memory:
            @pl.when((q == 0) & (i == 0))
            def _():
                _kv_body(*memory_refs, *kv_refs)

        if fused:
            xv = src_ref[...]
            r = lax.rsqrt(jnp.mean(xv * xv, axis=-1, keepdims=True) + EPS)
            h = (xv * r * g_ref[...]).astype(BF16)
            h_ref[...] = h
            ht_ref[...] = h.T
        else:
            h = src_ref[...]
        proj_ref[...] = _dot(h, wbuf[...], NN)

    row = lambda q, i, ch: (i, 0)
    inputs, in_specs = [src], [pl.BlockSpec((tm, d), row)]
    if fused:
        inputs.append(g_pre)
        in_specs.append(pl.BlockSpec((1, d), lambda q, i, ch: (0, 0)))
    if w_in is not None:
        inputs.append(w_in)
        in_specs.append(ANY)
    aliases = {}
    if proj_in is not None:
        aliases[len(inputs)] = 0
        inputs.append(proj_in)
        in_specs.append(ANY)
    out_specs = [pl.BlockSpec((tm, cols), lambda q, i, ch: (i, ch[first + q]))]
    out_shape = [jax.ShapeDtypeStruct((s, n_cols), F32)]
    if fused:
        assert n_shards == 1
        out_specs += [pl.BlockSpec((tm, d), row), pl.BlockSpec((d, tm), lambda q, i, ch: (0, i))]
        out_shape += [jax.ShapeDtypeStruct((s, d), BF16), jax.ShapeDtypeStruct((d, s), BF16)]
    for shard, kind in casts:
        assert n_shards == 1
        inputs.append(shard)
        if kind == 3:
            in_specs.append(pl.BlockSpec(shard.shape, lambda q, i, ch: (0, 0, 0)))
            out_specs.append(pl.BlockSpec(shard.shape, lambda q, i, ch: (0, ch[0], 0)))
            out_shape.append(jax.ShapeDtypeStruct((shard.shape[0], 4 * shard.shape[1], shard.shape[2]), BF16))
        else:
            block = (shard.shape[0] // n_steps, shard.shape[1])
            in_specs.append(pl.BlockSpec(block, row))
            out_specs.append(pl.BlockSpec(block, lambda q, i, ch: (ch[0] * n_steps + i, 0)))
            out_shape.append(jax.ShapeDtypeStruct((4 * shard.shape[0], shard.shape[1]), BF16))
    if memory:
        mem = memory[0]
        whole = lambda q, i, ch: (0, 0)
        inputs += list(memory)
        in_specs += [pl.BlockSpec(mem.shape, whole), pl.BlockSpec(memory[1].shape, whole),
                     pl.BlockSpec(memory[2].shape, whole, pipeline_mode=pl.Buffered(1))]
        out_specs += [pl.BlockSpec((mem.shape[0], WIDTH), whole)] * 2
        out_shape += [jax.ShapeDtypeStruct((mem.shape[0], WIDTH), BF16)] * 2
    return _call(body, name, (n_shards, s // tm), in_specs, out_specs, out_shape,
                 [pltpu.VMEM((d, cols), BF16), pltpu.SemaphoreType.DMA(())], inputs, rider, prefetch=chips,
                 aliases=aliases, rider_refs=True)


def _kv_body(mem_ref, g_ref, w_ref, k_ref, v_ref):
    mv = mem_ref[...]
    r = lax.rsqrt(jnp.mean(mv * mv, axis=-1, keepdims=True) + EPS)
    mem_n = (mv * r * g_ref[...]).astype(BF16)
    kv = _dot(mem_n, w_ref[...], NN)
    k_ref[...] = kv[:, :WIDTH].astype(BF16)
    v_ref[...] = kv[:, WIDTH:].astype(BF16)


def _kv_bwd(mem, g, w_kv, dk, dv):
    m, d = mem.shape
    n = w_kv.shape[1]
    col = 512

    def body(mem_ref, g_ref, w_ref, dk_ref, dv_ref, dw_ref, dg_ref):
        mv = mem_ref[...]
        r = lax.rsqrt(jnp.mean(mv * mv, axis=-1, keepdims=True) + EPS)
        mem_hat = mv * r
        mem_n = (mem_hat * g_ref[...]).astype(BF16)
        dkv = jnp.concatenate([dk_ref[...], dv_ref[...]], axis=1).astype(BF16)
        for j in range(n // col):
            dw_ref[:, j * col:(j + 1) * col] = _dot(mem_n, dkv[:, j * col:(j + 1) * col], TN).astype(BF16)
        dmem_n = _dot(dkv, w_ref[...], NT)
        dg_ref[...] = jnp.sum(dmem_n * mem_hat, axis=0, keepdims=True)

    return pl.pallas_call(
        body, name="kv_bwd",
        out_shape=(jax.ShapeDtypeStruct((d, n), BF16), jax.ShapeDtypeStruct((1, d), F32)),
        compiler_params=_params(0),
    )(mem, g, w_kv, dk, dv)


def _sigmoid(x):
    return 1.0 / (1.0 + jnp.exp(-x))


def _inv_counts(t0, t):
    pos = (t0 + lax.broadcasted_iota(jnp.int32, (t, 1), 0) + 1).astype(F32)
    return [1.0 / jnp.minimum(pos, float(w)) for w in POOL_WINDOWS]


def _window_sums(ext, t, backward):
    n = t + HALO
    parts = []
    for gi, w in enumerate(POOL_WINDOWS):
        s = ext[:, gi * GROUP:(gi + 1) * GROUP]
        k = 1
        while k < w:
            s = s + pltpu.roll(s, (n - k) if backward else k, axis=0)
            k *= 2
        parts.append(s[:t] if backward else s[HALO:])
    return parts


def _pool_fwd(xa, halo, inv, pool_w):
    t = xa.shape[0]
    sums = _window_sums(jnp.concatenate([halo, xa], axis=0), t, backward=False)
    d = jnp.concatenate([sums[gi] * inv[gi] - xa[:, gi * GROUP:(gi + 1) * GROUP] for gi in range(4)], axis=1)
    d = d.astype(BF16)
    y = jnp.concatenate([_dot(d[:, gi * GROUP:(gi + 1) * GROUP], pool_w[gi], NN) for gi in range(4)], axis=1)
    return d, y


def _layernorm_fwd(v):
    mu = jnp.mean(v, axis=-1, keepdims=True)
    xc = v - mu
    rstd = lax.rsqrt(jnp.mean(xc * xc, axis=-1, keepdims=True) + EPS)
    return xc * rstd, rstd


def _tril_mask(transposed):
    r = lax.broadcasted_iota(jnp.int32, (CHUNK, CHUNK), 0)
    c = lax.broadcasted_iota(jnp.int32, (CHUNK, CHUNK), 1)
    return (r <= c) if transposed else (r >= c)


def _sgu_mix(w_ref, vals, transposed):
    t = vals.shape[0]
    mask = _tril_mask(transposed)
    ws = [jnp.where(mask, w_ref[h], 0.0).astype(BF16) for h in range(N_SGU_HEADS)]
    rows = []
    for ci in range(t // CHUNK):
        blk = vals[ci * CHUNK:(ci + 1) * CHUNK]
        rows.append(jnp.concatenate(
            [_dot(ws[h], blk[:, h * CHUNK:(h + 1) * CHUNK], NN) for h in range(N_SGU_HEADS)], axis=1))
    return jnp.concatenate(rows, axis=0)


def _attn_fwd(q, k, v):
    ps, os_ = [], []
    for h in range(N_ATT_HEADS):
        sl = slice(h * ATT_DIM, (h + 1) * ATT_DIM)
        s = _dot(q[:, sl], k[:, sl], NT) * ATT_SCALE
        s = s - jnp.max(s, axis=-1, keepdims=True)
        e = jnp.exp(s)
        p = e * (1.0 / jnp.sum(e, axis=-1, keepdims=True))
        ps.append(p)
        os_.append(_dot(p.astype(BF16), v[:, sl], NN))
    return ps, jnp.concatenate(os_, axis=1)


def _rms_branch(y_pre):
    r = lax.rsqrt(jnp.mean(y_pre * y_pre, axis=-1, keepdims=True) + EPS)
    return y_pre * r, r


def _branch_specs(t, n_tiles, order):
    width_in = 7 * WIDTH
    tile = lambda i: order(i)
    per_halo = t // HALO
    const2 = lambda i: (0, 0)
    const3 = lambda i: (0, 0, 0)
    return [
        pl.BlockSpec((t, width_in), lambda i: (tile(i), 0)),
        pl.BlockSpec((HALO, WIDTH), lambda i: (jnp.maximum(tile(i) * per_halo - 1, 0), 0)),
        pl.BlockSpec((4, GROUP, GROUP), const3),
        pl.BlockSpec((1, WIDTH), const2),
        pl.BlockSpec((1, WIDTH), const2),
        pl.BlockSpec((1, WIDTH), const2),
        pl.BlockSpec((N_SGU_HEADS, CHUNK, CHUNK), const3),
        pl.BlockSpec((CHUNK, WIDTH), const2),
        pl.BlockSpec((MEM_ROWS, WIDTH), const2),
        pl.BlockSpec((MEM_ROWS, WIDTH), const2),
        pl.BlockSpec((1, 3 * WIDTH), const2),
    ]


MEM_ROWS = 256


def _branches_fwd(proj, pool_w, pool_scale, ln_g, ln_b, sgu_w, bias_full, k, v, branch_norm, t, rider=None):
    s = proj.shape[0]
    n_tiles = s // t

    def body(proj_ref, halo_ref, pw_ref, ps_ref, lg_ref, lb_ref, sw_ref, sb_ref, k_ref, v_ref, bn_ref, y_ref, yt_ref):
        i = pl.program_id(0)
        col = lambda j: proj_ref[:, j * WIDTH:(j + 1) * WIDTH]

        def put(branch, y_pre):
            sl = slice(branch * WIDTH, (branch + 1) * WIDTH)
            val = (_rms_branch(y_pre)[0] * bn[:, sl]).astype(BF16)
            y_ref[:, sl] = val
            yt_ref[sl, :] = val.T

        bn = bn_ref[...]
        halo = jnp.where(i > 0, halo_ref[...], 0.0)
        _, y_pool = _pool_fwd(col(0), halo, _inv_counts(i * t, t), pw_ref[...])
        ga = col(1)
        ya = y_pool * ps_ref[...] * (ga * _sigmoid(ga))
        put(0, ya)
        vhat, _ = _layernorm_fwd(col(3))
        vn = (vhat * lg_ref[...] + lb_ref[...]).astype(BF16)
        z = _sgu_mix(sw_ref, vn, transposed=False) + jnp.tile(sb_ref[...], (t // CHUNK, 1))
        gb = col(4)
        yb = col(2) * z * (gb * _sigmoid(gb))
        put(1, yb)
        _, o = _attn_fwd(col(5).astype(BF16), k_ref[...], v_ref[...])
        gc = col(6)
        yc = o * (gc * _sigmoid(gc))
        put(2, yc)

    return _call(body, "branches_fwd", (n_tiles,), _branch_specs(t, n_tiles, lambda i: i),
                 [pl.BlockSpec((t, 3 * WIDTH), lambda i: (i, 0)), pl.BlockSpec((3 * WIDTH, t), lambda i: (0, i))],
                 [jax.ShapeDtypeStruct((s, 3 * WIDTH), BF16), jax.ShapeDtypeStruct((3 * WIDTH, s), BF16)], [],
                 [proj, proj, pool_w, pool_scale, ln_g, ln_b, sgu_w, bias_full, k, v, branch_norm], rider)


def _branches_bwd(proj, dy, pool_w, pool_scale, ln_g, ln_b, sgu_w, sgu_wt, bias_full, k, v, branch_norm, t, rider=None):
    s = proj.shape[0]
    n_tiles = s // t
    n_chunks = t // CHUNK
    order = lambda i: n_tiles - 1 - i

    def body(proj_ref, halo_ref, pw_ref, ps_ref, lg_ref, lb_ref, sw_ref, sb_ref, k_ref, v_ref, bn_ref,
             swt_ref, dy_ref,
             dproj_ref, dpw_ref, dps_ref, dlg_ref, dlb_ref, dsw_ref, dsb_ref, dbn_ref, dk_ref, dv_ref,
             carry_ref, dbias_ref):
        step = pl.program_id(0)
        i = order(step)

        @pl.when(step == 0)
        def _():
            for ref in (dpw_ref, dps_ref, dlg_ref, dlb_ref, dsw_ref, dbn_ref, dk_ref, dv_ref, carry_ref, dbias_ref):
                ref[...] = jnp.zeros(ref.shape, ref.dtype)

        col = lambda j: proj_ref[:, j * WIDTH:(j + 1) * WIDTH]
        bn = bn_ref[...]

        def norm_bwd(y_pre, sl):
            yhat, r = _rms_branch(y_pre)
            dyv = dy_ref[:, sl].astype(F32)
            dbn_ref[:, sl] += jnp.sum(dyv * yhat, axis=0, keepdims=True)
            dyhat = dyv * bn[:, sl]
            return r * (dyhat - yhat * jnp.mean(dyhat * yhat, axis=-1, keepdims=True))

        def gate(gv):
            sg = _sigmoid(gv)
            return gv * sg, sg * (1.0 + gv * (1.0 - sg))

        inv = _inv_counts(i * t, t)
        halo = jnp.where(i > 0, halo_ref[...], 0.0)
        pw = pw_ref[...]
        d, y_pool = _pool_fwd(col(0), halo, inv, pw)
        scale = ps_ref[...]
        silu_a, dsilu_a = gate(col(1))
        pa = y_pool * scale
        dya = norm_bwd(pa * silu_a, slice(0, WIDTH))
        dproj_ref[:, WIDTH:2 * WIDTH] = (dya * pa * dsilu_a).astype(BF16)
        dpa = dya * silu_a
        dps_ref[...] += jnp.sum(dpa * y_pool, axis=0, keepdims=True)
        dy_pool = (dpa * scale).astype(BF16)
        dd_parts, ddc_parts = [], []
        for gi in range(4):
            sl = slice(gi * GROUP, (gi + 1) * GROUP)
            dpw_ref[gi] += _dot(d[:, sl], dy_pool[:, sl], TN)
            dd = _dot(dy_pool[:, sl], pw[gi], NT)
            dd_parts.append(dd)
            ddc_parts.append(dd * inv[gi])
        ddc = jnp.concatenate(ddc_parts, axis=1)
        sums = _window_sums(jnp.concatenate([ddc, carry_ref[...]], axis=0), t, backward=True)
        carry_ref[...] = ddc[:HALO]
        dproj_ref[:, 0:WIDTH] = jnp.concatenate([sums[gi] - dd_parts[gi] for gi in range(4)], axis=1).astype(BF16)

        vhat, rstd = _layernorm_fwd(col(3))
        lg = lg_ref[...]
        vn = (vhat * lg + lb_ref[...]).astype(BF16)
        z = _sgu_mix(sw_ref, vn, transposed=False) + jnp.tile(sb_ref[...], (n_chunks, 1))
        u = col(2)
        silu_b, dsilu_b = gate(col(4))
        uz = u * z
        dyb = norm_bwd(uz * silu_b, slice(WIDTH, 2 * WIDTH))
        dproj_ref[:, 4 * WIDTH:5 * WIDTH] = (dyb * uz * dsilu_b).astype(BF16)
        duz = dyb * silu_b
        dproj_ref[:, 2 * WIDTH:3 * WIDTH] = (duz * z).astype(BF16)
        dz = duz * u
        dz_b = dz.astype(BF16)
        for ci in range(n_chunks):
            rows = slice(ci * CHUNK, (ci + 1) * CHUNK)
            dbias_ref[...] += dz[rows]
            for h in range(N_SGU_HEADS):
                sl = slice(h * CHUNK, (h + 1) * CHUNK)
                dsw_ref[h] += _dot(dz_b[rows, sl], vn[rows, sl], NT)
        dvn = _sgu_mix(swt_ref, dz_b, transposed=True)
        dlg_ref[...] += jnp.sum(dvn * vhat, axis=0, keepdims=True)
        dlb_ref[...] += jnp.sum(dvn, axis=0, keepdims=True)
        dvhat = dvn * lg
        dvb = rstd * (dvhat - jnp.mean(dvhat, axis=-1, keepdims=True)
                      - vhat * jnp.mean(dvhat * vhat, axis=-1, keepdims=True))
        dproj_ref[:, 3 * WIDTH:4 * WIDTH] = dvb.astype(BF16)

        q = col(5).astype(BF16)
        kv_k, kv_v = k_ref[...], v_ref[...]
        ps, o = _attn_fwd(q, kv_k, kv_v)
        silu_c, dsilu_c = gate(col(6))
        dyc = norm_bwd(o * silu_c, slice(2 * WIDTH, 3 * WIDTH))
        dproj_ref[:, 6 * WIDTH:7 * WIDTH] = (dyc * o * dsilu_c).astype(BF16)
        do = (dyc * silu_c).astype(BF16)
        dq_parts = []
        for h in range(N_ATT_HEADS):
            sl = slice(h * ATT_DIM, (h + 1) * ATT_DIM)
            p = ps[h]
            dp = _dot(do[:, sl], kv_v[:, sl], NT)
            ds = (p * (dp - jnp.sum(p * dp, axis=-1, keepdims=True)) * ATT_SCALE).astype(BF16)
            dq_parts.append(_dot(ds, kv_k[:, sl], NN))
            dk_ref[:, sl] += _dot(ds, q[:, sl], TN)
            dv_ref[:, sl] += _dot(p.astype(BF16), do[:, sl], TN)
        dproj_ref[:, 5 * WIDTH:6 * WIDTH] = jnp.concatenate(dq_parts, axis=1).astype(BF16)

        @pl.when(step == n_tiles - 1)
        def _():
            keep = _tril_mask(transposed=False)
            for h in range(N_SGU_HEADS):
                dsw_ref[h] = jnp.where(keep, dsw_ref[h], 0.0)
            dsb_ref[...] = jnp.concatenate(
                [jnp.sum(dbias_ref[:, h * CHUNK:(h + 1) * CHUNK], axis=1, keepdims=True)
                 for h in range(N_SGU_HEADS)], axis=1)

    const2 = lambda i: (0, 0)
    const3 = lambda i: (0, 0, 0)
    out_shapes = (
        jax.ShapeDtypeStruct((s, 7 * WIDTH), BF16),
        jax.ShapeDtypeStruct((4, GROUP, GROUP), F32),
        jax.ShapeDtypeStruct((1, WIDTH), F32),
        jax.ShapeDtypeStruct((1, WIDTH), F32),
        jax.ShapeDtypeStruct((1, WIDTH), F32),
        jax.ShapeDtypeStruct((N_SGU_HEADS, CHUNK, CHUNK), F32),
        jax.ShapeDtypeStruct((CHUNK, N_SGU_HEADS), F32),
        jax.ShapeDtypeStruct((1, 3 * WIDTH), F32),
        jax.ShapeDtypeStruct((MEM_ROWS, WIDTH), F32),
        jax.ShapeDtypeStruct((MEM_ROWS, WIDTH), F32),
    )
    out_specs = (
        pl.BlockSpec((t, 7 * WIDTH), lambda i: (order(i), 0)),
        pl.BlockSpec((4, GROUP, GROUP), const3),
        pl.BlockSpec((1, WIDTH), const2),
        pl.BlockSpec((1, WIDTH), const2),
        pl.BlockSpec((1, WIDTH), const2),
        pl.BlockSpec((N_SGU_HEADS, CHUNK, CHUNK), const3),
        pl.BlockSpec((CHUNK, N_SGU_HEADS), const2),
        pl.BlockSpec((1, 3 * WIDTH), const2),
        pl.BlockSpec((MEM_ROWS, WIDTH), const2),
        pl.BlockSpec((MEM_ROWS, WIDTH), const2),
    )
    in_specs = _branch_specs(t, n_tiles, order) + [
        pl.BlockSpec((N_SGU_HEADS, CHUNK, CHUNK), const3),
        pl.BlockSpec((t, 3 * WIDTH), lambda i: (order(i), 0)),
    ]
    return _call(body, "branches_bwd", (n_tiles,), in_specs, out_specs, out_shapes,
                 [pltpu.VMEM((HALO, WIDTH), F32), pltpu.VMEM((CHUNK, WIDTH), F32)],
                 [proj, proj, pool_w, pool_scale, ln_g, ln_b, sgu_w, bias_full, k, v, branch_norm, sgu_wt, dy], rider)


def _out_loss(y, w_out, x, target, g_post, tm):
    s, d = x.shape
    e_w = y.shape[1]
    n_tiles = s // tm

    def body(y_ref, w_ref, x_ref, t_ref, g_ref, loss_ref, dz_ref, dout_ref, dy_ref, dg_ref, sq_ref):
        i = pl.program_id(0)

        @pl.when(i == 0)
        def _():
            sq_ref[...] = jnp.zeros(sq_ref.shape, F32)
            dg_ref[...] = jnp.zeros(dg_ref.shape, F32)

        w = w_ref[...]
        out = _dot(y_ref[...], w, NN)
        r = lax.rsqrt(jnp.mean(out * out, axis=-1, keepdims=True) + EPS)
        outn = out * r
        g = g_ref[...]
        err = (x_ref[...] + outn * g) - t_ref[...]
        sq_ref[...] += jnp.sum(err * err, axis=0, keepdims=True)
        dz = err * (1.0 / d)
        dz_ref[...] = dz
        dg_ref[...] += jnp.sum(dz * outn, axis=0, keepdims=True)
        doutn = dz * g
        dout = (r * (doutn - outn * jnp.mean(doutn * outn, axis=-1, keepdims=True))).astype(BF16)
        dout_ref[...] = dout
        dy_ref[...] = _dot(dout, w, NT).astype(BF16)

        @pl.when(i == n_tiles - 1)
        def _():
            loss_ref[...] = 0.5 * jnp.sum(sq_ref[...], axis=1, keepdims=True) * (1.0 / d)

    row = lambda i: (i, 0)
    const2 = lambda i: (0, 0)
    return pl.pallas_call(
        body, name="out_loss", grid=(n_tiles,),
        in_specs=[
            pl.BlockSpec((tm, e_w), row),
            pl.BlockSpec((e_w, d), const2, pipeline_mode=pl.Buffered(1)),
            pl.BlockSpec((tm, d), row),
            pl.BlockSpec((tm, d), row),
            pl.BlockSpec((1, d), const2),
        ],
        out_specs=(
            pl.BlockSpec((1, 1), const2),
            pl.BlockSpec((tm, d), row),
            pl.BlockSpec((tm, d), row),
            pl.BlockSpec((tm, e_w), row),
            pl.BlockSpec((1, d), const2),
        ),
        out_shape=(
            jax.ShapeDtypeStruct((1, 1), F32),
            jax.ShapeDtypeStruct((s, d), F32),
            jax.ShapeDtypeStruct((s, d), BF16),
            jax.ShapeDtypeStruct((s, e_w), BF16),
            jax.ShapeDtypeStruct((1, d), F32),
        ),
        scratch_shapes=[pltpu.VMEM((1, d), F32)],
        compiler_params=_params(1),
    )(y, w_out, x, target, g_post)


def _dx_call(dproj, w_in, x, dz, g_pre, tm, tk, rider=None):
    s, d = x.shape
    k_total = dproj.shape[1]
    nk = k_total // tk
    n_tiles = s // tm

    def body(dp_ref, w_ref, x_ref, dz_ref, g_ref, dx_ref, dg_ref, acc_ref):
        i, kk = pl.program_id(0), pl.program_id(1)
        part = lambda: _dot(dp_ref[...], w_ref[...], NT)

        @pl.when(kk == 0)
        def _():
            acc_ref[...] = part()

        @pl.when((kk > 0) & (kk < nk - 1))
        def _():
            acc_ref[...] += part()

        @pl.when((i == 0) & (kk == 0))
        def _():
            dg_ref[...] = jnp.zeros(dg_ref.shape, F32)

        @pl.when(kk == nk - 1)
        def _():
            dh = acc_ref[...] + part()
            xv = x_ref[...]
            r = lax.rsqrt(jnp.mean(xv * xv, axis=-1, keepdims=True) + EPS)
            xhat = xv * r
            dg_ref[...] += jnp.sum(dh * xhat, axis=0, keepdims=True)
            dxhat = dh * g_ref[...]
            dx_ref[...] = dz_ref[...] + r * (dxhat - xhat * jnp.mean(dxhat * xhat, axis=-1, keepdims=True))

    row = lambda i, kk: (i, 0)
    const2 = lambda i, kk: (0, 0)
    return _call(
        body, "dx", (n_tiles, nk),
        [
            pl.BlockSpec((tm, tk), lambda i, kk: (i, kk)),
            pl.BlockSpec((d, tk), lambda i, kk: (0, kk)),
            pl.BlockSpec((tm, d), row),
            pl.BlockSpec((tm, d), row),
            pl.BlockSpec((1, d), const2),
        ],
        [pl.BlockSpec((tm, d), row), pl.BlockSpec((1, d), const2)],
        [jax.ShapeDtypeStruct((s, d), F32), jax.ShapeDtypeStruct((1, d), F32)],
        [pltpu.VMEM((tm, d), F32)], [dproj, w_in, x, dz, g_pre], rider)


def _rows_tile(rows, cols, n_arrays, itemsize=4):
    budget = ELEMENTWISE_VMEM // (2 * n_arrays * cols * itemsize)
    if rows <= budget:
        return rows
    best = None
    for cand in range(16, rows + 1, 16):
        if rows % cand == 0 and cand <= max(budget, 16):
            best = cand
    return best if best is not None else rows


def _elementwise(fn, inputs, out_dtypes, name):
    rows, cols = inputs[0].shape
    tr = _rows_tile(rows, cols, len(inputs) + len(out_dtypes))
    n_in = len(inputs)

    def body(*refs):
        outs = fn(*[r[...] for r in refs[:n_in]])
        for o_ref, o in zip(refs[n_in:], outs):
            o_ref[...] = o.astype(o_ref.dtype)

    spec = pl.BlockSpec((tr, cols), lambda i: (i, 0))
    return pl.pallas_call(
        body, name=name, grid=(rows // tr,),
        in_specs=[spec] * n_in, out_specs=tuple([spec] * len(out_dtypes)),
        out_shape=tuple(jax.ShapeDtypeStruct((rows, cols), dt) for dt in out_dtypes),
        compiler_params=_params(1),
    )(*inputs)


def _blockwise(fn, pos, inputs, in_specs, out_shape, out_spec, grid, name):
    n_in = len(inputs)

    def body(pos_ref, *refs):
        o_ref = refs[n_in]
        (out,) = fn(*[r[...].reshape(o_ref.shape) for r in refs[:n_in]])
        o_ref[...] = out.astype(o_ref.dtype)

    return pl.pallas_call(
        body, name=name,
        grid_spec=pltpu.PrefetchScalarGridSpec(num_scalar_prefetch=1, grid=grid, in_specs=in_specs,
                                               out_specs=out_spec),
        out_shape=out_shape,
        compiler_params=_params(len(grid)),
    )(pos, *inputs)


def _cast_copy(x):
    return (x,)


def _pair_sum(mine, theirs):
    return ((mine.astype(F32) + theirs.astype(F32)),)


def _four_sum(own, t0, t1, t2):
    return ((((own.astype(F32) + t0.astype(F32)) + t1.astype(F32)) + t2.astype(F32)),)


def _adamw(w, g, m, v):
    m = ADAM_B1 * m + (1.0 - ADAM_B1) * g
    v = ADAM_B2 * v + (1.0 - ADAM_B2) * jnp.square(g)
    m_hat = m / (1.0 - ADAM_B1 ** ADAM_STEP)
    v_hat = v / (1.0 - ADAM_B2 ** ADAM_STEP)
    delta = -ADAM_LR * (m_hat / (jnp.sqrt(v_hat) + ADAM_EPS) + ADAM_WD * w)
    return delta, m, v


def _place():
    x, y, c = lax.axis_index("x"), lax.axis_index("y"), lax.axis_index("c")
    chips = [(1 - x, y), (x, 1 - y), (1 - x, 1 - y)]
    return x, y, c, chips


def _remote(src, dst, send_sem, recv_sem, to):
    return pltpu.make_async_remote_copy(src_ref=src, dst_ref=dst, send_sem=send_sem, recv_sem=recv_sem,
                                        device_id=to, device_id_type=MESH)


def _split(ref, plan):
    views = [ref]
    for axis, parts in plan:
        size = ref.shape[axis] // parts
        assert size * parts == ref.shape[axis]
        views = [v.at[tuple(pl.ds(q * size, size) if i == axis else slice(None) for i in range(len(ref.shape)))]
                 for v in views for q in range(parts)]
    return views


def _started(src, dst, send_sem, recv_sem, to):
    copy = _remote(src, dst, send_sem, recv_sem, to)
    copy.start()
    return copy


def _shard_half(kind, ref, chip, cc):
    if kind == 0:
        rows, cols = ref.shape[0] // 2, ref.shape[1] // 4
        return ref.at[pl.ds(cc * rows, rows), pl.ds(pl.multiple_of(chip * cols, LANES), cols)]
    if kind == 3:
        rows = ref.shape[1] // 8
        return ref.at[:, pl.ds(pl.multiple_of((2 * chip + cc) * rows, BF16_ROWS), rows), :]
    rows = ref.shape[0] // 8
    return ref.at[pl.ds(pl.multiple_of((2 * chip + cc) * rows, BF16_ROWS), rows), :]


def _relay_rider(full):
    kind = 0

    def quarter(ref, chip_no, cc, q):
        return _split(_shard_half(kind, ref, chip_no, cc), [(0, 2)])[q]

    def run(in_refs, full_refs, send_sems, recv_sems, start):
        (ref,) = full_refs
        x, y, c, chips = _place()
        sibling = (x, y, 1 - c)
        chip_no = [2 * ch[0] + ch[1] for ch in chips]
        if start:
            for p in (0, 1):
                held = quarter(ref, chip_no[1 - p], c, p)
                _remote(held, held, send_sems.at[p], recv_sems.at[p], (*chips[p], c)).start()
            return
        for p in (0, 1):
            landed = quarter(ref, chip_no[2], c, p)
            _remote(landed, landed, send_sems.at[p], recv_sems.at[p], (*chips[p], c)).wait_recv()
            _remote(landed, landed, send_sems.at[2], recv_sems.at[2], sibling).start()
        mine, theirs = _shard_half(kind, ref, chip_no[2], c), _shard_half(kind, ref, chip_no[2], 1 - c)
        _remote(mine, mine, send_sems.at[2], recv_sems.at[2], sibling).wait_send()
        _remote(theirs, theirs, send_sems.at[2], recv_sems.at[2], sibling).wait_recv()
        for p in (0, 1):
            held = quarter(ref, chip_no[1 - p], c, p)
            _remote(held, held, send_sems.at[p], recv_sems.at[p], (*chips[p], c)).wait_send()

    return _Rider([full], [jax.ShapeDtypeStruct(full.shape, full.dtype)], 3, run, aliases={0: 0})


def _riders(riders):
    def bounds(counts):
        ends = [sum(counts[:i + 1]) for i in range(len(counts))]
        return list(zip([0] + ends[:-1], ends))

    ins = bounds([len(r.inputs) for r in riders])
    outs = bounds([len(r.out_shapes) for r in riders])
    sems = bounds([r.n_sems for r in riders])

    class From:
        def __init__(self, sem_refs, base):
            self.sem_refs, self.base, self.at = sem_refs, base, self

        def __getitem__(self, k):
            return self.sem_refs.at[self.base + k]

    def run(in_refs, out_refs, send_sems, recv_sems, start):
        for r, (i0, i1), (o0, o1), (s0, _) in zip(riders, ins, outs, sems):
            r.run(in_refs[i0:i1], out_refs[o0:o1], From(send_sems, s0), From(recv_sems, s0), start)

    aliases = {}
    for r, (i0, _), (o0, _) in zip(riders, ins, outs):
        aliases.update({i0 + i: o0 + o for i, o in r.aliases.items()})
    return _Rider([a for r in riders for a in r.inputs], [o for r in riders for o in r.out_shapes],
                  sems[-1][1], run, aliases)


def _gather_rider(fulls, kinds, peers=(0, 1, 2)):
    n = len(fulls)
    full_half = _shard_half

    def run(in_refs, full_refs, send_sems, recv_sems, start):
        x, y, c, chips = _place()
        me = 2 * x + y
        sibling = (x, y, 1 - c)
        plans = [[(0, MAX_PARTS)], [(0, 2)], [(0, 2)], []]
        chips = [(p, chips[p]) for p in peers]
        across = lambda a, p, k: (3 * a + p) * MAX_PARTS + k
        onward = lambda a, p: 3 * n * MAX_PARTS + 3 * a + p

        def parts(a, chip_no, cc):
            return _split(full_half(kinds[a], full_refs[a], chip_no, cc), plans[kinds[a]])

        if start:
            for p, chip in chips:
                for a in range(n):
                    for k, mine in enumerate(parts(a, me, c)):
                        _remote(mine, mine, send_sems.at[across(a, p, k)], recv_sems.at[across(a, p, k)],
                                (*chip, c)).start()
            return
        for k in range(MAX_PARTS):
            for p, chip in chips:
                for a in range(n):
                    landed = parts(a, 2 * chip[0] + chip[1], c)
                    if k < len(landed):
                        _remote(landed[k], landed[k], send_sems.at[across(a, p, k)], recv_sems.at[across(a, p, k)],
                                (*chip, c)).wait_recv()
                        _remote(landed[k], landed[k], send_sems.at[onward(a, p)], recv_sems.at[onward(a, p)],
                                sibling).start()
        for p, chip in chips:
            them = 2 * chip[0] + chip[1]
            for a in range(n):
                passed = full_half(kinds[a], full_refs[a], them, 1 - c)
                _remote(passed, passed, send_sems.at[onward(a, p)], recv_sems.at[onward(a, p)], sibling).wait_recv()
                landed = full_half(kinds[a], full_refs[a], them, c)
                _remote(landed, landed, send_sems.at[onward(a, p)], recv_sems.at[onward(a, p)], sibling).wait_send()
                for k, mine in enumerate(parts(a, me, c)):
                    _remote(mine, mine, send_sems.at[across(a, p, k)], recv_sems.at[across(a, p, k)],
                            (*chip, c)).wait_send()

    return _Rider(fulls, [jax.ShapeDtypeStruct(f.shape, f.dtype) for f in fulls], 3 * n * (MAX_PARTS + 1), run,
                  aliases={a: a for a in range(n)})


def _exchange_rider(arrays, half_axes=None):
    n = len(arrays)
    half_axes = half_axes or [None] * n
    out_shapes = [jax.ShapeDtypeStruct(tuple(1 if i == ax else dim for i, dim in enumerate(g.shape)), g.dtype)
                  for g, ax in zip(arrays, half_axes)]

    def run(in_refs, out_refs, send_sems, recv_sems, start):
        x, y, c, _ = _place()
        sibling = (x, y, 1 - c)
        for a in range(n):
            src, ax = in_refs[a], half_axes[a]
            if ax is not None:
                src = src.at[tuple(pl.ds(1 - c, 1) if i == ax else slice(None) for i in range(len(src.shape)))]
            sems = (send_sems.at[a], recv_sems.at[a])
            if start:
                _started(src, out_refs[a], *sems, sibling)
            else:
                _remote(src, out_refs[a], *sems, sibling).wait()

    return _Rider(arrays, out_shapes, n, run)


def _scatter_rider(parts):
    n = len(parts)
    arrays = [p for p, _ in parts]

    def block_shape(p, ax):
        if ax == len(p.shape) - 1:
            return p.shape[:-1] + (p.shape[-1] // 4,)
        return tuple(1 if i == ax else dim for i, dim in enumerate(p.shape))

    out_shapes = [jax.ShapeDtypeStruct((3,) + block_shape(p, ax), p.dtype) for p, ax in parts]

    def block(ref, ax, chip):
        rank = len(ref.shape)
        if ax == rank - 1:
            cols = ref.shape[-1] // 4
            last = pl.ds(pl.multiple_of(chip * cols, LANES), cols)
            return ref.at[tuple([slice(None)] * (rank - 1) + [last])]
        return ref.at[tuple(pl.ds(chip, 1) if i == ax else slice(None) for i in range(rank))]

    def run(in_refs, out_refs, send_sems, recv_sems, start):
        x, y, c, chips = _place()
        for a in range(n):
            ax = parts[a][1]
            for p, chip in enumerate(chips):
                src, dst = block(in_refs[a], ax, 2 * chip[0] + chip[1]), out_refs[a].at[p]
                sems = (send_sems.at[3 * a + p], recv_sems.at[3 * a + p])
                if start:
                    _started(src, dst, *sems, (*chip, c))
                else:
                    _remote(src, dst, *sems, (*chip, c)).wait()

    return _Rider(arrays, out_shapes, 3 * n, run)


def _join_rider(joined):
    n = len(joined)
    arrays = [j for j, _ in joined]

    def run(in_refs, out_refs, send_sems, recv_sems, start):
        x, y, c, _ = _place()
        sibling = (x, y, 1 - c)

        def half(a, cc):
            rank = len(out_refs[a].shape)
            return out_refs[a].at[tuple(pl.ds(cc, 1) if i == joined[a][1] else slice(None) for i in range(rank))]

        for a in range(n):
            sems = (send_sems.at[a], recv_sems.at[a])
            if start:
                _started(half(a, c), half(a, c), *sems, sibling)
            else:
                _remote(half(a, c), half(a, c), *sems, sibling).wait_send()
                _remote(half(a, 1 - c), half(a, 1 - c), *sems, sibling).wait_recv()

    return _Rider(arrays, [jax.ShapeDtypeStruct(j.shape, j.dtype) for j in arrays], n, run,
                  aliases={a: a for a in range(n)})


def _allreduce_small(packed, rider):
    rows, lanes = packed.shape
    half = rows // 2
    r_in, r_out = len(rider.inputs), len(rider.out_shapes)

    def body(in_ref, *refs):
        rider_ins, out_ref, rider_outs = refs[:r_in], refs[r_in], refs[r_in + 1:r_in + 1 + r_out]
        pair_ref, gath_ref, send_sems, recv_sems, rider_send, rider_recv = refs[r_in + 1 + r_out:]
        x, y, c, chips = _place()
        me = 2 * x + y
        sibling = (x, y, 1 - c)
        mine = pl.ds(pl.multiple_of(c * half, 8), half)
        theirs = pl.ds(pl.multiple_of((1 - c) * half, 8), half)
        to_sib = _remote(in_ref.at[theirs], pair_ref, send_sems.at[0], recv_sems.at[0], sibling)
        to_sib.start()
        to_sib.wait()
        rider.run(rider_ins, rider_outs, rider_send, rider_recv, True)
        gath_ref[me] = in_ref[mine] + pair_ref[...]
        sends = [_remote(gath_ref.at[me], gath_ref.at[me], send_sems.at[1 + p], recv_sems.at[1 + p], (*chip, c))
                 for p, chip in enumerate(chips)]
        for cp in sends:
            cp.start()
        for p, chip in enumerate(chips):
            slot = gath_ref.at[2 * chip[0] + chip[1]]
            _remote(slot, slot, send_sems.at[1 + p], recv_sems.at[1 + p], (*chip, c)).wait_recv()
        for cp in sends:
            cp.wait_send()
        out_ref[mine] = ((gath_ref[0] + gath_ref[1]) + gath_ref[2]) + gath_ref[3]
        back = _remote(out_ref.at[mine], out_ref.at[mine], send_sems.at[4], recv_sems.at[4], sibling)
        back.start()
        back.wait_send()
        _remote(out_ref.at[theirs], out_ref.at[theirs], send_sems.at[4], recv_sems.at[4], sibling).wait_recv()
        rider.run(rider_ins, rider_outs, rider_send, rider_recv, False)

    vmem = pl.BlockSpec(memory_space=pltpu.VMEM)
    return pl.pallas_call(
        body, name="allreduce_small",
        in_specs=[vmem] + [ANY] * r_in, out_specs=(vmem,) + (ANY,) * r_out,
        out_shape=(jax.ShapeDtypeStruct((rows, lanes), F32),) + tuple(rider.out_shapes),
        scratch_shapes=[pltpu.VMEM((half, lanes), F32), pltpu.VMEM((4, half, lanes), F32),
                        pltpu.SemaphoreType.DMA((5,)), pltpu.SemaphoreType.DMA((5,)),
                        pltpu.SemaphoreType.DMA((rider.n_sems,)), pltpu.SemaphoreType.DMA((rider.n_sems,))],
        input_output_aliases={1 + i: 1 + o for i, o in rider.aliases.items()},
        compiler_params=pltpu.CompilerParams(has_side_effects=True, vmem_limit_bytes=32 * 1024 * 1024),
    )(packed, *rider.inputs)


SMALL = ("norm_pre", "pool_scale", "sgu_ln_g", "sgu_ln_b", "sgu_w", "sgu_b", "mem_norm", "branch_norm", "norm_post")
LARGE = ("w_in", "pool_w", "w_kv", "w_out")
ORDER = ("norm_pre", "w_in", "pool_w", "pool_scale", "sgu_ln_g", "sgu_ln_b", "sgu_w", "sgu_b", "mem_norm", "w_kv",
         "branch_norm", "w_out", "norm_post")


def _pack(arrays, extra=()):
    rows = [a.reshape(-1, 128) for a in arrays] + list(extra)
    pad = -sum(r.shape[0] for r in rows) % 16
    return jnp.concatenate(rows + ([jnp.zeros((pad, 128), F32)] if pad else []), axis=0)


def _unpack(packed, like):
    out, row = [], 0
    for a in like:
        rows = a.size // 128
        out.append(packed[row:row + rows].reshape(a.shape))
        row += rows
    return out


def kernel(x, mem, norm_pre, w_in, pool_w, pool_scale, sgu_ln_g, sgu_ln_b, sgu_w, sgu_b, mem_norm, w_kv, branch_norm, w_out, norm_post, loss_target, m_norm_pre, m_w_in, m_pool_w, m_pool_scale, m_sgu_ln_g, m_sgu_ln_b, m_sgu_w, m_sgu_b, m_mem_norm, m_w_kv, m_branch_norm, m_w_out, m_norm_post, v_norm_pre, v_w_in, v_pool_w, v_pool_scale, v_sgu_ln_g, v_sgu_ln_b, v_sgu_w, v_sgu_b, v_mem_norm, v_w_kv, v_branch_norm, v_w_out, v_norm_post):
    weights = dict(norm_pre=norm_pre, w_in=w_in, pool_w=pool_w, pool_scale=pool_scale, sgu_ln_g=sgu_ln_g,
                   sgu_ln_b=sgu_ln_b, sgu_w=sgu_w, sgu_b=sgu_b, mem_norm=mem_norm, w_kv=w_kv, branch_norm=branch_norm,
                   w_out=w_out, norm_post=norm_post)
    mom1 = dict(norm_pre=m_norm_pre, w_in=m_w_in, pool_w=m_pool_w, pool_scale=m_pool_scale, sgu_ln_g=m_sgu_ln_g,
                sgu_ln_b=m_sgu_ln_b, sgu_w=m_sgu_w, sgu_b=m_sgu_b, mem_norm=m_mem_norm, w_kv=m_w_kv,
                branch_norm=m_branch_norm, w_out=m_w_out, norm_post=m_norm_post)
    mom2 = dict(norm_pre=v_norm_pre, w_in=v_w_in, pool_w=v_pool_w, pool_scale=v_pool_scale, sgu_ln_g=v_sgu_ln_g,
                sgu_ln_b=v_sgu_ln_b, sgu_w=v_sgu_w, sgu_b=v_sgu_b, mem_norm=v_mem_norm, w_kv=v_w_kv,
                branch_norm=v_branch_norm, w_out=v_w_out, norm_post=v_norm_post)

    s, d = x.shape[1], x.shape[2]
    x2, mem2, tgt2 = x[0], mem[0], loss_target[0]
    t_branch = min(256, s)
    tm = min(512, s)

    core = lax.axis_index("c")
    chip = 2 * lax.axis_index("x") + lax.axis_index("y")
    pos = jnp.stack([core, chip]).astype(jnp.int32)
    n_in, n_kv, n_out = 4 * w_in.shape[2], 4 * w_kv.shape[1], 4 * w_out.shape[1]
    wi_rows, kv_rows, wo_rows = d // 8, n_kv // 8, n_out // 8

    kv_cols, pw_rows = w_kv.shape[2], GROUP // 8
    wi_own = _blockwise(_cast_copy, pos, [w_in[0]], [pl.BlockSpec((wi_rows, n_in // 4), lambda i, p: (i, 0))],
                        jax.ShapeDtypeStruct((d, n_in), BF16),
                        pl.BlockSpec((wi_rows, n_in // 4), lambda i, p: (i, p[1])), (8,), "place_w_in")

    x_pos, y_pos = lax.axis_index("x"), lax.axis_index("y")
    chips = jnp.stack([chip, 2 * (1 - x_pos) + y_pos, 2 * x_pos + 1 - y_pos,
                       2 * (1 - x_pos) + 1 - y_pos]).astype(jnp.int32)
    mem_g = mem_norm.reshape(1, d)
    proj, h, h_t, wkv_own, wo_own, pw_own, wi_full = _proj_piece(
        chips, 0, 1, x2, norm_pre, None, None, n_in, _gather_rider([wi_own], [0], peers=(0, 1)), tm, "proj_own",
        casts=[(w_kv[0], 1), (w_out[0], 2), (pool_w[0], 3)])
    proj, wi_full, wkv_full, pw_full = _proj_piece(
        chips, 1, 2, h, None, None, proj, n_in,
        _riders([_relay_rider(wi_full), _gather_rider([wkv_own, pw_own], [1, 3])]), tm, "proj_neighbours")
    proj, k_m, v_m, wo_part = _proj_piece(chips, 3, 1, h, None, wi_full, proj, n_in,
                                          _gather_rider([wo_own], [2], peers=(0, 1)), tm, "proj_diagonal",
                                          memory=(mem2, mem_g, wkv_full))
    bias_full = jnp.repeat(sgu_b[0].T, CHUNK, axis=1)
    y, y_t, wo_full = _branches_fwd(proj, pw_full, pool_scale, sgu_ln_g, sgu_ln_b, sgu_w[0], bias_full, k_m, v_m,
                                    branch_norm, t_branch, _gather_rider([wo_part], [2], peers=(2,)))
    loss_local, dz, dout, dy, g_norm_post = _out_loss(y, wo_full, x2, tgt2, norm_post, min(256, s))

    tk = min(1024, s)
    (dproj, g_pw, g_pool_scale, g_ln_g, g_ln_b, g_sgu_w, g_sgu_b_t, g_branch_norm, dk, dv) = _branches_bwd(
        proj, dy, pw_full, pool_scale, sgu_ln_g, sgu_ln_b, sgu_w[0], jnp.swapaxes(sgu_w[0], 1, 2), bias_full,
        k_m, v_m, branch_norm, t_branch)
    g_wkv, g_mem_norm = _kv_bwd(mem2, mem_g, wkv_full, dk, dv)
    g_wkv = g_wkv.reshape(4, 2, kv_rows, kv_cols)
    g_pw = g_pw.astype(BF16).reshape(4, 4, 2, pw_rows, GROUP)
    g_wo, gkv_from_sibling, gpw_from_sibling = _grad_rows(y_t, dout, pos, lambda i, p: i, n_out, n_out // 2, 1024, tk,
                                                          "grad_w_out", _exchange_rider([g_wkv, g_pw], [1, 2]))
    g_wo = g_wo.reshape(4, 2, wo_rows, d)
    ps_kv = _blockwise(_pair_sum, pos, [g_wkv, gkv_from_sibling],
                       [pl.BlockSpec((1, 1, kv_rows, kv_cols), lambda i, p: (i, p[0], 0, 0)),
                        pl.BlockSpec((1, 1, kv_rows, kv_cols), lambda i, p: (i, 0, 0, 0))],
                       jax.ShapeDtypeStruct((4, kv_rows, kv_cols), BF16),
                       pl.BlockSpec((1, kv_rows, kv_cols), lambda i, p: (i, 0, 0)), (4,), "pair_sum_w_kv")
    ps_pw = _blockwise(_pair_sum, pos, [g_pw, gpw_from_sibling],
                       [pl.BlockSpec((1, 4, 1, pw_rows, GROUP), lambda i, p: (i, 0, p[0], 0, 0)),
                        pl.BlockSpec((1, 4, 1, pw_rows, GROUP), lambda i, p: (i, 0, 0, 0, 0))],
                       jax.ShapeDtypeStruct((4, 4, pw_rows, GROUP), BF16),
                       pl.BlockSpec((1, 4, pw_rows, GROUP), lambda i, p: (i, 0, 0, 0)), (4,), "pair_sum_pool_w")
    gwi_theirs, landed_kv, landed_pw, gwo_from_sibling = _grad_rows(
        h_t, dproj, pos, lambda i, p: 1 - p[0], d // 2, d // 2, n_in // 4, tk, "grad_w_in_sibling_half",
        _riders([_scatter_rider([(ps_kv, 0), (ps_pw, 1)]), _exchange_rider([g_wo], [1])]))
    ps_wo = _blockwise(_pair_sum, pos, [g_wo, gwo_from_sibling],
                       [pl.BlockSpec((1, 1, wo_rows, d), lambda i, p: (i, p[0], 0, 0)),
                        pl.BlockSpec((1, 1, wo_rows, d), lambda i, p: (i, 0, 0, 0))],
                       jax.ShapeDtypeStruct((4, wo_rows, d), BF16),
                       pl.BlockSpec((1, wo_rows, d), lambda i, p: (i, 0, 0)), (4,), "pair_sum_w_out")
    gwi_mine, gwi_from_sibling, landed_wo = _grad_rows(
        h_t, dproj, pos, lambda i, p: p[0], d // 2, d // 2, n_in // 4, tk, "grad_w_in_own_half",
        _riders([_exchange_rider([gwi_theirs]), _scatter_rider([(ps_wo, 0)])]))
    ps_wi = _elementwise(_pair_sum, [gwi_mine, gwi_from_sibling], [BF16], "pair_sum_w_in")[0]
    grad_x, g_norm_pre, landed_wi = _dx_call(dproj, wi_full, x2, dz, norm_pre, tm, 1024,
                                             _scatter_rider([(ps_wi, 1)]))
    psum = [ps_wi, ps_kv, ps_wo, ps_pw]
    landed = [landed_wi, landed_kv, landed_wo, landed_pw]
    from_chip = lambda spec_shape, rank: [
        pl.BlockSpec(spec_shape, functools.partial(lambda i, p, q: (q, i) + (0,) * (rank - 2), q=q))
        for q in range(3)]
    join_rider = _join_rider([
        (_blockwise(_four_sum, pos, [psum[0]] + [landed[0]] * 3,
                    [pl.BlockSpec((256, n_in // 4), lambda i, p: (i, p[1]))] + from_chip((1, 256, n_in // 4), 3),
                    jax.ShapeDtypeStruct((2, d // 2, n_in // 4), F32),
                    pl.BlockSpec((1, 256, n_in // 4), lambda i, p: (p[0], i, 0)), (d // 2 // 256,), "chip_sum_w_in"),
         0),
        (_blockwise(_four_sum, pos, [psum[1]] + [landed[1]] * 3,
                    [pl.BlockSpec((1, kv_rows, kv_cols), lambda i, p: (p[1], 0, 0))]
                    + from_chip((1, 1, kv_rows, kv_cols), 4),
                    jax.ShapeDtypeStruct((2, kv_rows, kv_cols), F32),
                    pl.BlockSpec((1, kv_rows, kv_cols), lambda i, p: (p[0], 0, 0)), (1,), "chip_sum_w_kv"),
         0),
        (_blockwise(_four_sum, pos, [psum[2]] + [landed[2]] * 3,
                    [pl.BlockSpec((1, wo_rows, d), lambda i, p: (p[1], 0, 0))] + from_chip((1, 1, wo_rows, d), 4),
                    jax.ShapeDtypeStruct((2, wo_rows, d), F32),
                    pl.BlockSpec((1, wo_rows, d), lambda i, p: (p[0], 0, 0)), (1,), "chip_sum_w_out"),
         0),
        (_blockwise(_four_sum, pos, [psum[3]] + [landed[3]] * 3,
                    [pl.BlockSpec((4, 1, pw_rows, GROUP), lambda i, p: (0, p[1], 0, 0))]
                    + from_chip((1, 4, 1, pw_rows, GROUP), 5),
                    jax.ShapeDtypeStruct((4, 2, pw_rows, GROUP), F32),
                    pl.BlockSpec((4, 1, pw_rows, GROUP), lambda i, p: (0, p[0], 0, 0)), (1,), "chip_sum_pool_w"),
         1),
    ])

    small_local = dict(norm_pre=g_norm_pre, pool_scale=g_pool_scale, sgu_ln_g=g_ln_g, sgu_ln_b=g_ln_b,
                       sgu_w=g_sgu_w, sgu_b=g_sgu_b_t.T, mem_norm=g_mem_norm, branch_norm=g_branch_norm,
                       norm_post=g_norm_post)
    small_rows = sum(weights[n].size for n in SMALL) // 128
    small_sum, *joined = _allreduce_small(
        _pack([small_local[n] for n in SMALL], [jnp.pad(loss_local, ((0, 7), (0, 127)))]), join_rider)
    grads = {"w_in": joined[0].reshape(w_in.shape), "w_kv": joined[1].reshape(w_kv.shape),
             "w_out": joined[2].reshape(w_out.shape), "pool_w": joined[3].reshape(pool_w.shape)}
    for n, g in zip(SMALL, _unpack(small_sum, [weights[n] for n in SMALL])):
        grads[n] = g
    loss = small_sum[small_rows, 0]

    delta, new_m, new_v = {}, {}, {}
    packed = [small_sum if src is grads else _pack([src[n] for n in SMALL]) for src in (weights, grads, mom1, mom2)]
    outs = _elementwise(_adamw, packed, [F32, F32, F32], "adamw_small")
    for dst, o in zip((delta, new_m, new_v), outs):
        for n, a in zip(SMALL, _unpack(o, [weights[n] for n in SMALL])):
            dst[n] = a
    for n in LARGE:
        cols = weights[n].shape[-1]
        outs = _elementwise(lambda w, g, m, v: _adamw(w, g, m, v) + (g,),
                            [src[n].reshape(-1, cols) for src in (weights, grads, mom1, mom2)],
                            [F32, F32, F32, F32], "adamw_" + n)
        for dst, o in zip((delta, new_m, new_v, grads), outs):
            dst[n] = o.reshape(weights[n].shape)

    return (loss, grad_x[None], *[grads[n] for n in ORDER], *[delta[n] for n in ORDER],
            *[new_m[n] for n in ORDER], *[new_v[n] for n in ORDER])
```

```python
import functools

import jax
import jax.numpy as jnp
from jax import lax
from jax.experimental import pallas as pl
from jax.experimental.pallas import tpu as pltpu

F32 = jnp.float32
BF16 = jnp.bfloat16
EPS = 1e-6
MESH = pl.DeviceIdType.MESH
ANY = pl.BlockSpec(memory_space=pl.ANY)

POOL_WINDOWS = (2, 4, 8, 16)
GROUP = 256
HALO = 16
CHUNK = 128
N_SGU_HEADS = 8
N_ATT_HEADS = 4
ATT_DIM = 256
WIDTH = 1024
ATT_SCALE = 1.0 / 16.0

ADAM_LR = 0.001
ADAM_B1 = 0.9
ADAM_B2 = 0.999
ADAM_EPS = 1e-08
ADAM_WD = 0.01
ADAM_STEP = 10

VMEM_LIMIT = 60 * 1024 * 1024
ELEMENTWISE_VMEM = 24 * 1024 * 1024
LANES = 128
BF16_ROWS = 16
MAX_PARTS = 4


def _params(n_grid_axes, vmem=VMEM_LIMIT):
    return pltpu.CompilerParams(dimension_semantics=("arbitrary",) * n_grid_axes, vmem_limit_bytes=vmem)


def _dot(a, b, dims):
    return lax.dot_general(a, b, (dims, ((), ())), preferred_element_type=F32)


NN = ((1,), (0,))
NT = ((1,), (1,))
TN = ((0,), (0,))


class _Rider:
    def __init__(self, inputs, out_shapes, n_sems, run, aliases=None):
        self.inputs, self.out_shapes, self.n_sems, self.run = list(inputs), list(out_shapes), n_sems, run
        self.aliases = aliases or {}


def _call(body, name, grid, in_specs, out_specs, out_shape, scratch_shapes, inputs, rider=None, prefetch=None,
          aliases=None, rider_refs=False):
    n_in, n_out, n_scr = len(in_specs), len(out_specs), len(scratch_shapes)
    r_in = len(rider.inputs) if rider else 0
    r_out = len(rider.out_shapes) if rider else 0
    n_pre = 0 if prefetch is None else 1

    def whole_body(*refs):
        pre, refs = refs[:n_pre], refs[n_pre:]
        ins, rider_ins = refs[:n_in], refs[n_in:n_in + r_in]
        refs = refs[n_in + r_in:]
        outs, rider_outs = refs[:n_out], refs[n_out:n_out + r_out]
        refs = refs[n_out + r_out:]
        scratch, sems = refs[:n_scr], refs[n_scr:]
        extra = {"rider_outs": rider_outs} if rider_refs else {}
        if rider is None:
            body(*pre, *ins, *outs, *scratch, **extra)
            return
        ids = [pl.program_id(ax) for ax in range(len(grid))]
        first = functools.reduce(lambda p, q: p & q, [i == 0 for i in ids])
        last = functools.reduce(lambda p, q: p & q, [i == g - 1 for i, g in zip(ids, grid)])

        @pl.when(first)
        def _():
            rider.run(rider_ins, rider_outs, *sems, True)

        body(*pre, *ins, *outs, *scratch, **extra)

        @pl.when(last)
        def _():
            rider.run(rider_ins, rider_outs, *sems, False)

    io_aliases = {n_pre + i: o for i, o in (aliases or {}).items()}
    scratch_all = list(scratch_shapes)
    if rider:
        io_aliases.update({n_pre + n_in + i: n_out + o for i, o in rider.aliases.items()})
        scratch_all += [pltpu.SemaphoreType.DMA((rider.n_sems,)), pltpu.SemaphoreType.DMA((rider.n_sems,))]
    specs = dict(grid=grid, in_specs=list(in_specs) + [ANY] * r_in, out_specs=tuple(out_specs) + (ANY,) * r_out,
                 scratch_shapes=scratch_all)
    if n_pre:
        specs = dict(grid_spec=pltpu.PrefetchScalarGridSpec(num_scalar_prefetch=1, **specs))
    outs = pl.pallas_call(
        whole_body, name=name, **specs,
        out_shape=tuple(out_shape) + tuple(rider.out_shapes if rider else ()),
        input_output_aliases=io_aliases, compiler_params=_params(len(grid)),
    )(*([prefetch] if n_pre else []), *inputs, *(rider.inputs if rider else []))
    return tuple(outs)


def _grad_rows(a_t, b, pos, row_of, m, tm, tn, tk, name, rider=None):
    k, n = a_t.shape[1], b.shape[1]
    nk = k // tk
    out_dtype, dims, a = BF16, NN, a_t
    a_spec = pl.BlockSpec((tm, tk), lambda i, j, kk, p: (row_of(i, p), kk))
    b_spec = pl.BlockSpec((tk, tn), lambda i, j, kk, p: (kk, j))

    def body(pos_ref, a_ref, b_ref, o_ref, *acc):
        part = lambda: _dot(a_ref[...], b_ref[...], dims)
        if nk == 1:
            o_ref[...] = part().astype(out_dtype)
            return
        (acc_ref,) = acc
        kk = pl.program_id(2)

        @pl.when(kk == 0)
        def _():
            acc_ref[...] = part()

        @pl.when((kk > 0) & (kk < nk - 1))
        def _():
            acc_ref[...] += part()

        @pl.when(kk == nk - 1)
        def _():
            o_ref[...] = (acc_ref[...] + part()).astype(out_dtype)

    return _call(body, name, (m // tm, n // tn, nk), [a_spec, b_spec],
                 [pl.BlockSpec((tm, tn), lambda i, j, kk, p: (i, j))], [jax.ShapeDtypeStruct((m, n), out_dtype)],
                 [pltpu.VMEM((tm, tn), F32)] if nk > 1 else [], [a, b], rider, prefetch=pos)


def _proj_piece(chips, first, n_shards, src, g_pre, w_in, proj_in, n_cols, rider, tm, name, casts=(), memory=()):
    s, d = src.shape
    cols = n_cols // 4
    fused = g_pre is not None
    n_steps = s // tm

    def body(chips_ref, *refs, rider_outs=()):
        refs = list(refs)
        src_ref = refs.pop(0)
        g_ref = refs.pop(0) if fused else None
        w_ref = refs.pop(0) if w_in is not None else rider_outs[0]
        if proj_in is not None:
            refs.pop(0)
        shard_refs = [refs.pop(0) for _ in casts]
        memory_refs = [refs.pop(0) for _ in memory]
        proj_ref = refs.pop(0)
        h_ref, ht_ref = (refs.pop(0), refs.pop(0)) if fused else (None, None)
        for shard_ref in shard_refs:
            refs.pop(0)[...] = shard_ref[...].astype(BF16)
        kv_refs = [refs.pop(0) for _ in memory[:2]]
        wbuf, sem = refs
        q, i = pl.program_id(0), pl.program_id(1)

        @pl.when(i == 0)
        def _():
            at = pl.multiple_of(chips_ref[first + q] * cols, LANES)
            cp = pltpu.make_async_copy(w_ref.at[:, pl.ds(at, cols)], wbuf, sem)
            cp.start()
            cp.wait()

        if memory:
            @pl.when((q == 0) & (i == 0))
            def _():
                _kv_body(*memory_refs, *kv_refs)

        if fused:
            xv = src_ref[...]
            r = lax.rsqrt(jnp.mean(xv * xv, axis=-1, keepdims=True) + EPS)
            h = (xv * r * g_ref[...]).astype(BF16)
            h_ref[...] = h
            ht_ref[...] = h.T
        else:
            h = src_ref[...]
        proj_ref[...] = _dot(h, wbuf[...], NN)

    row = lambda q, i, ch: (i, 0)
    inputs, in_specs = [src], [pl.BlockSpec((tm, d), row)]
    if fused:
        inputs.append(g_pre)
        in_specs.append(pl.BlockSpec((1, d), lambda q, i, ch: (0, 0)))
    if w_in is not None:
        inputs.append(w_in)
        in_specs.append(ANY)
    aliases = {}
    if proj_in is not None:
        aliases[len(inputs)] = 0
        inputs.append(proj_in)
        in_specs.append(ANY)
    out_specs = [pl.BlockSpec((tm, cols), lambda q, i, ch: (i, ch[first + q]))]
    out_shape = [jax.ShapeDtypeStruct((s, n_cols), F32)]
    if fused:
        assert n_shards == 1
        out_specs += [pl.BlockSpec((tm, d), row), pl.BlockSpec((d, tm), lambda q, i, ch: (0, i))]
        out_shape += [jax.ShapeDtypeStruct((s, d), BF16), jax.ShapeDtypeStruct((d, s), BF16)]
    for shard, kind in casts:
        assert n_shards == 1
        inputs.append(shard)
        if kind == 3:
            in_specs.append(pl.BlockSpec(shard.shape, lambda q, i, ch: (0, 0, 0)))
            out_specs.append(pl.BlockSpec(shard.shape, lambda q, i, ch: (0, ch[0], 0)))
            out_shape.append(jax.ShapeDtypeStruct((shard.shape[0], 4 * shard.shape[1], shard.shape[2]), BF16))
        else:
            block = (shard.shape[0] // n_steps, shard.shape[1])
            in_specs.append(pl.BlockSpec(block, row))
            out_specs.append(pl.BlockSpec(block, lambda q, i, ch: (ch[0] * n_steps + i, 0)))
            out_shape.append(jax.ShapeDtypeStruct((4 * shard.shape[0], shard.shape[1]), BF16))
    if memory:
        mem = memory[0]
        whole = lambda q, i, ch: (0, 0)
        inputs += list(memory)
        in_specs += [pl.BlockSpec(mem.shape, whole), pl.BlockSpec(memory[1].shape, whole),
                     pl.BlockSpec(memory[2].shape, whole, pipeline_mode=pl.Buffered(1))]
        out_specs += [pl.BlockSpec((mem.shape[0], WIDTH), whole)] * 2
        out_shape += [jax.ShapeDtypeStruct((mem.shape[0], WIDTH), BF16)] * 2
    return _call(body, name, (n_shards, s // tm), in_specs, out_specs, out_shape,
                 [pltpu.VMEM((d, cols), BF16), pltpu.SemaphoreType.DMA(())], inputs, rider, prefetch=chips,
                 aliases=aliases, rider_refs=True)


def _kv_body(mem_ref, g_ref, w_ref, k_ref, v_ref):
    mv = mem_ref[...]
    r = lax.rsqrt(jnp.mean(mv * mv, axis=-1, keepdims=True) + EPS)
    mem_n = (mv * r * g_ref[...]).astype(BF16)
    kv = _dot(mem_n, w_ref[...], NN)
    k_ref[...] = kv[:, :WIDTH].astype(BF16)
    v_ref[...] = kv[:, WIDTH:].astype(BF16)


def _kv_bwd(mem, g, w_kv, dk, dv):
    m, d = mem.shape
    n = w_kv.shape[1]
    col = 512

    def body(mem_ref, g_ref, w_ref, dk_ref, dv_ref, dw_ref, dg_ref):
        mv = mem_ref[...]
        r = lax.rsqrt(jnp.mean(mv * mv, axis=-1, keepdims=True) + EPS)
        mem_hat = mv * r
        mem_n = (mem_hat * g_ref[...]).astype(BF16)
        dkv = jnp.concatenate([dk_ref[...], dv_ref[...]], axis=1).astype(BF16)
        for j in range(n // col):
            dw_ref[:, j * col:(j + 1) * col] = _dot(mem_n, dkv[:, j * col:(j + 1) * col], TN).astype(BF16)
        dmem_n = _dot(dkv, w_ref[...], NT)
        dg_ref[...] = jnp.sum(dmem_n * mem_hat, axis=0, keepdims=True)

    return pl.pallas_call(
        body, name="kv_bwd",
        out_shape=(jax.ShapeDtypeStruct((d, n), BF16), jax.ShapeDtypeStruct((1, d), F32)),
        compiler_params=_params(0),
    )(mem, g, w_kv, dk, dv)


def _sigmoid(x):
    return 1.0 / (1.0 + jnp.exp(-x))


def _inv_counts(t0, t):
    pos = (t0 + lax.broadcasted_iota(jnp.int32, (t, 1), 0) + 1).astype(F32)
    return [1.0 / jnp.minimum(pos, float(w)) for w in POOL_WINDOWS]


def _window_sums(ext, t, backward):
    n = t + HALO
    parts = []
    for gi, w in enumerate(POOL_WINDOWS):
        s = ext[:, gi * GROUP:(gi + 1) * GROUP]
        k = 1
        while k < w:
            s = s + pltpu.roll(s, (n - k) if backward else k, axis=0)
            k *= 2
        parts.append(s[:t] if backward else s[HALO:])
    return parts


def _pool_fwd(xa, halo, inv, pool_w):
    t = xa.shape[0]
    sums = _window_sums(jnp.concatenate([halo, xa], axis=0), t, backward=False)
    d = jnp.concatenate([sums[gi] * inv[gi] - xa[:, gi * GROUP:(gi + 1) * GROUP] for gi in range(4)], axis=1)
    d = d.astype(BF16)
    y = jnp.concatenate([_dot(d[:, gi * GROUP:(gi + 1) * GROUP], pool_w[gi], NN) for gi in range(4)], axis=1)
    return d, y


def _layernorm_fwd(v):
    mu = jnp.mean(v, axis=-1, keepdims=True)
    xc = v - mu
    rstd = lax.rsqrt(jnp.mean(xc * xc, axis=-1, keepdims=True) + EPS)
    return xc * rstd, rstd


def _tril_mask(transposed):
    r = lax.broadcasted_iota(jnp.int32, (CHUNK, CHUNK), 0)
    c = lax.broadcasted_iota(jnp.int32, (CHUNK, CHUNK), 1)
    return (r <= c) if transposed else (r >= c)


def _sgu_mix(w_ref, vals, transposed):
    t = vals.shape[0]
    mask = _tril_mask(transposed)
    ws = [jnp.where(mask, w_ref[h], 0.0).astype(BF16) for h in range(N_SGU_HEADS)]
    rows = []
    for ci in range(t // CHUNK):
        blk = vals[ci * CHUNK:(ci + 1) * CHUNK]
        rows.append(jnp.concatenate(
            [_dot(ws[h], blk[:, h * CHUNK:(h + 1) * CHUNK], NN) for h in range(N_SGU_HEADS)], axis=1))
    return jnp.concatenate(rows, axis=0)


def _attn_fwd(q, k, v):
    ps, os_ = [], []
    for h in range(N_ATT_HEADS):
        sl = slice(h * ATT_DIM, (h + 1) * ATT_DIM)
        s = _dot(q[:, sl], k[:, sl], NT) * ATT_SCALE
        s = s - jnp.max(s, axis=-1, keepdims=True)
        e = jnp.exp(s)
        p = e * (1.0 / jnp.sum(e, axis=-1, keepdims=True))
        ps.append(p)
        os_.append(_dot(p.astype(BF16), v[:, sl], NN))
    return ps, jnp.concatenate(os_, axis=1)


def _rms_branch(y_pre):
    r = lax.rsqrt(jnp.mean(y_pre * y_pre, axis=-1, keepdims=True) + EPS)
    return y_pre * r, r


def _branch_specs(t, n_tiles, order):
    width_in = 7 * WIDTH
    tile = lambda i: order(i)
    per_halo = t // HALO
    const2 = lambda i: (0, 0)
    const3 = lambda i: (0, 0, 0)
    return [
        pl.BlockSpec((t, width_in), lambda i: (tile(i), 0)),
        pl.BlockSpec((HALO, WIDTH), lambda i: (jnp.maximum(tile(i) * per_halo - 1, 0), 0)),
        pl.BlockSpec((4, GROUP, GROUP), const3),
        pl.BlockSpec((1, WIDTH), const2),
        pl.BlockSpec((1, WIDTH), const2),
        pl.BlockSpec((1, WIDTH), const2),
        pl.BlockSpec((N_SGU_HEADS, CHUNK, CHUNK), const3),
        pl.BlockSpec((CHUNK, WIDTH), const2),
        pl.BlockSpec((MEM_ROWS, WIDTH), const2),
        pl.BlockSpec((MEM_ROWS, WIDTH), const2),
        pl.BlockSpec((1, 3 * WIDTH), const2),
    ]


MEM_ROWS = 256


def _branches_fwd(proj, pool_w, pool_scale, ln_g, ln_b, sgu_w, bias_full, k, v, branch_norm, t, rider=None):
    s = proj.shape[0]
    n_tiles = s // t

    def body(proj_ref, halo_ref, pw_ref, ps_ref, lg_ref, lb_ref, sw_ref, sb_ref, k_ref, v_ref, bn_ref, y_ref, yt_ref):
        i = pl.program_id(0)
        col = lambda j: proj_ref[:, j * WIDTH:(j + 1) * WIDTH]

        def put(branch, y_pre):
            sl = slice(branch * WIDTH, (branch + 1) * WIDTH)
            val = (_rms_branch(y_pre)[0] * bn[:, sl]).astype(BF16)
            y_ref[:, sl] = val
            yt_ref[sl, :] = val.T

        bn = bn_ref[...]
        halo = jnp.where(i > 0, halo_ref[...], 0.0)
        _, y_pool = _pool_fwd(col(0), halo, _inv_counts(i * t, t), pw_ref[...])
        ga = col(1)
        ya = y_pool * ps_ref[...] * (ga * _sigmoid(ga))
        put(0, ya)
        vhat, _ = _layernorm_fwd(col(3))
        vn = (vhat * lg_ref[...] + lb_ref[...]).astype(BF16)
        z = _sgu_mix(sw_ref, vn, transposed=False) + jnp.tile(sb_ref[...], (t // CHUNK, 1))
        gb = col(4)
        yb = col(2) * z * (gb * _sigmoid(gb))
        put(1, yb)
        _, o = _attn_fwd(col(5).astype(BF16), k_ref[...], v_ref[...])
        gc = col(6)
        yc = o * (gc * _sigmoid(gc))
        put(2, yc)

    return _call(body, "branches_fwd", (n_tiles,), _branch_specs(t, n_tiles, lambda i: i),
                 [pl.BlockSpec((t, 3 * WIDTH), lambda i: (i, 0)), pl.BlockSpec((3 * WIDTH, t), lambda i: (0, i))],
                 [jax.ShapeDtypeStruct((s, 3 * WIDTH), BF16), jax.ShapeDtypeStruct((3 * WIDTH, s), BF16)], [],
                 [proj, proj, pool_w, pool_scale, ln_g, ln_b, sgu_w, bias_full, k, v, branch_norm], rider)


def _branches_bwd(proj, dy, pool_w, pool_scale, ln_g, ln_b, sgu_w, sgu_wt, bias_full, k, v, branch_norm, t, rider=None):
    s = proj.shape[0]
    n_tiles = s // t
    n_chunks = t // CHUNK
    order = lambda i: n_tiles - 1 - i

    def body(proj_ref, halo_ref, pw_ref, ps_ref, lg_ref, lb_ref, sw_ref, sb_ref, k_ref, v_ref, bn_ref,
             swt_ref, dy_ref,
             dproj_ref, dpw_ref, dps_ref, dlg_ref, dlb_ref, dsw_ref, dsb_ref, dbn_ref, dk_ref, dv_ref,
             carry_ref, dbias_ref):
        step = pl.program_id(0)
        i = order(step)

        @pl.when(step == 0)
        def _():
            for ref in (dpw_ref, dps_ref, dlg_ref, dlb_ref, dsw_ref, dbn_ref, dk_ref, dv_ref, carry_ref, dbias_ref):
                ref[...] = jnp.zeros(ref.shape, ref.dtype)

        col = lambda j: proj_ref[:, j * WIDTH:(j + 1) * WIDTH]
        bn = bn_ref[...]

        def norm_bwd(y_pre, sl):
            yhat, r = _rms_branch(y_pre)
            dyv = dy_ref[:, sl].astype(F32)
            dbn_ref[:, sl] += jnp.sum(dyv * yhat, axis=0, keepdims=True)
            dyhat = dyv * bn[:, sl]
            return r * (dyhat - yhat * jnp.mean(dyhat * yhat, axis=-1, keepdims=True))

        def gate(gv):
            sg = _sigmoid(gv)
            return gv * sg, sg * (1.0 + gv * (1.0 - sg))

        inv = _inv_counts(i * t, t)
        halo = jnp.where(i > 0, halo_ref[...], 0.0)
        pw = pw_ref[...]
        d, y_pool = _pool_fwd(col(0), halo, inv, pw)
        scale = ps_ref[...]
        silu_a, dsilu_a = gate(col(1))
        pa = y_pool * scale
        dya = norm_bwd(pa * silu_a, slice(0, WIDTH))
        dproj_ref[:, WIDTH:2 * WIDTH] = (dya * pa * dsilu_a).astype(BF16)
        dpa = dya * silu_a
        dps_ref[...] += jnp.sum(dpa * y_pool, axis=0, keepdims=True)
        dy_pool = (dpa * scale).astype(BF16)
        dd_parts, ddc_parts = [], []
        for gi in range(4):
            sl = slice(gi * GROUP, (gi + 1) * GROUP)
            dpw_ref[gi] += _dot(d[:, sl], dy_pool[:, sl], TN)
            dd = _dot(dy_pool[:, sl], pw[gi], NT)
            dd_parts.append(dd)
            ddc_parts.append(dd * inv[gi])
        ddc = jnp.concatenate(ddc_parts, axis=1)
        sums = _window_sums(jnp.concatenate([ddc, carry_ref[...]], axis=0), t, backward=True)
        carry_ref[...] = ddc[:HALO]
        dproj_ref[:, 0:WIDTH] = jnp.concatenate([sums[gi] - dd_parts[gi] for gi in range(4)], axis=1).astype(BF16)

        vhat, rstd = _layernorm_fwd(col(3))
        lg = lg_ref[...]
        vn = (vhat * lg + lb_ref[...]).astype(BF16)
        z = _sgu_mix(sw_ref, vn, transposed=False) + jnp.tile(sb_ref[...], (n_chunks, 1))
        u = col(2)
        silu_b, dsilu_b = gate(col(4))
        uz = u * z
        dyb = norm_bwd(uz * silu_b, slice(WIDTH, 2 * WIDTH))
        dproj_ref[:, 4 * WIDTH:5 * WIDTH] = (dyb * uz * dsilu_b).astype(BF16)
        duz = dyb * silu_b
        dproj_ref[:, 2 * WIDTH:3 * WIDTH] = (duz * z).astype(BF16)
        dz = duz * u
        dz_b = dz.astype(BF16)
        for ci in range(n_chunks):
            rows = slice(ci * CHUNK, (ci + 1) * CHUNK)
            dbias_ref[...] += dz[rows]
            for h in range(N_SGU_HEADS):
                sl = slice(h * CHUNK, (h + 1) * CHUNK)
                dsw_ref[h] += _dot(dz_b[rows, sl], vn[rows, sl], NT)
        dvn = _sgu_mix(swt_ref, dz_b, transposed=True)
        dlg_ref[...] += jnp.sum(dvn * vhat, axis=0, keepdims=True)
        dlb_ref[...] += jnp.sum(dvn, axis=0, keepdims=True)
        dvhat = dvn * lg
        dvb = rstd * (dvhat - jnp.mean(dvhat, axis=-1, keepdims=True)
                      - vhat * jnp.mean(dvhat * vhat, axis=-1, keepdims=True))
        dproj_ref[:, 3 * WIDTH:4 * WIDTH] = dvb.astype(BF16)

        q = col(5).astype(BF16)
        kv_k, kv_v = k_ref[...], v_ref[...]
        ps, o = _attn_fwd(q, kv_k, kv_v)
        silu_c, dsilu_c = gate(col(6))
        dyc = norm_bwd(o * silu_c, slice(2 * WIDTH, 3 * WIDTH))
        dproj_ref[:, 6 * WIDTH:7 * WIDTH] = (dyc * o * dsilu_c).astype(BF16)
        do = (dyc * silu_c).astype(BF16)
        dq_parts = []
        for h in range(N_ATT_HEADS):
            sl = slice(h * ATT_DIM, (h + 1) * ATT_DIM)
            p = ps[h]
            dp = _dot(do[:, sl], kv_v[:, sl], NT)
            ds = (p * (dp - jnp.sum(p * dp, axis=-1, keepdims=True)) * ATT_SCALE).astype(BF16)
            dq_parts.append(_dot(ds, kv_k[:, sl], NN))
            dk_ref[:, sl] += _dot(ds, q[:, sl], TN)
            dv_ref[:, sl] += _dot(p.astype(BF16), do[:, sl], TN)
        dproj_ref[:, 5 * WIDTH:6 * WIDTH] = jnp.concatenate(dq_parts, axis=1).astype(BF16)

        @pl.when(step == n_tiles - 1)
        def _():
            keep = _tril_mask(transposed=False)
            for h in range(N_SGU_HEADS):
                dsw_ref[h] = jnp.where(keep, dsw_ref[h], 0.0)
            dsb_ref[...] = jnp.concatenate(
                [jnp.sum(dbias_ref[:, h * CHUNK:(h + 1) * CHUNK], axis=1, keepdims=True)
                 for h in range(N_SGU_HEADS)], axis=1)

    const2 = lambda i: (0, 0)
    const3 = lambda i: (0, 0, 0)
    out_shapes = (
        jax.ShapeDtypeStruct((s, 7 * WIDTH), BF16),
        jax.ShapeDtypeStruct((4, GROUP, GROUP), F32),
        jax.ShapeDtypeStruct((1, WIDTH), F32),
        jax.ShapeDtypeStruct((1, WIDTH), F32),
        jax.ShapeDtypeStruct((1, WIDTH), F32),
        jax.ShapeDtypeStruct((N_SGU_HEADS, CHUNK, CHUNK), F32),
        jax.ShapeDtypeStruct((CHUNK, N_SGU_HEADS), F32),
        jax.ShapeDtypeStruct((1, 3 * WIDTH), F32),
        jax.ShapeDtypeStruct((MEM_ROWS, WIDTH), F32),
        jax.ShapeDtypeStruct((MEM_ROWS, WIDTH), F32),
    )
    out_specs = (
        pl.BlockSpec((t, 7 * WIDTH), lambda i: (order(i), 0)),
        pl.BlockSpec((4, GROUP, GROUP), const3),
        pl.BlockSpec((1, WIDTH), const2),
        pl.BlockSpec((1, WIDTH), const2),
        pl.BlockSpec((1, WIDTH), const2),
        pl.BlockSpec((N_SGU_HEADS, CHUNK, CHUNK), const3),
        pl.BlockSpec((CHUNK, N_SGU_HEADS), const2),
        pl.BlockSpec((1, 3 * WIDTH), const2),
        pl.BlockSpec((MEM_ROWS, WIDTH), const2),
        pl.BlockSpec((MEM_ROWS, WIDTH), const2),
    )
    in_specs = _branch_specs(t, n_tiles, order) + [
        pl.BlockSpec((N_SGU_HEADS, CHUNK, CHUNK), const3),
        pl.BlockSpec((t, 3 * WIDTH), lambda i: (order(i), 0)),
    ]
    return _call(body, "branches_bwd", (n_tiles,), in_specs, out_specs, out_shapes,
                 [pltpu.VMEM((HALO, WIDTH), F32), pltpu.VMEM((CHUNK, WIDTH), F32)],
                 [proj, proj, pool_w, pool_scale, ln_g, ln_b, sgu_w, bias_full, k, v, branch_norm, sgu_wt, dy], rider)


def _out_loss(y, w_out, x, target, g_post, tm):
    s, d = x.shape
    e_w = y.shape[1]
    n_tiles = s // tm

    def body(y_ref, w_ref, x_ref, t_ref, g_ref, loss_ref, dz_ref, dout_ref, dy_ref, dg_ref, sq_ref):
        i = pl.program_id(0)

        @pl.when(i == 0)
        def _():
            sq_ref[...] = jnp.zeros(sq_ref.shape, F32)
            dg_ref[...] = jnp.zeros(dg_ref.shape, F32)

        w = w_ref[...]
        out = _dot(y_ref[...], w, NN)
        r = lax.rsqrt(jnp.mean(out * out, axis=-1, keepdims=True) + EPS)
        outn = out * r
        g = g_ref[...]
        err = (x_ref[...] + outn * g) - t_ref[...]
        sq_ref[...] += jnp.sum(err * err, axis=0, keepdims=True)
        dz = err * (1.0 / d)
        dz_ref[...] = dz
        dg_ref[...] += jnp.sum(dz * outn, axis=0, keepdims=True)
        doutn = dz * g
        dout = (r * (doutn - outn * jnp.mean(doutn * outn, axis=-1, keepdims=True))).astype(BF16)
        dout_ref[...] = dout
        dy_ref[...] = _dot(dout, w, NT).astype(BF16)

        @pl.when(i == n_tiles - 1)
        def _():
            loss_ref[...] = 0.5 * jnp.sum(sq_ref[...], axis=1, keepdims=True) * (1.0 / d)

    row = lambda i: (i, 0)
    const2 = lambda i: (0, 0)
    return pl.pallas_call(
        body, name="out_loss", grid=(n_tiles,),
        in_specs=[
            pl.BlockSpec((tm, e_w), row),
            pl.BlockSpec((e_w, d), const2, pipeline_mode=pl.Buffered(1)),
            pl.BlockSpec((tm, d), row),
            pl.BlockSpec((tm, d), row),
            pl.BlockSpec((1, d), const2),
        ],
        out_specs=(
            pl.BlockSpec((1, 1), const2),
            pl.BlockSpec((tm, d), row),
            pl.BlockSpec((tm, d), row),
            pl.BlockSpec((tm, e_w), row),
            pl.BlockSpec((1, d), const2),
        ),
        out_shape=(
            jax.ShapeDtypeStruct((1, 1), F32),
            jax.ShapeDtypeStruct((s, d), F32),
            jax.ShapeDtypeStruct((s, d), BF16),
            jax.ShapeDtypeStruct((s, e_w), BF16),
            jax.ShapeDtypeStruct((1, d), F32),
        ),
        scratch_shapes=[pltpu.VMEM((1, d), F32)],
        compiler_params=_params(1),
    )(y, w_out, x, target, g_post)


def _dx_call(dproj, w_in, x, dz, g_pre, tm, tk, rider=None):
    s, d = x.shape
    k_total = dproj.shape[1]
    nk = k_total // tk
    n_tiles = s // tm

    def body(dp_ref, w_ref, x_ref, dz_ref, g_ref, dx_ref, dg_ref, acc_ref):
        i, kk = pl.program_id(0), pl.program_id(1)
        part = lambda: _dot(dp_ref[...], w_ref[...], NT)

        @pl.when(kk == 0)
        def _():
            acc_ref[...] = part()

        @pl.when((kk > 0) & (kk < nk - 1))
        def _():
            acc_ref[...] += part()

        @pl.when((i == 0) & (kk == 0))
        def _():
            dg_ref[...] = jnp.zeros(dg_ref.shape, F32)

        @pl.when(kk == nk - 1)
        def _():
            dh = acc_ref[...] + part()
            xv = x_ref[...]
            r = lax.rsqrt(jnp.mean(xv * xv, axis=-1, keepdims=True) + EPS)
            xhat = xv * r
            dg_ref[...] += jnp.sum(dh * xhat, axis=0, keepdims=True)
            dxhat = dh * g_ref[...]
            dx_ref[...] = dz_ref[...] + r * (dxhat - xhat * jnp.mean(dxhat * xhat, axis=-1, keepdims=True))

    row = lambda i, kk: (i, 0)
    const2 = lambda i, kk: (0, 0)
    return _call(
        body, "dx", (n_tiles, nk),
        [
            pl.BlockSpec((tm, tk), lambda i, kk: (i, kk)),
            pl.BlockSpec((d, tk), lambda i, kk: (0, kk)),
            pl.BlockSpec((tm, d), row),
            pl.BlockSpec((tm, d), row),
            pl.BlockSpec((1, d), const2),
        ],
        [pl.BlockSpec((tm, d), row), pl.BlockSpec((1, d), const2)],
        [jax.ShapeDtypeStruct((s, d), F32), jax.ShapeDtypeStruct((1, d), F32)],
        [pltpu.VMEM((tm, d), F32)], [dproj, w_in, x, dz, g_pre], rider)


def _rows_tile(rows, cols, n_arrays, itemsize=4):
    budget = ELEMENTWISE_VMEM // (2 * n_arrays * cols * itemsize)
    if rows <= budget:
        return rows
    best = None
    for cand in range(16, rows + 1, 16):
        if rows % cand == 0 and cand <= max(budget, 16):
            best = cand
    return best if best is not None else rows


def _elementwise(fn, inputs, out_dtypes, name):
    rows, cols = inputs[0].shape
    tr = _rows_tile(rows, cols, len(inputs) + len(out_dtypes))
    n_in = len(inputs)

    def body(*refs):
        outs = fn(*[r[...] for r in refs[:n_in]])
        for o_ref, o in zip(refs[n_in:], outs):
            o_ref[...] = o.astype(o_ref.dtype)

    spec = pl.BlockSpec((tr, cols), lambda i: (i, 0))
    return pl.pallas_call(
        body, name=name, grid=(rows // tr,),
        in_specs=[spec] * n_in, out_specs=tuple([spec] * len(out_dtypes)),
        out_shape=tuple(jax.ShapeDtypeStruct((rows, cols), dt) for dt in out_dtypes),
        compiler_params=_params(1),
    )(*inputs)


def _blockwise(fn, pos, inputs, in_specs, out_shape, out_spec, grid, name):
    n_in = len(inputs)

    def body(pos_ref, *refs):
        o_ref = refs[n_in]
        (out,) = fn(*[r[...].reshape(o_ref.shape) for r in refs[:n_in]])
        o_ref[...] = out.astype(o_ref.dtype)

    return pl.pallas_call(
        body, name=name,
        grid_spec=pltpu.PrefetchScalarGridSpec(num_scalar_prefetch=1, grid=grid, in_specs=in_specs,
                                               out_specs=out_spec),
        out_shape=out_shape,
        compiler_params=_params(len(grid)),
    )(pos, *inputs)


def _cast_copy(x):
    return (x,)


def _pair_sum(mine, theirs):
    return ((mine.astype(F32) + theirs.astype(F32)),)


def _four_sum(own, t0, t1, t2):
    return ((((own.astype(F32) + t0.astype(F32)) + t1.astype(F32)) + t2.astype(F32)),)


def _adamw(w, g, m, v):
    m = ADAM_B1 * m + (1.0 - ADAM_B1) * g
    v = ADAM_B2 * v + (1.0 - ADAM_B2) * jnp.square(g)
    m_hat = m / (1.0 - ADAM_B1 ** ADAM_STEP)
    v_hat = v / (1.0 - ADAM_B2 ** ADAM_STEP)
    delta = -ADAM_LR * (m_hat / (jnp.sqrt(v_hat) + ADAM_EPS) + ADAM_WD * w)
    return delta, m, v


def _place():
    x, y, c = lax.axis_index("x"), lax.axis_index("y"), lax.axis_index("c")
    chips = [(1 - x, y), (x, 1 - y), (1 - x, 1 - y)]
    return x, y, c, chips


def _remote(src, dst, send_sem, recv_sem, to):
    return pltpu.make_async_remote_copy(src_ref=src, dst_ref=dst, send_sem=send_sem, recv_sem=recv_sem,
                                        device_id=to, device_id_type=MESH)


def _split(ref, plan):
    views = [ref]
    for axis, parts in plan:
        size = ref.shape[axis] // parts
        assert size * parts == ref.shape[axis]
        views = [v.at[tuple(pl.ds(q * size, size) if i == axis else slice(None) for i in range(len(ref.shape)))]
                 for v in views for q in range(parts)]
    return views


def _started(src, dst, send_sem, recv_sem, to):
    copy = _remote(src, dst, send_sem, recv_sem, to)
    copy.start()
    return copy


def _shard_half(kind, ref, chip, cc):
    if kind == 0:
        rows, cols = ref.shape[0] // 2, ref.shape[1] // 4
        return ref.at[pl.ds(cc * rows, rows), pl.ds(pl.multiple_of(chip * cols, LANES), cols)]
    if kind == 3:
        rows = ref.shape[1] // 8
        return ref.at[:, pl.ds(pl.multiple_of((2 * chip + cc) * rows, BF16_ROWS), rows), :]
    rows = ref.shape[0] // 8
    return ref.at[pl.ds(pl.multiple_of((2 * chip + cc) * rows, BF16_ROWS), rows), :]


def _relay_rider(full):
    kind = 0

    def quarter(ref, chip_no, cc, q):
        return _split(_shard_half(kind, ref, chip_no, cc), [(0, 2)])[q]

    def run(in_refs, full_refs, send_sems, recv_sems, start):
        (ref,) = full_refs
        x, y, c, chips = _place()
        sibling = (x, y, 1 - c)
        chip_no = [2 * ch[0] + ch[1] for ch in chips]
        if start:
            for p in (0, 1):
                held = quarter(ref, chip_no[1 - p], c, p)
                _remote(held, held, send_sems.at[p], recv_sems.at[p], (*chips[p], c)).start()
            return
        for p in (0, 1):
            landed = quarter(ref, chip_no[2], c, p)
            _remote(landed, landed, send_sems.at[p], recv_sems.at[p], (*chips[p], c)).wait_recv()
            _remote(landed, landed, send_sems.at[2], recv_sems.at[2], sibling).start()
        mine, theirs = _shard_half(kind, ref, chip_no[2], c), _shard_half(kind, ref, chip_no[2], 1 - c)
        _remote(mine, mine, send_sems.at[2], recv_sems.at[2], sibling).wait_send()
        _remote(theirs, theirs, send_sems.at[2], recv_sems.at[2], sibling).wait_recv()
        for p in (0, 1):
            held = quarter(ref, chip_no[1 - p], c, p)
            _remote(held, held, send_sems.at[p], recv_sems.at[p], (*chips[p], c)).wait_send()

    return _Rider([full], [jax.ShapeDtypeStruct(full.shape, full.dtype)], 3, run, aliases={0: 0})


def _riders(riders):
    def bounds(counts):
        ends = [sum(counts[:i + 1]) for i in range(len(counts))]
        return list(zip([0] + ends[:-1], ends))

    ins = bounds([len(r.inputs) for r in riders])
    outs = bounds([len(r.out_shapes) for r in riders])
    sems = bounds([r.n_sems for r in riders])

    class From:
        def __init__(self, sem_refs, base):
            self.sem_refs, self.base, self.at = sem_refs, base, self

        def __getitem__(self, k):
            return self.sem_refs.at[self.base + k]

    def run(in_refs, out_refs, send_sems, recv_sems, start):
        for r, (i0, i1), (o0, o1), (s0, _) in zip(riders, ins, outs, sems):
            r.run(in_refs[i0:i1], out_refs[o0:o1], From(send_sems, s0), From(recv_sems, s0), start)

    aliases = {}
    for r, (i0, _), (o0, _) in zip(riders, ins, outs):
        aliases.update({i0 + i: o0 + o for i, o in r.aliases.items()})
    return _Rider([a for r in riders for a in r.inputs], [o for r in riders for o in r.out_shapes],
                  sems[-1][1], run, aliases)


def _gather_rider(fulls, kinds, peers=(0, 1, 2)):
    n = len(fulls)
    full_half = _shard_half

    def run(in_refs, full_refs, send_sems, recv_sems, start):
        x, y, c, chips = _place()
        me = 2 * x + y
        sibling = (x, y, 1 - c)
        plans = [[(0, MAX_PARTS)], [(0, 2)], [(0, 2)], []]
        chips = [(p, chips[p]) for p in peers]
        across = lambda a, p, k: (3 * a + p) * MAX_PARTS + k
        onward = lambda a, p: 3 * n * MAX_PARTS + 3 * a + p

        def parts(a, chip_no, cc):
            return _split(full_half(kinds[a], full_refs[a], chip_no, cc), plans[kinds[a]])

        if start:
            for p, chip in chips:
                for a in range(n):
                    for k, mine in enumerate(parts(a, me, c)):
                        _remote(mine, mine, send_sems.at[across(a, p, k)], recv_sems.at[across(a, p, k)],
                                (*chip, c)).start()
            return
        for k in range(MAX_PARTS):
            for p, chip in chips:
                for a in range(n):
                    landed = parts(a, 2 * chip[0] + chip[1], c)
                    if k < len(landed):
                        _remote(landed[k], landed[k], send_sems.at[across(a, p, k)], recv_sems.at[across(a, p, k)],
                                (*chip, c)).wait_recv()
                        _remote(landed[k], landed[k], send_sems.at[onward(a, p)], recv_sems.at[onward(a, p)],
                                sibling).start()
        for p, chip in chips:
            them = 2 * chip[0] + chip[1]
            for a in range(n):
                passed = full_half(kinds[a], full_refs[a], them, 1 - c)
                _remote(passed, passed, send_sems.at[onward(a, p)], recv_sems.at[onward(a, p)], sibling).wait_recv()
                landed = full_half(kinds[a], full_refs[a], them, c)
                _remote(landed, landed, send_sems.at[onward(a, p)], recv_sems.at[onward(a, p)], sibling).wait_send()
                for k, mine in enumerate(parts(a, me, c)):
                    _remote(mine, mine, send_sems.at[across(a, p, k)], recv_sems.at[across(a, p, k)],
                            (*chip, c)).wait_send()

    return _Rider(fulls, [jax.ShapeDtypeStruct(f.shape, f.dtype) for f in fulls], 3 * n * (MAX_PARTS + 1), run,
                  aliases={a: a for a in range(n)})


def _exchange_rider(arrays, half_axes=None):
    n = len(arrays)
    half_axes = half_axes or [None] * n
    out_shapes = [jax.ShapeDtypeStruct(tuple(1 if i == ax else dim for i, dim in enumerate(g.shape)), g.dtype)
                  for g, ax in zip(arrays, half_axes)]

    def run(in_refs, out_refs, send_sems, recv_sems, start):
        x, y, c, _ = _place()
        sibling = (x, y, 1 - c)
        for a in range(n):
            src, ax = in_refs[a], half_axes[a]
            if ax is not None:
                src = src.at[tuple(pl.ds(1 - c, 1) if i == ax else slice(None) for i in range(len(src.shape)))]
            sems = (send_sems.at[a], recv_sems.at[a])
            if start:
                _started(src, out_refs[a], *sems, sibling)
            else:
                _remote(src, out_refs[a], *sems, sibling).wait()

    return _Rider(arrays, out_shapes, n, run)


def _scatter_rider(parts):
    n = len(parts)
    arrays = [p for p, _ in parts]

    def block_shape(p, ax):
        if ax == len(p.shape) - 1:
            return p.shape[:-1] + (p.shape[-1] // 4,)
        return tuple(1 if i == ax else dim for i, dim in enumerate(p.shape))

    out_shapes = [jax.ShapeDtypeStruct((3,) + block_shape(p, ax), p.dtype) for p, ax in parts]

    def block(ref, ax, chip):
        rank = len(ref.shape)
        if ax == rank - 1:
            cols = ref.shape[-1] // 4
            last = pl.ds(pl.multiple_of(chip * cols, LANES), cols)
            return ref.at[tuple([slice(None)] * (rank - 1) + [last])]
        return ref.at[tuple(pl.ds(chip, 1) if i == ax else slice(None) for i in range(rank))]

    def run(in_refs, out_refs, send_sems, recv_sems, start):
        x, y, c, chips = _place()
        for a in range(n):
            ax = parts[a][1]
            for p, chip in enumerate(chips):
                src, dst = block(in_refs[a], ax, 2 * chip[0] + chip[1]), out_refs[a].at[p]
                sems = (send_sems.at[3 * a + p], recv_sems.at[3 * a + p])
                if start:
                    _started(src, dst, *sems, (*chip, c))
                else:
                    _remote(src, dst, *sems, (*chip, c)).wait()

    return _Rider(arrays, out_shapes, 3 * n, run)


def _join_rider(joined):
    n = len(joined)
    arrays = [j for j, _ in joined]

    def run(in_refs, out_refs, send_sems, recv_sems, start):
        x, y, c, _ = _place()
        sibling = (x, y, 1 - c)

        def half(a, cc):
            rank = len(out_refs[a].shape)
            return out_refs[a].at[tuple(pl.ds(cc, 1) if i == joined[a][1] else slice(None) for i in range(rank))]

        for a in range(n):
            sems = (send_sems.at[a], recv_sems.at[a])
            if start:
                _started(half(a, c), half(a, c), *sems, sibling)
            else:
                _remote(half(a, c), half(a, c), *sems, sibling).wait_send()
                _remote(half(a, 1 - c), half(a, 1 - c), *sems, sibling).wait_recv()

    return _Rider(arrays, [jax.ShapeDtypeStruct(j.shape, j.dtype) for j in arrays], n, run,
                  aliases={a: a for a in range(n)})


def _allreduce_small(packed, rider):
    rows, lanes = packed.shape
    half = rows // 2
    r_in, r_out = len(rider.inputs), len(rider.out_shapes)

    def body(in_ref, *refs):
        rider_ins, out_ref, rider_outs = refs[:r_in], refs[r_in], refs[r_in + 1:r_in + 1 + r_out]
        pair_ref, gath_ref, send_sems, recv_sems, rider_send, rider_recv = refs[r_in + 1 + r_out:]
        x, y, c, chips = _place()
        me = 2 * x + y
        sibling = (x, y, 1 - c)
        mine = pl.ds(pl.multiple_of(c * half, 8), half)
        theirs = pl.ds(pl.multiple_of((1 - c) * half, 8), half)
        to_sib = _remote(in_ref.at[theirs], pair_ref, send_sems.at[0], recv_sems.at[0], sibling)
        to_sib.start()
        to_sib.wait()
        rider.run(rider_ins, rider_outs, rider_send, rider_recv, True)
        gath_ref[me] = in_ref[mine] + pair_ref[...]
        sends = [_remote(gath_ref.at[me], gath_ref.at[me], send_sems.at[1 + p], recv_sems.at[1 + p], (*chip, c))
                 for p, chip in enumerate(chips)]
        for cp in sends:
            cp.start()
        for p, chip in enumerate(chips):
            slot = gath_ref.at[2 * chip[0] + chip[1]]
            _remote(slot, slot, send_sems.at[1 + p], recv_sems.at[1 + p], (*chip, c)).wait_recv()
        for cp in sends:
            cp.wait_send()
        out_ref[mine] = ((gath_ref[0] + gath_ref[1]) + gath_ref[2]) + gath_ref[3]
        back = _remote(out_ref.at[mine], out_ref.at[mine], send_sems.at[4], recv_sems.at[4], sibling)
        back.start()
        back.wait_send()
        _remote(out_ref.at[theirs], out_ref.at[theirs], send_sems.at[4], recv_sems.at[4], sibling).wait_recv()
        rider.run(rider_ins, rider_outs, rider_send, rider_recv, False)

    vmem = pl.BlockSpec(memory_space=pltpu.VMEM)
    return pl.pallas_call(
        body, name="allreduce_small",
        in_specs=[vmem] + [ANY] * r_in, out_specs=(vmem,) + (ANY,) * r_out,
        out_shape=(jax.ShapeDtypeStruct((rows, lanes), F32),) + tuple(rider.out_shapes),
        scratch_shapes=[pltpu.VMEM((half, lanes), F32), pltpu.VMEM((4, half, lanes), F32),
                        pltpu.SemaphoreType.DMA((5,)), pltpu.SemaphoreType.DMA((5,)),
                        pltpu.SemaphoreType.DMA((rider.n_sems,)), pltpu.SemaphoreType.DMA((rider.n_sems,))],
        input_output_aliases={1 + i: 1 + o for i, o in rider.aliases.items()},
        compiler_params=pltpu.CompilerParams(has_side_effects=True, vmem_limit_bytes=32 * 1024 * 1024),
    )(packed, *rider.inputs)


SMALL = ("norm_pre", "pool_scale", "sgu_ln_g", "sgu_ln_b", "sgu_w", "sgu_b", "mem_norm", "branch_norm", "norm_post")
LARGE = ("w_in", "pool_w", "w_kv", "w_out")
ORDER = ("norm_pre", "w_in", "pool_w", "pool_scale", "sgu_ln_g", "sgu_ln_b", "sgu_w", "sgu_b", "mem_norm", "w_kv",
         "branch_norm", "w_out", "norm_post")


def _pack(arrays, extra=()):
    rows = [a.reshape(-1, 128) for a in arrays] + list(extra)
    pad = -sum(r.shape[0] for r in rows) % 16
    return jnp.concatenate(rows + ([jnp.zeros((pad, 128), F32)] if pad else []), axis=0)


def _unpack(packed, like):
    out, row = [], 0
    for a in like:
        rows = a.size // 128
        out.append(packed[row:row + rows].reshape(a.shape))
        row += rows
    return out


def kernel(x, mem, norm_pre, w_in, pool_w, pool_scale, sgu_ln_g, sgu_ln_b, sgu_w, sgu_b, mem_norm, w_kv, branch_norm, w_out, norm_post, loss_target, m_norm_pre, m_w_in, m_pool_w, m_pool_scale, m_sgu_ln_g, m_sgu_ln_b, m_sgu_w, m_sgu_b, m_mem_norm, m_w_kv, m_branch_norm, m_w_out, m_norm_post, v_norm_pre, v_w_in, v_pool_w, v_pool_scale, v_sgu_ln_g, v_sgu_ln_b, v_sgu_w, v_sgu_b, v_mem_norm, v_w_kv, v_branch_norm, v_w_out, v_norm_post):
    weights = dict(norm_pre=norm_pre, w_in=w_in, pool_w=pool_w, pool_scale=pool_scale, sgu_ln_g=sgu_ln_g,
                   sgu_ln_b=sgu_ln_b, sgu_w=sgu_w, sgu_b=sgu_b, mem_norm=mem_norm, w_kv=w_kv, branch_norm=branch_norm,
                   w_out=w_out, norm_post=norm_post)
    mom1 = dict(norm_pre=m_norm_pre, w_in=m_w_in, pool_w=m_pool_w, pool_scale=m_pool_scale, sgu_ln_g=m_sgu_ln_g,
                sgu_ln_b=m_sgu_ln_b, sgu_w=m_sgu_w, sgu_b=m_sgu_b, mem_norm=m_mem_norm, w_kv=m_w_kv,
                branch_norm=m_branch_norm, w_out=m_w_out, norm_post=m_norm_post)
    mom2 = dict(norm_pre=v_norm_pre, w_in=v_w_in, pool_w=v_pool_w, pool_scale=v_pool_scale, sgu_ln_g=v_sgu_ln_g,
                sgu_ln_b=v_sgu_ln_b, sgu_w=v_sgu_w, sgu_b=v_sgu_b, mem_norm=v_mem_norm, w_kv=v_w_kv,
                branch_norm=v_branch_norm, w_out=v_w_out, norm_post=v_norm_post)

    s, d = x.shape[1], x.shape[2]
    x2, mem2, tgt2 = x[0], mem[0], loss_target[0]
    t_branch = min(256, s)
    tm = min(512, s)

    core = lax.axis_index("c")
    chip = 2 * lax.axis_index("x") + lax.axis_index("y")
    pos = jnp.stack([core, chip]).astype(jnp.int32)
    n_in, n_kv, n_out = 4 * w_in.shape[2], 4 * w_kv.shape[1], 4 * w_out.shape[1]
    wi_rows, kv_rows, wo_rows = d // 8, n_kv // 8, n_out // 8

    kv_cols, pw_rows = w_kv.shape[2], GROUP // 8
    wi_own = _blockwise(_cast_copy, pos, [w_in[0]], [pl.BlockSpec((wi_rows, n_in // 4), lambda i, p: (i, 0))],
                        jax.ShapeDtypeStruct((d, n_in), BF16),
                        pl.BlockSpec((wi_rows, n_in // 4), lambda i, p: (i, p[1])), (8,), "place_w_in")

    x_pos, y_pos = lax.axis_index("x"), lax.axis_index("y")
    chips = jnp.stack([chip, 2 * (1 - x_pos) + y_pos, 2 * x_pos + 1 - y_pos,
                       2 * (1 - x_pos) + 1 - y_pos]).astype(jnp.int32)
    mem_g = mem_norm.reshape(1, d)
    proj, h, h_t, wkv_own, wo_own, pw_own, wi_full = _proj_piece(
        chips, 0, 1, x2, norm_pre, None, None, n_in, _gather_rider([wi_own], [0], peers=(0, 1)), tm, "proj_own",
        casts=[(w_kv[0], 1), (w_out[0], 2), (pool_w[0], 3)])
    proj, wi_full, wkv_full, pw_full = _proj_piece(
        chips, 1, 2, h, None, None, proj, n_in,
        _riders([_relay_rider(wi_full), _gather_rider([wkv_own, pw_own], [1, 3])]), tm, "proj_neighbours")
    proj, k_m, v_m, wo_part = _proj_piece(chips, 3, 1, h, None, wi_full, proj, n_in,
                                          _gather_rider([wo_own], [2], peers=(0, 1)), tm, "proj_diagonal",
                                          memory=(mem2, mem_g, wkv_full))
    bias_full = jnp.repeat(sgu_b[0].T, CHUNK, axis=1)
    y, y_t, wo_full = _branches_fwd(proj, pw_full, pool_scale, sgu_ln_g, sgu_ln_b, sgu_w[0], bias_full, k_m, v_m,
                                    branch_norm, t_branch, _gather_rider([wo_part], [2], peers=(2,)))
    loss_local, dz, dout, dy, g_norm_post = _out_loss(y, wo_full, x2, tgt2, norm_post, min(256, s))

    tk = min(1024, s)
    (dproj, g_pw, g_pool_scale, g_ln_g, g_ln_b, g_sgu_w, g_sgu_b_t, g_branch_norm, dk, dv) = _branches_bwd(
        proj, dy, pw_full, pool_scale, sgu_ln_g, sgu_ln_b, sgu_w[0], jnp.swapaxes(sgu_w[0], 1, 2), bias_full,
        k_m, v_m, branch_norm, t_branch)
    g_wkv, g_mem_norm = _kv_bwd(mem2, mem_g, wkv_full, dk, dv)
    g_wkv = g_wkv.reshape(4, 2, kv_rows, kv_cols)
    g_pw = g_pw.astype(BF16).reshape(4, 4, 2, pw_rows, GROUP)
    g_wo, gkv_from_sibling, gpw_from_sibling = _grad_rows(y_t, dout, pos, lambda i, p: i, n_out, n_out // 2, 1024, tk,
                                                          "grad_w_out", _exchange_rider([g_wkv, g_pw], [1, 2]))
    g_wo = g_wo.reshape(4, 2, wo_rows, d)
    ps_kv = _blockwise(_pair_sum, pos, [g_wkv, gkv_from_sibling],
                       [pl.BlockSpec((1, 1, kv_rows, kv_cols), lambda i, p: (i, p[0], 0, 0)),
                        pl.BlockSpec((1, 1, kv_rows, kv_cols), lambda i, p: (i, 0, 0, 0))],
                       jax.ShapeDtypeStruct((4, kv_rows, kv_cols), BF16),
                       pl.BlockSpec((1, kv_rows, kv_cols), lambda i, p: (i, 0, 0)), (4,), "pair_sum_w_kv")
    ps_pw = _blockwise(_pair_sum, pos, [g_pw, gpw_from_sibling],
                       [pl.BlockSpec((1, 4, 1, pw_rows, GROUP), lambda i, p: (i, 0, p[0], 0, 0)),
                        pl.BlockSpec((1, 4, 1, pw_rows, GROUP), lambda i, p: (i, 0, 0, 0, 0))],
                       jax.ShapeDtypeStruct((4, 4, pw_rows, GROUP), BF16),
                       pl.BlockSpec((1, 4, pw_rows, GROUP), lambda i, p: (i, 0, 0, 0)), (4,), "pair_sum_pool_w")
    gwi_theirs, landed_kv, landed_pw, gwo_from_sibling = _grad_rows(
        h_t, dproj, pos, lambda i, p: 1 - p[0], d // 2, d // 2, n_in // 4, tk, "grad_w_in_sibling_half",
        _riders([_scatter_rider([(ps_kv, 0), (ps_pw, 1)]), _exchange_rider([g_wo], [1])]))
    ps_wo = _blockwise(_pair_sum, pos, [g_wo, gwo_from_sibling],
                       [pl.BlockSpec((1, 1, wo_rows, d), lambda i, p: (i, p[0], 0, 0)),
                        pl.BlockSpec((1, 1, wo_rows, d), lambda i, p: (i, 0, 0, 0))],
                       jax.ShapeDtypeStruct((4, wo_rows, d), BF16),
                       pl.BlockSpec((1, wo_rows, d), lambda i, p: (i, 0, 0)), (4,), "pair_sum_w_out")
    gwi_mine, landed_wo, gwi_from_sibling = _grad_rows(
        h_t, dproj, pos, lambda i, p: p[0], d // 2, d // 2, n_in // 4, tk, "grad_w_in_own_half",
        _riders([_scatter_rider([(ps_wo, 0)]), _exchange_rider([gwi_theirs])]))
    ps_wi = _elementwise(_pair_sum, [gwi_mine, gwi_from_sibling], [BF16], "pair_sum_w_in")[0]
    grad_x, g_norm_pre, landed_wi = _dx_call(dproj, wi_full, x2, dz, norm_pre, tm, 1024,
                                             _scatter_rider([(ps_wi, 1)]))
    psum = [ps_wi, ps_kv, ps_wo, ps_pw]
    landed = [landed_wi, landed_kv, landed_wo, landed_pw]
    from_chip = lambda spec_shape, rank: [
        pl.BlockSpec(spec_shape, functools.partial(lambda i, p, q: (q, i) + (0,) * (rank - 2), q=q))
        for q in range(3)]
    join_rider = _join_rider([
        (_blockwise(_four_sum, pos, [psum[0]] + [landed[0]] * 3,
                    [pl.BlockSpec((256, n_in // 4), lambda i, p: (i, p[1]))] + from_chip((1, 256, n_in // 4), 3),
                    jax.ShapeDtypeStruct((2, d // 2, n_in // 4), F32),
                    pl.BlockSpec((1, 256, n_in // 4), lambda i, p: (p[0], i, 0)), (d // 2 // 256,), "chip_sum_w_in"),
         0),
        (_blockwise(_four_sum, pos, [psum[1]] + [landed[1]] * 3,
                    [pl.BlockSpec((1, kv_rows, kv_cols), lambda i, p: (p[1], 0, 0))]
                    + from_chip((1, 1, kv_rows, kv_cols), 4),
                    jax.ShapeDtypeStruct((2, kv_rows, kv_cols), F32),
                    pl.BlockSpec((1, kv_rows, kv_cols), lambda i, p: (p[0], 0, 0)), (1,), "chip_sum_w_kv"),
         0),
        (_blockwise(_four_sum, pos, [psum[2]] + [landed[2]] * 3,
                    [pl.BlockSpec((1, wo_rows, d), lambda i, p: (p[1], 0, 0))] + from_chip((1, 1, wo_rows, d), 4),
                    jax.ShapeDtypeStruct((2, wo_rows, d), F32),
                    pl.BlockSpec((1, wo_rows, d), lambda i, p: (p[0], 0, 0)), (1,), "chip_sum_w_out"),
         0),
        (_blockwise(_four_sum, pos, [psum[3]] + [landed[3]] * 3,
                    [pl.BlockSpec((4, 1, pw_rows, GROUP), lambda i, p: (0, p[1], 0, 0))]
                    + from_chip((1, 4, 1, pw_rows, GROUP), 5),
                    jax.ShapeDtypeStruct((4, 2, pw_rows, GROUP), F32),
                    pl.BlockSpec((4, 1, pw_rows, GROUP), lambda i, p: (0, p[0], 0, 0)), (1,), "chip_sum_pool_w"),
         1),
    ])

    small_local = dict(norm_pre=g_norm_pre, pool_scale=g_pool_scale, sgu_ln_g=g_ln_g, sgu_ln_b=g_ln_b,
                       sgu_w=g_sgu_w, sgu_b=g_sgu_b_t.T, mem_norm=g_mem_norm, branch_norm=g_branch_norm,
                       norm_post=g_norm_post)
    small_rows = sum(weights[n].size for n in SMALL) // 128
    small_sum, *joined = _allreduce_small(
        _pack([small_local[n] for n in SMALL], [jnp.pad(loss_local, ((0, 7), (0, 127)))]), join_rider)
    grads = {"w_in": joined[0].reshape(w_in.shape), "w_kv": joined[1].reshape(w_kv.shape),
             "w_out": joined[2].reshape(w_out.shape), "pool_w": joined[3].reshape(pool_w.shape)}
    for n, g in zip(SMALL, _unpack(small_sum, [weights[n] for n in SMALL])):
        grads[n] = g
    loss = small_sum[small_rows, 0]

    delta, new_m, new_v = {}, {}, {}
    packed = [small_sum if src is grads else _pack([src[n] for n in SMALL]) for src in (weights, grads, mom1, mom2)]
    outs = _elementwise(_adamw, packed, [F32, F32, F32], "adamw_small")
    for dst, o in zip((delta, new_m, new_v), outs):
        for n, a in zip(SMALL, _unpack(o, [weights[n] for n in SMALL])):
            dst[n] = a
    for n in LARGE:
        cols = weights[n].shape[-1]
        outs = _elementwise(lambda w, g, m, v: _adamw(w, g, m, v) + (g,),
                            [src[n].reshape(-1, cols) for src in (weights, grads, mom1, mom2)],
                            [F32, F32, F32, F32], "adamw_" + n)
        for dst, o in zip((delta, new_m, new_v, grads), outs):
            dst[n] = o.reshape(weights[n].shape)

    return (loss, grad_x[None], *[grads[n] for n in ORDER], *[delta[n] for n in ORDER],
            *[new_m[n] for n in ORDER], *[new_v[n] for n in ORDER])
```

```python
import functools

import jax
import jax.numpy as jnp
from jax import lax
from jax.experimental import pallas as pl
from jax.experimental.pallas import tpu as pltpu

F32 = jnp.float32
BF16 = jnp.bfloat16
EPS = 1e-6
MESH = pl.DeviceIdType.MESH
ANY = pl.BlockSpec(memory_space=pl.ANY)

POOL_WINDOWS = (2, 4, 8, 16)
GROUP = 256
HALO = 16
CHUNK = 128
N_SGU_HEADS = 8
N_ATT_HEADS = 4
ATT_DIM = 256
WIDTH = 1024
ATT_SCALE = 1.0 / 16.0

ADAM_LR = 0.001
ADAM_B1 = 0.9
ADAM_B2 = 0.999
ADAM_EPS = 1e-08
ADAM_WD = 0.01
ADAM_STEP = 10

VMEM_LIMIT = 60 * 1024 * 1024
ELEMENTWISE_VMEM = 24 * 1024 * 1024
LANES = 128
BF16_ROWS = 16
MAX_PARTS = 4


def _params(n_grid_axes, vmem=VMEM_LIMIT):
    return pltpu.CompilerParams(dimension_semantics=("arbitrary",) * n_grid_axes, vmem_limit_bytes=vmem)


def _dot(a, b, dims):
    return lax.dot_general(a, b, (dims, ((), ())), preferred_element_type=F32)


NN = ((1,), (0,))
NT = ((1,), (1,))
TN = ((0,), (0,))


class _Rider:
    def __init__(self, inputs, out_shapes, n_sems, run, aliases=None):
        self.inputs, self.out_shapes, self.n_sems, self.run = list(inputs), list(out_shapes), n_sems, run
        self.aliases = aliases or {}


def _call(body, name, grid, in_specs, out_specs, out_shape, scratch_shapes, inputs, rider=None, prefetch=None,
          aliases=None, rider_refs=False):
    n_in, n_out, n_scr = len(in_specs), len(out_specs), len(scratch_shapes)
    r_in = len(rider.inputs) if rider else 0
    r_out = len(rider.out_shapes) if rider else 0
    n_pre = 0 if prefetch is None else 1

    def whole_body(*refs):
        pre, refs = refs[:n_pre], refs[n_pre:]
        ins, rider_ins = refs[:n_in], refs[n_in:n_in + r_in]
        refs = refs[n_in + r_in:]
        outs, rider_outs = refs[:n_out], refs[n_out:n_out + r_out]
        refs = refs[n_out + r_out:]
        scratch, sems = refs[:n_scr], refs[n_scr:]
        extra = {"rider_outs": rider_outs} if rider_refs else {}
        if rider is None:
            body(*pre, *ins, *outs, *scratch, **extra)
            return
        ids = [pl.program_id(ax) for ax in range(len(grid))]
        first = functools.reduce(lambda p, q: p & q, [i == 0 for i in ids])
        last = functools.reduce(lambda p, q: p & q, [i == g - 1 for i, g in zip(ids, grid)])

        @pl.when(first)
        def _():
            rider.run(rider_ins, rider_outs, *sems, True)

        body(*pre, *ins, *outs, *scratch, **extra)

        @pl.when(last)
        def _():
            rider.run(rider_ins, rider_outs, *sems, False)

    io_aliases = {n_pre + i: o for i, o in (aliases or {}).items()}
    scratch_all = list(scratch_shapes)
    if rider:
        io_aliases.update({n_pre + n_in + i: n_out + o for i, o in rider.aliases.items()})
        scratch_all += [pltpu.SemaphoreType.DMA((rider.n_sems,)), pltpu.SemaphoreType.DMA((rider.n_sems,))]
    specs = dict(grid=grid, in_specs=list(in_specs) + [ANY] * r_in, out_specs=tuple(out_specs) + (ANY,) * r_out,
                 scratch_shapes=scratch_all)
    if n_pre:
        specs = dict(grid_spec=pltpu.PrefetchScalarGridSpec(num_scalar_prefetch=1, **specs))
    outs = pl.pallas_call(
        whole_body, name=name, **specs,
        out_shape=tuple(out_shape) + tuple(rider.out_shapes if rider else ()),
        input_output_aliases=io_aliases, compiler_params=_params(len(grid)),
    )(*([prefetch] if n_pre else []), *inputs, *(rider.inputs if rider else []))
    return tuple(outs)


def _grad_rows(a_t, b, pos, row_of, m, tm, tn, tk, name, rider=None):
    k, n = a_t.shape[1], b.shape[1]
    nk = k // tk
    out_dtype, dims, a = BF16, NN, a_t
    a_spec = pl.BlockSpec((tm, tk), lambda i, j, kk, p: (row_of(i, p), kk))
    b_spec = pl.BlockSpec((tk, tn), lambda i, j, kk, p: (kk, j))

    def body(pos_ref, a_ref, b_ref, o_ref, *acc):
        part = lambda: _dot(a_ref[...], b_ref[...], dims)
        if nk == 1:
            o_ref[...] = part().astype(out_dtype)
            return
        (acc_ref,) = acc
        kk = pl.program_id(2)

        @pl.when(kk == 0)
        def _():
            acc_ref[...] = part()

        @pl.when((kk > 0) & (kk < nk - 1))
        def _():
            acc_ref[...] += part()

        @pl.when(kk == nk - 1)
        def _():
            o_ref[...] = (acc_ref[...] + part()).astype(out_dtype)

    return _call(body, name, (m // tm, n // tn, nk), [a_spec, b_spec],
                 [pl.BlockSpec((tm, tn), lambda i, j, kk, p: (i, j))], [jax.ShapeDtypeStruct((m, n), out_dtype)],
                 [pltpu.VMEM((tm, tn), F32)] if nk > 1 else [], [a, b], rider, prefetch=pos)


def _proj_piece(chips, first, n_shards, src, g_pre, w_in, proj_in, n_cols, rider, tm, name, casts=(), memory=()):
    s, d = src.shape
    cols = n_cols // 4
    fused = g_pre is not None
    n_steps = s // tm

    def body(chips_ref, *refs, rider_outs=()):
        refs = list(refs)
        src_ref = refs.pop(0)
        g_ref = refs.pop(0) if fused else None
        w_ref = refs.pop(0) if w_in is not None else rider_outs[0]
        if proj_in is not None:
            refs.pop(0)
        shard_refs = [refs.pop(0) for _ in casts]
        memory_refs = [refs.pop(0) for _ in memory]
        proj_ref = refs.pop(0)
        h_ref, ht_ref = (refs.pop(0), refs.pop(0)) if fused else (None, None)
        for shard_ref in shard_refs:
            refs.pop(0)[...] = shard_ref[...].astype(BF16)
        kv_refs = [refs.pop(0) for _ in memory[:2]]
        wbuf, sem = refs
        q, i = pl.program_id(0), pl.program_id(1)

        @pl.when(i == 0)
        def _():
            at = pl.multiple_of(chips_ref[first + q] * cols, LANES)
            cp = pltpu.make_async_copy(w_ref.at[:, pl.ds(at, cols)], wbuf, sem)
            cp.start()
            cp.wait()

        if memory:
            @pl.when((q == 0) & (i == 0))
            def _():
                _kv_body(*memory_refs, *kv_refs)

        if fused:
            xv = src_ref[...]
            r = lax.rsqrt(jnp.mean(xv * xv, axis=-1, keepdims=True) + EPS)
            h = (xv * r * g_ref[...]).astype(BF16)
            h_ref[...] = h
            ht_ref[...] = h.T
        else:
            h = src_ref[...]
        proj_ref[...] = _dot(h, wbuf[...], NN)

    row = lambda q, i, ch: (i, 0)
    inputs, in_specs = [src], [pl.BlockSpec((tm, d), row)]
    if fused:
        inputs.append(g_pre)
        in_specs.append(pl.BlockSpec((1, d), lambda q, i, ch: (0, 0)))
    if w_in is not None:
        inputs.append(w_in)
        in_specs.append(ANY)
    aliases = {}
    if proj_in is not None:
        aliases[len(inputs)] = 0
        inputs.append(proj_in)
        in_specs.append(ANY)
    out_specs = [pl.BlockSpec((tm, cols), lambda q, i, ch: (i, ch[first + q]))]
    out_shape = [jax.ShapeDtypeStruct((s, n_cols), F32)]
    if fused:
        assert n_shards == 1
        out_specs += [pl.BlockSpec((tm, d), row), pl.BlockSpec((d, tm), lambda q, i, ch: (0, i))]
        out_shape += [jax.ShapeDtypeStruct((s, d), BF16), jax.ShapeDtypeStruct((d, s), BF16)]
    for shard, kind in casts:
        assert n_shards == 1
        inputs.append(shard)
        if kind == 3:
            in_specs.append(pl.BlockSpec(shard.shape, lambda q, i, ch: (0, 0, 0)))
            out_specs.append(pl.BlockSpec(shard.shape, lambda q, i, ch: (0, ch[0], 0)))
            out_shape.append(jax.ShapeDtypeStruct((shard.shape[0], 4 * shard.shape[1], shard.shape[2]), BF16))
        else:
            block = (shard.shape[0] // n_steps, shard.shape[1])
            in_specs.append(pl.BlockSpec(block, row))
            out_specs.append(pl.BlockSpec(block, lambda q, i, ch: (ch[0] * n_steps + i, 0)))
            out_shape.append(jax.ShapeDtypeStruct((4 * shard.shape[0], shard.shape[1]), BF16))
    if memory:
        mem = memory[0]
        whole = lambda q, i, ch: (0, 0)
        inputs += list(memory)
        in_specs += [pl.BlockSpec(mem.shape, whole), pl.BlockSpec(memory[1].shape, whole),
                     pl.BlockSpec(memory[2].shape, whole, pipeline_mode=pl.Buffered(1))]
        out_specs += [pl.BlockSpec((mem.shape[0], WIDTH), whole)] * 2
        out_shape += [jax.ShapeDtypeStruct((mem.shape[0], WIDTH), BF16)] * 2
    return _call(body, name, (n_shards, s // tm), in_specs, out_specs, out_shape,
                 [pltpu.VMEM((d, cols), BF16), pltpu.SemaphoreType.DMA(())], inputs, rider, prefetch=chips,
                 aliases=aliases, rider_refs=True)


def _kv_body(mem_ref, g_ref, w_ref, k_ref, v_ref):
    mv = mem_ref[...]
    r = lax.rsqrt(jnp.mean(mv * mv, axis=-1, keepdims=True) + EPS)
    mem_n = (mv * r * g_ref[...]).astype(BF16)
    kv = _dot(mem_n, w_ref[...], NN)
    k_ref[...] = kv[:, :WIDTH].astype(BF16)
    v_ref[...] = kv[:, WIDTH:].astype(BF16)


def _kv_bwd(mem, g, w_kv, dk, dv):
    m, d = mem.shape
    n = w_kv.shape[1]
    col = 512

    def body(mem_ref, g_ref, w_ref, dk_ref, dv_ref, dw_ref, dg_ref):
        mv = mem_ref[...]
        r = lax.rsqrt(jnp.mean(mv * mv, axis=-1, keepdims=True) + EPS)
        mem_hat = mv * r
        mem_n = (mem_hat * g_ref[...]).astype(BF16)
        dkv = jnp.concatenate([dk_ref[...], dv_ref[...]], axis=1).astype(BF16)
        for j in range(n // col):
            dw_ref[:, j * col:(j + 1) * col] = _dot(mem_n, dkv[:, j * col:(j + 1) * col], TN).astype(BF16)
        dmem_n = _dot(dkv, w_ref[...], NT)
        dg_ref[...] = jnp.sum(dmem_n * mem_hat, axis=0, keepdims=True)

    return pl.pallas_call(
        body, name="kv_bwd",
        out_shape=(jax.ShapeDtypeStruct((d, n), BF16), jax.ShapeDtypeStruct((1, d), F32)),
        compiler_params=_params(0),
    )(mem, g, w_kv, dk, dv)


def _sigmoid(x):
    return 1.0 / (1.0 + jnp.exp(-x))


def _inv_counts(t0, t):
    pos = (t0 + lax.broadcasted_iota(jnp.int32, (t, 1), 0) + 1).astype(F32)
    return [1.0 / jnp.minimum(pos, float(w)) for w in POOL_WINDOWS]


def _window_sums(ext, t, backward):
    n = t + HALO
    parts = []
    for gi, w in enumerate(POOL_WINDOWS):
        s = ext[:, gi * GROUP:(gi + 1) * GROUP]
        k = 1
        while k < w:
            s = s + pltpu.roll(s, (n - k) if backward else k, axis=0)
            k *= 2
        parts.append(s[:t] if backward else s[HALO:])
    return parts


def _pool_fwd(xa, halo, inv, pool_w):
    t = xa.shape[0]
    sums = _window_sums(jnp.concatenate([halo, xa], axis=0), t, backward=False)
    d = jnp.concatenate([sums[gi] * inv[gi] - xa[:, gi * GROUP:(gi + 1) * GROUP] for gi in range(4)], axis=1)
    d = d.astype(BF16)
    y = jnp.concatenate([_dot(d[:, gi * GROUP:(gi + 1) * GROUP], pool_w[gi], NN) for gi in range(4)], axis=1)
    return d, y


def _layernorm_fwd(v):
    mu = jnp.mean(v, axis=-1, keepdims=True)
    xc = v - mu
    rstd = lax.rsqrt(jnp.mean(xc * xc, axis=-1, keepdims=True) + EPS)
    return xc * rstd, rstd


def _tril_mask(transposed):
    r = lax.broadcasted_iota(jnp.int32, (CHUNK, CHUNK), 0)
    c = lax.broadcasted_iota(jnp.int32, (CHUNK, CHUNK), 1)
    return (r <= c) if transposed else (r >= c)


def _sgu_mix(w_ref, vals, transposed):
    t = vals.shape[0]
    mask = _tril_mask(transposed)
    ws = [jnp.where(mask, w_ref[h], 0.0).astype(BF16) for h in range(N_SGU_HEADS)]
    rows = []
    for ci in range(t // CHUNK):
        blk = vals[ci * CHUNK:(ci + 1) * CHUNK]
        rows.append(jnp.concatenate(
            [_dot(ws[h], blk[:, h * CHUNK:(h + 1) * CHUNK], NN) for h in range(N_SGU_HEADS)], axis=1))
    return jnp.concatenate(rows, axis=0)


def _attn_fwd(q, k, v):
    ps, os_ = [], []
    for h in range(N_ATT_HEADS):
        sl = slice(h * ATT_DIM, (h + 1) * ATT_DIM)
        s = _dot(q[:, sl], k[:, sl], NT) * ATT_SCALE
        s = s - jnp.max(s, axis=-1, keepdims=True)
        e = jnp.exp(s)
        p = e * (1.0 / jnp.sum(e, axis=-1, keepdims=True))
        ps.append(p)
        os_.append(_dot(p.astype(BF16), v[:, sl], NN))
    return ps, jnp.concatenate(os_, axis=1)


def _rms_branch(y_pre):
    r = lax.rsqrt(jnp.mean(y_pre * y_pre, axis=-1, keepdims=True) + EPS)
    return y_pre * r, r


def _branch_specs(t, n_tiles, order):
    width_in = 7 * WIDTH
    tile = lambda i: order(i)
    per_halo = t // HALO
    const2 = lambda i: (0, 0)
    const3 = lambda i: (0, 0, 0)
    return [
        pl.BlockSpec((t, width_in), lambda i: (tile(i), 0)),
        pl.BlockSpec((HALO, WIDTH), lambda i: (jnp.maximum(tile(i) * per_halo - 1, 0), 0)),
        pl.BlockSpec((4, GROUP, GROUP), const3),
        pl.BlockSpec((1, WIDTH), const2),
        pl.BlockSpec((1, WIDTH), const2),
        pl.BlockSpec((1, WIDTH), const2),
        pl.BlockSpec((N_SGU_HEADS, CHUNK, CHUNK), const3),
        pl.BlockSpec((CHUNK, WIDTH), const2),
        pl.BlockSpec((MEM_ROWS, WIDTH), const2),
        pl.BlockSpec((MEM_ROWS, WIDTH), const2),
        pl.BlockSpec((1, 3 * WIDTH), const2),
    ]


MEM_ROWS = 256


def _branches_fwd(proj, pool_w, pool_scale, ln_g, ln_b, sgu_w, bias_full, k, v, branch_norm, t, rider=None):
    s = proj.shape[0]
    n_tiles = s // t

    def body(proj_ref, halo_ref, pw_ref, ps_ref, lg_ref, lb_ref, sw_ref, sb_ref, k_ref, v_ref, bn_ref, y_ref, yt_ref):
        i = pl.program_id(0)
        col = lambda j: proj_ref[:, j * WIDTH:(j + 1) * WIDTH]

        def put(branch, y_pre):
            sl = slice(branch * WIDTH, (branch + 1) * WIDTH)
            val = (_rms_branch(y_pre)[0] * bn[:, sl]).astype(BF16)
            y_ref[:, sl] = val
            yt_ref[sl, :] = val.T

        bn = bn_ref[...]
        halo = jnp.where(i > 0, halo_ref[...], 0.0)
        _, y_pool = _pool_fwd(col(0), halo, _inv_counts(i * t, t), pw_ref[...])
        ga = col(1)
        ya = y_pool * ps_ref[...] * (ga * _sigmoid(ga))
        put(0, ya)
        vhat, _ = _layernorm_fwd(col(3))
        vn = (vhat * lg_ref[...] + lb_ref[...]).astype(BF16)
        z = _sgu_mix(sw_ref, vn, transposed=False) + jnp.tile(sb_ref[...], (t // CHUNK, 1))
        gb = col(4)
        yb = col(2) * z * (gb * _sigmoid(gb))
        put(1, yb)
        _, o = _attn_fwd(col(5).astype(BF16), k_ref[...], v_ref[...])
        gc = col(6)
        yc = o * (gc * _sigmoid(gc))
        put(2, yc)

    return _call(body, "branches_fwd", (n_tiles,), _branch_specs(t, n_tiles, lambda i: i),
                 [pl.BlockSpec((t, 3 * WIDTH), lambda i: (i, 0)), pl.BlockSpec((3 * WIDTH, t), lambda i: (0, i))],
                 [jax.ShapeDtypeStruct((s, 3 * WIDTH), BF16), jax.ShapeDtypeStruct((3 * WIDTH, s), BF16)], [],
                 [proj, proj, pool_w, pool_scale, ln_g, ln_b, sgu_w, bias_full, k, v, branch_norm], rider)


def _branches_bwd(proj, dy, pool_w, pool_scale, ln_g, ln_b, sgu_w, sgu_wt, bias_full, k, v, branch_norm, t, rider=None):
    s = proj.shape[0]
    n_tiles = s // t
    n_chunks = t // CHUNK
    order = lambda i: n_tiles - 1 - i

    def body(proj_ref, halo_ref, pw_ref, ps_ref, lg_ref, lb_ref, sw_ref, sb_ref, k_ref, v_ref, bn_ref,
             swt_ref, dy_ref,
             dproj_ref, dpw_ref, dps_ref, dlg_ref, dlb_ref, dsw_ref, dsb_ref, dbn_ref, dk_ref, dv_ref,
             carry_ref, dbias_ref):
        step = pl.program_id(0)
        i = order(step)

        @pl.when(step == 0)
        def _():
            for ref in (dpw_ref, dps_ref, dlg_ref, dlb_ref, dsw_ref, dbn_ref, dk_ref, dv_ref, carry_ref, dbias_ref):
                ref[...] = jnp.zeros(ref.shape, ref.dtype)

        col = lambda j: proj_ref[:, j * WIDTH:(j + 1) * WIDTH]
        bn = bn_ref[...]

        def norm_bwd(y_pre, sl):
            yhat, r = _rms_branch(y_pre)
            dyv = dy_ref[:, sl].astype(F32)
            dbn_ref[:, sl] += jnp.sum(dyv * yhat, axis=0, keepdims=True)
            dyhat = dyv * bn[:, sl]
            return r * (dyhat - yhat * jnp.mean(dyhat * yhat, axis=-1, keepdims=True))

        def gate(gv):
            sg = _sigmoid(gv)
            return gv * sg, sg * (1.0 + gv * (1.0 - sg))

        inv = _inv_counts(i * t, t)
        halo = jnp.where(i > 0, halo_ref[...], 0.0)
        pw = pw_ref[...]
        d, y_pool = _pool_fwd(col(0), halo, inv, pw)
        scale = ps_ref[...]
        silu_a, dsilu_a = gate(col(1))
        pa = y_pool * scale
        dya = norm_bwd(pa * silu_a, slice(0, WIDTH))
        dproj_ref[:, WIDTH:2 * WIDTH] = (dya * pa * dsilu_a).astype(BF16)
        dpa = dya * silu_a
        dps_ref[...] += jnp.sum(dpa * y_pool, axis=0, keepdims=True)
        dy_pool = (dpa * scale).astype(BF16)
        dd_parts, ddc_parts = [], []
        for gi in range(4):
            sl = slice(gi * GROUP, (gi + 1) * GROUP)
            dpw_ref[gi] += _dot(d[:, sl], dy_pool[:, sl], TN)
            dd = _dot(dy_pool[:, sl], pw[gi], NT)
            dd_parts.append(dd)
            ddc_parts.append(dd * inv[gi])
        ddc = jnp.concatenate(ddc_parts, axis=1)
        sums = _window_sums(jnp.concatenate([ddc, carry_ref[...]], axis=0), t, backward=True)
        carry_ref[...] = ddc[:HALO]
        dproj_ref[:, 0:WIDTH] = jnp.concatenate([sums[gi] - dd_parts[gi] for gi in range(4)], axis=1).astype(BF16)

        vhat, rstd = _layernorm_fwd(col(3))
        lg = lg_ref[...]
        vn = (vhat * lg + lb_ref[...]).astype(BF16)
        z = _sgu_mix(sw_ref, vn, transposed=False) + jnp.tile(sb_ref[...], (n_chunks, 1))
        u = col(2)
        silu_b, dsilu_b = gate(col(4))
        uz = u * z
        dyb = norm_bwd(uz * silu_b, slice(WIDTH, 2 * WIDTH))
        dproj_ref[:, 4 * WIDTH:5 * WIDTH] = (dyb * uz * dsilu_b).astype(BF16)
        duz = dyb * silu_b
        dproj_ref[:, 2 * WIDTH:3 * WIDTH] = (duz * z).astype(BF16)
        dz = duz * u
        dz_b = dz.astype(BF16)
        for ci in range(n_chunks):
            rows = slice(ci * CHUNK, (ci + 1) * CHUNK)
            dbias_ref[...] += dz[rows]
            for h in range(N_SGU_HEADS):
                sl = slice(h * CHUNK, (h + 1) * CHUNK)
                dsw_ref[h] += _dot(dz_b[rows, sl], vn[rows, sl], NT)
        dvn = _sgu_mix(swt_ref, dz_b, transposed=True)
        dlg_ref[...] += jnp.sum(dvn * vhat, axis=0, keepdims=True)
        dlb_ref[...] += jnp.sum(dvn, axis=0, keepdims=True)
        dvhat = dvn * lg
        dvb = rstd * (dvhat - jnp.mean(dvhat, axis=-1, keepdims=True)
                      - vhat * jnp.mean(dvhat * vhat, axis=-1, keepdims=True))
        dproj_ref[:, 3 * WIDTH:4 * WIDTH] = dvb.astype(BF16)

        q = col(5).astype(BF16)
        kv_k, kv_v = k_ref[...], v_ref[...]
        ps, o = _attn_fwd(q, kv_k, kv_v)
        silu_c, dsilu_c = gate(col(6))
        dyc = norm_bwd(o * silu_c, slice(2 * WIDTH, 3 * WIDTH))
        dproj_ref[:, 6 * WIDTH:7 * WIDTH] = (dyc * o * dsilu_c).astype(BF16)
        do = (dyc * silu_c).astype(BF16)
        dq_parts = []
        for h in range(N_ATT_HEADS):
            sl = slice(h * ATT_DIM, (h + 1) * ATT_DIM)
            p = ps[h]
            dp = _dot(do[:, sl], kv_v[:, sl], NT)
            ds = (p * (dp - jnp.sum(p * dp, axis=-1, keepdims=True)) * ATT_SCALE).astype(BF16)
            dq_parts.append(_dot(ds, kv_k[:, sl], NN))
            dk_ref[:, sl] += _dot(ds, q[:, sl], TN)
            dv_ref[:, sl] += _dot(p.astype(BF16), do[:, sl], TN)
        dproj_ref[:, 5 * WIDTH:6 * WIDTH] = jnp.concatenate(dq_parts, axis=1).astype(BF16)

        @pl.when(step == n_tiles - 1)
        def _():
            keep = _tril_mask(transposed=False)
            for h in range(N_SGU_HEADS):
                dsw_ref[h] = jnp.where(keep, dsw_ref[h], 0.0)
            dsb_ref[...] = jnp.concatenate(
                [jnp.sum(dbias_ref[:, h * CHUNK:(h + 1) * CHUNK], axis=1, keepdims=True)
                 for h in range(N_SGU_HEADS)], axis=1)

    const2 = lambda i: (0, 0)
    const3 = lambda i: (0, 0, 0)
    out_shapes = (
        jax.ShapeDtypeStruct((s, 7 * WIDTH), BF16),
        jax.ShapeDtypeStruct((4, GROUP, GROUP), F32),
        jax.ShapeDtypeStruct((1, WIDTH), F32),
        jax.ShapeDtypeStruct((1, WIDTH), F32),
        jax.ShapeDtypeStruct((1, WIDTH), F32),
        jax.ShapeDtypeStruct((N_SGU_HEADS, CHUNK, CHUNK), F32),
        jax.ShapeDtypeStruct((CHUNK, N_SGU_HEADS), F32),
        jax.ShapeDtypeStruct((1, 3 * WIDTH), F32),
        jax.ShapeDtypeStruct((MEM_ROWS, WIDTH), F32),
        jax.ShapeDtypeStruct((MEM_ROWS, WIDTH), F32),
    )
    out_specs = (
        pl.BlockSpec((t, 7 * WIDTH), lambda i: (order(i), 0)),
        pl.BlockSpec((4, GROUP, GROUP), const3),
        pl.BlockSpec((1, WIDTH), const2),
        pl.BlockSpec((1, WIDTH), const2),
        pl.BlockSpec((1, WIDTH), const2),
        pl.BlockSpec((N_SGU_HEADS, CHUNK, CHUNK), const3),
        pl.BlockSpec((CHUNK, N_SGU_HEADS), const2),
        pl.BlockSpec((1, 3 * WIDTH), const2),
        pl.BlockSpec((MEM_ROWS, WIDTH), const2),
        pl.BlockSpec((MEM_ROWS, WIDTH), const2),
    )
    in_specs = _branch_specs(t, n_tiles, order) + [
        pl.BlockSpec((N_SGU_HEADS, CHUNK, CHUNK), const3),
        pl.BlockSpec((t, 3 * WIDTH), lambda i: (order(i), 0)),
    ]
    return _call(body, "branches_bwd", (n_tiles,), in_specs, out_specs, out_shapes,
                 [pltpu.VMEM((HALO, WIDTH), F32), pltpu.VMEM((CHUNK, WIDTH), F32)],
                 [proj, proj, pool_w, pool_scale, ln_g, ln_b, sgu_w, bias_full, k, v, branch_norm, sgu_wt, dy], rider)


def _out_loss(y, w_out, x, target, g_post, tm):
    s, d = x.shape
    e_w = y.shape[1]
    n_tiles = s // tm

    def body(y_ref, w_ref, x_ref, t_ref, g_ref, loss_ref, dz_ref, dout_ref, dy_ref, dg_ref, sq_ref):
        i = pl.program_id(0)

        @pl.when(i == 0)
        def _():
            sq_ref[...] = jnp.zeros(sq_ref.shape, F32)
            dg_ref[...] = jnp.zeros(dg_ref.shape, F32)

        w = w_ref[...]
        out = _dot(y_ref[...], w, NN)
        r = lax.rsqrt(jnp.mean(out * out, axis=-1, keepdims=True) + EPS)
        outn = out * r
        g = g_ref[...]
        err = (x_ref[...] + outn * g) - t_ref[...]
        sq_ref[...] += jnp.sum(err * err, axis=0, keepdims=True)
        dz = err * (1.0 / d)
        dz_ref[...] = dz
        dg_ref[...] += jnp.sum(dz * outn, axis=0, keepdims=True)
        doutn = dz * g
        dout = (r * (doutn - outn * jnp.mean(doutn * outn, axis=-1, keepdims=True))).astype(BF16)
        dout_ref[...] = dout
        dy_ref[...] = _dot(dout, w, NT).astype(BF16)

        @pl.when(i == n_tiles - 1)
        def _():
            loss_ref[...] = 0.5 * jnp.sum(sq_ref[...], axis=1, keepdims=True) * (1.0 / d)

    row = lambda i: (i, 0)
    const2 = lambda i: (0, 0)
    return pl.pallas_call(
        body, name="out_loss", grid=(n_tiles,),
        in_specs=[
            pl.BlockSpec((tm, e_w), row),
            pl.BlockSpec((e_w, d), const2, pipeline_mode=pl.Buffered(1)),
            pl.BlockSpec((tm, d), row),
            pl.BlockSpec((tm, d), row),
            pl.BlockSpec((1, d), const2),
        ],
        out_specs=(
            pl.BlockSpec((1, 1), const2),
            pl.BlockSpec((tm, d), row),
            pl.BlockSpec((tm, d), row),
            pl.BlockSpec((tm, e_w), row),
            pl.BlockSpec((1, d), const2),
        ),
        out_shape=(
            jax.ShapeDtypeStruct((1, 1), F32),
            jax.ShapeDtypeStruct((s, d), F32),
            jax.ShapeDtypeStruct((s, d), BF16),
            jax.ShapeDtypeStruct((s, e_w), BF16),
            jax.ShapeDtypeStruct((1, d), F32),
        ),
        scratch_shapes=[pltpu.VMEM((1, d), F32)],
        compiler_params=_params(1),
    )(y, w_out, x, target, g_post)


def _dx_call(dproj, w_in, x, dz, g_pre, tm, tk, rider=None):
    s, d = x.shape
    k_total = dproj.shape[1]
    nk = k_total // tk
    n_tiles = s // tm

    def body(dp_ref, w_ref, x_ref, dz_ref, g_ref, dx_ref, dg_ref, acc_ref):
        i, kk = pl.program_id(0), pl.program_id(1)
        part = lambda: _dot(dp_ref[...], w_ref[...], NT)

        @pl.when(kk == 0)
        def _():
            acc_ref[...] = part()

        @pl.when((kk > 0) & (kk < nk - 1))
        def _():
            acc_ref[...] += part()

        @pl.when((i == 0) & (kk == 0))
        def _():
            dg_ref[...] = jnp.zeros(dg_ref.shape, F32)

        @pl.when(kk == nk - 1)
        def _():
            dh = acc_ref[...] + part()
            xv = x_ref[...]
            r = lax.rsqrt(jnp.mean(xv * xv, axis=-1, keepdims=True) + EPS)
            xhat = xv * r
            dg_ref[...] += jnp.sum(dh * xhat, axis=0, keepdims=True)
            dxhat = dh * g_ref[...]
            dx_ref[...] = dz_ref[...] + r * (dxhat - xhat * jnp.mean(dxhat * xhat, axis=-1, keepdims=True))

    row = lambda i, kk: (i, 0)
    const2 = lambda i, kk: (0, 0)
    return _call(
        body, "dx", (n_tiles, nk),
        [
            pl.BlockSpec((tm, tk), lambda i, kk: (i, kk)),
            pl.BlockSpec((d, tk), lambda i, kk: (0, kk)),
            pl.BlockSpec((tm, d), row),
            pl.BlockSpec((tm, d), row),
            pl.BlockSpec((1, d), const2),
        ],
        [pl.BlockSpec((tm, d), row), pl.BlockSpec((1, d), const2)],
        [jax.ShapeDtypeStruct((s, d), F32), jax.ShapeDtypeStruct((1, d), F32)],
        [pltpu.VMEM((tm, d), F32)], [dproj, w_in, x, dz, g_pre], rider)


def _rows_tile(rows, cols, n_arrays, itemsize=4):
    budget = ELEMENTWISE_VMEM // (2 * n_arrays * cols * itemsize)
    if rows <= budget:
        return rows
    best = None
    for cand in range(16, rows + 1, 16):
        if rows % cand == 0 and cand <= max(budget, 16):
            best = cand
    return best if best is not None else rows


def _elementwise(fn, inputs, out_dtypes, name):
    rows, cols = inputs[0].shape
    tr = _rows_tile(rows, cols, len(inputs) + len(out_dtypes))
    n_in = len(inputs)

    def body(*refs):
        outs = fn(*[r[...] for r in refs[:n_in]])
        for o_ref, o in zip(refs[n_in:], outs):
            o_ref[...] = o.astype(o_ref.dtype)

    spec = pl.BlockSpec((tr, cols), lambda i: (i, 0))
    return pl.pallas_call(
        body, name=name, grid=(rows // tr,),
        in_specs=[spec] * n_in, out_specs=tuple([spec] * len(out_dtypes)),
        out_shape=tuple(jax.ShapeDtypeStruct((rows, cols), dt) for dt in out_dtypes),
        compiler_params=_params(1),
    )(*inputs)


def _blockwise(fn, pos, inputs, in_specs, out_shape, out_spec, grid, name):
    n_in = len(inputs)

    def body(pos_ref, *refs):
        o_ref = refs[n_in]
        (out,) = fn(*[r[...].reshape(o_ref.shape) for r in refs[:n_in]])
        o_ref[...] = out.astype(o_ref.dtype)

    return pl.pallas_call(
        body, name=name,
        grid_spec=pltpu.PrefetchScalarGridSpec(num_scalar_prefetch=1, grid=grid, in_specs=in_specs,
                                               out_specs=out_spec),
        out_shape=out_shape,
        compiler_params=_params(len(grid)),
    )(pos, *inputs)


def _cast_copy(x):
    return (x,)


def _pair_sum(mine, theirs):
    return ((mine.astype(F32) + theirs.astype(F32)),)


def _four_sum(own, t0, t1, t2):
    return ((((own.astype(F32) + t0.astype(F32)) + t1.astype(F32)) + t2.astype(F32)),)


def _adamw(w, g, m, v):
    m = ADAM_B1 * m + (1.0 - ADAM_B1) * g
    v = ADAM_B2 * v + (1.0 - ADAM_B2) * jnp.square(g)
    m_hat = m / (1.0 - ADAM_B1 ** ADAM_STEP)
    v_hat = v / (1.0 - ADAM_B2 ** ADAM_STEP)
    delta = -ADAM_LR * (m_hat / (jnp.sqrt(v_hat) + ADAM_EPS) + ADAM_WD * w)
    return delta, m, v


def _place():
    x, y, c = lax.axis_index("x"), lax.axis_index("y"), lax.axis_index("c")
    chips = [(1 - x, y), (x, 1 - y), (1 - x, 1 - y)]
    return x, y, c, chips


def _remote(src, dst, send_sem, recv_sem, to):
    return pltpu.make_async_remote_copy(src_ref=src, dst_ref=dst, send_sem=send_sem, recv_sem=recv_sem,
                                        device_id=to, device_id_type=MESH)


def _split(ref, plan):
    views = [ref]
    for axis, parts in plan:
        size = ref.shape[axis] // parts
        assert size * parts == ref.shape[axis]
        views = [v.at[tuple(pl.ds(q * size, size) if i == axis else slice(None) for i in range(len(ref.shape)))]
                 for v in views for q in range(parts)]
    return views


def _started(src, dst, send_sem, recv_sem, to):
    copy = _remote(src, dst, send_sem, recv_sem, to)
    copy.start()
    return copy


def _shard_half(kind, ref, chip, cc):
    if kind == 0:
        rows, cols = ref.shape[0] // 2, ref.shape[1] // 4
        return ref.at[pl.ds(cc * rows, rows), pl.ds(pl.multiple_of(chip * cols, LANES), cols)]
    if kind == 3:
        rows = ref.shape[1] // 8
        return ref.at[:, pl.ds(pl.multiple_of((2 * chip + cc) * rows, BF16_ROWS), rows), :]
    rows = ref.shape[0] // 8
    return ref.at[pl.ds(pl.multiple_of((2 * chip + cc) * rows, BF16_ROWS), rows), :]


def _relay_rider(full):
    kind = 0

    def quarter(ref, chip_no, cc, q):
        return _split(_shard_half(kind, ref, chip_no, cc), [(0, 2)])[q]

    def run(in_refs, full_refs, send_sems, recv_sems, start):
        (ref,) = full_refs
        x, y, c, chips = _place()
        sibling = (x, y, 1 - c)
        chip_no = [2 * ch[0] + ch[1] for ch in chips]
        if start:
            for p in (0, 1):
                held = quarter(ref, chip_no[1 - p], c, p)
                _remote(held, held, send_sems.at[p], recv_sems.at[p], (*chips[p], c)).start()
            return
        for p in (0, 1):
            landed = quarter(ref, chip_no[2], c, p)
            _remote(landed, landed, send_sems.at[p], recv_sems.at[p], (*chips[p], c)).wait_recv()
            _remote(landed, landed, send_sems.at[2], recv_sems.at[2], sibling).start()
        mine, theirs = _shard_half(kind, ref, chip_no[2], c), _shard_half(kind, ref, chip_no[2], 1 - c)
        _remote(mine, mine, send_sems.at[2], recv_sems.at[2], sibling).wait_send()
        _remote(theirs, theirs, send_sems.at[2], recv_sems.at[2], sibling).wait_recv()
        for p in (0, 1):
            held = quarter(ref, chip_no[1 - p], c, p)
            _remote(held, held, send_sems.at[p], recv_sems.at[p], (*chips[p], c)).wait_send()

    return _Rider([full], [jax.ShapeDtypeStruct(full.shape, full.dtype)], 3, run, aliases={0: 0})


def _riders(riders):
    def bounds(counts):
        ends = [sum(counts[:i + 1]) for i in range(len(counts))]
        return list(zip([0] + ends[:-1], ends))

    ins = bounds([len(r.inputs) for r in riders])
    outs = bounds([len(r.out_shapes) for r in riders])
    sems = bounds([r.n_sems for r in riders])

    class From:
        def __init__(self, sem_refs, base):
            self.sem_refs, self.base, self.at = sem_refs, base, self

        def __getitem__(self, k):
            return self.sem_refs.at[self.base + k]

    def run(in_refs, out_refs, send_sems, recv_sems, start):
        for r, (i0, i1), (o0, o1), (s0, _) in zip(riders, ins, outs, sems):
            r.run(in_refs[i0:i1], out_refs[o0:o1], From(send_sems, s0), From(recv_sems, s0), start)

    aliases = {}
    for r, (i0, _), (o0, _) in zip(riders, ins, outs):
        aliases.update({i0 + i: o0 + o for i, o in r.aliases.items()})
    return _Rider([a for r in riders for a in r.inputs], [o for r in riders for o in r.out_shapes],
                  sems[-1][1], run, aliases)


def _gather_rider(fulls, kinds, peers=(0, 1, 2)):
    n = len(fulls)
    full_half = _shard_half

    def run(in_refs, full_refs, send_sems, recv_sems, start):
        x, y, c, chips = _place()
        me = 2 * x + y
        sibling = (x, y, 1 - c)
        plans = [[(0, MAX_PARTS)], [(0, 2)], [(0, 2)], []]
        chips = [(p, chips[p]) for p in peers]
        across = lambda a, p, k: (3 * a + p) * MAX_PARTS + k
        onward = lambda a, p: 3 * n * MAX_PARTS + 3 * a + p

        def parts(a, chip_no, cc):
            return _split(full_half(kinds[a], full_refs[a], chip_no, cc), plans[kinds[a]])

        if start:
            for p, chip in chips:
                for a in range(n):
                    for k, mine in enumerate(parts(a, me, c)):
                        _remote(mine, mine, send_sems.at[across(a, p, k)], recv_sems.at[across(a, p, k)],
                                (*chip, c)).start()
            return
        for k in range(MAX_PARTS):
            for p, chip in chips:
                for a in range(n):
                    landed = parts(a, 2 * chip[0] + chip[1], c)
                    if k < len(landed):
                        _remote(landed[k], landed[k], send_sems.at[across(a, p, k)], recv_sems.at[across(a, p, k)],
                                (*chip, c)).wait_recv()
                        _remote(landed[k], landed[k], send_sems.at[onward(a, p)], recv_sems.at[onward(a, p)],
                                sibling).start()
        for p, chip in chips:
            them = 2 * chip[0] + chip[1]
            for a in range(n):
                passed = full_half(kinds[a], full_refs[a], them, 1 - c)
                _remote(passed, passed, send_sems.at[onward(a, p)], recv_sems.at[onward(a, p)], sibling).wait_recv()
                landed = full_half(kinds[a], full_refs[a], them, c)
                _remote(landed, landed, send_sems.at[onward(a, p)], recv_sems.at[onward(a, p)], sibling).wait_send()
                for k, mine in enumerate(parts(a, me, c)):
                    _remote(mine, mine, send_sems.at[across(a, p, k)], recv_sems.at[across(a, p, k)],
                            (*chip, c)).wait_send()

    return _Rider(fulls, [jax.ShapeDtypeStruct(f.shape, f.dtype) for f in fulls], 3 * n * (MAX_PARTS + 1), run,
                  aliases={a: a for a in range(n)})


def _exchange_rider(arrays, half_axes=None):
    n = len(arrays)
    half_axes = half_axes or [None] * n
    out_shapes = [jax.ShapeDtypeStruct(tuple(1 if i == ax else dim for i, dim in enumerate(g.shape)), g.dtype)
                  for g, ax in zip(arrays, half_axes)]

    def run(in_refs, out_refs, send_sems, recv_sems, start):
        x, y, c, _ = _place()
        sibling = (x, y, 1 - c)
        for a in range(n):
            src, ax = in_refs[a], half_axes[a]
            if ax is not None:
                src = src.at[tuple(pl.ds(1 - c, 1) if i == ax else slice(None) for i in range(len(src.shape)))]
            sems = (send_sems.at[a], recv_sems.at[a])
            if start:
                _started(src, out_refs[a], *sems, sibling)
            else:
                _remote(src, out_refs[a], *sems, sibling).wait()

    return _Rider(arrays, out_shapes, n, run)


def _scatter_rider(parts):
    n = len(parts)
    arrays = [p for p, _ in parts]

    def block_shape(p, ax):
        if ax == len(p.shape) - 1:
            return p.shape[:-1] + (p.shape[-1] // 4,)
        return tuple(1 if i == ax else dim for i, dim in enumerate(p.shape))

    out_shapes = [jax.ShapeDtypeStruct((3,) + block_shape(p, ax), p.dtype) for p, ax in parts]

    def block(ref, ax, chip):
        rank = len(ref.shape)
        if ax == rank - 1:
            cols = ref.shape[-1] // 4
            last = pl.ds(pl.multiple_of(chip * cols, LANES), cols)
            return ref.at[tuple([slice(None)] * (rank - 1) + [last])]
        return ref.at[tuple(pl.ds(chip, 1) if i == ax else slice(None) for i in range(rank))]

    def run(in_refs, out_refs, send_sems, recv_sems, start):
        x, y, c, chips = _place()
        for a in range(n):
            ax = parts[a][1]
            for p, chip in enumerate(chips):
                src, dst = block(in_refs[a], ax, 2 * chip[0] + chip[1]), out_refs[a].at[p]
                sems = (send_sems.at[3 * a + p], recv_sems.at[3 * a + p])
                if start:
                    _started(src, dst, *sems, (*chip, c))
                else:
                    _remote(src, dst, *sems, (*chip, c)).wait()

    return _Rider(arrays, out_shapes, 3 * n, run)


def _join_rider(joined):
    n = len(joined)
    arrays = [j for j, _ in joined]

    def run(in_refs, out_refs, send_sems, recv_sems, start):
        x, y, c, _ = _place()
        sibling = (x, y, 1 - c)

        def half(a, cc):
            rank = len(out_refs[a].shape)
            return out_refs[a].at[tuple(pl.ds(cc, 1) if i == joined[a][1] else slice(None) for i in range(rank))]

        for a in range(n):
            sems = (send_sems.at[a], recv_sems.at[a])
            if start:
                _started(half(a, c), half(a, c), *sems, sibling)
            else:
                _remote(half(a, c), half(a, c), *sems, sibling).wait_send()
                _remote(half(a, 1 - c), half(a, 1 - c), *sems, sibling).wait_recv()

    return _Rider(arrays, [jax.ShapeDtypeStruct(j.shape, j.dtype) for j in arrays], n, run,
                  aliases={a: a for a in range(n)})


def _allreduce_small(packed, rider):
    rows, lanes = packed.shape
    half = rows // 2
    r_in, r_out = len(rider.inputs), len(rider.out_shapes)

    def body(in_ref, *refs):
        rider_ins, out_ref, rider_outs = refs[:r_in], refs[r_in], refs[r_in + 1:r_in + 1 + r_out]
        pair_ref, gath_ref, send_sems, recv_sems, rider_send, rider_recv = refs[r_in + 1 + r_out:]
        x, y, c, chips = _place()
        me = 2 * x + y
        sibling = (x, y, 1 - c)
        mine = pl.ds(pl.multiple_of(c * half, 8), half)
        theirs = pl.ds(pl.multiple_of((1 - c) * half, 8), half)
        to_sib = _remote(in_ref.at[theirs], pair_ref, send_sems.at[0], recv_sems.at[0], sibling)
        to_sib.start()
        to_sib.wait()
        rider.run(rider_ins, rider_outs, rider_send, rider_recv, True)
        gath_ref[me] = in_ref[mine] + pair_ref[...]
        sends = [_remote(gath_ref.at[me], gath_ref.at[me], send_sems.at[1 + p], recv_sems.at[1 + p], (*chip, c))
                 for p, chip in enumerate(chips)]
        for cp in sends:
            cp.start()
        for p, chip in enumerate(chips):
            slot = gath_ref.at[2 * chip[0] + chip[1]]
            _remote(slot, slot, send_sems.at[1 + p], recv_sems.at[1 + p], (*chip, c)).wait_recv()
        for cp in sends:
            cp.wait_send()
        out_ref[mine] = ((gath_ref[0] + gath_ref[1]) + gath_ref[2]) + gath_ref[3]
        back = _remote(out_ref.at[mine], out_ref.at[mine], send_sems.at[4], recv_sems.at[4], sibling)
        back.start()
        back.wait_send()
        _remote(out_ref.at[theirs], out_ref.at[theirs], send_sems.at[4], recv_sems.at[4], sibling).wait_recv()
        rider.run(rider_ins, rider_outs, rider_send, rider_recv, False)

    vmem = pl.BlockSpec(memory_space=pltpu.VMEM)
    return pl.pallas_call(
        body, name="allreduce_small",
        in_specs=[vmem] + [ANY] * r_in, out_specs=(vmem,) + (ANY,) * r_out,
        out_shape=(jax.ShapeDtypeStruct((rows, lanes), F32),) + tuple(rider.out_shapes),
        scratch_shapes=[pltpu.VMEM((half, lanes), F32), pltpu.VMEM((4, half, lanes), F32),
                        pltpu.SemaphoreType.DMA((5,)), pltpu.SemaphoreType.DMA((5,)),
                        pltpu.SemaphoreType.DMA((rider.n_sems,)), pltpu.SemaphoreType.DMA((rider.n_sems,))],
        input_output_aliases={1 + i: 1 + o for i, o in rider.aliases.items()},
        compiler_params=pltpu.CompilerParams(has_side_effects=True, vmem_limit_bytes=32 * 1024 * 1024),
    )(packed, *rider.inputs)


SMALL = ("norm_pre", "pool_scale", "sgu_ln_g", "sgu_ln_b", "sgu_w", "sgu_b", "mem_norm", "branch_norm", "norm_post")
LARGE = ("w_in", "pool_w", "w_kv", "w_out")
ORDER = ("norm_pre", "w_in", "pool_w", "pool_scale", "sgu_ln_g", "sgu_ln_b", "sgu_w", "sgu_b", "mem_norm", "w_kv",
         "branch_norm", "w_out", "norm_post")


def _pack(arrays, extra=()):
    rows = [a.reshape(-1, 128) for a in arrays] + list(extra)
    pad = -sum(r.shape[0] for r in rows) % 16
    return jnp.concatenate(rows + ([jnp.zeros((pad, 128), F32)] if pad else []), axis=0)


def _unpack(packed, like):
    out, row = [], 0
    for a in like:
        rows = a.size // 128
        out.append(packed[row:row + rows].reshape(a.shape))
        row += rows
    return out


def kernel(x, mem, norm_pre, w_in, pool_w, pool_scale, sgu_ln_g, sgu_ln_b, sgu_w, sgu_b, mem_norm, w_kv, branch_norm, w_out, norm_post, loss_target, m_norm_pre, m_w_in, m_pool_w, m_pool_scale, m_sgu_ln_g, m_sgu_ln_b, m_sgu_w, m_sgu_b, m_mem_norm, m_w_kv, m_branch_norm, m_w_out, m_norm_post, v_norm_pre, v_w_in, v_pool_w, v_pool_scale, v_sgu_ln_g, v_sgu_ln_b, v_sgu_w, v_sgu_b, v_mem_norm, v_w_kv, v_branch_norm, v_w_out, v_norm_post):
    weights = dict(norm_pre=norm_pre, w_in=w_in, pool_w=pool_w, pool_scale=pool_scale, sgu_ln_g=sgu_ln_g,
                   sgu_ln_b=sgu_ln_b, sgu_w=sgu_w, sgu_b=sgu_b, mem_norm=mem_norm, w_kv=w_kv, branch_norm=branch_norm,
                   w_out=w_out, norm_post=norm_post)
    mom1 = dict(norm_pre=m_norm_pre, w_in=m_w_in, pool_w=m_pool_w, pool_scale=m_pool_scale, sgu_ln_g=m_sgu_ln_g,
                sgu_ln_b=m_sgu_ln_b, sgu_w=m_sgu_w, sgu_b=m_sgu_b, mem_norm=m_mem_norm, w_kv=m_w_kv,
                branch_norm=m_branch_norm, w_out=m_w_out, norm_post=m_norm_post)
    mom2 = dict(norm_pre=v_norm_pre, w_in=v_w_in, pool_w=v_pool_w, pool_scale=v_pool_scale, sgu_ln_g=v_sgu_ln_g,
                sgu_ln_b=v_sgu_ln_b, sgu_w=v_sgu_w, sgu_b=v_sgu_b, mem_norm=v_mem_norm, w_kv=v_w_kv,
                branch_norm=v_branch_norm, w_out=v_w_out, norm_post=v_norm_post)

    s, d = x.shape[1], x.shape[2]
    x2, mem2, tgt2 = x[0], mem[0], loss_target[0]
    t_branch = min(256, s)
    tm = min(512, s)

    core = lax.axis_index("c")
    chip = 2 * lax.axis_index("x") + lax.axis_index("y")
    pos = jnp.stack([core, chip]).astype(jnp.int32)
    n_in, n_kv, n_out = 4 * w_in.shape[2], 4 * w_kv.shape[1], 4 * w_out.shape[1]
    wi_rows, kv_rows, wo_rows = d // 8, n_kv // 8, n_out // 8

    kv_cols, pw_rows = w_kv.shape[2], GROUP // 8
    wi_own = _blockwise(_cast_copy, pos, [w_in[0]], [pl.BlockSpec((wi_rows, n_in // 4), lambda i, p: (i, 0))],
                        jax.ShapeDtypeStruct((d, n_in), BF16),
                        pl.BlockSpec((wi_rows, n_in // 4), lambda i, p: (i, p[1])), (8,), "place_w_in")

    x_pos, y_pos = lax.axis_index("x"), lax.axis_index("y")
    chips = jnp.stack([chip, 2 * (1 - x_pos) + y_pos, 2 * x_pos + 1 - y_pos,
                       2 * (1 - x_pos) + 1 - y_pos]).astype(jnp.int32)
    mem_g = mem_norm.reshape(1, d)
    proj, h, h_t, wkv_own, wo_own, pw_own, wi_full = _proj_piece(
        chips, 0, 1, x2, norm_pre, None, None, n_in, _gather_rider([wi_own], [0], peers=(0, 1)), tm, "proj_own",
        casts=[(w_kv[0], 1), (w_out[0], 2), (pool_w[0], 3)])
    proj, wi_full, wkv_full, pw_full = _proj_piece(
        chips, 1, 2, h, None, None, proj, n_in,
        _riders([_relay_rider(wi_full), _gather_rider([wkv_own, pw_own], [1, 3])]), tm, "proj_neighbours")
    proj, k_m, v_m, wo_part = _proj_piece(chips, 3, 1, h, None, wi_full, proj, n_in,
                                          _gather_rider([wo_own], [2], peers=(0, 1)), tm, "proj_diagonal",
                                          memory=(mem2, mem_g, wkv_full))
    bias_full = jnp.repeat(sgu_b[0].T, CHUNK, axis=1)
    y, y_t, wo_full = _branches_fwd(proj, pw_full, pool_scale, sgu_ln_g, sgu_ln_b, sgu_w[0], bias_full, k_m, v_m,
                                    branch_norm, t_branch, _gather_rider([wo_part], [2], peers=(2,)))
    loss_local, dz, dout, dy, g_norm_post = _out_loss(y, wo_full, x2, tgt2, norm_post, min(256, s))

    tk = min(1024, s)
    (dproj, g_pw, g_pool_scale, g_ln_g, g_ln_b, g_sgu_w, g_sgu_b_t, g_branch_norm, dk, dv) = _branches_bwd(
        proj, dy, pw_full, pool_scale, sgu_ln_g, sgu_ln_b, sgu_w[0], jnp.swapaxes(sgu_w[0], 1, 2), bias_full,
        k_m, v_m, branch_norm, t_branch)
    g_wkv, g_mem_norm = _kv_bwd(mem2, mem_g, wkv_full, dk, dv)
    g_wkv = g_wkv.reshape(4, 2, kv_rows, kv_cols)
    g_pw = g_pw.astype(BF16).reshape(4, 4, 2, pw_rows, GROUP)
    g_wo, gkv_from_sibling, gpw_from_sibling = _grad_rows(y_t, dout, pos, lambda i, p: i, n_out, n_out // 2, 1024, tk,
                                                          "grad_w_out", _exchange_rider([g_wkv, g_pw], [1, 2]))
    g_wo = g_wo.reshape(4, 2, wo_rows, d)
    ps_kv = _blockwise(_pair_sum, pos, [g_wkv, gkv_from_sibling],
                       [pl.BlockSpec((1, 1, kv_rows, kv_cols), lambda i, p: (i, p[0], 0, 0)),
                        pl.BlockSpec((1, 1, kv_rows, kv_cols), lambda i, p: (i, 0, 0, 0))],
                       jax.ShapeDtypeStruct((4, kv_rows, kv_cols), BF16),
                       pl.BlockSpec((1, kv_rows, kv_cols), lambda i, p: (i, 0, 0)), (4,), "pair_sum_w_kv")
    ps_pw = _blockwise(_pair_sum, pos, [g_pw, gpw_from_sibling],
                       [pl.BlockSpec((1, 4, 1, pw_rows, GROUP), lambda i, p: (i, 0, p[0], 0, 0)),
                        pl.BlockSpec((1, 4, 1, pw_rows, GROUP), lambda i, p: (i, 0, 0, 0, 0))],
                       jax.ShapeDtypeStruct((4, 4, pw_rows, GROUP), BF16),
                       pl.BlockSpec((1, 4, pw_rows, GROUP), lambda i, p: (i, 0, 0, 0)), (4,), "pair_sum_pool_w")
    gwi_theirs, landed_kv, landed_pw, gwo_from_sibling = _grad_rows(
        h_t, dproj, pos, lambda i, p: 1 - p[0], d // 2, d // 2, n_in // 4, min(2 * tk, s), "grad_w_in_sibling_half",
        _riders([_scatter_rider([(ps_kv, 0), (ps_pw, 1)]), _exchange_rider([g_wo], [1])]))
    ps_wo = _blockwise(_pair_sum, pos, [g_wo, gwo_from_sibling],
                       [pl.BlockSpec((1, 1, wo_rows, d), lambda i, p: (i, p[0], 0, 0)),
                        pl.BlockSpec((1, 1, wo_rows, d), lambda i, p: (i, 0, 0, 0))],
                       jax.ShapeDtypeStruct((4, wo_rows, d), BF16),
                       pl.BlockSpec((1, wo_rows, d), lambda i, p: (i, 0, 0)), (4,), "pair_sum_w_out")
    gwi_mine, landed_wo, gwi_from_sibling = _grad_rows(
        h_t, dproj, pos, lambda i, p: p[0], d // 2, d // 2, n_in // 4, min(2 * tk, s), "grad_w_in_own_half",
        _riders([_scatter_rider([(ps_wo, 0)]), _exchange_rider([gwi_theirs])]))
    ps_wi = _elementwise(_pair_sum, [gwi_mine, gwi_from_sibling], [BF16], "pair_sum_w_in")[0]
    grad_x, g_norm_pre, landed_wi = _dx_call(dproj, wi_full, x2, dz, norm_pre, tm, 1024,
                                             _scatter_rider([(ps_wi, 1)]))
    psum = [ps_wi, ps_kv, ps_wo, ps_pw]
    landed = [landed_wi, landed_kv, landed_wo, landed_pw]
    from_chip = lambda spec_shape, rank: [
        pl.BlockSpec(spec_shape, functools.partial(lambda i, p, q: (q, i) + (0,) * (rank - 2), q=q))
        for q in range(3)]
    join_rider = _join_rider([
        (_blockwise(_four_sum, pos, [psum[0]] + [landed[0]] * 3,
                    [pl.BlockSpec((256, n_in // 4), lambda i, p: (i, p[1]))] + from_chip((1, 256, n_in // 4), 3),
                    jax.ShapeDtypeStruct((2, d // 2, n_in // 4), F32),
                    pl.BlockSpec((1, 256, n_in // 4), lambda i, p: (p[0], i, 0)), (d // 2 // 256,), "chip_sum_w_in"),
         0),
        (_blockwise(_four_sum, pos, [psum[1]] + [landed[1]] * 3,
                    [pl.BlockSpec((1, kv_rows, kv_cols), lambda i, p: (p[1], 0, 0))]
                    + from_chip((1, 1, kv_rows, kv_cols), 4),
                    jax.ShapeDtypeStruct((2, kv_rows, kv_cols), F32),
                    pl.BlockSpec((1, kv_rows, kv_cols), lambda i, p: (p[0], 0, 0)), (1,), "chip_sum_w_kv"),
         0),
        (_blockwise(_four_sum, pos, [psum[2]] + [landed[2]] * 3,
                    [pl.BlockSpec((1, wo_rows, d), lambda i, p: (p[1], 0, 0))] + from_chip((1, 1, wo_rows, d), 4),
                    jax.ShapeDtypeStruct((2, wo_rows, d), F32),
                    pl.BlockSpec((1, wo_rows, d), lambda i, p: (p[0], 0, 0)), (1,), "chip_sum_w_out"),
         0),
        (_blockwise(_four_sum, pos, [psum[3]] + [landed[3]] * 3,
                    [pl.BlockSpec((4, 1, pw_rows, GROUP), lambda i, p: (0, p[1], 0, 0))]
                    + from_chip((1, 4, 1, pw_rows, GROUP), 5),
                    jax.ShapeDtypeStruct((4, 2, pw_rows, GROUP), F32),
                    pl.BlockSpec((4, 1, pw_rows, GROUP), lambda i, p: (0, p[0], 0, 0)), (1,), "chip_sum_pool_w"),
         1),
    ])

    small_local = dict(norm_pre=g_norm_pre, pool_scale=g_pool_scale, sgu_ln_g=g_ln_g, sgu_ln_b=g_ln_b,
                       sgu_w=g_sgu_w, sgu_b=g_sgu_b_t.T, mem_norm=g_mem_norm, branch_norm=g_branch_norm,
                       norm_post=g_norm_post)
    small_rows = sum(weights[n].size for n in SMALL) // 128
    small_sum, *joined = _allreduce_small(
        _pack([small_local[n] for n in SMALL], [jnp.pad(loss_local, ((0, 7), (0, 127)))]), join_rider)
    grads = {"w_in": joined[0].reshape(w_in.shape), "w_kv": joined[1].reshape(w_kv.shape),
             "w_out": joined[2].reshape(w_out.shape), "pool_w": joined[3].reshape(pool_w.shape)}
    for n, g in zip(SMALL, _unpack(small_sum, [weights[n] for n in SMALL])):
        grads[n] = g
    loss = small_sum[small_rows, 0]

    delta, new_m, new_v = {}, {}, {}
    packed = [small_sum if src is grads else _pack([src[n] for n in SMALL]) for src in (weights, grads, mom1, mom2)]
    outs = _elementwise(_adamw, packed, [F32, F32, F32], "adamw_small")
    for dst, o in zip((delta, new_m, new_v), outs):
        for n, a in zip(SMALL, _unpack(o, [weights[n] for n in SMALL])):
            dst[n] = a
    for n in LARGE:
        cols = weights[n].shape[-1]
        outs = _elementwise(lambda w, g, m, v: _adamw(w, g, m, v) + (g,),
                            [src[n].reshape(-1, cols) for src in (weights, grads, mom1, mom2)],
                            [F32, F32, F32, F32], "adamw_" + n)
        for dst, o in zip((delta, new_m, new_v, grads), outs):
            dst[n] = o.reshape(weights[n].shape)

    return (loss, grad_x[None], *[grads[n] for n in ORDER], *[delta[n] for n in ORDER],
            *[new_m[n] for n in ORDER], *[new_v[n] for n in ORDER])
```

```python
import functools

import jax
import jax.numpy as jnp
from jax import lax
from jax.experimental import pallas as pl
from jax.experimental.pallas import tpu as pltpu

F32 = jnp.float32
BF16 = jnp.bfloat16
EPS = 1e-6
MESH = pl.DeviceIdType.MESH
ANY = pl.BlockSpec(memory_space=pl.ANY)

POOL_WINDOWS = (2, 4, 8, 16)
GROUP = 256
HALO = 16
CHUNK = 128
N_SGU_HEADS = 8
N_ATT_HEADS = 4
ATT_DIM = 256
WIDTH = 1024
ATT_SCALE = 1.0 / 16.0

ADAM_LR = 0.001
ADAM_B1 = 0.9
ADAM_B2 = 0.999
ADAM_EPS = 1e-08
ADAM_WD = 0.01
ADAM_STEP = 10

VMEM_LIMIT = 60 * 1024 * 1024
ELEMENTWISE_VMEM = 24 * 1024 * 1024
LANES = 128
BF16_ROWS = 16
MAX_PARTS = 4


def _params(n_grid_axes, vmem=VMEM_LIMIT):
    return pltpu.CompilerParams(dimension_semantics=("arbitrary",) * n_grid_axes, vmem_limit_bytes=vmem)


def _dot(a, b, dims):
    return lax.dot_general(a, b, (dims, ((), ())), preferred_element_type=F32)


NN = ((1,), (0,))
NT = ((1,), (1,))
TN = ((0,), (0,))


class _Rider:
    def __init__(self, inputs, out_shapes, n_sems, run, aliases=None):
        self.inputs, self.out_shapes, self.n_sems, self.run = list(inputs), list(out_shapes), n_sems, run
        self.aliases = aliases or {}


def _call(body, name, grid, in_specs, out_specs, out_shape, scratch_shapes, inputs, rider=None, prefetch=None,
          aliases=None, rider_refs=False):
    n_in, n_out, n_scr = len(in_specs), len(out_specs), len(scratch_shapes)
    r_in = len(rider.inputs) if rider else 0
    r_out = len(rider.out_shapes) if rider else 0
    n_pre = 0 if prefetch is None else 1

    def whole_body(*refs):
        pre, refs = refs[:n_pre], refs[n_pre:]
        ins, rider_ins = refs[:n_in], refs[n_in:n_in + r_in]
        refs = refs[n_in + r_in:]
        outs, rider_outs = refs[:n_out], refs[n_out:n_out + r_out]
        refs = refs[n_out + r_out:]
        scratch, sems = refs[:n_scr], refs[n_scr:]
        extra = {"rider_outs": rider_outs} if rider_refs else {}
        if rider is None:
            body(*pre, *ins, *outs, *scratch, **extra)
            return
        ids = [pl.program_id(ax) for ax in range(len(grid))]
        first = functools.reduce(lambda p, q: p & q, [i == 0 for i in ids])
        last = functools.reduce(lambda p, q: p & q, [i == g - 1 for i, g in zip(ids, grid)])

        @pl.when(first)
        def _():
            rider.run(rider_ins, rider_outs, *sems, True)

        body(*pre, *ins, *outs, *scratch, **extra)

        @pl.when(last)
        def _():
            rider.run(rider_ins, rider_outs, *sems, False)

    io_aliases = {n_pre + i: o for i, o in (aliases or {}).items()}
    scratch_all = list(scratch_shapes)
    if rider:
        io_aliases.update({n_pre + n_in + i: n_out + o for i, o in rider.aliases.items()})
        scratch_all += [pltpu.SemaphoreType.DMA((rider.n_sems,)), pltpu.SemaphoreType.DMA((rider.n_sems,))]
    specs = dict(grid=grid, in_specs=list(in_specs) + [ANY] * r_in, out_specs=tuple(out_specs) + (ANY,) * r_out,
                 scratch_shapes=scratch_all)
    if n_pre:
        specs = dict(grid_spec=pltpu.PrefetchScalarGridSpec(num_scalar_prefetch=1, **specs))
    outs = pl.pallas_call(
        whole_body, name=name, **specs,
        out_shape=tuple(out_shape) + tuple(rider.out_shapes if rider else ()),
        input_output_aliases=io_aliases, compiler_params=_params(len(grid)),
    )(*([prefetch] if n_pre else []), *inputs, *(rider.inputs if rider else []))
    return tuple(outs)


def _grad_rows(a_t, b, pos, row_of, m, tm, tn, tk, name, rider=None):
    k, n = a_t.shape[1], b.shape[1]
    nk = k // tk
    out_dtype, dims, a = BF16, NN, a_t
    a_spec = pl.BlockSpec((tm, tk), lambda i, j, kk, p: (row_of(i, p), kk))
    b_spec = pl.BlockSpec((tk, tn), lambda i, j, kk, p: (kk, j))

    def body(pos_ref, a_ref, b_ref, o_ref, *acc):
        part = lambda: _dot(a_ref[...], b_ref[...], dims)
        if nk == 1:
            o_ref[...] = part().astype(out_dtype)
            return
        (acc_ref,) = acc
        kk = pl.program_id(2)

        @pl.when(kk == 0)
        def _():
            acc_ref[...] = part()

        @pl.when((kk > 0) & (kk < nk - 1))
        def _():
            acc_ref[...] += part()

        @pl.when(kk == nk - 1)
        def _():
            o_ref[...] = (acc_ref[...] + part()).astype(out_dtype)

    return _call(body, name, (m // tm, n // tn, nk), [a_spec, b_spec],
                 [pl.BlockSpec((tm, tn), lambda i, j, kk, p: (i, j))], [jax.ShapeDtypeStruct((m, n), out_dtype)],
                 [pltpu.VMEM((tm, tn), F32)] if nk > 1 else [], [a, b], rider, prefetch=pos)


def _proj_piece(chips, first, n_shards, src, g_pre, w_in, proj_in, n_cols, rider, tm, name, casts=(), memory=()):
    s, d = src.shape
    cols = n_cols // 4
    fused = g_pre is not None
    n_steps = s // tm

    def body(chips_ref, *refs, rider_outs=()):
        refs = list(refs)
        src_ref = refs.pop(0)
        g_ref = refs.pop(0) if fused else None
        w_ref = refs.pop(0) if w_in is not None else rider_outs[0]
        if proj_in is not None:
            refs.pop(0)
        shard_refs = [refs.pop(0) for _ in casts]
        memory_refs = [refs.pop(0) for _ in memory]
        proj_ref = refs.pop(0)
        h_ref, ht_ref = (refs.pop(0), refs.pop(0)) if fused else (None, None)
        for shard_ref in shard_refs:
            refs.pop(0)[...] = shard_ref[...].astype(BF16)
        kv_refs = [refs.pop(0) for _ in memory[:2]]
        wbuf, sem = refs
        q, i = pl.program_id(0), pl.program_id(1)

        @pl.when(i == 0)
        def _():
            at = pl.multiple_of(chips_ref[first + q] * cols, LANES)
            cp = pltpu.make_async_copy(w_ref.at[:, pl.ds(at, cols)], wbuf, sem)
            cp.start()
            cp.wait()

        if memory:
            @pl.when((q == 0) & (i == 0))
            def _():
                _kv_body(*memory_refs, *kv_refs)

        if fused:
            xv = src_ref[...]
            r = lax.rsqrt(jnp.mean(xv * xv, axis=-1, keepdims=True) + EPS)
            h = (xv * r * g_ref[...]).astype(BF16)
            h_ref[...] = h
            ht_ref[...] = h.T
        else:
            h = src_ref[...]
        proj_ref[...] = _dot(h, wbuf[...], NN).astype(BF16)

    row = lambda q, i, ch: (i, 0)
    inputs, in_specs = [src], [pl.BlockSpec((tm, d), row)]
    if fused:
        inputs.append(g_pre)
        in_specs.append(pl.BlockSpec((1, d), lambda q, i, ch: (0, 0)))
    if w_in is not None:
        inputs.append(w_in)
        in_specs.append(ANY)
    aliases = {}
    if proj_in is not None:
        aliases[len(inputs)] = 0
        inputs.append(proj_in)
        in_specs.append(ANY)
    out_specs = [pl.BlockSpec((tm, cols), lambda q, i, ch: (i, ch[first + q]))]
    out_shape = [jax.ShapeDtypeStruct((s, n_cols), BF16)]
    if fused:
        assert n_shards == 1
        out_specs += [pl.BlockSpec((tm, d), row), pl.BlockSpec((d, tm), lambda q, i, ch: (0, i))]
        out_shape += [jax.ShapeDtypeStruct((s, d), BF16), jax.ShapeDtypeStruct((d, s), BF16)]
    for shard, kind in casts:
        assert n_shards == 1
        inputs.append(shard)
        if kind == 3:
            in_specs.append(pl.BlockSpec(shard.shape, lambda q, i, ch: (0, 0, 0)))
            out_specs.append(pl.BlockSpec(shard.shape, lambda q, i, ch: (0, ch[0], 0)))
            out_shape.append(jax.ShapeDtypeStruct((shard.shape[0], 4 * shard.shape[1], shard.shape[2]), BF16))
        else:
            block = (shard.shape[0] // n_steps, shard.shape[1])
            in_specs.append(pl.BlockSpec(block, row))
            out_specs.append(pl.BlockSpec(block, lambda q, i, ch: (ch[0] * n_steps + i, 0)))
            out_shape.append(jax.ShapeDtypeStruct((4 * shard.shape[0], shard.shape[1]), BF16))
    if memory:
        mem = memory[0]
        whole = lambda q, i, ch: (0, 0)
        inputs += list(memory)
        in_specs += [pl.BlockSpec(mem.shape, whole), pl.BlockSpec(memory[1].shape, whole),
                     pl.BlockSpec(memory[2].shape, whole, pipeline_mode=pl.Buffered(1))]
        out_specs += [pl.BlockSpec((mem.shape[0], WIDTH), whole)] * 2
        out_shape += [jax.ShapeDtypeStruct((mem.shape[0], WIDTH), BF16)] * 2
    return _call(body, name, (n_shards, s // tm), in_specs, out_specs, out_shape,
                 [pltpu.VMEM((d, cols), BF16), pltpu.SemaphoreType.DMA(())], inputs, rider, prefetch=chips,
                 aliases=aliases, rider_refs=True)


def _kv_body(mem_ref, g_ref, w_ref, k_ref, v_ref):
    mv = mem_ref[...]
    r = lax.rsqrt(jnp.mean(mv * mv, axis=-1, keepdims=True) + EPS)
    mem_n = (mv * r * g_ref[...]).astype(BF16)
    kv = _dot(mem_n, w_ref[...], NN)
    k_ref[...] = kv[:, :WIDTH].astype(BF16)
    v_ref[...] = kv[:, WIDTH:].astype(BF16)


def _kv_bwd(mem, g, w_kv, dk, dv):
    m, d = mem.shape
    n = w_kv.shape[1]
    col = 512

    def body(mem_ref, g_ref, w_ref, dk_ref, dv_ref, dw_ref, dg_ref):
        mv = mem_ref[...]
        r = lax.rsqrt(jnp.mean(mv * mv, axis=-1, keepdims=True) + EPS)
        mem_hat = mv * r
        mem_n = (mem_hat * g_ref[...]).astype(BF16)
        dkv = jnp.concatenate([dk_ref[...], dv_ref[...]], axis=1).astype(BF16)
        for j in range(n // col):
            dw_ref[:, j * col:(j + 1) * col] = _dot(mem_n, dkv[:, j * col:(j + 1) * col], TN).astype(BF16)
        dmem_n = _dot(dkv, w_ref[...], NT)
        dg_ref[...] = jnp.sum(dmem_n * mem_hat, axis=0, keepdims=True)

    return pl.pallas_call(
        body, name="kv_bwd",
        out_shape=(jax.ShapeDtypeStruct((d, n), BF16), jax.ShapeDtypeStruct((1, d), F32)),
        compiler_params=_params(0),
    )(mem, g, w_kv, dk, dv)


def _sigmoid(x):
    return 1.0 / (1.0 + jnp.exp(-x))


def _inv_counts(t0, t):
    pos = (t0 + lax.broadcasted_iota(jnp.int32, (t, 1), 0) + 1).astype(F32)
    return [1.0 / jnp.minimum(pos, float(w)) for w in POOL_WINDOWS]


def _window_sums(ext, t, backward):
    n = t + HALO
    parts = []
    for gi, w in enumerate(POOL_WINDOWS):
        s = ext[:, gi * GROUP:(gi + 1) * GROUP]
        k = 1
        while k < w:
            s = s + pltpu.roll(s, (n - k) if backward else k, axis=0)
            k *= 2
        parts.append(s[:t] if backward else s[HALO:])
    return parts


def _pool_fwd(xa, halo, inv, pool_w):
    t = xa.shape[0]
    sums = _window_sums(jnp.concatenate([halo, xa], axis=0), t, backward=False)
    d = jnp.concatenate([sums[gi] * inv[gi] - xa[:, gi * GROUP:(gi + 1) * GROUP] for gi in range(4)], axis=1)
    d = d.astype(BF16)
    y = jnp.concatenate([_dot(d[:, gi * GROUP:(gi + 1) * GROUP], pool_w[gi], NN) for gi in range(4)], axis=1)
    return d, y


def _layernorm_fwd(v):
    mu = jnp.mean(v, axis=-1, keepdims=True)
    xc = v - mu
    rstd = lax.rsqrt(jnp.mean(xc * xc, axis=-1, keepdims=True) + EPS)
    return xc * rstd, rstd


def _tril_mask(transposed):
    r = lax.broadcasted_iota(jnp.int32, (CHUNK, CHUNK), 0)
    c = lax.broadcasted_iota(jnp.int32, (CHUNK, CHUNK), 1)
    return (r <= c) if transposed else (r >= c)


def _sgu_mix(w_ref, vals, transposed):
    t = vals.shape[0]
    mask = _tril_mask(transposed)
    ws = [jnp.where(mask, w_ref[h], 0.0).astype(BF16) for h in range(N_SGU_HEADS)]
    rows = []
    for ci in range(t // CHUNK):
        blk = vals[ci * CHUNK:(ci + 1) * CHUNK]
        rows.append(jnp.concatenate(
            [_dot(ws[h], blk[:, h * CHUNK:(h + 1) * CHUNK], NN) for h in range(N_SGU_HEADS)], axis=1))
    return jnp.concatenate(rows, axis=0)


def _attn_fwd(q, k, v):
    ps, os_ = [], []
    for h in range(N_ATT_HEADS):
        sl = slice(h * ATT_DIM, (h + 1) * ATT_DIM)
        s = _dot(q[:, sl], k[:, sl], NT) * ATT_SCALE
        s = s - jnp.max(s, axis=-1, keepdims=True)
        e = jnp.exp(s)
        p = e * (1.0 / jnp.sum(e, axis=-1, keepdims=True))
        ps.append(p)
        os_.append(_dot(p.astype(BF16), v[:, sl], NN))
    return ps, jnp.concatenate(os_, axis=1)


def _rms_branch(y_pre):
    r = lax.rsqrt(jnp.mean(y_pre * y_pre, axis=-1, keepdims=True) + EPS)
    return y_pre * r, r


def _branch_specs(t, n_tiles, order):
    width_in = 7 * WIDTH
    tile = lambda i: order(i)
    per_halo = t // HALO
    const2 = lambda i: (0, 0)
    const3 = lambda i: (0, 0, 0)
    return [
        pl.BlockSpec((t, width_in), lambda i: (tile(i), 0)),
        pl.BlockSpec((HALO, WIDTH), lambda i: (jnp.maximum(tile(i) * per_halo - 1, 0), 0)),
        pl.BlockSpec((4, GROUP, GROUP), const3),
        pl.BlockSpec((1, WIDTH), const2),
        pl.BlockSpec((1, WIDTH), const2),
        pl.BlockSpec((1, WIDTH), const2),
        pl.BlockSpec((N_SGU_HEADS, CHUNK, CHUNK), const3),
        pl.BlockSpec((CHUNK, WIDTH), const2),
        pl.BlockSpec((MEM_ROWS, WIDTH), const2),
        pl.BlockSpec((MEM_ROWS, WIDTH), const2),
        pl.BlockSpec((1, 3 * WIDTH), const2),
    ]


MEM_ROWS = 256


def _branches_fwd(proj, pool_w, pool_scale, ln_g, ln_b, sgu_w, bias_full, k, v, branch_norm, t, rider=None):
    s = proj.shape[0]
    n_tiles = s // t

    def body(proj_ref, halo_ref, pw_ref, ps_ref, lg_ref, lb_ref, sw_ref, sb_ref, k_ref, v_ref, bn_ref, y_ref, yt_ref):
        i = pl.program_id(0)
        col = lambda j: proj_ref[:, j * WIDTH:(j + 1) * WIDTH].astype(F32)

        def put(branch, y_pre):
            sl = slice(branch * WIDTH, (branch + 1) * WIDTH)
            val = (_rms_branch(y_pre)[0] * bn[:, sl]).astype(BF16)
            y_ref[:, sl] = val
            yt_ref[sl, :] = val.T

        bn = bn_ref[...]
        halo = jnp.where(i > 0, halo_ref[...].astype(F32), 0.0)
        _, y_pool = _pool_fwd(col(0), halo, _inv_counts(i * t, t), pw_ref[...])
        ga = col(1)
        ya = y_pool * ps_ref[...] * (ga * _sigmoid(ga))
        put(0, ya)
        vhat, _ = _layernorm_fwd(col(3))
        vn = (vhat * lg_ref[...] + lb_ref[...]).astype(BF16)
        z = _sgu_mix(sw_ref, vn, transposed=False) + jnp.tile(sb_ref[...], (t // CHUNK, 1))
        gb = col(4)
        yb = col(2) * z * (gb * _sigmoid(gb))
        put(1, yb)
        _, o = _attn_fwd(col(5).astype(BF16), k_ref[...], v_ref[...])
        gc = col(6)
        yc = o * (gc * _sigmoid(gc))
        put(2, yc)

    return _call(body, "branches_fwd", (n_tiles,), _branch_specs(t, n_tiles, lambda i: i),
                 [pl.BlockSpec((t, 3 * WIDTH), lambda i: (i, 0)), pl.BlockSpec((3 * WIDTH, t), lambda i: (0, i))],
                 [jax.ShapeDtypeStruct((s, 3 * WIDTH), BF16), jax.ShapeDtypeStruct((3 * WIDTH, s), BF16)], [],
                 [proj, proj, pool_w, pool_scale, ln_g, ln_b, sgu_w, bias_full, k, v, branch_norm], rider)


def _branches_bwd(proj, dy, pool_w, pool_scale, ln_g, ln_b, sgu_w, sgu_wt, bias_full, k, v, branch_norm, t, rider=None):
    s = proj.shape[0]
    n_tiles = s // t
    n_chunks = t // CHUNK
    order = lambda i: n_tiles - 1 - i

    def body(proj_ref, halo_ref, pw_ref, ps_ref, lg_ref, lb_ref, sw_ref, sb_ref, k_ref, v_ref, bn_ref,
             swt_ref, dy_ref,
             dproj_ref, dpw_ref, dps_ref, dlg_ref, dlb_ref, dsw_ref, dsb_ref, dbn_ref, dk_ref, dv_ref,
             carry_ref, dbias_ref):
        step = pl.program_id(0)
        i = order(step)

        @pl.when(step == 0)
        def _():
            for ref in (dpw_ref, dps_ref, dlg_ref, dlb_ref, dsw_ref, dbn_ref, dk_ref, dv_ref, carry_ref, dbias_ref):
                ref[...] = jnp.zeros(ref.shape, ref.dtype)

        col = lambda j: proj_ref[:, j * WIDTH:(j + 1) * WIDTH].astype(F32)
        bn = bn_ref[...]

        def norm_bwd(y_pre, sl):
            yhat, r = _rms_branch(y_pre)
            dyv = dy_ref[:, sl].astype(F32)
            dbn_ref[:, sl] += jnp.sum(dyv * yhat, axis=0, keepdims=True)
            dyhat = dyv * bn[:, sl]
            return r * (dyhat - yhat * jnp.mean(dyhat * yhat, axis=-1, keepdims=True))

        def gate(gv):
            sg = _sigmoid(gv)
            return gv * sg, sg * (1.0 + gv * (1.0 - sg))

        inv = _inv_counts(i * t, t)
        halo = jnp.where(i > 0, halo_ref[...].astype(F32), 0.0)
        pw = pw_ref[...]
        d, y_pool = _pool_fwd(col(0), halo, inv, pw)
        scale = ps_ref[...]
        silu_a, dsilu_a = gate(col(1))
        pa = y_pool * scale
        dya = norm_bwd(pa * silu_a, slice(0, WIDTH))
        dproj_ref[:, WIDTH:2 * WIDTH] = (dya * pa * dsilu_a).astype(BF16)
        dpa = dya * silu_a
        dps_ref[...] += jnp.sum(dpa * y_pool, axis=0, keepdims=True)
        dy_pool = (dpa * scale).astype(BF16)
        dd_parts, ddc_parts = [], []
        for gi in range(4):
            sl = slice(gi * GROUP, (gi + 1) * GROUP)
            dpw_ref[gi] += _dot(d[:, sl], dy_pool[:, sl], TN)
            dd = _dot(dy_pool[:, sl], pw[gi], NT)
            dd_parts.append(dd)
            ddc_parts.append(dd * inv[gi])
        ddc = jnp.concatenate(ddc_parts, axis=1)
        sums = _window_sums(jnp.concatenate([ddc, carry_ref[...]], axis=0), t, backward=True)
        carry_ref[...] = ddc[:HALO]
        dproj_ref[:, 0:WIDTH] = jnp.concatenate([sums[gi] - dd_parts[gi] for gi in range(4)], axis=1).astype(BF16)

        vhat, rstd = _layernorm_fwd(col(3))
        lg = lg_ref[...]
        vn = (vhat * lg + lb_ref[...]).astype(BF16)
        z = _sgu_mix(sw_ref, vn, transposed=False) + jnp.tile(sb_ref[...], (n_chunks, 1))
        u = col(2)
        silu_b, dsilu_b = gate(col(4))
        uz = u * z
        dyb = norm_bwd(uz * silu_b, slice(WIDTH, 2 * WIDTH))
        dproj_ref[:, 4 * WIDTH:5 * WIDTH] = (dyb * uz * dsilu_b).astype(BF16)
        duz = dyb * silu_b
        dproj_ref[:, 2 * WIDTH:3 * WIDTH] = (duz * z).astype(BF16)
        dz = duz * u
        dz_b = dz.astype(BF16)
        for ci in range(n_chunks):
            rows = slice(ci * CHUNK, (ci + 1) * CHUNK)
            dbias_ref[...] += dz[rows]
            for h in range(N_SGU_HEADS):
                sl = slice(h * CHUNK, (h + 1) * CHUNK)
                dsw_ref[h] += _dot(dz_b[rows, sl], vn[rows, sl], NT)
        dvn = _sgu_mix(swt_ref, dz_b, transposed=True)
        dlg_ref[...] += jnp.sum(dvn * vhat, axis=0, keepdims=True)
        dlb_ref[...] += jnp.sum(dvn, axis=0, keepdims=True)
        dvhat = dvn * lg
        dvb = rstd * (dvhat - jnp.mean(dvhat, axis=-1, keepdims=True)
                      - vhat * jnp.mean(dvhat * vhat, axis=-1, keepdims=True))
        dproj_ref[:, 3 * WIDTH:4 * WIDTH] = dvb.astype(BF16)

        q = col(5).astype(BF16)
        kv_k, kv_v = k_ref[...], v_ref[...]
        ps, o = _attn_fwd(q, kv_k, kv_v)
        silu_c, dsilu_c = gate(col(6))
        dyc = norm_bwd(o * silu_c, slice(2 * WIDTH, 3 * WIDTH))
        dproj_ref[:, 6 * WIDTH:7 * WIDTH] = (dyc * o * dsilu_c).astype(BF16)
        do = (dyc * silu_c).astype(BF16)
        dq_parts = []
        for h in range(N_ATT_HEADS):
            sl = slice(h * ATT_DIM, (h + 1) * ATT_DIM)
            p = ps[h]
            dp = _dot(do[:, sl], kv_v[:, sl], NT)
            ds = (p * (dp - jnp.sum(p * dp, axis=-1, keepdims=True)) * ATT_SCALE).astype(BF16)
            dq_parts.append(_dot(ds, kv_k[:, sl], NN))
            dk_ref[:, sl] += _dot(ds, q[:, sl], TN)
            dv_ref[:, sl] += _dot(p.astype(BF16), do[:, sl], TN)
        dproj_ref[:, 5 * WIDTH:6 * WIDTH] = jnp.concatenate(dq_parts, axis=1).astype(BF16)

        @pl.when(step == n_tiles - 1)
        def _():
            keep = _tril_mask(transposed=False)
            for h in range(N_SGU_HEADS):
                dsw_ref[h] = jnp.where(keep, dsw_ref[h], 0.0)
            dsb_ref[...] = jnp.concatenate(
                [jnp.sum(dbias_ref[:, h * CHUNK:(h + 1) * CHUNK], axis=1, keepdims=True)
                 for h in range(N_SGU_HEADS)], axis=1)

    const2 = lambda i: (0, 0)
    const3 = lambda i: (0, 0, 0)
    out_shapes = (
        jax.ShapeDtypeStruct((s, 7 * WIDTH), BF16),
        jax.ShapeDtypeStruct((4, GROUP, GROUP), F32),
        jax.ShapeDtypeStruct((1, WIDTH), F32),
        jax.ShapeDtypeStruct((1, WIDTH), F32),
        jax.ShapeDtypeStruct((1, WIDTH), F32),
        jax.ShapeDtypeStruct((N_SGU_HEADS, CHUNK, CHUNK), F32),
        jax.ShapeDtypeStruct((CHUNK, N_SGU_HEADS), F32),
        jax.ShapeDtypeStruct((1, 3 * WIDTH), F32),
        jax.ShapeDtypeStruct((MEM_ROWS, WIDTH), F32),
        jax.ShapeDtypeStruct((MEM_ROWS, WIDTH), F32),
    )
    out_specs = (
        pl.BlockSpec((t, 7 * WIDTH), lambda i: (order(i), 0)),
        pl.BlockSpec((4, GROUP, GROUP), const3),
        pl.BlockSpec((1, WIDTH), const2),
        pl.BlockSpec((1, WIDTH), const2),
        pl.BlockSpec((1, WIDTH), const2),
        pl.BlockSpec((N_SGU_HEADS, CHUNK, CHUNK), const3),
        pl.BlockSpec((CHUNK, N_SGU_HEADS), const2),
        pl.BlockSpec((1, 3 * WIDTH), const2),
        pl.BlockSpec((MEM_ROWS, WIDTH), const2),
        pl.BlockSpec((MEM_ROWS, WIDTH), const2),
    )
    in_specs = _branch_specs(t, n_tiles, order) + [
        pl.BlockSpec((N_SGU_HEADS, CHUNK, CHUNK), const3),
        pl.BlockSpec((t, 3 * WIDTH), lambda i: (order(i), 0)),
    ]
    return _call(body, "branches_bwd", (n_tiles,), in_specs, out_specs, out_shapes,
                 [pltpu.VMEM((HALO, WIDTH), F32), pltpu.VMEM((CHUNK, WIDTH), F32)],
                 [proj, proj, pool_w, pool_scale, ln_g, ln_b, sgu_w, bias_full, k, v, branch_norm, sgu_wt, dy], rider)


def _out_loss(y, w_out, x, target, g_post, tm):
    s, d = x.shape
    e_w = y.shape[1]
    n_tiles = s // tm

    def body(y_ref, w_ref, x_ref, t_ref, g_ref, loss_ref, dz_ref, dout_ref, dy_ref, dg_ref, sq_ref):
        i = pl.program_id(0)

        @pl.when(i == 0)
        def _():
            sq_ref[...] = jnp.zeros(sq_ref.shape, F32)
            dg_ref[...] = jnp.zeros(dg_ref.shape, F32)

        w = w_ref[...]
        out = _dot(y_ref[...], w, NN)
        r = lax.rsqrt(jnp.mean(out * out, axis=-1, keepdims=True) + EPS)
        outn = out * r
        g = g_ref[...]
        err = (x_ref[...] + outn * g) - t_ref[...]
        sq_ref[...] += jnp.sum(err * err, axis=0, keepdims=True)
        dz = err * (1.0 / d)
        dz_ref[...] = dz
        dg_ref[...] += jnp.sum(dz * outn, axis=0, keepdims=True)
        doutn = dz * g
        dout = (r * (doutn - outn * jnp.mean(doutn * outn, axis=-1, keepdims=True))).astype(BF16)
        dout_ref[...] = dout
        dy_ref[...] = _dot(dout, w, NT).astype(BF16)

        @pl.when(i == n_tiles - 1)
        def _():
            loss_ref[...] = 0.5 * jnp.sum(sq_ref[...], axis=1, keepdims=True) * (1.0 / d)

    row = lambda i: (i, 0)
    const2 = lambda i: (0, 0)
    return pl.pallas_call(
        body, name="out_loss", grid=(n_tiles,),
        in_specs=[
            pl.BlockSpec((tm, e_w), row),
            pl.BlockSpec((e_w, d), const2, pipeline_mode=pl.Buffered(1)),
            pl.BlockSpec((tm, d), row),
            pl.BlockSpec((tm, d), row),
            pl.BlockSpec((1, d), const2),
        ],
        out_specs=(
            pl.BlockSpec((1, 1), const2),
            pl.BlockSpec((tm, d), row),
            pl.BlockSpec((tm, d), row),
            pl.BlockSpec((tm, e_w), row),
            pl.BlockSpec((1, d), const2),
        ),
        out_shape=(
            jax.ShapeDtypeStruct((1, 1), F32),
            jax.ShapeDtypeStruct((s, d), F32),
            jax.ShapeDtypeStruct((s, d), BF16),
            jax.ShapeDtypeStruct((s, e_w), BF16),
            jax.ShapeDtypeStruct((1, d), F32),
        ),
        scratch_shapes=[pltpu.VMEM((1, d), F32)],
        compiler_params=_params(1),
    )(y, w_out, x, target, g_post)


def _dx_call(dproj, w_in, x, dz, g_pre, tm, tk, rider=None):
    s, d = x.shape
    k_total = dproj.shape[1]
    nk = k_total // tk
    n_tiles = s // tm

    def body(dp_ref, w_ref, x_ref, dz_ref, g_ref, dx_ref, dg_ref, acc_ref):
        i, kk = pl.program_id(0), pl.program_id(1)
        part = lambda: _dot(dp_ref[...], w_ref[...], NT)

        @pl.when(kk == 0)
        def _():
            acc_ref[...] = part()

        @pl.when((kk > 0) & (kk < nk - 1))
        def _():
            acc_ref[...] += part()

        @pl.when((i == 0) & (kk == 0))
        def _():
            dg_ref[...] = jnp.zeros(dg_ref.shape, F32)

        @pl.when(kk == nk - 1)
        def _():
            dh = acc_ref[...] + part()
            xv = x_ref[...]
            r = lax.rsqrt(jnp.mean(xv * xv, axis=-1, keepdims=True) + EPS)
            xhat = xv * r
            dg_ref[...] += jnp.sum(dh * xhat, axis=0, keepdims=True)
            dxhat = dh * g_ref[...]
            dx_ref[...] = dz_ref[...] + r * (dxhat - xhat * jnp.mean(dxhat * xhat, axis=-1, keepdims=True))

    row = lambda i, kk: (i, 0)
    const2 = lambda i, kk: (0, 0)
    return _call(
        body, "dx", (n_tiles, nk),
        [
            pl.BlockSpec((tm, tk), lambda i, kk: (i, kk)),
            pl.BlockSpec((d, tk), lambda i, kk: (0, kk)),
            pl.BlockSpec((tm, d), row),
            pl.BlockSpec((tm, d), row),
            pl.BlockSpec((1, d), const2),
        ],
        [pl.BlockSpec((tm, d), row), pl.BlockSpec((1, d), const2)],
        [jax.ShapeDtypeStruct((s, d), F32), jax.ShapeDtypeStruct((1, d), F32)],
        [pltpu.VMEM((tm, d), F32)], [dproj, w_in, x, dz, g_pre], rider)


def _rows_tile(rows, cols, n_arrays, itemsize=4):
    budget = ELEMENTWISE_VMEM // (2 * n_arrays * cols * itemsize)
    if rows <= budget:
        return rows
    best = None
    for cand in range(16, rows + 1, 16):
        if rows % cand == 0 and cand <= max(budget, 16):
            best = cand
    return best if best is not None else rows


def _elementwise(fn, inputs, out_dtypes, name):
    rows, cols = inputs[0].shape
    tr = _rows_tile(rows, cols, len(inputs) + len(out_dtypes))
    n_in = len(inputs)

    def body(*refs):
        outs = fn(*[r[...] for r in refs[:n_in]])
        for o_ref, o in zip(refs[n_in:], outs):
            o_ref[...] = o.astype(o_ref.dtype)

    spec = pl.BlockSpec((tr, cols), lambda i: (i, 0))
    return pl.pallas_call(
        body, name=name, grid=(rows // tr,),
        in_specs=[spec] * n_in, out_specs=tuple([spec] * len(out_dtypes)),
        out_shape=tuple(jax.ShapeDtypeStruct((rows, cols), dt) for dt in out_dtypes),
        compiler_params=_params(1),
    )(*inputs)


def _blockwise(fn, pos, inputs, in_specs, out_shape, out_spec, grid, name):
    n_in = len(inputs)

    def body(pos_ref, *refs):
        o_ref = refs[n_in]
        (out,) = fn(*[r[...].reshape(o_ref.shape) for r in refs[:n_in]])
        o_ref[...] = out.astype(o_ref.dtype)

    return pl.pallas_call(
        body, name=name,
        grid_spec=pltpu.PrefetchScalarGridSpec(num_scalar_prefetch=1, grid=grid, in_specs=in_specs,
                                               out_specs=out_spec),
        out_shape=out_shape,
        compiler_params=_params(len(grid)),
    )(pos, *inputs)


def _cast_copy(x):
    return (x,)


def _pair_sum(mine, theirs):
    return ((mine.astype(F32) + theirs.astype(F32)),)


def _four_sum(own, t0, t1, t2):
    return ((((own.astype(F32) + t0.astype(F32)) + t1.astype(F32)) + t2.astype(F32)),)


def _adamw(w, g, m, v):
    m = ADAM_B1 * m + (1.0 - ADAM_B1) * g
    v = ADAM_B2 * v + (1.0 - ADAM_B2) * jnp.square(g)
    m_hat = m / (1.0 - ADAM_B1 ** ADAM_STEP)
    v_hat = v / (1.0 - ADAM_B2 ** ADAM_STEP)
    delta = -ADAM_LR * (m_hat / (jnp.sqrt(v_hat) + ADAM_EPS) + ADAM_WD * w)
    return delta, m, v


def _place():
    x, y, c = lax.axis_index("x"), lax.axis_index("y"), lax.axis_index("c")
    chips = [(1 - x, y), (x, 1 - y), (1 - x, 1 - y)]
    return x, y, c, chips


def _remote(src, dst, send_sem, recv_sem, to):
    return pltpu.make_async_remote_copy(src_ref=src, dst_ref=dst, send_sem=send_sem, recv_sem=recv_sem,
                                        device_id=to, device_id_type=MESH)


def _split(ref, plan):
    views = [ref]
    for axis, parts in plan:
        size = ref.shape[axis] // parts
        assert size * parts == ref.shape[axis]
        views = [v.at[tuple(pl.ds(q * size, size) if i == axis else slice(None) for i in range(len(ref.shape)))]
                 for v in views for q in range(parts)]
    return views


def _started(src, dst, send_sem, recv_sem, to):
    copy = _remote(src, dst, send_sem, recv_sem, to)
    copy.start()
    return copy


def _shard_half(kind, ref, chip, cc):
    if kind == 0:
        rows, cols = ref.shape[0] // 2, ref.shape[1] // 4
        return ref.at[pl.ds(cc * rows, rows), pl.ds(pl.multiple_of(chip * cols, LANES), cols)]
    if kind == 3:
        rows = ref.shape[1] // 8
        return ref.at[:, pl.ds(pl.multiple_of((2 * chip + cc) * rows, BF16_ROWS), rows), :]
    rows = ref.shape[0] // 8
    return ref.at[pl.ds(pl.multiple_of((2 * chip + cc) * rows, BF16_ROWS), rows), :]


def _relay_rider(full):
    kind = 0

    def quarter(ref, chip_no, cc, q):
        return _split(_shard_half(kind, ref, chip_no, cc), [(0, 2)])[q]

    def run(in_refs, full_refs, send_sems, recv_sems, start):
        (ref,) = full_refs
        x, y, c, chips = _place()
        sibling = (x, y, 1 - c)
        chip_no = [2 * ch[0] + ch[1] for ch in chips]
        if start:
            for p in (0, 1):
                held = quarter(ref, chip_no[1 - p], c, p)
                _remote(held, held, send_sems.at[p], recv_sems.at[p], (*chips[p], c)).start()
            return
        for p in (0, 1):
            landed = quarter(ref, chip_no[2], c, p)
            _remote(landed, landed, send_sems.at[p], recv_sems.at[p], (*chips[p], c)).wait_recv()
            _remote(landed, landed, send_sems.at[2], recv_sems.at[2], sibling).start()
        mine, theirs = _shard_half(kind, ref, chip_no[2], c), _shard_half(kind, ref, chip_no[2], 1 - c)
        _remote(mine, mine, send_sems.at[2], recv_sems.at[2], sibling).wait_send()
        _remote(theirs, theirs, send_sems.at[2], recv_sems.at[2], sibling).wait_recv()
        for p in (0, 1):
            held = quarter(ref, chip_no[1 - p], c, p)
            _remote(held, held, send_sems.at[p], recv_sems.at[p], (*chips[p], c)).wait_send()

    return _Rider([full], [jax.ShapeDtypeStruct(full.shape, full.dtype)], 3, run, aliases={0: 0})


def _riders(riders):
    def bounds(counts):
        ends = [sum(counts[:i + 1]) for i in range(len(counts))]
        return list(zip([0] + ends[:-1], ends))

    ins = bounds([len(r.inputs) for r in riders])
    outs = bounds([len(r.out_shapes) for r in riders])
    sems = bounds([r.n_sems for r in riders])

    class From:
        def __init__(self, sem_refs, base):
            self.sem_refs, self.base, self.at = sem_refs, base, self

        def __getitem__(self, k):
            return self.sem_refs.at[self.base + k]

    def run(in_refs, out_refs, send_sems, recv_sems, start):
        for r, (i0, i1), (o0, o1), (s0, _) in zip(riders, ins, outs, sems):
            r.run(in_refs[i0:i1], out_refs[o0:o1], From(send_sems, s0), From(recv_sems, s0), start)

    aliases = {}
    for r, (i0, _), (o0, _) in zip(riders, ins, outs):
        aliases.update({i0 + i: o0 + o for i, o in r.aliases.items()})
    return _Rider([a for r in riders for a in r.inputs], [o for r in riders for o in r.out_shapes],
                  sems[-1][1], run, aliases)


def _gather_rider(fulls, kinds, peers=(0, 1, 2)):
    n = len(fulls)
    full_half = _shard_half

    def run(in_refs, full_refs, send_sems, recv_sems, start):
        x, y, c, chips = _place()
        me = 2 * x + y
        sibling = (x, y, 1 - c)
        plans = [[(0, MAX_PARTS)], [(0, 2)], [(0, 2)], []]
        chips = [(p, chips[p]) for p in peers]
        across = lambda a, p, k: (3 * a + p) * MAX_PARTS + k
        onward = lambda a, p: 3 * n * MAX_PARTS + 3 * a + p

        def parts(a, chip_no, cc):
            return _split(full_half(kinds[a], full_refs[a], chip_no, cc), plans[kinds[a]])

        if start:
            for p, chip in chips:
                for a in range(n):
                    for k, mine in enumerate(parts(a, me, c)):
                        _remote(mine, mine, send_sems.at[across(a, p, k)], recv_sems.at[across(a, p, k)],
                                (*chip, c)).start()
            return
        for k in range(MAX_PARTS):
            for p, chip in chips:
                for a in range(n):
                    landed = parts(a, 2 * chip[0] + chip[1], c)
                    if k < len(landed):
                        _remote(landed[k], landed[k], send_sems.at[across(a, p, k)], recv_sems.at[across(a, p, k)],
                                (*chip, c)).wait_recv()
                        _remote(landed[k], landed[k], send_sems.at[onward(a, p)], recv_sems.at[onward(a, p)],
                                sibling).start()
        for p, chip in chips:
            them = 2 * chip[0] + chip[1]
            for a in range(n):
                passed = full_half(kinds[a], full_refs[a], them, 1 - c)
                _remote(passed, passed, send_sems.at[onward(a, p)], recv_sems.at[onward(a, p)], sibling).wait_recv()
                landed = full_half(kinds[a], full_refs[a], them, c)
                _remote(landed, landed, send_sems.at[onward(a, p)], recv_sems.at[onward(a, p)], sibling).wait_send()
                for k, mine in enumerate(parts(a, me, c)):
                    _remote(mine, mine, send_sems.at[across(a, p, k)], recv_sems.at[across(a, p, k)],
                            (*chip, c)).wait_send()

    return _Rider(fulls, [jax.ShapeDtypeStruct(f.shape, f.dtype) for f in fulls], 3 * n * (MAX_PARTS + 1), run,
                  aliases={a: a for a in range(n)})


def _exchange_rider(arrays, half_axes=None):
    n = len(arrays)
    half_axes = half_axes or [None] * n
    out_shapes = [jax.ShapeDtypeStruct(tuple(1 if i == ax else dim for i, dim in enumerate(g.shape)), g.dtype)
                  for g, ax in zip(arrays, half_axes)]

    def run(in_refs, out_refs, send_sems, recv_sems, start):
        x, y, c, _ = _place()
        sibling = (x, y, 1 - c)
        for a in range(n):
            src, ax = in_refs[a], half_axes[a]
            if ax is not None:
                src = src.at[tuple(pl.ds(1 - c, 1) if i == ax else slice(None) for i in range(len(src.shape)))]
            sems = (send_sems.at[a], recv_sems.at[a])
            if start:
                _started(src, out_refs[a], *sems, sibling)
            else:
                _remote(src, out_refs[a], *sems, sibling).wait()

    return _Rider(arrays, out_shapes, n, run)


def _scatter_rider(parts):
    n = len(parts)
    arrays = [p for p, _ in parts]

    def block_shape(p, ax):
        if ax == len(p.shape) - 1:
            return p.shape[:-1] + (p.shape[-1] // 4,)
        return tuple(1 if i == ax else dim for i, dim in enumerate(p.shape))

    out_shapes = [jax.ShapeDtypeStruct((3,) + block_shape(p, ax), p.dtype) for p, ax in parts]

    def block(ref, ax, chip):
        rank = len(ref.shape)
        if ax == rank - 1:
            cols = ref.shape[-1] // 4
            last = pl.ds(pl.multiple_of(chip * cols, LANES), cols)
            return ref.at[tuple([slice(None)] * (rank - 1) + [last])]
        return ref.at[tuple(pl.ds(chip, 1) if i == ax else slice(None) for i in range(rank))]

    def run(in_refs, out_refs, send_sems, recv_sems, start):
        x, y, c, chips = _place()
        for a in range(n):
            ax = parts[a][1]
            for p, chip in enumerate(chips):
                src, dst = block(in_refs[a], ax, 2 * chip[0] + chip[1]), out_refs[a].at[p]
                sems = (send_sems.at[3 * a + p], recv_sems.at[3 * a + p])
                if start:
                    _started(src, dst, *sems, (*chip, c))
                else:
                    _remote(src, dst, *sems, (*chip, c)).wait()

    return _Rider(arrays, out_shapes, 3 * n, run)


def _join_rider(joined):
    n = len(joined)
    arrays = [j for j, _ in joined]

    def run(in_refs, out_refs, send_sems, recv_sems, start):
        x, y, c, _ = _place()
        sibling = (x, y, 1 - c)

        def half(a, cc):
            rank = len(out_refs[a].shape)
            return out_refs[a].at[tuple(pl.ds(cc, 1) if i == joined[a][1] else slice(None) for i in range(rank))]

        for a in range(n):
            sems = (send_sems.at[a], recv_sems.at[a])
            if start:
                _started(half(a, c), half(a, c), *sems, sibling)
            else:
                _remote(half(a, c), half(a, c), *sems, sibling).wait_send()
                _remote(half(a, 1 - c), half(a, 1 - c), *sems, sibling).wait_recv()

    return _Rider(arrays, [jax.ShapeDtypeStruct(j.shape, j.dtype) for j in arrays], n, run,
                  aliases={a: a for a in range(n)})


def _allreduce_small(packed, rider):
    rows, lanes = packed.shape
    half = rows // 2
    r_in, r_out = len(rider.inputs), len(rider.out_shapes)

    def body(in_ref, *refs):
        rider_ins, out_ref, rider_outs = refs[:r_in], refs[r_in], refs[r_in + 1:r_in + 1 + r_out]
        pair_ref, gath_ref, send_sems, recv_sems, rider_send, rider_recv = refs[r_in + 1 + r_out:]
        x, y, c, chips = _place()
        me = 2 * x + y
        sibling = (x, y, 1 - c)
        mine = pl.ds(pl.multiple_of(c * half, 8), half)
        theirs = pl.ds(pl.multiple_of((1 - c) * half, 8), half)
        to_sib = _remote(in_ref.at[theirs], pair_ref, send_sems.at[0], recv_sems.at[0], sibling)
        to_sib.start()
        to_sib.wait()
        rider.run(rider_ins, rider_outs, rider_send, rider_recv, True)
        gath_ref[me] = in_ref[mine] + pair_ref[...]
        sends = [_remote(gath_ref.at[me], gath_ref.at[me], send_sems.at[1 + p], recv_sems.at[1 + p], (*chip, c))
                 for p, chip in enumerate(chips)]
        for cp in sends:
            cp.start()
        for p, chip in enumerate(chips):
            slot = gath_ref.at[2 * chip[0] + chip[1]]
            _remote(slot, slot, send_sems.at[1 + p], recv_sems.at[1 + p], (*chip, c)).wait_recv()
        for cp in sends:
            cp.wait_send()
        out_ref[mine] = ((gath_ref[0] + gath_ref[1]) + gath_ref[2]) + gath_ref[3]
        back = _remote(out_ref.at[mine], out_ref.at[mine], send_sems.at[4], recv_sems.at[4], sibling)
        back.start()
        back.wait_send()
        _remote(out_ref.at[theirs], out_ref.at[theirs], send_sems.at[4], recv_sems.at[4], sibling).wait_recv()
        rider.run(rider_ins, rider_outs, rider_send, rider_recv, False)

    vmem = pl.BlockSpec(memory_space=pltpu.VMEM)
    return pl.pallas_call(
        body, name="allreduce_small",
        in_specs=[vmem] + [ANY] * r_in, out_specs=(vmem,) + (ANY,) * r_out,
        out_shape=(jax.ShapeDtypeStruct((rows, lanes), F32),) + tuple(rider.out_shapes),
        scratch_shapes=[pltpu.VMEM((half, lanes), F32), pltpu.VMEM((4, half, lanes), F32),
                        pltpu.SemaphoreType.DMA((5,)), pltpu.SemaphoreType.DMA((5,)),
                        pltpu.SemaphoreType.DMA((rider.n_sems,)), pltpu.SemaphoreType.DMA((rider.n_sems,))],
        input_output_aliases={1 + i: 1 + o for i, o in rider.aliases.items()},
        compiler_params=pltpu.CompilerParams(has_side_effects=True, vmem_limit_bytes=32 * 1024 * 1024),
    )(packed, *rider.inputs)


SMALL = ("norm_pre", "pool_scale", "sgu_ln_g", "sgu_ln_b", "sgu_w", "sgu_b", "mem_norm", "branch_norm", "norm_post")
LARGE = ("w_in", "pool_w", "w_kv", "w_out")
ORDER = ("norm_pre", "w_in", "pool_w", "pool_scale", "sgu_ln_g", "sgu_ln_b", "sgu_w", "sgu_b", "mem_norm", "w_kv",
         "branch_norm", "w_out", "norm_post")


def _pack(arrays, extra=()):
    rows = [a.reshape(-1, 128) for a in arrays] + list(extra)
    pad = -sum(r.shape[0] for r in rows) % 16
    return jnp.concatenate(rows + ([jnp.zeros((pad, 128), F32)] if pad else []), axis=0)


def _unpack(packed, like):
    out, row = [], 0
    for a in like:
        rows = a.size // 128
        out.append(packed[row:row + rows].reshape(a.shape))
        row += rows
    return out


def kernel(x, mem, norm_pre, w_in, pool_w, pool_scale, sgu_ln_g, sgu_ln_b, sgu_w, sgu_b, mem_norm, w_kv, branch_norm, w_out, norm_post, loss_target, m_norm_pre, m_w_in, m_pool_w, m_pool_scale, m_sgu_ln_g, m_sgu_ln_b, m_sgu_w, m_sgu_b, m_mem_norm, m_w_kv, m_branch_norm, m_w_out, m_norm_post, v_norm_pre, v_w_in, v_pool_w, v_pool_scale, v_sgu_ln_g, v_sgu_ln_b, v_sgu_w, v_sgu_b, v_mem_norm, v_w_kv, v_branch_norm, v_w_out, v_norm_post):
    weights = dict(norm_pre=norm_pre, w_in=w_in, pool_w=pool_w, pool_scale=pool_scale, sgu_ln_g=sgu_ln_g,
                   sgu_ln_b=sgu_ln_b, sgu_w=sgu_w, sgu_b=sgu_b, mem_norm=mem_norm, w_kv=w_kv, branch_norm=branch_norm,
                   w_out=w_out, norm_post=norm_post)
    mom1 = dict(norm_pre=m_norm_pre, w_in=m_w_in, pool_w=m_pool_w, pool_scale=m_pool_scale, sgu_ln_g=m_sgu_ln_g,
                sgu_ln_b=m_sgu_ln_b, sgu_w=m_sgu_w, sgu_b=m_sgu_b, mem_norm=m_mem_norm, w_kv=m_w_kv,
                branch_norm=m_branch_norm, w_out=m_w_out, norm_post=m_norm_post)
    mom2 = dict(norm_pre=v_norm_pre, w_in=v_w_in, pool_w=v_pool_w, pool_scale=v_pool_scale, sgu_ln_g=v_sgu_ln_g,
                sgu_ln_b=v_sgu_ln_b, sgu_w=v_sgu_w, sgu_b=v_sgu_b, mem_norm=v_mem_norm, w_kv=v_w_kv,
                branch_norm=v_branch_norm, w_out=v_w_out, norm_post=v_norm_post)

    s, d = x.shape[1], x.shape[2]
    x2, mem2, tgt2 = x[0], mem[0], loss_target[0]
    t_branch = min(256, s)
    tm = min(512, s)

    core = lax.axis_index("c")
    chip = 2 * lax.axis_index("x") + lax.axis_index("y")
    pos = jnp.stack([core, chip]).astype(jnp.int32)
    n_in, n_kv, n_out = 4 * w_in.shape[2], 4 * w_kv.shape[1], 4 * w_out.shape[1]
    wi_rows, kv_rows, wo_rows = d // 8, n_kv // 8, n_out // 8

    kv_cols, pw_rows = w_kv.shape[2], GROUP // 8
    wi_own = _blockwise(_cast_copy, pos, [w_in[0]], [pl.BlockSpec((wi_rows, n_in // 4), lambda i, p: (i, 0))],
                        jax.ShapeDtypeStruct((d, n_in), BF16),
                        pl.BlockSpec((wi_rows, n_in // 4), lambda i, p: (i, p[1])), (8,), "place_w_in")

    x_pos, y_pos = lax.axis_index("x"), lax.axis_index("y")
    chips = jnp.stack([chip, 2 * (1 - x_pos) + y_pos, 2 * x_pos + 1 - y_pos,
                       2 * (1 - x_pos) + 1 - y_pos]).astype(jnp.int32)
    mem_g = mem_norm.reshape(1, d)
    proj, h, h_t, wkv_own, wo_own, pw_own, wi_full = _proj_piece(
        chips, 0, 1, x2, norm_pre, None, None, n_in, _gather_rider([wi_own], [0], peers=(0, 1)), tm, "proj_own",
        casts=[(w_kv[0], 1), (w_out[0], 2), (pool_w[0], 3)])
    proj, wi_full, wkv_full, pw_full = _proj_piece(
        chips, 1, 2, h, None, None, proj, n_in,
        _riders([_relay_rider(wi_full), _gather_rider([wkv_own, pw_own], [1, 3])]), tm, "proj_neighbours")
    proj, k_m, v_m, wo_part = _proj_piece(chips, 3, 1, h, None, wi_full, proj, n_in,
                                          _gather_rider([wo_own], [2], peers=(0, 1)), tm, "proj_diagonal",
                                          memory=(mem2, mem_g, wkv_full))
    bias_full = jnp.repeat(sgu_b[0].T, CHUNK, axis=1)
    y, y_t, wo_full = _branches_fwd(proj, pw_full, pool_scale, sgu_ln_g, sgu_ln_b, sgu_w[0], bias_full, k_m, v_m,
                                    branch_norm, t_branch, _gather_rider([wo_part], [2], peers=(2,)))
    loss_local, dz, dout, dy, g_norm_post = _out_loss(y, wo_full, x2, tgt2, norm_post, min(256, s))

    tk = min(1024, s)
    (dproj, g_pw, g_pool_scale, g_ln_g, g_ln_b, g_sgu_w, g_sgu_b_t, g_branch_norm, dk, dv) = _branches_bwd(
        proj, dy, pw_full, pool_scale, sgu_ln_g, sgu_ln_b, sgu_w[0], jnp.swapaxes(sgu_w[0], 1, 2), bias_full,
        k_m, v_m, branch_norm, t_branch)
    g_wkv, g_mem_norm = _kv_bwd(mem2, mem_g, wkv_full, dk, dv)
    g_wkv = g_wkv.reshape(4, 2, kv_rows, kv_cols)
    g_pw = g_pw.astype(BF16).reshape(4, 4, 2, pw_rows, GROUP)
    g_wo, gkv_from_sibling, gpw_from_sibling = _grad_rows(y_t, dout, pos, lambda i, p: i, n_out, n_out // 2, 1024, tk,
                                                          "grad_w_out", _exchange_rider([g_wkv, g_pw], [1, 2]))
    g_wo = g_wo.reshape(4, 2, wo_rows, d)
    ps_kv = _blockwise(_pair_sum, pos, [g_wkv, gkv_from_sibling],
                       [pl.BlockSpec((1, 1, kv_rows, kv_cols), lambda i, p: (i, p[0], 0, 0)),
                        pl.BlockSpec((1, 1, kv_rows, kv_cols), lambda i, p: (i, 0, 0, 0))],
                       jax.ShapeDtypeStruct((4, kv_rows, kv_cols), BF16),
                       pl.BlockSpec((1, kv_rows, kv_cols), lambda i, p: (i, 0, 0)), (4,), "pair_sum_w_kv")
    ps_pw = _blockwise(_pair_sum, pos, [g_pw, gpw_from_sibling],
                       [pl.BlockSpec((1, 4, 1, pw_rows, GROUP), lambda i, p: (i, 0, p[0], 0, 0)),
                        pl.BlockSpec((1, 4, 1, pw_rows, GROUP), lambda i, p: (i, 0, 0, 0, 0))],
                       jax.ShapeDtypeStruct((4, 4, pw_rows, GROUP), BF16),
                       pl.BlockSpec((1, 4, pw_rows, GROUP), lambda i, p: (i, 0, 0, 0)), (4,), "pair_sum_pool_w")
    gwi_theirs, landed_kv, landed_pw, gwo_from_sibling = _grad_rows(
        h_t, dproj, pos, lambda i, p: 1 - p[0], d // 2, d // 2, n_in // 4, tk, "grad_w_in_sibling_half",
        _riders([_scatter_rider([(ps_kv, 0), (ps_pw, 1)]), _exchange_rider([g_wo], [1])]))
    ps_wo = _blockwise(_pair_sum, pos, [g_wo, gwo_from_sibling],
                       [pl.BlockSpec((1, 1, wo_rows, d), lambda i, p: (i, p[0], 0, 0)),
                        pl.BlockSpec((1, 1, wo_rows, d), lambda i, p: (i, 0, 0, 0))],
                       jax.ShapeDtypeStruct((4, wo_rows, d), BF16),
                       pl.BlockSpec((1, wo_rows, d), lambda i, p: (i, 0, 0)), (4,), "pair_sum_w_out")
    gwi_mine, landed_wo, gwi_from_sibling = _grad_rows(
        h_t, dproj, pos, lambda i, p: p[0], d // 2, d // 2, n_in // 4, tk, "grad_w_in_own_half",
        _riders([_scatter_rider([(ps_wo, 0)]), _exchange_rider([gwi_theirs])]))
    ps_wi = _elementwise(_pair_sum, [gwi_mine, gwi_from_sibling], [BF16], "pair_sum_w_in")[0]
    grad_x, g_norm_pre, landed_wi = _dx_call(dproj, wi_full, x2, dz, norm_pre, tm, 1024,
                                             _scatter_rider([(ps_wi, 1)]))
    psum = [ps_wi, ps_kv, ps_wo, ps_pw]
    landed = [landed_wi, landed_kv, landed_wo, landed_pw]
    from_chip = lambda spec_shape, rank: [
        pl.BlockSpec(spec_shape, functools.partial(lambda i, p, q: (q, i) + (0,) * (rank - 2), q=q))
        for q in range(3)]
    join_rider = _join_rider([
        (_blockwise(_four_sum, pos, [psum[0]] + [landed[0]] * 3,
                    [pl.BlockSpec((256, n_in // 4), lambda i, p: (i, p[1]))] + from_chip((1, 256, n_in // 4), 3),
                    jax.ShapeDtypeStruct((2, d // 2, n_in // 4), F32),
                    pl.BlockSpec((1, 256, n_in // 4), lambda i, p: (p[0], i, 0)), (d // 2 // 256,), "chip_sum_w_in"),
         0),
        (_blockwise(_four_sum, pos, [psum[1]] + [landed[1]] * 3,
                    [pl.BlockSpec((1, kv_rows, kv_cols), lambda i, p: (p[1], 0, 0))]
                    + from_chip((1, 1, kv_rows, kv_cols), 4),
                    jax.ShapeDtypeStruct((2, kv_rows, kv_cols), F32),
                    pl.BlockSpec((1, kv_rows, kv_cols), lambda i, p: (p[0], 0, 0)), (1,), "chip_sum_w_kv"),
         0),
        (_blockwise(_four_sum, pos, [psum[2]] + [landed[2]] * 3,
                    [pl.BlockSpec((1, wo_rows, d), lambda i, p: (p[1], 0, 0))] + from_chip((1, 1, wo_rows, d), 4),
                    jax.ShapeDtypeStruct((2, wo_rows, d), F32),
                    pl.BlockSpec((1, wo_rows, d), lambda i, p: (p[0], 0, 0)), (1,), "chip_sum_w_out"),
         0),
        (_blockwise(_four_sum, pos, [psum[3]] + [landed[3]] * 3,
                    [pl.BlockSpec((4, 1, pw_rows, GROUP), lambda i, p: (0, p[1], 0, 0))]
                    + from_chip((1, 4, 1, pw_rows, GROUP), 5),
                    jax.ShapeDtypeStruct((4, 2, pw_rows, GROUP), F32),
                    pl.BlockSpec((4, 1, pw_rows, GROUP), lambda i, p: (0, p[0], 0, 0)), (1,), "chip_sum_pool_w"),
         1),
    ])

    small_local = dict(norm_pre=g_norm_pre, pool_scale=g_pool_scale, sgu_ln_g=g_ln_g, sgu_ln_b=g_ln_b,
                       sgu_w=g_sgu_w, sgu_b=g_sgu_b_t.T, mem_norm=g_mem_norm, branch_norm=g_branch_norm,
                       norm_post=g_norm_post)
    small_rows = sum(weights[n].size for n in SMALL) // 128
    small_sum, *joined = _allreduce_small(
        _pack([small_local[n] for n in SMALL], [jnp.pad(loss_local, ((0, 7), (0, 127)))]), join_rider)
    grads = {"w_in": joined[0].reshape(w_in.shape), "w_kv": joined[1].reshape(w_kv.shape),
             "w_out": joined[2].reshape(w_out.shape), "pool_w": joined[3].reshape(pool_w.shape)}
    for n, g in zip(SMALL, _unpack(small_sum, [weights[n] for n in SMALL])):
        grads[n] = g
    loss = small_sum[small_rows, 0]

    delta, new_m, new_v = {}, {}, {}
    packed = [small_sum if src is grads else _pack([src[n] for n in SMALL]) for src in (weights, grads, mom1, mom2)]
    outs = _elementwise(_adamw, packed, [F32, F32, F32], "adamw_small")
    for dst, o in zip((delta, new_m, new_v), outs):
        for n, a in zip(SMALL, _unpack(o, [weights[n] for n in SMALL])):
            dst[n] = a
    for n in LARGE:
        cols = weights[n].shape[-1]
        outs = _elementwise(lambda w, g, m, v: _adamw(w, g, m, v) + (g,),
                            [src[n].reshape(-1, cols) for src in (weights, grads, mom1, mom2)],
                            [F32, F32, F32, F32], "adamw_" + n)
        for dst, o in zip((delta, new_m, new_v, grads), outs):
            dst[n] = o.reshape(weights[n].shape)

    return (loss, grad_x[None], *[grads[n] for n in ORDER], *[delta[n] for n in ORDER],
            *[new_m[n] for n in ORDER], *[new_v[n] for n in ORDER])
```
